```python
import jax, jax.numpy as jnp
from jax import lax
import numpy as np

D_MODEL = 1024
BATCH = 8
SEQ = 4096
DEPTH = 4

N_META = 16
GRID_W = 64
GLA_HEADS = 4
GLA_DK = 64
GLA_DV = 128
GLA_KEY = GLA_HEADS * GLA_DK
GLA_VAL = GLA_HEADS * GLA_DV
GLA_RANK = 16
GLA_TAU = 16.0
CHUNK = 64
ATT_HEADS = 8
ATT_KV_HEADS = 2
HEAD_DIM = 64
ATT_Q = ATT_HEADS * HEAD_DIM
ATT_KV = ATT_KV_HEADS * HEAD_DIM
Q_BLOCK = 128
ROPE_THETA = 10000.0
D_FF = 2816
EPS = 1e-6

IN_SIZES = (GLA_KEY, GLA_KEY, GLA_VAL, GLA_VAL, GLA_RANK, GLA_RANK,
            ATT_Q, ATT_KV, ATT_KV, D_MODEL, D_MODEL)
D_IN = sum(IN_SIZES)
SPLIT_POINTS = tuple(int(s) for s in np.cumsum(IN_SIZES)[:-1])

kernel_name = "hybrid_gla_gqa_macaron_encoder"


def rmsnorm(x, g):
    xf = x.astype(jnp.float32)
    y = xf * lax.rsqrt(jnp.mean(xf * xf, axis=-1, keepdims=True) + EPS)
    return (y * g.astype(jnp.float32)).astype(x.dtype)


def swiglu(x, w_gate, w_up, w_down):
    return (jax.nn.silu(x @ w_gate) * (x @ w_up)) @ w_down


def gla_causal_chunked(q, k, v, log_a):
    f32 = jnp.float32
    B, H, T, dk = q.shape
    dv = v.shape[-1]
    n = T // CHUNK
    qc = q.astype(f32).reshape(B, H, n, CHUNK, dk)
    kc = k.astype(f32).reshape(B, H, n, CHUNK, dk)
    vc = v.astype(f32).reshape(B, H, n, CHUNK, dv)
    bcum = jnp.cumsum(log_a.astype(f32).reshape(B, H, n, CHUNK, dk), axis=3)
    btot = bcum[:, :, :, -1:, :]
    q_dec = qc * jnp.exp(bcum)
    k_inv = kc * jnp.exp(-bcum)
    k_end = kc * jnp.exp(btot - bcum)
    mask = jnp.tril(jnp.ones((CHUNK, CHUNK), dtype=bool))
    att = jnp.where(mask, jnp.einsum('bhnid,bhnjd->bhnij', q_dec, k_inv), 0.0)
    o_intra = jnp.einsum('bhnij,bhnjv->bhniv', att, vc)
    kv_chunk = jnp.einsum('bhnjd,bhnjv->bhndv', k_end, vc)
    decay = jnp.exp(btot[:, :, :, 0, :])

    def step(S, inp):
        d, kv = inp
        return d[..., None] * S + kv, S

    S0 = jnp.zeros((B, H, dk, dv), f32)
    _, S_prev = lax.scan(step, S0, (jnp.moveaxis(decay, 2, 0), jnp.moveaxis(kv_chunk, 2, 0)))
    S_prev = jnp.moveaxis(S_prev, 0, 2)
    o_inter = jnp.einsum('bhnid,bhndv->bhniv', q_dec, S_prev)
    return (o_intra + o_inter).reshape(B, H, T, dv)


def gla_branch(q, k, v, r, lr_f, lr_b, w2, b2, gn_gain):
    B, L, _ = q.shape
    pad = CHUNK - N_META

    def heads(t, d):
        return t.reshape(B, L, GLA_HEADS, d).transpose(0, 2, 1, 3)

    def log_gate(lr, w, b):
        return jax.nn.log_sigmoid((lr @ w + b).astype(jnp.float32)) / GLA_TAU

    def padseq(t):
        return jnp.pad(t, ((0, 0), (0, 0), (pad, 0), (0, 0)))

    def flip(t):
        return jnp.flip(t, axis=2)

    qh = padseq(heads(q * GLA_DK ** -0.5, GLA_DK))
    kh = padseq(heads(k, GLA_DK))
    vh = padseq(heads(v, GLA_DV))
    gf = padseq(heads(log_gate(lr_f, w2[0], b2[0]), GLA_DK))
    gb = padseq(heads(log_gate(lr_b, w2[1], b2[1]), GLA_DK))
    o_f = gla_causal_chunked(qh, kh, vh, gf)
    o_b = flip(gla_causal_chunked(flip(qh), flip(kh), flip(vh), flip(gb)))
    o = (o_f + o_b)[:, :, pad:, :].transpose(0, 2, 1, 3)
    o = o * lax.rsqrt(jnp.mean(o * o, axis=-1, keepdims=True) + EPS)
    o = o.reshape(B, L, GLA_VAL) * gn_gain.astype(jnp.float32)
    return o.astype(r.dtype) * jax.nn.silu(r)


def axial_rope_tables(n_tok):
    f32 = jnp.float32
    rows = n_tok // GRID_W
    row = jnp.repeat(jnp.arange(rows), GRID_W).astype(f32)
    col = jnp.tile(jnp.arange(GRID_W), rows).astype(f32)
    axis_dim = HEAD_DIM // 2
    inv = ROPE_THETA ** (-jnp.arange(0, axis_dim, 2, dtype=f32) / axis_dim)
    ang = jnp.concatenate([row[:, None] * inv, col[:, None] * inv], axis=-1)
    ang = jnp.concatenate([jnp.zeros((N_META, axis_dim), f32), ang], axis=0)
    return jnp.cos(ang), jnp.sin(ang)


def apply_axial_rope(x, cos, sin):
    L = x.shape[-2]
    q4 = HEAD_DIM // 4
    xs = x.astype(jnp.float32).reshape(*x.shape[:-1], 2, 2, q4)
    x1, x2 = xs[..., 0, :], xs[..., 1, :]
    c = cos.reshape(L, 2, q4)
    s = sin.reshape(L, 2, q4)
    return jnp.stack([x1 * c - x2 * s, x2 * c + x1 * s], axis=-2).reshape(x.shape)


def attention_branch(q, k, v, gq, gk, cos, sin):
    B, L, _ = q.shape
    G = ATT_HEADS // ATT_KV_HEADS
    qh = q.reshape(B, L, ATT_KV_HEADS, G, HEAD_DIM).transpose(0, 2, 3, 1, 4)
    kh = k.reshape(B, L, ATT_KV_HEADS, HEAD_DIM).transpose(0, 2, 1, 3)
    vh = v.reshape(B, L, ATT_KV_HEADS, HEAD_DIM).transpose(0, 2, 1, 3)
    qh = apply_axial_rope(rmsnorm(qh, gq), cos, sin) * HEAD_DIM ** -0.5
    kh = apply_axial_rope(rmsnorm(kh, gk), cos, sin)
    pad = Q_BLOCK - N_META
    qp = jnp.pad(qh, ((0, 0), (0, 0), (0, 0), (pad, 0), (0, 0)))
    nblk = qp.shape[3] // Q_BLOCK
    qb = jnp.moveaxis(qp.reshape(B, ATT_KV_HEADS, G, nblk, Q_BLOCK, HEAD_DIM), 3, 0)

    def block(qblk):
        s = jnp.einsum('bkgqd,bksd->bkgqs', qblk, kh)
        p = jax.nn.softmax(s, axis=-1)
        return jnp.einsum('bkgqs,bksd->bkgqd', p.astype(vh.dtype), vh)

    o = lax.map(block, qb)
    o = jnp.moveaxis(o, 0, 3).reshape(B, ATT_KV_HEADS, G, nblk * Q_BLOCK, HEAD_DIM)[:, :, :, pad:]
    return o.transpose(0, 3, 1, 2, 4).reshape(B, L, ATT_Q).astype(q.dtype)


def mixer(z, w_in, gla_w2, gla_b2, gla_gn, q_norm, k_norm, w_pa, w_pb, b_merge, w_out, cos, sin):
    h = z @ w_in
    (q_a, k_a, v_a, r_a, lr_f, lr_b, q_b, k_b, v_b, g_a, g_b) = jnp.split(h, SPLIT_POINTS, axis=-1)
    a = gla_branch(q_a, k_a, v_a, r_a, lr_f, lr_b, gla_w2, gla_b2, gla_gn)
    b = attention_branch(q_b, k_b, v_b, q_norm, k_norm, cos, sin)
    y = (jax.nn.sigmoid(g_a + b_merge[0]) * (a @ w_pa)
         + jax.nn.sigmoid(g_b + b_merge[1]) * (b @ w_pb))
    return y @ w_out


def _fwd_setup_inputs(seed: int = 0) -> dict:
    key = jax.random.key(seed)
    ks = jax.random.split(key, 18)

    def nrm(k, shape, scale):
        return jax.random.normal(k, shape, jnp.float32) * scale

    return {
        "x": nrm(ks[0], (BATCH, SEQ, D_MODEL), 1.0),
        "meta_tokens": nrm(ks[1], (N_META, D_MODEL), 1.0),
        "norm_gains": 1.0 + nrm(ks[2], (DEPTH, 3, D_MODEL), 0.02),
        "ffn_w_gate": nrm(ks[3], (DEPTH, 2, D_MODEL, D_FF), D_MODEL ** -0.5),
        "ffn_w_up": nrm(ks[4], (DEPTH, 2, D_MODEL, D_FF), D_MODEL ** -0.5),
        "ffn_w_down": nrm(ks[5], (DEPTH, 2, D_FF, D_MODEL), D_FF ** -0.5),
        "w_in": nrm(ks[6], (DEPTH, D_MODEL, D_IN), D_MODEL ** -0.5),
        "gla_w2": nrm(ks[7], (DEPTH, 2, GLA_RANK, GLA_KEY), GLA_RANK ** -0.5),
        "gla_b2": nrm(ks[8], (DEPTH, 2, GLA_KEY), 0.1),
        "gla_gn": 1.0 + nrm(ks[9], (DEPTH, GLA_VAL), 0.02),
        "q_norm": 1.0 + nrm(ks[10], (DEPTH, HEAD_DIM), 0.02),
        "k_norm": 1.0 + nrm(ks[11], (DEPTH, HEAD_DIM), 0.02),
        "w_pa": nrm(ks[12], (DEPTH, GLA_VAL, D_MODEL), GLA_VAL ** -0.5),
        "w_pb": nrm(ks[13], (DEPTH, ATT_Q, D_MODEL), ATT_Q ** -0.5),
        "b_merge": nrm(ks[14], (DEPTH, 2, D_MODEL), 0.02),
        "w_out": nrm(ks[15], (DEPTH, D_MODEL, D_MODEL), D_MODEL ** -0.5),
        "final_norm": 1.0 + nrm(ks[16], (D_MODEL,), 0.02),
    }


def _fwd_reference(x, meta_tokens, norm_gains, ffn_w_gate, ffn_w_up, ffn_w_down, w_in, gla_w2, gla_b2,
              gla_gn, q_norm, k_norm, w_pa, w_pb, b_merge, w_out, final_norm):
    B, N, D = x.shape
    meta = jnp.broadcast_to(meta_tokens[None].astype(x.dtype), (B, N_META, D))
    h = jnp.concatenate([meta, x], axis=1)
    cos, sin = axial_rope_tables(N)
    for l in range(DEPTH):
        h = h + 0.5 * swiglu(rmsnorm(h, norm_gains[l, 0]),
                             ffn_w_gate[l, 0], ffn_w_up[l, 0], ffn_w_down[l, 0])
        h = h + mixer(rmsnorm(h, norm_gains[l, 1]), w_in[l], gla_w2[l], gla_b2[l], gla_gn[l],
                      q_norm[l], k_norm[l], w_pa[l], w_pb[l], b_merge[l], w_out[l], cos, sin)
        h = h + 0.5 * swiglu(rmsnorm(h, norm_gains[l, 2]),
                             ffn_w_gate[l, 1], ffn_w_up[l, 1], ffn_w_down[l, 1])
    return rmsnorm(h, final_norm)[:, N_META:]


import jax as _jax
import jax.numpy as _jnp

TWIN_FORMAT = 'train_step'
FWD_PARAMS = ['x', 'meta_tokens', 'norm_gains', 'ffn_w_gate', 'ffn_w_up', 'ffn_w_down', 'w_in', 'gla_w2', 'gla_b2', 'gla_gn', 'q_norm', 'k_norm', 'w_pa', 'w_pb', 'b_merge', 'w_out', 'final_norm']
TWIN_WEIGHTS = ['meta_tokens', 'norm_gains', 'ffn_w_gate', 'ffn_w_up', 'ffn_w_down', 'w_in', 'gla_w2', 'gla_b2', 'gla_gn', 'q_norm', 'k_norm', 'w_pa', 'w_pb', 'b_merge', 'w_out', 'final_norm']
TWIN_DIFF_INPUT = 'x'
TWIN_INPUTS = ['x', 'meta_tokens', 'norm_gains', 'ffn_w_gate', 'ffn_w_up', 'ffn_w_down', 'w_in', 'gla_w2', 'gla_b2', 'gla_gn', 'q_norm', 'k_norm', 'w_pa', 'w_pb', 'b_merge', 'w_out', 'final_norm', 'loss_target', 'm_meta_tokens', 'm_norm_gains', 'm_ffn_w_gate', 'm_ffn_w_up', 'm_ffn_w_down', 'm_w_in', 'm_gla_w2', 'm_gla_b2', 'm_gla_gn', 'm_q_norm', 'm_k_norm', 'm_w_pa', 'm_w_pb', 'm_b_merge', 'm_w_out', 'm_final_norm', 'v_meta_tokens', 'v_norm_gains', 'v_ffn_w_gate', 'v_ffn_w_up', 'v_ffn_w_down', 'v_w_in', 'v_gla_w2', 'v_gla_b2', 'v_gla_gn', 'v_q_norm', 'v_k_norm', 'v_w_pa', 'v_w_pb', 'v_b_merge', 'v_w_out', 'v_final_norm']
TWIN_OUTPUTS = ['loss', 'grad_x', 'grad_meta_tokens', 'grad_norm_gains', 'grad_ffn_w_gate', 'grad_ffn_w_up', 'grad_ffn_w_down', 'grad_w_in', 'grad_gla_w2', 'grad_gla_b2', 'grad_gla_gn', 'grad_q_norm', 'grad_k_norm', 'grad_w_pa', 'grad_w_pb', 'grad_b_merge', 'grad_w_out', 'grad_final_norm', 'delta_meta_tokens', 'delta_norm_gains', 'delta_ffn_w_gate', 'delta_ffn_w_up', 'delta_ffn_w_down', 'delta_w_in', 'delta_gla_w2', 'delta_gla_b2', 'delta_gla_gn', 'delta_q_norm', 'delta_k_norm', 'delta_w_pa', 'delta_w_pb', 'delta_b_merge', 'delta_w_out', 'delta_final_norm', 'new_m_meta_tokens', 'new_m_norm_gains', 'new_m_ffn_w_gate', 'new_m_ffn_w_up', 'new_m_ffn_w_down', 'new_m_w_in', 'new_m_gla_w2', 'new_m_gla_b2', 'new_m_gla_gn', 'new_m_q_norm', 'new_m_k_norm', 'new_m_w_pa', 'new_m_w_pb', 'new_m_b_merge', 'new_m_w_out', 'new_m_final_norm', 'new_v_meta_tokens', 'new_v_norm_gains', 'new_v_ffn_w_gate', 'new_v_ffn_w_up', 'new_v_ffn_w_down', 'new_v_w_in', 'new_v_gla_w2', 'new_v_gla_b2', 'new_v_gla_gn', 'new_v_q_norm', 'new_v_k_norm', 'new_v_w_pa', 'new_v_w_pb', 'new_v_b_merge', 'new_v_w_out', 'new_v_final_norm']
TWIN_LEAF_KINDS = {'loss': 'loss', 'grad_x': 'grad_x', 'grad_meta_tokens': 'grad_w', 'grad_norm_gains': 'grad_w', 'grad_ffn_w_gate': 'grad_w', 'grad_ffn_w_up': 'grad_w', 'grad_ffn_w_down': 'grad_w', 'grad_w_in': 'grad_w', 'grad_gla_w2': 'grad_w', 'grad_gla_b2': 'grad_w', 'grad_gla_gn': 'grad_w', 'grad_q_norm': 'grad_w', 'grad_k_norm': 'grad_w', 'grad_w_pa': 'grad_w', 'grad_w_pb': 'grad_w', 'grad_b_merge': 'grad_w', 'grad_w_out': 'grad_w', 'grad_final_norm': 'grad_w', 'delta_meta_tokens': 'delta_w', 'delta_norm_gains': 'delta_w', 'delta_ffn_w_gate': 'delta_w', 'delta_ffn_w_up': 'delta_w', 'delta_ffn_w_down': 'delta_w', 'delta_w_in': 'delta_w', 'delta_gla_w2': 'delta_w', 'delta_gla_b2': 'delta_w', 'delta_gla_gn': 'delta_w', 'delta_q_norm': 'delta_w', 'delta_k_norm': 'delta_w', 'delta_w_pa': 'delta_w', 'delta_w_pb': 'delta_w', 'delta_b_merge': 'delta_w', 'delta_w_out': 'delta_w', 'delta_final_norm': 'delta_w', 'new_m_meta_tokens': 'new_m', 'new_m_norm_gains': 'new_m', 'new_m_ffn_w_gate': 'new_m', 'new_m_ffn_w_up': 'new_m', 'new_m_ffn_w_down': 'new_m', 'new_m_w_in': 'new_m', 'new_m_gla_w2': 'new_m', 'new_m_gla_b2': 'new_m', 'new_m_gla_gn': 'new_m', 'new_m_q_norm': 'new_m', 'new_m_k_norm': 'new_m', 'new_m_w_pa': 'new_m', 'new_m_w_pb': 'new_m', 'new_m_b_merge': 'new_m', 'new_m_w_out': 'new_m', 'new_m_final_norm': 'new_m', 'new_v_meta_tokens': 'new_v', 'new_v_norm_gains': 'new_v', 'new_v_ffn_w_gate': 'new_v', 'new_v_ffn_w_up': 'new_v', 'new_v_ffn_w_down': 'new_v', 'new_v_w_in': 'new_v', 'new_v_gla_w2': 'new_v', 'new_v_gla_b2': 'new_v', 'new_v_gla_gn': 'new_v', 'new_v_q_norm': 'new_v', 'new_v_k_norm': 'new_v', 'new_v_w_pa': 'new_v', 'new_v_w_pb': 'new_v', 'new_v_b_merge': 'new_v', 'new_v_w_out': 'new_v', 'new_v_final_norm': 'new_v'}


def _forward(args):
    return _fwd_reference(*[args[k] for k in FWD_PARAMS])


def _output_shape():
    out = _jax.eval_shape(lambda: _forward(_fwd_setup_inputs(0)))
    return out.shape, out.dtype

N_MICROBATCH = 1
ADAM_LR = 0.001
ADAM_B1 = 0.9
ADAM_B2 = 0.999
ADAM_EPS = 1e-08
ADAM_WD = 0.01
ADAM_STEP = 10
PER_EXAMPLE_BATCH_AXIS = {'x': 0, 'loss_target': 0}
SHARED_INPUTS = []
_WEIGHT_DTYPES = {'meta_tokens': _jnp.float32, 'norm_gains': _jnp.float32, 'ffn_w_gate': _jnp.float32, 'ffn_w_up': _jnp.float32, 'ffn_w_down': _jnp.float32, 'w_in': _jnp.float32, 'gla_w2': _jnp.float32, 'gla_b2': _jnp.float32, 'gla_gn': _jnp.float32, 'q_norm': _jnp.float32, 'k_norm': _jnp.float32, 'w_pa': _jnp.float32, 'w_pb': _jnp.float32, 'b_merge': _jnp.float32, 'w_out': _jnp.float32, 'final_norm': _jnp.float32}
MOMENT_SCALE = {'meta_tokens': 6.831007e-03, 'norm_gains': 1.030936e-01, 'ffn_w_gate': 3.730665e-02, 'ffn_w_up': 3.616704e-02, 'ffn_w_down': 5.990832e-02, 'w_in': 6.244106e-02, 'gla_w2': 9.591057e-03, 'gla_b2': 4.221725e-02, 'gla_gn': 8.543819e-02, 'q_norm': 4.115351e-02, 'k_norm': 4.191661e-02, 'w_pa': 6.156038e-02, 'w_pb': 1.103309e-02, 'b_merge': 1.736564e-02, 'w_out': 6.257089e-02, 'final_norm': 3.200831e+01}


def _to_microbatches(a, axis):
    t = _jnp.moveaxis(a, axis, 0)
    t = t.reshape((N_MICROBATCH, t.shape[0] // N_MICROBATCH) + t.shape[1:])
    return _jnp.moveaxis(t, 1, axis + 1)


def setup_inputs(seed: int = 0) -> dict:
    inp = _fwd_setup_inputs(seed)
    key = _jax.random.fold_in(_jax.random.key(seed), 7919)
    shape, _ = _output_shape()
    out = dict(inp)
    out["loss_target"] = _jax.random.normal(_jax.random.fold_in(key, 0), shape, _jnp.float32)
    for i, name in enumerate(TWIN_WEIGHTS):
        w = inp[name].astype(_jnp.float32)
        if MOMENT_SCALE is None:
            s = _jnp.sqrt(_jnp.mean(_jnp.square(w)) + 1e-30)
        else:
            s = MOMENT_SCALE[name]
        km, kv = _jax.random.split(_jax.random.fold_in(key, i + 1))
        out[name] = w
        out["m_" + name] = s * _jax.random.normal(km, w.shape, _jnp.float32)
        out["v_" + name] = (s * s) * _jax.random.uniform(kv, w.shape, _jnp.float32, 0.5, 1.5)
    if N_MICROBATCH > 1:
        for name, axis in PER_EXAMPLE_BATCH_AXIS.items():
            out[name] = _to_microbatches(out[name], axis)
    return {'x': out['x'], 'meta_tokens': out['meta_tokens'], 'norm_gains': out['norm_gains'], 'ffn_w_gate': out['ffn_w_gate'], 'ffn_w_up': out['ffn_w_up'], 'ffn_w_down': out['ffn_w_down'], 'w_in': out['w_in'], 'gla_w2': out['gla_w2'], 'gla_b2': out['gla_b2'], 'gla_gn': out['gla_gn'], 'q_norm': out['q_norm'], 'k_norm': out['k_norm'], 'w_pa': out['w_pa'], 'w_pb': out['w_pb'], 'b_merge': out['b_merge'], 'w_out': out['w_out'], 'final_norm': out['final_norm'], 'loss_target': out['loss_target'], 'm_meta_tokens': out['m_meta_tokens'], 'm_norm_gains': out['m_norm_gains'], 'm_ffn_w_gate': out['m_ffn_w_gate'], 'm_ffn_w_up': out['m_ffn_w_up'], 'm_ffn_w_down': out['m_ffn_w_down'], 'm_w_in': out['m_w_in'], 'm_gla_w2': out['m_gla_w2'], 'm_gla_b2': out['m_gla_b2'], 'm_gla_gn': out['m_gla_gn'], 'm_q_norm': out['m_q_norm'], 'm_k_norm': out['m_k_norm'], 'm_w_pa': out['m_w_pa'], 'm_w_pb': out['m_w_pb'], 'm_b_merge': out['m_b_merge'], 'm_w_out': out['m_w_out'], 'm_final_norm': out['m_final_norm'], 'v_meta_tokens': out['v_meta_tokens'], 'v_norm_gains': out['v_norm_gains'], 'v_ffn_w_gate': out['v_ffn_w_gate'], 'v_ffn_w_up': out['v_ffn_w_up'], 'v_ffn_w_down': out['v_ffn_w_down'], 'v_w_in': out['v_w_in'], 'v_gla_w2': out['v_gla_w2'], 'v_gla_b2': out['v_gla_b2'], 'v_gla_gn': out['v_gla_gn'], 'v_q_norm': out['v_q_norm'], 'v_k_norm': out['v_k_norm'], 'v_w_pa': out['v_w_pa'], 'v_w_pb': out['v_w_pb'], 'v_b_merge': out['v_b_merge'], 'v_w_out': out['v_w_out'], 'v_final_norm': out['v_final_norm']}


def _loss(weights, diff, rest, loss_target):
    with _jax.named_scope("forward"):
        args = {**rest, TWIN_DIFF_INPUT: diff, **{k: w.astype(_WEIGHT_DTYPES[k]) for k, w in weights.items()}}
        y = _forward(args)
    with _jax.named_scope("loss_head"):
        err = _jnp.square(y.astype(_jnp.float32) - loss_target)
        return 0.5 * _jnp.sum(_jnp.mean(err, axis=-1)) if err.ndim else 0.5 * err


def _adamw(w, g, m, v):
    m = ADAM_B1 * m + (1.0 - ADAM_B1) * g
    v = ADAM_B2 * v + (1.0 - ADAM_B2) * _jnp.square(g)
    m_hat = m / (1.0 - ADAM_B1 ** ADAM_STEP)
    v_hat = v / (1.0 - ADAM_B2 ** ADAM_STEP)
    delta = -ADAM_LR * (m_hat / (_jnp.sqrt(v_hat) + ADAM_EPS) + ADAM_WD * w)
    return delta, m, v


def reference(x, meta_tokens, norm_gains, ffn_w_gate, ffn_w_up, ffn_w_down, w_in, gla_w2, gla_b2, gla_gn, q_norm, k_norm, w_pa, w_pb, b_merge, w_out, final_norm, loss_target, m_meta_tokens, m_norm_gains, m_ffn_w_gate, m_ffn_w_up, m_ffn_w_down, m_w_in, m_gla_w2, m_gla_b2, m_gla_gn, m_q_norm, m_k_norm, m_w_pa, m_w_pb, m_b_merge, m_w_out, m_final_norm, v_meta_tokens, v_norm_gains, v_ffn_w_gate, v_ffn_w_up, v_ffn_w_down, v_w_in, v_gla_w2, v_gla_b2, v_gla_gn, v_q_norm, v_k_norm, v_w_pa, v_w_pb, v_b_merge, v_w_out, v_final_norm):
    given = dict(x=x, meta_tokens=meta_tokens, norm_gains=norm_gains, ffn_w_gate=ffn_w_gate, ffn_w_up=ffn_w_up, ffn_w_down=ffn_w_down, w_in=w_in, gla_w2=gla_w2, gla_b2=gla_b2, gla_gn=gla_gn, q_norm=q_norm, k_norm=k_norm, w_pa=w_pa, w_pb=w_pb, b_merge=b_merge, w_out=w_out, final_norm=final_norm, loss_target=loss_target, m_meta_tokens=m_meta_tokens, m_norm_gains=m_norm_gains, m_ffn_w_gate=m_ffn_w_gate, m_ffn_w_up=m_ffn_w_up, m_ffn_w_down=m_ffn_w_down, m_w_in=m_w_in, m_gla_w2=m_gla_w2, m_gla_b2=m_gla_b2, m_gla_gn=m_gla_gn, m_q_norm=m_q_norm, m_k_norm=m_k_norm, m_w_pa=m_w_pa, m_w_pb=m_w_pb, m_b_merge=m_b_merge, m_w_out=m_w_out, m_final_norm=m_final_norm, v_meta_tokens=v_meta_tokens, v_norm_gains=v_norm_gains, v_ffn_w_gate=v_ffn_w_gate, v_ffn_w_up=v_ffn_w_up, v_ffn_w_down=v_ffn_w_down, v_w_in=v_w_in, v_gla_w2=v_gla_w2, v_gla_b2=v_gla_b2, v_gla_gn=v_gla_gn, v_q_norm=v_q_norm, v_k_norm=v_k_norm, v_w_pa=v_w_pa, v_w_pb=v_w_pb, v_b_merge=v_b_merge, v_w_out=v_w_out, v_final_norm=v_final_norm)
    weights = {n: given[n] for n in TWIN_WEIGHTS}
    shared = {n: given[n] for n in SHARED_INPUTS}
    per_example = {n: given[n] for n in ['x']}
    grad_fn = _jax.value_and_grad(_loss, argnums=(0, 1))

    def one_microbatch(ex, loss_target):
        ex = dict(ex)
        diff = ex.pop(TWIN_DIFF_INPUT)
        return grad_fn(weights, diff, {**shared, **ex}, loss_target)

    if N_MICROBATCH == 1:
        loss, (grad_w, grad_x) = one_microbatch(per_example, given["loss_target"])
    else:
        def body(carry, xs):
            loss_sum, grad_sum = carry
            l_k, (gw_k, gx_k) = one_microbatch(xs[0], xs[1])
            with _jax.named_scope("update"):
                return (loss_sum + l_k, _jax.tree.map(_jnp.add, grad_sum, gw_k)), gx_k

        init = (_jnp.zeros((), _jnp.float32), _jax.tree.map(_jnp.zeros_like, weights))
        (loss, grad_w), grad_x = _jax.lax.scan(body, init, (per_example, given["loss_target"]))
    with _jax.named_scope("update"):
        delta_w, new_m, new_v = {}, {}, {}
        for n in TWIN_WEIGHTS:
            delta_w[n], new_m[n], new_v[n] = _adamw(weights[n], grad_w[n], given["m_" + n], given["v_" + n])
    return (loss, grad_x, *[grad_w[n] for n in TWIN_WEIGHTS], *[delta_w[n] for n in TWIN_WEIGHTS],
            *[new_m[n] for n in TWIN_WEIGHTS], *[new_v[n] for n in TWIN_WEIGHTS])
```

```python
import functools
import math

import jax
import jax.numpy as jnp
from jax import lax
from jax.experimental import pallas as pl
from jax.experimental.pallas import tpu as pltpu

F32 = jnp.float32
BF16 = jnp.bfloat16
MESH = pl.DeviceIdType.MESH
ANY = pl.BlockSpec(memory_space=pl.ANY)

NDEV = 8
D = 1024
DFF = 2816
DEPTH = 4
NMETA = 16
NULL = 112
GRID_W = 64
EPS = 1e-6
HP = 128
GLA_H = 4
GLA_DK = 64
GLA_RANK = 16
GLA_TAU = 16.0
CHUNK = 64
ATT_H = 8
ATT_KV = 2
ATT_G = ATT_H // ATT_KV
HEAD_DIM = 64
ROPE_THETA = 10000.0

IN_SIZES = (256, 256, 512, 512, 16, 16, 512, 128, 128, 1024, 1024)
IN_NAMES = ("qa", "ka", "va", "ra", "lrf", "lrb", "qb", "kb", "vb", "ga", "gb")
D_IN = sum(IN_SIZES)
P_ORDER = ("qb", "ga", "gb", "qa", "ka", "va", "ra", "kb", "vb", "lrf", "lrb")
P_WIDTH = dict(qb=1024, ga=1024, gb=1024, qa=512, ka=512, va=512, ra=512, kb=256, vb=256, lrf=128, lrb=128)
P_OFF = {}
_o = 0
for _n in P_ORDER:
    P_OFF[_n] = _o
    _o += P_WIDTH[_n]
D_INP = _o
P_HEADS = dict(qa=(4, 64), ka=(4, 64), qb=(8, 64), kb=(2, 64), vb=(2, 64), lrf=(1, 16), lrb=(1, 16))

ADAM_LR = 0.001
ADAM_B1 = 0.9
ADAM_B2 = 0.999
ADAM_EPS = 1e-08
ADAM_WD = 0.01
ADAM_STEP = 10

VMEM_BIG = 56 * 1024 * 1024


def _cparams(vmem=None):
    return pltpu.CompilerParams(vmem_limit_bytes=vmem) if vmem else pltpu.CompilerParams()


def _pick(n, prefs):
    for p in prefs:
        if n % p == 0:
            return p
    return n


def _tm(lp):
    return _pick(lp, (528, 512, 256, 128))


_DN = {"nn": (((1,), (0,)), ((), ())), "nt": (((1,), (1,)), ((), ())), "tn": (((0,), (0,)), ((), ()))}


def _dot(a, b, mode="nn", precision=None):
    return lax.dot_general(a, b, _DN[mode], preferred_element_type=F32, precision=precision)


def _sigmoid(x):
    return 1.0 / (1.0 + jnp.exp(-x))


def _mm(name, m, n, terms, outs, epilogue, extras=(), *, tm, tn, nk=1, i_outer=False, vmem=None):
    gm, gn = m // tm, n // tn
    assert gm * tm == m and gn * tn == n, (name, m, n, tm, tn)
    n_acc = 1 + max(t[3] for t in terms)

    def gmap(f):
        if i_outer:
            return lambda i, j, kk: f(i, j, kk)
        return lambda j, i, kk: f(i, j, kk)

    in_specs, args = [], []
    for a, b, mode, _, pa, pb in terms:
        kdim = a.shape[-2] if mode == "tn" else a.shape[-1]
        tk = kdim // nk
        assert tk * nk == kdim
        na, nb = (None,) * len(pa), (None,) * len(pb)
        if mode == "tn":
            in_specs.append(pl.BlockSpec(na + (tk, tm), gmap(lambda i, j, kk, pa=pa: pa + (kk, i))))
        else:
            in_specs.append(pl.BlockSpec(na + (tm, tk), gmap(lambda i, j, kk, pa=pa: pa + (i, kk))))
        if mode == "nt":
            in_specs.append(pl.BlockSpec(nb + (tn, tk), gmap(lambda i, j, kk, pb=pb: pb + (j, kk))))
        else:
            in_specs.append(pl.BlockSpec(nb + (tk, tn), gmap(lambda i, j, kk, pb=pb: pb + (kk, j))))
        args += [a, b]
    for arr, kind, off, pe in extras:
        ne = (None,) * len(pe)
        if kind == "mn":
            in_specs.append(pl.BlockSpec(ne + (tm, tn), gmap(lambda i, j, kk, off=off, pe=pe: pe + (i, j + off))))
        else:
            in_specs.append(pl.BlockSpec(ne + (1, tn), gmap(lambda i, j, kk, off=off, pe=pe: pe + (0, j + off))))
        args.append(arr)
    out_shape, out_specs = [], []
    for shape, dtype, kind, off, po in outs:
        no = (None,) * len(po)
        out_shape.append(jax.ShapeDtypeStruct(shape, dtype))
        if kind == "mn":
            out_specs.append(pl.BlockSpec(no + (tm, tn), gmap(lambda i, j, kk, off=off, po=po: po + (i, j + off))))
        else:
            assert not i_outer
            out_specs.append(pl.BlockSpec(no + (1, tn), gmap(lambda i, j, kk, off=off, po=po: po + (0, j + off))))
    n_t, n_e, n_o = len(terms), len(extras), len(outs)
    i_axis = 0 if i_outer else 1

    def body(*refs):
        ins = refs[: 2 * n_t]
        exs = refs[2 * n_t: 2 * n_t + n_e]
        ors = refs[2 * n_t + n_e: 2 * n_t + n_e + n_o]
        accs = refs[2 * n_t + n_e + n_o:]
        i = pl.program_id(i_axis)
        kk = pl.program_id(2)
        part = [None] * n_acc
        for t, (_, _, mode, ai, _, _) in enumerate(terms):
            p = _dot(ins[2 * t][...], ins[2 * t + 1][...], mode)
            part[ai] = p if part[ai] is None else part[ai] + p

        def finish(vals):
            res = epilogue(vals, [e[...] for e in exs], i * tm)
            for (_, dtype, kind, _, _), o_ref, v in zip(outs, ors, res):
                if kind == "mn":
                    o_ref[...] = v.astype(dtype)
                else:
                    @pl.when(i == 0)
                    def _():
                        o_ref[...] = v.astype(dtype)

                    @pl.when(i != 0)
                    def _():
                        o_ref[...] += v.astype(dtype)

        if nk == 1:
            finish(part)
        else:
            @pl.when(kk == 0)
            def _():
                for a_ref, p in zip(accs, part):
                    a_ref[...] = p

            @pl.when(kk != 0)
            def _():
                for a_ref, p in zip(accs, part):
                    a_ref[...] += p

            @pl.when(kk == nk - 1)
            def _():
                finish([a_ref[...] for a_ref in accs])

    scratch = [pltpu.VMEM((tm, tn), F32) for _ in range(n_acc)] if nk > 1 else []
    grid = (gm, gn, nk) if i_outer else (gn, gm, nk)
    res = pl.pallas_call(
        body, name=name, grid=grid, in_specs=in_specs, out_specs=out_specs, out_shape=out_shape,
        scratch_shapes=scratch, compiler_params=_cparams(vmem),
    )(*args)
    return res


def _term(a, b, mode, acc=0, pa=(), pb=()):
    return (a, b, mode, acc, tuple(pa), tuple(pb))


def _row_tile(lp):
    return _pick(lp, (384, 256, 128))


def _rmsnorm_fwd(h, gain):
    lp = h.shape[0]
    tr = _row_tile(lp)

    def body(h_ref, g_ref, z_ref):
        x = h_ref[...]
        r = lax.rsqrt(jnp.mean(x * x, axis=-1, keepdims=True) + EPS)
        z_ref[...] = (x * r * g_ref[...]).astype(BF16)

    return pl.pallas_call(
        body, name="rmsnorm_fwd", grid=(lp // tr,),
        in_specs=[pl.BlockSpec((tr, D), lambda i: (i, 0)), pl.BlockSpec((1, D), lambda i: (0, 0))],
        out_specs=pl.BlockSpec((tr, D), lambda i: (i, 0)),
        out_shape=jax.ShapeDtypeStruct((lp, D), BF16),
    )(h, gain)


def _rmsnorm_bwd(h, dz, gain, dh_res):
    lp = h.shape[0]
    tr = _row_tile(lp)

    def body(h_ref, dz_ref, g_ref, res_ref, dh_ref, dhb_ref, dg_ref):
        i = pl.program_id(0)
        x = h_ref[...]
        r = lax.rsqrt(jnp.mean(x * x, axis=-1, keepdims=True) + EPS)
        xh = x * r
        dz_v = dz_ref[...]
        dxh = dz_v * g_ref[...]
        dx = r * (dxh - xh * jnp.mean(dxh * xh, axis=-1, keepdims=True))
        rows = i * tr + lax.broadcasted_iota(jnp.int32, (tr, 1), 0)
        dh = jnp.where(rows >= NULL, res_ref[...] + dx, 0.0)
        dh_ref[...] = dh
        dhb_ref[...] = dh.astype(BF16)
        part = jnp.sum(dz_v * xh, axis=0, keepdims=True)

        @pl.when(i == 0)
        def _():
            dg_ref[...] = part

        @pl.when(i != 0)
        def _():
            dg_ref[...] += part

    row = pl.BlockSpec((tr, D), lambda i: (i, 0))
    vec = pl.BlockSpec((1, D), lambda i: (0, 0))
    return pl.pallas_call(
        body, name="rmsnorm_bwd", grid=(lp // tr,),
        in_specs=[row, row, vec, row], out_specs=[row, row, vec],
        out_shape=[jax.ShapeDtypeStruct((lp, D), F32), jax.ShapeDtypeStruct((lp, D), BF16),
                   jax.ShapeDtypeStruct((1, D), F32)],
    )(h, dz, gain, dh_res)


def _loss_head(h, target, gain):
    lp = h.shape[0]
    tr = 128

    def body(h_ref, t_ref, g_ref, loss_ref, dh_ref, dhb_ref, dg_ref):
        i = pl.program_id(0)

        @pl.when(i == 0)
        def _():
            loss_ref[...] = jnp.zeros_like(loss_ref)
            dg_ref[...] = jnp.zeros_like(dg_ref)
            dh_ref[...] = jnp.zeros_like(dh_ref)
            dhb_ref[...] = jnp.zeros_like(dhb_ref)

        @pl.when(i != 0)
        def _():
            x = h_ref[...]
            g = g_ref[...]
            r = lax.rsqrt(jnp.mean(x * x, axis=-1, keepdims=True) + EPS)
            xh = x * r
            y = xh * g
            err = y - t_ref[...]
            loss_ref[...] += 0.5 * jnp.sum(jnp.sum(err * err, axis=-1, keepdims=True), axis=0, keepdims=True) / D
            dy = err * (1.0 / D)
            dg_ref[...] += jnp.sum(dy * xh, axis=0, keepdims=True)
            dxh = dy * g
            dx = r * (dxh - xh * jnp.mean(dxh * xh, axis=-1, keepdims=True))
            dh_ref[...] = dx
            dhb_ref[...] = dx.astype(BF16)

    row = pl.BlockSpec((tr, D), lambda i: (i, 0))
    vec = pl.BlockSpec((1, D), lambda i: (0, 0))
    return pl.pallas_call(
        body, name="loss_head", grid=(lp // tr,),
        in_specs=[row, pl.BlockSpec((tr, D), lambda i: (jnp.maximum(i - 1, 0), 0)), vec],
        out_specs=[pl.BlockSpec((1, 1), lambda i: (0, 0)), row, row, vec],
        out_shape=[jax.ShapeDtypeStruct((1, 1), F32), jax.ShapeDtypeStruct((lp, D), F32),
                   jax.ShapeDtypeStruct((lp, D), BF16), jax.ShapeDtypeStruct((1, D), F32)],
    )(h, target, gain)


def _silu_parts(g):
    s = _sigmoid(g)
    return g * s, s * (1.0 + g * (1.0 - s))


def _ffn_fwd(h, gain, wg_t, wu_t, wd, pre):
    lp = h.shape[0]
    tm = _tm(lp)
    z = _rmsnorm_fwd(h, gain)

    def up_epi(accs, exs, row0):
        g, u = accs
        sg, _ = _silu_parts(g)
        return [g, u, sg * u]

    tn = _pick(DFF, (1408, 256, 128))
    bshape = (lp, DFF)
    g_, u_, act = _mm("ffn_up", lp, DFF, [_term(z, wg_t, "nt", 0, (), pre), _term(z, wu_t, "nt", 1, (), pre)],
                      [(bshape, BF16, "mn", 0, ())] * 3, up_epi, tm=tm, tn=tn, vmem=VMEM_BIG)

    def down_epi(accs, exs, row0):
        return [exs[0] + 0.5 * accs[0]]

    nk = _pick(DFF // 128, (2, 1))
    (h2,) = _mm("ffn_down", lp, D, [_term(act, wd, "nn", 0, (), pre)], [((lp, D), F32, "mn", 0, ())], down_epi,
                extras=[(h, "mn", 0, ())], tm=tm, tn=D, nk=nk, i_outer=True, vmem=VMEM_BIG)
    return h2, dict(h=h, z=z, g=g_, u=u_, act=act)


def _dw(name, a, b, m, n, scale=1.0):
    lp = a.shape[0]
    nk = lp // _pick(lp, (384, 256, 128))
    tm = _pick(m, (2944, 1408, 1024, 512, 256, 128))
    tn = _pick(n, (1024, 512, 256, 128))

    def epi(accs, exs, row0):
        return [accs[0] * scale]

    (w,) = _mm(name, m, n, [_term(a, b, "tn")], [((m, n), BF16, "mn", 0, ())], epi, tm=tm, tn=tn, nk=nk,
               i_outer=True, vmem=VMEM_BIG)
    return w


def _ffn_bwd(dh, dhb, sv, gain, wg_t, wu_t, wd, pre):
    lp = dh.shape[0]
    tm = _tm(lp)

    def dact_epi(accs, exs, row0):
        g = exs[0].astype(F32)
        u = exs[1].astype(F32)
        da = 0.5 * accs[0]
        sg, dsg = _silu_parts(g)
        return [da * u * dsg, da * sg]

    tn = _pick(DFF, (1408, 256, 128))
    dg_, du_ = _mm("ffn_dact", lp, DFF, [_term(dhb, wd, "nt", 0, (), pre)],
                   [((lp, DFF), BF16, "mn", 0, ())] * 2, dact_epi,
                   extras=[(sv["g"], "mn", 0, ()), (sv["u"], "mn", 0, ())], tm=tm, tn=tn, vmem=VMEM_BIG)
    d_wd = _dw("dw_down", sv["act"], dhb, DFF, D, 0.5)
    d_wg = _dw("dw_gate", dg_, sv["z"], DFF, D)
    d_wu = _dw("dw_up", du_, sv["z"], DFF, D)

    def id_epi(accs, exs, row0):
        return [accs[0]]

    nk = _pick(DFF // 128, (2, 1))
    (dz,) = _mm("ffn_dz", lp, D, [_term(dg_, wg_t, "nn", 0, (), pre), _term(du_, wu_t, "nn", 0, (), pre)],
                [((lp, D), F32, "mn", 0, ())], id_epi, tm=tm, tn=D, nk=nk, i_outer=True, vmem=VMEM_BIG)
    dh2, dhb2, dgain = _rmsnorm_bwd(sv["h"], dz, gain, dh)
    return dh2, dhb2, dgain, d_wg, d_wu, d_wd


def _hin_spec(tr, name, width=None, extra=0):
    w = P_WIDTH[name] if width is None else width
    base = P_OFF[name] // w
    return w, base


def _gla_gates(hin, w2p, b2p):
    lp = hin.shape[0]
    tr = _row_tile(lp)
    bf, bb = P_OFF["lrf"] // HP, P_OFF["lrb"] // HP

    def body(lf_ref, lb_ref, w_ref, b_ref, o_ref):
        i = pl.program_id(0)
        rows = i * tr + lax.broadcasted_iota(jnp.int32, (tr, 1), 0)
        for d, l_ref in enumerate((lf_ref, lb_ref)):
            logit = _dot(l_ref[...], w_ref[d], precision=lax.Precision.HIGHEST) + b_ref[d]
            g = jax.nn.log_sigmoid(logit) * (1.0 / GLA_TAU)
            o_ref[d] = jnp.where(rows >= NULL, g, 0.0)

    return pl.pallas_call(
        body, name="gla_gates", grid=(lp // tr,),
        in_specs=[pl.BlockSpec((tr, HP), lambda i: (i, bf)), pl.BlockSpec((tr, HP), lambda i: (i, bb)),
                  pl.BlockSpec((2, HP, 512), lambda i: (0, 0, 0)), pl.BlockSpec((2, 1, 512), lambda i: (0, 0, 0))],
        out_specs=pl.BlockSpec((2, tr, 512), lambda i: (0, i, 0)),
        out_shape=jax.ShapeDtypeStruct((2, lp, 512), F32),
    )(hin, hin, w2p, b2p)


def _gla_rows(lp):
    return _pick(lp, (384, 256, 128))


def _tri(d):
    r = lax.broadcasted_iota(jnp.int32, (CHUNK, CHUNK), 0)
    c = lax.broadcasted_iota(jnp.int32, (CHUNK, CHUNK), 1)
    return (r - c) * (1 - 2 * d) >= 0


def _gla_fwd(hin, gates):
    lp = hin.shape[0]
    rb = _gla_rows(lp)
    nb = lp // rb
    cpb = rb // CHUNK
    nchunk = lp // CHUNK
    qo, ko, vo = P_OFF["qa"] // HP, P_OFF["ka"] // HP, P_OFF["va"] // HP
    scale = GLA_DK ** -0.5

    def rbi(d, b):
        return b + d * (nb - 1 - 2 * b)

    def body(q_ref, k_ref, v_ref, g_ref, o_ref, s_ref, st):
        d = pl.program_id(0)
        blk = pl.program_id(2)

        @pl.when(blk == 0)
        def _():
            st[...] = jnp.zeros_like(st)

        tri = _tri(d)
        tmat = tri.astype(F32)
        for ci in range(cpb):
            c = ci + d * (cpb - 1 - 2 * ci)
            r0 = pl.multiple_of(c * CHUNK, CHUNK)
            q = q_ref[pl.ds(r0, CHUNK), :] * scale
            k = k_ref[pl.ds(r0, CHUNK), :]
            v = v_ref[pl.ds(r0, CHUNK), :]
            g = g_ref[pl.ds(r0, CHUNK), :]
            b = _dot(tmat, g, precision=lax.Precision.HIGHEST)
            btot = jnp.sum(g, axis=0, keepdims=True)
            qd = (q * jnp.exp(b)).astype(BF16)
            ki = (k * jnp.exp(-b)).astype(BF16)
            ke = (k * jnp.exp(btot - b)).astype(BF16)
            vb = v.astype(BF16)
            att = jnp.where(tri, _dot(qd, ki, "nt"), 0.0)
            s_prev = st[...]
            o = _dot(att.astype(BF16), vb) + _dot(qd, s_prev.astype(BF16), "nt")
            s_ref[c] = s_prev
            st[...] = s_prev * jnp.exp(btot) + _dot(vb, ke, "tn")
            o_ref[pl.ds(r0, CHUNK), :] = o

    def hspec(off):
        return pl.BlockSpec((rb, HP), lambda d, h, b: (rbi(d, b), off + h))

    return pl.pallas_call(
        body, name="gla_fwd", grid=(2, GLA_H, nb),
        in_specs=[hspec(qo), hspec(ko), hspec(vo), pl.BlockSpec((None, rb, HP), lambda d, h, b: (d, rbi(d, b), h))],
        out_specs=[pl.BlockSpec((None, rb, HP), lambda d, h, b: (d, rbi(d, b), h)),
                   pl.BlockSpec((None, None, cpb, HP, HP), lambda d, h, b: (d, h, rbi(d, b), 0, 0))],
        out_shape=[jax.ShapeDtypeStruct((2, lp, GLA_H * HP), F32),
                   jax.ShapeDtypeStruct((2, GLA_H, nchunk, HP, HP), F32)],
        scratch_shapes=[pltpu.VMEM((HP, HP), F32)],
    )(hin, hin, hin, gates)


def _gla_bwd(hin, gates, states, do):
    lp = hin.shape[0]
    rb = _gla_rows(lp)
    nb = lp // rb
    cpb = rb // CHUNK
    qo, ko, vo = P_OFF["qa"] // HP, P_OFF["ka"] // HP, P_OFF["va"] // HP
    scale = GLA_DK ** -0.5

    def rbi(d, b):
        return (nb - 1 - b) - d * (nb - 1 - 2 * b)

    def body(q_ref, k_ref, v_ref, g_ref, s_ref, do_ref, dq_ref, dk_ref, dv_ref, dg_ref, dst):
        d = pl.program_id(0)
        blk = pl.program_id(2)

        @pl.when(blk == 0)
        def _():
            dst[...] = jnp.zeros_like(dst)

        tri = _tri(d)
        tmat = tri.astype(F32)
        for ci in range(cpb):
            c = (cpb - 1 - ci) - d * (cpb - 1 - 2 * ci)
            r0 = pl.multiple_of(c * CHUNK, CHUNK)
            q = q_ref[pl.ds(r0, CHUNK), :] * scale
            k = k_ref[pl.ds(r0, CHUNK), :]
            v = v_ref[pl.ds(r0, CHUNK), :]
            g = g_ref[pl.ds(r0, CHUNK), :]
            dout = do_ref[pl.ds(r0, CHUNK), :].astype(BF16)
            b = _dot(tmat, g, precision=lax.Precision.HIGHEST)
            btot = jnp.sum(g, axis=0, keepdims=True)
            e = jnp.exp(b)
            ei = jnp.exp(-b)
            et = jnp.exp(btot - b)
            etot = jnp.exp(btot)
            qd = q * e
            ki = k * ei
            ke = k * et
            qdb, kib, keb, vb = qd.astype(BF16), ki.astype(BF16), ke.astype(BF16), v.astype(BF16)
            att = jnp.where(tri, _dot(qdb, kib, "nt"), 0.0).astype(BF16)
            d_att = jnp.where(tri, _dot(dout, vb, "nt"), 0.0).astype(BF16)
            s_prev = s_ref[c]
            ds_t = dst[...]
            ds_b = ds_t.astype(BF16)
            dv = _dot(att, dout, "tn") + _dot(keb, ds_b, "nt")
            d_qd = _dot(d_att, kib) + _dot(dout, s_prev.astype(BF16))
            d_ki = _dot(d_att, qdb, "tn")
            d_ke = _dot(vb, ds_b)
            d_e = jnp.sum(s_prev * ds_t, axis=0, keepdims=True)
            dst[...] = _dot(dout, qdb, "tn") + ds_t * etot
            db = d_qd * qd - d_ki * ki - d_ke * ke
            dbtot = jnp.sum(d_ke * ke, axis=0, keepdims=True) + d_e * etot
            dg = _dot(tmat, db, "tn", precision=lax.Precision.HIGHEST) + dbtot
            dq_ref[pl.ds(r0, CHUNK), :] = d_qd * e * scale
            dk_ref[pl.ds(r0, CHUNK), :] = d_ki * ei + d_ke * et
            dv_ref[pl.ds(r0, CHUNK), :] = dv
            dg_ref[pl.ds(r0, CHUNK), :] = dg

    def hspec(off):
        return pl.BlockSpec((rb, HP), lambda d, h, b: (rbi(d, b), off + h))

    dspec = pl.BlockSpec((None, rb, HP), lambda d, h, b: (d, rbi(d, b), h))
    osh = jax.ShapeDtypeStruct((2, lp, GLA_H * HP), F32)
    return pl.pallas_call(
        body, name="gla_bwd", grid=(2, GLA_H, nb),
        in_specs=[hspec(qo), hspec(ko), hspec(vo), dspec,
                  pl.BlockSpec((None, None, cpb, HP, HP), lambda d, h, b: (d, h, rbi(d, b), 0, 0)),
                  pl.BlockSpec((rb, HP), lambda d, h, b: (rbi(d, b), h))],
        out_specs=[dspec, dspec, dspec, dspec],
        out_shape=[osh, osh, osh, osh],
        scratch_shapes=[pltpu.VMEM((HP, HP), F32)],
    )(hin, hin, hin, gates, states, do)


def _gla_out_fwd(o2, hin, gn):
    lp = hin.shape[0]
    tr = _row_tile(lp)
    ro = P_OFF["ra"] // 512

    def body(o_ref, r_ref, gn_ref, a_ref):
        r = r_ref[...]
        sr, _ = _silu_parts(r)
        for h in range(GLA_H):
            sl = slice(h * HP, (h + 1) * HP)
            o = o_ref[0, :, sl] + o_ref[1, :, sl]
            rs = lax.rsqrt(jnp.mean(o * o, axis=-1, keepdims=True) + EPS)
            a_ref[:, sl] = (o * rs * gn_ref[:, sl] * sr[:, sl]).astype(BF16)

    return pl.pallas_call(
        body, name="gla_out_fwd", grid=(lp // tr,),
        in_specs=[pl.BlockSpec((2, tr, 512), lambda i: (0, i, 0)), pl.BlockSpec((tr, 512), lambda i: (i, ro)),
                  pl.BlockSpec((1, 512), lambda i: (0, 0))],
        out_specs=pl.BlockSpec((tr, 512), lambda i: (i, 0)),
        out_shape=jax.ShapeDtypeStruct((lp, 512), BF16),
    )(o2, hin, gn)


def _gla_out_bwd(da, o2, hin, gn):
    lp = hin.shape[0]
    tr = _row_tile(lp)
    ro = P_OFF["ra"] // 512

    def body(da_ref, o_ref, r_ref, gn_ref, do_ref, dr_ref, dgn_ref):
        i = pl.program_id(0)
        r = r_ref[...]
        sr, dsr = _silu_parts(r)
        da_v = da_ref[...]
        parts = []
        for h in range(GLA_H):
            sl = slice(h * HP, (h + 1) * HP)
            o = o_ref[0, :, sl] + o_ref[1, :, sl]
            rs = lax.rsqrt(jnp.mean(o * o, axis=-1, keepdims=True) + EPS)
            oh = o * rs
            gn_h = gn_ref[:, sl]
            dah = da_v[:, sl]
            dr_ref[:, sl] = (dah * oh * gn_h * dsr[:, sl]).astype(BF16)
            t = dah * sr[:, sl]
            parts.append(jnp.sum(t * oh, axis=0, keepdims=True))
            doh = t * gn_h
            do_ref[:, sl] = rs * (doh - oh * jnp.mean(doh * oh, axis=-1, keepdims=True))
        part = jnp.concatenate(parts, axis=1)

        @pl.when(i == 0)
        def _():
            dgn_ref[...] = part

        @pl.when(i != 0)
        def _():
            dgn_ref[...] += part

    row = pl.BlockSpec((tr, 512), lambda i: (i, 0))
    return pl.pallas_call(
        body, name="gla_out_bwd", grid=(lp // tr,),
        in_specs=[row, pl.BlockSpec((2, tr, 512), lambda i: (0, i, 0)), pl.BlockSpec((tr, 512), lambda i: (i, ro)),
                  pl.BlockSpec((1, 512), lambda i: (0, 0))],
        out_specs=[row, row, pl.BlockSpec((1, 512), lambda i: (0, 0))],
        out_shape=[jax.ShapeDtypeStruct((lp, 512), F32), jax.ShapeDtypeStruct((lp, 512), BF16),
                   jax.ShapeDtypeStruct((1, 512), F32)],
    )(da, o2, hin, gn)


def _gla_in_bwd(dq2, dk2, dv2, dg2, gates, hin, w2p):
    lp = hin.shape[0]
    tr = _row_tile(lp)
    bf, bb = P_OFF["lrf"] // HP, P_OFF["lrb"] // HP

    def body(dq_ref, dk_ref, dv_ref, dg_ref, g_ref, lf_ref, lb_ref, w_ref,
             oq_ref, ok_ref, ov_ref, olr_ref, dw_ref, db_ref):
        i = pl.program_id(0)
        oq_ref[...] = (dq_ref[0] + dq_ref[1]).astype(BF16)
        ok_ref[...] = (dk_ref[0] + dk_ref[1]).astype(BF16)
        ov_ref[...] = (dv_ref[0] + dv_ref[1]).astype(BF16)
        rows = i * tr + lax.broadcasted_iota(jnp.int32, (tr, 1), 0)
        for d, l_ref in enumerate((lf_ref, lb_ref)):
            sig_neg = 1.0 - jnp.exp(GLA_TAU * g_ref[d])
            dlogit = jnp.where(rows >= NULL, dg_ref[d] * (1.0 / GLA_TAU) * sig_neg, 0.0)
            olr_ref[:, d * HP:(d + 1) * HP] = _dot(dlogit, w_ref[d], "nt", precision=lax.Precision.HIGHEST).astype(BF16)
            dw = _dot(l_ref[...], dlogit, "tn", precision=lax.Precision.HIGHEST)
            dbias = jnp.sum(dlogit, axis=0, keepdims=True)

            @pl.when(i == 0)
            def _():
                dw_ref[d] = dw
                db_ref[d] = dbias

            @pl.when(i != 0)
            def _():
                dw_ref[d] += dw
                db_ref[d] += dbias

    two = pl.BlockSpec((2, tr, 512), lambda i: (0, i, 0))
    row = pl.BlockSpec((tr, 512), lambda i: (i, 0))
    return pl.pallas_call(
        body, name="gla_in_bwd", grid=(lp // tr,),
        in_specs=[two, two, two, two, two, pl.BlockSpec((tr, HP), lambda i: (i, bf)),
                  pl.BlockSpec((tr, HP), lambda i: (i, bb)), pl.BlockSpec((2, HP, 512), lambda i: (0, 0, 0))],
        out_specs=[row, row, row, pl.BlockSpec((tr, 2 * HP), lambda i: (i, 0)),
                   pl.BlockSpec((2, HP, 512), lambda i: (0, 0, 0)), pl.BlockSpec((2, 1, 512), lambda i: (0, 0, 0))],
        out_shape=[jax.ShapeDtypeStruct((lp, 512), BF16)] * 3 + [
            jax.ShapeDtypeStruct((lp, 2 * HP), BF16), jax.ShapeDtypeStruct((2, HP, 512), F32),
            jax.ShapeDtypeStruct((2, 1, 512), F32)],
    )(dq2, dk2, dv2, dg2, gates, hin, hin, w2p)


def _rope_tables(lp):
    n_tok = lp - NULL - NMETA
    rows = n_tok // GRID_W
    row = jnp.repeat(jnp.arange(rows), GRID_W).astype(F32)
    col = jnp.tile(jnp.arange(GRID_W), rows).astype(F32)
    inv = ROPE_THETA ** (-jnp.arange(0, 32, 2, dtype=F32) / 32)
    ang = jnp.concatenate([row[:, None] * inv, col[:, None] * inv], axis=-1)
    ang = jnp.concatenate([jnp.zeros((NULL + NMETA, 32), F32), ang], axis=0)
    cos, sin = jnp.cos(ang), jnp.sin(ang)
    z16 = jnp.zeros((lp, 16), F32)
    z64 = jnp.zeros((lp, 64), F32)
    c = jnp.concatenate([cos[:, :16], cos[:, :16], cos[:, 16:], cos[:, 16:], z64], axis=1)
    a = jnp.concatenate([-sin[:, :16], z16, -sin[:, 16:], z16, z64], axis=1)
    b = jnp.concatenate([z16, sin[:, :16], z16, sin[:, 16:], z64], axis=1)
    return c, a, b


def _rope(x, c, a, b):
    return x * c + pltpu.roll(x, HP - 16, 1) * a + pltpu.roll(x, 16, 1) * b


def _rope_t(dx, c, a, b):
    return dx * c + pltpu.roll(dx * a, 16, 1) + pltpu.roll(dx * b, HP - 16, 1)


def _attn_prep(hin, gq, gk, tabs):
    lp = hin.shape[0]
    tr = _row_tile(lp)
    qo, ko, vo = P_OFF["qb"] // 1024, P_OFF["kb"] // 256, P_OFF["vb"] // 256

    def body(q_ref, k_ref, v_ref, gq_ref, gk_ref, c_ref, a_ref, b_ref, oq_ref, ok_ref, ov_ref):
        c, a, b = c_ref[...], a_ref[...], b_ref[...]
        for src, g_ref, dst, nh, sc in ((q_ref, gq_ref, oq_ref, ATT_H, HEAD_DIM ** -0.5), (k_ref, gk_ref, ok_ref, ATT_KV, 1.0)):
            for h in range(nh):
                sl = slice(h * HP, (h + 1) * HP)
                x = src[:, sl]
                r = lax.rsqrt(jnp.sum(x * x, axis=-1, keepdims=True) * (1.0 / HEAD_DIM) + EPS)
                dst[:, sl] = (_rope(x * r * g_ref[...], c, a, b) * sc).astype(BF16)
        ov_ref[...] = v_ref[...].astype(BF16)

    tab = pl.BlockSpec((tr, HP), lambda i: (i, 0))
    vec = pl.BlockSpec((1, HP), lambda i: (0, 0))
    return pl.pallas_call(
        body, name="attn_prep", grid=(lp // tr,),
        in_specs=[pl.BlockSpec((tr, 1024), lambda i: (i, qo)), pl.BlockSpec((tr, 256), lambda i: (i, ko)),
                  pl.BlockSpec((tr, 256), lambda i: (i, vo)), vec, vec, tab, tab, tab],
        out_specs=[pl.BlockSpec((tr, 1024), lambda i: (i, 0)), pl.BlockSpec((tr, 256), lambda i: (i, 0)),
                   pl.BlockSpec((tr, 256), lambda i: (i, 0))],
        out_shape=[jax.ShapeDtypeStruct((lp, 1024), BF16), jax.ShapeDtypeStruct((lp, 256), BF16),
                   jax.ShapeDtypeStruct((lp, 256), BF16)],
    )(hin, hin, hin, gq, gk, *tabs)


def _attn_prep_bwd(dqr, dkr, hin, gq, gk, tabs):
    lp = hin.shape[0]
    tr = _row_tile(lp)
    qo, ko = P_OFF["qb"] // 1024, P_OFF["kb"] // 256

    def body(dq_ref, dk_ref, q_ref, k_ref, gq_ref, gk_ref, c_ref, a_ref, b_ref, oq_ref, ok_ref, dgq_ref, dgk_ref):
        i = pl.program_id(0)
        c, a, b = c_ref[...], a_ref[...], b_ref[...]
        for src, dsrc, g_ref, dst, dg_ref, nh, sc in (
                (q_ref, dq_ref, gq_ref, oq_ref, dgq_ref, ATT_H, HEAD_DIM ** -0.5),
                (k_ref, dk_ref, gk_ref, ok_ref, dgk_ref, ATT_KV, 1.0)):
            acc = jnp.zeros((1, HP), F32)
            for h in range(nh):
                sl = slice(h * HP, (h + 1) * HP)
                x = src[:, sl]
                r = lax.rsqrt(jnp.sum(x * x, axis=-1, keepdims=True) * (1.0 / HEAD_DIM) + EPS)
                xh = x * r
                dxn = _rope_t(dsrc[:, sl] * sc, c, a, b)
                acc = acc + jnp.sum(dxn * xh, axis=0, keepdims=True)
                dxh = dxn * g_ref[...]
                dx = r * (dxh - xh * (jnp.sum(dxh * xh, axis=-1, keepdims=True) * (1.0 / HEAD_DIM)))
                dst[:, sl] = dx.astype(BF16)

            @pl.when(i == 0)
            def _():
                dg_ref[...] = acc

            @pl.when(i != 0)
            def _():
                dg_ref[...] += acc

    tab = pl.BlockSpec((tr, HP), lambda i: (i, 0))
    vec = pl.BlockSpec((1, HP), lambda i: (0, 0))
    return pl.pallas_call(
        body, name="attn_prep_bwd", grid=(lp // tr,),
        in_specs=[pl.BlockSpec((tr, 1024), lambda i: (i, 0)), pl.BlockSpec((tr, 256), lambda i: (i, 0)),
                  pl.BlockSpec((tr, 1024), lambda i: (i, qo)), pl.BlockSpec((tr, 256), lambda i: (i, ko)),
                  vec, vec, tab, tab, tab],
        out_specs=[pl.BlockSpec((tr, 1024), lambda i: (i, 0)), pl.BlockSpec((tr, 256), lambda i: (i, 0)), vec, vec],
        out_shape=[jax.ShapeDtypeStruct((lp, 1024), BF16), jax.ShapeDtypeStruct((lp, 256), BF16),
                   jax.ShapeDtypeStruct((1, HP), F32), jax.ShapeDtypeStruct((1, HP), F32)],
    )(dqr, dkr, hin, hin, gq, gk, *tabs)


QB = 128
GH = 2


def _stack(ref, n):
    return jnp.concatenate([ref[:, g * HP:(g + 1) * HP] for g in range(n)], axis=0)


def _attn_fwd(qr, kr, vb):
    lp = qr.shape[0]
    nq = lp // QB

    def body(q_ref, k_ref, v_ref, o_ref, lse_ref):
        qb = pl.program_id(1)
        qs = _stack(q_ref, GH)
        s = _dot(qs, k_ref[...], "nt")
        keys = lax.broadcasted_iota(jnp.int32, (1, lp), 1)
        s = jnp.where(keys >= NULL, s, -1e30)
        m = jnp.max(s, axis=-1, keepdims=True)
        p = jnp.exp(s - m)
        l = jnp.sum(p, axis=-1, keepdims=True)
        o = _dot(p.astype(BF16), v_ref[...]) / l
        rows = qb * QB + lax.broadcasted_iota(jnp.int32, (QB, 1), 0)
        lse = m + jnp.log(l)
        for g in range(GH):
            og = o[g * QB:(g + 1) * QB]
            o_ref[:, g * HP:(g + 1) * HP] = jnp.where(rows >= NULL, og, 0.0).astype(BF16)
            lse_ref[:, g * HP:(g + 1) * HP] = jnp.broadcast_to(lse[g * QB:(g + 1) * QB], (QB, HP))

    npk = ATT_G // GH
    qspec = pl.BlockSpec((QB, GH * HP), lambda kv, qb, gh: (qb, kv * npk + gh))
    kspec = pl.BlockSpec((lp, HP), lambda kv, qb, gh: (0, kv))
    return pl.pallas_call(
        body, name="attn_fwd", grid=(ATT_KV, nq, npk),
        in_specs=[qspec, kspec, kspec], out_specs=[qspec, qspec],
        out_shape=[jax.ShapeDtypeStruct((lp, ATT_H * HP), BF16), jax.ShapeDtypeStruct((lp, ATT_H * HP), F32)],
        compiler_params=_cparams(VMEM_BIG),
    )(qr, kr, vb)


def _attn_bwd(qr, kr, vb, o, lse, do):
    lp = qr.shape[0]
    nq = lp // QB

    def body(q_ref, k_ref, v_ref, o_ref, lse_ref, do_ref, dq_ref, dk_ref, dv_ref):
        qb = pl.program_id(1)
        gh = pl.program_id(2)

        @pl.when((qb == 0) & (gh == 0))
        def _():
            dk_ref[...] = jnp.zeros_like(dk_ref)
            dv_ref[...] = jnp.zeros_like(dv_ref)

        qs = _stack(q_ref, GH)
        dos = _stack(do_ref, GH)
        os_ = _stack(o_ref, GH).astype(F32)
        lse_s = jnp.concatenate([lse_ref[:, g * HP:g * HP + 1] for g in range(GH)], axis=0)
        delta = jnp.sum(dos * os_, axis=-1, keepdims=True)
        k = k_ref[...]
        s = _dot(qs, k, "nt")
        keys = lax.broadcasted_iota(jnp.int32, (1, lp), 1)
        p = jnp.where(keys >= NULL, jnp.exp(s - lse_s), 0.0)
        dob = dos.astype(BF16)
        dp = _dot(dob, v_ref[...], "nt")
        ds = (p * (dp - delta)).astype(BF16)
        dq = _dot(ds, k)
        for g in range(GH):
            dq_ref[:, g * HP:(g + 1) * HP] = dq[g * QB:(g + 1) * QB]
        dv_ref[...] += _dot(p.astype(BF16), dob, "tn")
        dk_ref[...] += _dot(ds, qs, "tn")

    npk = ATT_G // GH
    qspec = pl.BlockSpec((QB, GH * HP), lambda kv, qb, gh: (qb, kv * npk + gh))
    kspec = pl.BlockSpec((lp, HP), lambda kv, qb, gh: (0, kv))
    return pl.pallas_call(
        body, name="attn_bwd", grid=(ATT_KV, nq, npk),
        in_specs=[qspec, kspec, kspec, qspec, qspec, qspec], out_specs=[qspec, kspec, kspec],
        out_shape=[jax.ShapeDtypeStruct((lp, ATT_H * HP), F32), jax.ShapeDtypeStruct((lp, ATT_KV * HP), F32),
                   jax.ShapeDtypeStruct((lp, ATT_KV * HP), F32)],
        compiler_params=_cparams(VMEM_BIG),
    )(qr, kr, vb, o, lse, do)


def _mixer_fwd(h, gain, wl, l, tabs):
    lp = h.shape[0]
    tm = _tm(lp)
    z = _rmsnorm_fwd(h, gain)

    def id_epi(accs, exs, row0):
        return [accs[0]]

    (hin,) = _mm("in_proj", lp, D_INP, [_term(z, wl["win_t"], "nt", 0, (), (l,))], [((lp, D_INP), F32, "mn", 0, ())],
                 id_epi, tm=tm, tn=D_INP // 2, vmem=VMEM_BIG)
    gates = _gla_gates(hin, wl["w2p"][l], wl["b2p"][l])
    o2, states = _gla_fwd(hin, gates)
    a = _gla_out_fwd(o2, hin, wl["gn"][l])
    qr, kr, vb = _attn_prep(hin, wl["gq"][l], wl["gk"][l], tabs)
    b, lse = _attn_fwd(qr, kr, vb)

    def merge_epi(accs, exs, row0):
        pa, pb = accs
        ga, gb, bma, bmb = exs
        y = _sigmoid(ga + bma) * pa + _sigmoid(gb + bmb) * pb
        return [y, pa, pb]

    tn = 512
    y, pa, pb = _mm("merge", lp, D, [_term(a, wl["wpa_t"], "nt", 0, (), (l,)), _term(b, wl["wpb_t"], "nt", 1, (), (l,))],
                    [((lp, D), BF16, "mn", 0, ())] * 3, merge_epi,
                    extras=[(hin, "mn", P_OFF["ga"] // tn, ()), (hin, "mn", P_OFF["gb"] // tn, ()),
                            (wl["bm"], "n", 0, (l, 0)), (wl["bm"], "n", 0, (l, 1))],
                    tm=tm, tn=tn, i_outer=True, vmem=VMEM_BIG)

    def res_epi(accs, exs, row0):
        return [exs[0] + accs[0]]

    (h2,) = _mm("out_proj", lp, D, [_term(y, wl["wout"], "nn", 0, (), (l,))], [((lp, D), F32, "mn", 0, ())], res_epi,
                extras=[(h, "mn", 0, ())], tm=tm, tn=D, i_outer=True, vmem=VMEM_BIG)
    sv = dict(h=h, z=z, hin=hin, gates=gates, o2=o2, states=states, a=a, qr=qr, kr=kr, vb=vb, b=b, lse=lse,
              y=y, pa=pa, pb=pb)
    return h2, sv


def _mixer_bwd(dh, dhb, sv, gain, wl, l, tabs):
    lp = dh.shape[0]
    tm = _tm(lp)
    hin = sv["hin"]
    tn = 512

    def merge_bwd_epi(accs, exs, row0):
        dy = accs[0]
        ga, gb, pa, pb, bma, bmb = exs
        sa = _sigmoid(ga + bma)
        sb = _sigmoid(gb + bmb)
        dga = dy * pa.astype(F32) * sa * (1.0 - sa)
        dgb = dy * pb.astype(F32) * sb * (1.0 - sb)
        return [dy * sa, dy * sb, dga, dgb, jnp.sum(dga, axis=0, keepdims=True), jnp.sum(dgb, axis=0, keepdims=True)]

    big = ((lp, D), BF16, "mn", 0, ())
    vec = ((1, D), F32, "nsum", 0, ())
    dpa, dpb, dga, dgb, dbma, dbmb = _mm(
        "merge_bwd", lp, D, [_term(dhb, wl["wout"], "nt", 0, (), (l,))], [big, big, big, big, vec, vec], merge_bwd_epi,
        extras=[(hin, "mn", P_OFF["ga"] // tn, ()), (hin, "mn", P_OFF["gb"] // tn, ()), (sv["pa"], "mn", 0, ()),
                (sv["pb"], "mn", 0, ()), (wl["bm"], "n", 0, (l, 0)), (wl["bm"], "n", 0, (l, 1))],
        tm=tm, tn=tn, vmem=VMEM_BIG)
    d_wout = _dw("dw_out", sv["y"], dhb, D, D)
    d_wpa_t = _dw("dw_pa", dpa, sv["a"], D, 512)
    d_wpb_t = _dw("dw_pb", dpb, sv["b"], D, ATT_H * HP)

    def id_epi(accs, exs, row0):
        return [accs[0]]

    (da,) = _mm("d_a", lp, 512, [_term(dpa, wl["wpa_t"], "nn", 0, (), (l,))], [((lp, 512), F32, "mn", 0, ())], id_epi,
                tm=tm, tn=512, i_outer=True, vmem=VMEM_BIG)
    (db,) = _mm("d_b", lp, ATT_H * HP, [_term(dpb, wl["wpb_t"], "nn", 0, (), (l,))],
                [((lp, ATT_H * HP), F32, "mn", 0, ())], id_epi, tm=tm, tn=512, i_outer=True, vmem=VMEM_BIG)
    d_o, d_ra, d_gn = _gla_out_bwd(da, sv["o2"], hin, wl["gn"][l])
    dq2, dk2, dv2, dg2 = _gla_bwd(hin, sv["gates"], sv["states"], d_o)
    d_qa, d_ka, d_va, d_lr, d_w2p, d_b2p = _gla_in_bwd(dq2, dk2, dv2, dg2, sv["gates"], hin, wl["w2p"][l])
    dqr, dkr, dvb = _attn_bwd(sv["qr"], sv["kr"], sv["vb"], sv["b"], sv["lse"], db)
    d_qb, d_kb, d_gq, d_gk = _attn_prep_bwd(dqr, dkr, hin, wl["gq"][l], wl["gk"][l], tabs)
    pieces = dict(qb=d_qb, ga=dga, gb=dgb, qa=d_qa, ka=d_ka, va=d_va, ra=d_ra, kb=d_kb, vb=dvb.astype(BF16), lr=d_lr)
    dhin = jnp.concatenate([pieces[n] for n in ("qb", "ga", "gb", "qa", "ka", "va", "ra", "kb", "vb", "lr")], axis=1)
    d_win_t = _dw("dw_in", dhin, sv["z"], D_INP, D)
    (dz,) = _mm("in_proj_dz", lp, D, [_term(dhin, wl["win_t"], "nn", 0, (), (l,))], [((lp, D), F32, "mn", 0, ())],
                id_epi, tm=tm, tn=D, nk=2, i_outer=True, vmem=VMEM_BIG)
    dh2, dhb2, dgain = _rmsnorm_bwd(sv["h"], dz, gain, dh)
    grads = dict(gain=dgain, wout=d_wout, wpa_t=d_wpa_t, wpb_t=d_wpb_t, win_t=d_win_t, gn=d_gn, w2p=d_w2p, b2p=d_b2p,
                 gq=d_gq, gk=d_gk, bma=dbma, bmb=dbmb)
    return dh2, dhb2, grads


def _mesh_pos():
    x, y, c = lax.axis_index("x"), lax.axis_index("y"), lax.axis_index("c")
    chips = [(1 - x, y), (x, 1 - y), (1 - x, 1 - y)]
    return x, y, c, chips


def _dev_index(x, y, c):
    return 4 * x + 2 * y + c


def _all_gather(name, shards, leads):
    nt = len(shards)

    def blk(ref, lead, idx):
        return ref.at[(slice(None),) * lead + (idx,)]

    def body(*refs):
        xs, outs = refs[:nt], refs[nt:2 * nt]
        send_sems, recv_sems, local_sems = refs[2 * nt:]
        x, y, c, chips = _mesh_pos()
        me, sibling = (x, y, c), (x, y, 1 - c)

        def copy(t, k, block, to, own=False):
            dst = blk(outs[t], leads[t], _dev_index(*block))
            return pltpu.make_async_remote_copy(
                src_ref=xs[t] if own else dst, dst_ref=dst, send_sem=send_sems.at[t, k], recv_sem=recv_sems.at[t, k],
                device_id=to, device_id_type=MESH)

        locals_ = [pltpu.make_async_copy(xs[t], blk(outs[t], leads[t], _dev_index(*me)), local_sems.at[t])
                   for t in range(nt)]
        for cp in locals_:
            cp.start()
        first = []
        for t in range(nt):
            first.append(copy(t, 0, me, sibling, own=True))
            first += [copy(t, 1 + j, me, (*chip, c), own=True) for j, chip in enumerate(chips)]
        for cp in first:
            cp.start()
        passed = []
        for j, chip in enumerate(chips):
            for t in range(nt):
                copy(t, 1 + j, (*chip, c), me).wait_recv()
                fw = copy(t, 4 + j, (*chip, c), sibling)
                fw.start()
                passed.append(fw)
        for t in range(nt):
            copy(t, 0, sibling, me).wait_recv()
        for j, chip in enumerate(chips):
            for t in range(nt):
                copy(t, 4 + j, (*chip, 1 - c), me).wait_recv()
        for cp in first + passed:
            cp.wait_send()
        for cp in locals_:
            cp.wait()

    out_shape = [jax.ShapeDtypeStruct(s.shape[:ld] + (NDEV,) + s.shape[ld:], s.dtype) for s, ld in zip(shards, leads)]
    return pl.pallas_call(
        body, name=name, in_specs=[ANY] * nt, out_specs=[ANY] * nt, out_shape=out_shape,
        scratch_shapes=[pltpu.SemaphoreType.DMA((nt, 7)), pltpu.SemaphoreType.DMA((nt, 7)),
                        pltpu.SemaphoreType.DMA((nt,))],
    )(*shards)


def _exchange_sibling(name, gs):
    nt = len(gs)

    def body(*refs):
        xs, outs = refs[:nt], refs[nt:2 * nt]
        send_sems, recv_sems = refs[2 * nt:]
        x, y, c, _ = _mesh_pos()
        sibling = (x, y, 1 - c)
        copies = []
        for t in range(nt):
            for ch in range(4):
                copies.append(pltpu.make_async_remote_copy(
                    src_ref=xs[t].at[2 * ch + (1 - c)], dst_ref=outs[t].at[ch],
                    send_sem=send_sems.at[t, ch], recv_sem=recv_sems.at[t, ch],
                    device_id=sibling, device_id_type=MESH))
        for cp in copies:
            cp.start()
        for cp in copies:
            cp.wait()

    out_shape = [jax.ShapeDtypeStruct((4,) + g.shape[1:], g.dtype) for g in gs]
    return pl.pallas_call(
        body, name=name, in_specs=[ANY] * nt, out_specs=[ANY] * nt, out_shape=out_shape,
        scratch_shapes=[pltpu.SemaphoreType.DMA((nt, 4)), pltpu.SemaphoreType.DMA((nt, 4))],
    )(*gs)


def _pair_sum(name, gs, recv):
    c = lax.axis_index("c")
    outs = []
    for t, (g, rv) in enumerate(zip(gs, recv)):
        _, r, cols = rv.shape

        def body(c_ref, g_ref, r_ref, o_ref):
            o_ref[...] = (g_ref[...].astype(F32) + r_ref[...].astype(F32)).astype(o_ref.dtype)

        outs.append(pl.pallas_call(
            body, name=f"{name}_{t}",
            grid_spec=pltpu.PrefetchScalarGridSpec(
                num_scalar_prefetch=1, grid=(4,),
                in_specs=[pl.BlockSpec((None, r, cols), lambda ch, cr: (2 * ch + cr[0], 0, 0)),
                          pl.BlockSpec((None, r, cols), lambda ch, cr: (ch, 0, 0))],
                out_specs=pl.BlockSpec((None, r, cols), lambda ch, cr: (ch, 0, 0))),
            out_shape=jax.ShapeDtypeStruct(rv.shape, rv.dtype),
        )(jnp.reshape(c, (1,)).astype(jnp.int32), g, rv))
    return outs


def _exchange_chips(name, ps):
    nt = len(ps)

    def body(*refs):
        xs, outs = refs[:nt], refs[nt:2 * nt]
        send_sems, recv_sems = refs[2 * nt:]
        x, y, c, chips = _mesh_pos()
        copies = []
        for t in range(nt):
            for j, (cx, cy) in enumerate(chips):
                copies.append(pltpu.make_async_remote_copy(
                    src_ref=xs[t].at[2 * cx + cy], dst_ref=outs[t].at[j],
                    send_sem=send_sems.at[t, j], recv_sem=recv_sems.at[t, j],
                    device_id=(cx, cy, c), device_id_type=MESH))
        for cp in copies:
            cp.start()
        for cp in copies:
            cp.wait()

    out_shape = [jax.ShapeDtypeStruct((3,) + p.shape[1:], p.dtype) for p in ps]
    return pl.pallas_call(
        body, name=name, in_specs=[ANY] * nt, out_specs=[ANY] * nt, out_shape=out_shape,
        scratch_shapes=[pltpu.SemaphoreType.DMA((nt, 3)), pltpu.SemaphoreType.DMA((nt, 3))],
    )(*ps)


def _final_sum(name, ps, recv):
    chip = 2 * lax.axis_index("x") + lax.axis_index("y")
    outs = []
    for t, (p, rv) in enumerate(zip(ps, recv)):
        _, r, cols = rv.shape

        def body(c_ref, p_ref, r0_ref, r1_ref, r2_ref, o_ref):
            o_ref[...] = ((p_ref[...].astype(F32) + r0_ref[...].astype(F32)) + r1_ref[...].astype(F32)) + r2_ref[...].astype(F32)

        outs.append(pl.pallas_call(
            body, name=f"{name}_{t}",
            grid_spec=pltpu.PrefetchScalarGridSpec(
                num_scalar_prefetch=1, grid=(1,),
                in_specs=[pl.BlockSpec((None, r, cols), lambda i, cr: (cr[0], 0, 0))] +
                         [pl.BlockSpec((None, r, cols), lambda i, cr, j=j: (j, 0, 0)) for j in range(3)],
                out_specs=pl.BlockSpec((r, cols), lambda i, cr: (0, 0))),
            out_shape=jax.ShapeDtypeStruct((r, cols), F32),
        )(jnp.reshape(chip, (1,)).astype(jnp.int32), p, rv, rv, rv))
    return outs


def _reduce_scatter(tag, gs):
    recv1 = _exchange_sibling(f"rs_sibling_{tag}", gs)
    ps = _pair_sum(f"rs_pair_{tag}", gs, recv1)
    recv2 = _exchange_chips(f"rs_chips_{tag}", ps)
    return _final_sum(f"rs_sum_{tag}", ps, recv2)


def _sum_gathered(g):
    _, r, cols = g.shape

    def body(g_ref, o_ref):
        acc = g_ref[0]
        for d in range(1, NDEV):
            acc = acc + g_ref[d]
        o_ref[...] = acc

    return pl.pallas_call(body, name="small_sum", out_shape=jax.ShapeDtypeStruct((r, cols), F32))(g)


def _adamw(w, g, m, v):
    shape = w.shape
    cols = shape[-1]
    rows = math.prod(shape[:-1]) if len(shape) > 1 else 1
    w2, g2, m2, v2 = (jnp.reshape(t, (rows, cols)) for t in (w, g, m, v))
    tr = _pick(rows, (512, 256, 128)) if rows * cols > 65536 else rows
    c1 = 1.0 / (1.0 - ADAM_B1 ** ADAM_STEP)
    c2 = 1.0 / (1.0 - ADAM_B2 ** ADAM_STEP)

    def body(w_ref, g_ref, m_ref, v_ref, d_ref, nm_ref, nv_ref):
        gv = g_ref[...]
        nm = ADAM_B1 * m_ref[...] + (1.0 - ADAM_B1) * gv
        nv = ADAM_B2 * v_ref[...] + (1.0 - ADAM_B2) * (gv * gv)
        d_ref[...] = -ADAM_LR * ((nm * c1) / (jnp.sqrt(nv * c2) + ADAM_EPS) + ADAM_WD * w_ref[...])
        nm_ref[...] = nm
        nv_ref[...] = nv

    spec = pl.BlockSpec((tr, cols), lambda i: (i, 0))
    osh = jax.ShapeDtypeStruct((rows, cols), F32)
    d, nm, nv = pl.pallas_call(
        body, name="adamw", grid=(rows // tr,), in_specs=[spec] * 4, out_specs=[spec] * 3, out_shape=[osh] * 3,
    )(w2, g2, m2, v2)
    return jnp.reshape(d, shape), jnp.reshape(nm, shape), jnp.reshape(nv, shape)


def _pad_heads(w, name):
    if name not in P_HEADS:
        return w
    nh, real = P_HEADS[name]
    w = jnp.reshape(w, w.shape[:-2] + (nh, real, w.shape[-1]))
    w = jnp.pad(w, [(0, 0)] * (w.ndim - 2) + [(0, HP - real), (0, 0)])
    return jnp.reshape(w, w.shape[:-3] + (nh * HP, w.shape[-1]))


def _unpad_heads(w, name):
    if name not in P_HEADS:
        return w
    nh, real = P_HEADS[name]
    w = jnp.reshape(w, w.shape[:-2] + (nh, HP, w.shape[-1]))[..., :real, :]
    return jnp.reshape(w, w.shape[:-3] + (nh * real, w.shape[-1]))


def _win_pad(win_t):
    segs, o = {}, 0
    for n, s in zip(IN_NAMES, IN_SIZES):
        segs[n] = win_t[..., o:o + s, :]
        o += s
    return jnp.concatenate([_pad_heads(segs[n], n) for n in P_ORDER], axis=-2)


def _win_unpad(win_p):
    segs = {n: _unpad_heads(win_p[..., P_OFF[n]:P_OFF[n] + P_WIDTH[n], :], n) for n in P_ORDER}
    return jnp.concatenate([segs[n] for n in IN_NAMES], axis=-2)


def _t(w):
    return jnp.swapaxes(w, -1, -2)


def _local_step(x, target, wl):
    seq = x.shape[0]
    lp = seq + NULL + NMETA
    tabs = _rope_tables(lp)
    h = jnp.concatenate([jnp.zeros((NULL, D), F32), wl["meta"], x], axis=0)
    saved = []
    for l in range(DEPTH):
        h, s0 = _ffn_fwd(h, wl["gains"][l, 0], wl["wg_t"], wl["wu_t"], wl["wd"], (l, 0))
        h, s1 = _mixer_fwd(h, wl["gains"][l, 1], wl, l, tabs)
        h, s2 = _ffn_fwd(h, wl["gains"][l, 2], wl["wg_t"], wl["wu_t"], wl["wd"], (l, 1))
        saved.append((s0, s1, s2))
    loss, dh, dhb, d_final = _loss_head(h, target, wl["final"])
    grads = []
    for l in reversed(range(DEPTH)):
        s0, s1, s2 = saved[l]
        dh, dhb, dg2, dwg1, dwu1, dwd1 = _ffn_bwd(dh, dhb, s2, wl["gains"][l, 2], wl["wg_t"], wl["wu_t"], wl["wd"], (l, 1))
        dh, dhb, gm = _mixer_bwd(dh, dhb, s1, wl["gains"][l, 1], wl, l, tabs)
        dh, dhb, dg0, dwg0, dwu0, dwd0 = _ffn_bwd(dh, dhb, s0, wl["gains"][l, 0], wl["wg_t"], wl["wu_t"], wl["wd"], (l, 0))
        gm.update(gain0=dg0, gain2=dg2, wg0=dwg0, wu0=dwu0, wd0=dwd0, wg1=dwg1, wu1=dwu1, wd1=dwd1)
        grads.append(gm)
    grads = grads[::-1]
    return loss, dh, d_final, grads


def kernel(x, meta_tokens, norm_gains, ffn_w_gate, ffn_w_up, ffn_w_down, w_in, gla_w2, gla_b2, gla_gn, q_norm, k_norm, w_pa, w_pb, b_merge, w_out, final_norm, loss_target, m_meta_tokens, m_norm_gains, m_ffn_w_gate, m_ffn_w_up, m_ffn_w_down, m_w_in, m_gla_w2, m_gla_b2, m_gla_gn, m_q_norm, m_k_norm, m_w_pa, m_w_pb, m_b_merge, m_w_out, m_final_norm, v_meta_tokens, v_norm_gains, v_ffn_w_gate, v_ffn_w_up, v_ffn_w_down, v_w_in, v_gla_w2, v_gla_b2, v_gla_gn, v_q_norm, v_k_norm, v_w_pa, v_w_pb, v_b_merge, v_w_out, v_final_norm):
    dev = _dev_index(lax.axis_index("x"), lax.axis_index("y"), lax.axis_index("c"))
    sh_g = _t(ffn_w_gate).astype(BF16)
    sh_u = _t(ffn_w_up).astype(BF16)
    sh_d = ffn_w_down.astype(BF16)
    sh_in = _t(w_in).astype(BF16)
    sh_pa = _t(w_pa).astype(BF16)
    sh_pb = _t(w_pb).astype(BF16)
    sh_out = w_out.astype(BF16)
    small = jnp.concatenate([jnp.reshape(t, (-1, 128)) for t in
                             (meta_tokens, norm_gains, gla_w2, gla_b2, b_merge)], axis=0)
    small = jnp.pad(small, ((0, 2), (0, 0)))
    g_g, g_u, g_d, g_in, g_pa, g_pb, g_out, g_small = _all_gather(
        "gather_weights", [sh_g, sh_u, sh_d, sh_in, sh_pa, sh_pb, sh_out, small], [2, 2, 2, 1, 1, 1, 1, 0])
    meta_full = jnp.reshape(jnp.transpose(g_small[:, 0:16], (1, 0, 2)), (NMETA, D))
    gains_full = jnp.reshape(jnp.transpose(jnp.reshape(g_small[:, 16:28], (NDEV, DEPTH, 3, 128)), (1, 2, 0, 3)), (DEPTH, 3, 1, D))
    w2_full = jnp.reshape(jnp.transpose(jnp.reshape(g_small[:, 28:60], (NDEV, DEPTH, 2, GLA_RANK, 32)), (1, 2, 3, 0, 4)),
                          (DEPTH, 2, GLA_RANK, 256))
    b2_full = jnp.reshape(jnp.transpose(jnp.reshape(g_small[:, 60:62], (NDEV, DEPTH, 2, 32)), (1, 2, 0, 3)), (DEPTH, 2, 256))
    bm_full = jnp.reshape(jnp.transpose(jnp.reshape(g_small[:, 62:70], (NDEV, DEPTH, 2, 128)), (1, 2, 0, 3)), (DEPTH, 2, 1, D))
    w2p = jnp.pad(jnp.reshape(w2_full, (DEPTH, 2, GLA_RANK, GLA_H, GLA_DK)),
                  ((0, 0), (0, 0), (0, HP - GLA_RANK), (0, 0), (0, HP - GLA_DK)))
    w2p = jnp.reshape(w2p, (DEPTH, 2, HP, GLA_H * HP))
    b2p = jnp.reshape(jnp.pad(jnp.reshape(b2_full, (DEPTH, 2, 1, GLA_H, GLA_DK)),
                              ((0, 0), (0, 0), (0, 0), (0, 0), (0, HP - GLA_DK))), (DEPTH, 2, 1, GLA_H * HP))
    wpb_t = jnp.reshape(g_pb, (DEPTH, D, ATT_H, HEAD_DIM))
    wpb_t = jnp.reshape(jnp.pad(wpb_t, ((0, 0), (0, 0), (0, 0), (0, HP - HEAD_DIM))), (DEPTH, D, ATT_H * HP))
    wl = dict(
        meta=meta_full, gains=gains_full, final=jnp.reshape(final_norm, (1, D)),
        wg_t=jnp.reshape(g_g, (DEPTH, 2, DFF, D)), wu_t=jnp.reshape(g_u, (DEPTH, 2, DFF, D)),
        wd=jnp.reshape(g_d, (DEPTH, 2, DFF, D)),
        win_t=_win_pad(jnp.reshape(g_in, (DEPTH, D_IN, D))),
        wpa_t=jnp.reshape(g_pa, (DEPTH, D, 512)), wpb_t=wpb_t, wout=jnp.reshape(g_out, (DEPTH, D, D)),
        w2p=w2p, b2p=b2p, bm=bm_full,
        gn=jnp.reshape(gla_gn, (DEPTH, 1, 512)),
        gq=jnp.pad(jnp.reshape(q_norm, (DEPTH, 1, HEAD_DIM)), ((0, 0), (0, 0), (0, HP - HEAD_DIM))),
        gk=jnp.pad(jnp.reshape(k_norm, (DEPTH, 1, HEAD_DIM)), ((0, 0), (0, 0), (0, HP - HEAD_DIM))),
    )
    loss, dh, d_final, grads = _local_step(x[0], loss_target[0], wl)
    loss = lax.psum(loss[0, 0], ("x", "y", "c"))
    grad_x = dh[NULL + NMETA:][None]

    red = []
    for l in range(DEPTH):
        g = grads[l]
        d_in = _win_unpad(g["win_t"])
        d_pb = jnp.reshape(jnp.reshape(g["wpb_t"], (D, ATT_H, HP))[:, :, :HEAD_DIM], (D, 512))
        ts = [g["wg0"], g["wu0"], g["wd0"], g["wg1"], g["wu1"], g["wd1"], d_in, g["wpa_t"], d_pb, g["wout"]]
        red.append(_reduce_scatter(str(l), [jnp.reshape(t, (NDEV, t.shape[0] // NDEV, t.shape[1])) for t in ts]))
    g_gate = jnp.stack([jnp.stack([_t(red[l][0]), _t(red[l][3])]) for l in range(DEPTH)])
    g_up = jnp.stack([jnp.stack([_t(red[l][1]), _t(red[l][4])]) for l in range(DEPTH)])
    g_down = jnp.stack([jnp.stack([red[l][2], red[l][5]]) for l in range(DEPTH)])
    g_win = jnp.stack([_t(red[l][6]) for l in range(DEPTH)])
    g_wpa = jnp.stack([_t(red[l][7]) for l in range(DEPTH)])
    g_wpb = jnp.stack([_t(red[l][8]) for l in range(DEPTH)])
    g_wout = jnp.stack([red[l][9] for l in range(DEPTH)])

    d_meta = dh[NULL:NULL + NMETA]
    d_gains = jnp.stack([jnp.concatenate([grads[l]["gain0"], grads[l]["gain"], grads[l]["gain2"]], axis=0)
                         for l in range(DEPTH)])
    d_w2 = jnp.stack([jnp.reshape(jnp.reshape(grads[l]["w2p"], (2, HP, GLA_H, HP))[:, :GLA_RANK, :, :GLA_DK],
                                  (2, GLA_RANK, 256)) for l in range(DEPTH)])
    d_b2 = jnp.stack([jnp.reshape(jnp.reshape(grads[l]["b2p"], (2, GLA_H, HP))[:, :, :GLA_DK], (2, 256))
                      for l in range(DEPTH)])
    d_gn = jnp.stack([grads[l]["gn"][0] for l in range(DEPTH)])
    d_gq = jnp.stack([grads[l]["gq"][0, :HEAD_DIM] for l in range(DEPTH)])
    d_gk = jnp.stack([grads[l]["gk"][0, :HEAD_DIM] for l in range(DEPTH)])
    d_bm = jnp.stack([jnp.concatenate([grads[l]["bma"], grads[l]["bmb"]], axis=0) for l in range(DEPTH)])
    parts = [d_meta, d_gains, d_w2, d_b2, d_gn, d_gq, d_gk, d_bm, d_final[0]]
    sizes = [p.size for p in parts]
    flat = jnp.concatenate([jnp.reshape(p, (-1,)) for p in parts])
    flat = jnp.reshape(flat, (-1, 128))
    nrow = flat.shape[0]
    flat = jnp.pad(flat, ((0, (-nrow) % 8), (0, 0)))
    (g_flat,) = _all_gather("gather_small_grads", [flat], [0])
    tot = jnp.reshape(_sum_gathered(g_flat), (-1,))
    full, o = [], 0
    for p, s in zip(parts, sizes):
        full.append(jnp.reshape(tot[o:o + s], p.shape))
        o += s
    f_meta, f_gains, f_w2, f_b2, f_gn, f_gq, f_gk, f_bm, f_final = full

    def mine(t, width):
        return lax.dynamic_slice_in_dim(t, dev * width, width, axis=t.ndim - 1)

    g_small = dict(meta_tokens=mine(f_meta, 128), norm_gains=mine(f_gains, 128), gla_w2=mine(f_w2, 32),
                   gla_b2=mine(f_b2, 32), gla_gn=f_gn, q_norm=f_gq, k_norm=f_gk, b_merge=mine(f_bm, 128),
                   final_norm=f_final)
    gr = dict(g_small, ffn_w_gate=g_gate, ffn_w_up=g_up, ffn_w_down=g_down, w_in=g_win, w_pa=g_wpa, w_pb=g_wpb,
              w_out=g_wout)
    ws = dict(meta_tokens=meta_tokens, norm_gains=norm_gains, ffn_w_gate=ffn_w_gate, ffn_w_up=ffn_w_up,
              ffn_w_down=ffn_w_down, w_in=w_in, gla_w2=gla_w2, gla_b2=gla_b2, gla_gn=gla_gn, q_norm=q_norm,
              k_norm=k_norm, w_pa=w_pa, w_pb=w_pb, b_merge=b_merge, w_out=w_out, final_norm=final_norm)
    ms = dict(meta_tokens=m_meta_tokens, norm_gains=m_norm_gains, ffn_w_gate=m_ffn_w_gate, ffn_w_up=m_ffn_w_up,
              ffn_w_down=m_ffn_w_down, w_in=m_w_in, gla_w2=m_gla_w2, gla_b2=m_gla_b2, gla_gn=m_gla_gn, q_norm=m_q_norm,
              k_norm=m_k_norm, w_pa=m_w_pa, w_pb=m_w_pb, b_merge=m_b_merge, w_out=m_w_out, final_norm=m_final_norm)
    vs = dict(meta_tokens=v_meta_tokens, norm_gains=v_norm_gains, ffn_w_gate=v_ffn_w_gate, ffn_w_up=v_ffn_w_up,
              ffn_w_down=v_ffn_w_down, w_in=v_w_in, gla_w2=v_gla_w2, gla_b2=v_gla_b2, gla_gn=v_gla_gn, q_norm=v_q_norm,
              k_norm=v_k_norm, w_pa=v_w_pa, w_pb=v_w_pb, b_merge=v_b_merge, w_out=v_w_out, final_norm=v_final_norm)
    names = ["meta_tokens", "norm_gains", "ffn_w_gate", "ffn_w_up", "ffn_w_down", "w_in", "gla_w2", "gla_b2", "gla_gn",
             "q_norm", "k_norm", "w_pa", "w_pb", "b_merge", "w_out", "final_norm"]
    deltas, new_m, new_v = [], [], []
    for n in names:
        dlt, nm, nv = _adamw(ws[n], gr[n], ms[n], vs[n])
        deltas.append(dlt)
        new_m.append(nm)
        new_v.append(nv)
    return (loss, grad_x, *[gr[n] for n in names], *deltas, *new_m, *new_v)
```

```python
import functools
import math

import jax
import jax.numpy as jnp
from jax import lax
from jax.experimental import pallas as pl
from jax.experimental.pallas import tpu as pltpu

F32 = jnp.float32
BF16 = jnp.bfloat16
MESH = pl.DeviceIdType.MESH
ANY = pl.BlockSpec(memory_space=pl.ANY)

NDEV = 8
D = 1024
DFF = 2816
DEPTH = 4
NMETA = 16
NULL = 112
GRID_W = 64
EPS = 1e-6
HP = 128
GLA_H = 4
GLA_DK = 64
GLA_RANK = 16
GLA_TAU = 16.0
CHUNK = 64
ATT_H = 8
ATT_KV = 2
ATT_G = ATT_H // ATT_KV
HEAD_DIM = 64
ROPE_THETA = 10000.0

IN_SIZES = (256, 256, 512, 512, 16, 16, 512, 128, 128, 1024, 1024)
IN_NAMES = ("qa", "ka", "va", "ra", "lrf", "lrb", "qb", "kb", "vb", "ga", "gb")
D_IN = sum(IN_SIZES)
P_ORDER = ("qb", "ga", "gb", "qa", "ka", "va", "ra", "kb", "vb", "lrf", "lrb")
P_WIDTH = dict(qb=1024, ga=1024, gb=1024, qa=512, ka=512, va=512, ra=512, kb=256, vb=256, lrf=128, lrb=128)
P_OFF = {}
_o = 0
for _n in P_ORDER:
    P_OFF[_n] = _o
    _o += P_WIDTH[_n]
D_INP = _o
P_HEADS = dict(qa=(4, 64), ka=(4, 64), qb=(8, 64), kb=(2, 64), vb=(2, 64), lrf=(1, 16), lrb=(1, 16))

ADAM_LR = 0.001
ADAM_B1 = 0.9
ADAM_B2 = 0.999
ADAM_EPS = 1e-08
ADAM_WD = 0.01
ADAM_STEP = 10

VMEM_BIG = 56 * 1024 * 1024


def _cparams(vmem=None):
    return pltpu.CompilerParams(vmem_limit_bytes=vmem) if vmem else pltpu.CompilerParams()


def _pick(n, prefs):
    for p in prefs:
        if n % p == 0:
            return p
    return n


def _tm(lp):
    return _pick(lp, (528, 512, 256, 128))


_DN = {"nn": (((1,), (0,)), ((), ())), "nt": (((1,), (1,)), ((), ())), "tn": (((0,), (0,)), ((), ()))}


def _dot(a, b, mode="nn", precision=None):
    return lax.dot_general(a, b, _DN[mode], preferred_element_type=F32, precision=precision)


def _sigmoid(x):
    return 1.0 / (1.0 + jnp.exp(-x))


def _mm(name, m, n, terms, outs, epilogue, extras=(), *, tm, tn, nk=1, i_outer=False, vmem=None):
    gm, gn = m // tm, n // tn
    assert gm * tm == m and gn * tn == n, (name, m, n, tm, tn)
    n_acc = 1 + max(t[3] for t in terms)

    def gmap(f):
        if i_outer:
            return lambda i, j, kk: f(i, j, kk)
        return lambda j, i, kk: f(i, j, kk)

    in_specs, args = [], []
    for a, b, mode, _, pa, pb in terms:
        kdim = a.shape[-2] if mode == "tn" else a.shape[-1]
        tk = kdim // nk
        assert tk * nk == kdim
        na, nb = (None,) * len(pa), (None,) * len(pb)
        if mode == "tn":
            in_specs.append(pl.BlockSpec(na + (tk, tm), gmap(lambda i, j, kk, pa=pa: pa + (kk, i))))
        else:
            in_specs.append(pl.BlockSpec(na + (tm, tk), gmap(lambda i, j, kk, pa=pa: pa + (i, kk))))
        if mode == "nt":
            in_specs.append(pl.BlockSpec(nb + (tn, tk), gmap(lambda i, j, kk, pb=pb: pb + (j, kk))))
        else:
            in_specs.append(pl.BlockSpec(nb + (tk, tn), gmap(lambda i, j, kk, pb=pb: pb + (kk, j))))
        args += [a, b]
    for arr, kind, off, pe in extras:
        ne = (None,) * len(pe)
        if kind == "mn":
            in_specs.append(pl.BlockSpec(ne + (tm, tn), gmap(lambda i, j, kk, off=off, pe=pe: pe + (i, j + off))))
        else:
            in_specs.append(pl.BlockSpec(ne + (1, tn), gmap(lambda i, j, kk, off=off, pe=pe: pe + (0, j + off))))
        args.append(arr)
    out_shape, out_specs = [], []
    for shape, dtype, kind, off, po in outs:
        no = (None,) * len(po)
        out_shape.append(jax.ShapeDtypeStruct(shape, dtype))
        if kind == "mn":
            out_specs.append(pl.BlockSpec(no + (tm, tn), gmap(lambda i, j, kk, off=off, po=po: po + (i, j + off))))
        else:
            assert not i_outer
            out_specs.append(pl.BlockSpec(no + (1, tn), gmap(lambda i, j, kk, off=off, po=po: po + (0, j + off))))
    n_t, n_e, n_o = len(terms), len(extras), len(outs)
    i_axis = 0 if i_outer else 1

    def body(*refs):
        ins = refs[: 2 * n_t]
        exs = refs[2 * n_t: 2 * n_t + n_e]
        ors = refs[2 * n_t + n_e: 2 * n_t + n_e + n_o]
        accs = refs[2 * n_t + n_e + n_o:]
        i = pl.program_id(i_axis)
        kk = pl.program_id(2)
        part = [None] * n_acc
        for t, (_, _, mode, ai, _, _) in enumerate(terms):
            p = _dot(ins[2 * t][...], ins[2 * t + 1][...], mode)
            part[ai] = p if part[ai] is None else part[ai] + p

        def finish(vals):
            res = epilogue(vals, [e[...] for e in exs], i * tm)
            for (_, dtype, kind, _, _), o_ref, v in zip(outs, ors, res):
                if kind == "mn":
                    o_ref[...] = v.astype(dtype)
                else:
                    @pl.when(i == 0)
                    def _():
                        o_ref[...] = v.astype(dtype)

                    @pl.when(i != 0)
                    def _():
                        o_ref[...] += v.astype(dtype)

        if nk == 1:
            finish(part)
        else:
            @pl.when(kk == 0)
            def _():
                for a_ref, p in zip(accs, part):
                    a_ref[...] = p

            @pl.when(kk != 0)
            def _():
                for a_ref, p in zip(accs, part):
                    a_ref[...] += p

            @pl.when(kk == nk - 1)
            def _():
                finish([a_ref[...] for a_ref in accs])

    scratch = [pltpu.VMEM((tm, tn), F32) for _ in range(n_acc)] if nk > 1 else []
    grid = (gm, gn, nk) if i_outer else (gn, gm, nk)
    res = pl.pallas_call(
        body, name=name, grid=grid, in_specs=in_specs, out_specs=out_specs, out_shape=out_shape,
        scratch_shapes=scratch, compiler_params=_cparams(vmem),
    )(*args)
    return res


def _term(a, b, mode, acc=0, pa=(), pb=()):
    return (a, b, mode, acc, tuple(pa), tuple(pb))


def _row_tile(lp):
    return _pick(lp, (384, 256, 128))


def _rmsnorm_fwd(h, gain):
    lp = h.shape[0]
    tr = _row_tile(lp)

    def body(h_ref, g_ref, z_ref):
        x = h_ref[...]
        r = lax.rsqrt(jnp.mean(x * x, axis=-1, keepdims=True) + EPS)
        z_ref[...] = (x * r * g_ref[...]).astype(BF16)

    return pl.pallas_call(
        body, name="rmsnorm_fwd", grid=(lp // tr,),
        in_specs=[pl.BlockSpec((tr, D), lambda i: (i, 0)), pl.BlockSpec((1, D), lambda i: (0, 0))],
        out_specs=pl.BlockSpec((tr, D), lambda i: (i, 0)),
        out_shape=jax.ShapeDtypeStruct((lp, D), BF16),
    )(h, gain)


def _rmsnorm_bwd(h, dz, gain, dh_res):
    lp = h.shape[0]
    tr = _row_tile(lp)

    def body(h_ref, dz_ref, g_ref, res_ref, dh_ref, dhb_ref, dg_ref):
        i = pl.program_id(0)
        x = h_ref[...]
        r = lax.rsqrt(jnp.mean(x * x, axis=-1, keepdims=True) + EPS)
        xh = x * r
        dz_v = dz_ref[...]
        dxh = dz_v * g_ref[...]
        dx = r * (dxh - xh * jnp.mean(dxh * xh, axis=-1, keepdims=True))
        rows = i * tr + lax.broadcasted_iota(jnp.int32, (tr, 1), 0)
        dh = jnp.where(rows >= NULL, res_ref[...] + dx, 0.0)
        dh_ref[...] = dh
        dhb_ref[...] = dh.astype(BF16)
        part = jnp.sum(dz_v * xh, axis=0, keepdims=True)

        @pl.when(i == 0)
        def _():
            dg_ref[...] = part

        @pl.when(i != 0)
        def _():
            dg_ref[...] += part

    row = pl.BlockSpec((tr, D), lambda i: (i, 0))
    vec = pl.BlockSpec((1, D), lambda i: (0, 0))
    return pl.pallas_call(
        body, name="rmsnorm_bwd", grid=(lp // tr,),
        in_specs=[row, row, vec, row], out_specs=[row, row, vec],
        out_shape=[jax.ShapeDtypeStruct((lp, D), F32), jax.ShapeDtypeStruct((lp, D), BF16),
                   jax.ShapeDtypeStruct((1, D), F32)],
    )(h, dz, gain, dh_res)


def _loss_head(h, target, gain):
    lp = h.shape[0]
    tr = 128

    def body(h_ref, t_ref, g_ref, loss_ref, dh_ref, dhb_ref, dg_ref):
        i = pl.program_id(0)

        @pl.when(i == 0)
        def _():
            loss_ref[...] = jnp.zeros_like(loss_ref)
            dg_ref[...] = jnp.zeros_like(dg_ref)
            dh_ref[...] = jnp.zeros_like(dh_ref)
            dhb_ref[...] = jnp.zeros_like(dhb_ref)

        @pl.when(i != 0)
        def _():
            x = h_ref[...]
            g = g_ref[...]
            r = lax.rsqrt(jnp.mean(x * x, axis=-1, keepdims=True) + EPS)
            xh = x * r
            y = xh * g
            err = y - t_ref[...]
            loss_ref[...] += 0.5 * jnp.sum(jnp.sum(err * err, axis=-1, keepdims=True), axis=0, keepdims=True) / D
            dy = err * (1.0 / D)
            dg_ref[...] += jnp.sum(dy * xh, axis=0, keepdims=True)
            dxh = dy * g
            dx = r * (dxh - xh * jnp.mean(dxh * xh, axis=-1, keepdims=True))
            dh_ref[...] = dx
            dhb_ref[...] = dx.astype(BF16)

    row = pl.BlockSpec((tr, D), lambda i: (i, 0))
    vec = pl.BlockSpec((1, D), lambda i: (0, 0))
    return pl.pallas_call(
        body, name="loss_head", grid=(lp // tr,),
        in_specs=[row, pl.BlockSpec((tr, D), lambda i: (jnp.maximum(i - 1, 0), 0)), vec],
        out_specs=[pl.BlockSpec((1, 1), lambda i: (0, 0)), row, row, vec],
        out_shape=[jax.ShapeDtypeStruct((1, 1), F32), jax.ShapeDtypeStruct((lp, D), F32),
                   jax.ShapeDtypeStruct((lp, D), BF16), jax.ShapeDtypeStruct((1, D), F32)],
    )(h, target, gain)


def _silu_parts(g):
    s = _sigmoid(g)
    return g * s, s * (1.0 + g * (1.0 - s))


def _ffn_fwd(h, gain, wg_t, wu_t, wd, pre):
    lp = h.shape[0]
    tm = _tm(lp)
    z = _rmsnorm_fwd(h, gain)

    def up_epi(accs, exs, row0):
        g, u = accs
        sg, _ = _silu_parts(g)
        return [g, u, sg * u]

    tn = _pick(DFF, (1408, 256, 128))
    bshape = (lp, DFF)
    g_, u_, act = _mm("ffn_up", lp, DFF, [_term(z, wg_t, "nt", 0, (), pre), _term(z, wu_t, "nt", 1, (), pre)],
                      [(bshape, BF16, "mn", 0, ())] * 3, up_epi, tm=tm, tn=tn, vmem=VMEM_BIG)

    def down_epi(accs, exs, row0):
        return [exs[0] + 0.5 * accs[0]]

    nk = _pick(DFF // 128, (2, 1))
    (h2,) = _mm("ffn_down", lp, D, [_term(act, wd, "nn", 0, (), pre)], [((lp, D), F32, "mn", 0, ())], down_epi,
                extras=[(h, "mn", 0, ())], tm=tm, tn=D, nk=nk, i_outer=True, vmem=VMEM_BIG)
    return h2, dict(h=h, z=z, g=g_, u=u_, act=act)


def _dw(name, a, b, m, n, scale=1.0):
    lp = a.shape[0]
    nk = lp // _pick(lp, (384, 256, 128))
    tm = _pick(m, (2944, 1408, 1024, 512, 256, 128))
    tn = _pick(n, (1024, 512, 256, 128))

    def epi(accs, exs, row0):
        return [accs[0] * scale]

    (w,) = _mm(name, m, n, [_term(a, b, "tn")], [((m, n), BF16, "mn", 0, ())], epi, tm=tm, tn=tn, nk=nk,
               i_outer=True, vmem=VMEM_BIG)
    return w


def _ffn_bwd(dh, dhb, sv, gain, wg_t, wu_t, wd, pre):
    lp = dh.shape[0]
    tm = _tm(lp)

    def dact_epi(accs, exs, row0):
        g = exs[0].astype(F32)
        u = exs[1].astype(F32)
        da = 0.5 * accs[0]
        sg, dsg = _silu_parts(g)
        return [da * u * dsg, da * sg]

    tn = _pick(DFF, (1408, 256, 128))
    dg_, du_ = _mm("ffn_dact", lp, DFF, [_term(dhb, wd, "nt", 0, (), pre)],
                   [((lp, DFF), BF16, "mn", 0, ())] * 2, dact_epi,
                   extras=[(sv["g"], "mn", 0, ()), (sv["u"], "mn", 0, ())], tm=tm, tn=tn, vmem=VMEM_BIG)
    d_wd = _dw("dw_down", sv["act"], dhb, DFF, D, 0.5)
    d_wg = _dw("dw_gate", dg_, sv["z"], DFF, D)
    d_wu = _dw("dw_up", du_, sv["z"], DFF, D)

    def id_epi(accs, exs, row0):
        return [accs[0]]

    nk = _pick(DFF // 128, (2, 1))
    (dz,) = _mm("ffn_dz", lp, D, [_term(dg_, wg_t, "nn", 0, (), pre), _term(du_, wu_t, "nn", 0, (), pre)],
                [((lp, D), F32, "mn", 0, ())], id_epi, tm=tm, tn=D, nk=nk, i_outer=True, vmem=VMEM_BIG)
    dh2, dhb2, dgain = _rmsnorm_bwd(sv["h"], dz, gain, dh)
    return dh2, dhb2, dgain, d_wg, d_wu, d_wd


def _hin_spec(tr, name, width=None, extra=0):
    w = P_WIDTH[name] if width is None else width
    base = P_OFF[name] // w
    return w, base


def _gla_gates(hin, w2p, b2p):
    lp = hin.shape[0]
    tr = _row_tile(lp)
    bf, bb = P_OFF["lrf"] // HP, P_OFF["lrb"] // HP

    def body(lf_ref, lb_ref, w_ref, b_ref, o_ref):
        i = pl.program_id(0)
        rows = i * tr + lax.broadcasted_iota(jnp.int32, (tr, 1), 0)
        for d, l_ref in enumerate((lf_ref, lb_ref)):
            logit = _dot(l_ref[...], w_ref[d], precision=lax.Precision.HIGHEST) + b_ref[d]
            g = jax.nn.log_sigmoid(logit) * (1.0 / GLA_TAU)
            o_ref[d] = jnp.where(rows >= NULL, g, 0.0)

    return pl.pallas_call(
        body, name="gla_gates", grid=(lp // tr,),
        in_specs=[pl.BlockSpec((tr, HP), lambda i: (i, bf)), pl.BlockSpec((tr, HP), lambda i: (i, bb)),
                  pl.BlockSpec((2, HP, 512), lambda i: (0, 0, 0)), pl.BlockSpec((2, 1, 512), lambda i: (0, 0, 0))],
        out_specs=pl.BlockSpec((2, tr, 512), lambda i: (0, i, 0)),
        out_shape=jax.ShapeDtypeStruct((2, lp, 512), F32),
    )(hin, hin, w2p, b2p)


def _gla_rows(lp):
    return _pick(lp, (384, 256, 128))


def _tri(d):
    r = lax.broadcasted_iota(jnp.int32, (CHUNK, CHUNK), 0)
    c = lax.broadcasted_iota(jnp.int32, (CHUNK, CHUNK), 1)
    return (r >= c) if d == 0 else (r <= c)


def _gla_fwd(hin, gates):
    lp = hin.shape[0]
    rb = _gla_rows(lp)
    nb = lp // rb
    cpb = rb // CHUNK
    nchunk = lp // CHUNK
    qo, ko, vo = P_OFF["qa"] // 512, P_OFF["ka"] // 512, P_OFF["va"] // 512
    scale = GLA_DK ** -0.5

    def body(qf, kf, vf, gf, qb, kb, vb_, gb, of, ob, sf, sb, st):
        @pl.when(pl.program_id(0) == 0)
        def _():
            st[...] = jnp.zeros_like(st)

        ins = ((qf, kf, vf, gf, of, sf), (qb, kb, vb_, gb, ob, sb))
        for ci in range(cpb):
            for d in range(2):
                q_ref, k_ref, v_ref, g_ref, o_ref, s_ref = ins[d]
                tri = _tri(d)
                tmat = tri.astype(F32)
                c = ci if d == 0 else cpb - 1 - ci
                rows = slice(c * CHUNK, (c + 1) * CHUNK)
                for h in range(GLA_H):
                    sl = slice(h * HP, (h + 1) * HP)
                    q = q_ref[rows, sl] * scale
                    k = k_ref[rows, sl]
                    v = v_ref[rows, sl]
                    g = g_ref[rows, sl]
                    b = _dot(tmat, g, precision=lax.Precision.HIGHEST)
                    btot = jnp.sum(g, axis=0, keepdims=True)
                    qd = (q * jnp.exp(b)).astype(BF16)
                    ki = (k * jnp.exp(-b)).astype(BF16)
                    ke = (k * jnp.exp(btot - b)).astype(BF16)
                    vb = v.astype(BF16)
                    att = jnp.where(tri, _dot(qd, ki, "nt"), 0.0)
                    s_prev = st[d, h]
                    o_ref[rows, sl] = _dot(att.astype(BF16), vb) + _dot(qd, s_prev.astype(BF16), "nt")
                    s_ref[h, c] = s_prev
                    st[d, h] = s_prev * jnp.exp(btot) + _dot(vb, ke, "tn")

    def specs(off):
        return (pl.BlockSpec((rb, 512), lambda b: (b, off)), pl.BlockSpec((rb, 512), lambda b: (nb - 1 - b, off)))

    (qf, qb), (kf, kb), (vf, vb2) = specs(qo), specs(ko), specs(vo)
    gf = pl.BlockSpec((None, rb, 512), lambda b: (0, b, 0))
    gb = pl.BlockSpec((None, rb, 512), lambda b: (1, nb - 1 - b, 0))
    of, ob = specs(0)
    sf = pl.BlockSpec((GLA_H, cpb, HP, HP), lambda b: (0, b, 0, 0))
    sb = pl.BlockSpec((GLA_H, cpb, HP, HP), lambda b: (0, nb - 1 - b, 0, 0))
    osh = jax.ShapeDtypeStruct((lp, GLA_H * HP), F32)
    ssh = jax.ShapeDtypeStruct((GLA_H, nchunk, HP, HP), F32)
    return pl.pallas_call(
        body, name="gla_fwd", grid=(nb,),
        in_specs=[qf, kf, vf, gf, qb, kb, vb2, gb], out_specs=[of, ob, sf, sb], out_shape=[osh, osh, ssh, ssh],
        scratch_shapes=[pltpu.VMEM((2, GLA_H, HP, HP), F32)], compiler_params=_cparams(VMEM_BIG),
    )(hin, hin, hin, gates, hin, hin, hin, gates)


def _gla_bwd(hin, gates, states, do):
    lp = hin.shape[0]
    rb = _gla_rows(lp)
    nb = lp // rb
    cpb = rb // CHUNK
    qo, ko, vo = P_OFF["qa"] // 512, P_OFF["ka"] // 512, P_OFF["va"] // 512
    scale = GLA_DK ** -0.5

    def body(qf, kf, vf, gf, sf, dof, qb, kb, vb_, gb, sb, dob,
             dqf, dkf, dvf, dgf, dqb, dkb, dvb, dgb, dst):
        @pl.when(pl.program_id(0) == 0)
        def _():
            dst[...] = jnp.zeros_like(dst)

        ins = ((qf, kf, vf, gf, sf, dof, dqf, dkf, dvf, dgf), (qb, kb, vb_, gb, sb, dob, dqb, dkb, dvb, dgb))
        for ci in range(cpb):
            for d in range(2):
                q_ref, k_ref, v_ref, g_ref, s_ref, do_ref, dq_ref, dk_ref, dv_ref, dg_ref = ins[d]
                tri = _tri(d)
                tmat = tri.astype(F32)
                c = cpb - 1 - ci if d == 0 else ci
                rows = slice(c * CHUNK, (c + 1) * CHUNK)
                for h in range(GLA_H):
                    sl = slice(h * HP, (h + 1) * HP)
                    q = q_ref[rows, sl] * scale
                    k = k_ref[rows, sl]
                    v = v_ref[rows, sl]
                    g = g_ref[rows, sl]
                    dout = do_ref[rows, sl].astype(BF16)
                    b = _dot(tmat, g, precision=lax.Precision.HIGHEST)
                    btot = jnp.sum(g, axis=0, keepdims=True)
                    e = jnp.exp(b)
                    ei = jnp.exp(-b)
                    et = jnp.exp(btot - b)
                    etot = jnp.exp(btot)
                    qd = q * e
                    ki = k * ei
                    ke = k * et
                    qdb, kib, keb, vb = qd.astype(BF16), ki.astype(BF16), ke.astype(BF16), v.astype(BF16)
                    att = jnp.where(tri, _dot(qdb, kib, "nt"), 0.0).astype(BF16)
                    d_att = jnp.where(tri, _dot(dout, vb, "nt"), 0.0).astype(BF16)
                    s_prev = s_ref[h, c]
                    ds_t = dst[d, h]
                    ds_b = ds_t.astype(BF16)
                    dv = _dot(att, dout, "tn") + _dot(keb, ds_b, "nt")
                    d_qd = _dot(d_att, kib) + _dot(dout, s_prev.astype(BF16))
                    d_ki = _dot(d_att, qdb, "tn")
                    d_ke = _dot(vb, ds_b)
                    d_e = jnp.sum(s_prev * ds_t, axis=0, keepdims=True)
                    dst[d, h] = _dot(dout, qdb, "tn") + ds_t * etot
                    db = d_qd * qd - d_ki * ki - d_ke * ke
                    dbtot = jnp.sum(d_ke * ke, axis=0, keepdims=True) + d_e * etot
                    dq_ref[rows, sl] = d_qd * e * scale
                    dk_ref[rows, sl] = d_ki * ei + d_ke * et
                    dv_ref[rows, sl] = dv
                    dg_ref[rows, sl] = _dot(tmat, db, "tn", precision=lax.Precision.HIGHEST) + dbtot

    def fw(off):
        return pl.BlockSpec((rb, 512), lambda b: (nb - 1 - b, off))

    def bw(off):
        return pl.BlockSpec((rb, 512), lambda b: (b, off))

    gf = pl.BlockSpec((None, rb, 512), lambda b: (0, nb - 1 - b, 0))
    gb = pl.BlockSpec((None, rb, 512), lambda b: (1, b, 0))
    sf = pl.BlockSpec((GLA_H, cpb, HP, HP), lambda b: (0, nb - 1 - b, 0, 0))
    sb = pl.BlockSpec((GLA_H, cpb, HP, HP), lambda b: (0, b, 0, 0))
    osh = jax.ShapeDtypeStruct((lp, GLA_H * HP), F32)
    res = pl.pallas_call(
        body, name="gla_bwd", grid=(nb,),
        in_specs=[fw(qo), fw(ko), fw(vo), gf, sf, fw(0), bw(qo), bw(ko), bw(vo), gb, sb, bw(0)],
        out_specs=[fw(0)] * 4 + [bw(0)] * 4, out_shape=[osh] * 8,
        scratch_shapes=[pltpu.VMEM((2, GLA_H, HP, HP), F32)], compiler_params=_cparams(VMEM_BIG),
    )(hin, hin, hin, gates, states[0], do, hin, hin, hin, gates, states[1], do)
    return res[:4], res[4:]


def _gla_out_fwd(o2, hin, gn):
    lp = hin.shape[0]
    tr = _row_tile(lp)
    ro = P_OFF["ra"] // 512

    def body(of_ref, ob_ref, r_ref, gn_ref, a_ref):
        r = r_ref[...]
        sr, _ = _silu_parts(r)
        for h in range(GLA_H):
            sl = slice(h * HP, (h + 1) * HP)
            o = of_ref[:, sl] + ob_ref[:, sl]
            rs = lax.rsqrt(jnp.mean(o * o, axis=-1, keepdims=True) + EPS)
            a_ref[:, sl] = (o * rs * gn_ref[:, sl] * sr[:, sl]).astype(BF16)

    row = pl.BlockSpec((tr, 512), lambda i: (i, 0))
    return pl.pallas_call(
        body, name="gla_out_fwd", grid=(lp // tr,),
        in_specs=[row, row, pl.BlockSpec((tr, 512), lambda i: (i, ro)), pl.BlockSpec((1, 512), lambda i: (0, 0))],
        out_specs=row,
        out_shape=jax.ShapeDtypeStruct((lp, 512), BF16),
    )(o2[0], o2[1], hin, gn)


def _gla_out_bwd(da, o2, hin, gn):
    lp = hin.shape[0]
    tr = _row_tile(lp)
    ro = P_OFF["ra"] // 512

    def body(da_ref, of_ref, ob_ref, r_ref, gn_ref, do_ref, dr_ref, dgn_ref):
        i = pl.program_id(0)
        r = r_ref[...]
        sr, dsr = _silu_parts(r)
        da_v = da_ref[...]
        parts = []
        for h in range(GLA_H):
            sl = slice(h * HP, (h + 1) * HP)
            o = of_ref[:, sl] + ob_ref[:, sl]
            rs = lax.rsqrt(jnp.mean(o * o, axis=-1, keepdims=True) + EPS)
            oh = o * rs
            gn_h = gn_ref[:, sl]
            dah = da_v[:, sl]
            dr_ref[:, sl] = (dah * oh * gn_h * dsr[:, sl]).astype(BF16)
            t = dah * sr[:, sl]
            parts.append(jnp.sum(t * oh, axis=0, keepdims=True))
            doh = t * gn_h
            do_ref[:, sl] = rs * (doh - oh * jnp.mean(doh * oh, axis=-1, keepdims=True))
        part = jnp.concatenate(parts, axis=1)

        @pl.when(i == 0)
        def _():
            dgn_ref[...] = part

        @pl.when(i != 0)
        def _():
            dgn_ref[...] += part

    row = pl.BlockSpec((tr, 512), lambda i: (i, 0))
    return pl.pallas_call(
        body, name="gla_out_bwd", grid=(lp // tr,),
        in_specs=[row, row, row, pl.BlockSpec((tr, 512), lambda i: (i, ro)), pl.BlockSpec((1, 512), lambda i: (0, 0))],
        out_specs=[row, row, pl.BlockSpec((1, 512), lambda i: (0, 0))],
        out_shape=[jax.ShapeDtypeStruct((lp, 512), F32), jax.ShapeDtypeStruct((lp, 512), BF16),
                   jax.ShapeDtypeStruct((1, 512), F32)],
    )(da, o2[0], o2[1], hin, gn)


def _gla_in_bwd(gf, gb, gates, hin, w2p):
    lp = hin.shape[0]
    tr = _row_tile(lp)
    bf, bb = P_OFF["lrf"] // HP, P_OFF["lrb"] // HP

    def body(dqf_ref, dkf_ref, dvf_ref, dgf_ref, dqb_ref, dkb_ref, dvb_ref, dgb_ref, g_ref, lf_ref, lb_ref, w_ref,
             oq_ref, ok_ref, ov_ref, olr_ref, dw_ref, db_ref):
        i = pl.program_id(0)
        oq_ref[...] = (dqf_ref[...] + dqb_ref[...]).astype(BF16)
        ok_ref[...] = (dkf_ref[...] + dkb_ref[...]).astype(BF16)
        ov_ref[...] = (dvf_ref[...] + dvb_ref[...]).astype(BF16)
        rows = i * tr + lax.broadcasted_iota(jnp.int32, (tr, 1), 0)
        for d, (l_ref, dg_ref) in enumerate(((lf_ref, dgf_ref), (lb_ref, dgb_ref))):
            sig_neg = 1.0 - jnp.exp(GLA_TAU * g_ref[d])
            dlogit = jnp.where(rows >= NULL, dg_ref[...] * (1.0 / GLA_TAU) * sig_neg, 0.0)
            olr_ref[:, d * HP:(d + 1) * HP] = _dot(dlogit, w_ref[d], "nt", precision=lax.Precision.HIGHEST).astype(BF16)
            dw = _dot(l_ref[...], dlogit, "tn", precision=lax.Precision.HIGHEST)
            dbias = jnp.sum(dlogit, axis=0, keepdims=True)

            @pl.when(i == 0)
            def _():
                dw_ref[d] = dw
                db_ref[d] = dbias

            @pl.when(i != 0)
            def _():
                dw_ref[d] += dw
                db_ref[d] += dbias

    two = pl.BlockSpec((2, tr, 512), lambda i: (0, i, 0))
    row = pl.BlockSpec((tr, 512), lambda i: (i, 0))
    return pl.pallas_call(
        body, name="gla_in_bwd", grid=(lp // tr,),
        in_specs=[row] * 8 + [two, pl.BlockSpec((tr, HP), lambda i: (i, bf)),
                  pl.BlockSpec((tr, HP), lambda i: (i, bb)), pl.BlockSpec((2, HP, 512), lambda i: (0, 0, 0))],
        out_specs=[row, row, row, pl.BlockSpec((tr, 2 * HP), lambda i: (i, 0)),
                   pl.BlockSpec((2, HP, 512), lambda i: (0, 0, 0)), pl.BlockSpec((2, 1, 512), lambda i: (0, 0, 0))],
        out_shape=[jax.ShapeDtypeStruct((lp, 512), BF16)] * 3 + [
            jax.ShapeDtypeStruct((lp, 2 * HP), BF16), jax.ShapeDtypeStruct((2, HP, 512), F32),
            jax.ShapeDtypeStruct((2, 1, 512), F32)],
    )(*gf, *gb, gates, hin, hin, w2p)


def _rope_tables(lp):
    n_tok = lp - NULL - NMETA
    rows = n_tok // GRID_W
    row = jnp.repeat(jnp.arange(rows), GRID_W).astype(F32)
    col = jnp.tile(jnp.arange(GRID_W), rows).astype(F32)
    inv = ROPE_THETA ** (-jnp.arange(0, 32, 2, dtype=F32) / 32)
    ang = jnp.concatenate([row[:, None] * inv, col[:, None] * inv], axis=-1)
    ang = jnp.concatenate([jnp.zeros((NULL + NMETA, 32), F32), ang], axis=0)
    cos, sin = jnp.cos(ang), jnp.sin(ang)
    z16 = jnp.zeros((lp, 16), F32)
    z64 = jnp.zeros((lp, 64), F32)
    c = jnp.concatenate([cos[:, :16], cos[:, :16], cos[:, 16:], cos[:, 16:], z64], axis=1)
    a = jnp.concatenate([-sin[:, :16], z16, -sin[:, 16:], z16, z64], axis=1)
    b = jnp.concatenate([z16, sin[:, :16], z16, sin[:, 16:], z64], axis=1)
    return c, a, b


def _rope(x, c, a, b):
    return x * c + pltpu.roll(x, HP - 16, 1) * a + pltpu.roll(x, 16, 1) * b


def _rope_t(dx, c, a, b):
    return dx * c + pltpu.roll(dx * a, 16, 1) + pltpu.roll(dx * b, HP - 16, 1)


def _attn_prep(hin, gq, gk, tabs):
    lp = hin.shape[0]
    tr = _row_tile(lp)
    qo, ko, vo = P_OFF["qb"] // 1024, P_OFF["kb"] // 256, P_OFF["vb"] // 256

    def body(q_ref, k_ref, v_ref, gq_ref, gk_ref, c_ref, a_ref, b_ref, oq_ref, ok_ref, ov_ref):
        c, a, b = c_ref[...], a_ref[...], b_ref[...]
        for src, g_ref, dst, nh, sc in ((q_ref, gq_ref, oq_ref, ATT_H, Q_SCALE), (k_ref, gk_ref, ok_ref, ATT_KV, 1.0)):
            for h in range(nh):
                sl = slice(h * HP, (h + 1) * HP)
                x = src[:, sl]
                r = lax.rsqrt(jnp.sum(x * x, axis=-1, keepdims=True) * (1.0 / HEAD_DIM) + EPS)
                dst[:, sl] = (_rope(x * r * g_ref[...], c, a, b) * sc).astype(BF16)
        lane = lax.broadcasted_iota(jnp.int32, (1, ATT_KV * HP), 1)
        ov_ref[...] = jnp.where(lane % HP == HEAD_DIM, 1.0, v_ref[...]).astype(BF16)

    tab = pl.BlockSpec((tr, HP), lambda i: (i, 0))
    vec = pl.BlockSpec((1, HP), lambda i: (0, 0))
    return pl.pallas_call(
        body, name="attn_prep", grid=(lp // tr,),
        in_specs=[pl.BlockSpec((tr, 1024), lambda i: (i, qo)), pl.BlockSpec((tr, 256), lambda i: (i, ko)),
                  pl.BlockSpec((tr, 256), lambda i: (i, vo)), vec, vec, tab, tab, tab],
        out_specs=[pl.BlockSpec((tr, 1024), lambda i: (i, 0)), pl.BlockSpec((tr, 256), lambda i: (i, 0)),
                   pl.BlockSpec((tr, 256), lambda i: (i, 0))],
        out_shape=[jax.ShapeDtypeStruct((lp, 1024), BF16), jax.ShapeDtypeStruct((lp, 256), BF16),
                   jax.ShapeDtypeStruct((lp, 256), BF16)],
    )(hin, hin, hin, gq, gk, *tabs)


def _attn_prep_bwd(dqr, dkr, hin, gq, gk, tabs):
    lp = hin.shape[0]
    tr = _row_tile(lp)
    qo, ko = P_OFF["qb"] // 1024, P_OFF["kb"] // 256

    def body(dq_ref, dk_ref, q_ref, k_ref, gq_ref, gk_ref, c_ref, a_ref, b_ref, oq_ref, ok_ref, dgq_ref, dgk_ref):
        i = pl.program_id(0)
        c, a, b = c_ref[...], a_ref[...], b_ref[...]
        for src, dsrc, g_ref, dst, dg_ref, nh, sc in (
                (q_ref, dq_ref, gq_ref, oq_ref, dgq_ref, ATT_H, Q_SCALE),
                (k_ref, dk_ref, gk_ref, ok_ref, dgk_ref, ATT_KV, 1.0)):
            acc = jnp.zeros((1, HP), F32)
            for h in range(nh):
                sl = slice(h * HP, (h + 1) * HP)
                x = src[:, sl]
                r = lax.rsqrt(jnp.sum(x * x, axis=-1, keepdims=True) * (1.0 / HEAD_DIM) + EPS)
                xh = x * r
                dxn = _rope_t(dsrc[:, sl] * sc, c, a, b)
                acc = acc + jnp.sum(dxn * xh, axis=0, keepdims=True)
                dxh = dxn * g_ref[...]
                dx = r * (dxh - xh * (jnp.sum(dxh * xh, axis=-1, keepdims=True) * (1.0 / HEAD_DIM)))
                dst[:, sl] = dx.astype(BF16)

            @pl.when(i == 0)
            def _():
                dg_ref[...] = acc

            @pl.when(i != 0)
            def _():
                dg_ref[...] += acc

    tab = pl.BlockSpec((tr, HP), lambda i: (i, 0))
    vec = pl.BlockSpec((1, HP), lambda i: (0, 0))
    return pl.pallas_call(
        body, name="attn_prep_bwd", grid=(lp // tr,),
        in_specs=[pl.BlockSpec((tr, 1024), lambda i: (i, 0)), pl.BlockSpec((tr, 256), lambda i: (i, 0)),
                  pl.BlockSpec((tr, 1024), lambda i: (i, qo)), pl.BlockSpec((tr, 256), lambda i: (i, ko)),
                  vec, vec, tab, tab, tab],
        out_specs=[pl.BlockSpec((tr, 1024), lambda i: (i, 0)), pl.BlockSpec((tr, 256), lambda i: (i, 0)), vec, vec],
        out_shape=[jax.ShapeDtypeStruct((lp, 1024), BF16), jax.ShapeDtypeStruct((lp, 256), BF16),
                   jax.ShapeDtypeStruct((1, HP), F32), jax.ShapeDtypeStruct((1, HP), F32)],
    )(dqr, dkr, hin, hin, gq, gk, *tabs)


QB = 128
GH = 2
Q_SCALE = HEAD_DIM ** -0.5 * math.log2(math.e)
LN2 = math.log(2.0)


def _stack(ref, g0, n):
    return jnp.concatenate([ref[:, (g0 + g) * HP:(g0 + g + 1) * HP] for g in range(n)], axis=0)


def _attn_fwd(qr, kr, vb):
    lp = qr.shape[0]
    nq = lp // QB

    def body(q_ref, k_ref, v_ref, o_ref, lse_ref):
        qb = pl.program_id(1)
        keys = lax.broadcasted_iota(jnp.int32, (1, lp), 1)
        lane = lax.broadcasted_iota(jnp.int32, (1, HP), 1)
        rows = qb * QB + lax.broadcasted_iota(jnp.int32, (QB, 1), 0)
        for ch in range(ATT_G // GH):
            qs = _stack(q_ref, ch * GH, GH)
            s = _dot(qs, k_ref[...], "nt")
            s = jnp.where(keys >= NULL, s, -1e30)
            m = jnp.max(s, axis=-1, keepdims=True)
            p = jnp.exp2(s - m).astype(BF16)
            o_raw = _dot(p, v_ref[...])
            l = jnp.sum(jnp.where(lane == HEAD_DIM, o_raw, 0.0), axis=-1, keepdims=True)
            o = jnp.where(lane < HEAD_DIM, o_raw / l, 0.0)
            lse = m + jnp.log2(l)
            for g in range(GH):
                sl = slice((ch * GH + g) * HP, (ch * GH + g + 1) * HP)
                o_ref[:, sl] = jnp.where(rows >= NULL, o[g * QB:(g + 1) * QB], 0.0).astype(BF16)
                lse_ref[:, sl] = jnp.broadcast_to(lse[g * QB:(g + 1) * QB], (QB, HP))

    qspec = pl.BlockSpec((QB, ATT_G * HP), lambda kv, qb: (qb, kv))
    kspec = pl.BlockSpec((lp, HP), lambda kv, qb: (0, kv))
    return pl.pallas_call(
        body, name="attn_fwd", grid=(ATT_KV, nq),
        in_specs=[qspec, kspec, kspec], out_specs=[qspec, qspec],
        out_shape=[jax.ShapeDtypeStruct((lp, ATT_H * HP), BF16), jax.ShapeDtypeStruct((lp, ATT_H * HP), F32)],
        compiler_params=_cparams(VMEM_BIG),
    )(qr, kr, vb)


def _attn_bwd(qr, kr, vb, o, lse, do):
    lp = qr.shape[0]
    nq = lp // QB

    def body(q_ref, k_ref, v_ref, o_ref, lse_ref, do_ref, dq_ref, dk_ref, dv_ref):
        qb = pl.program_id(1)

        @pl.when(qb == 0)
        def _():
            dk_ref[...] = jnp.zeros_like(dk_ref)
            dv_ref[...] = jnp.zeros_like(dv_ref)

        keys = lax.broadcasted_iota(jnp.int32, (1, lp), 1)
        k = k_ref[...]
        dk_acc, dv_acc = None, None
        for ch in range(ATT_G // GH):
            g0 = ch * GH
            qs = _stack(q_ref, g0, GH)
            dos = _stack(do_ref, g0, GH)
            os_ = _stack(o_ref, g0, GH).astype(F32)
            lse_s = jnp.concatenate([lse_ref[:, (g0 + g) * HP:(g0 + g) * HP + 1] for g in range(GH)], axis=0)
            delta = jnp.sum(dos * os_, axis=-1, keepdims=True) * LN2
            s = _dot(qs, k, "nt")
            p = jnp.where(keys >= NULL, jnp.exp2(s - lse_s), 0.0)
            dob = dos.astype(BF16)
            dp = _dot((dos * LN2).astype(BF16), v_ref[...], "nt")
            ds = (p * (dp - delta)).astype(BF16)
            dq = _dot(ds, k)
            for g in range(GH):
                dq_ref[:, (g0 + g) * HP:(g0 + g + 1) * HP] = dq[g * QB:(g + 1) * QB]
            dv_c = _dot(p.astype(BF16), dob, "tn")
            dk_c = _dot(ds, qs, "tn")
            dv_acc = dv_c if dv_acc is None else dv_acc + dv_c
            dk_acc = dk_c if dk_acc is None else dk_acc + dk_c
        dv_ref[...] += dv_acc
        dk_ref[...] += dk_acc

    qspec = pl.BlockSpec((QB, ATT_G * HP), lambda kv, qb: (qb, kv))
    kspec = pl.BlockSpec((lp, HP), lambda kv, qb: (0, kv))
    return pl.pallas_call(
        body, name="attn_bwd", grid=(ATT_KV, nq),
        in_specs=[qspec, kspec, kspec, qspec, qspec, qspec], out_specs=[qspec, kspec, kspec],
        out_shape=[jax.ShapeDtypeStruct((lp, ATT_H * HP), F32), jax.ShapeDtypeStruct((lp, ATT_KV * HP), F32),
                   jax.ShapeDtypeStruct((lp, ATT_KV * HP), F32)],
        compiler_params=_cparams(VMEM_BIG),
    )(qr, kr, vb, o, lse, do)


def _mixer_fwd(h, gain, wl, l, tabs):
    lp = h.shape[0]
    tm = _tm(lp)
    z = _rmsnorm_fwd(h, gain)

    def id_epi(accs, exs, row0):
        return [accs[0]]

    (hin,) = _mm("in_proj", lp, D_INP, [_term(z, wl["win_t"], "nt", 0, (), (l,))], [((lp, D_INP), F32, "mn", 0, ())],
                 id_epi, tm=tm, tn=D_INP // 2, vmem=VMEM_BIG)
    gates = _gla_gates(hin, wl["w2p"][l], wl["b2p"][l])
    o_f, o_b, s_f, s_b = _gla_fwd(hin, gates)
    o2, states = (o_f, o_b), (s_f, s_b)
    a = _gla_out_fwd(o2, hin, wl["gn"][l])
    qr, kr, vb = _attn_prep(hin, wl["gq"][l], wl["gk"][l], tabs)
    b, lse = _attn_fwd(qr, kr, vb)

    def merge_epi(accs, exs, row0):
        pa, pb = accs
        ga, gb, bma, bmb = exs
        y = _sigmoid(ga + bma) * pa + _sigmoid(gb + bmb) * pb
        return [y, pa, pb]

    tn = 512
    y, pa, pb = _mm("merge", lp, D, [_term(a, wl["wpa_t"], "nt", 0, (), (l,)), _term(b, wl["wpb_t"], "nt", 1, (), (l,))],
                    [((lp, D), BF16, "mn", 0, ())] * 3, merge_epi,
                    extras=[(hin, "mn", P_OFF["ga"] // tn, ()), (hin, "mn", P_OFF["gb"] // tn, ()),
                            (wl["bm"], "n", 0, (l, 0)), (wl["bm"], "n", 0, (l, 1))],
                    tm=tm, tn=tn, i_outer=True, vmem=VMEM_BIG)

    def res_epi(accs, exs, row0):
        return [exs[0] + accs[0]]

    (h2,) = _mm("out_proj", lp, D, [_term(y, wl["wout"], "nn", 0, (), (l,))], [((lp, D), F32, "mn", 0, ())], res_epi,
                extras=[(h, "mn", 0, ())], tm=tm, tn=D, i_outer=True, vmem=VMEM_BIG)
    sv = dict(h=h, z=z, hin=hin, gates=gates, o2=o2, states=states, a=a, qr=qr, kr=kr, vb=vb, b=b, lse=lse,
              y=y, pa=pa, pb=pb)
    return h2, sv


def _mixer_bwd(dh, dhb, sv, gain, wl, l, tabs):
    lp = dh.shape[0]
    tm = _tm(lp)
    hin = sv["hin"]
    tn = 512

    def merge_bwd_epi(accs, exs, row0):
        dy = accs[0]
        ga, gb, pa, pb, bma, bmb = exs
        sa = _sigmoid(ga + bma)
        sb = _sigmoid(gb + bmb)
        dga = dy * pa.astype(F32) * sa * (1.0 - sa)
        dgb = dy * pb.astype(F32) * sb * (1.0 - sb)
        return [dy * sa, dy * sb, dga, dgb, jnp.sum(dga, axis=0, keepdims=True), jnp.sum(dgb, axis=0, keepdims=True)]

    big = ((lp, D), BF16, "mn", 0, ())
    vec = ((1, D), F32, "nsum", 0, ())
    dpa, dpb, dga, dgb, dbma, dbmb = _mm(
        "merge_bwd", lp, D, [_term(dhb, wl["wout"], "nt", 0, (), (l,))], [big, big, big, big, vec, vec], merge_bwd_epi,
        extras=[(hin, "mn", P_OFF["ga"] // tn, ()), (hin, "mn", P_OFF["gb"] // tn, ()), (sv["pa"], "mn", 0, ()),
                (sv["pb"], "mn", 0, ()), (wl["bm"], "n", 0, (l, 0)), (wl["bm"], "n", 0, (l, 1))],
        tm=tm, tn=tn, vmem=VMEM_BIG)
    d_wout = _dw("dw_out", sv["y"], dhb, D, D)
    d_wpa_t = _dw("dw_pa", dpa, sv["a"], D, 512)
    d_wpb_t = _dw("dw_pb", dpb, sv["b"], D, ATT_H * HP)

    def id_epi(accs, exs, row0):
        return [accs[0]]

    (da,) = _mm("d_a", lp, 512, [_term(dpa, wl["wpa_t"], "nn", 0, (), (l,))], [((lp, 512), F32, "mn", 0, ())], id_epi,
                tm=tm, tn=512, i_outer=True, vmem=VMEM_BIG)
    (db,) = _mm("d_b", lp, ATT_H * HP, [_term(dpb, wl["wpb_t"], "nn", 0, (), (l,))],
                [((lp, ATT_H * HP), F32, "mn", 0, ())], id_epi, tm=tm, tn=512, i_outer=True, vmem=VMEM_BIG)
    d_o, d_ra, d_gn = _gla_out_bwd(da, sv["o2"], hin, wl["gn"][l])
    g_fw, g_bw = _gla_bwd(hin, sv["gates"], sv["states"], d_o)
    d_qa, d_ka, d_va, d_lr, d_w2p, d_b2p = _gla_in_bwd(g_fw, g_bw, sv["gates"], hin, wl["w2p"][l])
    dqr, dkr, dvb = _attn_bwd(sv["qr"], sv["kr"], sv["vb"], sv["b"], sv["lse"], db)
    d_qb, d_kb, d_gq, d_gk = _attn_prep_bwd(dqr, dkr, hin, wl["gq"][l], wl["gk"][l], tabs)
    pieces = dict(qb=d_qb, ga=dga, gb=dgb, qa=d_qa, ka=d_ka, va=d_va, ra=d_ra, kb=d_kb, vb=dvb.astype(BF16), lr=d_lr)
    dhin = jnp.concatenate([pieces[n] for n in ("qb", "ga", "gb", "qa", "ka", "va", "ra", "kb", "vb", "lr")], axis=1)
    d_win_t = _dw("dw_in", dhin, sv["z"], D_INP, D)
    (dz,) = _mm("in_proj_dz", lp, D, [_term(dhin, wl["win_t"], "nn", 0, (), (l,))], [((lp, D), F32, "mn", 0, ())],
                id_epi, tm=tm, tn=D, nk=2, i_outer=True, vmem=VMEM_BIG)
    dh2, dhb2, dgain = _rmsnorm_bwd(sv["h"], dz, gain, dh)
    grads = dict(gain=dgain, wout=d_wout, wpa_t=d_wpa_t, wpb_t=d_wpb_t, win_t=d_win_t, gn=d_gn, w2p=d_w2p, b2p=d_b2p,
                 gq=d_gq, gk=d_gk, bma=dbma, bmb=dbmb)
    return dh2, dhb2, grads


def _mesh_pos():
    x, y, c = lax.axis_index("x"), lax.axis_index("y"), lax.axis_index("c")
    chips = [(1 - x, y), (x, 1 - y), (1 - x, 1 - y)]
    return x, y, c, chips


def _dev_index(x, y, c):
    return 4 * x + 2 * y + c


def _all_gather(name, shards, leads):
    nt = len(shards)

    def blk(ref, lead, idx):
        return ref.at[(slice(None),) * lead + (idx,)]

    def body(*refs):
        xs, outs = refs[:nt], refs[nt:2 * nt]
        send_sems, recv_sems, local_sems = refs[2 * nt:]
        x, y, c, chips = _mesh_pos()
        me, sibling = (x, y, c), (x, y, 1 - c)

        def copy(t, k, block, to, own=False):
            dst = blk(outs[t], leads[t], _dev_index(*block))
            return pltpu.make_async_remote_copy(
                src_ref=xs[t] if own else dst, dst_ref=dst, send_sem=send_sems.at[t, k], recv_sem=recv_sems.at[t, k],
                device_id=to, device_id_type=MESH)

        locals_ = [pltpu.make_async_copy(xs[t], blk(outs[t], leads[t], _dev_index(*me)), local_sems.at[t])
                   for t in range(nt)]
        for cp in locals_:
            cp.start()
        first = []
        for t in range(nt):
            first.append(copy(t, 0, me, sibling, own=True))
            first += [copy(t, 1 + j, me, (*chip, c), own=True) for j, chip in enumerate(chips)]
        for cp in first:
            cp.start()
        passed = []
        for j, chip in enumerate(chips):
            for t in range(nt):
                copy(t, 1 + j, (*chip, c), me).wait_recv()
                fw = copy(t, 4 + j, (*chip, c), sibling)
                fw.start()
                passed.append(fw)
        for t in range(nt):
            copy(t, 0, sibling, me).wait_recv()
        for j, chip in enumerate(chips):
            for t in range(nt):
                copy(t, 4 + j, (*chip, 1 - c), me).wait_recv()
        for cp in first + passed:
            cp.wait_send()
        for cp in locals_:
            cp.wait()

    out_shape = [jax.ShapeDtypeStruct(s.shape[:ld] + (NDEV,) + s.shape[ld:], s.dtype) for s, ld in zip(shards, leads)]
    return pl.pallas_call(
        body, name=name, in_specs=[ANY] * nt, out_specs=[ANY] * nt, out_shape=out_shape,
        scratch_shapes=[pltpu.SemaphoreType.DMA((nt, 7)), pltpu.SemaphoreType.DMA((nt, 7)),
                        pltpu.SemaphoreType.DMA((nt,))],
    )(*shards)


def _exchange_sibling(name, gs):
    nt = len(gs)

    def body(*refs):
        xs, outs = refs[:nt], refs[nt:2 * nt]
        send_sems, recv_sems = refs[2 * nt:]
        x, y, c, _ = _mesh_pos()
        sibling = (x, y, 1 - c)
        copies = []
        for t in range(nt):
            for ch in range(4):
                copies.append(pltpu.make_async_remote_copy(
                    src_ref=xs[t].at[2 * ch + (1 - c)], dst_ref=outs[t].at[ch],
                    send_sem=send_sems.at[t, ch], recv_sem=recv_sems.at[t, ch],
                    device_id=sibling, device_id_type=MESH))
        for cp in copies:
            cp.start()
        for cp in copies:
            cp.wait()

    out_shape = [jax.ShapeDtypeStruct((4,) + g.shape[1:], g.dtype) for g in gs]
    return pl.pallas_call(
        body, name=name, in_specs=[ANY] * nt, out_specs=[ANY] * nt, out_shape=out_shape,
        scratch_shapes=[pltpu.SemaphoreType.DMA((nt, 4)), pltpu.SemaphoreType.DMA((nt, 4))],
    )(*gs)


def _pair_sum(name, gs, recv):
    c = lax.axis_index("c")
    outs = []
    for t, (g, rv) in enumerate(zip(gs, recv)):
        _, r, cols = rv.shape

        def body(c_ref, g_ref, r_ref, o_ref):
            o_ref[...] = (g_ref[...].astype(F32) + r_ref[...].astype(F32)).astype(o_ref.dtype)

        outs.append(pl.pallas_call(
            body, name=f"{name}_{t}",
            grid_spec=pltpu.PrefetchScalarGridSpec(
                num_scalar_prefetch=1, grid=(4,),
                in_specs=[pl.BlockSpec((None, r, cols), lambda ch, cr: (2 * ch + cr[0], 0, 0)),
                          pl.BlockSpec((None, r, cols), lambda ch, cr: (ch, 0, 0))],
                out_specs=pl.BlockSpec((None, r, cols), lambda ch, cr: (ch, 0, 0))),
            out_shape=jax.ShapeDtypeStruct(rv.shape, rv.dtype),
        )(jnp.reshape(c, (1,)).astype(jnp.int32), g, rv))
    return outs


def _exchange_chips(name, ps):
    nt = len(ps)

    def body(*refs):
        xs, outs = refs[:nt], refs[nt:2 * nt]
        send_sems, recv_sems = refs[2 * nt:]
        x, y, c, chips = _mesh_pos()
        copies = []
        for t in range(nt):
            for j, (cx, cy) in enumerate(chips):
                copies.append(pltpu.make_async_remote_copy(
                    src_ref=xs[t].at[2 * cx + cy], dst_ref=outs[t].at[j],
                    send_sem=send_sems.at[t, j], recv_sem=recv_sems.at[t, j],
                    device_id=(cx, cy, c), device_id_type=MESH))
        for cp in copies:
            cp.start()
        for cp in copies:
            cp.wait()

    out_shape = [jax.ShapeDtypeStruct((3,) + p.shape[1:], p.dtype) for p in ps]
    return pl.pallas_call(
        body, name=name, in_specs=[ANY] * nt, out_specs=[ANY] * nt, out_shape=out_shape,
        scratch_shapes=[pltpu.SemaphoreType.DMA((nt, 3)), pltpu.SemaphoreType.DMA((nt, 3))],
    )(*ps)


def _final_sum(name, ps, recv):
    chip = 2 * lax.axis_index("x") + lax.axis_index("y")
    outs = []
    for t, (p, rv) in enumerate(zip(ps, recv)):
        _, r, cols = rv.shape

        def body(c_ref, p_ref, r0_ref, r1_ref, r2_ref, o_ref):
            o_ref[...] = ((p_ref[...].astype(F32) + r0_ref[...].astype(F32)) + r1_ref[...].astype(F32)) + r2_ref[...].astype(F32)

        outs.append(pl.pallas_call(
            body, name=f"{name}_{t}",
            grid_spec=pltpu.PrefetchScalarGridSpec(
                num_scalar_prefetch=1, grid=(1,),
                in_specs=[pl.BlockSpec((None, r, cols), lambda i, cr: (cr[0], 0, 0))] +
                         [pl.BlockSpec((None, r, cols), lambda i, cr, j=j: (j, 0, 0)) for j in range(3)],
                out_specs=pl.BlockSpec((r, cols), lambda i, cr: (0, 0))),
            out_shape=jax.ShapeDtypeStruct((r, cols), F32),
        )(jnp.reshape(chip, (1,)).astype(jnp.int32), p, rv, rv, rv))
    return outs


def _reduce_scatter(tag, gs):
    recv1 = _exchange_sibling(f"rs_sibling_{tag}", gs)
    ps = _pair_sum(f"rs_pair_{tag}", gs, recv1)
    recv2 = _exchange_chips(f"rs_chips_{tag}", ps)
    return _final_sum(f"rs_sum_{tag}", ps, recv2)


def _sum_gathered(g):
    _, r, cols = g.shape

    def body(g_ref, o_ref):
        acc = g_ref[0]
        for d in range(1, NDEV):
            acc = acc + g_ref[d]
        o_ref[...] = acc

    return pl.pallas_call(body, name="small_sum", out_shape=jax.ShapeDtypeStruct((r, cols), F32))(g)


def _adamw(w, g, m, v):
    shape = w.shape
    cols = shape[-1]
    rows = math.prod(shape[:-1]) if len(shape) > 1 else 1
    w2, g2, m2, v2 = (jnp.reshape(t, (rows, cols)) for t in (w, g, m, v))
    tr = _pick(rows, (512, 256, 128)) if rows * cols > 65536 else rows
    c1 = 1.0 / (1.0 - ADAM_B1 ** ADAM_STEP)
    c2 = 1.0 / (1.0 - ADAM_B2 ** ADAM_STEP)

    def body(w_ref, g_ref, m_ref, v_ref, d_ref, nm_ref, nv_ref):
        gv = g_ref[...]
        nm = ADAM_B1 * m_ref[...] + (1.0 - ADAM_B1) * gv
        nv = ADAM_B2 * v_ref[...] + (1.0 - ADAM_B2) * (gv * gv)
        d_ref[...] = -ADAM_LR * ((nm * c1) / (jnp.sqrt(nv * c2) + ADAM_EPS) + ADAM_WD * w_ref[...])
        nm_ref[...] = nm
        nv_ref[...] = nv

    spec = pl.BlockSpec((tr, cols), lambda i: (i, 0))
    osh = jax.ShapeDtypeStruct((rows, cols), F32)
    d, nm, nv = pl.pallas_call(
        body, name="adamw", grid=(rows // tr,), in_specs=[spec] * 4, out_specs=[spec] * 3, out_shape=[osh] * 3,
    )(w2, g2, m2, v2)
    return jnp.reshape(d, shape), jnp.reshape(nm, shape), jnp.reshape(nv, shape)


def _pad_heads(w, name):
    if name not in P_HEADS:
        return w
    nh, real = P_HEADS[name]
    w = jnp.reshape(w, w.shape[:-2] + (nh, real, w.shape[-1]))
    w = jnp.pad(w, [(0, 0)] * (w.ndim - 2) + [(0, HP - real), (0, 0)])
    return jnp.reshape(w, w.shape[:-3] + (nh * HP, w.shape[-1]))


def _unpad_heads(w, name):
    if name not in P_HEADS:
        return w
    nh, real = P_HEADS[name]
    w = jnp.reshape(w, w.shape[:-2] + (nh, HP, w.shape[-1]))[..., :real, :]
    return jnp.reshape(w, w.shape[:-3] + (nh * real, w.shape[-1]))


def _win_pad(win_t):
    segs, o = {}, 0
    for n, s in zip(IN_NAMES, IN_SIZES):
        segs[n] = win_t[..., o:o + s, :]
        o += s
    return jnp.concatenate([_pad_heads(segs[n], n) for n in P_ORDER], axis=-2)


def _win_unpad(win_p):
    segs = {n: _unpad_heads(win_p[..., P_OFF[n]:P_OFF[n] + P_WIDTH[n], :], n) for n in P_ORDER}
    return jnp.concatenate([segs[n] for n in IN_NAMES], axis=-2)


def _t(w):
    return jnp.swapaxes(w, -1, -2)


def _local_step(x, target, wl):
    seq = x.shape[0]
    lp = seq + NULL + NMETA
    tabs = _rope_tables(lp)
    h = jnp.concatenate([jnp.zeros((NULL, D), F32), wl["meta"], x], axis=0)
    saved = []
    for l in range(DEPTH):
        h, s0 = _ffn_fwd(h, wl["gains"][l, 0], wl["wg_t"], wl["wu_t"], wl["wd"], (l, 0))
        h, s1 = _mixer_fwd(h, wl["gains"][l, 1], wl, l, tabs)
        h, s2 = _ffn_fwd(h, wl["gains"][l, 2], wl["wg_t"], wl["wu_t"], wl["wd"], (l, 1))
        saved.append((s0, s1, s2))
    loss, dh, dhb, d_final = _loss_head(h, target, wl["final"])
    grads = []
    for l in reversed(range(DEPTH)):
        s0, s1, s2 = saved[l]
        dh, dhb, dg2, dwg1, dwu1, dwd1 = _ffn_bwd(dh, dhb, s2, wl["gains"][l, 2], wl["wg_t"], wl["wu_t"], wl["wd"], (l, 1))
        dh, dhb, gm = _mixer_bwd(dh, dhb, s1, wl["gains"][l, 1], wl, l, tabs)
        dh, dhb, dg0, dwg0, dwu0, dwd0 = _ffn_bwd(dh, dhb, s0, wl["gains"][l, 0], wl["wg_t"], wl["wu_t"], wl["wd"], (l, 0))
        gm.update(gain0=dg0, gain2=dg2, wg0=dwg0, wu0=dwu0, wd0=dwd0, wg1=dwg1, wu1=dwu1, wd1=dwd1)
        grads.append(gm)
    grads = grads[::-1]
    return loss, dh, d_final, grads


def kernel(x, meta_tokens, norm_gains, ffn_w_gate, ffn_w_up, ffn_w_down, w_in, gla_w2, gla_b2, gla_gn, q_norm, k_norm, w_pa, w_pb, b_merge, w_out, final_norm, loss_target, m_meta_tokens, m_norm_gains, m_ffn_w_gate, m_ffn_w_up, m_ffn_w_down, m_w_in, m_gla_w2, m_gla_b2, m_gla_gn, m_q_norm, m_k_norm, m_w_pa, m_w_pb, m_b_merge, m_w_out, m_final_norm, v_meta_tokens, v_norm_gains, v_ffn_w_gate, v_ffn_w_up, v_ffn_w_down, v_w_in, v_gla_w2, v_gla_b2, v_gla_gn, v_q_norm, v_k_norm, v_w_pa, v_w_pb, v_b_merge, v_w_out, v_final_norm):
    dev = _dev_index(lax.axis_index("x"), lax.axis_index("y"), lax.axis_index("c"))
    sh_g = _t(ffn_w_gate).astype(BF16)
    sh_u = _t(ffn_w_up).astype(BF16)
    sh_d = ffn_w_down.astype(BF16)
    sh_in = _t(w_in).astype(BF16)
    sh_pa = _t(w_pa).astype(BF16)
    sh_pb = _t(w_pb).astype(BF16)
    sh_out = w_out.astype(BF16)
    small = jnp.concatenate([jnp.reshape(t, (-1, 128)) for t in
                             (meta_tokens, norm_gains, gla_w2, gla_b2, b_merge)], axis=0)
    small = jnp.pad(small, ((0, 2), (0, 0)))
    g_g, g_u, g_d, g_in, g_pa, g_pb, g_out, g_small = _all_gather(
        "gather_weights", [sh_g, sh_u, sh_d, sh_in, sh_pa, sh_pb, sh_out, small], [2, 2, 2, 1, 1, 1, 1, 0])
    meta_full = jnp.reshape(jnp.transpose(g_small[:, 0:16], (1, 0, 2)), (NMETA, D))
    gains_full = jnp.reshape(jnp.transpose(jnp.reshape(g_small[:, 16:28], (NDEV, DEPTH, 3, 128)), (1, 2, 0, 3)), (DEPTH, 3, 1, D))
    w2_full = jnp.reshape(jnp.transpose(jnp.reshape(g_small[:, 28:60], (NDEV, DEPTH, 2, GLA_RANK, 32)), (1, 2, 3, 0, 4)),
                          (DEPTH, 2, GLA_RANK, 256))
    b2_full = jnp.reshape(jnp.transpose(jnp.reshape(g_small[:, 60:62], (NDEV, DEPTH, 2, 32)), (1, 2, 0, 3)), (DEPTH, 2, 256))
    bm_full = jnp.reshape(jnp.transpose(jnp.reshape(g_small[:, 62:70], (NDEV, DEPTH, 2, 128)), (1, 2, 0, 3)), (DEPTH, 2, 1, D))
    w2p = jnp.pad(jnp.reshape(w2_full, (DEPTH, 2, GLA_RANK, GLA_H, GLA_DK)),
                  ((0, 0), (0, 0), (0, HP - GLA_RANK), (0, 0), (0, HP - GLA_DK)))
    w2p = jnp.reshape(w2p, (DEPTH, 2, HP, GLA_H * HP))
    b2p = jnp.reshape(jnp.pad(jnp.reshape(b2_full, (DEPTH, 2, 1, GLA_H, GLA_DK)),
                              ((0, 0), (0, 0), (0, 0), (0, 0), (0, HP - GLA_DK))), (DEPTH, 2, 1, GLA_H * HP))
    wpb_t = jnp.reshape(g_pb, (DEPTH, D, ATT_H, HEAD_DIM))
    wpb_t = jnp.reshape(jnp.pad(wpb_t, ((0, 0), (0, 0), (0, 0), (0, HP - HEAD_DIM))), (DEPTH, D, ATT_H * HP))
    wl = dict(
        meta=meta_full, gains=gains_full, final=jnp.reshape(final_norm, (1, D)),
        wg_t=jnp.reshape(g_g, (DEPTH, 2, DFF, D)), wu_t=jnp.reshape(g_u, (DEPTH, 2, DFF, D)),
        wd=jnp.reshape(g_d, (DEPTH, 2, DFF, D)),
        win_t=_win_pad(jnp.reshape(g_in, (DEPTH, D_IN, D))),
        wpa_t=jnp.reshape(g_pa, (DEPTH, D, 512)), wpb_t=wpb_t, wout=jnp.reshape(g_out, (DEPTH, D, D)),
        w2p=w2p, b2p=b2p, bm=bm_full,
        gn=jnp.reshape(gla_gn, (DEPTH, 1, 512)),
        gq=jnp.pad(jnp.reshape(q_norm, (DEPTH, 1, HEAD_DIM)), ((0, 0), (0, 0), (0, HP - HEAD_DIM))),
        gk=jnp.pad(jnp.reshape(k_norm, (DEPTH, 1, HEAD_DIM)), ((0, 0), (0, 0), (0, HP - HEAD_DIM))),
    )
    loss, dh, d_final, grads = _local_step(x[0], loss_target[0], wl)
    loss = lax.psum(loss[0, 0], ("x", "y", "c"))
    grad_x = dh[NULL + NMETA:][None]

    red = []
    for l in range(DEPTH):
        g = grads[l]
        d_in = _win_unpad(g["win_t"])
        d_pb = jnp.reshape(jnp.reshape(g["wpb_t"], (D, ATT_H, HP))[:, :, :HEAD_DIM], (D, 512))
        ts = [g["wg0"], g["wu0"], g["wd0"], g["wg1"], g["wu1"], g["wd1"], d_in, g["wpa_t"], d_pb, g["wout"]]
        red.append(_reduce_scatter(str(l), [jnp.reshape(t, (NDEV, t.shape[0] // NDEV, t.shape[1])) for t in ts]))
    g_gate = jnp.stack([jnp.stack([_t(red[l][0]), _t(red[l][3])]) for l in range(DEPTH)])
    g_up = jnp.stack([jnp.stack([_t(red[l][1]), _t(red[l][4])]) for l in range(DEPTH)])
    g_down = jnp.stack([jnp.stack([red[l][2], red[l][5]]) for l in range(DEPTH)])
    g_win = jnp.stack([_t(red[l][6]) for l in range(DEPTH)])
    g_wpa = jnp.stack([_t(red[l][7]) for l in range(DEPTH)])
    g_wpb = jnp.stack([_t(red[l][8]) for l in range(DEPTH)])
    g_wout = jnp.stack([red[l][9] for l in range(DEPTH)])

    d_meta = dh[NULL:NULL + NMETA]
    d_gains = jnp.stack([jnp.concatenate([grads[l]["gain0"], grads[l]["gain"], grads[l]["gain2"]], axis=0)
                         for l in range(DEPTH)])
    d_w2 = jnp.stack([jnp.reshape(jnp.reshape(grads[l]["w2p"], (2, HP, GLA_H, HP))[:, :GLA_RANK, :, :GLA_DK],
                                  (2, GLA_RANK, 256)) for l in range(DEPTH)])
    d_b2 = jnp.stack([jnp.reshape(jnp.reshape(grads[l]["b2p"], (2, GLA_H, HP))[:, :, :GLA_DK], (2, 256))
                      for l in range(DEPTH)])
    d_gn = jnp.stack([grads[l]["gn"][0] for l in range(DEPTH)])
    d_gq = jnp.stack([grads[l]["gq"][0, :HEAD_DIM] for l in range(DEPTH)])
    d_gk = jnp.stack([grads[l]["gk"][0, :HEAD_DIM] for l in range(DEPTH)])
    d_bm = jnp.stack([jnp.concatenate([grads[l]["bma"], grads[l]["bmb"]], axis=0) for l in range(DEPTH)])
    parts = [d_meta, d_gains, d_w2, d_b2, d_gn, d_gq, d_gk, d_bm, d_final[0]]
    sizes = [p.size for p in parts]
    flat = jnp.concatenate([jnp.reshape(p, (-1,)) for p in parts])
    flat = jnp.reshape(flat, (-1, 128))
    nrow = flat.shape[0]
    flat = jnp.pad(flat, ((0, (-nrow) % 8), (0, 0)))
    (g_flat,) = _all_gather("gather_small_grads", [flat], [0])
    tot = jnp.reshape(_sum_gathered(g_flat), (-1,))
    full, o = [], 0
    for p, s in zip(parts, sizes):
        full.append(jnp.reshape(tot[o:o + s], p.shape))
        o += s
    f_meta, f_gains, f_w2, f_b2, f_gn, f_gq, f_gk, f_bm, f_final = full

    def mine(t, width):
        return lax.dynamic_slice_in_dim(t, dev * width, width, axis=t.ndim - 1)

    g_small = dict(meta_tokens=mine(f_meta, 128), norm_gains=mine(f_gains, 128), gla_w2=mine(f_w2, 32),
                   gla_b2=mine(f_b2, 32), gla_gn=f_gn, q_norm=f_gq, k_norm=f_gk, b_merge=mine(f_bm, 128),
                   final_norm=f_final)
    gr = dict(g_small, ffn_w_gate=g_gate, ffn_w_up=g_up, ffn_w_down=g_down, w_in=g_win, w_pa=g_wpa, w_pb=g_wpb,
              w_out=g_wout)
    ws = dict(meta_tokens=meta_tokens, norm_gains=norm_gains, ffn_w_gate=ffn_w_gate, ffn_w_up=ffn_w_up,
              ffn_w_down=ffn_w_down, w_in=w_in, gla_w2=gla_w2, gla_b2=gla_b2, gla_gn=gla_gn, q_norm=q_norm,
              k_norm=k_norm, w_pa=w_pa, w_pb=w_pb, b_merge=b_merge, w_out=w_out, final_norm=final_norm)
    ms = dict(meta_tokens=m_meta_tokens, norm_gains=m_norm_gains, ffn_w_gate=m_ffn_w_gate, ffn_w_up=m_ffn_w_up,
              ffn_w_down=m_ffn_w_down, w_in=m_w_in, gla_w2=m_gla_w2, gla_b2=m_gla_b2, gla_gn=m_gla_gn, q_norm=m_q_norm,
              k_norm=m_k_norm, w_pa=m_w_pa, w_pb=m_w_pb, b_merge=m_b_merge, w_out=m_w_out, final_norm=m_final_norm)
    vs = dict(meta_tokens=v_meta_tokens, norm_gains=v_norm_gains, ffn_w_gate=v_ffn_w_gate, ffn_w_up=v_ffn_w_up,
              ffn_w_down=v_ffn_w_down, w_in=v_w_in, gla_w2=v_gla_w2, gla_b2=v_gla_b2, gla_gn=v_gla_gn, q_norm=v_q_norm,
              k_norm=v_k_norm, w_pa=v_w_pa, w_pb=v_w_pb, b_merge=v_b_merge, w_out=v_w_out, final_norm=v_final_norm)
    names = ["meta_tokens", "norm_gains", "ffn_w_gate", "ffn_w_up", "ffn_w_down", "w_in", "gla_w2", "gla_b2", "gla_gn",
             "q_norm", "k_norm", "w_pa", "w_pb", "b_merge", "w_out", "final_norm"]
    deltas, new_m, new_v = [], [], []
    for n in names:
        dlt, nm, nv = _adamw(ws[n], gr[n], ms[n], vs[n])
        deltas.append(dlt)
        new_m.append(nm)
        new_v.append(nv)
    return (loss, grad_x, *[gr[n] for n in names], *deltas, *new_m, *new_v)
```

```python
import functools
import math

import jax
import jax.numpy as jnp
from jax import lax
from jax.experimental import pallas as pl
from jax.experimental.pallas import tpu as pltpu

F32 = jnp.float32
BF16 = jnp.bfloat16
MESH = pl.DeviceIdType.MESH
ANY = pl.BlockSpec(memory_space=pl.ANY)

NDEV = 8
D = 1024
DFF = 2816
DEPTH = 4
NMETA = 16
NULL = 112
GRID_W = 64
EPS = 1e-6
HP = 128
GLA_H = 4
GLA_DK = 64
GLA_RANK = 16
GLA_TAU = 16.0
CHUNK = 64
ATT_H = 8
ATT_KV = 2
ATT_G = ATT_H // ATT_KV
HEAD_DIM = 64
ROPE_THETA = 10000.0

IN_SIZES = (256, 256, 512, 512, 16, 16, 512, 128, 128, 1024, 1024)
IN_NAMES = ("qa", "ka", "va", "ra", "lrf", "lrb", "qb", "kb", "vb", "ga", "gb")
D_IN = sum(IN_SIZES)
P_ORDER = ("qb", "ga", "gb", "qa", "ka", "va", "ra", "kb", "vb", "lrf", "lrb")
P_WIDTH = dict(qb=1024, ga=1024, gb=1024, qa=512, ka=512, va=512, ra=512, kb=256, vb=256, lrf=128, lrb=128)
P_OFF = {}
_o = 0
for _n in P_ORDER:
    P_OFF[_n] = _o
    _o += P_WIDTH[_n]
D_INP = _o
P_HEADS = dict(qa=(4, 64), ka=(4, 64), qb=(8, 64), kb=(2, 64), vb=(2, 64), lrf=(1, 16), lrb=(1, 16))

ADAM_LR = 0.001
ADAM_B1 = 0.9
ADAM_B2 = 0.999
ADAM_EPS = 1e-08
ADAM_WD = 0.01
ADAM_STEP = 10

VMEM_BIG = 56 * 1024 * 1024


def _cparams(vmem=None):
    return pltpu.CompilerParams(vmem_limit_bytes=vmem) if vmem else pltpu.CompilerParams()


def _pick(n, prefs):
    for p in prefs:
        if n % p == 0:
            return p
    return n


def _tm(lp):
    return _pick(lp, (528, 512, 256, 128))


_DN = {"nn": (((1,), (0,)), ((), ())), "nt": (((1,), (1,)), ((), ())), "tn": (((0,), (0,)), ((), ()))}


def _dot(a, b, mode="nn", precision=None):
    return lax.dot_general(a, b, _DN[mode], preferred_element_type=F32, precision=precision)


def _sigmoid(x):
    return 1.0 / (1.0 + jnp.exp(-x))


def _mm(name, m, n, terms, outs, epilogue, extras=(), *, tm, tn, nk=1, i_outer=False, vmem=None):
    gm, gn = m // tm, n // tn
    assert gm * tm == m and gn * tn == n, (name, m, n, tm, tn)
    n_acc = 1 + max(t[3] for t in terms)

    def gmap(f):
        if i_outer:
            return lambda i, j, kk: f(i, j, kk)
        return lambda j, i, kk: f(i, j, kk)

    in_specs, args = [], []
    for a, b, mode, _, pa, pb in terms:
        kdim = a.shape[-2] if mode == "tn" else a.shape[-1]
        tk = kdim // nk
        assert tk * nk == kdim
        na, nb = (None,) * len(pa), (None,) * len(pb)
        if mode == "tn":
            in_specs.append(pl.BlockSpec(na + (tk, tm), gmap(lambda i, j, kk, pa=pa: pa + (kk, i))))
        else:
            in_specs.append(pl.BlockSpec(na + (tm, tk), gmap(lambda i, j, kk, pa=pa: pa + (i, kk))))
        if mode == "nt":
            in_specs.append(pl.BlockSpec(nb + (tn, tk), gmap(lambda i, j, kk, pb=pb: pb + (j, kk))))
        else:
            in_specs.append(pl.BlockSpec(nb + (tk, tn), gmap(lambda i, j, kk, pb=pb: pb + (kk, j))))
        args += [a, b]
    for arr, kind, off, pe in extras:
        ne = (None,) * len(pe)
        if kind == "mn":
            in_specs.append(pl.BlockSpec(ne + (tm, tn), gmap(lambda i, j, kk, off=off, pe=pe: pe + (i, j + off))))
        else:
            in_specs.append(pl.BlockSpec(ne + (1, tn), gmap(lambda i, j, kk, off=off, pe=pe: pe + (0, j + off))))
        args.append(arr)
    out_shape, out_specs = [], []
    for shape, dtype, kind, off, po in outs:
        no = (None,) * len(po)
        out_shape.append(jax.ShapeDtypeStruct(shape, dtype))
        if kind == "mn":
            out_specs.append(pl.BlockSpec(no + (tm, tn), gmap(lambda i, j, kk, off=off, po=po: po + (i, j + off))))
        else:
            assert not i_outer
            out_specs.append(pl.BlockSpec(no + (1, tn), gmap(lambda i, j, kk, off=off, po=po: po + (0, j + off))))
    n_t, n_e, n_o = len(terms), len(extras), len(outs)
    i_axis = 0 if i_outer else 1

    def body(*refs):
        ins = refs[: 2 * n_t]
        exs = refs[2 * n_t: 2 * n_t + n_e]
        ors = refs[2 * n_t + n_e: 2 * n_t + n_e + n_o]
        accs = refs[2 * n_t + n_e + n_o:]
        i = pl.program_id(i_axis)
        kk = pl.program_id(2)
        part = [None] * n_acc
        for t, (_, _, mode, ai, _, _) in enumerate(terms):
            p = _dot(ins[2 * t][...], ins[2 * t + 1][...], mode)
            part[ai] = p if part[ai] is None else part[ai] + p

        def finish(vals):
            res = epilogue(vals, [e[...] for e in exs], i * tm)
            for (_, dtype, kind, _, _), o_ref, v in zip(outs, ors, res):
                if kind == "mn":
                    o_ref[...] = v.astype(dtype)
                else:
                    @pl.when(i == 0)
                    def _():
                        o_ref[...] = v.astype(dtype)

                    @pl.when(i != 0)
                    def _():
                        o_ref[...] += v.astype(dtype)

        if nk == 1:
            finish(part)
        else:
            @pl.when(kk == 0)
            def _():
                for a_ref, p in zip(accs, part):
                    a_ref[...] = p

            @pl.when(kk != 0)
            def _():
                for a_ref, p in zip(accs, part):
                    a_ref[...] += p

            @pl.when(kk == nk - 1)
            def _():
                finish([a_ref[...] for a_ref in accs])

    scratch = [pltpu.VMEM((tm, tn), F32) for _ in range(n_acc)] if nk > 1 else []
    grid = (gm, gn, nk) if i_outer else (gn, gm, nk)
    res = pl.pallas_call(
        body, name=name, grid=grid, in_specs=in_specs, out_specs=out_specs, out_shape=out_shape,
        scratch_shapes=scratch, compiler_params=_cparams(vmem),
    )(*args)
    return res


def _term(a, b, mode, acc=0, pa=(), pb=()):
    return (a, b, mode, acc, tuple(pa), tuple(pb))


def _row_tile(lp):
    return _pick(lp, (384, 256, 128))


def _rmsnorm_fwd(h, gain):
    lp = h.shape[0]
    tr = _row_tile(lp)

    def body(h_ref, g_ref, z_ref):
        x = h_ref[...]
        r = lax.rsqrt(jnp.mean(x * x, axis=-1, keepdims=True) + EPS)
        z_ref[...] = (x * r * g_ref[...]).astype(BF16)

    return pl.pallas_call(
        body, name="rmsnorm_fwd", grid=(lp // tr,),
        in_specs=[pl.BlockSpec((tr, D), lambda i: (i, 0)), pl.BlockSpec((1, D), lambda i: (0, 0))],
        out_specs=pl.BlockSpec((tr, D), lambda i: (i, 0)),
        out_shape=jax.ShapeDtypeStruct((lp, D), BF16),
    )(h, gain)


def _rmsnorm_bwd(h, dz, gain, dh_res):
    lp = h.shape[0]
    tr = _row_tile(lp)

    def body(h_ref, dz_ref, g_ref, res_ref, dh_ref, dhb_ref, dg_ref):
        i = pl.program_id(0)
        x = h_ref[...]
        r = lax.rsqrt(jnp.mean(x * x, axis=-1, keepdims=True) + EPS)
        xh = x * r
        dz_v = dz_ref[...]
        dxh = dz_v * g_ref[...]
        dx = r * (dxh - xh * jnp.mean(dxh * xh, axis=-1, keepdims=True))
        rows = i * tr + lax.broadcasted_iota(jnp.int32, (tr, 1), 0)
        dh = jnp.where(rows >= NULL, res_ref[...] + dx, 0.0)
        dh_ref[...] = dh
        dhb_ref[...] = dh.astype(BF16)
        part = jnp.sum(dz_v * xh, axis=0, keepdims=True)

        @pl.when(i == 0)
        def _():
            dg_ref[...] = part

        @pl.when(i != 0)
        def _():
            dg_ref[...] += part

    row = pl.BlockSpec((tr, D), lambda i: (i, 0))
    vec = pl.BlockSpec((1, D), lambda i: (0, 0))
    return pl.pallas_call(
        body, name="rmsnorm_bwd", grid=(lp // tr,),
        in_specs=[row, row, vec, row], out_specs=[row, row, vec],
        out_shape=[jax.ShapeDtypeStruct((lp, D), F32), jax.ShapeDtypeStruct((lp, D), BF16),
                   jax.ShapeDtypeStruct((1, D), F32)],
    )(h, dz, gain, dh_res)


def _loss_head(h, target, gain):
    lp = h.shape[0]
    tr = 128

    def body(h_ref, t_ref, g_ref, loss_ref, dh_ref, dhb_ref, dg_ref):
        i = pl.program_id(0)

        @pl.when(i == 0)
        def _():
            loss_ref[...] = jnp.zeros_like(loss_ref)
            dg_ref[...] = jnp.zeros_like(dg_ref)
            dh_ref[...] = jnp.zeros_like(dh_ref)
            dhb_ref[...] = jnp.zeros_like(dhb_ref)

        @pl.when(i != 0)
        def _():
            x = h_ref[...]
            g = g_ref[...]
            r = lax.rsqrt(jnp.mean(x * x, axis=-1, keepdims=True) + EPS)
            xh = x * r
            y = xh * g
            err = y - t_ref[...]
            loss_ref[...] += 0.5 * jnp.sum(jnp.sum(err * err, axis=-1, keepdims=True), axis=0, keepdims=True) / D
            dy = err * (1.0 / D)
            dg_ref[...] += jnp.sum(dy * xh, axis=0, keepdims=True)
            dxh = dy * g
            dx = r * (dxh - xh * jnp.mean(dxh * xh, axis=-1, keepdims=True))
            dh_ref[...] = dx
            dhb_ref[...] = dx.astype(BF16)

    row = pl.BlockSpec((tr, D), lambda i: (i, 0))
    vec = pl.BlockSpec((1, D), lambda i: (0, 0))
    return pl.pallas_call(
        body, name="loss_head", grid=(lp // tr,),
        in_specs=[row, pl.BlockSpec((tr, D), lambda i: (jnp.maximum(i - 1, 0), 0)), vec],
        out_specs=[pl.BlockSpec((1, 1), lambda i: (0, 0)), row, row, vec],
        out_shape=[jax.ShapeDtypeStruct((1, 1), F32), jax.ShapeDtypeStruct((lp, D), F32),
                   jax.ShapeDtypeStruct((lp, D), BF16), jax.ShapeDtypeStruct((1, D), F32)],
    )(h, target, gain)


def _silu_parts(g):
    s = _sigmoid(g)
    return g * s, s * (1.0 + g * (1.0 - s))


def _ffn_fwd(h, gain, wg_t, wu_t, wd, pre):
    lp = h.shape[0]
    tm = _tm(lp)
    z = _rmsnorm_fwd(h, gain)

    def up_epi(accs, exs, row0):
        g, u = accs
        sg, _ = _silu_parts(g)
        return [g, u, sg * u]

    tn = _pick(DFF, (1408, 256, 128))
    bshape = (lp, DFF)
    g_, u_, act = _mm("ffn_up", lp, DFF, [_term(z, wg_t, "nt", 0, (), pre), _term(z, wu_t, "nt", 1, (), pre)],
                      [(bshape, BF16, "mn", 0, ())] * 3, up_epi, tm=tm, tn=tn, vmem=VMEM_BIG)

    def down_epi(accs, exs, row0):
        return [exs[0] + 0.5 * accs[0]]

    nk = _pick(DFF // 128, (2, 1))
    (h2,) = _mm("ffn_down", lp, D, [_term(act, wd, "nn", 0, (), pre)], [((lp, D), F32, "mn", 0, ())], down_epi,
                extras=[(h, "mn", 0, ())], tm=tm, tn=D, nk=nk, i_outer=True, vmem=VMEM_BIG)
    return h2, dict(h=h, z=z, g=g_, u=u_, act=act)


def _dw(name, a, b, m, n, scale=1.0):
    lp = a.shape[0]
    nk = lp // _pick(lp, (384, 256, 128))
    tm = _pick(m, (2944, 1408, 1024, 512, 256, 128))
    tn = _pick(n, (1024, 512, 256, 128))

    def epi(accs, exs, row0):
        return [accs[0] * scale]

    (w,) = _mm(name, m, n, [_term(a, b, "tn")], [((m, n), BF16, "mn", 0, ())], epi, tm=tm, tn=tn, nk=nk,
               i_outer=True, vmem=VMEM_BIG)
    return w


def _ffn_bwd(dh, dhb, sv, gain, wg_t, wu_t, wd, pre):
    lp = dh.shape[0]
    tm = _tm(lp)

    def dact_epi(accs, exs, row0):
        g = exs[0].astype(F32)
        u = exs[1].astype(F32)
        da = 0.5 * accs[0]
        sg, dsg = _silu_parts(g)
        return [da * u * dsg, da * sg]

    tn = _pick(DFF, (1408, 256, 128))
    dg_, du_ = _mm("ffn_dact", lp, DFF, [_term(dhb, wd, "nt", 0, (), pre)],
                   [((lp, DFF), BF16, "mn", 0, ())] * 2, dact_epi,
                   extras=[(sv["g"], "mn", 0, ()), (sv["u"], "mn", 0, ())], tm=tm, tn=tn, vmem=VMEM_BIG)
    d_wd = _dw("dw_down", sv["act"], dhb, DFF, D, 0.5)
    d_wg = _dw("dw_gate", dg_, sv["z"], DFF, D)
    d_wu = _dw("dw_up", du_, sv["z"], DFF, D)

    def id_epi(accs, exs, row0):
        return [accs[0]]

    nk = _pick(DFF // 128, (2, 1))
    (dz,) = _mm("ffn_dz", lp, D, [_term(dg_, wg_t, "nn", 0, (), pre), _term(du_, wu_t, "nn", 0, (), pre)],
                [((lp, D), F32, "mn", 0, ())], id_epi, tm=tm, tn=D, nk=nk, i_outer=True, vmem=VMEM_BIG)
    dh2, dhb2, dgain = _rmsnorm_bwd(sv["h"], dz, gain, dh)
    return dh2, dhb2, dgain, d_wg, d_wu, d_wd


def _hin_spec(tr, name, width=None, extra=0):
    w = P_WIDTH[name] if width is None else width
    base = P_OFF[name] // w
    return w, base


def _gla_gates(hin, w2p, b2p):
    lp = hin.shape[0]
    tr = _row_tile(lp)
    bf, bb = P_OFF["lrf"] // HP, P_OFF["lrb"] // HP

    def body(lf_ref, lb_ref, w_ref, b_ref, o_ref):
        i = pl.program_id(0)
        rows = i * tr + lax.broadcasted_iota(jnp.int32, (tr, 1), 0)
        for d, l_ref in enumerate((lf_ref, lb_ref)):
            logit = _dot(l_ref[...], w_ref[d], precision=lax.Precision.HIGHEST) + b_ref[d]
            g = jax.nn.log_sigmoid(logit) * (1.0 / GLA_TAU)
            o_ref[d] = jnp.where(rows >= NULL, g, 0.0)

    return pl.pallas_call(
        body, name="gla_gates", grid=(lp // tr,),
        in_specs=[pl.BlockSpec((tr, HP), lambda i: (i, bf)), pl.BlockSpec((tr, HP), lambda i: (i, bb)),
                  pl.BlockSpec((2, HP, 512), lambda i: (0, 0, 0)), pl.BlockSpec((2, 1, 512), lambda i: (0, 0, 0))],
        out_specs=pl.BlockSpec((2, tr, 512), lambda i: (0, i, 0)),
        out_shape=jax.ShapeDtypeStruct((2, lp, 512), F32),
    )(hin, hin, w2p, b2p)


def _gla_rows(lp):
    return _pick(lp, (384, 256, 128))


def _tri(d):
    r = lax.broadcasted_iota(jnp.int32, (CHUNK, CHUNK), 0)
    c = lax.broadcasted_iota(jnp.int32, (CHUNK, CHUNK), 1)
    return (r >= c) if d == 0 else (r <= c)


def _gla_fwd(hin, gates):
    lp = hin.shape[0]
    rb = _gla_rows(lp)
    nb = lp // rb
    cpb = rb // CHUNK
    nchunk = lp // CHUNK
    qo, ko, vo = P_OFF["qa"] // 512, P_OFF["ka"] // 512, P_OFF["va"] // 512
    scale = GLA_DK ** -0.5

    def body(qf, kf, vf, gf, qb, kb, vb_, gb, of, ob, sf, sb, st):
        @pl.when(pl.program_id(0) == 0)
        def _():
            st[...] = jnp.zeros_like(st)

        ins = ((qf, kf, vf, gf, of, sf), (qb, kb, vb_, gb, ob, sb))
        for ci in range(cpb):
            for d in range(2):
                q_ref, k_ref, v_ref, g_ref, o_ref, s_ref = ins[d]
                tri = _tri(d)
                tmat = tri.astype(F32)
                c = ci if d == 0 else cpb - 1 - ci
                rows = slice(c * CHUNK, (c + 1) * CHUNK)
                for h in range(GLA_H):
                    sl = slice(h * HP, (h + 1) * HP)
                    q = q_ref[rows, sl] * scale
                    k = k_ref[rows, sl]
                    v = v_ref[rows, sl]
                    g = g_ref[rows, sl]
                    b = _dot(tmat, g, precision=lax.Precision.HIGHEST)
                    btot = jnp.sum(g, axis=0, keepdims=True)
                    qd = (q * jnp.exp(b)).astype(BF16)
                    ki = (k * jnp.exp(-b)).astype(BF16)
                    ke = (k * jnp.exp(btot - b)).astype(BF16)
                    vb = v.astype(BF16)
                    att = jnp.where(tri, _dot(qd, ki, "nt"), 0.0)
                    s_prev = st[d, h]
                    o_ref[rows, sl] = _dot(att.astype(BF16), vb) + _dot(qd, s_prev.astype(BF16), "nt")
                    s_ref[h, c] = s_prev
                    st[d, h] = s_prev * jnp.exp(btot) + _dot(vb, ke, "tn")

    def specs(off):
        return (pl.BlockSpec((rb, 512), lambda b: (b, off)), pl.BlockSpec((rb, 512), lambda b: (nb - 1 - b, off)))

    (qf, qb), (kf, kb), (vf, vb2) = specs(qo), specs(ko), specs(vo)
    gf = pl.BlockSpec((None, rb, 512), lambda b: (0, b, 0))
    gb = pl.BlockSpec((None, rb, 512), lambda b: (1, nb - 1 - b, 0))
    of, ob = specs(0)
    sf = pl.BlockSpec((GLA_H, cpb, HP, HP), lambda b: (0, b, 0, 0))
    sb = pl.BlockSpec((GLA_H, cpb, HP, HP), lambda b: (0, nb - 1 - b, 0, 0))
    osh = jax.ShapeDtypeStruct((lp, GLA_H * HP), F32)
    ssh = jax.ShapeDtypeStruct((GLA_H, nchunk, HP, HP), F32)
    return pl.pallas_call(
        body, name="gla_fwd", grid=(nb,),
        in_specs=[qf, kf, vf, gf, qb, kb, vb2, gb], out_specs=[of, ob, sf, sb], out_shape=[osh, osh, ssh, ssh],
        scratch_shapes=[pltpu.VMEM((2, GLA_H, HP, HP), F32)], compiler_params=_cparams(VMEM_BIG),
    )(hin, hin, hin, gates, hin, hin, hin, gates)


def _gla_bwd(hin, gates, states, do):
    lp = hin.shape[0]
    rb = _gla_rows(lp)
    nb = lp // rb
    cpb = rb // CHUNK
    qo, ko, vo = P_OFF["qa"] // 512, P_OFF["ka"] // 512, P_OFF["va"] // 512
    scale = GLA_DK ** -0.5

    def body(qf, kf, vf, gf, sf, dof, qb, kb, vb_, gb, sb, dob,
             dqf, dkf, dvf, dgf, dqb, dkb, dvb, dgb, dst):
        @pl.when(pl.program_id(0) == 0)
        def _():
            dst[...] = jnp.zeros_like(dst)

        ins = ((qf, kf, vf, gf, sf, dof, dqf, dkf, dvf, dgf), (qb, kb, vb_, gb, sb, dob, dqb, dkb, dvb, dgb))
        for ci in range(cpb):
            for d in range(2):
                q_ref, k_ref, v_ref, g_ref, s_ref, do_ref, dq_ref, dk_ref, dv_ref, dg_ref = ins[d]
                tri = _tri(d)
                tmat = tri.astype(F32)
                c = cpb - 1 - ci if d == 0 else ci
                rows = slice(c * CHUNK, (c + 1) * CHUNK)
                for h in range(GLA_H):
                    sl = slice(h * HP, (h + 1) * HP)
                    q = q_ref[rows, sl] * scale
                    k = k_ref[rows, sl]
                    v = v_ref[rows, sl]
                    g = g_ref[rows, sl]
                    dout = do_ref[rows, sl].astype(BF16)
                    b = _dot(tmat, g, precision=lax.Precision.HIGHEST)
                    btot = jnp.sum(g, axis=0, keepdims=True)
                    e = jnp.exp(b)
                    ei = jnp.exp(-b)
                    et = jnp.exp(btot - b)
                    etot = jnp.exp(btot)
                    qd = q * e
                    ki = k * ei
                    ke = k * et
                    qdb, kib, keb, vb = qd.astype(BF16), ki.astype(BF16), ke.astype(BF16), v.astype(BF16)
                    att = jnp.where(tri, _dot(qdb, kib, "nt"), 0.0).astype(BF16)
                    d_att = jnp.where(tri, _dot(dout, vb, "nt"), 0.0).astype(BF16)
                    s_prev = s_ref[h, c]
                    ds_t = dst[d, h]
                    ds_b = ds_t.astype(BF16)
                    dv = _dot(att, dout, "tn") + _dot(keb, ds_b, "nt")
                    d_qd = _dot(d_att, kib) + _dot(dout, s_prev.astype(BF16))
                    d_ki = _dot(d_att, qdb, "tn")
                    d_ke = _dot(vb, ds_b)
                    d_e = jnp.sum(s_prev * ds_t, axis=0, keepdims=True)
                    dst[d, h] = _dot(dout, qdb, "tn") + ds_t * etot
                    db = d_qd * qd - d_ki * ki - d_ke * ke
                    dbtot = jnp.sum(d_ke * ke, axis=0, keepdims=True) + d_e * etot
                    dq_ref[rows, sl] = d_qd * e * scale
                    dk_ref[rows, sl] = d_ki * ei + d_ke * et
                    dv_ref[rows, sl] = dv
                    dg_ref[rows, sl] = _dot(tmat, db, "tn", precision=lax.Precision.HIGHEST) + dbtot

    def fw(off):
        return pl.BlockSpec((rb, 512), lambda b: (nb - 1 - b, off))

    def bw(off):
        return pl.BlockSpec((rb, 512), lambda b: (b, off))

    gf = pl.BlockSpec((None, rb, 512), lambda b: (0, nb - 1 - b, 0))
    gb = pl.BlockSpec((None, rb, 512), lambda b: (1, b, 0))
    sf = pl.BlockSpec((GLA_H, cpb, HP, HP), lambda b: (0, nb - 1 - b, 0, 0))
    sb = pl.BlockSpec((GLA_H, cpb, HP, HP), lambda b: (0, b, 0, 0))
    osh = jax.ShapeDtypeStruct((lp, GLA_H * HP), F32)
    res = pl.pallas_call(
        body, name="gla_bwd", grid=(nb,),
        in_specs=[fw(qo), fw(ko), fw(vo), gf, sf, fw(0), bw(qo), bw(ko), bw(vo), gb, sb, bw(0)],
        out_specs=[fw(0)] * 4 + [bw(0)] * 4, out_shape=[osh] * 8,
        scratch_shapes=[pltpu.VMEM((2, GLA_H, HP, HP), F32)], compiler_params=_cparams(VMEM_BIG),
    )(hin, hin, hin, gates, states[0], do, hin, hin, hin, gates, states[1], do)
    return res[:4], res[4:]


def _gla_out_fwd(o2, hin, gn):
    lp = hin.shape[0]
    tr = _row_tile(lp)
    ro = P_OFF["ra"] // 512

    def body(of_ref, ob_ref, r_ref, gn_ref, a_ref):
        r = r_ref[...]
        sr, _ = _silu_parts(r)
        for h in range(GLA_H):
            sl = slice(h * HP, (h + 1) * HP)
            o = of_ref[:, sl] + ob_ref[:, sl]
            rs = lax.rsqrt(jnp.mean(o * o, axis=-1, keepdims=True) + EPS)
            a_ref[:, sl] = (o * rs * gn_ref[:, sl] * sr[:, sl]).astype(BF16)

    row = pl.BlockSpec((tr, 512), lambda i: (i, 0))
    return pl.pallas_call(
        body, name="gla_out_fwd", grid=(lp // tr,),
        in_specs=[row, row, pl.BlockSpec((tr, 512), lambda i: (i, ro)), pl.BlockSpec((1, 512), lambda i: (0, 0))],
        out_specs=row,
        out_shape=jax.ShapeDtypeStruct((lp, 512), BF16),
    )(o2[0], o2[1], hin, gn)


def _gla_out_bwd(da, o2, hin, gn):
    lp = hin.shape[0]
    tr = _row_tile(lp)
    ro = P_OFF["ra"] // 512

    def body(da_ref, of_ref, ob_ref, r_ref, gn_ref, do_ref, dr_ref, dgn_ref):
        i = pl.program_id(0)
        r = r_ref[...]
        sr, dsr = _silu_parts(r)
        da_v = da_ref[...]
        parts = []
        for h in range(GLA_H):
            sl = slice(h * HP, (h + 1) * HP)
            o = of_ref[:, sl] + ob_ref[:, sl]
            rs = lax.rsqrt(jnp.mean(o * o, axis=-1, keepdims=True) + EPS)
            oh = o * rs
            gn_h = gn_ref[:, sl]
            dah = da_v[:, sl]
            dr_ref[:, sl] = (dah * oh * gn_h * dsr[:, sl]).astype(BF16)
            t = dah * sr[:, sl]
            parts.append(jnp.sum(t * oh, axis=0, keepdims=True))
            doh = t * gn_h
            do_ref[:, sl] = rs * (doh - oh * jnp.mean(doh * oh, axis=-1, keepdims=True))
        part = jnp.concatenate(parts, axis=1)

        @pl.when(i == 0)
        def _():
            dgn_ref[...] = part

        @pl.when(i != 0)
        def _():
            dgn_ref[...] += part

    row = pl.BlockSpec((tr, 512), lambda i: (i, 0))
    return pl.pallas_call(
        body, name="gla_out_bwd", grid=(lp // tr,),
        in_specs=[row, row, row, pl.BlockSpec((tr, 512), lambda i: (i, ro)), pl.BlockSpec((1, 512), lambda i: (0, 0))],
        out_specs=[row, row, pl.BlockSpec((1, 512), lambda i: (0, 0))],
        out_shape=[jax.ShapeDtypeStruct((lp, 512), F32), jax.ShapeDtypeStruct((lp, 512), BF16),
                   jax.ShapeDtypeStruct((1, 512), F32)],
    )(da, o2[0], o2[1], hin, gn)


def _gla_in_bwd(gf, gb, gates, hin, w2p):
    lp = hin.shape[0]
    tr = _row_tile(lp)
    bf, bb = P_OFF["lrf"] // HP, P_OFF["lrb"] // HP

    def body(dqf_ref, dkf_ref, dvf_ref, dgf_ref, dqb_ref, dkb_ref, dvb_ref, dgb_ref, g_ref, lf_ref, lb_ref, w_ref,
             oq_ref, ok_ref, ov_ref, olr_ref, dw_ref, db_ref):
        i = pl.program_id(0)
        oq_ref[...] = (dqf_ref[...] + dqb_ref[...]).astype(BF16)
        ok_ref[...] = (dkf_ref[...] + dkb_ref[...]).astype(BF16)
        ov_ref[...] = (dvf_ref[...] + dvb_ref[...]).astype(BF16)
        rows = i * tr + lax.broadcasted_iota(jnp.int32, (tr, 1), 0)
        for d, (l_ref, dg_ref) in enumerate(((lf_ref, dgf_ref), (lb_ref, dgb_ref))):
            sig_neg = 1.0 - jnp.exp(GLA_TAU * g_ref[d])
            dlogit = jnp.where(rows >= NULL, dg_ref[...] * (1.0 / GLA_TAU) * sig_neg, 0.0)
            olr_ref[:, d * HP:(d + 1) * HP] = _dot(dlogit, w_ref[d], "nt", precision=lax.Precision.HIGHEST).astype(BF16)
            dw = _dot(l_ref[...], dlogit, "tn", precision=lax.Precision.HIGHEST)
            dbias = jnp.sum(dlogit, axis=0, keepdims=True)

            @pl.when(i == 0)
            def _():
                dw_ref[d] = dw
                db_ref[d] = dbias

            @pl.when(i != 0)
            def _():
                dw_ref[d] += dw
                db_ref[d] += dbias

    two = pl.BlockSpec((2, tr, 512), lambda i: (0, i, 0))
    row = pl.BlockSpec((tr, 512), lambda i: (i, 0))
    return pl.pallas_call(
        body, name="gla_in_bwd", grid=(lp // tr,),
        in_specs=[row] * 8 + [two, pl.BlockSpec((tr, HP), lambda i: (i, bf)),
                  pl.BlockSpec((tr, HP), lambda i: (i, bb)), pl.BlockSpec((2, HP, 512), lambda i: (0, 0, 0))],
        out_specs=[row, row, row, pl.BlockSpec((tr, 2 * HP), lambda i: (i, 0)),
                   pl.BlockSpec((2, HP, 512), lambda i: (0, 0, 0)), pl.BlockSpec((2, 1, 512), lambda i: (0, 0, 0))],
        out_shape=[jax.ShapeDtypeStruct((lp, 512), BF16)] * 3 + [
            jax.ShapeDtypeStruct((lp, 2 * HP), BF16), jax.ShapeDtypeStruct((2, HP, 512), F32),
            jax.ShapeDtypeStruct((2, 1, 512), F32)],
    )(*gf, *gb, gates, hin, hin, w2p)


def _rope_tables(lp):
    n_tok = lp - NULL - NMETA
    rows = n_tok // GRID_W
    row = jnp.repeat(jnp.arange(rows), GRID_W).astype(F32)
    col = jnp.tile(jnp.arange(GRID_W), rows).astype(F32)
    inv = ROPE_THETA ** (-jnp.arange(0, 32, 2, dtype=F32) / 32)
    ang = jnp.concatenate([row[:, None] * inv, col[:, None] * inv], axis=-1)
    ang = jnp.concatenate([jnp.zeros((NULL + NMETA, 32), F32), ang], axis=0)
    cos, sin = jnp.cos(ang), jnp.sin(ang)
    z16 = jnp.zeros((lp, 16), F32)
    z64 = jnp.zeros((lp, 64), F32)
    c = jnp.concatenate([cos[:, :16], cos[:, :16], cos[:, 16:], cos[:, 16:], z64], axis=1)
    a = jnp.concatenate([-sin[:, :16], z16, -sin[:, 16:], z16, z64], axis=1)
    b = jnp.concatenate([z16, sin[:, :16], z16, sin[:, 16:], z64], axis=1)
    return c, a, b


def _rope(x, c, a, b):
    return x * c + pltpu.roll(x, HP - 16, 1) * a + pltpu.roll(x, 16, 1) * b


def _rope_t(dx, c, a, b):
    return dx * c + pltpu.roll(dx * a, 16, 1) + pltpu.roll(dx * b, HP - 16, 1)


def _attn_prep(hin, gq, gk, tabs):
    lp = hin.shape[0]
    tr = _row_tile(lp)
    qo, ko, vo = P_OFF["qb"] // 1024, P_OFF["kb"] // 256, P_OFF["vb"] // 256

    def body(q_ref, k_ref, v_ref, gq_ref, gk_ref, c_ref, a_ref, b_ref, oq_ref, ok_ref, ov_ref):
        c, a, b = c_ref[...], a_ref[...], b_ref[...]
        for src, g_ref, dst, nh, sc in ((q_ref, gq_ref, oq_ref, ATT_H, Q_SCALE), (k_ref, gk_ref, ok_ref, ATT_KV, 1.0)):
            for h in range(nh):
                sl = slice(h * HP, (h + 1) * HP)
                x = src[:, sl]
                r = lax.rsqrt(jnp.sum(x * x, axis=-1, keepdims=True) * (1.0 / HEAD_DIM) + EPS)
                dst[:, sl] = (_rope(x * r * g_ref[...], c, a, b) * sc).astype(BF16)
        lane = lax.broadcasted_iota(jnp.int32, (1, ATT_KV * HP), 1)
        ov_ref[...] = jnp.where(lane % HP == HEAD_DIM, 1.0, v_ref[...]).astype(BF16)

    tab = pl.BlockSpec((tr, HP), lambda i: (i, 0))
    vec = pl.BlockSpec((1, HP), lambda i: (0, 0))
    return pl.pallas_call(
        body, name="attn_prep", grid=(lp // tr,),
        in_specs=[pl.BlockSpec((tr, 1024), lambda i: (i, qo)), pl.BlockSpec((tr, 256), lambda i: (i, ko)),
                  pl.BlockSpec((tr, 256), lambda i: (i, vo)), vec, vec, tab, tab, tab],
        out_specs=[pl.BlockSpec((tr, 1024), lambda i: (i, 0)), pl.BlockSpec((tr, 256), lambda i: (i, 0)),
                   pl.BlockSpec((tr, 256), lambda i: (i, 0))],
        out_shape=[jax.ShapeDtypeStruct((lp, 1024), BF16), jax.ShapeDtypeStruct((lp, 256), BF16),
                   jax.ShapeDtypeStruct((lp, 256), BF16)],
    )(hin, hin, hin, gq, gk, *tabs)


def _attn_prep_bwd(dqr, dkr, hin, gq, gk, tabs):
    lp = hin.shape[0]
    tr = _row_tile(lp)
    qo, ko = P_OFF["qb"] // 1024, P_OFF["kb"] // 256

    def body(dq_ref, dk_ref, q_ref, k_ref, gq_ref, gk_ref, c_ref, a_ref, b_ref, oq_ref, ok_ref, dgq_ref, dgk_ref):
        i = pl.program_id(0)
        c, a, b = c_ref[...], a_ref[...], b_ref[...]
        for src, dsrc, g_ref, dst, dg_ref, nh, sc in (
                (q_ref, dq_ref, gq_ref, oq_ref, dgq_ref, ATT_H, Q_SCALE),
                (k_ref, dk_ref, gk_ref, ok_ref, dgk_ref, ATT_KV, 1.0)):
            acc = jnp.zeros((1, HP), F32)
            for h in range(nh):
                sl = slice(h * HP, (h + 1) * HP)
                x = src[:, sl]
                r = lax.rsqrt(jnp.sum(x * x, axis=-1, keepdims=True) * (1.0 / HEAD_DIM) + EPS)
                xh = x * r
                dxn = _rope_t(dsrc[:, sl] * sc, c, a, b)
                acc = acc + jnp.sum(dxn * xh, axis=0, keepdims=True)
                dxh = dxn * g_ref[...]
                dx = r * (dxh - xh * (jnp.sum(dxh * xh, axis=-1, keepdims=True) * (1.0 / HEAD_DIM)))
                dst[:, sl] = dx.astype(BF16)

            @pl.when(i == 0)
            def _():
                dg_ref[...] = acc

            @pl.when(i != 0)
            def _():
                dg_ref[...] += acc

    tab = pl.BlockSpec((tr, HP), lambda i: (i, 0))
    vec = pl.BlockSpec((1, HP), lambda i: (0, 0))
    return pl.pallas_call(
        body, name="attn_prep_bwd", grid=(lp // tr,),
        in_specs=[pl.BlockSpec((tr, 1024), lambda i: (i, 0)), pl.BlockSpec((tr, 256), lambda i: (i, 0)),
                  pl.BlockSpec((tr, 1024), lambda i: (i, qo)), pl.BlockSpec((tr, 256), lambda i: (i, ko)),
                  vec, vec, tab, tab, tab],
        out_specs=[pl.BlockSpec((tr, 1024), lambda i: (i, 0)), pl.BlockSpec((tr, 256), lambda i: (i, 0)), vec, vec],
        out_shape=[jax.ShapeDtypeStruct((lp, 1024), BF16), jax.ShapeDtypeStruct((lp, 256), BF16),
                   jax.ShapeDtypeStruct((1, HP), F32), jax.ShapeDtypeStruct((1, HP), F32)],
    )(dqr, dkr, hin, hin, gq, gk, *tabs)


QB = 128
GH = 2
Q_SCALE = HEAD_DIM ** -0.5 * math.log2(math.e)
LN2 = math.log(2.0)


def _stack(ref, g0, n):
    return jnp.concatenate([ref[:, (g0 + g) * HP:(g0 + g + 1) * HP] for g in range(n)], axis=0)


def _attn_fwd(qr, kr, vb):
    lp = qr.shape[0]
    nq = lp // QB

    def body(q_ref, k_ref, v_ref, o_ref, lse_ref):
        qb = pl.program_id(1)
        keys = lax.broadcasted_iota(jnp.int32, (1, lp), 1)
        lane = lax.broadcasted_iota(jnp.int32, (1, HP), 1)
        rows = qb * QB + lax.broadcasted_iota(jnp.int32, (QB, 1), 0)
        for ch in range(ATT_G // GH):
            qs = _stack(q_ref, ch * GH, GH)
            s = _dot(qs, k_ref[...], "nt")
            s = jnp.where(keys >= NULL, s, -1e30)
            m = jnp.max(s, axis=-1, keepdims=True)
            p = jnp.exp2(s - m).astype(BF16)
            o_raw = _dot(p, v_ref[...])
            l = jnp.sum(jnp.where(lane == HEAD_DIM, o_raw, 0.0), axis=-1, keepdims=True)
            o = jnp.where(lane < HEAD_DIM, o_raw / l, 0.0)
            lse = m + jnp.log2(l)
            for g in range(GH):
                sl = slice((ch * GH + g) * HP, (ch * GH + g + 1) * HP)
                o_ref[:, sl] = jnp.where(rows >= NULL, o[g * QB:(g + 1) * QB], 0.0).astype(BF16)
                lse_ref[:, sl] = jnp.broadcast_to(lse[g * QB:(g + 1) * QB], (QB, HP))

    qspec = pl.BlockSpec((QB, ATT_G * HP), lambda kv, qb: (qb, kv))
    kspec = pl.BlockSpec((lp, HP), lambda kv, qb: (0, kv))
    return pl.pallas_call(
        body, name="attn_fwd", grid=(ATT_KV, nq),
        in_specs=[qspec, kspec, kspec], out_specs=[qspec, qspec],
        out_shape=[jax.ShapeDtypeStruct((lp, ATT_H * HP), BF16), jax.ShapeDtypeStruct((lp, ATT_H * HP), F32)],
        compiler_params=_cparams(VMEM_BIG),
    )(qr, kr, vb)


def _attn_bwd(qr, kr, vb, o, lse, do):
    lp = qr.shape[0]
    nq = lp // QB

    def body(q_ref, k_ref, v_ref, o_ref, lse_ref, do_ref, dq_ref, dk_ref, dv_ref):
        qb = pl.program_id(1)

        @pl.when(qb == 0)
        def _():
            dk_ref[...] = jnp.zeros_like(dk_ref)
            dv_ref[...] = jnp.zeros_like(dv_ref)

        keys = lax.broadcasted_iota(jnp.int32, (1, lp), 1)
        k = k_ref[...]
        dk_acc, dv_acc = None, None
        for ch in range(ATT_G // GH):
            g0 = ch * GH
            qs = _stack(q_ref, g0, GH)
            dos = _stack(do_ref, g0, GH)
            os_ = _stack(o_ref, g0, GH).astype(F32)
            lse_s = jnp.concatenate([lse_ref[:, (g0 + g) * HP:(g0 + g) * HP + 1] for g in range(GH)], axis=0)
            delta = jnp.sum(dos * os_, axis=-1, keepdims=True) * LN2
            s = _dot(qs, k, "nt")
            p = jnp.where(keys >= NULL, jnp.exp2(s - lse_s), 0.0)
            dob = dos.astype(BF16)
            dp = _dot((dos * LN2).astype(BF16), v_ref[...], "nt")
            ds = (p * (dp - delta)).astype(BF16)
            dq = _dot(ds, k)
            for g in range(GH):
                dq_ref[:, (g0 + g) * HP:(g0 + g + 1) * HP] = dq[g * QB:(g + 1) * QB]
            dv_c = _dot(p.astype(BF16), dob, "tn")
            dk_c = _dot(ds, qs, "tn")
            dv_acc = dv_c if dv_acc is None else dv_acc + dv_c
            dk_acc = dk_c if dk_acc is None else dk_acc + dk_c
        dv_ref[...] += dv_acc
        dk_ref[...] += dk_acc

    qspec = pl.BlockSpec((QB, ATT_G * HP), lambda kv, qb: (qb, kv))
    kspec = pl.BlockSpec((lp, HP), lambda kv, qb: (0, kv))
    return pl.pallas_call(
        body, name="attn_bwd", grid=(ATT_KV, nq),
        in_specs=[qspec, kspec, kspec, qspec, qspec, qspec], out_specs=[qspec, kspec, kspec],
        out_shape=[jax.ShapeDtypeStruct((lp, ATT_H * HP), F32), jax.ShapeDtypeStruct((lp, ATT_KV * HP), F32),
                   jax.ShapeDtypeStruct((lp, ATT_KV * HP), F32)],
        compiler_params=_cparams(VMEM_BIG),
    )(qr, kr, vb, o, lse, do)


def _mixer_fwd(h, gain, wl, l, tabs):
    lp = h.shape[0]
    tm = _tm(lp)
    z = _rmsnorm_fwd(h, gain)

    def id_epi(accs, exs, row0):
        return [accs[0]]

    (hin,) = _mm("in_proj", lp, D_INP, [_term(z, wl["win_t"], "nt", 0, (), (l,))], [((lp, D_INP), F32, "mn", 0, ())],
                 id_epi, tm=tm, tn=D_INP // 2, vmem=VMEM_BIG)
    gates = _gla_gates(hin, wl["w2p"][l], wl["b2p"][l])
    o_f, o_b, s_f, s_b = _gla_fwd(hin, gates)
    o2, states = (o_f, o_b), (s_f, s_b)
    a = _gla_out_fwd(o2, hin, wl["gn"][l])
    qr, kr, vb = _attn_prep(hin, wl["gq"][l], wl["gk"][l], tabs)
    b, lse = _attn_fwd(qr, kr, vb)

    def merge_epi(accs, exs, row0):
        pa, pb = accs
        ga, gb, bma, bmb = exs
        y = _sigmoid(ga + bma) * pa + _sigmoid(gb + bmb) * pb
        return [y, pa, pb]

    tn = 512
    y, pa, pb = _mm("merge", lp, D, [_term(a, wl["wpa_t"], "nt", 0, (), (l,)), _term(b, wl["wpb_t"], "nt", 1, (), (l,))],
                    [((lp, D), BF16, "mn", 0, ())] * 3, merge_epi,
                    extras=[(hin, "mn", P_OFF["ga"] // tn, ()), (hin, "mn", P_OFF["gb"] // tn, ()),
                            (wl["bm"], "n", 0, (l, 0)), (wl["bm"], "n", 0, (l, 1))],
                    tm=tm, tn=tn, i_outer=True, vmem=VMEM_BIG)

    def res_epi(accs, exs, row0):
        return [exs[0] + accs[0]]

    (h2,) = _mm("out_proj", lp, D, [_term(y, wl["wout"], "nn", 0, (), (l,))], [((lp, D), F32, "mn", 0, ())], res_epi,
                extras=[(h, "mn", 0, ())], tm=tm, tn=D, i_outer=True, vmem=VMEM_BIG)
    sv = dict(h=h, z=z, hin=hin, gates=gates, o2=o2, states=states, a=a, qr=qr, kr=kr, vb=vb, b=b, lse=lse,
              y=y, pa=pa, pb=pb)
    return h2, sv


def _mixer_bwd(dh, dhb, sv, gain, wl, l, tabs):
    lp = dh.shape[0]
    tm = _tm(lp)
    hin = sv["hin"]
    tn = 512

    def merge_bwd_epi(accs, exs, row0):
        dy = accs[0]
        ga, gb, pa, pb, bma, bmb = exs
        sa = _sigmoid(ga + bma)
        sb = _sigmoid(gb + bmb)
        dga = dy * pa.astype(F32) * sa * (1.0 - sa)
        dgb = dy * pb.astype(F32) * sb * (1.0 - sb)
        return [dy * sa, dy * sb, dga, dgb, jnp.sum(dga, axis=0, keepdims=True), jnp.sum(dgb, axis=0, keepdims=True)]

    big = ((lp, D), BF16, "mn", 0, ())
    vec = ((1, D), F32, "nsum", 0, ())
    dpa, dpb, dga, dgb, dbma, dbmb = _mm(
        "merge_bwd", lp, D, [_term(dhb, wl["wout"], "nt", 0, (), (l,))], [big, big, big, big, vec, vec], merge_bwd_epi,
        extras=[(hin, "mn", P_OFF["ga"] // tn, ()), (hin, "mn", P_OFF["gb"] // tn, ()), (sv["pa"], "mn", 0, ()),
                (sv["pb"], "mn", 0, ()), (wl["bm"], "n", 0, (l, 0)), (wl["bm"], "n", 0, (l, 1))],
        tm=tm, tn=tn, vmem=VMEM_BIG)
    d_wout = _dw("dw_out", sv["y"], dhb, D, D)
    d_wpa_t = _dw("dw_pa", dpa, sv["a"], D, 512)
    d_wpb_t = _dw("dw_pb", dpb, sv["b"], D, ATT_H * HP)

    def id_epi(accs, exs, row0):
        return [accs[0]]

    (da,) = _mm("d_a", lp, 512, [_term(dpa, wl["wpa_t"], "nn", 0, (), (l,))], [((lp, 512), F32, "mn", 0, ())], id_epi,
                tm=tm, tn=512, i_outer=True, vmem=VMEM_BIG)
    (db,) = _mm("d_b", lp, ATT_H * HP, [_term(dpb, wl["wpb_t"], "nn", 0, (), (l,))],
                [((lp, ATT_H * HP), F32, "mn", 0, ())], id_epi, tm=tm, tn=512, i_outer=True, vmem=VMEM_BIG)
    d_o, d_ra, d_gn = _gla_out_bwd(da, sv["o2"], hin, wl["gn"][l])
    g_fw, g_bw = _gla_bwd(hin, sv["gates"], sv["states"], d_o)
    d_qa, d_ka, d_va, d_lr, d_w2p, d_b2p = _gla_in_bwd(g_fw, g_bw, sv["gates"], hin, wl["w2p"][l])
    dqr, dkr, dvb = _attn_bwd(sv["qr"], sv["kr"], sv["vb"], sv["b"], sv["lse"], db)
    d_qb, d_kb, d_gq, d_gk = _attn_prep_bwd(dqr, dkr, hin, wl["gq"][l], wl["gk"][l], tabs)
    pieces = dict(qb=d_qb, ga=dga, gb=dgb, qa=d_qa, ka=d_ka, va=d_va, ra=d_ra, kb=d_kb, vb=dvb.astype(BF16), lr=d_lr)
    dhin = jnp.concatenate([pieces[n] for n in ("qb", "ga", "gb", "qa", "ka", "va", "ra", "kb", "vb", "lr")], axis=1)
    d_win_t = _dw("dw_in", dhin, sv["z"], D_INP, D)
    (dz,) = _mm("in_proj_dz", lp, D, [_term(dhin, wl["win_t"], "nn", 0, (), (l,))], [((lp, D), F32, "mn", 0, ())],
                id_epi, tm=tm, tn=D, nk=2, i_outer=True, vmem=VMEM_BIG)
    dh2, dhb2, dgain = _rmsnorm_bwd(sv["h"], dz, gain, dh)
    grads = dict(gain=dgain, wout=d_wout, wpa_t=d_wpa_t, wpb_t=d_wpb_t, win_t=d_win_t, gn=d_gn, w2p=d_w2p, b2p=d_b2p,
                 gq=d_gq, gk=d_gk, bma=dbma, bmb=dbmb)
    return dh2, dhb2, grads


def _mesh_pos():
    x, y, c = lax.axis_index("x"), lax.axis_index("y"), lax.axis_index("c")
    chips = [(1 - x, y), (x, 1 - y), (1 - x, 1 - y)]
    return x, y, c, chips


def _dev_index(x, y, c):
    return 4 * x + 2 * y + c


def _all_gather(name, shards, leads):
    nt = len(shards)

    def blk(ref, lead, idx):
        return ref.at[(slice(None),) * lead + (idx,)]

    def body(*refs):
        xs, outs = refs[:nt], refs[nt:2 * nt]
        send_sems, recv_sems, local_sems = refs[2 * nt:]
        x, y, c, chips = _mesh_pos()
        me, sibling = (x, y, c), (x, y, 1 - c)

        def copy(t, k, block, to, own=False):
            dst = blk(outs[t], leads[t], _dev_index(*block))
            return pltpu.make_async_remote_copy(
                src_ref=xs[t] if own else dst, dst_ref=dst, send_sem=send_sems.at[t, k], recv_sem=recv_sems.at[t, k],
                device_id=to, device_id_type=MESH)

        locals_ = [pltpu.make_async_copy(xs[t], blk(outs[t], leads[t], _dev_index(*me)), local_sems.at[t])
                   for t in range(nt)]
        for cp in locals_:
            cp.start()
        first = []
        for t in range(nt):
            first.append(copy(t, 0, me, sibling, own=True))
            first += [copy(t, 1 + j, me, (*chip, c), own=True) for j, chip in enumerate(chips)]
        for cp in first:
            cp.start()
        passed = []
        for j, chip in enumerate(chips):
            for t in range(nt):
                copy(t, 1 + j, (*chip, c), me).wait_recv()
                fw = copy(t, 4 + j, (*chip, c), sibling)
                fw.start()
                passed.append(fw)
        for t in range(nt):
            copy(t, 0, sibling, me).wait_recv()
        for j, chip in enumerate(chips):
            for t in range(nt):
                copy(t, 4 + j, (*chip, 1 - c), me).wait_recv()
        for cp in first + passed:
            cp.wait_send()
        for cp in locals_:
            cp.wait()

    out_shape = [jax.ShapeDtypeStruct(s.shape[:ld] + (NDEV,) + s.shape[ld:], s.dtype) for s, ld in zip(shards, leads)]
    return pl.pallas_call(
        body, name=name, in_specs=[ANY] * nt, out_specs=[ANY] * nt, out_shape=out_shape,
        scratch_shapes=[pltpu.SemaphoreType.DMA((nt, 7)), pltpu.SemaphoreType.DMA((nt, 7)),
                        pltpu.SemaphoreType.DMA((nt,))],
    )(*shards)


def _exchange_sibling(name, gs):
    nt = len(gs)

    def body(*refs):
        xs, outs = refs[:nt], refs[nt:2 * nt]
        send_sems, recv_sems = refs[2 * nt:]
        x, y, c, _ = _mesh_pos()
        sibling = (x, y, 1 - c)
        copies = []
        for t in range(nt):
            for ch in range(4):
                copies.append(pltpu.make_async_remote_copy(
                    src_ref=xs[t].at[2 * ch + (1 - c)], dst_ref=outs[t].at[ch],
                    send_sem=send_sems.at[t, ch], recv_sem=recv_sems.at[t, ch],
                    device_id=sibling, device_id_type=MESH))
        for cp in copies:
            cp.start()
        for cp in copies:
            cp.wait()

    out_shape = [jax.ShapeDtypeStruct((4,) + g.shape[1:], g.dtype) for g in gs]
    return pl.pallas_call(
        body, name=name, in_specs=[ANY] * nt, out_specs=[ANY] * nt, out_shape=out_shape,
        scratch_shapes=[pltpu.SemaphoreType.DMA((nt, 4)), pltpu.SemaphoreType.DMA((nt, 4))],
    )(*gs)


def _pair_sum(name, gs, recv):
    c = lax.axis_index("c")
    outs = []
    for t, (g, rv) in enumerate(zip(gs, recv)):
        _, r, cols = rv.shape

        def body(c_ref, g_ref, r_ref, o_ref):
            o_ref[...] = (g_ref[...].astype(F32) + r_ref[...].astype(F32)).astype(o_ref.dtype)

        outs.append(pl.pallas_call(
            body, name=f"{name}_{t}",
            grid_spec=pltpu.PrefetchScalarGridSpec(
                num_scalar_prefetch=1, grid=(4,),
                in_specs=[pl.BlockSpec((None, r, cols), lambda ch, cr: (2 * ch + cr[0], 0, 0)),
                          pl.BlockSpec((None, r, cols), lambda ch, cr: (ch, 0, 0))],
                out_specs=pl.BlockSpec((None, r, cols), lambda ch, cr: (ch, 0, 0))),
            out_shape=jax.ShapeDtypeStruct(rv.shape, rv.dtype),
        )(jnp.reshape(c, (1,)).astype(jnp.int32), g, rv))
    return outs


def _exchange_chips(name, ps):
    nt = len(ps)

    def body(*refs):
        xs, outs = refs[:nt], refs[nt:2 * nt]
        send_sems, recv_sems = refs[2 * nt:]
        x, y, c, chips = _mesh_pos()
        copies = []
        for t in range(nt):
            for j, (cx, cy) in enumerate(chips):
                copies.append(pltpu.make_async_remote_copy(
                    src_ref=xs[t].at[2 * cx + cy], dst_ref=outs[t].at[j],
                    send_sem=send_sems.at[t, j], recv_sem=recv_sems.at[t, j],
                    device_id=(cx, cy, c), device_id_type=MESH))
        for cp in copies:
            cp.start()
        for cp in copies:
            cp.wait()

    out_shape = [jax.ShapeDtypeStruct((3,) + p.shape[1:], p.dtype) for p in ps]
    return pl.pallas_call(
        body, name=name, in_specs=[ANY] * nt, out_specs=[ANY] * nt, out_shape=out_shape,
        scratch_shapes=[pltpu.SemaphoreType.DMA((nt, 3)), pltpu.SemaphoreType.DMA((nt, 3))],
    )(*ps)


def _final_sum(name, ps, recv):
    chip = 2 * lax.axis_index("x") + lax.axis_index("y")
    outs = []
    for t, (p, rv) in enumerate(zip(ps, recv)):
        _, r, cols = rv.shape

        def body(c_ref, p_ref, r0_ref, r1_ref, r2_ref, o_ref):
            o_ref[...] = ((p_ref[...].astype(F32) + r0_ref[...].astype(F32)) + r1_ref[...].astype(F32)) + r2_ref[...].astype(F32)

        outs.append(pl.pallas_call(
            body, name=f"{name}_{t}",
            grid_spec=pltpu.PrefetchScalarGridSpec(
                num_scalar_prefetch=1, grid=(1,),
                in_specs=[pl.BlockSpec((None, r, cols), lambda i, cr: (cr[0], 0, 0))] +
                         [pl.BlockSpec((None, r, cols), lambda i, cr, j=j: (j, 0, 0)) for j in range(3)],
                out_specs=pl.BlockSpec((r, cols), lambda i, cr: (0, 0))),
            out_shape=jax.ShapeDtypeStruct((r, cols), F32),
        )(jnp.reshape(chip, (1,)).astype(jnp.int32), p, rv, rv, rv))
    return outs


def _reduce_scatter(tag, gs):
    recv1 = _exchange_sibling(f"rs_sibling_{tag}", gs)
    ps = _pair_sum(f"rs_pair_{tag}", gs, recv1)
    recv2 = _exchange_chips(f"rs_chips_{tag}", ps)
    return _final_sum(f"rs_sum_{tag}", ps, recv2)


def _sum_gathered(g):
    _, r, cols = g.shape

    def body(g_ref, o_ref):
        acc = g_ref[0]
        for d in range(1, NDEV):
            acc = acc + g_ref[d]
        o_ref[...] = acc

    return pl.pallas_call(body, name="small_sum", out_shape=jax.ShapeDtypeStruct((r, cols), F32))(g)


HBM = pl.BlockSpec(memory_space=pltpu.HBM)
SEM = pl.BlockSpec(memory_space=pltpu.SEMAPHORE)
EFFECT = pltpu.SideEffectType.DATAFLOW_SIDE_EFFECTING
NREL = NDEV - 1


def _related(k):
    x, y, c = lax.axis_index("x"), lax.axis_index("y"), lax.axis_index("c")
    px = 1 - x if k & 4 else x
    py = 1 - y if k & 2 else y
    pc = 1 - c if k & 1 else c
    return (px, py, pc), _dev_index(px, py, pc)


def _in_hbm(a):
    return pltpu.with_memory_space_constraint(a, pltpu.HBM)


def _split_copies(xs, lands, send_sems, recv_sems, src_of, dst_of):
    copies = []
    for t in range(len(xs)):
        for k in range(1, NDEV):
            peer, peer_idx = _related(k)
            copies.append(pltpu.make_async_remote_copy(
                src_ref=src_of(xs[t], t, peer_idx), dst_ref=dst_of(lands[t], t, k, peer_idx),
                send_sem=send_sems.at[t * NREL + k - 1], recv_sem=recv_sems.at[t * NREL + k - 1],
                device_id=peer, device_id_type=MESH))
    return copies


def _exchange_start(name, xs, lands, src_of, dst_of, after):
    nt = len(xs)

    def body(*refs):
        x_refs, land_refs = refs[:nt], refs[nt:2 * nt]
        send_sems, recv_sems = refs[2 * nt + 1], refs[2 * nt + 2]
        token = refs[-1]
        for cp in _split_copies(x_refs, land_refs, send_sems, recv_sems, src_of, dst_of):
            cp.start()
        token[...] = jnp.zeros_like(token)

    res = pl.pallas_call(
        body, name=name,
        out_shape=(pltpu.SemaphoreType.DMA((nt * NREL,)), pltpu.SemaphoreType.DMA((nt * NREL,)),
                   *[pltpu.HBM(a.shape, a.dtype) for a in xs], *[pltpu.HBM(a.shape, a.dtype) for a in lands],
                   jax.ShapeDtypeStruct((8, 128), F32)),
        in_specs=[HBM] * (2 * nt) + [ANY],
        out_specs=(SEM, SEM, *[HBM] * (2 * nt), pl.BlockSpec(memory_space=pltpu.VMEM)),
        input_output_aliases={i: 2 + i for i in range(2 * nt)},
        compiler_params=pltpu.CompilerParams(has_side_effects=EFFECT),
    )(*[_in_hbm(a) for a in xs], *[_in_hbm(a) for a in lands], after)
    return res[0], res[1], res[2:2 + nt], res[2 + nt:2 + 2 * nt], res[-1]


def _exchange_wait(name, send_sems, recv_sems, xs, lands, src_of, dst_of, after):
    nt = len(xs)

    def body(*refs):
        x_refs, land_refs = refs[:nt], refs[nt:2 * nt]
        send_sems, recv_sems = refs[2 * nt], refs[2 * nt + 1]
        for cp in _split_copies(x_refs, land_refs, send_sems, recv_sems, src_of, dst_of):
            cp.wait_send()
            cp.wait_recv()

    res = pl.pallas_call(
        body, name=name,
        out_shape=(*[pltpu.HBM(a.shape, a.dtype) for a in xs], *[pltpu.HBM(a.shape, a.dtype) for a in lands]),
        in_specs=[HBM] * (2 * nt) + [SEM, SEM, ANY], out_specs=tuple([HBM] * (2 * nt)),
        input_output_aliases={i: i for i in range(2 * nt)},
        compiler_params=pltpu.CompilerParams(has_side_effects=EFFECT),
    )(*xs, *lands, send_sems, recv_sems, after)
    return res[:nt], res[nt:]


def _gather_start(name, shards, leads, after):
    def src_of(x_ref, t, peer_idx):
        return x_ref

    def dst_of(land_ref, t, k, peer_idx):
        me = _dev_index(lax.axis_index("x"), lax.axis_index("y"), lax.axis_index("c"))
        return land_ref.at[(slice(None),) * leads[t] + (me,)]

    lands = [lax.empty(s.shape[:ld] + (NDEV,) + s.shape[ld:], s.dtype) for s, ld in zip(shards, leads)]
    return _exchange_start(name, shards, lands, src_of, dst_of, after)


def _gather_wait(name, started, leads, after):
    send_sems, recv_sems, shards, lands, _ = started

    def src_of(x_ref, t, peer_idx):
        return x_ref

    def dst_of(land_ref, t, k, peer_idx):
        return land_ref.at[(slice(None),) * leads[t] + (peer_idx,)]

    shards, lands = _exchange_wait(name, send_sems, recv_sems, shards, lands, src_of, dst_of, after)
    me = _dev_index(lax.axis_index("x"), lax.axis_index("y"), lax.axis_index("c"))
    return [lax.dynamic_update_index_in_dim(g, s, me, ld) for g, s, ld in zip(lands, shards, leads)]


def _scatter_src(x_ref, t, peer_idx):
    return x_ref.at[peer_idx]


def _scatter_dst(land_ref, t, k, peer_idx):
    return land_ref.at[k - 1]


def _scatter_start(name, gs, after):
    lands = [lax.empty((NREL,) + g.shape[1:], g.dtype) for g in gs]
    return _exchange_start(name, gs, lands, _scatter_src, _scatter_dst, after)


def _scatter_wait(name, started, after):
    send_sems, recv_sems, gs, lands, _ = started
    gs, lands = _exchange_wait(name, send_sems, recv_sems, gs, lands, _scatter_src, _scatter_dst, after)
    me = _dev_index(lax.axis_index("x"), lax.axis_index("y"), lax.axis_index("c"))
    outs = []
    for t, (g, rv) in enumerate(zip(gs, lands)):
        _, r, cols = rv.shape

        def body(c_ref, own_ref, rv_ref, o_ref):
            acc = own_ref[...].astype(F32)
            for k in range(NREL):
                acc = acc + rv_ref[k].astype(F32)
            o_ref[...] = acc

        outs.append(pl.pallas_call(
            body, name=f"{name}_sum_{t}",
            grid_spec=pltpu.PrefetchScalarGridSpec(
                num_scalar_prefetch=1, grid=(1,),
                in_specs=[pl.BlockSpec((None, r, cols), lambda i, cr: (cr[0], 0, 0)),
                          pl.BlockSpec((NREL, r, cols), lambda i, cr: (0, 0, 0))],
                out_specs=pl.BlockSpec((r, cols), lambda i, cr: (0, 0))),
            out_shape=jax.ShapeDtypeStruct((r, cols), F32), compiler_params=_cparams(VMEM_BIG),
        )(jnp.reshape(me, (1,)).astype(jnp.int32), g, rv))
    return outs


def _adamw(w, g, m, v):
    shape = w.shape
    cols = shape[-1]
    rows = math.prod(shape[:-1]) if len(shape) > 1 else 1
    w2, g2, m2, v2 = (jnp.reshape(t, (rows, cols)) for t in (w, g, m, v))
    tr = _pick(rows, (512, 256, 128)) if rows * cols > 65536 else rows
    c1 = 1.0 / (1.0 - ADAM_B1 ** ADAM_STEP)
    c2 = 1.0 / (1.0 - ADAM_B2 ** ADAM_STEP)

    def body(w_ref, g_ref, m_ref, v_ref, d_ref, nm_ref, nv_ref):
        gv = g_ref[...]
        nm = ADAM_B1 * m_ref[...] + (1.0 - ADAM_B1) * gv
        nv = ADAM_B2 * v_ref[...] + (1.0 - ADAM_B2) * (gv * gv)
        d_ref[...] = -ADAM_LR * ((nm * c1) / (jnp.sqrt(nv * c2) + ADAM_EPS) + ADAM_WD * w_ref[...])
        nm_ref[...] = nm
        nv_ref[...] = nv

    spec = pl.BlockSpec((tr, cols), lambda i: (i, 0))
    osh = jax.ShapeDtypeStruct((rows, cols), F32)
    d, nm, nv = pl.pallas_call(
        body, name="adamw", grid=(rows // tr,), in_specs=[spec] * 4, out_specs=[spec] * 3, out_shape=[osh] * 3,
    )(w2, g2, m2, v2)
    return jnp.reshape(d, shape), jnp.reshape(nm, shape), jnp.reshape(nv, shape)


def _pad_heads(w, name):
    if name not in P_HEADS:
        return w
    nh, real = P_HEADS[name]
    w = jnp.reshape(w, w.shape[:-2] + (nh, real, w.shape[-1]))
    w = jnp.pad(w, [(0, 0)] * (w.ndim - 2) + [(0, HP - real), (0, 0)])
    return jnp.reshape(w, w.shape[:-3] + (nh * HP, w.shape[-1]))


def _unpad_heads(w, name):
    if name not in P_HEADS:
        return w
    nh, real = P_HEADS[name]
    w = jnp.reshape(w, w.shape[:-2] + (nh, HP, w.shape[-1]))[..., :real, :]
    return jnp.reshape(w, w.shape[:-3] + (nh * real, w.shape[-1]))


def _win_pad(win_t):
    segs, o = {}, 0
    for n, s in zip(IN_NAMES, IN_SIZES):
        segs[n] = win_t[..., o:o + s, :]
        o += s
    return jnp.concatenate([_pad_heads(segs[n], n) for n in P_ORDER], axis=-2)


def _win_unpad(win_p):
    segs = {n: _unpad_heads(win_p[..., P_OFF[n]:P_OFF[n] + P_WIDTH[n], :], n) for n in P_ORDER}
    return jnp.concatenate([segs[n] for n in IN_NAMES], axis=-2)


def _t(w):
    return jnp.swapaxes(w, -1, -2)


def _layer_weights(g_g, g_u, g_d, g_in, g_pa, g_pb, g_out, gains, w2, b2, bm, gn, gq, gk):
    w2p = jnp.pad(jnp.reshape(w2, (2, GLA_RANK, GLA_H, GLA_DK)), ((0, 0), (0, HP - GLA_RANK), (0, 0), (0, HP - GLA_DK)))
    b2p = jnp.pad(jnp.reshape(b2, (2, 1, GLA_H, GLA_DK)), ((0, 0), (0, 0), (0, 0), (0, HP - GLA_DK)))
    wpb_t = jnp.pad(jnp.reshape(g_pb, (D, ATT_H, HEAD_DIM)), ((0, 0), (0, 0), (0, HP - HEAD_DIM)))
    return dict(
        gains=jnp.reshape(gains, (1, 3, 1, D)),
        wg_t=jnp.reshape(g_g, (1, 2, DFF, D)), wu_t=jnp.reshape(g_u, (1, 2, DFF, D)), wd=jnp.reshape(g_d, (1, 2, DFF, D)),
        win_t=_win_pad(jnp.reshape(g_in, (1, D_IN, D))), wpa_t=jnp.reshape(g_pa, (1, D, 512)),
        wpb_t=jnp.reshape(wpb_t, (1, D, ATT_H * HP)), wout=jnp.reshape(g_out, (1, D, D)),
        w2p=jnp.reshape(w2p, (1, 2, HP, GLA_H * HP)), b2p=jnp.reshape(b2p, (1, 2, 1, GLA_H * HP)),
        bm=jnp.reshape(bm, (1, 2, 1, D)), gn=jnp.reshape(gn, (1, 1, GLA_H * HP)),
        gq=jnp.pad(jnp.reshape(gq, (1, 1, HEAD_DIM)), ((0, 0), (0, 0), (0, HP - HEAD_DIM))),
        gk=jnp.pad(jnp.reshape(gk, (1, 1, HEAD_DIM)), ((0, 0), (0, 0), (0, HP - HEAD_DIM))))


def _layer_fwd(h, w, tabs):
    h, s0 = _ffn_fwd(h, w["gains"][0, 0], w["wg_t"], w["wu_t"], w["wd"], (0, 0))
    h, s1 = _mixer_fwd(h, w["gains"][0, 1], w, 0, tabs)
    h, s2 = _ffn_fwd(h, w["gains"][0, 2], w["wg_t"], w["wu_t"], w["wd"], (0, 1))
    return h, (s0, s1, s2)


def _layer_bwd(dh, dhb, saved, w, tabs):
    s0, s1, s2 = saved
    dh, dhb, dg2, dwg1, dwu1, dwd1 = _ffn_bwd(dh, dhb, s2, w["gains"][0, 2], w["wg_t"], w["wu_t"], w["wd"], (0, 1))
    dh, dhb, gm = _mixer_bwd(dh, dhb, s1, w["gains"][0, 1], w, 0, tabs)
    dh, dhb, dg0, dwg0, dwu0, dwd0 = _ffn_bwd(dh, dhb, s0, w["gains"][0, 0], w["wg_t"], w["wu_t"], w["wd"], (0, 0))
    gm.update(gain0=dg0, gain2=dg2, wg0=dwg0, wu0=dwu0, wd0=dwd0, wg1=dwg1, wu1=dwu1, wd1=dwd1)
    return dh, dhb, gm


def _big_grads(g):
    d_in = _win_unpad(g["win_t"])
    d_pb = jnp.reshape(jnp.reshape(g["wpb_t"], (D, ATT_H, HP))[:, :, :HEAD_DIM], (D, 512))
    ts = [g["wg0"], g["wu0"], g["wd0"], g["wg1"], g["wu1"], g["wd1"], d_in, g["wpa_t"], d_pb, g["wout"]]
    return [jnp.reshape(t, (NDEV, t.shape[0] // NDEV, t.shape[1])) for t in ts]


def kernel(x, meta_tokens, norm_gains, ffn_w_gate, ffn_w_up, ffn_w_down, w_in, gla_w2, gla_b2, gla_gn, q_norm, k_norm, w_pa, w_pb, b_merge, w_out, final_norm, loss_target, m_meta_tokens, m_norm_gains, m_ffn_w_gate, m_ffn_w_up, m_ffn_w_down, m_w_in, m_gla_w2, m_gla_b2, m_gla_gn, m_q_norm, m_k_norm, m_w_pa, m_w_pb, m_b_merge, m_w_out, m_final_norm, v_meta_tokens, v_norm_gains, v_ffn_w_gate, v_ffn_w_up, v_ffn_w_down, v_w_in, v_gla_w2, v_gla_b2, v_gla_gn, v_q_norm, v_k_norm, v_w_pa, v_w_pb, v_b_merge, v_w_out, v_final_norm):
    dev = _dev_index(lax.axis_index("x"), lax.axis_index("y"), lax.axis_index("c"))
    sh_g = _t(ffn_w_gate).astype(BF16)
    sh_u = _t(ffn_w_up).astype(BF16)
    sh_d = ffn_w_down.astype(BF16)
    sh_in = _t(w_in).astype(BF16)
    sh_pa = _t(w_pa).astype(BF16)
    sh_pb = _t(w_pb).astype(BF16)
    sh_out = w_out.astype(BF16)
    small = jnp.concatenate([jnp.reshape(t, (-1, 128)) for t in
                             (meta_tokens, norm_gains, gla_w2, gla_b2, b_merge)], axis=0)
    small = jnp.pad(small, ((0, 2), (0, 0)))
    def shards(l):
        return [sh_g[l], sh_u[l], sh_d[l], sh_in[l], sh_pa[l], sh_pb[l], sh_out[l]]

    w_leads = [1, 1, 1, 0, 0, 0, 0]
    *g0, g_small = _all_gather("gather_layer0", shards(0) + [small], w_leads + [0])
    started = {l: _gather_start(f"gather_start_{l}", shards(l), w_leads, g_small) for l in range(1, DEPTH)}
    tok = sum(started[l][4][0, 0] for l in range(1, DEPTH))
    meta_full = jnp.reshape(jnp.transpose(g_small[:, 0:16], (1, 0, 2)), (NMETA, D)) + tok
    gains_full = jnp.reshape(jnp.transpose(jnp.reshape(g_small[:, 16:28], (NDEV, DEPTH, 3, 128)), (1, 2, 0, 3)), (DEPTH, 3, D))
    w2_full = jnp.reshape(jnp.transpose(jnp.reshape(g_small[:, 28:60], (NDEV, DEPTH, 2, GLA_RANK, 32)), (1, 2, 3, 0, 4)),
                          (DEPTH, 2, GLA_RANK, 256))
    b2_full = jnp.reshape(jnp.transpose(jnp.reshape(g_small[:, 60:62], (NDEV, DEPTH, 2, 32)), (1, 2, 0, 3)), (DEPTH, 2, 256))
    bm_full = jnp.reshape(jnp.transpose(jnp.reshape(g_small[:, 62:70], (NDEV, DEPTH, 2, 128)), (1, 2, 0, 3)), (DEPTH, 2, D))

    def layer_weights(l, gathered, gains_l):
        return _layer_weights(*gathered, gains_l, w2_full[l], b2_full[l], bm_full[l], gla_gn[l], q_norm[l], k_norm[l])

    xl = x[0]
    lp = xl.shape[0] + NULL + NMETA
    tabs = _rope_tables(lp)
    h = jnp.concatenate([jnp.zeros((NULL, D), F32), meta_full, xl], axis=0)
    weights, saved = [], []
    for l in range(DEPTH):
        gathered = g0 if l == 0 else _gather_wait(f"gather_wait_{l}", started[l], w_leads, h)
        weights.append(layer_weights(l, gathered, gains_full[l]))
        h, sv = _layer_fwd(h, weights[l], tabs)
        saved.append(sv)
    loss, dh, dhb, d_final = _loss_head(h, loss_target[0], jnp.reshape(final_norm, (1, D)))
    loss = lax.psum(loss[0, 0], ("x", "y", "c"))

    grads, scattering = [None] * DEPTH, {}
    tok = jnp.zeros((), F32)
    for l in reversed(range(DEPTH)):
        w = dict(weights[l], gains=weights[l]["gains"] + tok)
        dh, dhb, grads[l] = _layer_bwd(dh, dhb, saved[l], w, tabs)
        if l > 0:
            scattering[l] = _scatter_start(f"scatter_start_{l}", _big_grads(grads[l]), dhb)
            tok = scattering[l][4][0, 0]
    grad_x = dh[NULL + NMETA:][None]
    red = [None] * DEPTH
    for l in reversed(range(1, DEPTH)):
        red[l] = _scatter_wait(f"scatter_wait_{l}", scattering[l], dhb)
    red[0] = _reduce_scatter("0", _big_grads(grads[0]))
    g_gate = jnp.stack([jnp.stack([_t(red[l][0]), _t(red[l][3])]) for l in range(DEPTH)])
    g_up = jnp.stack([jnp.stack([_t(red[l][1]), _t(red[l][4])]) for l in range(DEPTH)])
    g_down = jnp.stack([jnp.stack([red[l][2], red[l][5]]) for l in range(DEPTH)])
    g_win = jnp.stack([_t(red[l][6]) for l in range(DEPTH)])
    g_wpa = jnp.stack([_t(red[l][7]) for l in range(DEPTH)])
    g_wpb = jnp.stack([_t(red[l][8]) for l in range(DEPTH)])
    g_wout = jnp.stack([red[l][9] for l in range(DEPTH)])

    d_meta = dh[NULL:NULL + NMETA]
    d_gains = jnp.stack([jnp.concatenate([grads[l]["gain0"], grads[l]["gain"], grads[l]["gain2"]], axis=0)
                         for l in range(DEPTH)])
    d_w2 = jnp.stack([jnp.reshape(jnp.reshape(grads[l]["w2p"], (2, HP, GLA_H, HP))[:, :GLA_RANK, :, :GLA_DK],
                                  (2, GLA_RANK, 256)) for l in range(DEPTH)])
    d_b2 = jnp.stack([jnp.reshape(jnp.reshape(grads[l]["b2p"], (2, GLA_H, HP))[:, :, :GLA_DK], (2, 256))
                      for l in range(DEPTH)])
    d_gn = jnp.stack([grads[l]["gn"][0] for l in range(DEPTH)])
    d_gq = jnp.stack([grads[l]["gq"][0, :HEAD_DIM] for l in range(DEPTH)])
    d_gk = jnp.stack([grads[l]["gk"][0, :HEAD_DIM] for l in range(DEPTH)])
    d_bm = jnp.stack([jnp.concatenate([grads[l]["bma"], grads[l]["bmb"]], axis=0) for l in range(DEPTH)])
    parts = [d_meta, d_gains, d_w2, d_b2, d_gn, d_gq, d_gk, d_bm, d_final[0]]
    sizes = [p.size for p in parts]
    flat = jnp.concatenate([jnp.reshape(p, (-1,)) for p in parts])
    flat = jnp.reshape(flat, (-1, 128))
    nrow = flat.shape[0]
    flat = jnp.pad(flat, ((0, (-nrow) % 8), (0, 0)))
    (g_flat,) = _all_gather("gather_small_grads", [flat], [0])
    tot = jnp.reshape(_sum_gathered(g_flat), (-1,))
    full, o = [], 0
    for p, s in zip(parts, sizes):
        full.append(jnp.reshape(tot[o:o + s], p.shape))
        o += s
    f_meta, f_gains, f_w2, f_b2, f_gn, f_gq, f_gk, f_bm, f_final = full

    def mine(t, width):
        return lax.dynamic_slice_in_dim(t, dev * width, width, axis=t.ndim - 1)

    g_small = dict(meta_tokens=mine(f_meta, 128), norm_gains=mine(f_gains, 128), gla_w2=mine(f_w2, 32),
                   gla_b2=mine(f_b2, 32), gla_gn=f_gn, q_norm=f_gq, k_norm=f_gk, b_merge=mine(f_bm, 128),
                   final_norm=f_final)
    gr = dict(g_small, ffn_w_gate=g_gate, ffn_w_up=g_up, ffn_w_down=g_down, w_in=g_win, w_pa=g_wpa, w_pb=g_wpb,
              w_out=g_wout)
    ws = dict(meta_tokens=meta_tokens, norm_gains=norm_gains, ffn_w_gate=ffn_w_gate, ffn_w_up=ffn_w_up,
              ffn_w_down=ffn_w_down, w_in=w_in, gla_w2=gla_w2, gla_b2=gla_b2, gla_gn=gla_gn, q_norm=q_norm,
              k_norm=k_norm, w_pa=w_pa, w_pb=w_pb, b_merge=b_merge, w_out=w_out, final_norm=final_norm)
    ms = dict(meta_tokens=m_meta_tokens, norm_gains=m_norm_gains, ffn_w_gate=m_ffn_w_gate, ffn_w_up=m_ffn_w_up,
              ffn_w_down=m_ffn_w_down, w_in=m_w_in, gla_w2=m_gla_w2, gla_b2=m_gla_b2, gla_gn=m_gla_gn, q_norm=m_q_norm,
              k_norm=m_k_norm, w_pa=m_w_pa, w_pb=m_w_pb, b_merge=m_b_merge, w_out=m_w_out, final_norm=m_final_norm)
    vs = dict(meta_tokens=v_meta_tokens, norm_gains=v_norm_gains, ffn_w_gate=v_ffn_w_gate, ffn_w_up=v_ffn_w_up,
              ffn_w_down=v_ffn_w_down, w_in=v_w_in, gla_w2=v_gla_w2, gla_b2=v_gla_b2, gla_gn=v_gla_gn, q_norm=v_q_norm,
              k_norm=v_k_norm, w_pa=v_w_pa, w_pb=v_w_pb, b_merge=v_b_merge, w_out=v_w_out, final_norm=v_final_norm)
    names = ["meta_tokens", "norm_gains", "ffn_w_gate", "ffn_w_up", "ffn_w_down", "w_in", "gla_w2", "gla_b2", "gla_gn",
             "q_norm", "k_norm", "w_pa", "w_pb", "b_merge", "w_out", "final_norm"]
    deltas, new_m, new_v = [], [], []
    for n in names:
        dlt, nm, nv = _adamw(ws[n], gr[n], ms[n], vs[n])
        deltas.append(dlt)
        new_m.append(nm)
        new_v.append(nv)
    return (loss, grad_x, *[gr[n] for n in names], *deltas, *new_m, *new_v)
```

```python
import functools
import math

import jax
import jax.numpy as jnp
from jax import lax
from jax.experimental import pallas as pl
from jax.experimental.pallas import tpu as pltpu

F32 = jnp.float32
BF16 = jnp.bfloat16
MESH = pl.DeviceIdType.MESH
ANY = pl.BlockSpec(memory_space=pl.ANY)

NDEV = 8
D = 1024
DFF = 2816
DEPTH = 4
NMETA = 16
NULL = 112
GRID_W = 64
EPS = 1e-6
HP = 128
GLA_H = 4
GLA_DK = 64
GLA_RANK = 16
GLA_TAU = 16.0
CHUNK = 64
ATT_H = 8
ATT_KV = 2
ATT_G = ATT_H // ATT_KV
HEAD_DIM = 64
ROPE_THETA = 10000.0

IN_SIZES = (256, 256, 512, 512, 16, 16, 512, 128, 128, 1024, 1024)
IN_NAMES = ("qa", "ka", "va", "ra", "lrf", "lrb", "qb", "kb", "vb", "ga", "gb")
D_IN = sum(IN_SIZES)
P_ORDER = ("qb", "ga", "gb", "qa", "ka", "va", "ra", "kb", "vb", "lrf", "lrb")
P_WIDTH = dict(qb=1024, ga=1024, gb=1024, qa=512, ka=512, va=512, ra=512, kb=256, vb=256, lrf=128, lrb=128)
P_OFF = {}
_o = 0
for _n in P_ORDER:
    P_OFF[_n] = _o
    _o += P_WIDTH[_n]
D_INP = _o
P_HEADS = dict(qa=(4, 64), ka=(4, 64), qb=(8, 64), kb=(2, 64), vb=(2, 64), lrf=(1, 16), lrb=(1, 16))

ADAM_LR = 0.001
ADAM_B1 = 0.9
ADAM_B2 = 0.999
ADAM_EPS = 1e-08
ADAM_WD = 0.01
ADAM_STEP = 10

VMEM_BIG = 56 * 1024 * 1024


def _cparams(vmem=None):
    return pltpu.CompilerParams(vmem_limit_bytes=vmem) if vmem else pltpu.CompilerParams()


def _pick(n, prefs):
    for p in prefs:
        if n % p == 0:
            return p
    return n


def _tm(lp):
    return _pick(lp, (528, 512, 256, 128))


_DN = {"nn": (((1,), (0,)), ((), ())), "nt": (((1,), (1,)), ((), ())), "tn": (((0,), (0,)), ((), ()))}


def _dot(a, b, mode="nn", precision=None):
    return lax.dot_general(a, b, _DN[mode], preferred_element_type=F32, precision=precision)


def _sigmoid(x):
    return 0.5 * jnp.tanh(0.5 * x) + 0.5


def _mm(name, m, n, terms, outs, epilogue, extras=(), *, tm, tn, nk=1, i_outer=False, vmem=None):
    gm, gn = m // tm, n // tn
    assert gm * tm == m and gn * tn == n, (name, m, n, tm, tn)
    n_acc = 1 + max(t[3] for t in terms)

    def gmap(f):
        if i_outer:
            return lambda i, j, kk: f(i, j, kk)
        return lambda j, i, kk: f(i, j, kk)

    in_specs, args = [], []
    for a, b, mode, _, pa, pb in terms:
        kdim = a.shape[-2] if mode == "tn" else a.shape[-1]
        tk = kdim // nk
        assert tk * nk == kdim
        na, nb = (None,) * len(pa), (None,) * len(pb)
        if mode == "tn":
            in_specs.append(pl.BlockSpec(na + (tk, tm), gmap(lambda i, j, kk, pa=pa: pa + (kk, i))))
        else:
            in_specs.append(pl.BlockSpec(na + (tm, tk), gmap(lambda i, j, kk, pa=pa: pa + (i, kk))))
        if mode == "nt":
            in_specs.append(pl.BlockSpec(nb + (tn, tk), gmap(lambda i, j, kk, pb=pb: pb + (j, kk))))
        else:
            in_specs.append(pl.BlockSpec(nb + (tk, tn), gmap(lambda i, j, kk, pb=pb: pb + (kk, j))))
        args += [a, b]
    for arr, kind, off, pe in extras:
        ne = (None,) * len(pe)
        if kind == "mn":
            in_specs.append(pl.BlockSpec(ne + (tm, tn), gmap(lambda i, j, kk, off=off, pe=pe: pe + (i, j + off))))
        else:
            in_specs.append(pl.BlockSpec(ne + (1, tn), gmap(lambda i, j, kk, off=off, pe=pe: pe + (0, j + off))))
        args.append(arr)
    out_shape, out_specs = [], []
    for shape, dtype, kind, off, po in outs:
        no = (None,) * len(po)
        out_shape.append(jax.ShapeDtypeStruct(shape, dtype))
        if kind == "mn":
            out_specs.append(pl.BlockSpec(no + (tm, tn), gmap(lambda i, j, kk, off=off, po=po: po + (i, j + off))))
        else:
            assert not i_outer
            out_specs.append(pl.BlockSpec(no + (1, tn), gmap(lambda i, j, kk, off=off, po=po: po + (0, j + off))))
    n_t, n_e, n_o = len(terms), len(extras), len(outs)
    i_axis = 0 if i_outer else 1

    def body(*refs):
        ins = refs[: 2 * n_t]
        exs = refs[2 * n_t: 2 * n_t + n_e]
        ors = refs[2 * n_t + n_e: 2 * n_t + n_e + n_o]
        accs = refs[2 * n_t + n_e + n_o:]
        i = pl.program_id(i_axis)
        kk = pl.program_id(2)
        part = [None] * n_acc
        for t, (_, _, mode, ai, _, _) in enumerate(terms):
            p = _dot(ins[2 * t][...], ins[2 * t + 1][...], mode)
            part[ai] = p if part[ai] is None else part[ai] + p

        def finish(vals):
            res = epilogue(vals, [e[...] for e in exs], i * tm)
            for (_, dtype, kind, _, _), o_ref, v in zip(outs, ors, res):
                if kind == "mn":
                    o_ref[...] = v.astype(dtype)
                else:
                    @pl.when(i == 0)
                    def _():
                        o_ref[...] = v.astype(dtype)

                    @pl.when(i != 0)
                    def _():
                        o_ref[...] += v.astype(dtype)

        if nk == 1:
            finish(part)
        else:
            @pl.when(kk == 0)
            def _():
                for a_ref, p in zip(accs, part):
                    a_ref[...] = p

            @pl.when(kk != 0)
            def _():
                for a_ref, p in zip(accs, part):
                    a_ref[...] += p

            @pl.when(kk == nk - 1)
            def _():
                finish([a_ref[...] for a_ref in accs])

    scratch = [pltpu.VMEM((tm, tn), F32) for _ in range(n_acc)] if nk > 1 else []
    grid = (gm, gn, nk) if i_outer else (gn, gm, nk)
    res = pl.pallas_call(
        body, name=name, grid=grid, in_specs=in_specs, out_specs=out_specs, out_shape=out_shape,
        scratch_shapes=scratch, compiler_params=_cparams(vmem),
    )(*args)
    return res


def _term(a, b, mode, acc=0, pa=(), pb=()):
    return (a, b, mode, acc, tuple(pa), tuple(pb))


def _row_tile(lp):
    return _pick(lp, (384, 256, 128))


def _rmsnorm_fwd(h, gain):
    lp = h.shape[0]
    tr = _row_tile(lp)

    def body(h_ref, g_ref, z_ref):
        x = h_ref[...]
        r = lax.rsqrt(jnp.mean(x * x, axis=-1, keepdims=True) + EPS)
        z_ref[...] = (x * r * g_ref[...]).astype(BF16)

    return pl.pallas_call(
        body, name="rmsnorm_fwd", grid=(lp // tr,),
        in_specs=[pl.BlockSpec((tr, D), lambda i: (i, 0)), pl.BlockSpec((1, D), lambda i: (0, 0))],
        out_specs=pl.BlockSpec((tr, D), lambda i: (i, 0)),
        out_shape=jax.ShapeDtypeStruct((lp, D), BF16),
    )(h, gain)


def _loss_head(h, target, gain):
    lp = h.shape[0]
    tr = 128

    def body(h_ref, t_ref, g_ref, loss_ref, dh_ref, dhb_ref, dg_ref):
        i = pl.program_id(0)

        @pl.when(i == 0)
        def _():
            loss_ref[...] = jnp.zeros_like(loss_ref)
            dg_ref[...] = jnp.zeros_like(dg_ref)
            dh_ref[...] = jnp.zeros_like(dh_ref)
            dhb_ref[...] = jnp.zeros_like(dhb_ref)

        @pl.when(i != 0)
        def _():
            x = h_ref[...]
            g = g_ref[...]
            r = lax.rsqrt(jnp.mean(x * x, axis=-1, keepdims=True) + EPS)
            xh = x * r
            y = xh * g
            err = y - t_ref[...]
            loss_ref[...] += 0.5 * jnp.sum(jnp.sum(err * err, axis=-1, keepdims=True), axis=0, keepdims=True) / D
            dy = err * (1.0 / D)
            dg_ref[...] += jnp.sum(dy * xh, axis=0, keepdims=True)
            dxh = dy * g
            dx = r * (dxh - xh * jnp.mean(dxh * xh, axis=-1, keepdims=True))
            dh_ref[...] = dx
            dhb_ref[...] = dx.astype(BF16)

    row = pl.BlockSpec((tr, D), lambda i: (i, 0))
    vec = pl.BlockSpec((1, D), lambda i: (0, 0))
    return pl.pallas_call(
        body, name="loss_head", grid=(lp // tr,),
        in_specs=[row, pl.BlockSpec((tr, D), lambda i: (jnp.maximum(i - 1, 0), 0)), vec],
        out_specs=[pl.BlockSpec((1, 1), lambda i: (0, 0)), row, row, vec],
        out_shape=[jax.ShapeDtypeStruct((1, 1), F32), jax.ShapeDtypeStruct((lp, D), F32),
                   jax.ShapeDtypeStruct((lp, D), BF16), jax.ShapeDtypeStruct((1, D), F32)],
    )(h, target, gain)


def _silu_parts(g):
    s = _sigmoid(g)
    return g * s, s * (1.0 + g * (1.0 - s))


def _residual_norm_epi(scale, with_norm):
    def epi(accs, exs, row0):
        h2 = exs[0] + scale * accs[0]
        if not with_norm:
            return [h2]
        r = lax.rsqrt(jnp.mean(h2 * h2, axis=-1, keepdims=True) + EPS)
        return [h2, h2 * r * exs[1]]
    return epi


def _norm_bwd_epi(accs, exs, row0):
    dz = accs[0]
    x, res, g = exs
    r = lax.rsqrt(jnp.mean(x * x, axis=-1, keepdims=True) + EPS)
    xh = x * r
    dxh = dz * g
    dx = r * (dxh - xh * jnp.mean(dxh * xh, axis=-1, keepdims=True))
    rows = row0 + lax.broadcasted_iota(jnp.int32, (dz.shape[0], 1), 0)
    dh = jnp.where(rows >= NULL, res + dx, 0.0)
    return [dh, dh, jnp.sum(dz * xh, axis=0, keepdims=True)]


def _norm_bwd_outs(lp):
    return [((lp, D), F32, "mn", 0, ()), ((lp, D), BF16, "mn", 0, ()), ((1, D), F32, "nsum", 0, ())]


def _ffn_fwd(h, z, wg_t, wu_t, wd, pre, next_gain):
    lp = h.shape[0]
    tm = _tm(lp)

    def up_epi(accs, exs, row0):
        g, u = accs
        sg, _ = _silu_parts(g)
        return [g, u, sg * u]

    tn = _pick(DFF, (1408, 256, 128))
    bshape = (lp, DFF)
    g_, u_, act = _mm("ffn_up", lp, DFF, [_term(z, wg_t, "nt", 0, (), pre), _term(z, wu_t, "nt", 1, (), pre)],
                      [(bshape, BF16, "mn", 0, ())] * 3, up_epi, tm=tm, tn=tn, vmem=VMEM_BIG)

    nk = _pick(DFF // 128, (2, 1))
    with_norm = next_gain is not None
    res = _mm("ffn_down", lp, D, [_term(act, wd, "nn", 0, (), pre)],
              [((lp, D), F32, "mn", 0, ())] + ([((lp, D), BF16, "mn", 0, ())] if with_norm else []),
              _residual_norm_epi(0.5, with_norm),
              extras=[(h, "mn", 0, ())] + ([(next_gain, "n", 0, ())] if with_norm else []),
              tm=tm, tn=D, nk=nk, i_outer=True, vmem=VMEM_BIG)
    return res[0], (res[1] if with_norm else None), dict(h=h, z=z, g=g_, u=u_, act=act)


def _dw(name, a, b, m, n, scale=1.0):
    lp = a.shape[0]
    nk = lp // _pick(lp, (1056, 256, 128))
    tm = _pick(m, (2944, 1408, 1024, 512, 256, 128))
    tn = _pick(n, (1024, 512, 256, 128))

    def epi(accs, exs, row0):
        return [accs[0] * scale]

    (w,) = _mm(name, m, n, [_term(a, b, "tn")], [((m, n), BF16, "mn", 0, ())], epi, tm=tm, tn=tn, nk=nk,
               i_outer=True, vmem=VMEM_BIG)
    return w


def _ffn_bwd(dh, dhb, sv, gain, wg_t, wu_t, wd, pre):
    lp = dh.shape[0]
    tm = _tm(lp)

    def dact_epi(accs, exs, row0):
        g = exs[0].astype(F32)
        u = exs[1].astype(F32)
        da = 0.5 * accs[0]
        sg, dsg = _silu_parts(g)
        return [da * u * dsg, da * sg]

    tn = _pick(DFF, (1408, 256, 128))
    dg_, du_ = _mm("ffn_dact", lp, DFF, [_term(dhb, wd, "nt", 0, (), pre)],
                   [((lp, DFF), BF16, "mn", 0, ())] * 2, dact_epi,
                   extras=[(sv["g"], "mn", 0, ()), (sv["u"], "mn", 0, ())], tm=tm, tn=tn, vmem=VMEM_BIG)
    d_wd = _dw("dw_down", sv["act"], dhb, DFF, D, 0.5)
    d_wg = _dw("dw_gate", dg_, sv["z"], DFF, D)
    d_wu = _dw("dw_up", du_, sv["z"], DFF, D)

    nk = _pick(DFF // 128, (2, 1))
    dh2, dhb2, dgain = _mm("ffn_dz", lp, D, [_term(dg_, wg_t, "nn", 0, (), pre), _term(du_, wu_t, "nn", 0, (), pre)],
                           _norm_bwd_outs(lp), _norm_bwd_epi,
                           extras=[(sv["h"], "mn", 0, ()), (dh, "mn", 0, ()), (gain, "n", 0, ())],
                           tm=tm, tn=D, nk=nk, vmem=VMEM_BIG)
    return dh2, dhb2, dgain, d_wg, d_wu, d_wd


def _gla_gates(hin, w2p, b2p):
    lp = hin.shape[0]
    tr = _row_tile(lp)
    bf, bb = P_OFF["lrf"] // HP, P_OFF["lrb"] // HP

    def body(lf_ref, lb_ref, w_ref, b_ref, o_ref, c_ref):
        i = pl.program_id(0)
        rows = i * tr + lax.broadcasted_iota(jnp.int32, (tr, 1), 0)
        r = lax.broadcasted_iota(jnp.int32, (tr, tr), 0)
        c = lax.broadcasted_iota(jnp.int32, (tr, tr), 1)
        same = (r // CHUNK) == (c // CHUNK)
        for d, l_ref in enumerate((lf_ref, lb_ref)):
            logit = _dot(l_ref[...], w_ref[d], precision=lax.Precision.HIGHEST) + b_ref[d]
            g = jnp.where(rows >= NULL, jax.nn.log_sigmoid(logit) * (1.0 / GLA_TAU), 0.0)
            o_ref[d] = g
            tmat = (same & ((r >= c) if d == 0 else (r <= c))).astype(F32)
            c_ref[d] = _dot(tmat, g, precision=lax.Precision.HIGHEST)

    spec = pl.BlockSpec((2, tr, 512), lambda i: (0, i, 0))
    return pl.pallas_call(
        body, name="gla_gates", grid=(lp // tr,),
        in_specs=[pl.BlockSpec((tr, HP), lambda i: (i, bf)), pl.BlockSpec((tr, HP), lambda i: (i, bb)),
                  pl.BlockSpec((2, HP, 512), lambda i: (0, 0, 0)), pl.BlockSpec((2, 1, 512), lambda i: (0, 0, 0))],
        out_specs=[spec, spec],
        out_shape=[jax.ShapeDtypeStruct((2, lp, 512), F32)] * 2,
    )(hin, hin, w2p, b2p)


def _gla_rows(lp):
    return _pick(lp, (384, 256, 128))


def _tri(d):
    r = lax.broadcasted_iota(jnp.int32, (CHUNK, CHUNK), 0)
    c = lax.broadcasted_iota(jnp.int32, (CHUNK, CHUNK), 1)
    return (r >= c) if d == 0 else (r <= c)


def _gla_fwd(hin, gates):
    lp = hin.shape[0]
    rb = _gla_rows(lp)
    nb = lp // rb
    cpb = rb // CHUNK
    nchunk = lp // CHUNK
    qo, ko, vo = P_OFF["qa"] // 512, P_OFF["ka"] // 512, P_OFF["va"] // 512
    scale = GLA_DK ** -0.5

    def body(qf, kf, vf, gf, qb, kb, vb_, gb, of, ob, sf, sb, st):
        @pl.when(pl.program_id(0) == 0)
        def _():
            st[...] = jnp.zeros_like(st)

        ins = ((qf, kf, vf, gf, of, sf), (qb, kb, vb_, gb, ob, sb))
        for ci in range(cpb):
            for d in range(2):
                q_ref, k_ref, v_ref, g_ref, o_ref, s_ref = ins[d]
                tri = _tri(d)
                c = ci if d == 0 else cpb - 1 - ci
                rows = slice(c * CHUNK, (c + 1) * CHUNK)
                for h in range(GLA_H):
                    sl = slice(h * HP, (h + 1) * HP)
                    q = q_ref[rows, sl] * scale
                    k = k_ref[rows, sl]
                    v = v_ref[rows, sl]
                    b = g_ref[rows, sl]
                    btot = b[CHUNK - 1:CHUNK] if d == 0 else b[0:1]
                    qd = (q * jnp.exp(b)).astype(BF16)
                    ki = (k * jnp.exp(-b)).astype(BF16)
                    ke = (k * jnp.exp(btot - b)).astype(BF16)
                    vb = v.astype(BF16)
                    att = jnp.where(tri, _dot(qd, ki, "nt"), 0.0)
                    s_prev = st[d, h]
                    o_ref[rows, sl] = _dot(att.astype(BF16), vb) + _dot(qd, s_prev.astype(BF16), "nt")
                    s_ref[h, c] = s_prev
                    st[d, h] = s_prev * jnp.exp(btot) + _dot(vb, ke, "tn")

    def specs(off):
        return (pl.BlockSpec((rb, 512), lambda b: (b, off)), pl.BlockSpec((rb, 512), lambda b: (nb - 1 - b, off)))

    (qf, qb), (kf, kb), (vf, vb2) = specs(qo), specs(ko), specs(vo)
    gf = pl.BlockSpec((None, rb, 512), lambda b: (0, b, 0))
    gb = pl.BlockSpec((None, rb, 512), lambda b: (1, nb - 1 - b, 0))
    of, ob = specs(0)
    sf = pl.BlockSpec((GLA_H, cpb, HP, HP), lambda b: (0, b, 0, 0))
    sb = pl.BlockSpec((GLA_H, cpb, HP, HP), lambda b: (0, nb - 1 - b, 0, 0))
    osh = jax.ShapeDtypeStruct((lp, GLA_H * HP), F32)
    ssh = jax.ShapeDtypeStruct((GLA_H, nchunk, HP, HP), F32)
    return pl.pallas_call(
        body, name="gla_fwd", grid=(nb,),
        in_specs=[qf, kf, vf, gf, qb, kb, vb2, gb], out_specs=[of, ob, sf, sb], out_shape=[osh, osh, ssh, ssh],
        scratch_shapes=[pltpu.VMEM((2, GLA_H, HP, HP), F32)], compiler_params=_cparams(VMEM_BIG),
    )(hin, hin, hin, gates, hin, hin, hin, gates)


def _gla_bwd(hin, gates, states, do):
    lp = hin.shape[0]
    rb = _gla_rows(lp)
    nb = lp // rb
    cpb = rb // CHUNK
    qo, ko, vo = P_OFF["qa"] // 512, P_OFF["ka"] // 512, P_OFF["va"] // 512
    scale = GLA_DK ** -0.5

    def body(qf, kf, vf, gf, sf, dof, qb, kb, vb_, gb, sb, dob,
             dqf, dkf, dvf, dgf, dqb, dkb, dvb, dgb, dst):
        @pl.when(pl.program_id(0) == 0)
        def _():
            dst[...] = jnp.zeros_like(dst)

        ins = ((qf, kf, vf, gf, sf, dof, dqf, dkf, dvf, dgf), (qb, kb, vb_, gb, sb, dob, dqb, dkb, dvb, dgb))
        for ci in range(cpb):
            for d in range(2):
                q_ref, k_ref, v_ref, g_ref, s_ref, do_ref, dq_ref, dk_ref, dv_ref, dg_ref = ins[d]
                tri = _tri(d)
                tmat = tri.astype(F32)
                c = cpb - 1 - ci if d == 0 else ci
                rows = slice(c * CHUNK, (c + 1) * CHUNK)
                for h in range(GLA_H):
                    sl = slice(h * HP, (h + 1) * HP)
                    q = q_ref[rows, sl] * scale
                    k = k_ref[rows, sl]
                    v = v_ref[rows, sl]
                    dout = do_ref[rows, sl].astype(BF16)
                    b = g_ref[rows, sl]
                    btot = b[CHUNK - 1:CHUNK] if d == 0 else b[0:1]
                    e = jnp.exp(b)
                    ei = jnp.exp(-b)
                    et = jnp.exp(btot - b)
                    etot = jnp.exp(btot)
                    qd = q * e
                    ki = k * ei
                    ke = k * et
                    qdb, kib, keb, vb = qd.astype(BF16), ki.astype(BF16), ke.astype(BF16), v.astype(BF16)
                    att = jnp.where(tri, _dot(qdb, kib, "nt"), 0.0).astype(BF16)
                    d_att = jnp.where(tri, _dot(dout, vb, "nt"), 0.0).astype(BF16)
                    s_prev = s_ref[h, c]
                    ds_t = dst[d, h]
                    ds_b = ds_t.astype(BF16)
                    dv = _dot(att, dout, "tn") + _dot(keb, ds_b, "nt")
                    d_qd = _dot(d_att, kib) + _dot(dout, s_prev.astype(BF16))
                    d_ki = _dot(d_att, qdb, "tn")
                    d_ke = _dot(vb, ds_b)
                    d_e = jnp.sum(s_prev * ds_t, axis=0, keepdims=True)
                    dst[d, h] = _dot(dout, qdb, "tn") + ds_t * etot
                    db = d_qd * qd - d_ki * ki - d_ke * ke
                    dbtot = jnp.sum(d_ke * ke, axis=0, keepdims=True) + d_e * etot
                    dq_ref[rows, sl] = d_qd * e * scale
                    dk_ref[rows, sl] = d_ki * ei + d_ke * et
                    dv_ref[rows, sl] = dv
                    dg_ref[rows, sl] = _dot(tmat, db, "tn", precision=lax.Precision.HIGHEST) + dbtot

    def fw(off):
        return pl.BlockSpec((rb, 512), lambda b: (nb - 1 - b, off))

    def bw(off):
        return pl.BlockSpec((rb, 512), lambda b: (b, off))

    gf = pl.BlockSpec((None, rb, 512), lambda b: (0, nb - 1 - b, 0))
    gb = pl.BlockSpec((None, rb, 512), lambda b: (1, b, 0))
    sf = pl.BlockSpec((GLA_H, cpb, HP, HP), lambda b: (0, nb - 1 - b, 0, 0))
    sb = pl.BlockSpec((GLA_H, cpb, HP, HP), lambda b: (0, b, 0, 0))
    osh = jax.ShapeDtypeStruct((lp, GLA_H * HP), F32)
    res = pl.pallas_call(
        body, name="gla_bwd", grid=(nb,),
        in_specs=[fw(qo), fw(ko), fw(vo), gf, sf, fw(0), bw(qo), bw(ko), bw(vo), gb, sb, bw(0)],
        out_specs=[fw(0)] * 4 + [bw(0)] * 4, out_shape=[osh] * 8,
        scratch_shapes=[pltpu.VMEM((2, GLA_H, HP, HP), F32)], compiler_params=_cparams(VMEM_BIG),
    )(hin, hin, hin, gates, states[0], do, hin, hin, hin, gates, states[1], do)
    return res[:4], res[4:]


def _gla_out_fwd(o2, hin, gn):
    lp = hin.shape[0]
    tr = _row_tile(lp)
    ro = P_OFF["ra"] // 512

    def body(of_ref, ob_ref, r_ref, gn_ref, a_ref):
        r = r_ref[...]
        sr, _ = _silu_parts(r)
        for h in range(GLA_H):
            sl = slice(h * HP, (h + 1) * HP)
            o = of_ref[:, sl] + ob_ref[:, sl]
            rs = lax.rsqrt(jnp.mean(o * o, axis=-1, keepdims=True) + EPS)
            a_ref[:, sl] = (o * rs * gn_ref[:, sl] * sr[:, sl]).astype(BF16)

    row = pl.BlockSpec((tr, 512), lambda i: (i, 0))
    return pl.pallas_call(
        body, name="gla_out_fwd", grid=(lp // tr,),
        in_specs=[row, row, pl.BlockSpec((tr, 512), lambda i: (i, ro)), pl.BlockSpec((1, 512), lambda i: (0, 0))],
        out_specs=row,
        out_shape=jax.ShapeDtypeStruct((lp, 512), BF16),
    )(o2[0], o2[1], hin, gn)


def _gla_out_bwd(da, o2, hin, gn):
    lp = hin.shape[0]
    tr = _row_tile(lp)
    ro = P_OFF["ra"] // 512

    def body(da_ref, of_ref, ob_ref, r_ref, gn_ref, do_ref, dr_ref, dgn_ref):
        i = pl.program_id(0)
        r = r_ref[...]
        sr, dsr = _silu_parts(r)
        da_v = da_ref[...]
        parts = []
        for h in range(GLA_H):
            sl = slice(h * HP, (h + 1) * HP)
            o = of_ref[:, sl] + ob_ref[:, sl]
            rs = lax.rsqrt(jnp.mean(o * o, axis=-1, keepdims=True) + EPS)
            oh = o * rs
            gn_h = gn_ref[:, sl]
            dah = da_v[:, sl]
            dr_ref[:, sl] = (dah * oh * gn_h * dsr[:, sl]).astype(BF16)
            t = dah * sr[:, sl]
            parts.append(jnp.sum(t * oh, axis=0, keepdims=True))
            doh = t * gn_h
            do_ref[:, sl] = rs * (doh - oh * jnp.mean(doh * oh, axis=-1, keepdims=True))
        part = jnp.concatenate(parts, axis=1)

        @pl.when(i == 0)
        def _():
            dgn_ref[...] = part

        @pl.when(i != 0)
        def _():
            dgn_ref[...] += part

    row = pl.BlockSpec((tr, 512), lambda i: (i, 0))
    return pl.pallas_call(
        body, name="gla_out_bwd", grid=(lp // tr,),
        in_specs=[row, row, row, pl.BlockSpec((tr, 512), lambda i: (i, ro)), pl.BlockSpec((1, 512), lambda i: (0, 0))],
        out_specs=[row, row, pl.BlockSpec((1, 512), lambda i: (0, 0))],
        out_shape=[jax.ShapeDtypeStruct((lp, 512), F32), jax.ShapeDtypeStruct((lp, 512), BF16),
                   jax.ShapeDtypeStruct((1, 512), F32)],
    )(da, o2[0], o2[1], hin, gn)


def _gla_in_bwd(gf, gb, gates, hin, w2p):
    lp = hin.shape[0]
    tr = _row_tile(lp)
    bf, bb = P_OFF["lrf"] // HP, P_OFF["lrb"] // HP

    def body(dqf_ref, dkf_ref, dvf_ref, dgf_ref, dqb_ref, dkb_ref, dvb_ref, dgb_ref, g_ref, lf_ref, lb_ref, w_ref,
             oq_ref, ok_ref, ov_ref, olr_ref, dw_ref, db_ref):
        i = pl.program_id(0)
        oq_ref[...] = (dqf_ref[...] + dqb_ref[...]).astype(BF16)
        ok_ref[...] = (dkf_ref[...] + dkb_ref[...]).astype(BF16)
        ov_ref[...] = (dvf_ref[...] + dvb_ref[...]).astype(BF16)
        rows = i * tr + lax.broadcasted_iota(jnp.int32, (tr, 1), 0)
        for d, (l_ref, dg_ref) in enumerate(((lf_ref, dgf_ref), (lb_ref, dgb_ref))):
            sig_neg = 1.0 - jnp.exp(GLA_TAU * g_ref[d])
            dlogit = jnp.where(rows >= NULL, dg_ref[...] * (1.0 / GLA_TAU) * sig_neg, 0.0)
            olr_ref[:, d * HP:(d + 1) * HP] = _dot(dlogit, w_ref[d], "nt", precision=lax.Precision.HIGHEST).astype(BF16)
            dw = _dot(l_ref[...], dlogit, "tn", precision=lax.Precision.HIGHEST)
            dbias = jnp.sum(dlogit, axis=0, keepdims=True)

            @pl.when(i == 0)
            def _():
                dw_ref[d] = dw
                db_ref[d] = dbias

            @pl.when(i != 0)
            def _():
                dw_ref[d] += dw
                db_ref[d] += dbias

    two = pl.BlockSpec((2, tr, 512), lambda i: (0, i, 0))
    row = pl.BlockSpec((tr, 512), lambda i: (i, 0))
    return pl.pallas_call(
        body, name="gla_in_bwd", grid=(lp // tr,),
        in_specs=[row] * 8 + [two, pl.BlockSpec((tr, HP), lambda i: (i, bf)),
                  pl.BlockSpec((tr, HP), lambda i: (i, bb)), pl.BlockSpec((2, HP, 512), lambda i: (0, 0, 0))],
        out_specs=[row, row, row, pl.BlockSpec((tr, 2 * HP), lambda i: (i, 0)),
                   pl.BlockSpec((2, HP, 512), lambda i: (0, 0, 0)), pl.BlockSpec((2, 1, 512), lambda i: (0, 0, 0))],
        out_shape=[jax.ShapeDtypeStruct((lp, 512), BF16)] * 3 + [
            jax.ShapeDtypeStruct((lp, 2 * HP), BF16), jax.ShapeDtypeStruct((2, HP, 512), F32),
            jax.ShapeDtypeStruct((2, 1, 512), F32)],
    )(*gf, *gb, gates, hin, hin, w2p)


def _rope_tables(lp):
    n_tok = lp - NULL - NMETA
    rows = n_tok // GRID_W
    row = jnp.repeat(jnp.arange(rows), GRID_W).astype(F32)
    col = jnp.tile(jnp.arange(GRID_W), rows).astype(F32)
    inv = ROPE_THETA ** (-jnp.arange(0, 32, 2, dtype=F32) / 32)
    ang = jnp.concatenate([row[:, None] * inv, col[:, None] * inv], axis=-1)
    ang = jnp.concatenate([jnp.zeros((NULL + NMETA, 32), F32), ang], axis=0)
    cos, sin = jnp.cos(ang), jnp.sin(ang)
    z16 = jnp.zeros((lp, 16), F32)
    z64 = jnp.zeros((lp, 64), F32)
    c = jnp.concatenate([cos[:, :16], cos[:, :16], cos[:, 16:], cos[:, 16:], z64], axis=1)
    a = jnp.concatenate([-sin[:, :16], z16, -sin[:, 16:], z16, z64], axis=1)
    b = jnp.concatenate([z16, sin[:, :16], z16, sin[:, 16:], z64], axis=1)
    return c, a, b


def _rope(x, c, a, b):
    return x * c + pltpu.roll(x, HP - 16, 1) * a + pltpu.roll(x, 16, 1) * b


def _rope_t(dx, c, a, b):
    return dx * c + pltpu.roll(dx * a, 16, 1) + pltpu.roll(dx * b, HP - 16, 1)


def _attn_prep(hin, gq, gk, tabs):
    lp = hin.shape[0]
    tr = _row_tile(lp)
    qo, ko, vo = P_OFF["qb"] // 1024, P_OFF["kb"] // 256, P_OFF["vb"] // 256

    def body(q_ref, k_ref, v_ref, gq_ref, gk_ref, c_ref, a_ref, b_ref, oq_ref, ok_ref, ov_ref):
        c, a, b = c_ref[...], a_ref[...], b_ref[...]
        for src, g_ref, dst, nh, sc in ((q_ref, gq_ref, oq_ref, ATT_H, Q_SCALE), (k_ref, gk_ref, ok_ref, ATT_KV, 1.0)):
            for h in range(nh):
                sl = slice(h * HP, (h + 1) * HP)
                x = src[:, sl]
                r = lax.rsqrt(jnp.sum(x * x, axis=-1, keepdims=True) * (1.0 / HEAD_DIM) + EPS)
                dst[:, sl] = (_rope(x * r * g_ref[...], c, a, b) * sc).astype(BF16)
        lane = lax.broadcasted_iota(jnp.int32, (1, ATT_KV * HP), 1)
        ov_ref[...] = jnp.where(lane % HP == HEAD_DIM, 1.0, v_ref[...]).astype(BF16)

    tab = pl.BlockSpec((tr, HP), lambda i: (i, 0))
    vec = pl.BlockSpec((1, HP), lambda i: (0, 0))
    return pl.pallas_call(
        body, name="attn_prep", grid=(lp // tr,),
        in_specs=[pl.BlockSpec((tr, 1024), lambda i: (i, qo)), pl.BlockSpec((tr, 256), lambda i: (i, ko)),
                  pl.BlockSpec((tr, 256), lambda i: (i, vo)), vec, vec, tab, tab, tab],
        out_specs=[pl.BlockSpec((tr, 1024), lambda i: (i, 0)), pl.BlockSpec((tr, 256), lambda i: (i, 0)),
                   pl.BlockSpec((tr, 256), lambda i: (i, 0))],
        out_shape=[jax.ShapeDtypeStruct((lp, 1024), BF16), jax.ShapeDtypeStruct((lp, 256), BF16),
                   jax.ShapeDtypeStruct((lp, 256), BF16)],
    )(hin, hin, hin, gq, gk, *tabs)


def _attn_prep_bwd(dqr, dkr, hin, gq, gk, tabs):
    lp = hin.shape[0]
    tr = _row_tile(lp)
    qo, ko = P_OFF["qb"] // 1024, P_OFF["kb"] // 256

    def body(dq_ref, dk_ref, q_ref, k_ref, gq_ref, gk_ref, c_ref, a_ref, b_ref, oq_ref, ok_ref, dgq_ref, dgk_ref):
        i = pl.program_id(0)
        c, a, b = c_ref[...], a_ref[...], b_ref[...]
        for src, dsrc, g_ref, dst, dg_ref, nh, sc in (
                (q_ref, dq_ref, gq_ref, oq_ref, dgq_ref, ATT_H, Q_SCALE),
                (k_ref, dk_ref, gk_ref, ok_ref, dgk_ref, ATT_KV, 1.0)):
            acc = jnp.zeros((1, HP), F32)
            for h in range(nh):
                sl = slice(h * HP, (h + 1) * HP)
                x = src[:, sl]
                r = lax.rsqrt(jnp.sum(x * x, axis=-1, keepdims=True) * (1.0 / HEAD_DIM) + EPS)
                xh = x * r
                dxn = _rope_t(dsrc[:, sl] * sc, c, a, b)
                acc = acc + jnp.sum(dxn * xh, axis=0, keepdims=True)
                dxh = dxn * g_ref[...]
                dx = r * (dxh - xh * (jnp.sum(dxh * xh, axis=-1, keepdims=True) * (1.0 / HEAD_DIM)))
                dst[:, sl] = dx.astype(BF16)

            @pl.when(i == 0)
            def _():
                dg_ref[...] = acc

            @pl.when(i != 0)
            def _():
                dg_ref[...] += acc

    tab = pl.BlockSpec((tr, HP), lambda i: (i, 0))
    vec = pl.BlockSpec((1, HP), lambda i: (0, 0))
    return pl.pallas_call(
        body, name="attn_prep_bwd", grid=(lp // tr,),
        in_specs=[pl.BlockSpec((tr, 1024), lambda i: (i, 0)), pl.BlockSpec((tr, 256), lambda i: (i, 0)),
                  pl.BlockSpec((tr, 1024), lambda i: (i, qo)), pl.BlockSpec((tr, 256), lambda i: (i, ko)),
                  vec, vec, tab, tab, tab],
        out_specs=[pl.BlockSpec((tr, 1024), lambda i: (i, 0)), pl.BlockSpec((tr, 256), lambda i: (i, 0)), vec, vec],
        out_shape=[jax.ShapeDtypeStruct((lp, 1024), BF16), jax.ShapeDtypeStruct((lp, 256), BF16),
                   jax.ShapeDtypeStruct((1, HP), F32), jax.ShapeDtypeStruct((1, HP), F32)],
    )(dqr, dkr, hin, hin, gq, gk, *tabs)


QB = 128
GH = 2
Q_SCALE = HEAD_DIM ** -0.5 * math.log2(math.e)
LN2 = math.log(2.0)


def _stack(ref, g0, n):
    return jnp.concatenate([ref[:, (g0 + g) * HP:(g0 + g + 1) * HP] for g in range(n)], axis=0)


def _attn_fwd(qr, kr, vb):
    lp = qr.shape[0]
    nq = lp // QB

    def body(q_ref, k_ref, v_ref, o_ref, lse_ref):
        qb = pl.program_id(1)
        keys = lax.broadcasted_iota(jnp.int32, (1, lp), 1)
        lane = lax.broadcasted_iota(jnp.int32, (1, HP), 1)
        rows = qb * QB + lax.broadcasted_iota(jnp.int32, (QB, 1), 0)
        for ch in range(ATT_G // GH):
            qs = _stack(q_ref, ch * GH, GH)
            s = _dot(qs, k_ref[...], "nt")
            s = jnp.where(keys >= NULL, s, -1e30)
            m = jnp.max(s, axis=-1, keepdims=True)
            p = jnp.exp2(s - m).astype(BF16)
            o_raw = _dot(p, v_ref[...])
            l = jnp.sum(jnp.where(lane == HEAD_DIM, o_raw, 0.0), axis=-1, keepdims=True)
            o = jnp.where(lane < HEAD_DIM, o_raw / l, 0.0)
            lse = m + jnp.log2(l)
            for g in range(GH):
                sl = slice((ch * GH + g) * HP, (ch * GH + g + 1) * HP)
                o_ref[:, sl] = jnp.where(rows >= NULL, o[g * QB:(g + 1) * QB], 0.0).astype(BF16)
                lse_ref[:, sl] = jnp.broadcast_to(lse[g * QB:(g + 1) * QB], (QB, HP))

    qspec = pl.BlockSpec((QB, ATT_G * HP), lambda kv, qb: (qb, kv))
    kspec = pl.BlockSpec((lp, HP), lambda kv, qb: (0, kv))
    return pl.pallas_call(
        body, name="attn_fwd", grid=(ATT_KV, nq),
        in_specs=[qspec, kspec, kspec], out_specs=[qspec, qspec],
        out_shape=[jax.ShapeDtypeStruct((lp, ATT_H * HP), BF16), jax.ShapeDtypeStruct((lp, ATT_H * HP), F32)],
        compiler_params=_cparams(VMEM_BIG),
    )(qr, kr, vb)


def _attn_bwd(qr, kr, vb, o, lse, do):
    lp = qr.shape[0]
    nq = lp // QB

    def body(q_ref, k_ref, v_ref, o_ref, lse_ref, do_ref, dq_ref, dk_ref, dv_ref):
        qb = pl.program_id(1)

        @pl.when(qb == 0)
        def _():
            dk_ref[...] = jnp.zeros_like(dk_ref)
            dv_ref[...] = jnp.zeros_like(dv_ref)

        keys = lax.broadcasted_iota(jnp.int32, (1, lp), 1)
        k = k_ref[...]
        dk_acc, dv_acc = None, None
        for ch in range(ATT_G // GH):
            g0 = ch * GH
            qs = _stack(q_ref, g0, GH)
            dos = _stack(do_ref, g0, GH)
            os_ = _stack(o_ref, g0, GH).astype(F32)
            lse_s = jnp.concatenate([lse_ref[:, (g0 + g) * HP:(g0 + g) * HP + 1] for g in range(GH)], axis=0)
            delta = jnp.sum(dos * os_, axis=-1, keepdims=True) * LN2
            s = _dot(qs, k, "nt")
            p = jnp.where(keys >= NULL, jnp.exp2(s - lse_s), 0.0)
            dob = dos.astype(BF16)
            dp = _dot((dos * LN2).astype(BF16), v_ref[...], "nt")
            ds = (p * (dp - delta)).astype(BF16)
            dq = _dot(ds, k)
            for g in range(GH):
                dq_ref[:, (g0 + g) * HP:(g0 + g + 1) * HP] = dq[g * QB:(g + 1) * QB]
            dv_c = _dot(p.astype(BF16), dob, "tn")
            dk_c = _dot(ds, qs, "tn")
            dv_acc = dv_c if dv_acc is None else dv_acc + dv_c
            dk_acc = dk_c if dk_acc is None else dk_acc + dk_c
        dv_ref[...] += dv_acc
        dk_ref[...] += dk_acc

    qspec = pl.BlockSpec((QB, ATT_G * HP), lambda kv, qb: (qb, kv))
    kspec = pl.BlockSpec((lp, HP), lambda kv, qb: (0, kv))
    return pl.pallas_call(
        body, name="attn_bwd", grid=(ATT_KV, nq),
        in_specs=[qspec, kspec, kspec, qspec, qspec, qspec], out_specs=[qspec, kspec, kspec],
        out_shape=[jax.ShapeDtypeStruct((lp, ATT_H * HP), F32), jax.ShapeDtypeStruct((lp, ATT_KV * HP), F32),
                   jax.ShapeDtypeStruct((lp, ATT_KV * HP), F32)],
        compiler_params=_cparams(VMEM_BIG),
    )(qr, kr, vb, o, lse, do)


def _mixer_fwd(h, z, wl, l, tabs, next_gain):
    lp = h.shape[0]
    tm = _tm(lp)

    def id_epi(accs, exs, row0):
        return [accs[0]]

    (hin,) = _mm("in_proj", lp, D_INP, [_term(z, wl["win_t"], "nt", 0, (), (l,))], [((lp, D_INP), F32, "mn", 0, ())],
                 id_epi, tm=tm, tn=D_INP // 2, vmem=VMEM_BIG)
    gates, cum = _gla_gates(hin, wl["w2p"][l], wl["b2p"][l])
    o_f, o_b, s_f, s_b = _gla_fwd(hin, cum)
    o2, states = (o_f, o_b), (s_f, s_b)
    a = _gla_out_fwd(o2, hin, wl["gn"][l])
    qr, kr, vb = _attn_prep(hin, wl["gq"][l], wl["gk"][l], tabs)
    b, lse = _attn_fwd(qr, kr, vb)

    def merge_epi(accs, exs, row0):
        pa, pb = accs
        ga, gb, bma, bmb = exs
        y = _sigmoid(ga + bma) * pa + _sigmoid(gb + bmb) * pb
        return [y, pa, pb]

    tn = 512
    y, pa, pb = _mm("merge", lp, D, [_term(a, wl["wpa_t"], "nt", 0, (), (l,)), _term(b, wl["wpb_t"], "nt", 1, (), (l,))],
                    [((lp, D), BF16, "mn", 0, ())] * 3, merge_epi,
                    extras=[(hin, "mn", P_OFF["ga"] // tn, ()), (hin, "mn", P_OFF["gb"] // tn, ()),
                            (wl["bm"], "n", 0, (l, 0)), (wl["bm"], "n", 0, (l, 1))],
                    tm=tm, tn=tn, i_outer=True, vmem=VMEM_BIG)

    h2, z2 = _mm("out_proj", lp, D, [_term(y, wl["wout"], "nn", 0, (), (l,))],
                 [((lp, D), F32, "mn", 0, ()), ((lp, D), BF16, "mn", 0, ())], _residual_norm_epi(1.0, True),
                 extras=[(h, "mn", 0, ()), (next_gain, "n", 0, ())], tm=tm, tn=D, i_outer=True, vmem=VMEM_BIG)
    sv = dict(h=h, z=z, hin=hin, gates=gates, cum=cum, o2=o2, states=states, a=a, qr=qr, kr=kr, vb=vb, b=b, lse=lse,
              y=y, pa=pa, pb=pb)
    return h2, z2, sv


def _mixer_bwd(dh, dhb, sv, gain, wl, l, tabs):
    lp = dh.shape[0]
    tm = _tm(lp)
    hin = sv["hin"]
    tn = 512

    def merge_bwd_epi(accs, exs, row0):
        dy = accs[0]
        ga, gb, pa, pb, bma, bmb = exs
        sa = _sigmoid(ga + bma)
        sb = _sigmoid(gb + bmb)
        dga = dy * pa.astype(F32) * sa * (1.0 - sa)
        dgb = dy * pb.astype(F32) * sb * (1.0 - sb)
        return [dy * sa, dy * sb, dga, dgb, jnp.sum(dga, axis=0, keepdims=True), jnp.sum(dgb, axis=0, keepdims=True)]

    big = ((lp, D), BF16, "mn", 0, ())
    vec = ((1, D), F32, "nsum", 0, ())
    dpa, dpb, dga, dgb, dbma, dbmb = _mm(
        "merge_bwd", lp, D, [_term(dhb, wl["wout"], "nt", 0, (), (l,))], [big, big, big, big, vec, vec], merge_bwd_epi,
        extras=[(hin, "mn", P_OFF["ga"] // tn, ()), (hin, "mn", P_OFF["gb"] // tn, ()), (sv["pa"], "mn", 0, ()),
                (sv["pb"], "mn", 0, ()), (wl["bm"], "n", 0, (l, 0)), (wl["bm"], "n", 0, (l, 1))],
        tm=tm, tn=tn, vmem=VMEM_BIG)
    d_wout = _dw("dw_out", sv["y"], dhb, D, D)
    d_wpa_t = _dw("dw_pa", dpa, sv["a"], D, 512)
    d_wpb_t = _dw("dw_pb", dpb, sv["b"], D, ATT_H * HP)

    def id_epi(accs, exs, row0):
        return [accs[0]]

    (da,) = _mm("d_a", lp, 512, [_term(dpa, wl["wpa_t"], "nn", 0, (), (l,))], [((lp, 512), F32, "mn", 0, ())], id_epi,
                tm=tm, tn=512, i_outer=True, vmem=VMEM_BIG)
    (db,) = _mm("d_b", lp, ATT_H * HP, [_term(dpb, wl["wpb_t"], "nn", 0, (), (l,))],
                [((lp, ATT_H * HP), F32, "mn", 0, ())], id_epi, tm=tm, tn=512, i_outer=True, vmem=VMEM_BIG)
    d_o, d_ra, d_gn = _gla_out_bwd(da, sv["o2"], hin, wl["gn"][l])
    g_fw, g_bw = _gla_bwd(hin, sv["cum"], sv["states"], d_o)
    d_qa, d_ka, d_va, d_lr, d_w2p, d_b2p = _gla_in_bwd(g_fw, g_bw, sv["gates"], hin, wl["w2p"][l])
    dqr, dkr, dvb = _attn_bwd(sv["qr"], sv["kr"], sv["vb"], sv["b"], sv["lse"], db)
    d_qb, d_kb, d_gq, d_gk = _attn_prep_bwd(dqr, dkr, hin, wl["gq"][l], wl["gk"][l], tabs)
    pieces = dict(qb=d_qb, ga=dga, gb=dgb, qa=d_qa, ka=d_ka, va=d_va, ra=d_ra, kb=d_kb, vb=dvb.astype(BF16), lr=d_lr)
    dhin = jnp.concatenate([pieces[n] for n in ("qb", "ga", "gb", "qa", "ka", "va", "ra", "kb", "vb", "lr")], axis=1)
    d_win_t = _dw("dw_in", dhin, sv["z"], D_INP, D)
    dh2, dhb2, dgain = _mm("in_proj_dz", lp, D, [_term(dhin, wl["win_t"], "nn", 0, (), (l,))], _norm_bwd_outs(lp),
                           _norm_bwd_epi, extras=[(sv["h"], "mn", 0, ()), (dh, "mn", 0, ()), (gain, "n", 0, ())],
                           tm=tm, tn=D, nk=2, vmem=VMEM_BIG)
    grads = dict(gain=dgain, wout=d_wout, wpa_t=d_wpa_t, wpb_t=d_wpb_t, win_t=d_win_t, gn=d_gn, w2p=d_w2p, b2p=d_b2p,
                 gq=d_gq, gk=d_gk, bma=dbma, bmb=dbmb)
    return dh2, dhb2, grads


def _mesh_pos():
    x, y, c = lax.axis_index("x"), lax.axis_index("y"), lax.axis_index("c")
    chips = [(1 - x, y), (x, 1 - y), (1 - x, 1 - y)]
    return x, y, c, chips


def _dev_index(x, y, c):
    return 4 * x + 2 * y + c


def _all_gather(name, shards, leads):
    nt = len(shards)

    def blk(ref, lead, idx):
        return ref.at[(slice(None),) * lead + (idx,)]

    def body(*refs):
        xs, outs = refs[:nt], refs[nt:2 * nt]
        send_sems, recv_sems, local_sems = refs[2 * nt:]
        x, y, c, chips = _mesh_pos()
        me, sibling = (x, y, c), (x, y, 1 - c)

        def copy(t, k, block, to, own=False):
            dst = blk(outs[t], leads[t], _dev_index(*block))
            return pltpu.make_async_remote_copy(
                src_ref=xs[t] if own else dst, dst_ref=dst, send_sem=send_sems.at[t, k], recv_sem=recv_sems.at[t, k],
                device_id=to, device_id_type=MESH)

        locals_ = [pltpu.make_async_copy(xs[t], blk(outs[t], leads[t], _dev_index(*me)), local_sems.at[t])
                   for t in range(nt)]
        for cp in locals_:
            cp.start()
        first = []
        for t in range(nt):
            first.append(copy(t, 0, me, sibling, own=True))
            first += [copy(t, 1 + j, me, (*chip, c), own=True) for j, chip in enumerate(chips)]
        for cp in first:
            cp.start()
        passed = []
        for j, chip in enumerate(chips):
            for t in range(nt):
                copy(t, 1 + j, (*chip, c), me).wait_recv()
                fw = copy(t, 4 + j, (*chip, c), sibling)
                fw.start()
                passed.append(fw)
        for t in range(nt):
            copy(t, 0, sibling, me).wait_recv()
        for j, chip in enumerate(chips):
            for t in range(nt):
                copy(t, 4 + j, (*chip, 1 - c), me).wait_recv()
        for cp in first + passed:
            cp.wait_send()
        for cp in locals_:
            cp.wait()

    out_shape = [jax.ShapeDtypeStruct(s.shape[:ld] + (NDEV,) + s.shape[ld:], s.dtype) for s, ld in zip(shards, leads)]
    return pl.pallas_call(
        body, name=name, in_specs=[ANY] * nt, out_specs=[ANY] * nt, out_shape=out_shape,
        scratch_shapes=[pltpu.SemaphoreType.DMA((nt, 7)), pltpu.SemaphoreType.DMA((nt, 7)),
                        pltpu.SemaphoreType.DMA((nt,))],
    )(*shards)


def _exchange_sibling(name, gs):
    nt = len(gs)

    def body(*refs):
        xs, outs = refs[:nt], refs[nt:2 * nt]
        send_sems, recv_sems = refs[2 * nt:]
        x, y, c, _ = _mesh_pos()
        sibling = (x, y, 1 - c)
        copies = []
        for t in range(nt):
            for ch in range(4):
                copies.append(pltpu.make_async_remote_copy(
                    src_ref=xs[t].at[2 * ch + (1 - c)], dst_ref=outs[t].at[ch],
                    send_sem=send_sems.at[t, ch], recv_sem=recv_sems.at[t, ch],
                    device_id=sibling, device_id_type=MESH))
        for cp in copies:
            cp.start()
        for cp in copies:
            cp.wait()

    out_shape = [jax.ShapeDtypeStruct((4,) + g.shape[1:], g.dtype) for g in gs]
    return pl.pallas_call(
        body, name=name, in_specs=[ANY] * nt, out_specs=[ANY] * nt, out_shape=out_shape,
        scratch_shapes=[pltpu.SemaphoreType.DMA((nt, 4)), pltpu.SemaphoreType.DMA((nt, 4))],
    )(*gs)


def _pair_sum(name, gs, recv):
    c = lax.axis_index("c")
    outs = []
    for t, (g, rv) in enumerate(zip(gs, recv)):
        _, r, cols = rv.shape

        def body(c_ref, g_ref, r_ref, o_ref):
            o_ref[...] = (g_ref[...].astype(F32) + r_ref[...].astype(F32)).astype(o_ref.dtype)

        outs.append(pl.pallas_call(
            body, name=f"{name}_{t}",
            grid_spec=pltpu.PrefetchScalarGridSpec(
                num_scalar_prefetch=1, grid=(4,),
                in_specs=[pl.BlockSpec((None, r, cols), lambda ch, cr: (2 * ch + cr[0], 0, 0)),
                          pl.BlockSpec((None, r, cols), lambda ch, cr: (ch, 0, 0))],
                out_specs=pl.BlockSpec((None, r, cols), lambda ch, cr: (ch, 0, 0))),
            out_shape=jax.ShapeDtypeStruct(rv.shape, rv.dtype),
        )(jnp.reshape(c, (1,)).astype(jnp.int32), g, rv))
    return outs


def _exchange_chips(name, ps):
    nt = len(ps)

    def body(*refs):
        xs, outs = refs[:nt], refs[nt:2 * nt]
        send_sems, recv_sems = refs[2 * nt:]
        x, y, c, chips = _mesh_pos()
        copies = []
        for t in range(nt):
            for j, (cx, cy) in enumerate(chips):
                copies.append(pltpu.make_async_remote_copy(
                    src_ref=xs[t].at[2 * cx + cy], dst_ref=outs[t].at[j],
                    send_sem=send_sems.at[t, j], recv_sem=recv_sems.at[t, j],
                    device_id=(cx, cy, c), device_id_type=MESH))
        for cp in copies:
            cp.start()
        for cp in copies:
            cp.wait()

    out_shape = [jax.ShapeDtypeStruct((3,) + p.shape[1:], p.dtype) for p in ps]
    return pl.pallas_call(
        body, name=name, in_specs=[ANY] * nt, out_specs=[ANY] * nt, out_shape=out_shape,
        scratch_shapes=[pltpu.SemaphoreType.DMA((nt, 3)), pltpu.SemaphoreType.DMA((nt, 3))],
    )(*ps)


def _final_sum(name, ps, recv):
    chip = 2 * lax.axis_index("x") + lax.axis_index("y")
    outs = []
    for t, (p, rv) in enumerate(zip(ps, recv)):
        _, r, cols = rv.shape

        def body(c_ref, p_ref, r0_ref, r1_ref, r2_ref, o_ref):
            o_ref[...] = ((p_ref[...].astype(F32) + r0_ref[...].astype(F32)) + r1_ref[...].astype(F32)) + r2_ref[...].astype(F32)

        outs.append(pl.pallas_call(
            body, name=f"{name}_{t}",
            grid_spec=pltpu.PrefetchScalarGridSpec(
                num_scalar_prefetch=1, grid=(1,),
                in_specs=[pl.BlockSpec((None, r, cols), lambda i, cr: (cr[0], 0, 0))] +
                         [pl.BlockSpec((None, r, cols), lambda i, cr, j=j: (j, 0, 0)) for j in range(3)],
                out_specs=pl.BlockSpec((r, cols), lambda i, cr: (0, 0))),
            out_shape=jax.ShapeDtypeStruct((r, cols), F32),
        )(jnp.reshape(chip, (1,)).astype(jnp.int32), p, rv, rv, rv))
    return outs


def _reduce_scatter(tag, gs):
    recv1 = _exchange_sibling(f"rs_sibling_{tag}", gs)
    ps = _pair_sum(f"rs_pair_{tag}", gs, recv1)
    recv2 = _exchange_chips(f"rs_chips_{tag}", ps)
    return _final_sum(f"rs_sum_{tag}", ps, recv2)


def _sum_gathered(g):
    _, r, cols = g.shape

    def body(g_ref, o_ref):
        acc = g_ref[0]
        for d in range(1, NDEV):
            acc = acc + g_ref[d]
        o_ref[...] = acc

    return pl.pallas_call(body, name="small_sum", out_shape=jax.ShapeDtypeStruct((r, cols), F32))(g)


HBM = pl.BlockSpec(memory_space=pltpu.HBM)
SEM = pl.BlockSpec(memory_space=pltpu.SEMAPHORE)
EFFECT = pltpu.SideEffectType.DATAFLOW_SIDE_EFFECTING
NREL = NDEV - 1


def _related(k):
    x, y, c = lax.axis_index("x"), lax.axis_index("y"), lax.axis_index("c")
    px = 1 - x if k & 4 else x
    py = 1 - y if k & 2 else y
    pc = 1 - c if k & 1 else c
    return (px, py, pc), _dev_index(px, py, pc)


def _in_hbm(a):
    return pltpu.with_memory_space_constraint(a, pltpu.HBM)


def _split_copies(xs, lands, send_sems, recv_sems, src_of, dst_of):
    copies = []
    for t in range(len(xs)):
        for k in range(1, NDEV):
            peer, peer_idx = _related(k)
            copies.append(pltpu.make_async_remote_copy(
                src_ref=src_of(xs[t], t, peer_idx), dst_ref=dst_of(lands[t], t, k, peer_idx),
                send_sem=send_sems.at[t * NREL + k - 1], recv_sem=recv_sems.at[t * NREL + k - 1],
                device_id=peer, device_id_type=MESH))
    return copies


def _exchange_start(name, xs, lands, src_of, dst_of, after):
    nt = len(xs)

    def body(*refs):
        x_refs, land_refs = refs[:nt], refs[nt:2 * nt]
        send_sems, recv_sems = refs[2 * nt + 1], refs[2 * nt + 2]
        token = refs[-1]
        for cp in _split_copies(x_refs, land_refs, send_sems, recv_sems, src_of, dst_of):
            cp.start()
        token[...] = jnp.zeros_like(token)

    res = pl.pallas_call(
        body, name=name,
        out_shape=(pltpu.SemaphoreType.DMA((nt * NREL,)), pltpu.SemaphoreType.DMA((nt * NREL,)),
                   *[pltpu.HBM(a.shape, a.dtype) for a in xs], *[pltpu.HBM(a.shape, a.dtype) for a in lands],
                   jax.ShapeDtypeStruct((8, 128), F32)),
        in_specs=[HBM] * (2 * nt) + [ANY],
        out_specs=(SEM, SEM, *[HBM] * (2 * nt), pl.BlockSpec(memory_space=pltpu.VMEM)),
        input_output_aliases={i: 2 + i for i in range(2 * nt)},
        compiler_params=pltpu.CompilerParams(has_side_effects=EFFECT),
    )(*[_in_hbm(a) for a in xs], *[_in_hbm(a) for a in lands], after)
    return res[0], res[1], res[2:2 + nt], res[2 + nt:2 + 2 * nt], res[-1]


def _exchange_wait(name, send_sems, recv_sems, xs, lands, src_of, dst_of, after):
    nt = len(xs)

    def body(*refs):
        x_refs, land_refs = refs[:nt], refs[nt:2 * nt]
        send_sems, recv_sems = refs[2 * nt], refs[2 * nt + 1]
        for cp in _split_copies(x_refs, land_refs, send_sems, recv_sems, src_of, dst_of):
            cp.wait_send()
            cp.wait_recv()

    res = pl.pallas_call(
        body, name=name,
        out_shape=(*[pltpu.HBM(a.shape, a.dtype) for a in xs], *[pltpu.HBM(a.shape, a.dtype) for a in lands]),
        in_specs=[HBM] * (2 * nt) + [SEM, SEM, ANY], out_specs=tuple([HBM] * (2 * nt)),
        input_output_aliases={i: i for i in range(2 * nt)},
        compiler_params=pltpu.CompilerParams(has_side_effects=EFFECT),
    )(*xs, *lands, send_sems, recv_sems, after)
    return res[:nt], res[nt:]


def _gather_start(name, shards, leads, after):
    def src_of(x_ref, t, peer_idx):
        return x_ref

    def dst_of(land_ref, t, k, peer_idx):
        me = _dev_index(lax.axis_index("x"), lax.axis_index("y"), lax.axis_index("c"))
        return land_ref.at[(slice(None),) * leads[t] + (me,)]

    lands = [lax.empty(s.shape[:ld] + (NDEV,) + s.shape[ld:], s.dtype) for s, ld in zip(shards, leads)]
    return _exchange_start(name, shards, lands, src_of, dst_of, after)


def _gather_wait(name, started, leads, after):
    send_sems, recv_sems, shards, lands, _ = started

    def src_of(x_ref, t, peer_idx):
        return x_ref

    def dst_of(land_ref, t, k, peer_idx):
        return land_ref.at[(slice(None),) * leads[t] + (peer_idx,)]

    shards, lands = _exchange_wait(name, send_sems, recv_sems, shards, lands, src_of, dst_of, after)
    me = _dev_index(lax.axis_index("x"), lax.axis_index("y"), lax.axis_index("c"))
    return [lax.dynamic_update_index_in_dim(g, s, me, ld) for g, s, ld in zip(lands, shards, leads)]


def _scatter_src(x_ref, t, peer_idx):
    return x_ref.at[peer_idx]


def _scatter_dst(land_ref, t, k, peer_idx):
    return land_ref.at[k - 1]


def _scatter_start(name, gs, after):
    lands = [lax.empty((NREL,) + g.shape[1:], g.dtype) for g in gs]
    return _exchange_start(name, gs, lands, _scatter_src, _scatter_dst, after)


def _scatter_wait(name, started, after):
    send_sems, recv_sems, gs, lands, _ = started
    gs, lands = _exchange_wait(name, send_sems, recv_sems, gs, lands, _scatter_src, _scatter_dst, after)
    me = _dev_index(lax.axis_index("x"), lax.axis_index("y"), lax.axis_index("c"))
    outs = []
    for t, (g, rv) in enumerate(zip(gs, lands)):
        _, r, cols = rv.shape

        def body(c_ref, own_ref, rv_ref, o_ref):
            acc = own_ref[...].astype(F32)
            for k in range(NREL):
                acc = acc + rv_ref[k].astype(F32)
            o_ref[...] = acc

        outs.append(pl.pallas_call(
            body, name=f"{name}_sum_{t}",
            grid_spec=pltpu.PrefetchScalarGridSpec(
                num_scalar_prefetch=1, grid=(1,),
                in_specs=[pl.BlockSpec((None, r, cols), lambda i, cr: (cr[0], 0, 0)),
                          pl.BlockSpec((NREL, r, cols), lambda i, cr: (0, 0, 0))],
                out_specs=pl.BlockSpec((r, cols), lambda i, cr: (0, 0))),
            out_shape=jax.ShapeDtypeStruct((r, cols), F32), compiler_params=_cparams(VMEM_BIG),
        )(jnp.reshape(me, (1,)).astype(jnp.int32), g, rv))
    return outs


def _adamw(w, g, m, v):
    shape = w.shape
    cols = shape[-1]
    rows = math.prod(shape[:-1]) if len(shape) > 1 else 1
    w2, g2, m2, v2 = (jnp.reshape(t, (rows, cols)) for t in (w, g, m, v))
    tr = _pick(rows, (512, 256, 128)) if rows * cols > 65536 else rows
    c1 = 1.0 / (1.0 - ADAM_B1 ** ADAM_STEP)
    c2 = 1.0 / (1.0 - ADAM_B2 ** ADAM_STEP)

    def body(w_ref, g_ref, m_ref, v_ref, d_ref, nm_ref, nv_ref):
        gv = g_ref[...]
        nm = ADAM_B1 * m_ref[...] + (1.0 - ADAM_B1) * gv
        nv = ADAM_B2 * v_ref[...] + (1.0 - ADAM_B2) * (gv * gv)
        d_ref[...] = -ADAM_LR * ((nm * c1) / (jnp.sqrt(nv * c2) + ADAM_EPS) + ADAM_WD * w_ref[...])
        nm_ref[...] = nm
        nv_ref[...] = nv

    spec = pl.BlockSpec((tr, cols), lambda i: (i, 0))
    osh = jax.ShapeDtypeStruct((rows, cols), F32)
    d, nm, nv = pl.pallas_call(
        body, name="adamw", grid=(rows // tr,), in_specs=[spec] * 4, out_specs=[spec] * 3, out_shape=[osh] * 3,
    )(w2, g2, m2, v2)
    return jnp.reshape(d, shape), jnp.reshape(nm, shape), jnp.reshape(nv, shape)


def _pad_heads(w, name):
    if name not in P_HEADS:
        return w
    nh, real = P_HEADS[name]
    w = jnp.reshape(w, w.shape[:-2] + (nh, real, w.shape[-1]))
    w = jnp.pad(w, [(0, 0)] * (w.ndim - 2) + [(0, HP - real), (0, 0)])
    return jnp.reshape(w, w.shape[:-3] + (nh * HP, w.shape[-1]))


def _unpad_heads(w, name):
    if name not in P_HEADS:
        return w
    nh, real = P_HEADS[name]
    w = jnp.reshape(w, w.shape[:-2] + (nh, HP, w.shape[-1]))[..., :real, :]
    return jnp.reshape(w, w.shape[:-3] + (nh * real, w.shape[-1]))


def _win_pad(win_t):
    segs, o = {}, 0
    for n, s in zip(IN_NAMES, IN_SIZES):
        segs[n] = win_t[..., o:o + s, :]
        o += s
    return jnp.concatenate([_pad_heads(segs[n], n) for n in P_ORDER], axis=-2)


def _win_unpad(win_p):
    segs = {n: _unpad_heads(win_p[..., P_OFF[n]:P_OFF[n] + P_WIDTH[n], :], n) for n in P_ORDER}
    return jnp.concatenate([segs[n] for n in IN_NAMES], axis=-2)


def _t(w):
    return jnp.swapaxes(w, -1, -2)


def _layer_weights(g_g, g_u, g_d, g_in, g_pa, g_pb, g_out, gains, w2, b2, bm, gn, gq, gk):
    w2p = jnp.pad(jnp.reshape(w2, (2, GLA_RANK, GLA_H, GLA_DK)), ((0, 0), (0, HP - GLA_RANK), (0, 0), (0, HP - GLA_DK)))
    b2p = jnp.pad(jnp.reshape(b2, (2, 1, GLA_H, GLA_DK)), ((0, 0), (0, 0), (0, 0), (0, HP - GLA_DK)))
    wpb_t = jnp.pad(jnp.reshape(g_pb, (D, ATT_H, HEAD_DIM)), ((0, 0), (0, 0), (0, HP - HEAD_DIM)))
    return dict(
        gains=jnp.reshape(gains, (1, 3, 1, D)),
        wg_t=jnp.reshape(g_g, (1, 2, DFF, D)), wu_t=jnp.reshape(g_u, (1, 2, DFF, D)), wd=jnp.reshape(g_d, (1, 2, DFF, D)),
        win_t=_win_pad(jnp.reshape(g_in, (1, D_IN, D))), wpa_t=jnp.reshape(g_pa, (1, D, 512)),
        wpb_t=jnp.reshape(wpb_t, (1, D, ATT_H * HP)), wout=jnp.reshape(g_out, (1, D, D)),
        w2p=jnp.reshape(w2p, (1, 2, HP, GLA_H * HP)), b2p=jnp.reshape(b2p, (1, 2, 1, GLA_H * HP)),
        bm=jnp.reshape(bm, (1, 2, 1, D)), gn=jnp.reshape(gn, (1, 1, GLA_H * HP)),
        gq=jnp.pad(jnp.reshape(gq, (1, 1, HEAD_DIM)), ((0, 0), (0, 0), (0, HP - HEAD_DIM))),
        gk=jnp.pad(jnp.reshape(gk, (1, 1, HEAD_DIM)), ((0, 0), (0, 0), (0, HP - HEAD_DIM))))


def _layer_fwd(h, z, w, tabs, next_gain):
    h, z, s0 = _ffn_fwd(h, z, w["wg_t"], w["wu_t"], w["wd"], (0, 0), w["gains"][0, 1])
    h, z, s1 = _mixer_fwd(h, z, w, 0, tabs, w["gains"][0, 2])
    h, z, s2 = _ffn_fwd(h, z, w["wg_t"], w["wu_t"], w["wd"], (0, 1), next_gain)
    return h, z, (s0, s1, s2)


def _layer_bwd(dh, dhb, saved, w, tabs):
    s0, s1, s2 = saved
    dh, dhb, dg2, dwg1, dwu1, dwd1 = _ffn_bwd(dh, dhb, s2, w["gains"][0, 2], w["wg_t"], w["wu_t"], w["wd"], (0, 1))
    dh, dhb, gm = _mixer_bwd(dh, dhb, s1, w["gains"][0, 1], w, 0, tabs)
    dh, dhb, dg0, dwg0, dwu0, dwd0 = _ffn_bwd(dh, dhb, s0, w["gains"][0, 0], w["wg_t"], w["wu_t"], w["wd"], (0, 0))
    gm.update(gain0=dg0, gain2=dg2, wg0=dwg0, wu0=dwu0, wd0=dwd0, wg1=dwg1, wu1=dwu1, wd1=dwd1)
    return dh, dhb, gm


def _big_grads(g):
    d_in = _win_unpad(g["win_t"])
    d_pb = jnp.reshape(jnp.reshape(g["wpb_t"], (D, ATT_H, HP))[:, :, :HEAD_DIM], (D, 512))
    ts = [g["wg0"], g["wu0"], g["wd0"], g["wg1"], g["wu1"], g["wd1"], d_in, g["wpa_t"], d_pb, g["wout"]]
    return [jnp.reshape(t, (NDEV, t.shape[0] // NDEV, t.shape[1])) for t in ts]


def kernel(x, meta_tokens, norm_gains, ffn_w_gate, ffn_w_up, ffn_w_down, w_in, gla_w2, gla_b2, gla_gn, q_norm, k_norm, w_pa, w_pb, b_merge, w_out, final_norm, loss_target, m_meta_tokens, m_norm_gains, m_ffn_w_gate, m_ffn_w_up, m_ffn_w_down, m_w_in, m_gla_w2, m_gla_b2, m_gla_gn, m_q_norm, m_k_norm, m_w_pa, m_w_pb, m_b_merge, m_w_out, m_final_norm, v_meta_tokens, v_norm_gains, v_ffn_w_gate, v_ffn_w_up, v_ffn_w_down, v_w_in, v_gla_w2, v_gla_b2, v_gla_gn, v_q_norm, v_k_norm, v_w_pa, v_w_pb, v_b_merge, v_w_out, v_final_norm):
    dev = _dev_index(lax.axis_index("x"), lax.axis_index("y"), lax.axis_index("c"))
    sh_g = _t(ffn_w_gate).astype(BF16)
    sh_u = _t(ffn_w_up).astype(BF16)
    sh_d = ffn_w_down.astype(BF16)
    sh_in = _t(w_in).astype(BF16)
    sh_pa = _t(w_pa).astype(BF16)
    sh_pb = _t(w_pb).astype(BF16)
    sh_out = w_out.astype(BF16)
    small = jnp.concatenate([jnp.reshape(t, (-1, 128)) for t in
                             (meta_tokens, norm_gains, gla_w2, gla_b2, b_merge)], axis=0)
    small = jnp.pad(small, ((0, 2), (0, 0)))
    def shards(l):
        return [sh_g[l], sh_u[l], sh_d[l], sh_in[l], sh_pa[l], sh_pb[l], sh_out[l]]

    w_leads = [1, 1, 1, 0, 0, 0, 0]
    *g0, g_small = _all_gather("gather_layer0", shards(0) + [small], w_leads + [0])
    started = {l: _gather_start(f"gather_start_{l}", shards(l), w_leads, g_small) for l in range(1, DEPTH)}
    tok = sum(started[l][4][0, 0] for l in range(1, DEPTH))
    meta_full = jnp.reshape(jnp.transpose(g_small[:, 0:16], (1, 0, 2)), (NMETA, D)) + tok
    gains_full = jnp.reshape(jnp.transpose(jnp.reshape(g_small[:, 16:28], (NDEV, DEPTH, 3, 128)), (1, 2, 0, 3)), (DEPTH, 3, D))
    w2_full = jnp.reshape(jnp.transpose(jnp.reshape(g_small[:, 28:60], (NDEV, DEPTH, 2, GLA_RANK, 32)), (1, 2, 3, 0, 4)),
                          (DEPTH, 2, GLA_RANK, 256))
    b2_full = jnp.reshape(jnp.transpose(jnp.reshape(g_small[:, 60:62], (NDEV, DEPTH, 2, 32)), (1, 2, 0, 3)), (DEPTH, 2, 256))
    bm_full = jnp.reshape(jnp.transpose(jnp.reshape(g_small[:, 62:70], (NDEV, DEPTH, 2, 128)), (1, 2, 0, 3)), (DEPTH, 2, D))

    def layer_weights(l, gathered, gains_l):
        return _layer_weights(*gathered, gains_l, w2_full[l], b2_full[l], bm_full[l], gla_gn[l], q_norm[l], k_norm[l])

    xl = x[0]
    lp = xl.shape[0] + NULL + NMETA
    tabs = _rope_tables(lp)
    h = jnp.concatenate([jnp.zeros((NULL, D), F32), meta_full, xl], axis=0)
    weights, saved = [], []
    z = _rmsnorm_fwd(h, jnp.reshape(gains_full[0, 0], (1, D)))
    for l in range(DEPTH):
        gathered = g0 if l == 0 else _gather_wait(f"gather_wait_{l}", started[l], w_leads, h)
        weights.append(layer_weights(l, gathered, gains_full[l]))
        next_gain = jnp.reshape(gains_full[l + 1, 0], (1, D)) if l + 1 < DEPTH else None
        h, z, sv = _layer_fwd(h, z, weights[l], tabs, next_gain)
        saved.append(sv)
    loss, dh, dhb, d_final = _loss_head(h, loss_target[0], jnp.reshape(final_norm, (1, D)))
    loss = lax.psum(loss[0, 0], ("x", "y", "c"))

    grads, scattering = [None] * DEPTH, {}
    tok = jnp.zeros((), F32)
    for l in reversed(range(DEPTH)):
        w = dict(weights[l], gains=weights[l]["gains"] + tok)
        dh, dhb, grads[l] = _layer_bwd(dh, dhb, saved[l], w, tabs)
        if l > 0:
            scattering[l] = _scatter_start(f"scatter_start_{l}", _big_grads(grads[l]), dhb)
            tok = scattering[l][4][0, 0]
    grad_x = dh[NULL + NMETA:][None]
    red = [None] * DEPTH
    for l in reversed(range(1, DEPTH)):
        red[l] = _scatter_wait(f"scatter_wait_{l}", scattering[l], dhb)
    red[0] = _reduce_scatter("0", _big_grads(grads[0]))
    g_gate = jnp.stack([jnp.stack([_t(red[l][0]), _t(red[l][3])]) for l in range(DEPTH)])
    g_up = jnp.stack([jnp.stack([_t(red[l][1]), _t(red[l][4])]) for l in range(DEPTH)])
    g_down = jnp.stack([jnp.stack([red[l][2], red[l][5]]) for l in range(DEPTH)])
    g_win = jnp.stack([_t(red[l][6]) for l in range(DEPTH)])
    g_wpa = jnp.stack([_t(red[l][7]) for l in range(DEPTH)])
    g_wpb = jnp.stack([_t(red[l][8]) for l in range(DEPTH)])
    g_wout = jnp.stack([red[l][9] for l in range(DEPTH)])

    d_meta = dh[NULL:NULL + NMETA]
    d_gains = jnp.stack([jnp.concatenate([grads[l]["gain0"], grads[l]["gain"], grads[l]["gain2"]], axis=0)
                         for l in range(DEPTH)])
    d_w2 = jnp.stack([jnp.reshape(jnp.reshape(grads[l]["w2p"], (2, HP, GLA_H, HP))[:, :GLA_RANK, :, :GLA_DK],
                                  (2, GLA_RANK, 256)) for l in range(DEPTH)])
    d_b2 = jnp.stack([jnp.reshape(jnp.reshape(grads[l]["b2p"], (2, GLA_H, HP))[:, :, :GLA_DK], (2, 256))
                      for l in range(DEPTH)])
    d_gn = jnp.stack([grads[l]["gn"][0] for l in range(DEPTH)])
    d_gq = jnp.stack([grads[l]["gq"][0, :HEAD_DIM] for l in range(DEPTH)])
    d_gk = jnp.stack([grads[l]["gk"][0, :HEAD_DIM] for l in range(DEPTH)])
    d_bm = jnp.stack([jnp.concatenate([grads[l]["bma"], grads[l]["bmb"]], axis=0) for l in range(DEPTH)])
    parts = [d_meta, d_gains, d_w2, d_b2, d_gn, d_gq, d_gk, d_bm, d_final[0]]
    sizes = [p.size for p in parts]
    flat = jnp.concatenate([jnp.reshape(p, (-1,)) for p in parts])
    flat = jnp.reshape(flat, (-1, 128))
    nrow = flat.shape[0]
    flat = jnp.pad(flat, ((0, (-nrow) % 8), (0, 0)))
    (g_flat,) = _all_gather("gather_small_grads", [flat], [0])
    tot = jnp.reshape(_sum_gathered(g_flat), (-1,))
    full, o = [], 0
    for p, s in zip(parts, sizes):
        full.append(jnp.reshape(tot[o:o + s], p.shape))
        o += s
    f_meta, f_gains, f_w2, f_b2, f_gn, f_gq, f_gk, f_bm, f_final = full

    def mine(t, width):
        return lax.dynamic_slice_in_dim(t, dev * width, width, axis=t.ndim - 1)

    g_small = dict(meta_tokens=mine(f_meta, 128), norm_gains=mine(f_gains, 128), gla_w2=mine(f_w2, 32),
                   gla_b2=mine(f_b2, 32), gla_gn=f_gn, q_norm=f_gq, k_norm=f_gk, b_merge=mine(f_bm, 128),
                   final_norm=f_final)
    gr = dict(g_small, ffn_w_gate=g_gate, ffn_w_up=g_up, ffn_w_down=g_down, w_in=g_win, w_pa=g_wpa, w_pb=g_wpb,
              w_out=g_wout)
    ws = dict(meta_tokens=meta_tokens, norm_gains=norm_gains, ffn_w_gate=ffn_w_gate, ffn_w_up=ffn_w_up,
              ffn_w_down=ffn_w_down, w_in=w_in, gla_w2=gla_w2, gla_b2=gla_b2, gla_gn=gla_gn, q_norm=q_norm,
              k_norm=k_norm, w_pa=w_pa, w_pb=w_pb, b_merge=b_merge, w_out=w_out, final_norm=final_norm)
    ms = dict(meta_tokens=m_meta_tokens, norm_gains=m_norm_gains, ffn_w_gate=m_ffn_w_gate, ffn_w_up=m_ffn_w_up,
              ffn_w_down=m_ffn_w_down, w_in=m_w_in, gla_w2=m_gla_w2, gla_b2=m_gla_b2, gla_gn=m_gla_gn, q_norm=m_q_norm,
              k_norm=m_k_norm, w_pa=m_w_pa, w_pb=m_w_pb, b_merge=m_b_merge, w_out=m_w_out, final_norm=m_final_norm)
    vs = dict(meta_tokens=v_meta_tokens, norm_gains=v_norm_gains, ffn_w_gate=v_ffn_w_gate, ffn_w_up=v_ffn_w_up,
              ffn_w_down=v_ffn_w_down, w_in=v_w_in, gla_w2=v_gla_w2, gla_b2=v_gla_b2, gla_gn=v_gla_gn, q_norm=v_q_norm,
              k_norm=v_k_norm, w_pa=v_w_pa, w_pb=v_w_pb, b_merge=v_b_merge, w_out=v_w_out, final_norm=v_final_norm)
    names = ["meta_tokens", "norm_gains", "ffn_w_gate", "ffn_w_up", "ffn_w_down", "w_in", "gla_w2", "gla_b2", "gla_gn",
             "q_norm", "k_norm", "w_pa", "w_pb", "b_merge", "w_out", "final_norm"]
    deltas, new_m, new_v = [], [], []
    for n in names:
        dlt, nm, nv = _adamw(ws[n], gr[n], ms[n], vs[n])
        deltas.append(dlt)
        new_m.append(nm)
        new_v.append(nv)
    return (loss, grad_x, *[gr[n] for n in names], *deltas, *new_m, *new_v)
```

```python
import functools
import math

import jax
import jax.numpy as jnp
from jax import lax
from jax.experimental import pallas as pl
from jax.experimental.pallas import tpu as pltpu

F32 = jnp.float32
BF16 = jnp.bfloat16
MESH = pl.DeviceIdType.MESH
ANY = pl.BlockSpec(memory_space=pl.ANY)

NDEV = 8
D = 1024
DFF = 2816
DEPTH = 4
NMETA = 16
NULL = 112
GRID_W = 64
EPS = 1e-6
HP = 128
GLA_H = 4
GLA_DK = 64
GLA_RANK = 16
GLA_TAU = 16.0
CHUNK = 64
ATT_H = 8
ATT_KV = 2
ATT_G = ATT_H // ATT_KV
HEAD_DIM = 64
ROPE_THETA = 10000.0

IN_SIZES = (256, 256, 512, 512, 16, 16, 512, 128, 128, 1024, 1024)
IN_NAMES = ("qa", "ka", "va", "ra", "lrf", "lrb", "qb", "kb", "vb", "ga", "gb")
D_IN = sum(IN_SIZES)
P_ORDER = ("qb", "ga", "gb", "qa", "ka", "va", "ra", "kb", "vb", "lrf", "lrb")
P_WIDTH = dict(qb=1024, ga=1024, gb=1024, qa=512, ka=512, va=512, ra=512, kb=256, vb=256, lrf=128, lrb=128)
P_OFF = {}
_o = 0
for _n in P_ORDER:
    P_OFF[_n] = _o
    _o += P_WIDTH[_n]
D_INP = _o
P_HEADS = dict(qa=(4, 64), ka=(4, 64), qb=(8, 64), kb=(2, 64), vb=(2, 64), lrf=(1, 16), lrb=(1, 16))

ADAM_LR = 0.001
ADAM_B1 = 0.9
ADAM_B2 = 0.999
ADAM_EPS = 1e-08
ADAM_WD = 0.01
ADAM_STEP = 10

VMEM_BIG = 56 * 1024 * 1024


def _cparams(vmem=None):
    return pltpu.CompilerParams(vmem_limit_bytes=vmem) if vmem else pltpu.CompilerParams()


def _pick(n, prefs):
    for p in prefs:
        if n % p == 0:
            return p
    return n


def _tm(lp):
    return _pick(lp, (528, 512, 256, 128))


_DN = {"nn": (((1,), (0,)), ((), ())), "nt": (((1,), (1,)), ((), ())), "tn": (((0,), (0,)), ((), ()))}


def _dot(a, b, mode="nn", precision=None):
    return lax.dot_general(a, b, _DN[mode], preferred_element_type=F32, precision=precision)


def _split(x):
    hi = x.astype(BF16)
    return hi, (x - hi.astype(F32)).astype(BF16)


def _dot_sel(t, x, mode="nn"):
    hi, lo = _split(x)
    return _dot(t, hi, mode) + _dot(t, lo, mode)


def _dot3(a, b, mode="nn"):
    ah, al = _split(a)
    bh, bl = _split(b)
    return _dot(ah, bh, mode) + (_dot(ah, bl, mode) + _dot(al, bh, mode))


def _sigmoid(x):
    return 0.5 * jnp.tanh(0.5 * x) + 0.5


def _mm(name, m, n, terms, outs, epilogue, extras=(), *, tm, tn, nk=1, i_outer=False, vmem=None):
    gm, gn = m // tm, n // tn
    assert gm * tm == m and gn * tn == n, (name, m, n, tm, tn)
    n_acc = 1 + max(t[3] for t in terms)

    def gmap(f):
        if i_outer:
            return lambda i, j, kk: f(i, j, kk)
        return lambda j, i, kk: f(i, j, kk)

    in_specs, args = [], []
    for a, b, mode, _, pa, pb in terms:
        kdim = a.shape[-2] if mode == "tn" else a.shape[-1]
        tk = kdim // nk
        assert tk * nk == kdim
        na, nb = (None,) * len(pa), (None,) * len(pb)
        if mode == "tn":
            in_specs.append(pl.BlockSpec(na + (tk, tm), gmap(lambda i, j, kk, pa=pa: pa + (kk, i))))
        else:
            in_specs.append(pl.BlockSpec(na + (tm, tk), gmap(lambda i, j, kk, pa=pa: pa + (i, kk))))
        if mode == "nt":
            in_specs.append(pl.BlockSpec(nb + (tn, tk), gmap(lambda i, j, kk, pb=pb: pb + (j, kk))))
        else:
            in_specs.append(pl.BlockSpec(nb + (tk, tn), gmap(lambda i, j, kk, pb=pb: pb + (kk, j))))
        args += [a, b]
    for arr, kind, off, pe in extras:
        ne = (None,) * len(pe)
        if kind == "mn":
            in_specs.append(pl.BlockSpec(ne + (tm, tn), gmap(lambda i, j, kk, off=off, pe=pe: pe + (i, j + off))))
        else:
            in_specs.append(pl.BlockSpec(ne + (1, tn), gmap(lambda i, j, kk, off=off, pe=pe: pe + (0, j + off))))
        args.append(arr)
    out_shape, out_specs = [], []
    for shape, dtype, kind, off, po in outs:
        no = (None,) * len(po)
        out_shape.append(jax.ShapeDtypeStruct(shape, dtype))
        if kind == "mn":
            out_specs.append(pl.BlockSpec(no + (tm, tn), gmap(lambda i, j, kk, off=off, po=po: po + (i, j + off))))
        else:
            assert not i_outer
            out_specs.append(pl.BlockSpec(no + (1, tn), gmap(lambda i, j, kk, off=off, po=po: po + (0, j + off))))
    n_t, n_e, n_o = len(terms), len(extras), len(outs)
    i_axis = 0 if i_outer else 1

    def body(*refs):
        ins = refs[: 2 * n_t]
        exs = refs[2 * n_t: 2 * n_t + n_e]
        ors = refs[2 * n_t + n_e: 2 * n_t + n_e + n_o]
        accs = refs[2 * n_t + n_e + n_o:]
        i = pl.program_id(i_axis)
        kk = pl.program_id(2)
        part = [None] * n_acc
        for t, (_, _, mode, ai, _, _) in enumerate(terms):
            p = _dot(ins[2 * t][...], ins[2 * t + 1][...], mode)
            part[ai] = p if part[ai] is None else part[ai] + p

        def finish(vals):
            res = epilogue(vals, [e[...] for e in exs], i * tm)
            for (_, dtype, kind, _, _), o_ref, v in zip(outs, ors, res):
                if kind == "mn":
                    o_ref[...] = v.astype(dtype)
                else:
                    @pl.when(i == 0)
                    def _():
                        o_ref[...] = v.astype(dtype)

                    @pl.when(i != 0)
                    def _():
                        o_ref[...] += v.astype(dtype)

        if nk == 1:
            finish(part)
        else:
            @pl.when(kk == 0)
            def _():
                for a_ref, p in zip(accs, part):
                    a_ref[...] = p

            @pl.when(kk != 0)
            def _():
                for a_ref, p in zip(accs, part):
                    a_ref[...] += p

            @pl.when(kk == nk - 1)
            def _():
                finish([a_ref[...] for a_ref in accs])

    scratch = [pltpu.VMEM((tm, tn), F32) for _ in range(n_acc)] if nk > 1 else []
    grid = (gm, gn, nk) if i_outer else (gn, gm, nk)
    res = pl.pallas_call(
        body, name=name, grid=grid, in_specs=in_specs, out_specs=out_specs, out_shape=out_shape,
        scratch_shapes=scratch, compiler_params=_cparams(vmem),
    )(*args)
    return res


def _term(a, b, mode, acc=0, pa=(), pb=()):
    return (a, b, mode, acc, tuple(pa), tuple(pb))


def _row_tile(lp):
    return _pick(lp, (384, 256, 128))


def _rmsnorm_fwd(h, gain):
    lp = h.shape[0]
    tr = _row_tile(lp)

    def body(h_ref, g_ref, z_ref):
        x = h_ref[...]
        r = lax.rsqrt(jnp.mean(x * x, axis=-1, keepdims=True) + EPS)
        z_ref[...] = (x * r * g_ref[...]).astype(BF16)

    return pl.pallas_call(
        body, name="rmsnorm_fwd", grid=(lp // tr,),
        in_specs=[pl.BlockSpec((tr, D), lambda i: (i, 0)), pl.BlockSpec((1, D), lambda i: (0, 0))],
        out_specs=pl.BlockSpec((tr, D), lambda i: (i, 0)),
        out_shape=jax.ShapeDtypeStruct((lp, D), BF16),
    )(h, gain)


def _loss_head(h, target, gain):
    lp = h.shape[0]
    tr = 128

    def body(h_ref, t_ref, g_ref, loss_ref, dh_ref, dhb_ref, dg_ref):
        i = pl.program_id(0)

        @pl.when(i == 0)
        def _():
            loss_ref[...] = jnp.zeros_like(loss_ref)
            dg_ref[...] = jnp.zeros_like(dg_ref)
            dh_ref[...] = jnp.zeros_like(dh_ref)
            dhb_ref[...] = jnp.zeros_like(dhb_ref)

        @pl.when(i != 0)
        def _():
            x = h_ref[...]
            g = g_ref[...]
            r = lax.rsqrt(jnp.mean(x * x, axis=-1, keepdims=True) + EPS)
            xh = x * r
            y = xh * g
            err = y - t_ref[...]
            loss_ref[...] += 0.5 * jnp.sum(jnp.sum(err * err, axis=-1, keepdims=True), axis=0, keepdims=True) / D
            dy = err * (1.0 / D)
            dg_ref[...] += jnp.sum(dy * xh, axis=0, keepdims=True)
            dxh = dy * g
            dx = r * (dxh - xh * jnp.mean(dxh * xh, axis=-1, keepdims=True))
            dh_ref[...] = dx
            dhb_ref[...] = dx.astype(BF16)

    row = pl.BlockSpec((tr, D), lambda i: (i, 0))
    vec = pl.BlockSpec((1, D), lambda i: (0, 0))
    return pl.pallas_call(
        body, name="loss_head", grid=(lp // tr,),
        in_specs=[row, pl.BlockSpec((tr, D), lambda i: (jnp.maximum(i - 1, 0), 0)), vec],
        out_specs=[pl.BlockSpec((1, 1), lambda i: (0, 0)), row, row, vec],
        out_shape=[jax.ShapeDtypeStruct((1, 1), F32), jax.ShapeDtypeStruct((lp, D), F32),
                   jax.ShapeDtypeStruct((lp, D), BF16), jax.ShapeDtypeStruct((1, D), F32)],
    )(h, target, gain)


def _silu_parts(g):
    s = _sigmoid(g)
    return g * s, s * (1.0 + g * (1.0 - s))


def _residual_norm_epi(scale, with_norm):
    def epi(accs, exs, row0):
        h2 = exs[0] + scale * accs[0]
        if not with_norm:
            return [h2]
        r = lax.rsqrt(jnp.mean(h2 * h2, axis=-1, keepdims=True) + EPS)
        return [h2, h2 * r * exs[1]]
    return epi


def _norm_bwd_epi(accs, exs, row0):
    dz = accs[0]
    x, res, g = exs
    r = lax.rsqrt(jnp.mean(x * x, axis=-1, keepdims=True) + EPS)
    xh = x * r
    dxh = dz * g
    dx = r * (dxh - xh * jnp.mean(dxh * xh, axis=-1, keepdims=True))
    rows = row0 + lax.broadcasted_iota(jnp.int32, (dz.shape[0], 1), 0)
    dh = jnp.where(rows >= NULL, res + dx, 0.0)
    return [dh, dh, jnp.sum(dz * xh, axis=0, keepdims=True)]


def _norm_bwd_outs(lp):
    return [((lp, D), F32, "mn", 0, ()), ((lp, D), BF16, "mn", 0, ()), ((1, D), F32, "nsum", 0, ())]


def _ffn_fwd(h, z, wg_t, wu_t, wd, pre, next_gain):
    lp = h.shape[0]
    tm = _tm(lp)

    def up_epi(accs, exs, row0):
        g, u = accs
        sg, _ = _silu_parts(g)
        return [g, u, sg * u]

    tn = _pick(DFF, (1408, 256, 128))
    bshape = (lp, DFF)
    g_, u_, act = _mm("ffn_up", lp, DFF, [_term(z, wg_t, "nt", 0, (), pre), _term(z, wu_t, "nt", 1, (), pre)],
                      [(bshape, BF16, "mn", 0, ())] * 3, up_epi, tm=tm, tn=tn, vmem=VMEM_BIG)

    nk = _pick(DFF // 128, (2, 1))
    with_norm = next_gain is not None
    res = _mm("ffn_down", lp, D, [_term(act, wd, "nn", 0, (), pre)],
              [((lp, D), F32, "mn", 0, ())] + ([((lp, D), BF16, "mn", 0, ())] if with_norm else []),
              _residual_norm_epi(0.5, with_norm),
              extras=[(h, "mn", 0, ())] + ([(next_gain, "n", 0, ())] if with_norm else []),
              tm=tm, tn=D, nk=nk, i_outer=True, vmem=VMEM_BIG)
    return res[0], (res[1] if with_norm else None), dict(h=h, z=z, g=g_, u=u_, act=act)


def _dw(name, a, b, m, n, scale=1.0):
    lp = a.shape[0]
    nk = lp // _pick(lp, (1056, 256, 128))
    tm = _pick(m, (2944, 1408, 1024, 512, 256, 128))
    tn = _pick(n, (1024, 512, 256, 128))

    def epi(accs, exs, row0):
        return [accs[0] * scale]

    (w,) = _mm(name, m, n, [_term(a, b, "tn")], [((m, n), BF16, "mn", 0, ())], epi, tm=tm, tn=tn, nk=nk,
               i_outer=True, vmem=VMEM_BIG)
    return w


def _ffn_bwd(dh, dhb, sv, gain, wg_t, wu_t, wd, pre):
    lp = dh.shape[0]
    tm = _tm(lp)

    def dact_epi(accs, exs, row0):
        g = exs[0].astype(F32)
        u = exs[1].astype(F32)
        da = 0.5 * accs[0]
        sg, dsg = _silu_parts(g)
        return [da * u * dsg, da * sg]

    tn = _pick(DFF, (1408, 256, 128))
    dg_, du_ = _mm("ffn_dact", lp, DFF, [_term(dhb, wd, "nt", 0, (), pre)],
                   [((lp, DFF), BF16, "mn", 0, ())] * 2, dact_epi,
                   extras=[(sv["g"], "mn", 0, ()), (sv["u"], "mn", 0, ())], tm=tm, tn=tn, vmem=VMEM_BIG)
    d_wd = _dw("dw_down", sv["act"], dhb, DFF, D, 0.5)
    d_wg = _dw("dw_gate", dg_, sv["z"], DFF, D)
    d_wu = _dw("dw_up", du_, sv["z"], DFF, D)

    nk = _pick(DFF // 128, (2, 1))
    dh2, dhb2, dgain = _mm("ffn_dz", lp, D, [_term(dg_, wg_t, "nn", 0, (), pre), _term(du_, wu_t, "nn", 0, (), pre)],
                           _norm_bwd_outs(lp), _norm_bwd_epi,
                           extras=[(sv["h"], "mn", 0, ()), (dh, "mn", 0, ()), (gain, "n", 0, ())],
                           tm=tm, tn=D, nk=nk, vmem=VMEM_BIG)
    return dh2, dhb2, dgain, d_wg, d_wu, d_wd


def _gla_gates(hin, w2p, b2p):
    lp = hin.shape[0]
    tr = _row_tile(lp)
    bf, bb = P_OFF["lrf"] // HP, P_OFF["lrb"] // HP

    def body(lf_ref, lb_ref, w_ref, b_ref, o_ref, c_ref):
        i = pl.program_id(0)
        rows = i * tr + lax.broadcasted_iota(jnp.int32, (tr, 1), 0)
        r = lax.broadcasted_iota(jnp.int32, (tr, tr), 0)
        c = lax.broadcasted_iota(jnp.int32, (tr, tr), 1)
        same = (r // CHUNK) == (c // CHUNK)
        for d, l_ref in enumerate((lf_ref, lb_ref)):
            logit = _dot3(l_ref[...], w_ref[d]) + b_ref[d]
            g = jnp.where(rows >= NULL, jax.nn.log_sigmoid(logit) * (1.0 / GLA_TAU), 0.0)
            o_ref[d] = g
            tmat = jnp.where(same & ((r >= c) if d == 0 else (r <= c)), 1.0, 0.0).astype(BF16)
            c_ref[d] = _dot_sel(tmat, g)

    spec = pl.BlockSpec((2, tr, 512), lambda i: (0, i, 0))
    return pl.pallas_call(
        body, name="gla_gates", grid=(lp // tr,),
        in_specs=[pl.BlockSpec((tr, HP), lambda i: (i, bf)), pl.BlockSpec((tr, HP), lambda i: (i, bb)),
                  pl.BlockSpec((2, HP, 512), lambda i: (0, 0, 0)), pl.BlockSpec((2, 1, 512), lambda i: (0, 0, 0))],
        out_specs=[spec, spec],
        out_shape=[jax.ShapeDtypeStruct((2, lp, 512), F32)] * 2,
    )(hin, hin, w2p, b2p)


def _gla_rows(lp):
    return _pick(lp, (384, 256, 128))


def _tri(d):
    r = lax.broadcasted_iota(jnp.int32, (CHUNK, CHUNK), 0)
    c = lax.broadcasted_iota(jnp.int32, (CHUNK, CHUNK), 1)
    return (r >= c) if d == 0 else (r <= c)


def _gla_fwd(hin, gates):
    lp = hin.shape[0]
    rb = _gla_rows(lp)
    nb = lp // rb
    cpb = rb // CHUNK
    nchunk = lp // CHUNK
    qo, ko, vo = P_OFF["qa"] // 512, P_OFF["ka"] // 512, P_OFF["va"] // 512
    scale = GLA_DK ** -0.5

    def body(qf, kf, vf, gf, qb, kb, vb_, gb, of, ob, sf, sb, st):
        @pl.when(pl.program_id(0) == 0)
        def _():
            st[...] = jnp.zeros_like(st)

        ins = ((qf, kf, vf, gf, of, sf), (qb, kb, vb_, gb, ob, sb))
        for ci in range(cpb):
            for d in range(2):
                q_ref, k_ref, v_ref, g_ref, o_ref, s_ref = ins[d]
                tri = _tri(d)
                c = ci if d == 0 else cpb - 1 - ci
                rows = slice(c * CHUNK, (c + 1) * CHUNK)
                for h in range(GLA_H):
                    sl = slice(h * HP, (h + 1) * HP)
                    q = q_ref[rows, sl] * scale
                    k = k_ref[rows, sl]
                    v = v_ref[rows, sl]
                    b = g_ref[rows, sl]
                    btot = b[CHUNK - 1:CHUNK] if d == 0 else b[0:1]
                    qd = (q * jnp.exp(b)).astype(BF16)
                    ki = (k * jnp.exp(-b)).astype(BF16)
                    ke = (k * jnp.exp(btot - b)).astype(BF16)
                    vb = v.astype(BF16)
                    att = jnp.where(tri, _dot(qd, ki, "nt"), 0.0)
                    s_prev = st[d, h]
                    o_ref[rows, sl] = _dot(att.astype(BF16), vb) + _dot(qd, s_prev.astype(BF16), "nt")
                    s_ref[h, c] = s_prev
                    st[d, h] = s_prev * jnp.exp(btot) + _dot(vb, ke, "tn")

    def specs(off):
        return (pl.BlockSpec((rb, 512), lambda b: (b, off)), pl.BlockSpec((rb, 512), lambda b: (nb - 1 - b, off)))

    (qf, qb), (kf, kb), (vf, vb2) = specs(qo), specs(ko), specs(vo)
    gf = pl.BlockSpec((None, rb, 512), lambda b: (0, b, 0))
    gb = pl.BlockSpec((None, rb, 512), lambda b: (1, nb - 1 - b, 0))
    of, ob = specs(0)
    sf = pl.BlockSpec((GLA_H, cpb, HP, HP), lambda b: (0, b, 0, 0))
    sb = pl.BlockSpec((GLA_H, cpb, HP, HP), lambda b: (0, nb - 1 - b, 0, 0))
    osh = jax.ShapeDtypeStruct((lp, GLA_H * HP), F32)
    ssh = jax.ShapeDtypeStruct((GLA_H, nchunk, HP, HP), F32)
    return pl.pallas_call(
        body, name="gla_fwd", grid=(nb,),
        in_specs=[qf, kf, vf, gf, qb, kb, vb2, gb], out_specs=[of, ob, sf, sb], out_shape=[osh, osh, ssh, ssh],
        scratch_shapes=[pltpu.VMEM((2, GLA_H, HP, HP), F32)], compiler_params=_cparams(VMEM_BIG),
    )(hin, hin, hin, gates, hin, hin, hin, gates)


def _gla_bwd(hin, gates, states, do):
    lp = hin.shape[0]
    rb = _gla_rows(lp)
    nb = lp // rb
    cpb = rb // CHUNK
    qo, ko, vo = P_OFF["qa"] // 512, P_OFF["ka"] // 512, P_OFF["va"] // 512
    scale = GLA_DK ** -0.5

    def body(qf, kf, vf, gf, sf, dof, qb, kb, vb_, gb, sb, dob,
             dqf, dkf, dvf, dgf, dqb, dkb, dvb, dgb, dst):
        @pl.when(pl.program_id(0) == 0)
        def _():
            dst[...] = jnp.zeros_like(dst)

        ins = ((qf, kf, vf, gf, sf, dof, dqf, dkf, dvf, dgf), (qb, kb, vb_, gb, sb, dob, dqb, dkb, dvb, dgb))
        for ci in range(cpb):
            for d in range(2):
                q_ref, k_ref, v_ref, g_ref, s_ref, do_ref, dq_ref, dk_ref, dv_ref, dg_ref = ins[d]
                tri = _tri(d)
                edge = lax.broadcasted_iota(jnp.int32, (CHUNK, 1), 0) == (CHUNK - 1 if d == 0 else 0)
                c = cpb - 1 - ci if d == 0 else ci
                rows = slice(c * CHUNK, (c + 1) * CHUNK)
                for h in range(GLA_H):
                    sl = slice(h * HP, (h + 1) * HP)
                    q = q_ref[rows, sl] * scale
                    k = k_ref[rows, sl]
                    v = v_ref[rows, sl]
                    dout = do_ref[rows, sl].astype(BF16)
                    b = g_ref[rows, sl]
                    btot = b[CHUNK - 1:CHUNK] if d == 0 else b[0:1]
                    e = jnp.exp(b)
                    ei = jnp.exp(-b)
                    et = jnp.exp(btot - b)
                    etot = jnp.exp(btot)
                    qd = q * e
                    ki = k * ei
                    ke = k * et
                    qdb, kib, keb, vb = qd.astype(BF16), ki.astype(BF16), ke.astype(BF16), v.astype(BF16)
                    att = jnp.where(tri, _dot(qdb, kib, "nt"), 0.0).astype(BF16)
                    d_att = jnp.where(tri, _dot(dout, vb, "nt"), 0.0).astype(BF16)
                    s_prev = s_ref[h, c]
                    ds_t = dst[d, h]
                    ds_b = ds_t.astype(BF16)
                    dv = _dot(att, dout, "tn") + _dot(keb, ds_b, "nt")
                    d_qd = _dot(d_att, kib) + _dot(dout, s_prev.astype(BF16))
                    d_ki = _dot(d_att, qdb, "tn")
                    d_ke = _dot(vb, ds_b)
                    d_e = jnp.sum(s_prev * ds_t, axis=0, keepdims=True)
                    dst[d, h] = _dot(dout, qdb, "tn") + ds_t * etot
                    db = d_qd * qd - d_ki * ki - d_ke * ke
                    dbtot = jnp.sum(d_ke * ke, axis=0, keepdims=True) + d_e * etot
                    dq_ref[rows, sl] = d_qd * e * scale
                    dk_ref[rows, sl] = d_ki * ei + d_ke * et
                    dv_ref[rows, sl] = dv
                    dg_ref[rows, sl] = db + jnp.where(edge, dbtot, 0.0)

    def fw(off):
        return pl.BlockSpec((rb, 512), lambda b: (nb - 1 - b, off))

    def bw(off):
        return pl.BlockSpec((rb, 512), lambda b: (b, off))

    gf = pl.BlockSpec((None, rb, 512), lambda b: (0, nb - 1 - b, 0))
    gb = pl.BlockSpec((None, rb, 512), lambda b: (1, b, 0))
    sf = pl.BlockSpec((GLA_H, cpb, HP, HP), lambda b: (0, nb - 1 - b, 0, 0))
    sb = pl.BlockSpec((GLA_H, cpb, HP, HP), lambda b: (0, b, 0, 0))
    osh = jax.ShapeDtypeStruct((lp, GLA_H * HP), F32)
    res = pl.pallas_call(
        body, name="gla_bwd", grid=(nb,),
        in_specs=[fw(qo), fw(ko), fw(vo), gf, sf, fw(0), bw(qo), bw(ko), bw(vo), gb, sb, bw(0)],
        out_specs=[fw(0)] * 4 + [bw(0)] * 4, out_shape=[osh] * 8,
        scratch_shapes=[pltpu.VMEM((2, GLA_H, HP, HP), F32)], compiler_params=_cparams(VMEM_BIG),
    )(hin, hin, hin, gates, states[0], do, hin, hin, hin, gates, states[1], do)
    return res[:4], res[4:]


def _gla_out_fwd(o2, hin, gn):
    lp = hin.shape[0]
    tr = _row_tile(lp)
    ro = P_OFF["ra"] // 512

    def body(of_ref, ob_ref, r_ref, gn_ref, a_ref):
        r = r_ref[...]
        sr, _ = _silu_parts(r)
        for h in range(GLA_H):
            sl = slice(h * HP, (h + 1) * HP)
            o = of_ref[:, sl] + ob_ref[:, sl]
            rs = lax.rsqrt(jnp.mean(o * o, axis=-1, keepdims=True) + EPS)
            a_ref[:, sl] = (o * rs * gn_ref[:, sl] * sr[:, sl]).astype(BF16)

    row = pl.BlockSpec((tr, 512), lambda i: (i, 0))
    return pl.pallas_call(
        body, name="gla_out_fwd", grid=(lp // tr,),
        in_specs=[row, row, pl.BlockSpec((tr, 512), lambda i: (i, ro)), pl.BlockSpec((1, 512), lambda i: (0, 0))],
        out_specs=row,
        out_shape=jax.ShapeDtypeStruct((lp, 512), BF16),
    )(o2[0], o2[1], hin, gn)


def _gla_out_bwd(da, o2, hin, gn):
    lp = hin.shape[0]
    tr = _row_tile(lp)
    ro = P_OFF["ra"] // 512

    def body(da_ref, of_ref, ob_ref, r_ref, gn_ref, do_ref, dr_ref, dgn_ref):
        i = pl.program_id(0)
        r = r_ref[...]
        sr, dsr = _silu_parts(r)
        da_v = da_ref[...]
        parts = []
        for h in range(GLA_H):
            sl = slice(h * HP, (h + 1) * HP)
            o = of_ref[:, sl] + ob_ref[:, sl]
            rs = lax.rsqrt(jnp.mean(o * o, axis=-1, keepdims=True) + EPS)
            oh = o * rs
            gn_h = gn_ref[:, sl]
            dah = da_v[:, sl]
            dr_ref[:, sl] = (dah * oh * gn_h * dsr[:, sl]).astype(BF16)
            t = dah * sr[:, sl]
            parts.append(jnp.sum(t * oh, axis=0, keepdims=True))
            doh = t * gn_h
            do_ref[:, sl] = rs * (doh - oh * jnp.mean(doh * oh, axis=-1, keepdims=True))
        part = jnp.concatenate(parts, axis=1)

        @pl.when(i == 0)
        def _():
            dgn_ref[...] = part

        @pl.when(i != 0)
        def _():
            dgn_ref[...] += part

    row = pl.BlockSpec((tr, 512), lambda i: (i, 0))
    return pl.pallas_call(
        body, name="gla_out_bwd", grid=(lp // tr,),
        in_specs=[row, row, row, pl.BlockSpec((tr, 512), lambda i: (i, ro)), pl.BlockSpec((1, 512), lambda i: (0, 0))],
        out_specs=[row, row, pl.BlockSpec((1, 512), lambda i: (0, 0))],
        out_shape=[jax.ShapeDtypeStruct((lp, 512), F32), jax.ShapeDtypeStruct((lp, 512), BF16),
                   jax.ShapeDtypeStruct((1, 512), F32)],
    )(da, o2[0], o2[1], hin, gn)


def _gla_in_bwd(gf, gb, gates, hin, w2p):
    lp = hin.shape[0]
    tr = _row_tile(lp)
    bf, bb = P_OFF["lrf"] // HP, P_OFF["lrb"] // HP

    def body(dqf_ref, dkf_ref, dvf_ref, dgf_ref, dqb_ref, dkb_ref, dvb_ref, dgb_ref, g_ref, lf_ref, lb_ref, w_ref,
             oq_ref, ok_ref, ov_ref, olr_ref, dw_ref, db_ref):
        i = pl.program_id(0)
        oq_ref[...] = (dqf_ref[...] + dqb_ref[...]).astype(BF16)
        ok_ref[...] = (dkf_ref[...] + dkb_ref[...]).astype(BF16)
        ov_ref[...] = (dvf_ref[...] + dvb_ref[...]).astype(BF16)
        rows = i * tr + lax.broadcasted_iota(jnp.int32, (tr, 1), 0)
        r = lax.broadcasted_iota(jnp.int32, (tr, tr), 0)
        c = lax.broadcasted_iota(jnp.int32, (tr, tr), 1)
        same = (r // CHUNK) == (c // CHUNK)
        for d, (l_ref, dg_ref) in enumerate(((lf_ref, dgf_ref), (lb_ref, dgb_ref))):
            tmat = jnp.where(same & ((r <= c) if d == 0 else (r >= c)), 1.0, 0.0).astype(BF16)
            dg = _dot_sel(tmat, dg_ref[...])
            sig_neg = 1.0 - jnp.exp(GLA_TAU * g_ref[d])
            dlogit = jnp.where(rows >= NULL, dg * (1.0 / GLA_TAU) * sig_neg, 0.0)
            olr_ref[:, d * HP:(d + 1) * HP] = _dot3(dlogit, w_ref[d], "nt").astype(BF16)
            dw = _dot3(l_ref[...], dlogit, "tn")
            dbias = jnp.sum(dlogit, axis=0, keepdims=True)

            @pl.when(i == 0)
            def _():
                dw_ref[d] = dw
                db_ref[d] = dbias

            @pl.when(i != 0)
            def _():
                dw_ref[d] += dw
                db_ref[d] += dbias

    two = pl.BlockSpec((2, tr, 512), lambda i: (0, i, 0))
    row = pl.BlockSpec((tr, 512), lambda i: (i, 0))
    return pl.pallas_call(
        body, name="gla_in_bwd", grid=(lp // tr,),
        in_specs=[row] * 8 + [two, pl.BlockSpec((tr, HP), lambda i: (i, bf)),
                  pl.BlockSpec((tr, HP), lambda i: (i, bb)), pl.BlockSpec((2, HP, 512), lambda i: (0, 0, 0))],
        out_specs=[row, row, row, pl.BlockSpec((tr, 2 * HP), lambda i: (i, 0)),
                   pl.BlockSpec((2, HP, 512), lambda i: (0, 0, 0)), pl.BlockSpec((2, 1, 512), lambda i: (0, 0, 0))],
        out_shape=[jax.ShapeDtypeStruct((lp, 512), BF16)] * 3 + [
            jax.ShapeDtypeStruct((lp, 2 * HP), BF16), jax.ShapeDtypeStruct((2, HP, 512), F32),
            jax.ShapeDtypeStruct((2, 1, 512), F32)],
    )(*gf, *gb, gates, hin, hin, w2p)


def _rope_tables(lp):
    n_tok = lp - NULL - NMETA
    rows = n_tok // GRID_W
    row = jnp.repeat(jnp.arange(rows), GRID_W).astype(F32)
    col = jnp.tile(jnp.arange(GRID_W), rows).astype(F32)
    inv = ROPE_THETA ** (-jnp.arange(0, 32, 2, dtype=F32) / 32)
    ang = jnp.concatenate([row[:, None] * inv, col[:, None] * inv], axis=-1)
    ang = jnp.concatenate([jnp.zeros((NULL + NMETA, 32), F32), ang], axis=0)
    cos, sin = jnp.cos(ang), jnp.sin(ang)
    z16 = jnp.zeros((lp, 16), F32)
    z64 = jnp.zeros((lp, 64), F32)
    c = jnp.concatenate([cos[:, :16], cos[:, :16], cos[:, 16:], cos[:, 16:], z64], axis=1)
    a = jnp.concatenate([-sin[:, :16], z16, -sin[:, 16:], z16, z64], axis=1)
    b = jnp.concatenate([z16, sin[:, :16], z16, sin[:, 16:], z64], axis=1)
    return c, a, b


def _rope(x, c, a, b):
    return x * c + pltpu.roll(x, HP - 16, 1) * a + pltpu.roll(x, 16, 1) * b


def _rope_t(dx, c, a, b):
    return dx * c + pltpu.roll(dx * a, 16, 1) + pltpu.roll(dx * b, HP - 16, 1)


def _attn_prep(hin, gq, gk, tabs):
    lp = hin.shape[0]
    tr = _row_tile(lp)
    qo, ko, vo = P_OFF["qb"] // 1024, P_OFF["kb"] // 256, P_OFF["vb"] // 256

    def body(q_ref, k_ref, v_ref, gq_ref, gk_ref, c_ref, a_ref, b_ref, oq_ref, ok_ref, ov_ref):
        c, a, b = c_ref[...], a_ref[...], b_ref[...]
        for src, g_ref, dst, nh, sc in ((q_ref, gq_ref, oq_ref, ATT_H, Q_SCALE), (k_ref, gk_ref, ok_ref, ATT_KV, 1.0)):
            for h in range(nh):
                sl = slice(h * HP, (h + 1) * HP)
                x = src[:, sl]
                r = lax.rsqrt(jnp.sum(x * x, axis=-1, keepdims=True) * (1.0 / HEAD_DIM) + EPS)
                dst[:, sl] = (_rope(x * r * g_ref[...], c, a, b) * sc).astype(BF16)
        lane = lax.broadcasted_iota(jnp.int32, (1, ATT_KV * HP), 1)
        ov_ref[...] = jnp.where(lane % HP == HEAD_DIM, 1.0, v_ref[...]).astype(BF16)

    tab = pl.BlockSpec((tr, HP), lambda i: (i, 0))
    vec = pl.BlockSpec((1, HP), lambda i: (0, 0))
    return pl.pallas_call(
        body, name="attn_prep", grid=(lp // tr,),
        in_specs=[pl.BlockSpec((tr, 1024), lambda i: (i, qo)), pl.BlockSpec((tr, 256), lambda i: (i, ko)),
                  pl.BlockSpec((tr, 256), lambda i: (i, vo)), vec, vec, tab, tab, tab],
        out_specs=[pl.BlockSpec((tr, 1024), lambda i: (i, 0)), pl.BlockSpec((tr, 256), lambda i: (i, 0)),
                   pl.BlockSpec((tr, 256), lambda i: (i, 0))],
        out_shape=[jax.ShapeDtypeStruct((lp, 1024), BF16), jax.ShapeDtypeStruct((lp, 256), BF16),
                   jax.ShapeDtypeStruct((lp, 256), BF16)],
    )(hin, hin, hin, gq, gk, *tabs)


def _attn_prep_bwd(dqr, dkr, hin, gq, gk, tabs):
    lp = hin.shape[0]
    tr = _row_tile(lp)
    qo, ko = P_OFF["qb"] // 1024, P_OFF["kb"] // 256

    def body(dq_ref, dk_ref, q_ref, k_ref, gq_ref, gk_ref, c_ref, a_ref, b_ref, oq_ref, ok_ref, dgq_ref, dgk_ref):
        i = pl.program_id(0)
        c, a, b = c_ref[...], a_ref[...], b_ref[...]
        for src, dsrc, g_ref, dst, dg_ref, nh, sc in (
                (q_ref, dq_ref, gq_ref, oq_ref, dgq_ref, ATT_H, Q_SCALE),
                (k_ref, dk_ref, gk_ref, ok_ref, dgk_ref, ATT_KV, 1.0)):
            acc = jnp.zeros((1, HP), F32)
            for h in range(nh):
                sl = slice(h * HP, (h + 1) * HP)
                x = src[:, sl]
                r = lax.rsqrt(jnp.sum(x * x, axis=-1, keepdims=True) * (1.0 / HEAD_DIM) + EPS)
                xh = x * r
                dxn = _rope_t(dsrc[:, sl] * sc, c, a, b)
                acc = acc + jnp.sum(dxn * xh, axis=0, keepdims=True)
                dxh = dxn * g_ref[...]
                dx = r * (dxh - xh * (jnp.sum(dxh * xh, axis=-1, keepdims=True) * (1.0 / HEAD_DIM)))
                dst[:, sl] = dx.astype(BF16)

            @pl.when(i == 0)
            def _():
                dg_ref[...] = acc

            @pl.when(i != 0)
            def _():
                dg_ref[...] += acc

    tab = pl.BlockSpec((tr, HP), lambda i: (i, 0))
    vec = pl.BlockSpec((1, HP), lambda i: (0, 0))
    return pl.pallas_call(
        body, name="attn_prep_bwd", grid=(lp // tr,),
        in_specs=[pl.BlockSpec((tr, 1024), lambda i: (i, 0)), pl.BlockSpec((tr, 256), lambda i: (i, 0)),
                  pl.BlockSpec((tr, 1024), lambda i: (i, qo)), pl.BlockSpec((tr, 256), lambda i: (i, ko)),
                  vec, vec, tab, tab, tab],
        out_specs=[pl.BlockSpec((tr, 1024), lambda i: (i, 0)), pl.BlockSpec((tr, 256), lambda i: (i, 0)), vec, vec],
        out_shape=[jax.ShapeDtypeStruct((lp, 1024), BF16), jax.ShapeDtypeStruct((lp, 256), BF16),
                   jax.ShapeDtypeStruct((1, HP), F32), jax.ShapeDtypeStruct((1, HP), F32)],
    )(dqr, dkr, hin, hin, gq, gk, *tabs)


QB = 128
GH = 2
Q_SCALE = HEAD_DIM ** -0.5 * math.log2(math.e)
LN2 = math.log(2.0)


def _stack(ref, g0, n):
    return jnp.concatenate([ref[:, (g0 + g) * HP:(g0 + g + 1) * HP] for g in range(n)], axis=0)


def _attn_fwd(qr, kr, vb):
    lp = qr.shape[0]
    nq = lp // QB

    def body(q_ref, k_ref, v_ref, o_ref, lse_ref):
        qb = pl.program_id(1)
        keys = lax.broadcasted_iota(jnp.int32, (1, lp), 1)
        lane = lax.broadcasted_iota(jnp.int32, (1, HP), 1)
        rows = qb * QB + lax.broadcasted_iota(jnp.int32, (QB, 1), 0)
        for ch in range(ATT_G // GH):
            qs = _stack(q_ref, ch * GH, GH)
            s = _dot(qs, k_ref[...], "nt")
            s = jnp.where(keys >= NULL, s, -1e30)
            m = jnp.max(s, axis=-1, keepdims=True)
            p = jnp.exp2(s - m).astype(BF16)
            o_raw = _dot(p, v_ref[...])
            l = jnp.sum(jnp.where(lane == HEAD_DIM, o_raw, 0.0), axis=-1, keepdims=True)
            o = jnp.where(lane < HEAD_DIM, o_raw / l, 0.0)
            lse = m + jnp.log2(l)
            for g in range(GH):
                sl = slice((ch * GH + g) * HP, (ch * GH + g + 1) * HP)
                o_ref[:, sl] = jnp.where(rows >= NULL, o[g * QB:(g + 1) * QB], 0.0).astype(BF16)
                lse_ref[:, sl] = jnp.broadcast_to(lse[g * QB:(g + 1) * QB], (QB, HP))

    qspec = pl.BlockSpec((QB, ATT_G * HP), lambda kv, qb: (qb, kv))
    kspec = pl.BlockSpec((lp, HP), lambda kv, qb: (0, kv))
    return pl.pallas_call(
        body, name="attn_fwd", grid=(ATT_KV, nq),
        in_specs=[qspec, kspec, kspec], out_specs=[qspec, qspec],
        out_shape=[jax.ShapeDtypeStruct((lp, ATT_H * HP), BF16), jax.ShapeDtypeStruct((lp, ATT_H * HP), F32)],
        compiler_params=_cparams(VMEM_BIG),
    )(qr, kr, vb)


def _attn_bwd(qr, kr, vb, o, lse, do):
    lp = qr.shape[0]
    nq = lp // QB

    def body(q_ref, k_ref, v_ref, o_ref, lse_ref, do_ref, dq_ref, dk_ref, dv_ref):
        qb = pl.program_id(1)

        @pl.when(qb == 0)
        def _():
            dk_ref[...] = jnp.zeros_like(dk_ref)
            dv_ref[...] = jnp.zeros_like(dv_ref)

        keys = lax.broadcasted_iota(jnp.int32, (1, lp), 1)
        k = k_ref[...]
        dk_acc, dv_acc = None, None
        for ch in range(ATT_G // GH):
            g0 = ch * GH
            qs = _stack(q_ref, g0, GH)
            dos = _stack(do_ref, g0, GH)
            os_ = _stack(o_ref, g0, GH).astype(F32)
            lse_s = jnp.concatenate([lse_ref[:, (g0 + g) * HP:(g0 + g) * HP + 1] for g in range(GH)], axis=0)
            delta = jnp.sum(dos * os_, axis=-1, keepdims=True) * LN2
            s = _dot(qs, k, "nt")
            p = jnp.where(keys >= NULL, jnp.exp2(s - lse_s), 0.0)
            dob = dos.astype(BF16)
            dp = _dot((dos * LN2).astype(BF16), v_ref[...], "nt")
            ds = (p * (dp - delta)).astype(BF16)
            dq = _dot(ds, k)
            for g in range(GH):
                dq_ref[:, (g0 + g) * HP:(g0 + g + 1) * HP] = dq[g * QB:(g + 1) * QB]
            dv_c = _dot(p.astype(BF16), dob, "tn")
            dk_c = _dot(ds, qs, "tn")
            dv_acc = dv_c if dv_acc is None else dv_acc + dv_c
            dk_acc = dk_c if dk_acc is None else dk_acc + dk_c
        dv_ref[...] += dv_acc
        dk_ref[...] += dk_acc

    qspec = pl.BlockSpec((QB, ATT_G * HP), lambda kv, qb: (qb, kv))
    kspec = pl.BlockSpec((lp, HP), lambda kv, qb: (0, kv))
    return pl.pallas_call(
        body, name="attn_bwd", grid=(ATT_KV, nq),
        in_specs=[qspec, kspec, kspec, qspec, qspec, qspec], out_specs=[qspec, kspec, kspec],
        out_shape=[jax.ShapeDtypeStruct((lp, ATT_H * HP), F32), jax.ShapeDtypeStruct((lp, ATT_KV * HP), F32),
                   jax.ShapeDtypeStruct((lp, ATT_KV * HP), F32)],
        compiler_params=_cparams(VMEM_BIG),
    )(qr, kr, vb, o, lse, do)


def _mixer_fwd(h, z, wl, l, tabs, next_gain):
    lp = h.shape[0]
    tm = _tm(lp)

    def id_epi(accs, exs, row0):
        return [accs[0]]

    (hin,) = _mm("in_proj", lp, D_INP, [_term(z, wl["win_t"], "nt", 0, (), (l,))], [((lp, D_INP), F32, "mn", 0, ())],
                 id_epi, tm=tm, tn=D_INP // 2, vmem=VMEM_BIG)
    gates, cum = _gla_gates(hin, wl["w2p"][l], wl["b2p"][l])
    o_f, o_b, s_f, s_b = _gla_fwd(hin, cum)
    o2, states = (o_f, o_b), (s_f, s_b)
    a = _gla_out_fwd(o2, hin, wl["gn"][l])
    qr, kr, vb = _attn_prep(hin, wl["gq"][l], wl["gk"][l], tabs)
    b, lse = _attn_fwd(qr, kr, vb)

    def merge_epi(accs, exs, row0):
        pa, pb = accs
        ga, gb, bma, bmb = exs
        y = _sigmoid(ga + bma) * pa + _sigmoid(gb + bmb) * pb
        return [y, pa, pb]

    tn = 512
    y, pa, pb = _mm("merge", lp, D, [_term(a, wl["wpa_t"], "nt", 0, (), (l,)), _term(b, wl["wpb_t"], "nt", 1, (), (l,))],
                    [((lp, D), BF16, "mn", 0, ())] * 3, merge_epi,
                    extras=[(hin, "mn", P_OFF["ga"] // tn, ()), (hin, "mn", P_OFF["gb"] // tn, ()),
                            (wl["bm"], "n", 0, (l, 0)), (wl["bm"], "n", 0, (l, 1))],
                    tm=tm, tn=tn, i_outer=True, vmem=VMEM_BIG)

    h2, z2 = _mm("out_proj", lp, D, [_term(y, wl["wout"], "nn", 0, (), (l,))],
                 [((lp, D), F32, "mn", 0, ()), ((lp, D), BF16, "mn", 0, ())], _residual_norm_epi(1.0, True),
                 extras=[(h, "mn", 0, ()), (next_gain, "n", 0, ())], tm=tm, tn=D, i_outer=True, vmem=VMEM_BIG)
    sv = dict(h=h, z=z, hin=hin, gates=gates, cum=cum, o2=o2, states=states, a=a, qr=qr, kr=kr, vb=vb, b=b, lse=lse,
              y=y, pa=pa, pb=pb)
    return h2, z2, sv


def _mixer_bwd(dh, dhb, sv, gain, wl, l, tabs):
    lp = dh.shape[0]
    tm = _tm(lp)
    hin = sv["hin"]
    tn = 512

    def merge_bwd_epi(accs, exs, row0):
        dy = accs[0]
        ga, gb, pa, pb, bma, bmb = exs
        sa = _sigmoid(ga + bma)
        sb = _sigmoid(gb + bmb)
        dga = dy * pa.astype(F32) * sa * (1.0 - sa)
        dgb = dy * pb.astype(F32) * sb * (1.0 - sb)
        return [dy * sa, dy * sb, dga, dgb, jnp.sum(dga, axis=0, keepdims=True), jnp.sum(dgb, axis=0, keepdims=True)]

    big = ((lp, D), BF16, "mn", 0, ())
    vec = ((1, D), F32, "nsum", 0, ())
    dpa, dpb, dga, dgb, dbma, dbmb = _mm(
        "merge_bwd", lp, D, [_term(dhb, wl["wout"], "nt", 0, (), (l,))], [big, big, big, big, vec, vec], merge_bwd_epi,
        extras=[(hin, "mn", P_OFF["ga"] // tn, ()), (hin, "mn", P_OFF["gb"] // tn, ()), (sv["pa"], "mn", 0, ()),
                (sv["pb"], "mn", 0, ()), (wl["bm"], "n", 0, (l, 0)), (wl["bm"], "n", 0, (l, 1))],
        tm=tm, tn=tn, vmem=VMEM_BIG)
    d_wout = _dw("dw_out", sv["y"], dhb, D, D)
    d_wpa_t = _dw("dw_pa", dpa, sv["a"], D, 512)
    d_wpb_t = _dw("dw_pb", dpb, sv["b"], D, ATT_H * HP)

    def id_epi(accs, exs, row0):
        return [accs[0]]

    (da,) = _mm("d_a", lp, 512, [_term(dpa, wl["wpa_t"], "nn", 0, (), (l,))], [((lp, 512), F32, "mn", 0, ())], id_epi,
                tm=tm, tn=512, i_outer=True, vmem=VMEM_BIG)
    (db,) = _mm("d_b", lp, ATT_H * HP, [_term(dpb, wl["wpb_t"], "nn", 0, (), (l,))],
                [((lp, ATT_H * HP), F32, "mn", 0, ())], id_epi, tm=tm, tn=512, i_outer=True, vmem=VMEM_BIG)
    d_o, d_ra, d_gn = _gla_out_bwd(da, sv["o2"], hin, wl["gn"][l])
    g_fw, g_bw = _gla_bwd(hin, sv["cum"], sv["states"], d_o)
    d_qa, d_ka, d_va, d_lr, d_w2p, d_b2p = _gla_in_bwd(g_fw, g_bw, sv["gates"], hin, wl["w2p"][l])
    dqr, dkr, dvb = _attn_bwd(sv["qr"], sv["kr"], sv["vb"], sv["b"], sv["lse"], db)
    d_qb, d_kb, d_gq, d_gk = _attn_prep_bwd(dqr, dkr, hin, wl["gq"][l], wl["gk"][l], tabs)
    pieces = dict(qb=d_qb, ga=dga, gb=dgb, qa=d_qa, ka=d_ka, va=d_va, ra=d_ra, kb=d_kb, vb=dvb.astype(BF16), lr=d_lr)
    dhin = jnp.concatenate([pieces[n] for n in ("qb", "ga", "gb", "qa", "ka", "va", "ra", "kb", "vb", "lr")], axis=1)
    d_win_t = _dw("dw_in", dhin, sv["z"], D_INP, D)
    dh2, dhb2, dgain = _mm("in_proj_dz", lp, D, [_term(dhin, wl["win_t"], "nn", 0, (), (l,))], _norm_bwd_outs(lp),
                           _norm_bwd_epi, extras=[(sv["h"], "mn", 0, ()), (dh, "mn", 0, ()), (gain, "n", 0, ())],
                           tm=tm, tn=D, nk=2, vmem=VMEM_BIG)
    grads = dict(gain=dgain, wout=d_wout, wpa_t=d_wpa_t, wpb_t=d_wpb_t, win_t=d_win_t, gn=d_gn, w2p=d_w2p, b2p=d_b2p,
                 gq=d_gq, gk=d_gk, bma=dbma, bmb=dbmb)
    return dh2, dhb2, grads


def _mesh_pos():
    x, y, c = lax.axis_index("x"), lax.axis_index("y"), lax.axis_index("c")
    chips = [(1 - x, y), (x, 1 - y), (1 - x, 1 - y)]
    return x, y, c, chips


def _dev_index(x, y, c):
    return 4 * x + 2 * y + c


def _all_gather(name, shards, leads):
    nt = len(shards)

    def blk(ref, lead, idx):
        return ref.at[(slice(None),) * lead + (idx,)]

    def body(*refs):
        xs, outs = refs[:nt], refs[nt:2 * nt]
        send_sems, recv_sems, local_sems = refs[2 * nt:]
        x, y, c, chips = _mesh_pos()
        me, sibling = (x, y, c), (x, y, 1 - c)

        def copy(t, k, block, to, own=False):
            dst = blk(outs[t], leads[t], _dev_index(*block))
            return pltpu.make_async_remote_copy(
                src_ref=xs[t] if own else dst, dst_ref=dst, send_sem=send_sems.at[t, k], recv_sem=recv_sems.at[t, k],
                device_id=to, device_id_type=MESH)

        locals_ = [pltpu.make_async_copy(xs[t], blk(outs[t], leads[t], _dev_index(*me)), local_sems.at[t])
                   for t in range(nt)]
        for cp in locals_:
            cp.start()
        first = []
        for t in range(nt):
            first.append(copy(t, 0, me, sibling, own=True))
            first += [copy(t, 1 + j, me, (*chip, c), own=True) for j, chip in enumerate(chips)]
        for cp in first:
            cp.start()
        passed = []
        for j, chip in enumerate(chips):
            for t in range(nt):
                copy(t, 1 + j, (*chip, c), me).wait_recv()
                fw = copy(t, 4 + j, (*chip, c), sibling)
                fw.start()
                passed.append(fw)
        for t in range(nt):
            copy(t, 0, sibling, me).wait_recv()
        for j, chip in enumerate(chips):
            for t in range(nt):
                copy(t, 4 + j, (*chip, 1 - c), me).wait_recv()
        for cp in first + passed:
            cp.wait_send()
        for cp in locals_:
            cp.wait()

    out_shape = [jax.ShapeDtypeStruct(s.shape[:ld] + (NDEV,) + s.shape[ld:], s.dtype) for s, ld in zip(shards, leads)]
    return pl.pallas_call(
        body, name=name, in_specs=[ANY] * nt, out_specs=[ANY] * nt, out_shape=out_shape,
        scratch_shapes=[pltpu.SemaphoreType.DMA((nt, 7)), pltpu.SemaphoreType.DMA((nt, 7)),
                        pltpu.SemaphoreType.DMA((nt,))],
    )(*shards)


def _exchange_sibling(name, gs):
    nt = len(gs)

    def body(*refs):
        xs, outs = refs[:nt], refs[nt:2 * nt]
        send_sems, recv_sems = refs[2 * nt:]
        x, y, c, _ = _mesh_pos()
        sibling = (x, y, 1 - c)
        copies = []
        for t in range(nt):
            for ch in range(4):
                copies.append(pltpu.make_async_remote_copy(
                    src_ref=xs[t].at[2 * ch + (1 - c)], dst_ref=outs[t].at[ch],
                    send_sem=send_sems.at[t, ch], recv_sem=recv_sems.at[t, ch],
                    device_id=sibling, device_id_type=MESH))
        for cp in copies:
            cp.start()
        for cp in copies:
            cp.wait()

    out_shape = [jax.ShapeDtypeStruct((4,) + g.shape[1:], g.dtype) for g in gs]
    return pl.pallas_call(
        body, name=name, in_specs=[ANY] * nt, out_specs=[ANY] * nt, out_shape=out_shape,
        scratch_shapes=[pltpu.SemaphoreType.DMA((nt, 4)), pltpu.SemaphoreType.DMA((nt, 4))],
    )(*gs)


def _pair_sum(name, gs, recv):
    c = lax.axis_index("c")
    outs = []
    for t, (g, rv) in enumerate(zip(gs, recv)):
        _, r, cols = rv.shape

        def body(c_ref, g_ref, r_ref, o_ref):
            o_ref[...] = (g_ref[...].astype(F32) + r_ref[...].astype(F32)).astype(o_ref.dtype)

        outs.append(pl.pallas_call(
            body, name=f"{name}_{t}",
            grid_spec=pltpu.PrefetchScalarGridSpec(
                num_scalar_prefetch=1, grid=(4,),
                in_specs=[pl.BlockSpec((None, r, cols), lambda ch, cr: (2 * ch + cr[0], 0, 0)),
                          pl.BlockSpec((None, r, cols), lambda ch, cr: (ch, 0, 0))],
                out_specs=pl.BlockSpec((None, r, cols), lambda ch, cr: (ch, 0, 0))),
            out_shape=jax.ShapeDtypeStruct(rv.shape, rv.dtype),
        )(jnp.reshape(c, (1,)).astype(jnp.int32), g, rv))
    return outs


def _exchange_chips(name, ps):
    nt = len(ps)

    def body(*refs):
        xs, outs = refs[:nt], refs[nt:2 * nt]
        send_sems, recv_sems = refs[2 * nt:]
        x, y, c, chips = _mesh_pos()
        copies = []
        for t in range(nt):
            for j, (cx, cy) in enumerate(chips):
                copies.append(pltpu.make_async_remote_copy(
                    src_ref=xs[t].at[2 * cx + cy], dst_ref=outs[t].at[j],
                    send_sem=send_sems.at[t, j], recv_sem=recv_sems.at[t, j],
                    device_id=(cx, cy, c), device_id_type=MESH))
        for cp in copies:
            cp.start()
        for cp in copies:
            cp.wait()

    out_shape = [jax.ShapeDtypeStruct((3,) + p.shape[1:], p.dtype) for p in ps]
    return pl.pallas_call(
        body, name=name, in_specs=[ANY] * nt, out_specs=[ANY] * nt, out_shape=out_shape,
        scratch_shapes=[pltpu.SemaphoreType.DMA((nt, 3)), pltpu.SemaphoreType.DMA((nt, 3))],
    )(*ps)


def _final_sum(name, ps, recv):
    chip = 2 * lax.axis_index("x") + lax.axis_index("y")
    outs = []
    for t, (p, rv) in enumerate(zip(ps, recv)):
        _, r, cols = rv.shape

        def body(c_ref, p_ref, r0_ref, r1_ref, r2_ref, o_ref):
            o_ref[...] = ((p_ref[...].astype(F32) + r0_ref[...].astype(F32)) + r1_ref[...].astype(F32)) + r2_ref[...].astype(F32)

        outs.append(pl.pallas_call(
            body, name=f"{name}_{t}",
            grid_spec=pltpu.PrefetchScalarGridSpec(
                num_scalar_prefetch=1, grid=(1,),
                in_specs=[pl.BlockSpec((None, r, cols), lambda i, cr: (cr[0], 0, 0))] +
                         [pl.BlockSpec((None, r, cols), lambda i, cr, j=j: (j, 0, 0)) for j in range(3)],
                out_specs=pl.BlockSpec((r, cols), lambda i, cr: (0, 0))),
            out_shape=jax.ShapeDtypeStruct((r, cols), F32),
        )(jnp.reshape(chip, (1,)).astype(jnp.int32), p, rv, rv, rv))
    return outs


def _reduce_scatter(tag, gs):
    recv1 = _exchange_sibling(f"rs_sibling_{tag}", gs)
    ps = _pair_sum(f"rs_pair_{tag}", gs, recv1)
    recv2 = _exchange_chips(f"rs_chips_{tag}", ps)
    return _final_sum(f"rs_sum_{tag}", ps, recv2)


def _sum_gathered(g):
    _, r, cols = g.shape

    def body(g_ref, o_ref):
        acc = g_ref[0]
        for d in range(1, NDEV):
            acc = acc + g_ref[d]
        o_ref[...] = acc

    return pl.pallas_call(body, name="small_sum", out_shape=jax.ShapeDtypeStruct((r, cols), F32))(g)


HBM = pl.BlockSpec(memory_space=pltpu.HBM)
SEM = pl.BlockSpec(memory_space=pltpu.SEMAPHORE)
EFFECT = pltpu.SideEffectType.DATAFLOW_SIDE_EFFECTING
NREL = NDEV - 1


def _related(k):
    x, y, c = lax.axis_index("x"), lax.axis_index("y"), lax.axis_index("c")
    px = 1 - x if k & 4 else x
    py = 1 - y if k & 2 else y
    pc = 1 - c if k & 1 else c
    return (px, py, pc), _dev_index(px, py, pc)


def _in_hbm(a):
    return pltpu.with_memory_space_constraint(a, pltpu.HBM)


def _split_copies(xs, lands, send_sems, recv_sems, src_of, dst_of):
    copies = []
    for t in range(len(xs)):
        for k in range(1, NDEV):
            peer, peer_idx = _related(k)
            copies.append(pltpu.make_async_remote_copy(
                src_ref=src_of(xs[t], t, peer_idx), dst_ref=dst_of(lands[t], t, k, peer_idx),
                send_sem=send_sems.at[t * NREL + k - 1], recv_sem=recv_sems.at[t * NREL + k - 1],
                device_id=peer, device_id_type=MESH))
    return copies


def _exchange_start(name, xs, lands, src_of, dst_of, after):
    nt = len(xs)

    def body(*refs):
        x_refs, land_refs = refs[:nt], refs[nt:2 * nt]
        send_sems, recv_sems = refs[2 * nt + 1], refs[2 * nt + 2]
        token = refs[-1]
        for cp in _split_copies(x_refs, land_refs, send_sems, recv_sems, src_of, dst_of):
            cp.start()
        token[...] = jnp.zeros_like(token)

    res = pl.pallas_call(
        body, name=name,
        out_shape=(pltpu.SemaphoreType.DMA((nt * NREL,)), pltpu.SemaphoreType.DMA((nt * NREL,)),
                   *[pltpu.HBM(a.shape, a.dtype) for a in xs], *[pltpu.HBM(a.shape, a.dtype) for a in lands],
                   jax.ShapeDtypeStruct((8, 128), F32)),
        in_specs=[HBM] * (2 * nt) + [ANY],
        out_specs=(SEM, SEM, *[HBM] * (2 * nt), pl.BlockSpec(memory_space=pltpu.VMEM)),
        input_output_aliases={i: 2 + i for i in range(2 * nt)},
        compiler_params=pltpu.CompilerParams(has_side_effects=EFFECT),
    )(*[_in_hbm(a) for a in xs], *[_in_hbm(a) for a in lands], after)
    return res[0], res[1], res[2:2 + nt], res[2 + nt:2 + 2 * nt], res[-1]


def _exchange_wait(name, send_sems, recv_sems, xs, lands, src_of, dst_of, after):
    nt = len(xs)

    def body(*refs):
        x_refs, land_refs = refs[:nt], refs[nt:2 * nt]
        send_sems, recv_sems = refs[2 * nt], refs[2 * nt + 1]
        for cp in _split_copies(x_refs, land_refs, send_sems, recv_sems, src_of, dst_of):
            cp.wait_send()
            cp.wait_recv()

    res = pl.pallas_call(
        body, name=name,
        out_shape=(*[pltpu.HBM(a.shape, a.dtype) for a in xs], *[pltpu.HBM(a.shape, a.dtype) for a in lands]),
        in_specs=[HBM] * (2 * nt) + [SEM, SEM, ANY], out_specs=tuple([HBM] * (2 * nt)),
        input_output_aliases={i: i for i in range(2 * nt)},
        compiler_params=pltpu.CompilerParams(has_side_effects=EFFECT),
    )(*xs, *lands, send_sems, recv_sems, after)
    return res[:nt], res[nt:]


def _gather_start(name, shards, leads, after):
    def src_of(x_ref, t, peer_idx):
        return x_ref

    def dst_of(land_ref, t, k, peer_idx):
        me = _dev_index(lax.axis_index("x"), lax.axis_index("y"), lax.axis_index("c"))
        return land_ref.at[(slice(None),) * leads[t] + (me,)]

    lands = [lax.empty(s.shape[:ld] + (NDEV,) + s.shape[ld:], s.dtype) for s, ld in zip(shards, leads)]
    return _exchange_start(name, shards, lands, src_of, dst_of, after)


def _gather_wait(name, started, leads, after):
    send_sems, recv_sems, shards, lands, _ = started

    def src_of(x_ref, t, peer_idx):
        return x_ref

    def dst_of(land_ref, t, k, peer_idx):
        return land_ref.at[(slice(None),) * leads[t] + (peer_idx,)]

    shards, lands = _exchange_wait(name, send_sems, recv_sems, shards, lands, src_of, dst_of, after)
    me = _dev_index(lax.axis_index("x"), lax.axis_index("y"), lax.axis_index("c"))
    return [lax.dynamic_update_index_in_dim(g, s, me, ld) for g, s, ld in zip(lands, shards, leads)]


def _scatter_src(x_ref, t, peer_idx):
    return x_ref.at[peer_idx]


def _scatter_dst(land_ref, t, k, peer_idx):
    return land_ref.at[k - 1]


def _scatter_start(name, gs, after):
    lands = [lax.empty((NREL,) + g.shape[1:], g.dtype) for g in gs]
    return _exchange_start(name, gs, lands, _scatter_src, _scatter_dst, after)


def _scatter_wait(name, started, after):
    send_sems, recv_sems, gs, lands, _ = started
    gs, lands = _exchange_wait(name, send_sems, recv_sems, gs, lands, _scatter_src, _scatter_dst, after)
    me = _dev_index(lax.axis_index("x"), lax.axis_index("y"), lax.axis_index("c"))
    outs = []
    for t, (g, rv) in enumerate(zip(gs, lands)):
        _, r, cols = rv.shape

        def body(c_ref, own_ref, rv_ref, o_ref):
            acc = own_ref[...].astype(F32)
            for k in range(NREL):
                acc = acc + rv_ref[k].astype(F32)
            o_ref[...] = acc

        outs.append(pl.pallas_call(
            body, name=f"{name}_sum_{t}",
            grid_spec=pltpu.PrefetchScalarGridSpec(
                num_scalar_prefetch=1, grid=(1,),
                in_specs=[pl.BlockSpec((None, r, cols), lambda i, cr: (cr[0], 0, 0)),
                          pl.BlockSpec((NREL, r, cols), lambda i, cr: (0, 0, 0))],
                out_specs=pl.BlockSpec((r, cols), lambda i, cr: (0, 0))),
            out_shape=jax.ShapeDtypeStruct((r, cols), F32), compiler_params=_cparams(VMEM_BIG),
        )(jnp.reshape(me, (1,)).astype(jnp.int32), g, rv))
    return outs


def _adamw(w, g, m, v):
    shape = w.shape
    cols = shape[-1]
    rows = math.prod(shape[:-1]) if len(shape) > 1 else 1
    w2, g2, m2, v2 = (jnp.reshape(t, (rows, cols)) for t in (w, g, m, v))
    tr = _pick(rows, (512, 256, 128)) if rows * cols > 65536 else rows
    c1 = 1.0 / (1.0 - ADAM_B1 ** ADAM_STEP)
    c2 = 1.0 / (1.0 - ADAM_B2 ** ADAM_STEP)

    def body(w_ref, g_ref, m_ref, v_ref, d_ref, nm_ref, nv_ref):
        gv = g_ref[...]
        nm = ADAM_B1 * m_ref[...] + (1.0 - ADAM_B1) * gv
        nv = ADAM_B2 * v_ref[...] + (1.0 - ADAM_B2) * (gv * gv)
        d_ref[...] = -ADAM_LR * ((nm * c1) / (jnp.sqrt(nv * c2) + ADAM_EPS) + ADAM_WD * w_ref[...])
        nm_ref[...] = nm
        nv_ref[...] = nv

    spec = pl.BlockSpec((tr, cols), lambda i: (i, 0))
    osh = jax.ShapeDtypeStruct((rows, cols), F32)
    d, nm, nv = pl.pallas_call(
        body, name="adamw", grid=(rows // tr,), in_specs=[spec] * 4, out_specs=[spec] * 3, out_shape=[osh] * 3,
    )(w2, g2, m2, v2)
    return jnp.reshape(d, shape), jnp.reshape(nm, shape), jnp.reshape(nv, shape)


def _pad_heads(w, name):
    if name not in P_HEADS:
        return w
    nh, real = P_HEADS[name]
    w = jnp.reshape(w, w.shape[:-2] + (nh, real, w.shape[-1]))
    w = jnp.pad(w, [(0, 0)] * (w.ndim - 2) + [(0, HP - real), (0, 0)])
    return jnp.reshape(w, w.shape[:-3] + (nh * HP, w.shape[-1]))


def _unpad_heads(w, name):
    if name not in P_HEADS:
        return w
    nh, real = P_HEADS[name]
    w = jnp.reshape(w, w.shape[:-2] + (nh, HP, w.shape[-1]))[..., :real, :]
    return jnp.reshape(w, w.shape[:-3] + (nh * real, w.shape[-1]))


def _win_pad(win_t):
    segs, o = {}, 0
    for n, s in zip(IN_NAMES, IN_SIZES):
        segs[n] = win_t[..., o:o + s, :]
        o += s
    return jnp.concatenate([_pad_heads(segs[n], n) for n in P_ORDER], axis=-2)


def _win_unpad(win_p):
    segs = {n: _unpad_heads(win_p[..., P_OFF[n]:P_OFF[n] + P_WIDTH[n], :], n) for n in P_ORDER}
    return jnp.concatenate([segs[n] for n in IN_NAMES], axis=-2)


def _t(w):
    return jnp.swapaxes(w, -1, -2)


def _layer_weights(g_g, g_u, g_d, g_in, g_pa, g_pb, g_out, gains, w2, b2, bm, gn, gq, gk):
    w2p = jnp.pad(jnp.reshape(w2, (2, GLA_RANK, GLA_H, GLA_DK)), ((0, 0), (0, HP - GLA_RANK), (0, 0), (0, HP - GLA_DK)))
    b2p = jnp.pad(jnp.reshape(b2, (2, 1, GLA_H, GLA_DK)), ((0, 0), (0, 0), (0, 0), (0, HP - GLA_DK)))
    wpb_t = jnp.pad(jnp.reshape(g_pb, (D, ATT_H, HEAD_DIM)), ((0, 0), (0, 0), (0, HP - HEAD_DIM)))
    return dict(
        gains=jnp.reshape(gains, (1, 3, 1, D)),
        wg_t=jnp.reshape(g_g, (1, 2, DFF, D)), wu_t=jnp.reshape(g_u, (1, 2, DFF, D)), wd=jnp.reshape(g_d, (1, 2, DFF, D)),
        win_t=_win_pad(jnp.reshape(g_in, (1, D_IN, D))), wpa_t=jnp.reshape(g_pa, (1, D, 512)),
        wpb_t=jnp.reshape(wpb_t, (1, D, ATT_H * HP)), wout=jnp.reshape(g_out, (1, D, D)),
        w2p=jnp.reshape(w2p, (1, 2, HP, GLA_H * HP)), b2p=jnp.reshape(b2p, (1, 2, 1, GLA_H * HP)),
        bm=jnp.reshape(bm, (1, 2, 1, D)), gn=jnp.reshape(gn, (1, 1, GLA_H * HP)),
        gq=jnp.pad(jnp.reshape(gq, (1, 1, HEAD_DIM)), ((0, 0), (0, 0), (0, HP - HEAD_DIM))),
        gk=jnp.pad(jnp.reshape(gk, (1, 1, HEAD_DIM)), ((0, 0), (0, 0), (0, HP - HEAD_DIM))))


def _layer_fwd(h, z, w, tabs, next_gain):
    h, z, s0 = _ffn_fwd(h, z, w["wg_t"], w["wu_t"], w["wd"], (0, 0), w["gains"][0, 1])
    h, z, s1 = _mixer_fwd(h, z, w, 0, tabs, w["gains"][0, 2])
    h, z, s2 = _ffn_fwd(h, z, w["wg_t"], w["wu_t"], w["wd"], (0, 1), next_gain)
    return h, z, (s0, s1, s2)


def _layer_bwd(dh, dhb, saved, w, tabs):
    s0, s1, s2 = saved
    dh, dhb, dg2, dwg1, dwu1, dwd1 = _ffn_bwd(dh, dhb, s2, w["gains"][0, 2], w["wg_t"], w["wu_t"], w["wd"], (0, 1))
    dh, dhb, gm = _mixer_bwd(dh, dhb, s1, w["gains"][0, 1], w, 0, tabs)
    dh, dhb, dg0, dwg0, dwu0, dwd0 = _ffn_bwd(dh, dhb, s0, w["gains"][0, 0], w["wg_t"], w["wu_t"], w["wd"], (0, 0))
    gm.update(gain0=dg0, gain2=dg2, wg0=dwg0, wu0=dwu0, wd0=dwd0, wg1=dwg1, wu1=dwu1, wd1=dwd1)
    return dh, dhb, gm


def _big_grads(g):
    d_in = _win_unpad(g["win_t"])
    d_pb = jnp.reshape(jnp.reshape(g["wpb_t"], (D, ATT_H, HP))[:, :, :HEAD_DIM], (D, 512))
    ts = [g["wg0"], g["wu0"], g["wd0"], g["wg1"], g["wu1"], g["wd1"], d_in, g["wpa_t"], d_pb, g["wout"]]
    return [jnp.reshape(t, (NDEV, t.shape[0] // NDEV, t.shape[1])) for t in ts]


def kernel(x, meta_tokens, norm_gains, ffn_w_gate, ffn_w_up, ffn_w_down, w_in, gla_w2, gla_b2, gla_gn, q_norm, k_norm, w_pa, w_pb, b_merge, w_out, final_norm, loss_target, m_meta_tokens, m_norm_gains, m_ffn_w_gate, m_ffn_w_up, m_ffn_w_down, m_w_in, m_gla_w2, m_gla_b2, m_gla_gn, m_q_norm, m_k_norm, m_w_pa, m_w_pb, m_b_merge, m_w_out, m_final_norm, v_meta_tokens, v_norm_gains, v_ffn_w_gate, v_ffn_w_up, v_ffn_w_down, v_w_in, v_gla_w2, v_gla_b2, v_gla_gn, v_q_norm, v_k_norm, v_w_pa, v_w_pb, v_b_merge, v_w_out, v_final_norm):
    dev = _dev_index(lax.axis_index("x"), lax.axis_index("y"), lax.axis_index("c"))
    sh_g = _t(ffn_w_gate).astype(BF16)
    sh_u = _t(ffn_w_up).astype(BF16)
    sh_d = ffn_w_down.astype(BF16)
    sh_in = _t(w_in).astype(BF16)
    sh_pa = _t(w_pa).astype(BF16)
    sh_pb = _t(w_pb).astype(BF16)
    sh_out = w_out.astype(BF16)
    small = jnp.concatenate([jnp.reshape(t, (-1, 128)) for t in
                             (meta_tokens, norm_gains, gla_w2, gla_b2, b_merge)], axis=0)
    small = jnp.pad(small, ((0, 2), (0, 0)))
    def shards(l):
        return [sh_g[l], sh_u[l], sh_d[l], sh_in[l], sh_pa[l], sh_pb[l], sh_out[l]]

    w_leads = [1, 1, 1, 0, 0, 0, 0]
    *g0, g_small = _all_gather("gather_layer0", shards(0) + [small], w_leads + [0])
    started = {1: _gather_start("gather_start_1", shards(1), w_leads, g_small)}
    meta_full = jnp.reshape(jnp.transpose(g_small[:, 0:16], (1, 0, 2)), (NMETA, D)) + started[1][4][0, 0]
    gains_full = jnp.reshape(jnp.transpose(jnp.reshape(g_small[:, 16:28], (NDEV, DEPTH, 3, 128)), (1, 2, 0, 3)), (DEPTH, 3, D))
    w2_full = jnp.reshape(jnp.transpose(jnp.reshape(g_small[:, 28:60], (NDEV, DEPTH, 2, GLA_RANK, 32)), (1, 2, 3, 0, 4)),
                          (DEPTH, 2, GLA_RANK, 256))
    b2_full = jnp.reshape(jnp.transpose(jnp.reshape(g_small[:, 60:62], (NDEV, DEPTH, 2, 32)), (1, 2, 0, 3)), (DEPTH, 2, 256))
    bm_full = jnp.reshape(jnp.transpose(jnp.reshape(g_small[:, 62:70], (NDEV, DEPTH, 2, 128)), (1, 2, 0, 3)), (DEPTH, 2, D))

    def layer_weights(l, gathered, gains_l):
        return _layer_weights(*gathered, gains_l, w2_full[l], b2_full[l], bm_full[l], gla_gn[l], q_norm[l], k_norm[l])

    xl = x[0]
    lp = xl.shape[0] + NULL + NMETA
    tabs = _rope_tables(lp)
    h = jnp.concatenate([jnp.zeros((NULL, D), F32), meta_full, xl], axis=0)
    weights, saved = [], []
    z = _rmsnorm_fwd(h, jnp.reshape(gains_full[0, 0], (1, D)))
    for l in range(DEPTH):
        tok = jnp.zeros((), F32)
        if 1 <= l < DEPTH - 1:
            started[l + 1] = _gather_start(f"gather_start_{l + 1}", shards(l + 1), w_leads, h)
            tok = started[l + 1][4][0, 0]
        gathered = g0 if l == 0 else _gather_wait(f"gather_wait_{l}", started[l], w_leads, h)
        weights.append(layer_weights(l, gathered, gains_full[l] + tok))
        next_gain = jnp.reshape(gains_full[l + 1, 0], (1, D)) if l + 1 < DEPTH else None
        h, z, sv = _layer_fwd(h, z, weights[l], tabs, next_gain)
        saved.append(sv)
    loss, dh, dhb, d_final = _loss_head(h, loss_target[0], jnp.reshape(final_norm, (1, D)))
    loss = lax.psum(loss[0, 0], ("x", "y", "c"))

    grads, scattering = [None] * DEPTH, {}
    tok = jnp.zeros((), F32)
    for l in reversed(range(DEPTH)):
        w = dict(weights[l], gains=weights[l]["gains"] + tok)
        dh, dhb, grads[l] = _layer_bwd(dh, dhb, saved[l], w, tabs)
        if l > 0:
            scattering[l] = _scatter_start(f"scatter_start_{l}", _big_grads(grads[l]), dhb)
            tok = scattering[l][4][0, 0]
    grad_x = dh[NULL + NMETA:][None]
    red = [None] * DEPTH
    for l in reversed(range(1, DEPTH)):
        red[l] = _scatter_wait(f"scatter_wait_{l}", scattering[l], dhb)
    red[0] = _reduce_scatter("0", _big_grads(grads[0]))
    g_gate = jnp.stack([jnp.stack([_t(red[l][0]), _t(red[l][3])]) for l in range(DEPTH)])
    g_up = jnp.stack([jnp.stack([_t(red[l][1]), _t(red[l][4])]) for l in range(DEPTH)])
    g_down = jnp.stack([jnp.stack([red[l][2], red[l][5]]) for l in range(DEPTH)])
    g_win = jnp.stack([_t(red[l][6]) for l in range(DEPTH)])
    g_wpa = jnp.stack([_t(red[l][7]) for l in range(DEPTH)])
    g_wpb = jnp.stack([_t(red[l][8]) for l in range(DEPTH)])
    g_wout = jnp.stack([red[l][9] for l in range(DEPTH)])

    d_meta = dh[NULL:NULL + NMETA]
    d_gains = jnp.stack([jnp.concatenate([grads[l]["gain0"], grads[l]["gain"], grads[l]["gain2"]], axis=0)
                         for l in range(DEPTH)])
    d_w2 = jnp.stack([jnp.reshape(jnp.reshape(grads[l]["w2p"], (2, HP, GLA_H, HP))[:, :GLA_RANK, :, :GLA_DK],
                                  (2, GLA_RANK, 256)) for l in range(DEPTH)])
    d_b2 = jnp.stack([jnp.reshape(jnp.reshape(grads[l]["b2p"], (2, GLA_H, HP))[:, :, :GLA_DK], (2, 256))
                      for l in range(DEPTH)])
    d_gn = jnp.stack([grads[l]["gn"][0] for l in range(DEPTH)])
    d_gq = jnp.stack([grads[l]["gq"][0, :HEAD_DIM] for l in range(DEPTH)])
    d_gk = jnp.stack([grads[l]["gk"][0, :HEAD_DIM] for l in range(DEPTH)])
    d_bm = jnp.stack([jnp.concatenate([grads[l]["bma"], grads[l]["bmb"]], axis=0) for l in range(DEPTH)])
    parts = [d_meta, d_gains, d_w2, d_b2, d_gn, d_gq, d_gk, d_bm, d_final[0]]
    sizes = [p.size for p in parts]
    flat = jnp.concatenate([jnp.reshape(p, (-1,)) for p in parts])
    flat = jnp.reshape(flat, (-1, 128))
    nrow = flat.shape[0]
    flat = jnp.pad(flat, ((0, (-nrow) % 8), (0, 0)))
    (g_flat,) = _all_gather("gather_small_grads", [flat], [0])
    tot = jnp.reshape(_sum_gathered(g_flat), (-1,))
    full, o = [], 0
    for p, s in zip(parts, sizes):
        full.append(jnp.reshape(tot[o:o + s], p.shape))
        o += s
    f_meta, f_gains, f_w2, f_b2, f_gn, f_gq, f_gk, f_bm, f_final = full

    def mine(t, width):
        return lax.dynamic_slice_in_dim(t, dev * width, width, axis=t.ndim - 1)

    g_small = dict(meta_tokens=mine(f_meta, 128), norm_gains=mine(f_gains, 128), gla_w2=mine(f_w2, 32),
                   gla_b2=mine(f_b2, 32), gla_gn=f_gn, q_norm=f_gq, k_norm=f_gk, b_merge=mine(f_bm, 128),
                   final_norm=f_final)
    gr = dict(g_small, ffn_w_gate=g_gate, ffn_w_up=g_up, ffn_w_down=g_down, w_in=g_win, w_pa=g_wpa, w_pb=g_wpb,
              w_out=g_wout)
    ws = dict(meta_tokens=meta_tokens, norm_gains=norm_gains, ffn_w_gate=ffn_w_gate, ffn_w_up=ffn_w_up,
              ffn_w_down=ffn_w_down, w_in=w_in, gla_w2=gla_w2, gla_b2=gla_b2, gla_gn=gla_gn, q_norm=q_norm,
              k_norm=k_norm, w_pa=w_pa, w_pb=w_pb, b_merge=b_merge, w_out=w_out, final_norm=final_norm)
    ms = dict(meta_tokens=m_meta_tokens, norm_gains=m_norm_gains, ffn_w_gate=m_ffn_w_gate, ffn_w_up=m_ffn_w_up,
              ffn_w_down=m_ffn_w_down, w_in=m_w_in, gla_w2=m_gla_w2, gla_b2=m_gla_b2, gla_gn=m_gla_gn, q_norm=m_q_norm,
              k_norm=m_k_norm, w_pa=m_w_pa, w_pb=m_w_pb, b_merge=m_b_merge, w_out=m_w_out, final_norm=m_final_norm)
    vs = dict(meta_tokens=v_meta_tokens, norm_gains=v_norm_gains, ffn_w_gate=v_ffn_w_gate, ffn_w_up=v_ffn_w_up,
              ffn_w_down=v_ffn_w_down, w_in=v_w_in, gla_w2=v_gla_w2, gla_b2=v_gla_b2, gla_gn=v_gla_gn, q_norm=v_q_norm,
              k_norm=v_k_norm, w_pa=v_w_pa, w_pb=v_w_pb, b_merge=v_b_merge, w_out=v_w_out, final_norm=v_final_norm)
    names = ["meta_tokens", "norm_gains", "ffn_w_gate", "ffn_w_up", "ffn_w_down", "w_in", "gla_w2", "gla_b2", "gla_gn",
             "q_norm", "k_norm", "w_pa", "w_pb", "b_merge", "w_out", "final_norm"]
    deltas, new_m, new_v = [], [], []
    for n in names:
        dlt, nm, nv = _adamw(ws[n], gr[n], ms[n], vs[n])
        deltas.append(dlt)
        new_m.append(nm)
        new_v.append(nv)
    return (loss, grad_x, *[gr[n] for n in names], *deltas, *new_m, *new_v)
```

```python
import functools
import math

import jax
import jax.numpy as jnp
from jax import lax
from jax.experimental import pallas as pl
from jax.experimental.pallas import tpu as pltpu

F32 = jnp.float32
BF16 = jnp.bfloat16
MESH = pl.DeviceIdType.MESH
ANY = pl.BlockSpec(memory_space=pl.ANY)

NDEV = 8
D = 1024
DFF = 2816
DEPTH = 4
NMETA = 16
NULL = 112
GRID_W = 64
EPS = 1e-6
HP = 128
GLA_H = 4
GLA_DK = 64
GLA_RANK = 16
GLA_TAU = 16.0
CHUNK = 64
ATT_H = 8
ATT_KV = 2
ATT_G = ATT_H // ATT_KV
HEAD_DIM = 64
ROPE_THETA = 10000.0

IN_SIZES = (256, 256, 512, 512, 16, 16, 512, 128, 128, 1024, 1024)
IN_NAMES = ("qa", "ka", "va", "ra", "lrf", "lrb", "qb", "kb", "vb", "ga", "gb")
D_IN = sum(IN_SIZES)
P_ORDER = ("qb", "ga", "gb", "qa", "ka", "va", "ra", "kb", "vb", "lrf", "lrb")
P_WIDTH = dict(qb=1024, ga=1024, gb=1024, qa=512, ka=512, va=512, ra=512, kb=256, vb=256, lrf=128, lrb=128)
P_OFF = {}
_o = 0
for _n in P_ORDER:
    P_OFF[_n] = _o
    _o += P_WIDTH[_n]
D_INP = _o
P_HEADS = dict(qa=(4, 64), ka=(4, 64), qb=(8, 64), kb=(2, 64), vb=(2, 64), lrf=(1, 16), lrb=(1, 16))

ADAM_LR = 0.001
ADAM_B1 = 0.9
ADAM_B2 = 0.999
ADAM_EPS = 1e-08
ADAM_WD = 0.01
ADAM_STEP = 10

VMEM_BIG = 56 * 1024 * 1024
MXU_N = 256


def _cparams(vmem=None):
    return pltpu.CompilerParams(vmem_limit_bytes=vmem) if vmem else pltpu.CompilerParams()


def _pick(n, prefs):
    for p in prefs:
        if n % p == 0:
            return p
    return n


def _tm(lp):
    return _pick(lp, (528, 512, 256, 128))


_DN = {"nn": (((1,), (0,)), ((), ())), "nt": (((1,), (1,)), ((), ())), "tn": (((0,), (0,)), ((), ()))}


def _dot(a, b, mode="nn", precision=None):
    return lax.dot_general(a, b, _DN[mode], preferred_element_type=F32, precision=precision)


def _split(x):
    hi = x.astype(BF16)
    return hi, (x - hi.astype(F32)).astype(BF16)


def _dot_sel(t, x, mode="nn"):
    hi, lo = _split(x)
    return _dot(t, hi, mode) + _dot(t, lo, mode)


def _dot3(a, b, mode="nn"):
    ah, al = _split(a)
    bh, bl = _split(b)
    return _dot(ah, bh, mode) + (_dot(ah, bl, mode) + _dot(al, bh, mode))


def _sigmoid(x):
    return 0.5 * jnp.tanh(0.5 * x) + 0.5


def _mm(name, m, n, terms, outs, epilogue, extras=(), *, tm, tn, nk=1, nsub=1, i_outer=False, vmem=None):
    gm, gn = m // tm, n // tn
    assert gm * tm == m and gn * tn == n, (name, m, n, tm, tn)
    n_acc = 1 + max(t[3] for t in terms)

    def gmap(f):
        if i_outer:
            return lambda i, j, kk: f(i, j, kk)
        return lambda j, i, kk: f(i, j, kk)

    in_specs, args = [], []
    for a, b, mode, _, pa, pb in terms:
        kdim = a.shape[-2] if mode == "tn" else a.shape[-1]
        tk = kdim // nk
        assert tk * nk == kdim
        na, nb = (None,) * len(pa), (None,) * len(pb)
        if mode == "tn":
            in_specs.append(pl.BlockSpec(na + (tk, tm), gmap(lambda i, j, kk, pa=pa: pa + (kk, i))))
        else:
            in_specs.append(pl.BlockSpec(na + (tm, tk), gmap(lambda i, j, kk, pa=pa: pa + (i, kk))))
        if mode == "nt":
            in_specs.append(pl.BlockSpec(nb + (tn, tk), gmap(lambda i, j, kk, pb=pb: pb + (j, kk))))
        else:
            in_specs.append(pl.BlockSpec(nb + (tk, tn), gmap(lambda i, j, kk, pb=pb: pb + (kk, j))))
        args += [a, b]
    for arr, kind, off, pe in extras:
        ne = (None,) * len(pe)
        if kind == "mn":
            in_specs.append(pl.BlockSpec(ne + (tm, tn), gmap(lambda i, j, kk, off=off, pe=pe: pe + (i, j + off))))
        else:
            in_specs.append(pl.BlockSpec(ne + (1, tn), gmap(lambda i, j, kk, off=off, pe=pe: pe + (0, j + off))))
        args.append(arr)
    out_shape, out_specs = [], []
    for shape, dtype, kind, off, po in outs:
        no = (None,) * len(po)
        out_shape.append(jax.ShapeDtypeStruct(shape, dtype))
        if kind == "mn":
            out_specs.append(pl.BlockSpec(no + (tm, tn), gmap(lambda i, j, kk, off=off, po=po: po + (i, j + off))))
        else:
            assert not i_outer
            out_specs.append(pl.BlockSpec(no + (1, tn), gmap(lambda i, j, kk, off=off, po=po: po + (0, j + off))))
    n_t, n_e, n_o = len(terms), len(extras), len(outs)
    i_axis = 0 if i_outer else 1

    def body(*refs):
        ins = refs[: 2 * n_t]
        exs = refs[2 * n_t: 2 * n_t + n_e]
        ors = refs[2 * n_t + n_e: 2 * n_t + n_e + n_o]
        accs = refs[2 * n_t + n_e + n_o:]
        i = pl.program_id(i_axis)
        kk = pl.program_id(2)

        def partials(cs):
            part = [None] * n_acc
            for t, (_, _, mode, ai, _, _) in enumerate(terms):
                b_ref = ins[2 * t + 1]
                b_val = b_ref[cs, :] if mode == "nt" else b_ref[:, cs]
                p = _dot(ins[2 * t][...], b_val, mode)
                part[ai] = p if part[ai] is None else part[ai] + p
            return part

        def finish(vals, cs):
            res = epilogue(vals, [e[:, cs] for e in exs], i * tm)
            for (_, dtype, kind, _, _), o_ref, v in zip(outs, ors, res):
                if kind == "mn":
                    o_ref[:, cs] = v.astype(dtype)
                else:
                    @pl.when(i == 0)
                    def _():
                        o_ref[:, cs] = v.astype(dtype)

                    @pl.when(i != 0)
                    def _():
                        o_ref[:, cs] += v.astype(dtype)

        if nk == 1:
            w = tn // nsub
            for s in range(nsub):
                cs = slice(s * w, (s + 1) * w)
                finish(partials(cs), cs)
        else:
            part = partials(slice(None))
            @pl.when(kk == 0)
            def _():
                for a_ref, p in zip(accs, part):
                    a_ref[...] = p

            @pl.when(kk != 0)
            def _():
                for a_ref, p in zip(accs, part):
                    a_ref[...] += p

            @pl.when(kk == nk - 1)
            def _():
                finish([a_ref[...] for a_ref in accs], slice(None))

    scratch = [pltpu.VMEM((tm, tn), F32) for _ in range(n_acc)] if nk > 1 else []
    grid = (gm, gn, nk) if i_outer else (gn, gm, nk)
    res = pl.pallas_call(
        body, name=name, grid=grid, in_specs=in_specs, out_specs=out_specs, out_shape=out_shape,
        scratch_shapes=scratch, compiler_params=_cparams(vmem),
    )(*args)
    return res


def _term(a, b, mode, acc=0, pa=(), pb=()):
    return (a, b, mode, acc, tuple(pa), tuple(pb))


def _row_tile(lp):
    return _pick(lp, (384, 256, 128))


def _rmsnorm_fwd(h, gain):
    lp = h.shape[0]
    tr = _row_tile(lp)

    def body(h_ref, g_ref, z_ref):
        x = h_ref[...]
        r = lax.rsqrt(jnp.mean(x * x, axis=-1, keepdims=True) + EPS)
        z_ref[...] = (x * r * g_ref[...]).astype(BF16)

    return pl.pallas_call(
        body, name="rmsnorm_fwd", grid=(lp // tr,),
        in_specs=[pl.BlockSpec((tr, D), lambda i: (i, 0)), pl.BlockSpec((1, D), lambda i: (0, 0))],
        out_specs=pl.BlockSpec((tr, D), lambda i: (i, 0)),
        out_shape=jax.ShapeDtypeStruct((lp, D), BF16),
    )(h, gain)


def _loss_head(h, target, gain):
    lp = h.shape[0]
    tr = 128

    def body(h_ref, t_ref, g_ref, loss_ref, dh_ref, dhb_ref, dg_ref):
        i = pl.program_id(0)

        @pl.when(i == 0)
        def _():
            loss_ref[...] = jnp.zeros_like(loss_ref)
            dg_ref[...] = jnp.zeros_like(dg_ref)
            dh_ref[...] = jnp.zeros_like(dh_ref)
            dhb_ref[...] = jnp.zeros_like(dhb_ref)

        @pl.when(i != 0)
        def _():
            x = h_ref[...]
            g = g_ref[...]
            r = lax.rsqrt(jnp.mean(x * x, axis=-1, keepdims=True) + EPS)
            xh = x * r
            y = xh * g
            err = y - t_ref[...]
            loss_ref[...] += 0.5 * jnp.sum(jnp.sum(err * err, axis=-1, keepdims=True), axis=0, keepdims=True) / D
            dy = err * (1.0 / D)
            dg_ref[...] += jnp.sum(dy * xh, axis=0, keepdims=True)
            dxh = dy * g
            dx = r * (dxh - xh * jnp.mean(dxh * xh, axis=-1, keepdims=True))
            dh_ref[...] = dx
            dhb_ref[...] = dx.astype(BF16)

    row = pl.BlockSpec((tr, D), lambda i: (i, 0))
    vec = pl.BlockSpec((1, D), lambda i: (0, 0))
    return pl.pallas_call(
        body, name="loss_head", grid=(lp // tr,),
        in_specs=[row, pl.BlockSpec((tr, D), lambda i: (jnp.maximum(i - 1, 0), 0)), vec],
        out_specs=[pl.BlockSpec((1, 1), lambda i: (0, 0)), row, row, vec],
        out_shape=[jax.ShapeDtypeStruct((1, 1), F32), jax.ShapeDtypeStruct((lp, D), F32),
                   jax.ShapeDtypeStruct((lp, D), BF16), jax.ShapeDtypeStruct((1, D), F32)],
    )(h, target, gain)


def _silu_parts(g):
    s = _sigmoid(g)
    return g * s, s * (1.0 + g * (1.0 - s))


def _residual_norm_epi(scale, with_norm):
    def epi(accs, exs, row0):
        h2 = exs[0] + scale * accs[0]
        if not with_norm:
            return [h2]
        r = lax.rsqrt(jnp.mean(h2 * h2, axis=-1, keepdims=True) + EPS)
        return [h2, h2 * r * exs[1]]
    return epi


def _norm_bwd_epi(accs, exs, row0):
    dz = accs[0]
    x, res, g = exs
    r = lax.rsqrt(jnp.mean(x * x, axis=-1, keepdims=True) + EPS)
    xh = x * r
    dxh = dz * g
    dx = r * (dxh - xh * jnp.mean(dxh * xh, axis=-1, keepdims=True))
    rows = row0 + lax.broadcasted_iota(jnp.int32, (dz.shape[0], 1), 0)
    dh = jnp.where(rows >= NULL, res + dx, 0.0)
    return [dh, dh, jnp.sum(dz * xh, axis=0, keepdims=True)]


def _norm_bwd_outs(lp):
    return [((lp, D), F32, "mn", 0, ()), ((lp, D), BF16, "mn", 0, ()), ((1, D), F32, "nsum", 0, ())]


def _ffn_fwd(h, z, wg_t, wu_t, wd, pre, next_gain):
    lp = h.shape[0]
    tm = _tm(lp)

    def up_epi(accs, exs, row0):
        g, u = accs
        sg, _ = _silu_parts(g)
        return [g, u, sg * u]

    bshape = (lp, DFF)
    g_, u_, act = _mm("ffn_up", lp, DFF, [_term(z, wg_t, "nt", 0, (), pre), _term(z, wu_t, "nt", 1, (), pre)],
                      [(bshape, BF16, "mn", 0, ())] * 3, up_epi, tm=tm, tn=DFF, nsub=DFF // MXU_N, vmem=VMEM_BIG)

    with_norm = next_gain is not None
    res = _mm("ffn_down", lp, D, [_term(act, wd, "nn", 0, (), pre)],
              [((lp, D), F32, "mn", 0, ())] + ([((lp, D), BF16, "mn", 0, ())] if with_norm else []),
              _residual_norm_epi(0.5, with_norm),
              extras=[(h, "mn", 0, ())] + ([(next_gain, "n", 0, ())] if with_norm else []),
              tm=tm, tn=D, i_outer=True, vmem=VMEM_BIG)
    return res[0], (res[1] if with_norm else None), dict(h=h, z=z, g=g_, u=u_, act=act)


def _dw(name, a, b, m, n, scale=1.0):
    lp = a.shape[0]
    tm = _pick(m, (2944, 1408, 1024, 512, 256, 128))
    tn = _pick(n, (1024, 512, 256, 128))
    nk = lp // _pick(lp, (2112, 256, 128) if tm <= 1408 else (1056, 256, 128))

    def epi(accs, exs, row0):
        return [accs[0] * scale]

    (w,) = _mm(name, m, n, [_term(a, b, "tn")], [((m, n), BF16, "mn", 0, ())], epi, tm=tm, tn=tn, nk=nk,
               i_outer=True, vmem=VMEM_BIG)
    return w


def _ffn_bwd(dh, dhb, sv, gain, wg_t, wu_t, wd, pre):
    lp = dh.shape[0]
    tm = _tm(lp)

    def dact_epi(accs, exs, row0):
        g = exs[0].astype(F32)
        u = exs[1].astype(F32)
        da = 0.5 * accs[0]
        sg, dsg = _silu_parts(g)
        return [da * u * dsg, da * sg]

    dg_, du_ = _mm("ffn_dact", lp, DFF, [_term(dhb, wd, "nt", 0, (), pre)],
                   [((lp, DFF), BF16, "mn", 0, ())] * 2, dact_epi,
                   extras=[(sv["g"], "mn", 0, ()), (sv["u"], "mn", 0, ())], tm=tm, tn=DFF, nsub=DFF // MXU_N,
                   vmem=VMEM_BIG)
    d_wd = _dw("dw_down", sv["act"], dhb, DFF, D, 0.5)
    d_wg = _dw("dw_gate", dg_, sv["z"], DFF, D)
    d_wu = _dw("dw_up", du_, sv["z"], DFF, D)

    nk = _pick(DFF // 128, (2, 1))
    dh2, dhb2, dgain = _mm("ffn_dz", lp, D, [_term(dg_, wg_t, "nn", 0, (), pre), _term(du_, wu_t, "nn", 0, (), pre)],
                           _norm_bwd_outs(lp), _norm_bwd_epi,
                           extras=[(sv["h"], "mn", 0, ()), (dh, "mn", 0, ()), (gain, "n", 0, ())],
                           tm=tm, tn=D, nk=nk, vmem=VMEM_BIG)
    return dh2, dhb2, dgain, d_wg, d_wu, d_wd


def _gla_gates(hin, w2p, b2p):
    lp = hin.shape[0]
    tr = _row_tile(lp)
    bf, bb = P_OFF["lrf"] // HP, P_OFF["lrb"] // HP

    def body(lf_ref, lb_ref, w_ref, b_ref, o_ref, c_ref):
        i = pl.program_id(0)
        rows = i * tr + lax.broadcasted_iota(jnp.int32, (tr, 1), 0)
        r = lax.broadcasted_iota(jnp.int32, (tr, tr), 0)
        c = lax.broadcasted_iota(jnp.int32, (tr, tr), 1)
        same = (r // CHUNK) == (c // CHUNK)
        for d, l_ref in enumerate((lf_ref, lb_ref)):
            logit = _dot3(l_ref[...], w_ref[d]) + b_ref[d]
            g = jnp.where(rows >= NULL, jax.nn.log_sigmoid(logit) * (1.0 / GLA_TAU), 0.0)
            o_ref[d] = g
            tmat = jnp.where(same & ((r >= c) if d == 0 else (r <= c)), 1.0, 0.0).astype(BF16)
            c_ref[d] = _dot_sel(tmat, g)

    spec = pl.BlockSpec((2, tr, 512), lambda i: (0, i, 0))
    return pl.pallas_call(
        body, name="gla_gates", grid=(lp // tr,),
        in_specs=[pl.BlockSpec((tr, HP), lambda i: (i, bf)), pl.BlockSpec((tr, HP), lambda i: (i, bb)),
                  pl.BlockSpec((2, HP, 512), lambda i: (0, 0, 0)), pl.BlockSpec((2, 1, 512), lambda i: (0, 0, 0))],
        out_specs=[spec, spec],
        out_shape=[jax.ShapeDtypeStruct((2, lp, 512), F32)] * 2,
    )(hin, hin, w2p, b2p)


def _gla_rows(lp):
    return _pick(lp, (384, 256, 128))


def _tri(d):
    r = lax.broadcasted_iota(jnp.int32, (CHUNK, CHUNK), 0)
    c = lax.broadcasted_iota(jnp.int32, (CHUNK, CHUNK), 1)
    return (r >= c) if d == 0 else (r <= c)


def _gla_fwd(hin, gates):
    lp = hin.shape[0]
    rb = _gla_rows(lp)
    nb = lp // rb
    cpb = rb // CHUNK
    nchunk = lp // CHUNK
    qo, ko, vo = P_OFF["qa"] // 512, P_OFF["ka"] // 512, P_OFF["va"] // 512
    scale = GLA_DK ** -0.5

    def body(qf, kf, vf, gf, qb, kb, vb_, gb, of, ob, sf, sb, st):
        @pl.when(pl.program_id(0) == 0)
        def _():
            st[...] = jnp.zeros_like(st)

        ins = ((qf, kf, vf, gf, of, sf), (qb, kb, vb_, gb, ob, sb))
        for ci in range(cpb):
            for d in range(2):
                q_ref, k_ref, v_ref, g_ref, o_ref, s_ref = ins[d]
                tri = _tri(d)
                c = ci if d == 0 else cpb - 1 - ci
                rows = slice(c * CHUNK, (c + 1) * CHUNK)
                for h in range(GLA_H):
                    sl = slice(h * HP, (h + 1) * HP)
                    q = q_ref[rows, sl] * scale
                    k = k_ref[rows, sl]
                    v = v_ref[rows, sl]
                    b = g_ref[rows, sl]
                    btot = b[CHUNK - 1:CHUNK] if d == 0 else b[0:1]
                    qd = (q * jnp.exp(b)).astype(BF16)
                    ki = (k * jnp.exp(-b)).astype(BF16)
                    ke = (k * jnp.exp(btot - b)).astype(BF16)
                    vb = v.astype(BF16)
                    att = jnp.where(tri, _dot(qd, ki, "nt"), 0.0)
                    s_prev = st[d, h]
                    o_ref[rows, sl] = _dot(att.astype(BF16), vb) + _dot(qd, s_prev.astype(BF16), "nt")
                    s_ref[h, c] = s_prev
                    st[d, h] = s_prev * jnp.exp(btot) + _dot(vb, ke, "tn")

    def specs(off):
        return (pl.BlockSpec((rb, 512), lambda b: (b, off)), pl.BlockSpec((rb, 512), lambda b: (nb - 1 - b, off)))

    (qf, qb), (kf, kb), (vf, vb2) = specs(qo), specs(ko), specs(vo)
    gf = pl.BlockSpec((None, rb, 512), lambda b: (0, b, 0))
    gb = pl.BlockSpec((None, rb, 512), lambda b: (1, nb - 1 - b, 0))
    of, ob = specs(0)
    sf = pl.BlockSpec((GLA_H, cpb, HP, HP), lambda b: (0, b, 0, 0))
    sb = pl.BlockSpec((GLA_H, cpb, HP, HP), lambda b: (0, nb - 1 - b, 0, 0))
    osh = jax.ShapeDtypeStruct((lp, GLA_H * HP), F32)
    ssh = jax.ShapeDtypeStruct((GLA_H, nchunk, HP, HP), F32)
    return pl.pallas_call(
        body, name="gla_fwd", grid=(nb,),
        in_specs=[qf, kf, vf, gf, qb, kb, vb2, gb], out_specs=[of, ob, sf, sb], out_shape=[osh, osh, ssh, ssh],
        scratch_shapes=[pltpu.VMEM((2, GLA_H, HP, HP), F32)], compiler_params=_cparams(VMEM_BIG),
    )(hin, hin, hin, gates, hin, hin, hin, gates)


def _gla_bwd(hin, gates, states, do):
    lp = hin.shape[0]
    rb = _gla_rows(lp)
    nb = lp // rb
    cpb = rb // CHUNK
    qo, ko, vo = P_OFF["qa"] // 512, P_OFF["ka"] // 512, P_OFF["va"] // 512
    scale = GLA_DK ** -0.5

    def body(qf, kf, vf, gf, sf, dof, qb, kb, vb_, gb, sb, dob,
             dqf, dkf, dvf, dgf, dqb, dkb, dvb, dgb, dst):
        @pl.when(pl.program_id(0) == 0)
        def _():
            dst[...] = jnp.zeros_like(dst)

        ins = ((qf, kf, vf, gf, sf, dof, dqf, dkf, dvf, dgf), (qb, kb, vb_, gb, sb, dob, dqb, dkb, dvb, dgb))
        for ci in range(cpb):
            for d in range(2):
                q_ref, k_ref, v_ref, g_ref, s_ref, do_ref, dq_ref, dk_ref, dv_ref, dg_ref = ins[d]
                tri, tri_t = _tri(d), _tri(1 - d)
                edge = lax.broadcasted_iota(jnp.int32, (CHUNK, 1), 0) == (CHUNK - 1 if d == 0 else 0)
                c = cpb - 1 - ci if d == 0 else ci
                rows = slice(c * CHUNK, (c + 1) * CHUNK)
                for h in range(GLA_H):
                    sl = slice(h * HP, (h + 1) * HP)
                    q = q_ref[rows, sl] * scale
                    k = k_ref[rows, sl]
                    v = v_ref[rows, sl]
                    dout = do_ref[rows, sl].astype(BF16)
                    b = g_ref[rows, sl]
                    btot = b[CHUNK - 1:CHUNK] if d == 0 else b[0:1]
                    e = jnp.exp(b)
                    ei = jnp.exp(-b)
                    et = jnp.exp(btot - b)
                    etot = jnp.exp(btot)
                    qd = q * e
                    ki = k * ei
                    ke = k * et
                    qdb, kib, keb, vb = qd.astype(BF16), ki.astype(BF16), ke.astype(BF16), v.astype(BF16)
                    att_t = jnp.where(tri_t, _dot(kib, qdb, "nt"), 0.0).astype(BF16)
                    d_att = jnp.where(tri, _dot(dout, vb, "nt"), 0.0).astype(BF16)
                    d_att_t = jnp.where(tri_t, _dot(vb, dout, "nt"), 0.0).astype(BF16)
                    s_prev = s_ref[h, c]
                    ds_t = dst[d, h]
                    ds_b = ds_t.astype(BF16)
                    dv = _dot(att_t, dout) + _dot(keb, ds_b, "nt")
                    d_qd = _dot(d_att, kib) + _dot(dout, s_prev.astype(BF16))
                    d_ki = _dot(d_att_t, qdb)
                    d_ke = _dot(vb, ds_b)
                    d_e = jnp.sum(s_prev * ds_t, axis=0, keepdims=True)
                    dst[d, h] = _dot(dout, qdb, "tn") + ds_t * etot
                    db = d_qd * qd - d_ki * ki - d_ke * ke
                    dbtot = jnp.sum(d_ke * ke, axis=0, keepdims=True) + d_e * etot
                    dq_ref[rows, sl] = d_qd * e * scale
                    dk_ref[rows, sl] = d_ki * ei + d_ke * et
                    dv_ref[rows, sl] = dv
                    dg_ref[rows, sl] = db + jnp.where(edge, dbtot, 0.0)

    def fw(off):
        return pl.BlockSpec((rb, 512), lambda b: (nb - 1 - b, off))

    def bw(off):
        return pl.BlockSpec((rb, 512), lambda b: (b, off))

    gf = pl.BlockSpec((None, rb, 512), lambda b: (0, nb - 1 - b, 0))
    gb = pl.BlockSpec((None, rb, 512), lambda b: (1, b, 0))
    sf = pl.BlockSpec((GLA_H, cpb, HP, HP), lambda b: (0, nb - 1 - b, 0, 0))
    sb = pl.BlockSpec((GLA_H, cpb, HP, HP), lambda b: (0, b, 0, 0))
    osh = jax.ShapeDtypeStruct((lp, GLA_H * HP), F32)
    res = pl.pallas_call(
        body, name="gla_bwd", grid=(nb,),
        in_specs=[fw(qo), fw(ko), fw(vo), gf, sf, fw(0), bw(qo), bw(ko), bw(vo), gb, sb, bw(0)],
        out_specs=[fw(0)] * 4 + [bw(0)] * 4, out_shape=[osh] * 8,
        scratch_shapes=[pltpu.VMEM((2, GLA_H, HP, HP), F32)], compiler_params=_cparams(VMEM_BIG),
    )(hin, hin, hin, gates, states[0], do, hin, hin, hin, gates, states[1], do)
    return res[:4], res[4:]


def _gla_out_fwd(o2, hin, gn):
    lp = hin.shape[0]
    tr = _row_tile(lp)
    ro = P_OFF["ra"] // 512

    def body(of_ref, ob_ref, r_ref, gn_ref, a_ref):
        r = r_ref[...]
        sr, _ = _silu_parts(r)
        for h in range(GLA_H):
            sl = slice(h * HP, (h + 1) * HP)
            o = of_ref[:, sl] + ob_ref[:, sl]
            rs = lax.rsqrt(jnp.mean(o * o, axis=-1, keepdims=True) + EPS)
            a_ref[:, sl] = (o * rs * gn_ref[:, sl] * sr[:, sl]).astype(BF16)

    row = pl.BlockSpec((tr, 512), lambda i: (i, 0))
    return pl.pallas_call(
        body, name="gla_out_fwd", grid=(lp // tr,),
        in_specs=[row, row, pl.BlockSpec((tr, 512), lambda i: (i, ro)), pl.BlockSpec((1, 512), lambda i: (0, 0))],
        out_specs=row,
        out_shape=jax.ShapeDtypeStruct((lp, 512), BF16),
    )(o2[0], o2[1], hin, gn)


def _gla_out_bwd(da, o2, hin, gn):
    lp = hin.shape[0]
    tr = _row_tile(lp)
    ro = P_OFF["ra"] // 512

    def body(da_ref, of_ref, ob_ref, r_ref, gn_ref, do_ref, dr_ref, dgn_ref):
        i = pl.program_id(0)
        r = r_ref[...]
        sr, dsr = _silu_parts(r)
        da_v = da_ref[...]
        parts = []
        for h in range(GLA_H):
            sl = slice(h * HP, (h + 1) * HP)
            o = of_ref[:, sl] + ob_ref[:, sl]
            rs = lax.rsqrt(jnp.mean(o * o, axis=-1, keepdims=True) + EPS)
            oh = o * rs
            gn_h = gn_ref[:, sl]
            dah = da_v[:, sl]
            dr_ref[:, sl] = (dah * oh * gn_h * dsr[:, sl]).astype(BF16)
            t = dah * sr[:, sl]
            parts.append(jnp.sum(t * oh, axis=0, keepdims=True))
            doh = t * gn_h
            do_ref[:, sl] = rs * (doh - oh * jnp.mean(doh * oh, axis=-1, keepdims=True))
        part = jnp.concatenate(parts, axis=1)

        @pl.when(i == 0)
        def _():
            dgn_ref[...] = part

        @pl.when(i != 0)
        def _():
            dgn_ref[...] += part

    row = pl.BlockSpec((tr, 512), lambda i: (i, 0))
    return pl.pallas_call(
        body, name="gla_out_bwd", grid=(lp // tr,),
        in_specs=[row, row, row, pl.BlockSpec((tr, 512), lambda i: (i, ro)), pl.BlockSpec((1, 512), lambda i: (0, 0))],
        out_specs=[row, row, pl.BlockSpec((1, 512), lambda i: (0, 0))],
        out_shape=[jax.ShapeDtypeStruct((lp, 512), F32), jax.ShapeDtypeStruct((lp, 512), BF16),
                   jax.ShapeDtypeStruct((1, 512), F32)],
    )(da, o2[0], o2[1], hin, gn)


def _gla_in_bwd(gf, gb, gates, hin, w2p):
    lp = hin.shape[0]
    tr = _row_tile(lp)
    bf, bb = P_OFF["lrf"] // HP, P_OFF["lrb"] // HP

    def body(dqf_ref, dkf_ref, dvf_ref, dgf_ref, dqb_ref, dkb_ref, dvb_ref, dgb_ref, g_ref, lf_ref, lb_ref, w_ref,
             oq_ref, ok_ref, ov_ref, olr_ref, dw_ref, db_ref):
        i = pl.program_id(0)
        oq_ref[...] = (dqf_ref[...] + dqb_ref[...]).astype(BF16)
        ok_ref[...] = (dkf_ref[...] + dkb_ref[...]).astype(BF16)
        ov_ref[...] = (dvf_ref[...] + dvb_ref[...]).astype(BF16)
        rows = i * tr + lax.broadcasted_iota(jnp.int32, (tr, 1), 0)
        r = lax.broadcasted_iota(jnp.int32, (tr, tr), 0)
        c = lax.broadcasted_iota(jnp.int32, (tr, tr), 1)
        same = (r // CHUNK) == (c // CHUNK)
        for d, (l_ref, dg_ref) in enumerate(((lf_ref, dgf_ref), (lb_ref, dgb_ref))):
            tmat = jnp.where(same & ((r <= c) if d == 0 else (r >= c)), 1.0, 0.0).astype(BF16)
            dg = _dot_sel(tmat, dg_ref[...])
            sig_neg = 1.0 - jnp.exp(GLA_TAU * g_ref[d])
            dlogit = jnp.where(rows >= NULL, dg * (1.0 / GLA_TAU) * sig_neg, 0.0)
            olr_ref[:, d * HP:(d + 1) * HP] = _dot3(dlogit, w_ref[d], "nt").astype(BF16)
            dw = _dot3(l_ref[...], dlogit, "tn")
            dbias = jnp.sum(dlogit, axis=0, keepdims=True)

            @pl.when(i == 0)
            def _():
                dw_ref[d] = dw
                db_ref[d] = dbias

            @pl.when(i != 0)
            def _():
                dw_ref[d] += dw
                db_ref[d] += dbias

    two = pl.BlockSpec((2, tr, 512), lambda i: (0, i, 0))
    row = pl.BlockSpec((tr, 512), lambda i: (i, 0))
    return pl.pallas_call(
        body, name="gla_in_bwd", grid=(lp // tr,),
        in_specs=[row] * 8 + [two, pl.BlockSpec((tr, HP), lambda i: (i, bf)),
                  pl.BlockSpec((tr, HP), lambda i: (i, bb)), pl.BlockSpec((2, HP, 512), lambda i: (0, 0, 0))],
        out_specs=[row, row, row, pl.BlockSpec((tr, 2 * HP), lambda i: (i, 0)),
                   pl.BlockSpec((2, HP, 512), lambda i: (0, 0, 0)), pl.BlockSpec((2, 1, 512), lambda i: (0, 0, 0))],
        out_shape=[jax.ShapeDtypeStruct((lp, 512), BF16)] * 3 + [
            jax.ShapeDtypeStruct((lp, 2 * HP), BF16), jax.ShapeDtypeStruct((2, HP, 512), F32),
            jax.ShapeDtypeStruct((2, 1, 512), F32)],
    )(*gf, *gb, gates, hin, hin, w2p)


def _rope_tables(lp):
    n_tok = lp - NULL - NMETA
    rows = n_tok // GRID_W
    row = jnp.repeat(jnp.arange(rows), GRID_W).astype(F32)
    col = jnp.tile(jnp.arange(GRID_W), rows).astype(F32)
    inv = ROPE_THETA ** (-jnp.arange(0, 32, 2, dtype=F32) / 32)
    ang = jnp.concatenate([row[:, None] * inv, col[:, None] * inv], axis=-1)
    ang = jnp.concatenate([jnp.zeros((NULL + NMETA, 32), F32), ang], axis=0)
    cos, sin = jnp.cos(ang), jnp.sin(ang)
    z16 = jnp.zeros((lp, 16), F32)
    z64 = jnp.zeros((lp, 64), F32)
    c = jnp.concatenate([cos[:, :16], cos[:, :16], cos[:, 16:], cos[:, 16:], z64], axis=1)
    a = jnp.concatenate([-sin[:, :16], z16, -sin[:, 16:], z16, z64], axis=1)
    b = jnp.concatenate([z16, sin[:, :16], z16, sin[:, 16:], z64], axis=1)
    return c, a, b


def _rope(x, c, a, b):
    return x * c + pltpu.roll(x, HP - 16, 1) * a + pltpu.roll(x, 16, 1) * b


def _rope_t(dx, c, a, b):
    return dx * c + pltpu.roll(dx * a, 16, 1) + pltpu.roll(dx * b, HP - 16, 1)


def _attn_prep(hin, gq, gk, tabs):
    lp = hin.shape[0]
    tr = _row_tile(lp)
    qo, ko, vo = P_OFF["qb"] // 1024, P_OFF["kb"] // 256, P_OFF["vb"] // 256

    def body(q_ref, k_ref, v_ref, gq_ref, gk_ref, c_ref, a_ref, b_ref, oq_ref, ok_ref, ov_ref):
        c, a, b = c_ref[...], a_ref[...], b_ref[...]
        for src, g_ref, dst, nh, sc in ((q_ref, gq_ref, oq_ref, ATT_H, Q_SCALE), (k_ref, gk_ref, ok_ref, ATT_KV, 1.0)):
            for h in range(nh):
                sl = slice(h * HP, (h + 1) * HP)
                x = src[:, sl]
                r = lax.rsqrt(jnp.sum(x * x, axis=-1, keepdims=True) * (1.0 / HEAD_DIM) + EPS)
                dst[:, sl] = (_rope(x * r * g_ref[...], c, a, b) * sc).astype(BF16)
        lane = lax.broadcasted_iota(jnp.int32, (1, ATT_KV * HP), 1)
        ov_ref[...] = jnp.where(lane % HP == HEAD_DIM, 1.0, v_ref[...]).astype(BF16)

    tab = pl.BlockSpec((tr, HP), lambda i: (i, 0))
    vec = pl.BlockSpec((1, HP), lambda i: (0, 0))
    return pl.pallas_call(
        body, name="attn_prep", grid=(lp // tr,),
        in_specs=[pl.BlockSpec((tr, 1024), lambda i: (i, qo)), pl.BlockSpec((tr, 256), lambda i: (i, ko)),
                  pl.BlockSpec((tr, 256), lambda i: (i, vo)), vec, vec, tab, tab, tab],
        out_specs=[pl.BlockSpec((tr, 1024), lambda i: (i, 0)), pl.BlockSpec((tr, 256), lambda i: (i, 0)),
                   pl.BlockSpec((tr, 256), lambda i: (i, 0))],
        out_shape=[jax.ShapeDtypeStruct((lp, 1024), BF16), jax.ShapeDtypeStruct((lp, 256), BF16),
                   jax.ShapeDtypeStruct((lp, 256), BF16)],
    )(hin, hin, hin, gq, gk, *tabs)


def _attn_prep_bwd(dqr, dkr, hin, gq, gk, tabs):
    lp = hin.shape[0]
    tr = _row_tile(lp)
    qo, ko = P_OFF["qb"] // 1024, P_OFF["kb"] // 256

    def body(dq_ref, dk_ref, q_ref, k_ref, gq_ref, gk_ref, c_ref, a_ref, b_ref, oq_ref, ok_ref, dgq_ref, dgk_ref):
        i = pl.program_id(0)
        c, a, b = c_ref[...], a_ref[...], b_ref[...]
        for src, dsrc, g_ref, dst, dg_ref, nh, sc in (
                (q_ref, dq_ref, gq_ref, oq_ref, dgq_ref, ATT_H, Q_SCALE),
                (k_ref, dk_ref, gk_ref, ok_ref, dgk_ref, ATT_KV, 1.0)):
            acc = jnp.zeros((1, HP), F32)
            for h in range(nh):
                sl = slice(h * HP, (h + 1) * HP)
                x = src[:, sl]
                r = lax.rsqrt(jnp.sum(x * x, axis=-1, keepdims=True) * (1.0 / HEAD_DIM) + EPS)
                xh = x * r
                dxn = _rope_t(dsrc[:, sl] * sc, c, a, b)
                acc = acc + jnp.sum(dxn * xh, axis=0, keepdims=True)
                dxh = dxn * g_ref[...]
                dx = r * (dxh - xh * (jnp.sum(dxh * xh, axis=-1, keepdims=True) * (1.0 / HEAD_DIM)))
                dst[:, sl] = dx.astype(BF16)

            @pl.when(i == 0)
            def _():
                dg_ref[...] = acc

            @pl.when(i != 0)
            def _():
                dg_ref[...] += acc

    tab = pl.BlockSpec((tr, HP), lambda i: (i, 0))
    vec = pl.BlockSpec((1, HP), lambda i: (0, 0))
    return pl.pallas_call(
        body, name="attn_prep_bwd", grid=(lp // tr,),
        in_specs=[pl.BlockSpec((tr, 1024), lambda i: (i, 0)), pl.BlockSpec((tr, 256), lambda i: (i, 0)),
                  pl.BlockSpec((tr, 1024), lambda i: (i, qo)), pl.BlockSpec((tr, 256), lambda i: (i, ko)),
                  vec, vec, tab, tab, tab],
        out_specs=[pl.BlockSpec((tr, 1024), lambda i: (i, 0)), pl.BlockSpec((tr, 256), lambda i: (i, 0)), vec, vec],
        out_shape=[jax.ShapeDtypeStruct((lp, 1024), BF16), jax.ShapeDtypeStruct((lp, 256), BF16),
                   jax.ShapeDtypeStruct((1, HP), F32), jax.ShapeDtypeStruct((1, HP), F32)],
    )(dqr, dkr, hin, hin, gq, gk, *tabs)


QB = 128
GH = 2
Q_SCALE = HEAD_DIM ** -0.5 * math.log2(math.e)
LN2 = math.log(2.0)


def _stack(ref, g0, n):
    return jnp.concatenate([ref[:, (g0 + g) * HP:(g0 + g + 1) * HP] for g in range(n)], axis=0)


def _attn_fwd(qr, kr, vb):
    lp = qr.shape[0]
    nq = lp // QB

    def body(q_ref, k_ref, v_ref, o_ref, lse_ref):
        qb = pl.program_id(1)
        keys = lax.broadcasted_iota(jnp.int32, (1, lp), 1)
        lane = lax.broadcasted_iota(jnp.int32, (1, HP), 1)
        rows = qb * QB + lax.broadcasted_iota(jnp.int32, (QB, 1), 0)
        for ch in range(ATT_G // GH):
            qs = _stack(q_ref, ch * GH, GH)
            s = _dot(qs, k_ref[...], "nt")
            s = jnp.where(keys >= NULL, s, -1e30)
            m = jnp.max(s, axis=-1, keepdims=True)
            p = jnp.exp2(s - m).astype(BF16)
            o_raw = _dot(p, v_ref[...])
            l = jnp.sum(jnp.where(lane == HEAD_DIM, o_raw, 0.0), axis=-1, keepdims=True)
            o = jnp.where(lane < HEAD_DIM, o_raw / l, 0.0)
            lse = m + jnp.log2(l)
            for g in range(GH):
                sl = slice((ch * GH + g) * HP, (ch * GH + g + 1) * HP)
                o_ref[:, sl] = jnp.where(rows >= NULL, o[g * QB:(g + 1) * QB], 0.0).astype(BF16)
                lse_ref[:, sl] = jnp.broadcast_to(lse[g * QB:(g + 1) * QB], (QB, HP))

    qspec = pl.BlockSpec((QB, ATT_G * HP), lambda kv, qb: (qb, kv))
    kspec = pl.BlockSpec((lp, HP), lambda kv, qb: (0, kv))
    return pl.pallas_call(
        body, name="attn_fwd", grid=(ATT_KV, nq),
        in_specs=[qspec, kspec, kspec], out_specs=[qspec, qspec],
        out_shape=[jax.ShapeDtypeStruct((lp, ATT_H * HP), BF16), jax.ShapeDtypeStruct((lp, ATT_H * HP), F32)],
        compiler_params=_cparams(VMEM_BIG),
    )(qr, kr, vb)


def _attn_bwd(qr, kr, vb, o, lse, do):
    lp = qr.shape[0]
    nq = lp // QB

    def body(q_ref, k_ref, v_ref, o_ref, lse_ref, do_ref, dq_ref, dk_ref, dv_ref):
        qb = pl.program_id(1)

        @pl.when(qb == 0)
        def _():
            dk_ref[...] = jnp.zeros_like(dk_ref)
            dv_ref[...] = jnp.zeros_like(dv_ref)

        keys = lax.broadcasted_iota(jnp.int32, (1, lp), 1)
        k = k_ref[...]
        dk_acc, dv_acc = None, None
        for ch in range(ATT_G // GH):
            g0 = ch * GH
            qs = _stack(q_ref, g0, GH)
            dos = _stack(do_ref, g0, GH)
            os_ = _stack(o_ref, g0, GH).astype(F32)
            lse_s = jnp.concatenate([lse_ref[:, (g0 + g) * HP:(g0 + g) * HP + 1] for g in range(GH)], axis=0)
            delta = jnp.sum(dos * os_, axis=-1, keepdims=True) * LN2
            s = _dot(qs, k, "nt")
            p = jnp.where(keys >= NULL, jnp.exp2(s - lse_s), 0.0)
            dob = dos.astype(BF16)
            dp = _dot((dos * LN2).astype(BF16), v_ref[...], "nt")
            ds = (p * (dp - delta)).astype(BF16)
            dq = _dot(ds, k)
            for g in range(GH):
                dq_ref[:, (g0 + g) * HP:(g0 + g + 1) * HP] = dq[g * QB:(g + 1) * QB]
            dv_c = _dot(p.astype(BF16), dob, "tn")
            dk_c = _dot(ds, qs, "tn")
            dv_acc = dv_c if dv_acc is None else dv_acc + dv_c
            dk_acc = dk_c if dk_acc is None else dk_acc + dk_c
        dv_ref[...] += dv_acc
        dk_ref[...] += dk_acc

    qspec = pl.BlockSpec((QB, ATT_G * HP), lambda kv, qb: (qb, kv))
    kspec = pl.BlockSpec((lp, HP), lambda kv, qb: (0, kv))
    return pl.pallas_call(
        body, name="attn_bwd", grid=(ATT_KV, nq),
        in_specs=[qspec, kspec, kspec, qspec, qspec, qspec], out_specs=[qspec, kspec, kspec],
        out_shape=[jax.ShapeDtypeStruct((lp, ATT_H * HP), F32), jax.ShapeDtypeStruct((lp, ATT_KV * HP), F32),
                   jax.ShapeDtypeStruct((lp, ATT_KV * HP), F32)],
        compiler_params=_cparams(VMEM_BIG),
    )(qr, kr, vb, o, lse, do)


def _mixer_fwd(h, z, wl, l, tabs, next_gain):
    lp = h.shape[0]
    tm = _tm(lp)

    def id_epi(accs, exs, row0):
        return [accs[0]]

    (hin,) = _mm("in_proj", lp, D_INP, [_term(z, wl["win_t"], "nt", 0, (), (l,))], [((lp, D_INP), F32, "mn", 0, ())],
                 id_epi, tm=tm, tn=D_INP // 2, vmem=VMEM_BIG)
    gates, cum = _gla_gates(hin, wl["w2p"][l], wl["b2p"][l])
    o_f, o_b, s_f, s_b = _gla_fwd(hin, cum)
    o2, states = (o_f, o_b), (s_f, s_b)
    a = _gla_out_fwd(o2, hin, wl["gn"][l])
    qr, kr, vb = _attn_prep(hin, wl["gq"][l], wl["gk"][l], tabs)
    b, lse = _attn_fwd(qr, kr, vb)

    def merge_epi(accs, exs, row0):
        pa, pb = accs
        ga, gb, bma, bmb = exs
        y = _sigmoid(ga + bma) * pa + _sigmoid(gb + bmb) * pb
        return [y, pa, pb]

    y, pa, pb = _mm("merge", lp, D, [_term(a, wl["wpa_t"], "nt", 0, (), (l,)), _term(b, wl["wpb_t"], "nt", 1, (), (l,))],
                    [((lp, D), BF16, "mn", 0, ())] * 3, merge_epi,
                    extras=[(hin, "mn", P_OFF["ga"] // D, ()), (hin, "mn", P_OFF["gb"] // D, ()),
                            (wl["bm"], "n", 0, (l, 0)), (wl["bm"], "n", 0, (l, 1))],
                    tm=tm, tn=D, nsub=D // MXU_N, i_outer=True, vmem=VMEM_BIG)

    h2, z2 = _mm("out_proj", lp, D, [_term(y, wl["wout"], "nn", 0, (), (l,))],
                 [((lp, D), F32, "mn", 0, ()), ((lp, D), BF16, "mn", 0, ())], _residual_norm_epi(1.0, True),
                 extras=[(h, "mn", 0, ()), (next_gain, "n", 0, ())], tm=tm, tn=D, i_outer=True, vmem=VMEM_BIG)
    sv = dict(h=h, z=z, hin=hin, gates=gates, cum=cum, o2=o2, states=states, a=a, qr=qr, kr=kr, vb=vb, b=b, lse=lse,
              y=y, pa=pa, pb=pb)
    return h2, z2, sv


def _mixer_bwd(dh, dhb, sv, gain, wl, l, tabs):
    lp = dh.shape[0]
    tm = _tm(lp)
    hin = sv["hin"]

    def merge_bwd_epi(accs, exs, row0):
        dy = accs[0]
        ga, gb, pa, pb, bma, bmb = exs
        sa = _sigmoid(ga + bma)
        sb = _sigmoid(gb + bmb)
        dga = dy * pa.astype(F32) * sa * (1.0 - sa)
        dgb = dy * pb.astype(F32) * sb * (1.0 - sb)
        return [dy * sa, dy * sb, dga, dgb, jnp.sum(dga, axis=0, keepdims=True), jnp.sum(dgb, axis=0, keepdims=True)]

    big = ((lp, D), BF16, "mn", 0, ())
    vec = ((1, D), F32, "nsum", 0, ())
    dpa, dpb, dga, dgb, dbma, dbmb = _mm(
        "merge_bwd", lp, D, [_term(dhb, wl["wout"], "nt", 0, (), (l,))], [big, big, big, big, vec, vec], merge_bwd_epi,
        extras=[(hin, "mn", P_OFF["ga"] // D, ()), (hin, "mn", P_OFF["gb"] // D, ()), (sv["pa"], "mn", 0, ()),
                (sv["pb"], "mn", 0, ()), (wl["bm"], "n", 0, (l, 0)), (wl["bm"], "n", 0, (l, 1))],
        tm=tm, tn=D, nsub=D // MXU_N, vmem=VMEM_BIG)
    d_wout = _dw("dw_out", sv["y"], dhb, D, D)
    d_wpa_t = _dw("dw_pa", dpa, sv["a"], D, 512)
    d_wpb_t = _dw("dw_pb", dpb, sv["b"], D, ATT_H * HP)

    def id_epi(accs, exs, row0):
        return [accs[0]]

    (da,) = _mm("d_a", lp, 512, [_term(dpa, wl["wpa_t"], "nn", 0, (), (l,))], [((lp, 512), F32, "mn", 0, ())], id_epi,
                tm=tm, tn=512, i_outer=True, vmem=VMEM_BIG)
    (db,) = _mm("d_b", lp, ATT_H * HP, [_term(dpb, wl["wpb_t"], "nn", 0, (), (l,))],
                [((lp, ATT_H * HP), F32, "mn", 0, ())], id_epi, tm=tm, tn=512, i_outer=True, vmem=VMEM_BIG)
    d_o, d_ra, d_gn = _gla_out_bwd(da, sv["o2"], hin, wl["gn"][l])
    g_fw, g_bw = _gla_bwd(hin, sv["cum"], sv["states"], d_o)
    d_qa, d_ka, d_va, d_lr, d_w2p, d_b2p = _gla_in_bwd(g_fw, g_bw, sv["gates"], hin, wl["w2p"][l])
    dqr, dkr, dvb = _attn_bwd(sv["qr"], sv["kr"], sv["vb"], sv["b"], sv["lse"], db)
    d_qb, d_kb, d_gq, d_gk = _attn_prep_bwd(dqr, dkr, hin, wl["gq"][l], wl["gk"][l], tabs)
    pieces = dict(qb=d_qb, ga=dga, gb=dgb, qa=d_qa, ka=d_ka, va=d_va, ra=d_ra, kb=d_kb, vb=dvb.astype(BF16), lr=d_lr)
    dhin = jnp.concatenate([pieces[n] for n in ("qb", "ga", "gb", "qa", "ka", "va", "ra", "kb", "vb", "lr")], axis=1)
    d_win_t = _dw("dw_in", dhin, sv["z"], D_INP, D)
    dh2, dhb2, dgain = _mm("in_proj_dz", lp, D, [_term(dhin, wl["win_t"], "nn", 0, (), (l,))], _norm_bwd_outs(lp),
                           _norm_bwd_epi, extras=[(sv["h"], "mn", 0, ()), (dh, "mn", 0, ()), (gain, "n", 0, ())],
                           tm=tm, tn=D, nk=2, vmem=VMEM_BIG)
    grads = dict(gain=dgain, wout=d_wout, wpa_t=d_wpa_t, wpb_t=d_wpb_t, win_t=d_win_t, gn=d_gn, w2p=d_w2p, b2p=d_b2p,
                 gq=d_gq, gk=d_gk, bma=dbma, bmb=dbmb)
    return dh2, dhb2, grads


def _mesh_pos():
    x, y, c = lax.axis_index("x"), lax.axis_index("y"), lax.axis_index("c")
    chips = [(1 - x, y), (x, 1 - y), (1 - x, 1 - y)]
    return x, y, c, chips


def _dev_index(x, y, c):
    return 4 * x + 2 * y + c


def _all_gather(name, shards, leads):
    nt = len(shards)

    def blk(ref, lead, idx):
        return ref.at[(slice(None),) * lead + (idx,)]

    def body(*refs):
        xs, outs = refs[:nt], refs[nt:2 * nt]
        send_sems, recv_sems, local_sems = refs[2 * nt:]
        x, y, c, chips = _mesh_pos()
        me, sibling = (x, y, c), (x, y, 1 - c)

        def copy(t, k, block, to, own=False):
            dst = blk(outs[t], leads[t], _dev_index(*block))
            return pltpu.make_async_remote_copy(
                src_ref=xs[t] if own else dst, dst_ref=dst, send_sem=send_sems.at[t, k], recv_sem=recv_sems.at[t, k],
                device_id=to, device_id_type=MESH)

        locals_ = [pltpu.make_async_copy(xs[t], blk(outs[t], leads[t], _dev_index(*me)), local_sems.at[t])
                   for t in range(nt)]
        for cp in locals_:
            cp.start()
        first = []
        for t in range(nt):
            first.append(copy(t, 0, me, sibling, own=True))
            first += [copy(t, 1 + j, me, (*chip, c), own=True) for j, chip in enumerate(chips)]
        for cp in first:
            cp.start()
        passed = []
        for j, chip in enumerate(chips):
            for t in range(nt):
                copy(t, 1 + j, (*chip, c), me).wait_recv()
                fw = copy(t, 4 + j, (*chip, c), sibling)
                fw.start()
                passed.append(fw)
        for t in range(nt):
            copy(t, 0, sibling, me).wait_recv()
        for j, chip in enumerate(chips):
            for t in range(nt):
                copy(t, 4 + j, (*chip, 1 - c), me).wait_recv()
        for cp in first + passed:
            cp.wait_send()
        for cp in locals_:
            cp.wait()

    out_shape = [jax.ShapeDtypeStruct(s.shape[:ld] + (NDEV,) + s.shape[ld:], s.dtype) for s, ld in zip(shards, leads)]
    return pl.pallas_call(
        body, name=name, in_specs=[ANY] * nt, out_specs=[ANY] * nt, out_shape=out_shape,
        scratch_shapes=[pltpu.SemaphoreType.DMA((nt, 7)), pltpu.SemaphoreType.DMA((nt, 7)),
                        pltpu.SemaphoreType.DMA((nt,))],
    )(*shards)


def _exchange_sibling(name, gs):
    nt = len(gs)

    def body(*refs):
        xs, outs = refs[:nt], refs[nt:2 * nt]
        send_sems, recv_sems = refs[2 * nt:]
        x, y, c, _ = _mesh_pos()
        sibling = (x, y, 1 - c)
        copies = []
        for t in range(nt):
            for ch in range(4):
                copies.append(pltpu.make_async_remote_copy(
                    src_ref=xs[t].at[2 * ch + (1 - c)], dst_ref=outs[t].at[ch],
                    send_sem=send_sems.at[t, ch], recv_sem=recv_sems.at[t, ch],
                    device_id=sibling, device_id_type=MESH))
        for cp in copies:
            cp.start()
        for cp in copies:
            cp.wait()

    out_shape = [jax.ShapeDtypeStruct((4,) + g.shape[1:], g.dtype) for g in gs]
    return pl.pallas_call(
        body, name=name, in_specs=[ANY] * nt, out_specs=[ANY] * nt, out_shape=out_shape,
        scratch_shapes=[pltpu.SemaphoreType.DMA((nt, 4)), pltpu.SemaphoreType.DMA((nt, 4))],
    )(*gs)


def _pair_sum(name, gs, recv):
    c = lax.axis_index("c")
    outs = []
    for t, (g, rv) in enumerate(zip(gs, recv)):
        _, r, cols = rv.shape

        def body(c_ref, g_ref, r_ref, o_ref):
            o_ref[...] = (g_ref[...].astype(F32) + r_ref[...].astype(F32)).astype(o_ref.dtype)

        outs.append(pl.pallas_call(
            body, name=f"{name}_{t}",
            grid_spec=pltpu.PrefetchScalarGridSpec(
                num_scalar_prefetch=1, grid=(4,),
                in_specs=[pl.BlockSpec((None, r, cols), lambda ch, cr: (2 * ch + cr[0], 0, 0)),
                          pl.BlockSpec((None, r, cols), lambda ch, cr: (ch, 0, 0))],
                out_specs=pl.BlockSpec((None, r, cols), lambda ch, cr: (ch, 0, 0))),
            out_shape=jax.ShapeDtypeStruct(rv.shape, rv.dtype),
        )(jnp.reshape(c, (1,)).astype(jnp.int32), g, rv))
    return outs


def _exchange_chips(name, ps):
    nt = len(ps)

    def body(*refs):
        xs, outs = refs[:nt], refs[nt:2 * nt]
        send_sems, recv_sems = refs[2 * nt:]
        x, y, c, chips = _mesh_pos()
        copies = []
        for t in range(nt):
            for j, (cx, cy) in enumerate(chips):
                copies.append(pltpu.make_async_remote_copy(
                    src_ref=xs[t].at[2 * cx + cy], dst_ref=outs[t].at[j],
                    send_sem=send_sems.at[t, j], recv_sem=recv_sems.at[t, j],
                    device_id=(cx, cy, c), device_id_type=MESH))
        for cp in copies:
            cp.start()
        for cp in copies:
            cp.wait()

    out_shape = [jax.ShapeDtypeStruct((3,) + p.shape[1:], p.dtype) for p in ps]
    return pl.pallas_call(
        body, name=name, in_specs=[ANY] * nt, out_specs=[ANY] * nt, out_shape=out_shape,
        scratch_shapes=[pltpu.SemaphoreType.DMA((nt, 3)), pltpu.SemaphoreType.DMA((nt, 3))],
    )(*ps)


def _final_sum(name, ps, recv):
    chip = 2 * lax.axis_index("x") + lax.axis_index("y")
    outs = []
    for t, (p, rv) in enumerate(zip(ps, recv)):
        _, r, cols = rv.shape

        def body(c_ref, p_ref, r0_ref, r1_ref, r2_ref, o_ref):
            o_ref[...] = ((p_ref[...].astype(F32) + r0_ref[...].astype(F32)) + r1_ref[...].astype(F32)) + r2_ref[...].astype(F32)

        outs.append(pl.pallas_call(
            body, name=f"{name}_{t}",
            grid_spec=pltpu.PrefetchScalarGridSpec(
                num_scalar_prefetch=1, grid=(1,),
                in_specs=[pl.BlockSpec((None, r, cols), lambda i, cr: (cr[0], 0, 0))] +
                         [pl.BlockSpec((None, r, cols), lambda i, cr, j=j: (j, 0, 0)) for j in range(3)],
                out_specs=pl.BlockSpec((r, cols), lambda i, cr: (0, 0))),
            out_shape=jax.ShapeDtypeStruct((r, cols), F32),
        )(jnp.reshape(chip, (1,)).astype(jnp.int32), p, rv, rv, rv))
    return outs


def _reduce_scatter(tag, gs):
    recv1 = _exchange_sibling(f"rs_sibling_{tag}", gs)
    ps = _pair_sum(f"rs_pair_{tag}", gs, recv1)
    recv2 = _exchange_chips(f"rs_chips_{tag}", ps)
    return _final_sum(f"rs_sum_{tag}", ps, recv2)


def _sum_gathered(g):
    _, r, cols = g.shape

    def body(g_ref, o_ref):
        acc = g_ref[0]
        for d in range(1, NDEV):
            acc = acc + g_ref[d]
        o_ref[...] = acc

    return pl.pallas_call(body, name="small_sum", out_shape=jax.ShapeDtypeStruct((r, cols), F32))(g)


HBM = pl.BlockSpec(memory_space=pltpu.HBM)
SEM = pl.BlockSpec(memory_space=pltpu.SEMAPHORE)
EFFECT = pltpu.SideEffectType.DATAFLOW_SIDE_EFFECTING
NREL = NDEV - 1


def _related(k):
    x, y, c = lax.axis_index("x"), lax.axis_index("y"), lax.axis_index("c")
    px = 1 - x if k & 4 else x
    py = 1 - y if k & 2 else y
    pc = 1 - c if k & 1 else c
    return (px, py, pc), _dev_index(px, py, pc)


def _in_hbm(a):
    return pltpu.with_memory_space_constraint(a, pltpu.HBM)


def _split_copies(xs, lands, send_sems, recv_sems, src_of, dst_of):
    copies = []
    for t in range(len(xs)):
        for k in range(1, NDEV):
            peer, peer_idx = _related(k)
            copies.append(pltpu.make_async_remote_copy(
                src_ref=src_of(xs[t], t, peer_idx), dst_ref=dst_of(lands[t], t, k, peer_idx),
                send_sem=send_sems.at[t * NREL + k - 1], recv_sem=recv_sems.at[t * NREL + k - 1],
                device_id=peer, device_id_type=MESH))
    return copies


def _exchange_start(name, xs, lands, src_of, dst_of, after):
    nt = len(xs)

    def body(*refs):
        x_refs, land_refs = refs[:nt], refs[nt:2 * nt]
        send_sems, recv_sems = refs[2 * nt + 1], refs[2 * nt + 2]
        token = refs[-1]
        for cp in _split_copies(x_refs, land_refs, send_sems, recv_sems, src_of, dst_of):
            cp.start()
        token[...] = jnp.zeros_like(token)

    res = pl.pallas_call(
        body, name=name,
        out_shape=(pltpu.SemaphoreType.DMA((nt * NREL,)), pltpu.SemaphoreType.DMA((nt * NREL,)),
                   *[pltpu.HBM(a.shape, a.dtype) for a in xs], *[pltpu.HBM(a.shape, a.dtype) for a in lands],
                   jax.ShapeDtypeStruct((8, 128), F32)),
        in_specs=[HBM] * (2 * nt) + [ANY],
        out_specs=(SEM, SEM, *[HBM] * (2 * nt), pl.BlockSpec(memory_space=pltpu.VMEM)),
        input_output_aliases={i: 2 + i for i in range(2 * nt)},
        compiler_params=pltpu.CompilerParams(has_side_effects=EFFECT),
    )(*[_in_hbm(a) for a in xs], *[_in_hbm(a) for a in lands], after)
    return res[0], res[1], res[2:2 + nt], res[2 + nt:2 + 2 * nt], res[-1]


def _exchange_wait(name, send_sems, recv_sems, xs, lands, src_of, dst_of, after):
    nt = len(xs)

    def body(*refs):
        x_refs, land_refs = refs[:nt], refs[nt:2 * nt]
        send_sems, recv_sems = refs[2 * nt], refs[2 * nt + 1]
        for cp in _split_copies(x_refs, land_refs, send_sems, recv_sems, src_of, dst_of):
            cp.wait_send()
            cp.wait_recv()

    res = pl.pallas_call(
        body, name=name,
        out_shape=(*[pltpu.HBM(a.shape, a.dtype) for a in xs], *[pltpu.HBM(a.shape, a.dtype) for a in lands]),
        in_specs=[HBM] * (2 * nt) + [SEM, SEM, ANY], out_specs=tuple([HBM] * (2 * nt)),
        input_output_aliases={i: i for i in range(2 * nt)},
        compiler_params=pltpu.CompilerParams(has_side_effects=EFFECT),
    )(*xs, *lands, send_sems, recv_sems, after)
    return res[:nt], res[nt:]


def _gather_start(name, shards, leads, after):
    def src_of(x_ref, t, peer_idx):
        return x_ref

    def dst_of(land_ref, t, k, peer_idx):
        me = _dev_index(lax.axis_index("x"), lax.axis_index("y"), lax.axis_index("c"))
        return land_ref.at[(slice(None),) * leads[t] + (me,)]

    lands = [lax.empty(s.shape[:ld] + (NDEV,) + s.shape[ld:], s.dtype) for s, ld in zip(shards, leads)]
    return _exchange_start(name, shards, lands, src_of, dst_of, after)


def _gather_wait(name, started, leads, after):
    send_sems, recv_sems, shards, lands, _ = started

    def src_of(x_ref, t, peer_idx):
        return x_ref

    def dst_of(land_ref, t, k, peer_idx):
        return land_ref.at[(slice(None),) * leads[t] + (peer_idx,)]

    shards, lands = _exchange_wait(name, send_sems, recv_sems, shards, lands, src_of, dst_of, after)
    me = _dev_index(lax.axis_index("x"), lax.axis_index("y"), lax.axis_index("c"))
    return [lax.dynamic_update_index_in_dim(g, s, me, ld) for g, s, ld in zip(lands, shards, leads)]


def _scatter_src(x_ref, t, peer_idx):
    return x_ref.at[peer_idx]


def _scatter_dst(land_ref, t, k, peer_idx):
    return land_ref.at[k - 1]


def _scatter_start(name, gs, after):
    lands = [lax.empty((NREL,) + g.shape[1:], g.dtype) for g in gs]
    return _exchange_start(name, gs, lands, _scatter_src, _scatter_dst, after)


def _scatter_wait(name, started, after):
    send_sems, recv_sems, gs, lands, _ = started
    gs, lands = _exchange_wait(name, send_sems, recv_sems, gs, lands, _scatter_src, _scatter_dst, after)
    me = _dev_index(lax.axis_index("x"), lax.axis_index("y"), lax.axis_index("c"))
    outs = []
    for t, (g, rv) in enumerate(zip(gs, lands)):
        _, r, cols = rv.shape

        def body(c_ref, own_ref, rv_ref, o_ref):
            acc = own_ref[...].astype(F32)
            for k in range(NREL):
                acc = acc + rv_ref[k].astype(F32)
            o_ref[...] = acc

        outs.append(pl.pallas_call(
            body, name=f"{name}_sum_{t}",
            grid_spec=pltpu.PrefetchScalarGridSpec(
                num_scalar_prefetch=1, grid=(1,),
                in_specs=[pl.BlockSpec((None, r, cols), lambda i, cr: (cr[0], 0, 0)),
                          pl.BlockSpec((NREL, r, cols), lambda i, cr: (0, 0, 0))],
                out_specs=pl.BlockSpec((r, cols), lambda i, cr: (0, 0))),
            out_shape=jax.ShapeDtypeStruct((r, cols), F32), compiler_params=_cparams(VMEM_BIG),
        )(jnp.reshape(me, (1,)).astype(jnp.int32), g, rv))
    return outs


def _adamw(w, g, m, v):
    shape = w.shape
    cols = shape[-1]
    rows = math.prod(shape[:-1]) if len(shape) > 1 else 1
    w2, g2, m2, v2 = (jnp.reshape(t, (rows, cols)) for t in (w, g, m, v))
    tr = _pick(rows, (512, 256, 128)) if rows * cols > 65536 else rows
    c1 = 1.0 / (1.0 - ADAM_B1 ** ADAM_STEP)
    c2 = 1.0 / (1.0 - ADAM_B2 ** ADAM_STEP)

    def body(w_ref, g_ref, m_ref, v_ref, d_ref, nm_ref, nv_ref):
        gv = g_ref[...]
        nm = ADAM_B1 * m_ref[...] + (1.0 - ADAM_B1) * gv
        nv = ADAM_B2 * v_ref[...] + (1.0 - ADAM_B2) * (gv * gv)
        d_ref[...] = -ADAM_LR * ((nm * c1) / (jnp.sqrt(nv * c2) + ADAM_EPS) + ADAM_WD * w_ref[...])
        nm_ref[...] = nm
        nv_ref[...] = nv

    spec = pl.BlockSpec((tr, cols), lambda i: (i, 0))
    osh = jax.ShapeDtypeStruct((rows, cols), F32)
    d, nm, nv = pl.pallas_call(
        body, name="adamw", grid=(rows // tr,), in_specs=[spec] * 4, out_specs=[spec] * 3, out_shape=[osh] * 3,
    )(w2, g2, m2, v2)
    return jnp.reshape(d, shape), jnp.reshape(nm, shape), jnp.reshape(nv, shape)


def _pad_heads(w, name):
    if name not in P_HEADS:
        return w
    nh, real = P_HEADS[name]
    w = jnp.reshape(w, w.shape[:-2] + (nh, real, w.shape[-1]))
    w = jnp.pad(w, [(0, 0)] * (w.ndim - 2) + [(0, HP - real), (0, 0)])
    return jnp.reshape(w, w.shape[:-3] + (nh * HP, w.shape[-1]))


def _unpad_heads(w, name):
    if name not in P_HEADS:
        return w
    nh, real = P_HEADS[name]
    w = jnp.reshape(w, w.shape[:-2] + (nh, HP, w.shape[-1]))[..., :real, :]
    return jnp.reshape(w, w.shape[:-3] + (nh * real, w.shape[-1]))


def _win_pad(win_t):
    segs, o = {}, 0
    for n, s in zip(IN_NAMES, IN_SIZES):
        segs[n] = win_t[..., o:o + s, :]
        o += s
    return jnp.concatenate([_pad_heads(segs[n], n) for n in P_ORDER], axis=-2)


def _win_unpad(win_p):
    segs = {n: _unpad_heads(win_p[..., P_OFF[n]:P_OFF[n] + P_WIDTH[n], :], n) for n in P_ORDER}
    return jnp.concatenate([segs[n] for n in IN_NAMES], axis=-2)


def _t(w):
    return jnp.swapaxes(w, -1, -2)


def _layer_weights(g_g, g_u, g_d, g_in, g_pa, g_pb, g_out, gains, w2, b2, bm, gn, gq, gk):
    w2p = jnp.pad(jnp.reshape(w2, (2, GLA_RANK, GLA_H, GLA_DK)), ((0, 0), (0, HP - GLA_RANK), (0, 0), (0, HP - GLA_DK)))
    b2p = jnp.pad(jnp.reshape(b2, (2, 1, GLA_H, GLA_DK)), ((0, 0), (0, 0), (0, 0), (0, HP - GLA_DK)))
    wpb_t = jnp.pad(jnp.reshape(g_pb, (D, ATT_H, HEAD_DIM)), ((0, 0), (0, 0), (0, HP - HEAD_DIM)))
    return dict(
        gains=jnp.reshape(gains, (1, 3, 1, D)),
        wg_t=jnp.reshape(g_g, (1, 2, DFF, D)), wu_t=jnp.reshape(g_u, (1, 2, DFF, D)), wd=jnp.reshape(g_d, (1, 2, DFF, D)),
        win_t=_win_pad(jnp.reshape(g_in, (1, D_IN, D))), wpa_t=jnp.reshape(g_pa, (1, D, 512)),
        wpb_t=jnp.reshape(wpb_t, (1, D, ATT_H * HP)), wout=jnp.reshape(g_out, (1, D, D)),
        w2p=jnp.reshape(w2p, (1, 2, HP, GLA_H * HP)), b2p=jnp.reshape(b2p, (1, 2, 1, GLA_H * HP)),
        bm=jnp.reshape(bm, (1, 2, 1, D)), gn=jnp.reshape(gn, (1, 1, GLA_H * HP)),
        gq=jnp.pad(jnp.reshape(gq, (1, 1, HEAD_DIM)), ((0, 0), (0, 0), (0, HP - HEAD_DIM))),
        gk=jnp.pad(jnp.reshape(gk, (1, 1, HEAD_DIM)), ((0, 0), (0, 0), (0, HP - HEAD_DIM))))


def _layer_fwd(h, z, w, tabs, next_gain):
    h, z, s0 = _ffn_fwd(h, z, w["wg_t"], w["wu_t"], w["wd"], (0, 0), w["gains"][0, 1])
    h, z, s1 = _mixer_fwd(h, z, w, 0, tabs, w["gains"][0, 2])
    h, z, s2 = _ffn_fwd(h, z, w["wg_t"], w["wu_t"], w["wd"], (0, 1), next_gain)
    return h, z, (s0, s1, s2)


def _layer_bwd_upper(dh, dhb, saved, w, tabs):
    _, s1, s2 = saved
    dh, dhb, dg2, dwg1, dwu1, dwd1 = _ffn_bwd(dh, dhb, s2, w["gains"][0, 2], w["wg_t"], w["wu_t"], w["wd"], (0, 1))
    dh, dhb, gm = _mixer_bwd(dh, dhb, s1, w["gains"][0, 1], w, 0, tabs)
    gm.update(gain2=dg2, wg1=dwg1, wu1=dwu1, wd1=dwd1)
    return dh, dhb, gm


def _layer_bwd_lower(dh, dhb, saved, w, gm):
    dh, dhb, dg0, dwg0, dwu0, dwd0 = _ffn_bwd(dh, dhb, saved[0], w["gains"][0, 0], w["wg_t"], w["wu_t"], w["wd"], (0, 0))
    gm.update(gain0=dg0, wg0=dwg0, wu0=dwu0, wd0=dwd0)
    return dh, dhb, gm


def _layer_bwd(dh, dhb, saved, w, tabs):
    dh, dhb, gm = _layer_bwd_upper(dh, dhb, saved, w, tabs)
    return _layer_bwd_lower(dh, dhb, saved, w, gm)


def _blocks(ts):
    return [jnp.reshape(t, (NDEV, t.shape[0] // NDEV, t.shape[1])) for t in ts]


def _upper_grads(g):
    d_in = _win_unpad(g["win_t"])
    d_pb = jnp.reshape(jnp.reshape(g["wpb_t"], (D, ATT_H, HP))[:, :, :HEAD_DIM], (D, 512))
    return _blocks([g["wg1"], g["wu1"], g["wd1"], d_in, g["wpa_t"], d_pb, g["wout"]])


def _lower_grads(g):
    return _blocks([g["wg0"], g["wu0"], g["wd0"]])


def _big_grads(g):
    return _lower_grads(g) + _upper_grads(g)


def kernel(x, meta_tokens, norm_gains, ffn_w_gate, ffn_w_up, ffn_w_down, w_in, gla_w2, gla_b2, gla_gn, q_norm, k_norm, w_pa, w_pb, b_merge, w_out, final_norm, loss_target, m_meta_tokens, m_norm_gains, m_ffn_w_gate, m_ffn_w_up, m_ffn_w_down, m_w_in, m_gla_w2, m_gla_b2, m_gla_gn, m_q_norm, m_k_norm, m_w_pa, m_w_pb, m_b_merge, m_w_out, m_final_norm, v_meta_tokens, v_norm_gains, v_ffn_w_gate, v_ffn_w_up, v_ffn_w_down, v_w_in, v_gla_w2, v_gla_b2, v_gla_gn, v_q_norm, v_k_norm, v_w_pa, v_w_pb, v_b_merge, v_w_out, v_final_norm):
    dev = _dev_index(lax.axis_index("x"), lax.axis_index("y"), lax.axis_index("c"))
    sh_g = _t(ffn_w_gate).astype(BF16)
    sh_u = _t(ffn_w_up).astype(BF16)
    sh_d = ffn_w_down.astype(BF16)
    sh_in = _t(w_in).astype(BF16)
    sh_pa = _t(w_pa).astype(BF16)
    sh_pb = _t(w_pb).astype(BF16)
    sh_out = w_out.astype(BF16)
    small = jnp.concatenate([jnp.reshape(t, (-1, 128)) for t in
                             (meta_tokens, norm_gains, gla_w2, gla_b2, b_merge)], axis=0)
    small = jnp.pad(small, ((0, 2), (0, 0)))
    def shards(l):
        return [sh_g[l], sh_u[l], sh_d[l], sh_in[l], sh_pa[l], sh_pb[l], sh_out[l]]

    w_leads = [1, 1, 1, 0, 0, 0, 0]
    *g0, g_small = _all_gather("gather_layer0", shards(0) + [small], w_leads + [0])
    started = {1: _gather_start("gather_start_1", shards(1), w_leads, g_small)}
    meta_full = jnp.reshape(jnp.transpose(g_small[:, 0:16], (1, 0, 2)), (NMETA, D)) + started[1][4][0, 0]
    gains_full = jnp.reshape(jnp.transpose(jnp.reshape(g_small[:, 16:28], (NDEV, DEPTH, 3, 128)), (1, 2, 0, 3)), (DEPTH, 3, D))
    w2_full = jnp.reshape(jnp.transpose(jnp.reshape(g_small[:, 28:60], (NDEV, DEPTH, 2, GLA_RANK, 32)), (1, 2, 3, 0, 4)),
                          (DEPTH, 2, GLA_RANK, 256))
    b2_full = jnp.reshape(jnp.transpose(jnp.reshape(g_small[:, 60:62], (NDEV, DEPTH, 2, 32)), (1, 2, 0, 3)), (DEPTH, 2, 256))
    bm_full = jnp.reshape(jnp.transpose(jnp.reshape(g_small[:, 62:70], (NDEV, DEPTH, 2, 128)), (1, 2, 0, 3)), (DEPTH, 2, D))

    def layer_weights(l, gathered, gains_l):
        return _layer_weights(*gathered, gains_l, w2_full[l], b2_full[l], bm_full[l], gla_gn[l], q_norm[l], k_norm[l])

    xl = x[0]
    lp = xl.shape[0] + NULL + NMETA
    tabs = _rope_tables(lp)
    h = jnp.concatenate([jnp.zeros((NULL, D), F32), meta_full, xl], axis=0)
    weights, saved = [], []
    z = _rmsnorm_fwd(h, jnp.reshape(gains_full[0, 0], (1, D)))
    for l in range(DEPTH):
        tok = jnp.zeros((), F32)
        if 1 <= l < DEPTH - 1:
            started[l + 1] = _gather_start(f"gather_start_{l + 1}", shards(l + 1), w_leads, h)
            tok = started[l + 1][4][0, 0]
        gathered = g0 if l == 0 else _gather_wait(f"gather_wait_{l}", started[l], w_leads, h)
        weights.append(layer_weights(l, gathered, gains_full[l] + tok))
        next_gain = jnp.reshape(gains_full[l + 1, 0], (1, D)) if l + 1 < DEPTH else None
        h, z, sv = _layer_fwd(h, z, weights[l], tabs, next_gain)
        saved.append(sv)
    loss, dh, dhb, d_final = _loss_head(h, loss_target[0], jnp.reshape(final_norm, (1, D)))
    loss = lax.psum(loss[0, 0], ("x", "y", "c"))

    grads, scattering = [None] * DEPTH, {}
    tok = jnp.zeros((), F32)
    for l in reversed(range(DEPTH)):
        w = dict(weights[l], gains=weights[l]["gains"] + tok)
        if l > 0:
            dh, dhb, grads[l] = _layer_bwd(dh, dhb, saved[l], w, tabs)
            scattering[l] = _scatter_start(f"scatter_start_{l}", _big_grads(grads[l]), dhb)
            tok = scattering[l][4][0, 0]
        else:
            dh, dhb, gm = _layer_bwd_upper(dh, dhb, saved[l], w, tabs)
            scattering[l] = _scatter_start(f"scatter_start_{l}", _upper_grads(gm), dhb)
            w = dict(w, gains=w["gains"] + scattering[l][4][0, 0])
            dh, dhb, grads[l] = _layer_bwd_lower(dh, dhb, saved[l], w, gm)
    grad_x = dh[NULL + NMETA:][None]
    red = [None] * DEPTH
    for l in reversed(range(1, DEPTH)):
        red[l] = _scatter_wait(f"scatter_wait_{l}", scattering[l], dhb)
    red_lower = _reduce_scatter("0", _lower_grads(grads[0]))
    red[0] = red_lower + _scatter_wait("scatter_wait_0", scattering[0], red_lower[0])
    g_gate = jnp.stack([jnp.stack([_t(red[l][0]), _t(red[l][3])]) for l in range(DEPTH)])
    g_up = jnp.stack([jnp.stack([_t(red[l][1]), _t(red[l][4])]) for l in range(DEPTH)])
    g_down = jnp.stack([jnp.stack([red[l][2], red[l][5]]) for l in range(DEPTH)])
    g_win = jnp.stack([_t(red[l][6]) for l in range(DEPTH)])
    g_wpa = jnp.stack([_t(red[l][7]) for l in range(DEPTH)])
    g_wpb = jnp.stack([_t(red[l][8]) for l in range(DEPTH)])
    g_wout = jnp.stack([red[l][9] for l in range(DEPTH)])

    d_meta = dh[NULL:NULL + NMETA]
    d_gains = jnp.stack([jnp.concatenate([grads[l]["gain0"], grads[l]["gain"], grads[l]["gain2"]], axis=0)
                         for l in range(DEPTH)])
    d_w2 = jnp.stack([jnp.reshape(jnp.reshape(grads[l]["w2p"], (2, HP, GLA_H, HP))[:, :GLA_RANK, :, :GLA_DK],
                                  (2, GLA_RANK, 256)) for l in range(DEPTH)])
    d_b2 = jnp.stack([jnp.reshape(jnp.reshape(grads[l]["b2p"], (2, GLA_H, HP))[:, :, :GLA_DK], (2, 256))
                      for l in range(DEPTH)])
    d_gn = jnp.stack([grads[l]["gn"][0] for l in range(DEPTH)])
    d_gq = jnp.stack([grads[l]["gq"][0, :HEAD_DIM] for l in range(DEPTH)])
    d_gk = jnp.stack([grads[l]["gk"][0, :HEAD_DIM] for l in range(DEPTH)])
    d_bm = jnp.stack([jnp.concatenate([grads[l]["bma"], grads[l]["bmb"]], axis=0) for l in range(DEPTH)])
    parts = [d_meta, d_gains, d_w2, d_b2, d_gn, d_gq, d_gk, d_bm, d_final[0]]
    sizes = [p.size for p in parts]
    flat = jnp.concatenate([jnp.reshape(p, (-1,)) for p in parts])
    flat = jnp.reshape(flat, (-1, 128))
    nrow = flat.shape[0]
    flat = jnp.pad(flat, ((0, (-nrow) % 8), (0, 0)))
    (g_flat,) = _all_gather("gather_small_grads", [flat], [0])
    tot = jnp.reshape(_sum_gathered(g_flat), (-1,))
    full, o = [], 0
    for p, s in zip(parts, sizes):
        full.append(jnp.reshape(tot[o:o + s], p.shape))
        o += s
    f_meta, f_gains, f_w2, f_b2, f_gn, f_gq, f_gk, f_bm, f_final = full

    def mine(t, width):
        return lax.dynamic_slice_in_dim(t, dev * width, width, axis=t.ndim - 1)

    g_small = dict(meta_tokens=mine(f_meta, 128), norm_gains=mine(f_gains, 128), gla_w2=mine(f_w2, 32),
                   gla_b2=mine(f_b2, 32), gla_gn=f_gn, q_norm=f_gq, k_norm=f_gk, b_merge=mine(f_bm, 128),
                   final_norm=f_final)
    gr = dict(g_small, ffn_w_gate=g_gate, ffn_w_up=g_up, ffn_w_down=g_down, w_in=g_win, w_pa=g_wpa, w_pb=g_wpb,
              w_out=g_wout)
    ws = dict(meta_tokens=meta_tokens, norm_gains=norm_gains, ffn_w_gate=ffn_w_gate, ffn_w_up=ffn_w_up,
              ffn_w_down=ffn_w_down, w_in=w_in, gla_w2=gla_w2, gla_b2=gla_b2, gla_gn=gla_gn, q_norm=q_norm,
              k_norm=k_norm, w_pa=w_pa, w_pb=w_pb, b_merge=b_merge, w_out=w_out, final_norm=final_norm)
    ms = dict(meta_tokens=m_meta_tokens, norm_gains=m_norm_gains, ffn_w_gate=m_ffn_w_gate, ffn_w_up=m_ffn_w_up,
              ffn_w_down=m_ffn_w_down, w_in=m_w_in, gla_w2=m_gla_w2, gla_b2=m_gla_b2, gla_gn=m_gla_gn, q_norm=m_q_norm,
              k_norm=m_k_norm, w_pa=m_w_pa, w_pb=m_w_pb, b_merge=m_b_merge, w_out=m_w_out, final_norm=m_final_norm)
    vs = dict(meta_tokens=v_meta_tokens, norm_gains=v_norm_gains, ffn_w_gate=v_ffn_w_gate, ffn_w_up=v_ffn_w_up,
              ffn_w_down=v_ffn_w_down, w_in=v_w_in, gla_w2=v_gla_w2, gla_b2=v_gla_b2, gla_gn=v_gla_gn, q_norm=v_q_norm,
              k_norm=v_k_norm, w_pa=v_w_pa, w_pb=v_w_pb, b_merge=v_b_merge, w_out=v_w_out, final_norm=v_final_norm)
    names = ["meta_tokens", "norm_gains", "ffn_w_gate", "ffn_w_up", "ffn_w_down", "w_in", "gla_w2", "gla_b2", "gla_gn",
             "q_norm", "k_norm", "w_pa", "w_pb", "b_merge", "w_out", "final_norm"]
    deltas, new_m, new_v = [], [], []
    for n in names:
        dlt, nm, nv = _adamw(ws[n], gr[n], ms[n], vs[n])
        deltas.append(dlt)
        new_m.append(nm)
        new_v.append(nv)
    return (loss, grad_x, *[gr[n] for n in names], *deltas, *new_m, *new_v)
```

```python
import functools
import math

import jax
import jax.numpy as jnp
from jax import lax
from jax.experimental import pallas as pl
from jax.experimental.pallas import tpu as pltpu

F32 = jnp.float32
BF16 = jnp.bfloat16
MESH = pl.DeviceIdType.MESH
ANY = pl.BlockSpec(memory_space=pl.ANY)

NDEV = 8
D = 1024
DFF = 2816
DEPTH = 4
NMETA = 16
NULL = 112
GRID_W = 64
EPS = 1e-6
HP = 128
GLA_H = 4
GLA_DK = 64
GLA_RANK = 16
GLA_TAU = 16.0
CHUNK = 64
ATT_H = 8
ATT_KV = 2
ATT_G = ATT_H // ATT_KV
HEAD_DIM = 64
ROPE_THETA = 10000.0

IN_SIZES = (256, 256, 512, 512, 16, 16, 512, 128, 128, 1024, 1024)
IN_NAMES = ("qa", "ka", "va", "ra", "lrf", "lrb", "qb", "kb", "vb", "ga", "gb")
D_IN = sum(IN_SIZES)
P_ORDER = ("qb", "ga", "gb", "qa", "ka", "va", "ra", "kb", "vb", "lrf", "lrb")
P_WIDTH = dict(qb=1024, ga=1024, gb=1024, qa=512, ka=512, va=512, ra=512, kb=256, vb=256, lrf=128, lrb=128)
P_OFF = {}
_o = 0
for _n in P_ORDER:
    P_OFF[_n] = _o
    _o += P_WIDTH[_n]
D_INP = _o
P_HEADS = dict(qa=(4, 64), ka=(4, 64), qb=(8, 64), kb=(2, 64), vb=(2, 64), lrf=(1, 16), lrb=(1, 16))

ADAM_LR = 0.001
ADAM_B1 = 0.9
ADAM_B2 = 0.999
ADAM_EPS = 1e-08
ADAM_WD = 0.01
ADAM_STEP = 10

VMEM_BIG = 56 * 1024 * 1024
MXU_N = 256


def _cparams(vmem=None):
    return pltpu.CompilerParams(vmem_limit_bytes=vmem) if vmem else pltpu.CompilerParams()


def _pick(n, prefs):
    for p in prefs:
        if n % p == 0:
            return p
    return n


def _tm(lp):
    return _pick(lp, (528, 512, 256, 128))


_DN = {"nn": (((1,), (0,)), ((), ())), "nt": (((1,), (1,)), ((), ())), "tn": (((0,), (0,)), ((), ()))}


def _dot(a, b, mode="nn", precision=None):
    return lax.dot_general(a, b, _DN[mode], preferred_element_type=F32, precision=precision)


def _split(x):
    hi = x.astype(BF16)
    return hi, (x - hi.astype(F32)).astype(BF16)


def _dot_sel(t, x, mode="nn"):
    hi, lo = _split(x)
    return _dot(t, hi, mode) + _dot(t, lo, mode)


def _dot3(a, b, mode="nn"):
    ah, al = _split(a)
    bh, bl = _split(b)
    return _dot(ah, bh, mode) + (_dot(ah, bl, mode) + _dot(al, bh, mode))


def _sigmoid(x):
    return 0.5 * jnp.tanh(0.5 * x) + 0.5


def _mm(name, m, n, terms, outs, epilogue, extras=(), *, tm, tn, nk=1, nsub=1, i_outer=False, vmem=None):
    gm, gn = m // tm, n // tn
    assert gm * tm == m and gn * tn == n, (name, m, n, tm, tn)
    n_acc = 1 + max(t[3] for t in terms)

    def gmap(f):
        if i_outer:
            return lambda i, j, kk: f(i, j, kk)
        return lambda j, i, kk: f(i, j, kk)

    in_specs, args = [], []
    for a, b, mode, _, pa, pb in terms:
        kdim = a.shape[-2] if mode == "tn" else a.shape[-1]
        tk = kdim // nk
        assert tk * nk == kdim
        na, nb = (None,) * len(pa), (None,) * len(pb)
        if mode == "tn":
            in_specs.append(pl.BlockSpec(na + (tk, tm), gmap(lambda i, j, kk, pa=pa: pa + (kk, i))))
        else:
            in_specs.append(pl.BlockSpec(na + (tm, tk), gmap(lambda i, j, kk, pa=pa: pa + (i, kk))))
        if mode == "nt":
            in_specs.append(pl.BlockSpec(nb + (tn, tk), gmap(lambda i, j, kk, pb=pb: pb + (j, kk))))
        else:
            in_specs.append(pl.BlockSpec(nb + (tk, tn), gmap(lambda i, j, kk, pb=pb: pb + (kk, j))))
        args += [a, b]
    for arr, kind, off, pe in extras:
        ne = (None,) * len(pe)
        if kind == "mn":
            in_specs.append(pl.BlockSpec(ne + (tm, tn), gmap(lambda i, j, kk, off=off, pe=pe: pe + (i, j + off))))
        else:
            in_specs.append(pl.BlockSpec(ne + (1, tn), gmap(lambda i, j, kk, off=off, pe=pe: pe + (0, j + off))))
        args.append(arr)
    out_shape, out_specs = [], []
    for shape, dtype, kind, off, po in outs:
        no = (None,) * len(po)
        out_shape.append(jax.ShapeDtypeStruct(shape, dtype))
        if kind == "mn":
            out_specs.append(pl.BlockSpec(no + (tm, tn), gmap(lambda i, j, kk, off=off, po=po: po + (i, j + off))))
        else:
            assert not i_outer
            out_specs.append(pl.BlockSpec(no + (1, tn), gmap(lambda i, j, kk, off=off, po=po: po + (0, j + off))))
    n_t, n_e, n_o = len(terms), len(extras), len(outs)
    i_axis = 0 if i_outer else 1

    def body(*refs):
        ins = refs[: 2 * n_t]
        exs = refs[2 * n_t: 2 * n_t + n_e]
        ors = refs[2 * n_t + n_e: 2 * n_t + n_e + n_o]
        accs = refs[2 * n_t + n_e + n_o:]
        i = pl.program_id(i_axis)
        kk = pl.program_id(2)

        def partials(cs):
            part = [None] * n_acc
            for t, (_, _, mode, ai, _, _) in enumerate(terms):
                b_ref = ins[2 * t + 1]
                b_val = b_ref[cs, :] if mode == "nt" else b_ref[:, cs]
                p = _dot(ins[2 * t][...], b_val, mode)
                part[ai] = p if part[ai] is None else part[ai] + p
            return part

        def finish(vals, cs):
            res = epilogue(vals, [e[:, cs] for e in exs], i * tm)
            for (_, dtype, kind, _, _), o_ref, v in zip(outs, ors, res):
                if kind == "mn":
                    o_ref[:, cs] = v.astype(dtype)
                else:
                    @pl.when(i == 0)
                    def _():
                        o_ref[:, cs] = v.astype(dtype)

                    @pl.when(i != 0)
                    def _():
                        o_ref[:, cs] += v.astype(dtype)

        if nk == 1:
            w = tn // nsub
            for s in range(nsub):
                cs = slice(s * w, (s + 1) * w)
                finish(partials(cs), cs)
        else:
            part = partials(slice(None))
            @pl.when(kk == 0)
            def _():
                for a_ref, p in zip(accs, part):
                    a_ref[...] = p

            @pl.when(kk != 0)
            def _():
                for a_ref, p in zip(accs, part):
                    a_ref[...] += p

            @pl.when(kk == nk - 1)
            def _():
                finish([a_ref[...] for a_ref in accs], slice(None))

    scratch = [pltpu.VMEM((tm, tn), F32) for _ in range(n_acc)] if nk > 1 else []
    grid = (gm, gn, nk) if i_outer else (gn, gm, nk)
    res = pl.pallas_call(
        body, name=name, grid=grid, in_specs=in_specs, out_specs=out_specs, out_shape=out_shape,
        scratch_shapes=scratch, compiler_params=_cparams(vmem),
    )(*args)
    return res


def _term(a, b, mode, acc=0, pa=(), pb=()):
    return (a, b, mode, acc, tuple(pa), tuple(pb))


def _row_tile(lp):
    return _pick(lp, (384, 256, 128))


def _rmsnorm_fwd(h, gain):
    lp = h.shape[0]
    tr = _row_tile(lp)

    def body(h_ref, g_ref, z_ref):
        x = h_ref[...]
        r = lax.rsqrt(jnp.mean(x * x, axis=-1, keepdims=True) + EPS)
        z_ref[...] = (x * r * g_ref[...]).astype(BF16)

    return pl.pallas_call(
        body, name="rmsnorm_fwd", grid=(lp // tr,),
        in_specs=[pl.BlockSpec((tr, D), lambda i: (i, 0)), pl.BlockSpec((1, D), lambda i: (0, 0))],
        out_specs=pl.BlockSpec((tr, D), lambda i: (i, 0)),
        out_shape=jax.ShapeDtypeStruct((lp, D), BF16),
    )(h, gain)


def _loss_head(h, target, gain):
    lp = h.shape[0]
    tr = 128

    def body(h_ref, t_ref, g_ref, loss_ref, dh_ref, dhb_ref, dg_ref):
        i = pl.program_id(0)

        @pl.when(i == 0)
        def _():
            loss_ref[...] = jnp.zeros_like(loss_ref)
            dg_ref[...] = jnp.zeros_like(dg_ref)
            dh_ref[...] = jnp.zeros_like(dh_ref)
            dhb_ref[...] = jnp.zeros_like(dhb_ref)

        @pl.when(i != 0)
        def _():
            x = h_ref[...]
            g = g_ref[...]
            r = lax.rsqrt(jnp.mean(x * x, axis=-1, keepdims=True) + EPS)
            xh = x * r
            y = xh * g
            err = y - t_ref[...]
            loss_ref[...] += 0.5 * jnp.sum(jnp.sum(err * err, axis=-1, keepdims=True), axis=0, keepdims=True) / D
            dy = err * (1.0 / D)
            dg_ref[...] += jnp.sum(dy * xh, axis=0, keepdims=True)
            dxh = dy * g
            dx = r * (dxh - xh * jnp.mean(dxh * xh, axis=-1, keepdims=True))
            dh_ref[...] = dx
            dhb_ref[...] = dx.astype(BF16)

    row = pl.BlockSpec((tr, D), lambda i: (i, 0))
    vec = pl.BlockSpec((1, D), lambda i: (0, 0))
    return pl.pallas_call(
        body, name="loss_head", grid=(lp // tr,),
        in_specs=[row, pl.BlockSpec((tr, D), lambda i: (jnp.maximum(i - 1, 0), 0)), vec],
        out_specs=[pl.BlockSpec((1, 1), lambda i: (0, 0)), row, row, vec],
        out_shape=[jax.ShapeDtypeStruct((1, 1), F32), jax.ShapeDtypeStruct((lp, D), F32),
                   jax.ShapeDtypeStruct((lp, D), BF16), jax.ShapeDtypeStruct((1, D), F32)],
    )(h, target, gain)


def _silu_parts(g):
    s = _sigmoid(g)
    return g * s, s * (1.0 + g * (1.0 - s))


def _residual_norm_epi(scale, with_norm):
    def epi(accs, exs, row0):
        h2 = exs[0] + scale * accs[0]
        if not with_norm:
            return [h2]
        r = lax.rsqrt(jnp.mean(h2 * h2, axis=-1, keepdims=True) + EPS)
        return [h2, h2 * r * exs[1]]
    return epi


def _norm_bwd_epi(accs, exs, row0):
    dz = accs[0]
    x, res, g = exs
    r = lax.rsqrt(jnp.mean(x * x, axis=-1, keepdims=True) + EPS)
    xh = x * r
    dxh = dz * g
    dx = r * (dxh - xh * jnp.mean(dxh * xh, axis=-1, keepdims=True))
    rows = row0 + lax.broadcasted_iota(jnp.int32, (dz.shape[0], 1), 0)
    dh = jnp.where(rows >= NULL, res + dx, 0.0)
    return [dh, dh, jnp.sum(dz * xh, axis=0, keepdims=True)]


def _norm_bwd_outs(lp):
    return [((lp, D), F32, "mn", 0, ()), ((lp, D), BF16, "mn", 0, ()), ((1, D), F32, "nsum", 0, ())]


def _ffn_fwd(h, z, wg_t, wu_t, wd, pre, next_gain):
    lp = h.shape[0]
    tm = _tm(lp)

    def up_epi(accs, exs, row0):
        g, u = accs
        sg, _ = _silu_parts(g)
        return [g, u, sg * u]

    bshape = (lp, DFF)
    g_, u_, act = _mm("ffn_up", lp, DFF, [_term(z, wg_t, "nt", 0, (), pre), _term(z, wu_t, "nt", 1, (), pre)],
                      [(bshape, BF16, "mn", 0, ())] * 3, up_epi, tm=tm, tn=DFF, nsub=DFF // MXU_N, vmem=VMEM_BIG)

    with_norm = next_gain is not None
    res = _mm("ffn_down", lp, D, [_term(act, wd, "nn", 0, (), pre)],
              [((lp, D), F32, "mn", 0, ())] + ([((lp, D), BF16, "mn", 0, ())] if with_norm else []),
              _residual_norm_epi(0.5, with_norm),
              extras=[(h, "mn", 0, ())] + ([(next_gain, "n", 0, ())] if with_norm else []),
              tm=tm, tn=D, i_outer=True, vmem=VMEM_BIG)
    return res[0], (res[1] if with_norm else None), dict(h=h, z=z, g=g_, u=u_, act=act)


def _dw(name, a, b, m, n, scale=1.0):
    lp = a.shape[0]
    tm = _pick(m, (2944, 1408, 1024, 512, 256, 128))
    tn = _pick(n, (1024, 512, 256, 128))
    nk = lp // _pick(lp, (2112, 256, 128) if tm <= 1408 else (1056, 256, 128))

    def epi(accs, exs, row0):
        return [accs[0] * scale]

    (w,) = _mm(name, m, n, [_term(a, b, "tn")], [((m, n), BF16, "mn", 0, ())], epi, tm=tm, tn=tn, nk=nk,
               i_outer=True, vmem=VMEM_BIG)
    return w


def _ffn_bwd(dh, dhb, sv, gain, wg_t, wu_t, wd, pre):
    lp = dh.shape[0]
    tm = _tm(lp)

    def dact_epi(accs, exs, row0):
        g = exs[0].astype(F32)
        u = exs[1].astype(F32)
        da = 0.5 * accs[0]
        sg, dsg = _silu_parts(g)
        return [da * u * dsg, da * sg]

    dg_, du_ = _mm("ffn_dact", lp, DFF, [_term(dhb, wd, "nt", 0, (), pre)],
                   [((lp, DFF), BF16, "mn", 0, ())] * 2, dact_epi,
                   extras=[(sv["g"], "mn", 0, ()), (sv["u"], "mn", 0, ())], tm=tm, tn=DFF, nsub=DFF // MXU_N,
                   vmem=VMEM_BIG)
    d_wd = _dw("dw_down", sv["act"], dhb, DFF, D, 0.5)
    d_wg = _dw("dw_gate", dg_, sv["z"], DFF, D)
    d_wu = _dw("dw_up", du_, sv["z"], DFF, D)

    nk = 1
    dh2, dhb2, dgain = _mm("ffn_dz", lp, D, [_term(dg_, wg_t, "nn", 0, (), pre), _term(du_, wu_t, "nn", 0, (), pre)],
                           _norm_bwd_outs(lp), _norm_bwd_epi,
                           extras=[(sv["h"], "mn", 0, ()), (dh, "mn", 0, ()), (gain, "n", 0, ())],
                           tm=tm, tn=D, nk=nk, vmem=VMEM_BIG)
    return dh2, dhb2, dgain, d_wg, d_wu, d_wd


def _gla_gates(hin, w2p, b2p):
    lp = hin.shape[0]
    tr = _row_tile(lp)
    bf, bb = P_OFF["lrf"] // HP, P_OFF["lrb"] // HP

    def body(lf_ref, lb_ref, w_ref, b_ref, o_ref, c_ref):
        i = pl.program_id(0)
        rows = i * tr + lax.broadcasted_iota(jnp.int32, (tr, 1), 0)
        r = lax.broadcasted_iota(jnp.int32, (tr, tr), 0)
        c = lax.broadcasted_iota(jnp.int32, (tr, tr), 1)
        same = (r // CHUNK) == (c // CHUNK)
        for d, l_ref in enumerate((lf_ref, lb_ref)):
            logit = _dot3(l_ref[...], w_ref[d]) + b_ref[d]
            g = jnp.where(rows >= NULL, jax.nn.log_sigmoid(logit) * (1.0 / GLA_TAU), 0.0)
            o_ref[d] = g
            tmat = jnp.where(same & ((r >= c) if d == 0 else (r <= c)), 1.0, 0.0).astype(BF16)
            c_ref[d] = _dot_sel(tmat, g)

    spec = pl.BlockSpec((2, tr, 512), lambda i: (0, i, 0))
    return pl.pallas_call(
        body, name="gla_gates", grid=(lp // tr,),
        in_specs=[pl.BlockSpec((tr, HP), lambda i: (i, bf)), pl.BlockSpec((tr, HP), lambda i: (i, bb)),
                  pl.BlockSpec((2, HP, 512), lambda i: (0, 0, 0)), pl.BlockSpec((2, 1, 512), lambda i: (0, 0, 0))],
        out_specs=[spec, spec],
        out_shape=[jax.ShapeDtypeStruct((2, lp, 512), F32)] * 2,
    )(hin, hin, w2p, b2p)


def _gla_rows(lp):
    return _pick(lp, (384, 256, 128))


def _tri(d):
    r = lax.broadcasted_iota(jnp.int32, (CHUNK, CHUNK), 0)
    c = lax.broadcasted_iota(jnp.int32, (CHUNK, CHUNK), 1)
    return (r >= c) if d == 0 else (r <= c)


def _gla_fwd(hin, gates):
    lp = hin.shape[0]
    rb = _gla_rows(lp)
    nb = lp // rb
    cpb = rb // CHUNK
    nchunk = lp // CHUNK
    qo, ko, vo = P_OFF["qa"] // 512, P_OFF["ka"] // 512, P_OFF["va"] // 512
    scale = GLA_DK ** -0.5

    def body(qf, kf, vf, gf, qb, kb, vb_, gb, of, ob, sf, sb, st):
        @pl.when(pl.program_id(0) == 0)
        def _():
            st[...] = jnp.zeros_like(st)

        ins = ((qf, kf, vf, gf, of, sf), (qb, kb, vb_, gb, ob, sb))
        for ci in range(cpb):
            for d in range(2):
                q_ref, k_ref, v_ref, g_ref, o_ref, s_ref = ins[d]
                tri = _tri(d)
                c = ci if d == 0 else cpb - 1 - ci
                rows = slice(c * CHUNK, (c + 1) * CHUNK)
                for h in range(GLA_H):
                    sl = slice(h * HP, (h + 1) * HP)
                    q = q_ref[rows, sl] * scale
                    k = k_ref[rows, sl]
                    v = v_ref[rows, sl]
                    b = g_ref[rows, sl]
                    btot = b[CHUNK - 1:CHUNK] if d == 0 else b[0:1]
                    qd = (q * jnp.exp(b)).astype(BF16)
                    ki = (k * jnp.exp(-b)).astype(BF16)
                    ke = (k * jnp.exp(btot - b)).astype(BF16)
                    vb = v.astype(BF16)
                    att = jnp.where(tri, _dot(qd, ki, "nt"), 0.0)
                    s_prev = st[d, h]
                    o_ref[rows, sl] = _dot(att.astype(BF16), vb) + _dot(qd, s_prev.astype(BF16), "nt")
                    s_ref[h, c] = s_prev
                    st[d, h] = s_prev * jnp.exp(btot) + _dot(vb, ke, "tn")

    def specs(off):
        return (pl.BlockSpec((rb, 512), lambda b: (b, off)), pl.BlockSpec((rb, 512), lambda b: (nb - 1 - b, off)))

    (qf, qb), (kf, kb), (vf, vb2) = specs(qo), specs(ko), specs(vo)
    gf = pl.BlockSpec((None, rb, 512), lambda b: (0, b, 0))
    gb = pl.BlockSpec((None, rb, 512), lambda b: (1, nb - 1 - b, 0))
    of, ob = specs(0)
    sf = pl.BlockSpec((GLA_H, cpb, HP, HP), lambda b: (0, b, 0, 0))
    sb = pl.BlockSpec((GLA_H, cpb, HP, HP), lambda b: (0, nb - 1 - b, 0, 0))
    osh = jax.ShapeDtypeStruct((lp, GLA_H * HP), F32)
    ssh = jax.ShapeDtypeStruct((GLA_H, nchunk, HP, HP), F32)
    return pl.pallas_call(
        body, name="gla_fwd", grid=(nb,),
        in_specs=[qf, kf, vf, gf, qb, kb, vb2, gb], out_specs=[of, ob, sf, sb], out_shape=[osh, osh, ssh, ssh],
        scratch_shapes=[pltpu.VMEM((2, GLA_H, HP, HP), F32)], compiler_params=_cparams(VMEM_BIG),
    )(hin, hin, hin, gates, hin, hin, hin, gates)


def _gla_bwd(hin, gates, states, do):
    lp = hin.shape[0]
    rb = _gla_rows(lp)
    nb = lp // rb
    cpb = rb // CHUNK
    qo, ko, vo = P_OFF["qa"] // 512, P_OFF["ka"] // 512, P_OFF["va"] // 512
    scale = GLA_DK ** -0.5

    def body(qf, kf, vf, gf, sf, dof, qb, kb, vb_, gb, sb, dob,
             dqf, dkf, dvf, dgf, dqb, dkb, dvb, dgb, dst):
        @pl.when(pl.program_id(0) == 0)
        def _():
            dst[...] = jnp.zeros_like(dst)

        ins = ((qf, kf, vf, gf, sf, dof, dqf, dkf, dvf, dgf), (qb, kb, vb_, gb, sb, dob, dqb, dkb, dvb, dgb))
        for ci in range(cpb):
            for d in range(2):
                q_ref, k_ref, v_ref, g_ref, s_ref, do_ref, dq_ref, dk_ref, dv_ref, dg_ref = ins[d]
                tri, tri_t = _tri(d), _tri(1 - d)
                edge = lax.broadcasted_iota(jnp.int32, (CHUNK, 1), 0) == (CHUNK - 1 if d == 0 else 0)
                c = cpb - 1 - ci if d == 0 else ci
                rows = slice(c * CHUNK, (c + 1) * CHUNK)
                for h in range(GLA_H):
                    sl = slice(h * HP, (h + 1) * HP)
                    q = q_ref[rows, sl] * scale
                    k = k_ref[rows, sl]
                    v = v_ref[rows, sl]
                    dout = do_ref[rows, sl].astype(BF16)
                    b = g_ref[rows, sl]
                    btot = b[CHUNK - 1:CHUNK] if d == 0 else b[0:1]
                    e = jnp.exp(b)
                    ei = jnp.exp(-b)
                    et = jnp.exp(btot - b)
                    etot = jnp.exp(btot)
                    qd = q * e
                    ki = k * ei
                    ke = k * et
                    qdb, kib, keb, vb = qd.astype(BF16), ki.astype(BF16), ke.astype(BF16), v.astype(BF16)
                    att_t = jnp.where(tri_t, _dot(kib, qdb, "nt"), 0.0).astype(BF16)
                    d_att = jnp.where(tri, _dot(dout, vb, "nt"), 0.0).astype(BF16)
                    d_att_t = jnp.where(tri_t, _dot(vb, dout, "nt"), 0.0).astype(BF16)
                    s_prev = s_ref[h, c]
                    ds_t = dst[d, h]
                    ds_b = ds_t.astype(BF16)
                    dv = _dot(att_t, dout) + _dot(keb, ds_b, "nt")
                    d_qd = _dot(d_att, kib) + _dot(dout, s_prev.astype(BF16))
                    d_ki = _dot(d_att_t, qdb)
                    d_ke = _dot(vb, ds_b)
                    d_e = jnp.sum(s_prev * ds_t, axis=0, keepdims=True)
                    dst[d, h] = _dot(dout, qdb, "tn") + ds_t * etot
                    db = d_qd * qd - d_ki * ki - d_ke * ke
                    dbtot = jnp.sum(d_ke * ke, axis=0, keepdims=True) + d_e * etot
                    dq_ref[rows, sl] = d_qd * e * scale
                    dk_ref[rows, sl] = d_ki * ei + d_ke * et
                    dv_ref[rows, sl] = dv
                    dg_ref[rows, sl] = db + jnp.where(edge, dbtot, 0.0)

    def fw(off):
        return pl.BlockSpec((rb, 512), lambda b: (nb - 1 - b, off))

    def bw(off):
        return pl.BlockSpec((rb, 512), lambda b: (b, off))

    gf = pl.BlockSpec((None, rb, 512), lambda b: (0, nb - 1 - b, 0))
    gb = pl.BlockSpec((None, rb, 512), lambda b: (1, b, 0))
    sf = pl.BlockSpec((GLA_H, cpb, HP, HP), lambda b: (0, nb - 1 - b, 0, 0))
    sb = pl.BlockSpec((GLA_H, cpb, HP, HP), lambda b: (0, b, 0, 0))
    osh = jax.ShapeDtypeStruct((lp, GLA_H * HP), F32)
    res = pl.pallas_call(
        body, name="gla_bwd", grid=(nb,),
        in_specs=[fw(qo), fw(ko), fw(vo), gf, sf, fw(0), bw(qo), bw(ko), bw(vo), gb, sb, bw(0)],
        out_specs=[fw(0)] * 4 + [bw(0)] * 4, out_shape=[osh] * 8,
        scratch_shapes=[pltpu.VMEM((2, GLA_H, HP, HP), F32)], compiler_params=_cparams(VMEM_BIG),
    )(hin, hin, hin, gates, states[0], do, hin, hin, hin, gates, states[1], do)
    return res[:4], res[4:]


def _gla_out_fwd(o2, hin, gn):
    lp = hin.shape[0]
    tr = _row_tile(lp)
    ro = P_OFF["ra"] // 512

    def body(of_ref, ob_ref, r_ref, gn_ref, a_ref):
        r = r_ref[...]
        sr, _ = _silu_parts(r)
        for h in range(GLA_H):
            sl = slice(h * HP, (h + 1) * HP)
            o = of_ref[:, sl] + ob_ref[:, sl]
            rs = lax.rsqrt(jnp.mean(o * o, axis=-1, keepdims=True) + EPS)
            a_ref[:, sl] = (o * rs * gn_ref[:, sl] * sr[:, sl]).astype(BF16)

    row = pl.BlockSpec((tr, 512), lambda i: (i, 0))
    return pl.pallas_call(
        body, name="gla_out_fwd", grid=(lp // tr,),
        in_specs=[row, row, pl.BlockSpec((tr, 512), lambda i: (i, ro)), pl.BlockSpec((1, 512), lambda i: (0, 0))],
        out_specs=row,
        out_shape=jax.ShapeDtypeStruct((lp, 512), BF16),
    )(o2[0], o2[1], hin, gn)


def _gla_out_bwd(da, o2, hin, gn):
    lp = hin.shape[0]
    tr = _row_tile(lp)
    ro = P_OFF["ra"] // 512

    def body(da_ref, of_ref, ob_ref, r_ref, gn_ref, do_ref, dr_ref, dgn_ref):
        i = pl.program_id(0)
        r = r_ref[...]
        sr, dsr = _silu_parts(r)
        da_v = da_ref[...]
        parts = []
        for h in range(GLA_H):
            sl = slice(h * HP, (h + 1) * HP)
            o = of_ref[:, sl] + ob_ref[:, sl]
            rs = lax.rsqrt(jnp.mean(o * o, axis=-1, keepdims=True) + EPS)
            oh = o * rs
            gn_h = gn_ref[:, sl]
            dah = da_v[:, sl]
            dr_ref[:, sl] = (dah * oh * gn_h * dsr[:, sl]).astype(BF16)
            t = dah * sr[:, sl]
            parts.append(jnp.sum(t * oh, axis=0, keepdims=True))
            doh = t * gn_h
            do_ref[:, sl] = rs * (doh - oh * jnp.mean(doh * oh, axis=-1, keepdims=True))
        part = jnp.concatenate(parts, axis=1)

        @pl.when(i == 0)
        def _():
            dgn_ref[...] = part

        @pl.when(i != 0)
        def _():
            dgn_ref[...] += part

    row = pl.BlockSpec((tr, 512), lambda i: (i, 0))
    return pl.pallas_call(
        body, name="gla_out_bwd", grid=(lp // tr,),
        in_specs=[row, row, row, pl.BlockSpec((tr, 512), lambda i: (i, ro)), pl.BlockSpec((1, 512), lambda i: (0, 0))],
        out_specs=[row, row, pl.BlockSpec((1, 512), lambda i: (0, 0))],
        out_shape=[jax.ShapeDtypeStruct((lp, 512), F32), jax.ShapeDtypeStruct((lp, 512), BF16),
                   jax.ShapeDtypeStruct((1, 512), F32)],
    )(da, o2[0], o2[1], hin, gn)


def _gla_in_bwd(gf, gb, gates, hin, w2p):
    lp = hin.shape[0]
    tr = _row_tile(lp)
    bf, bb = P_OFF["lrf"] // HP, P_OFF["lrb"] // HP

    def body(dqf_ref, dkf_ref, dvf_ref, dgf_ref, dqb_ref, dkb_ref, dvb_ref, dgb_ref, g_ref, lf_ref, lb_ref, w_ref,
             oq_ref, ok_ref, ov_ref, olr_ref, dw_ref, db_ref):
        i = pl.program_id(0)
        oq_ref[...] = (dqf_ref[...] + dqb_ref[...]).astype(BF16)
        ok_ref[...] = (dkf_ref[...] + dkb_ref[...]).astype(BF16)
        ov_ref[...] = (dvf_ref[...] + dvb_ref[...]).astype(BF16)
        rows = i * tr + lax.broadcasted_iota(jnp.int32, (tr, 1), 0)
        r = lax.broadcasted_iota(jnp.int32, (tr, tr), 0)
        c = lax.broadcasted_iota(jnp.int32, (tr, tr), 1)
        same = (r // CHUNK) == (c // CHUNK)
        for d, (l_ref, dg_ref) in enumerate(((lf_ref, dgf_ref), (lb_ref, dgb_ref))):
            tmat = jnp.where(same & ((r <= c) if d == 0 else (r >= c)), 1.0, 0.0).astype(BF16)
            dg = _dot_sel(tmat, dg_ref[...])
            sig_neg = 1.0 - jnp.exp(GLA_TAU * g_ref[d])
            dlogit = jnp.where(rows >= NULL, dg * (1.0 / GLA_TAU) * sig_neg, 0.0)
            olr_ref[:, d * HP:(d + 1) * HP] = _dot3(dlogit, w_ref[d], "nt").astype(BF16)
            dw = _dot3(l_ref[...], dlogit, "tn")
            dbias = jnp.sum(dlogit, axis=0, keepdims=True)

            @pl.when(i == 0)
            def _():
                dw_ref[d] = dw
                db_ref[d] = dbias

            @pl.when(i != 0)
            def _():
                dw_ref[d] += dw
                db_ref[d] += dbias

    two = pl.BlockSpec((2, tr, 512), lambda i: (0, i, 0))
    row = pl.BlockSpec((tr, 512), lambda i: (i, 0))
    return pl.pallas_call(
        body, name="gla_in_bwd", grid=(lp // tr,),
        in_specs=[row] * 8 + [two, pl.BlockSpec((tr, HP), lambda i: (i, bf)),
                  pl.BlockSpec((tr, HP), lambda i: (i, bb)), pl.BlockSpec((2, HP, 512), lambda i: (0, 0, 0))],
        out_specs=[row, row, row, pl.BlockSpec((tr, 2 * HP), lambda i: (i, 0)),
                   pl.BlockSpec((2, HP, 512), lambda i: (0, 0, 0)), pl.BlockSpec((2, 1, 512), lambda i: (0, 0, 0))],
        out_shape=[jax.ShapeDtypeStruct((lp, 512), BF16)] * 3 + [
            jax.ShapeDtypeStruct((lp, 2 * HP), BF16), jax.ShapeDtypeStruct((2, HP, 512), F32),
            jax.ShapeDtypeStruct((2, 1, 512), F32)],
    )(*gf, *gb, gates, hin, hin, w2p)


def _rope_tables(lp):
    n_tok = lp - NULL - NMETA
    rows = n_tok // GRID_W
    row = jnp.repeat(jnp.arange(rows), GRID_W).astype(F32)
    col = jnp.tile(jnp.arange(GRID_W), rows).astype(F32)
    inv = ROPE_THETA ** (-jnp.arange(0, 32, 2, dtype=F32) / 32)
    ang = jnp.concatenate([row[:, None] * inv, col[:, None] * inv], axis=-1)
    ang = jnp.concatenate([jnp.zeros((NULL + NMETA, 32), F32), ang], axis=0)
    cos, sin = jnp.cos(ang), jnp.sin(ang)
    z16 = jnp.zeros((lp, 16), F32)
    z64 = jnp.zeros((lp, 64), F32)
    c = jnp.concatenate([cos[:, :16], cos[:, :16], cos[:, 16:], cos[:, 16:], z64], axis=1)
    a = jnp.concatenate([-sin[:, :16], z16, -sin[:, 16:], z16, z64], axis=1)
    b = jnp.concatenate([z16, sin[:, :16], z16, sin[:, 16:], z64], axis=1)
    return c, a, b


def _rope(x, c, a, b):
    return x * c + pltpu.roll(x, HP - 16, 1) * a + pltpu.roll(x, 16, 1) * b


def _rope_t(dx, c, a, b):
    return dx * c + pltpu.roll(dx * a, 16, 1) + pltpu.roll(dx * b, HP - 16, 1)


def _attn_prep(hin, gq, gk, tabs):
    lp = hin.shape[0]
    tr = _row_tile(lp)
    qo, ko, vo = P_OFF["qb"] // 1024, P_OFF["kb"] // 256, P_OFF["vb"] // 256

    def body(q_ref, k_ref, v_ref, gq_ref, gk_ref, c_ref, a_ref, b_ref, oq_ref, ok_ref, ov_ref):
        c, a, b = c_ref[...], a_ref[...], b_ref[...]
        for src, g_ref, dst, nh, sc in ((q_ref, gq_ref, oq_ref, ATT_H, Q_SCALE), (k_ref, gk_ref, ok_ref, ATT_KV, 1.0)):
            for h in range(nh):
                sl = slice(h * HP, (h + 1) * HP)
                x = src[:, sl]
                r = lax.rsqrt(jnp.sum(x * x, axis=-1, keepdims=True) * (1.0 / HEAD_DIM) + EPS)
                dst[:, sl] = (_rope(x * r * g_ref[...], c, a, b) * sc).astype(BF16)
        lane = lax.broadcasted_iota(jnp.int32, (1, ATT_KV * HP), 1)
        ov_ref[...] = jnp.where(lane % HP == HEAD_DIM, 1.0, v_ref[...]).astype(BF16)

    tab = pl.BlockSpec((tr, HP), lambda i: (i, 0))
    vec = pl.BlockSpec((1, HP), lambda i: (0, 0))
    return pl.pallas_call(
        body, name="attn_prep", grid=(lp // tr,),
        in_specs=[pl.BlockSpec((tr, 1024), lambda i: (i, qo)), pl.BlockSpec((tr, 256), lambda i: (i, ko)),
                  pl.BlockSpec((tr, 256), lambda i: (i, vo)), vec, vec, tab, tab, tab],
        out_specs=[pl.BlockSpec((tr, 1024), lambda i: (i, 0)), pl.BlockSpec((tr, 256), lambda i: (i, 0)),
                   pl.BlockSpec((tr, 256), lambda i: (i, 0))],
        out_shape=[jax.ShapeDtypeStruct((lp, 1024), BF16), jax.ShapeDtypeStruct((lp, 256), BF16),
                   jax.ShapeDtypeStruct((lp, 256), BF16)],
    )(hin, hin, hin, gq, gk, *tabs)


def _attn_prep_bwd(dqr, dkr, hin, gq, gk, tabs):
    lp = hin.shape[0]
    tr = _row_tile(lp)
    qo, ko = P_OFF["qb"] // 1024, P_OFF["kb"] // 256

    def body(dq_ref, dk_ref, q_ref, k_ref, gq_ref, gk_ref, c_ref, a_ref, b_ref, oq_ref, ok_ref, dgq_ref, dgk_ref):
        i = pl.program_id(0)
        c, a, b = c_ref[...], a_ref[...], b_ref[...]
        for src, dsrc, g_ref, dst, dg_ref, nh, sc in (
                (q_ref, dq_ref, gq_ref, oq_ref, dgq_ref, ATT_H, Q_SCALE),
                (k_ref, dk_ref, gk_ref, ok_ref, dgk_ref, ATT_KV, 1.0)):
            acc = jnp.zeros((1, HP), F32)
            for h in range(nh):
                sl = slice(h * HP, (h + 1) * HP)
                x = src[:, sl]
                r = lax.rsqrt(jnp.sum(x * x, axis=-1, keepdims=True) * (1.0 / HEAD_DIM) + EPS)
                xh = x * r
                dxn = _rope_t(dsrc[:, sl] * sc, c, a, b)
                acc = acc + jnp.sum(dxn * xh, axis=0, keepdims=True)
                dxh = dxn * g_ref[...]
                dx = r * (dxh - xh * (jnp.sum(dxh * xh, axis=-1, keepdims=True) * (1.0 / HEAD_DIM)))
                dst[:, sl] = dx.astype(BF16)

            @pl.when(i == 0)
            def _():
                dg_ref[...] = acc

            @pl.when(i != 0)
            def _():
                dg_ref[...] += acc

    tab = pl.BlockSpec((tr, HP), lambda i: (i, 0))
    vec = pl.BlockSpec((1, HP), lambda i: (0, 0))
    return pl.pallas_call(
        body, name="attn_prep_bwd", grid=(lp // tr,),
        in_specs=[pl.BlockSpec((tr, 1024), lambda i: (i, 0)), pl.BlockSpec((tr, 256), lambda i: (i, 0)),
                  pl.BlockSpec((tr, 1024), lambda i: (i, qo)), pl.BlockSpec((tr, 256), lambda i: (i, ko)),
                  vec, vec, tab, tab, tab],
        out_specs=[pl.BlockSpec((tr, 1024), lambda i: (i, 0)), pl.BlockSpec((tr, 256), lambda i: (i, 0)), vec, vec],
        out_shape=[jax.ShapeDtypeStruct((lp, 1024), BF16), jax.ShapeDtypeStruct((lp, 256), BF16),
                   jax.ShapeDtypeStruct((1, HP), F32), jax.ShapeDtypeStruct((1, HP), F32)],
    )(dqr, dkr, hin, hin, gq, gk, *tabs)


QB = 128
GH = 2
Q_SCALE = HEAD_DIM ** -0.5 * math.log2(math.e)
LN2 = math.log(2.0)


def _stack(ref, g0, n):
    return jnp.concatenate([ref[:, (g0 + g) * HP:(g0 + g + 1) * HP] for g in range(n)], axis=0)


def _attn_fwd(qr, kr, vb):
    lp = qr.shape[0]
    nq = lp // QB

    def body(q_ref, k_ref, v_ref, o_ref, lse_ref):
        qb = pl.program_id(1)
        keys = lax.broadcasted_iota(jnp.int32, (1, lp), 1)
        lane = lax.broadcasted_iota(jnp.int32, (1, HP), 1)
        rows = qb * QB + lax.broadcasted_iota(jnp.int32, (QB, 1), 0)
        for ch in range(ATT_G // GH):
            qs = _stack(q_ref, ch * GH, GH)
            s = _dot(qs, k_ref[...], "nt")
            s = jnp.where(keys >= NULL, s, -1e30)
            m = jnp.max(s, axis=-1, keepdims=True)
            p = jnp.exp2(s - m).astype(BF16)
            o_raw = _dot(p, v_ref[...])
            l = jnp.sum(jnp.where(lane == HEAD_DIM, o_raw, 0.0), axis=-1, keepdims=True)
            o = jnp.where(lane < HEAD_DIM, o_raw / l, 0.0)
            lse = m + jnp.log2(l)
            for g in range(GH):
                sl = slice((ch * GH + g) * HP, (ch * GH + g + 1) * HP)
                o_ref[:, sl] = jnp.where(rows >= NULL, o[g * QB:(g + 1) * QB], 0.0).astype(BF16)
                lse_ref[:, sl] = jnp.broadcast_to(lse[g * QB:(g + 1) * QB], (QB, HP))

    qspec = pl.BlockSpec((QB, ATT_G * HP), lambda kv, qb: (qb, kv))
    kspec = pl.BlockSpec((lp, HP), lambda kv, qb: (0, kv))
    return pl.pallas_call(
        body, name="attn_fwd", grid=(ATT_KV, nq),
        in_specs=[qspec, kspec, kspec], out_specs=[qspec, qspec],
        out_shape=[jax.ShapeDtypeStruct((lp, ATT_H * HP), BF16), jax.ShapeDtypeStruct((lp, ATT_H * HP), F32)],
        compiler_params=_cparams(VMEM_BIG),
    )(qr, kr, vb)


def _attn_bwd(qr, kr, vb, o, lse, do):
    lp = qr.shape[0]
    nq = lp // QB

    def body(q_ref, k_ref, v_ref, o_ref, lse_ref, do_ref, dq_ref, dk_ref, dv_ref):
        qb = pl.program_id(1)

        @pl.when(qb == 0)
        def _():
            dk_ref[...] = jnp.zeros_like(dk_ref)
            dv_ref[...] = jnp.zeros_like(dv_ref)

        keys = lax.broadcasted_iota(jnp.int32, (1, lp), 1)
        k = k_ref[...]
        dk_acc, dv_acc = None, None
        for ch in range(ATT_G // GH):
            g0 = ch * GH
            qs = _stack(q_ref, g0, GH)
            dos = _stack(do_ref, g0, GH)
            os_ = _stack(o_ref, g0, GH).astype(F32)
            lse_s = jnp.concatenate([lse_ref[:, (g0 + g) * HP:(g0 + g) * HP + 1] for g in range(GH)], axis=0)
            delta = jnp.sum(dos * os_, axis=-1, keepdims=True) * LN2
            s = _dot(qs, k, "nt")
            p = jnp.where(keys >= NULL, jnp.exp2(s - lse_s), 0.0)
            dob = dos.astype(BF16)
            dp = _dot((dos * LN2).astype(BF16), v_ref[...], "nt")
            ds = (p * (dp - delta)).astype(BF16)
            dq = _dot(ds, k)
            for g in range(GH):
                dq_ref[:, (g0 + g) * HP:(g0 + g + 1) * HP] = dq[g * QB:(g + 1) * QB]
            dv_c = _dot(p.astype(BF16), dob, "tn")
            dk_c = _dot(ds, qs, "tn")
            dv_acc = dv_c if dv_acc is None else dv_acc + dv_c
            dk_acc = dk_c if dk_acc is None else dk_acc + dk_c
        dv_ref[...] += dv_acc
        dk_ref[...] += dk_acc

    qspec = pl.BlockSpec((QB, ATT_G * HP), lambda kv, qb: (qb, kv))
    kspec = pl.BlockSpec((lp, HP), lambda kv, qb: (0, kv))
    return pl.pallas_call(
        body, name="attn_bwd", grid=(ATT_KV, nq),
        in_specs=[qspec, kspec, kspec, qspec, qspec, qspec], out_specs=[qspec, kspec, kspec],
        out_shape=[jax.ShapeDtypeStruct((lp, ATT_H * HP), F32), jax.ShapeDtypeStruct((lp, ATT_KV * HP), F32),
                   jax.ShapeDtypeStruct((lp, ATT_KV * HP), F32)],
        compiler_params=_cparams(VMEM_BIG),
    )(qr, kr, vb, o, lse, do)


def _mixer_fwd(h, z, wl, l, tabs, next_gain):
    lp = h.shape[0]
    tm = _tm(lp)

    def id_epi(accs, exs, row0):
        return [accs[0]]

    (hin,) = _mm("in_proj", lp, D_INP, [_term(z, wl["win_t"], "nt", 0, (), (l,))], [((lp, D_INP), F32, "mn", 0, ())],
                 id_epi, tm=tm, tn=D_INP // 2, vmem=VMEM_BIG)
    gates, cum = _gla_gates(hin, wl["w2p"][l], wl["b2p"][l])
    o_f, o_b, s_f, s_b = _gla_fwd(hin, cum)
    o2, states = (o_f, o_b), (s_f, s_b)
    a = _gla_out_fwd(o2, hin, wl["gn"][l])
    qr, kr, vb = _attn_prep(hin, wl["gq"][l], wl["gk"][l], tabs)
    b, lse = _attn_fwd(qr, kr, vb)

    def merge_epi(accs, exs, row0):
        pa, pb = accs
        ga, gb, bma, bmb = exs
        y = _sigmoid(ga + bma) * pa + _sigmoid(gb + bmb) * pb
        return [y, pa, pb]

    y, pa, pb = _mm("merge", lp, D, [_term(a, wl["wpa_t"], "nt", 0, (), (l,)), _term(b, wl["wpb_t"], "nt", 1, (), (l,))],
                    [((lp, D), BF16, "mn", 0, ())] * 3, merge_epi,
                    extras=[(hin, "mn", P_OFF["ga"] // D, ()), (hin, "mn", P_OFF["gb"] // D, ()),
                            (wl["bm"], "n", 0, (l, 0)), (wl["bm"], "n", 0, (l, 1))],
                    tm=tm, tn=D, nsub=D // MXU_N, i_outer=True, vmem=VMEM_BIG)

    h2, z2 = _mm("out_proj", lp, D, [_term(y, wl["wout"], "nn", 0, (), (l,))],
                 [((lp, D), F32, "mn", 0, ()), ((lp, D), BF16, "mn", 0, ())], _residual_norm_epi(1.0, True),
                 extras=[(h, "mn", 0, ()), (next_gain, "n", 0, ())], tm=tm, tn=D, i_outer=True, vmem=VMEM_BIG)
    sv = dict(h=h, z=z, hin=hin, gates=gates, cum=cum, o2=o2, states=states, a=a, qr=qr, kr=kr, vb=vb, b=b, lse=lse,
              y=y, pa=pa, pb=pb)
    return h2, z2, sv


def _mixer_bwd(dh, dhb, sv, gain, wl, l, tabs):
    lp = dh.shape[0]
    tm = _tm(lp)
    hin = sv["hin"]

    def merge_bwd_epi(accs, exs, row0):
        dy = accs[0]
        ga, gb, pa, pb, bma, bmb = exs
        sa = _sigmoid(ga + bma)
        sb = _sigmoid(gb + bmb)
        dga = dy * pa.astype(F32) * sa * (1.0 - sa)
        dgb = dy * pb.astype(F32) * sb * (1.0 - sb)
        return [dy * sa, dy * sb, dga, dgb, jnp.sum(dga, axis=0, keepdims=True), jnp.sum(dgb, axis=0, keepdims=True)]

    big = ((lp, D), BF16, "mn", 0, ())
    vec = ((1, D), F32, "nsum", 0, ())
    dpa, dpb, dga, dgb, dbma, dbmb = _mm(
        "merge_bwd", lp, D, [_term(dhb, wl["wout"], "nt", 0, (), (l,))], [big, big, big, big, vec, vec], merge_bwd_epi,
        extras=[(hin, "mn", P_OFF["ga"] // D, ()), (hin, "mn", P_OFF["gb"] // D, ()), (sv["pa"], "mn", 0, ()),
                (sv["pb"], "mn", 0, ()), (wl["bm"], "n", 0, (l, 0)), (wl["bm"], "n", 0, (l, 1))],
        tm=tm, tn=D, nsub=D // MXU_N, vmem=VMEM_BIG)
    d_wout = _dw("dw_out", sv["y"], dhb, D, D)
    d_wpa_t = _dw("dw_pa", dpa, sv["a"], D, 512)
    d_wpb_t = _dw("dw_pb", dpb, sv["b"], D, ATT_H * HP)

    def id_epi(accs, exs, row0):
        return [accs[0]]

    (da,) = _mm("d_a", lp, 512, [_term(dpa, wl["wpa_t"], "nn", 0, (), (l,))], [((lp, 512), F32, "mn", 0, ())], id_epi,
                tm=tm, tn=512, i_outer=True, vmem=VMEM_BIG)
    (db,) = _mm("d_b", lp, ATT_H * HP, [_term(dpb, wl["wpb_t"], "nn", 0, (), (l,))],
                [((lp, ATT_H * HP), F32, "mn", 0, ())], id_epi, tm=tm, tn=512, i_outer=True, vmem=VMEM_BIG)
    d_o, d_ra, d_gn = _gla_out_bwd(da, sv["o2"], hin, wl["gn"][l])
    g_fw, g_bw = _gla_bwd(hin, sv["cum"], sv["states"], d_o)
    d_qa, d_ka, d_va, d_lr, d_w2p, d_b2p = _gla_in_bwd(g_fw, g_bw, sv["gates"], hin, wl["w2p"][l])
    dqr, dkr, dvb = _attn_bwd(sv["qr"], sv["kr"], sv["vb"], sv["b"], sv["lse"], db)
    d_qb, d_kb, d_gq, d_gk = _attn_prep_bwd(dqr, dkr, hin, wl["gq"][l], wl["gk"][l], tabs)
    pieces = dict(qb=d_qb, ga=dga, gb=dgb, qa=d_qa, ka=d_ka, va=d_va, ra=d_ra, kb=d_kb, vb=dvb.astype(BF16), lr=d_lr)
    dhin = jnp.concatenate([pieces[n] for n in ("qb", "ga", "gb", "qa", "ka", "va", "ra", "kb", "vb", "lr")], axis=1)
    d_win_t = _dw("dw_in", dhin, sv["z"], D_INP, D)
    dh2, dhb2, dgain = _mm("in_proj_dz", lp, D, [_term(dhin, wl["win_t"], "nn", 0, (), (l,))], _norm_bwd_outs(lp),
                           _norm_bwd_epi, extras=[(sv["h"], "mn", 0, ()), (dh, "mn", 0, ()), (gain, "n", 0, ())],
                           tm=tm, tn=D, nk=2, vmem=VMEM_BIG)
    grads = dict(gain=dgain, wout=d_wout, wpa_t=d_wpa_t, wpb_t=d_wpb_t, win_t=d_win_t, gn=d_gn, w2p=d_w2p, b2p=d_b2p,
                 gq=d_gq, gk=d_gk, bma=dbma, bmb=dbmb)
    return dh2, dhb2, grads


def _mesh_pos():
    x, y, c = lax.axis_index("x"), lax.axis_index("y"), lax.axis_index("c")
    chips = [(1 - x, y), (x, 1 - y), (1 - x, 1 - y)]
    return x, y, c, chips


def _dev_index(x, y, c):
    return 4 * x + 2 * y + c


def _all_gather(name, shards, leads):
    nt = len(shards)

    def blk(ref, lead, idx):
        return ref.at[(slice(None),) * lead + (idx,)]

    def body(*refs):
        xs, outs = refs[:nt], refs[nt:2 * nt]
        send_sems, recv_sems, local_sems = refs[2 * nt:]
        x, y, c, chips = _mesh_pos()
        me, sibling = (x, y, c), (x, y, 1 - c)

        def copy(t, k, block, to, own=False):
            dst = blk(outs[t], leads[t], _dev_index(*block))
            return pltpu.make_async_remote_copy(
                src_ref=xs[t] if own else dst, dst_ref=dst, send_sem=send_sems.at[t, k], recv_sem=recv_sems.at[t, k],
                device_id=to, device_id_type=MESH)

        locals_ = [pltpu.make_async_copy(xs[t], blk(outs[t], leads[t], _dev_index(*me)), local_sems.at[t])
                   for t in range(nt)]
        for cp in locals_:
            cp.start()
        first = []
        for t in range(nt):
            first.append(copy(t, 0, me, sibling, own=True))
            first += [copy(t, 1 + j, me, (*chip, c), own=True) for j, chip in enumerate(chips)]
        for cp in first:
            cp.start()
        passed = []
        for j, chip in enumerate(chips):
            for t in range(nt):
                copy(t, 1 + j, (*chip, c), me).wait_recv()
                fw = copy(t, 4 + j, (*chip, c), sibling)
                fw.start()
                passed.append(fw)
        for t in range(nt):
            copy(t, 0, sibling, me).wait_recv()
        for j, chip in enumerate(chips):
            for t in range(nt):
                copy(t, 4 + j, (*chip, 1 - c), me).wait_recv()
        for cp in first + passed:
            cp.wait_send()
        for cp in locals_:
            cp.wait()

    out_shape = [jax.ShapeDtypeStruct(s.shape[:ld] + (NDEV,) + s.shape[ld:], s.dtype) for s, ld in zip(shards, leads)]
    return pl.pallas_call(
        body, name=name, in_specs=[ANY] * nt, out_specs=[ANY] * nt, out_shape=out_shape,
        scratch_shapes=[pltpu.SemaphoreType.DMA((nt, 7)), pltpu.SemaphoreType.DMA((nt, 7)),
                        pltpu.SemaphoreType.DMA((nt,))],
    )(*shards)


def _exchange_sibling(name, gs):
    nt = len(gs)

    def body(*refs):
        xs, outs = refs[:nt], refs[nt:2 * nt]
        send_sems, recv_sems = refs[2 * nt:]
        x, y, c, _ = _mesh_pos()
        sibling = (x, y, 1 - c)
        copies = []
        for t in range(nt):
            for ch in range(4):
                copies.append(pltpu.make_async_remote_copy(
                    src_ref=xs[t].at[2 * ch + (1 - c)], dst_ref=outs[t].at[ch],
                    send_sem=send_sems.at[t, ch], recv_sem=recv_sems.at[t, ch],
                    device_id=sibling, device_id_type=MESH))
        for cp in copies:
            cp.start()
        for cp in copies:
            cp.wait()

    out_shape = [jax.ShapeDtypeStruct((4,) + g.shape[1:], g.dtype) for g in gs]
    return pl.pallas_call(
        body, name=name, in_specs=[ANY] * nt, out_specs=[ANY] * nt, out_shape=out_shape,
        scratch_shapes=[pltpu.SemaphoreType.DMA((nt, 4)), pltpu.SemaphoreType.DMA((nt, 4))],
    )(*gs)


def _pair_sum(name, gs, recv):
    c = lax.axis_index("c")
    outs = []
    for t, (g, rv) in enumerate(zip(gs, recv)):
        _, r, cols = rv.shape

        def body(c_ref, g_ref, r_ref, o_ref):
            o_ref[...] = (g_ref[...].astype(F32) + r_ref[...].astype(F32)).astype(o_ref.dtype)

        outs.append(pl.pallas_call(
            body, name=f"{name}_{t}",
            grid_spec=pltpu.PrefetchScalarGridSpec(
                num_scalar_prefetch=1, grid=(4,),
                in_specs=[pl.BlockSpec((None, r, cols), lambda ch, cr: (2 * ch + cr[0], 0, 0)),
                          pl.BlockSpec((None, r, cols), lambda ch, cr: (ch, 0, 0))],
                out_specs=pl.BlockSpec((None, r, cols), lambda ch, cr: (ch, 0, 0))),
            out_shape=jax.ShapeDtypeStruct(rv.shape, rv.dtype),
        )(jnp.reshape(c, (1,)).astype(jnp.int32), g, rv))
    return outs


def _exchange_chips(name, ps):
    nt = len(ps)

    def body(*refs):
        xs, outs = refs[:nt], refs[nt:2 * nt]
        send_sems, recv_sems = refs[2 * nt:]
        x, y, c, chips = _mesh_pos()
        copies = []
        for t in range(nt):
            for j, (cx, cy) in enumerate(chips):
                copies.append(pltpu.make_async_remote_copy(
                    src_ref=xs[t].at[2 * cx + cy], dst_ref=outs[t].at[j],
                    send_sem=send_sems.at[t, j], recv_sem=recv_sems.at[t, j],
                    device_id=(cx, cy, c), device_id_type=MESH))
        for cp in copies:
            cp.start()
        for cp in copies:
            cp.wait()

    out_shape = [jax.ShapeDtypeStruct((3,) + p.shape[1:], p.dtype) for p in ps]
    return pl.pallas_call(
        body, name=name, in_specs=[ANY] * nt, out_specs=[ANY] * nt, out_shape=out_shape,
        scratch_shapes=[pltpu.SemaphoreType.DMA((nt, 3)), pltpu.SemaphoreType.DMA((nt, 3))],
    )(*ps)


def _final_sum(name, ps, recv):
    chip = 2 * lax.axis_index("x") + lax.axis_index("y")
    outs = []
    for t, (p, rv) in enumerate(zip(ps, recv)):
        _, r, cols = rv.shape

        def body(c_ref, p_ref, r0_ref, r1_ref, r2_ref, o_ref):
            o_ref[...] = ((p_ref[...].astype(F32) + r0_ref[...].astype(F32)) + r1_ref[...].astype(F32)) + r2_ref[...].astype(F32)

        outs.append(pl.pallas_call(
            body, name=f"{name}_{t}",
            grid_spec=pltpu.PrefetchScalarGridSpec(
                num_scalar_prefetch=1, grid=(1,),
                in_specs=[pl.BlockSpec((None, r, cols), lambda i, cr: (cr[0], 0, 0))] +
                         [pl.BlockSpec((None, r, cols), lambda i, cr, j=j: (j, 0, 0)) for j in range(3)],
                out_specs=pl.BlockSpec((r, cols), lambda i, cr: (0, 0))),
            out_shape=jax.ShapeDtypeStruct((r, cols), F32),
        )(jnp.reshape(chip, (1,)).astype(jnp.int32), p, rv, rv, rv))
    return outs


def _reduce_scatter(tag, gs):
    recv1 = _exchange_sibling(f"rs_sibling_{tag}", gs)
    ps = _pair_sum(f"rs_pair_{tag}", gs, recv1)
    recv2 = _exchange_chips(f"rs_chips_{tag}", ps)
    return _final_sum(f"rs_sum_{tag}", ps, recv2)


def _sum_gathered(g):
    _, r, cols = g.shape

    def body(g_ref, o_ref):
        acc = g_ref[0]
        for d in range(1, NDEV):
            acc = acc + g_ref[d]
        o_ref[...] = acc

    return pl.pallas_call(body, name="small_sum", out_shape=jax.ShapeDtypeStruct((r, cols), F32))(g)


HBM = pl.BlockSpec(memory_space=pltpu.HBM)
SEM = pl.BlockSpec(memory_space=pltpu.SEMAPHORE)
EFFECT = pltpu.SideEffectType.DATAFLOW_SIDE_EFFECTING
NREL = NDEV - 1


def _related(k):
    x, y, c = lax.axis_index("x"), lax.axis_index("y"), lax.axis_index("c")
    px = 1 - x if k & 4 else x
    py = 1 - y if k & 2 else y
    pc = 1 - c if k & 1 else c
    return (px, py, pc), _dev_index(px, py, pc)


def _in_hbm(a):
    return pltpu.with_memory_space_constraint(a, pltpu.HBM)


ALL_RELS = tuple(range(1, NDEV))
CHIP_RELS = (4, 2, 6)


def _split_copies(xs, lands, send_sems, recv_sems, src_of, dst_of, rels):
    copies = []
    for t in range(len(xs)):
        for q, k in enumerate(rels):
            peer, peer_idx = _related(k)
            copies.append(pltpu.make_async_remote_copy(
                src_ref=src_of(xs[t], t, peer_idx), dst_ref=dst_of(lands[t], t, q, peer_idx),
                send_sem=send_sems.at[t * len(rels) + q], recv_sem=recv_sems.at[t * len(rels) + q],
                device_id=peer, device_id_type=MESH))
    return copies


def _exchange_start(name, xs, lands, src_of, dst_of, after, rels=ALL_RELS):
    nt = len(xs)

    def body(*refs):
        x_refs, land_refs = refs[:nt], refs[nt:2 * nt]
        send_sems, recv_sems = refs[2 * nt + 1], refs[2 * nt + 2]
        token = refs[-1]
        for cp in _split_copies(x_refs, land_refs, send_sems, recv_sems, src_of, dst_of, rels):
            cp.start()
        token[...] = jnp.zeros_like(token)

    res = pl.pallas_call(
        body, name=name,
        out_shape=(pltpu.SemaphoreType.DMA((nt * len(rels),)), pltpu.SemaphoreType.DMA((nt * len(rels),)),
                   *[pltpu.HBM(a.shape, a.dtype) for a in xs], *[pltpu.HBM(a.shape, a.dtype) for a in lands],
                   jax.ShapeDtypeStruct((8, 128), F32)),
        in_specs=[HBM] * (2 * nt) + [ANY],
        out_specs=(SEM, SEM, *[HBM] * (2 * nt), pl.BlockSpec(memory_space=pltpu.VMEM)),
        input_output_aliases={i: 2 + i for i in range(2 * nt)},
        compiler_params=pltpu.CompilerParams(has_side_effects=EFFECT),
    )(*[_in_hbm(a) for a in xs], *[_in_hbm(a) for a in lands], after)
    return res[0], res[1], res[2:2 + nt], res[2 + nt:2 + 2 * nt], res[-1]


def _exchange_wait(name, send_sems, recv_sems, xs, lands, src_of, dst_of, after, rels=ALL_RELS):
    nt = len(xs)

    def body(*refs):
        x_refs, land_refs = refs[:nt], refs[nt:2 * nt]
        send_sems, recv_sems = refs[2 * nt], refs[2 * nt + 1]
        for cp in _split_copies(x_refs, land_refs, send_sems, recv_sems, src_of, dst_of, rels):
            cp.wait_send()
            cp.wait_recv()

    res = pl.pallas_call(
        body, name=name,
        out_shape=(*[pltpu.HBM(a.shape, a.dtype) for a in xs], *[pltpu.HBM(a.shape, a.dtype) for a in lands]),
        in_specs=[HBM] * (2 * nt) + [SEM, SEM, ANY], out_specs=tuple([HBM] * (2 * nt)),
        input_output_aliases={i: i for i in range(2 * nt)},
        compiler_params=pltpu.CompilerParams(has_side_effects=EFFECT),
    )(*xs, *lands, send_sems, recv_sems, after)
    return res[:nt], res[nt:]


def _gather_start(name, shards, leads, after):
    def src_of(x_ref, t, peer_idx):
        return x_ref

    def dst_of(land_ref, t, k, peer_idx):
        me = _dev_index(lax.axis_index("x"), lax.axis_index("y"), lax.axis_index("c"))
        return land_ref.at[(slice(None),) * leads[t] + (me,)]

    lands = [lax.empty(s.shape[:ld] + (NDEV,) + s.shape[ld:], s.dtype) for s, ld in zip(shards, leads)]
    return _exchange_start(name, shards, lands, src_of, dst_of, after)


def _gather_wait(name, started, leads, after):
    send_sems, recv_sems, shards, lands, _ = started

    def src_of(x_ref, t, peer_idx):
        return x_ref

    def dst_of(land_ref, t, k, peer_idx):
        return land_ref.at[(slice(None),) * leads[t] + (peer_idx,)]

    shards, lands = _exchange_wait(name, send_sems, recv_sems, shards, lands, src_of, dst_of, after)
    me = _dev_index(lax.axis_index("x"), lax.axis_index("y"), lax.axis_index("c"))
    return [lax.dynamic_update_index_in_dim(g, s, me, ld) for g, s, ld in zip(lands, shards, leads)]


def _scatter_src(x_ref, t, peer_idx):
    return x_ref.at[peer_idx]


def _scatter_dst(land_ref, t, q, peer_idx):
    return land_ref.at[q]


def _chips_src(x_ref, t, peer_idx):
    return x_ref.at[peer_idx // 2]


def _chips_start(name, ps, after):
    lands = [lax.empty((len(CHIP_RELS),) + p.shape[1:], p.dtype) for p in ps]
    return _exchange_start(name, ps, lands, _chips_src, _scatter_dst, after, CHIP_RELS)


def _chips_wait(name, started, after):
    send_sems, recv_sems, ps, lands, _ = started
    return _exchange_wait(name, send_sems, recv_sems, ps, lands, _chips_src, _scatter_dst, after, CHIP_RELS)


def _scatter_start(name, gs, after):
    lands = [lax.empty((NREL,) + g.shape[1:], g.dtype) for g in gs]
    return _exchange_start(name, gs, lands, _scatter_src, _scatter_dst, after)


def _scatter_wait(name, started, after):
    send_sems, recv_sems, gs, lands, _ = started
    gs, lands = _exchange_wait(name, send_sems, recv_sems, gs, lands, _scatter_src, _scatter_dst, after)
    me = _dev_index(lax.axis_index("x"), lax.axis_index("y"), lax.axis_index("c"))
    outs = []
    for t, (g, rv) in enumerate(zip(gs, lands)):
        _, r, cols = rv.shape

        def body(c_ref, own_ref, rv_ref, o_ref):
            acc = own_ref[...].astype(F32)
            for k in range(NREL):
                acc = acc + rv_ref[k].astype(F32)
            o_ref[...] = acc

        outs.append(pl.pallas_call(
            body, name=f"{name}_sum_{t}",
            grid_spec=pltpu.PrefetchScalarGridSpec(
                num_scalar_prefetch=1, grid=(1,),
                in_specs=[pl.BlockSpec((None, r, cols), lambda i, cr: (cr[0], 0, 0)),
                          pl.BlockSpec((NREL, r, cols), lambda i, cr: (0, 0, 0))],
                out_specs=pl.BlockSpec((r, cols), lambda i, cr: (0, 0))),
            out_shape=jax.ShapeDtypeStruct((r, cols), F32), compiler_params=_cparams(VMEM_BIG),
        )(jnp.reshape(me, (1,)).astype(jnp.int32), g, rv))
    return outs


def _adamw(w, g, m, v):
    shape = w.shape
    cols = shape[-1]
    rows = math.prod(shape[:-1]) if len(shape) > 1 else 1
    w2, g2, m2, v2 = (jnp.reshape(t, (rows, cols)) for t in (w, g, m, v))
    tr = _pick(rows, (1024, 512, 256, 128)) if rows * cols > 65536 else rows
    c1 = 1.0 / (1.0 - ADAM_B1 ** ADAM_STEP)
    c2 = 1.0 / (1.0 - ADAM_B2 ** ADAM_STEP)

    def body(w_ref, g_ref, m_ref, v_ref, d_ref, nm_ref, nv_ref):
        gv = g_ref[...]
        nm = ADAM_B1 * m_ref[...] + (1.0 - ADAM_B1) * gv
        nv = ADAM_B2 * v_ref[...] + (1.0 - ADAM_B2) * (gv * gv)
        d_ref[...] = -ADAM_LR * ((nm * c1) / (jnp.sqrt(nv * c2) + ADAM_EPS) + ADAM_WD * w_ref[...])
        nm_ref[...] = nm
        nv_ref[...] = nv

    spec = pl.BlockSpec((tr, cols), lambda i: (i, 0))
    osh = jax.ShapeDtypeStruct((rows, cols), F32)
    d, nm, nv = pl.pallas_call(
        body, name="adamw", grid=(rows // tr,), in_specs=[spec] * 4, out_specs=[spec] * 3, out_shape=[osh] * 3,
        compiler_params=_cparams(VMEM_BIG),
    )(w2, g2, m2, v2)
    return jnp.reshape(d, shape), jnp.reshape(nm, shape), jnp.reshape(nv, shape)


def _pad_heads(w, name):
    if name not in P_HEADS:
        return w
    nh, real = P_HEADS[name]
    w = jnp.reshape(w, w.shape[:-2] + (nh, real, w.shape[-1]))
    w = jnp.pad(w, [(0, 0)] * (w.ndim - 2) + [(0, HP - real), (0, 0)])
    return jnp.reshape(w, w.shape[:-3] + (nh * HP, w.shape[-1]))


def _unpad_heads(w, name):
    if name not in P_HEADS:
        return w
    nh, real = P_HEADS[name]
    w = jnp.reshape(w, w.shape[:-2] + (nh, HP, w.shape[-1]))[..., :real, :]
    return jnp.reshape(w, w.shape[:-3] + (nh * real, w.shape[-1]))


def _win_pad(win_t):
    segs, o = {}, 0
    for n, s in zip(IN_NAMES, IN_SIZES):
        segs[n] = win_t[..., o:o + s, :]
        o += s
    return jnp.concatenate([_pad_heads(segs[n], n) for n in P_ORDER], axis=-2)


def _win_unpad(win_p):
    segs = {n: _unpad_heads(win_p[..., P_OFF[n]:P_OFF[n] + P_WIDTH[n], :], n) for n in P_ORDER}
    return jnp.concatenate([segs[n] for n in IN_NAMES], axis=-2)


def _t(w):
    return jnp.swapaxes(w, -1, -2)


def _layer_weights(g_g, g_u, g_d, g_in, g_pa, g_pb, g_out, gains, w2, b2, bm, gn, gq, gk):
    w2p = jnp.pad(jnp.reshape(w2, (2, GLA_RANK, GLA_H, GLA_DK)), ((0, 0), (0, HP - GLA_RANK), (0, 0), (0, HP - GLA_DK)))
    b2p = jnp.pad(jnp.reshape(b2, (2, 1, GLA_H, GLA_DK)), ((0, 0), (0, 0), (0, 0), (0, HP - GLA_DK)))
    wpb_t = jnp.pad(jnp.reshape(g_pb, (D, ATT_H, HEAD_DIM)), ((0, 0), (0, 0), (0, HP - HEAD_DIM)))
    return dict(
        gains=jnp.reshape(gains, (1, 3, 1, D)),
        wg_t=jnp.reshape(g_g, (1, 2, DFF, D)), wu_t=jnp.reshape(g_u, (1, 2, DFF, D)), wd=jnp.reshape(g_d, (1, 2, DFF, D)),
        win_t=_win_pad(jnp.reshape(g_in, (1, D_IN, D))), wpa_t=jnp.reshape(g_pa, (1, D, 512)),
        wpb_t=jnp.reshape(wpb_t, (1, D, ATT_H * HP)), wout=jnp.reshape(g_out, (1, D, D)),
        w2p=jnp.reshape(w2p, (1, 2, HP, GLA_H * HP)), b2p=jnp.reshape(b2p, (1, 2, 1, GLA_H * HP)),
        bm=jnp.reshape(bm, (1, 2, 1, D)), gn=jnp.reshape(gn, (1, 1, GLA_H * HP)),
        gq=jnp.pad(jnp.reshape(gq, (1, 1, HEAD_DIM)), ((0, 0), (0, 0), (0, HP - HEAD_DIM))),
        gk=jnp.pad(jnp.reshape(gk, (1, 1, HEAD_DIM)), ((0, 0), (0, 0), (0, HP - HEAD_DIM))))


def _layer_fwd(h, z, w, tabs, next_gain):
    h, z, s0 = _ffn_fwd(h, z, w["wg_t"], w["wu_t"], w["wd"], (0, 0), w["gains"][0, 1])
    h, z, s1 = _mixer_fwd(h, z, w, 0, tabs, w["gains"][0, 2])
    h, z, s2 = _ffn_fwd(h, z, w["wg_t"], w["wu_t"], w["wd"], (0, 1), next_gain)
    return h, z, (s0, s1, s2)


def _layer_bwd_upper(dh, dhb, saved, w, tabs):
    _, s1, s2 = saved
    dh, dhb, dg2, dwg1, dwu1, dwd1 = _ffn_bwd(dh, dhb, s2, w["gains"][0, 2], w["wg_t"], w["wu_t"], w["wd"], (0, 1))
    dh, dhb, gm = _mixer_bwd(dh, dhb, s1, w["gains"][0, 1], w, 0, tabs)
    gm.update(gain2=dg2, wg1=dwg1, wu1=dwu1, wd1=dwd1)
    return dh, dhb, gm


def _layer_bwd_lower(dh, dhb, saved, w, gm):
    dh, dhb, dg0, dwg0, dwu0, dwd0 = _ffn_bwd(dh, dhb, saved[0], w["gains"][0, 0], w["wg_t"], w["wu_t"], w["wd"], (0, 0))
    gm.update(gain0=dg0, wg0=dwg0, wu0=dwu0, wd0=dwd0)
    return dh, dhb, gm


def _layer_bwd(dh, dhb, saved, w, tabs):
    dh, dhb, gm = _layer_bwd_upper(dh, dhb, saved, w, tabs)
    return _layer_bwd_lower(dh, dhb, saved, w, gm)


def _blocks(ts):
    return [jnp.reshape(t, (NDEV, t.shape[0] // NDEV, t.shape[1])) for t in ts]


def _upper_grads(g):
    d_in = _win_unpad(g["win_t"])
    d_pb = jnp.reshape(jnp.reshape(g["wpb_t"], (D, ATT_H, HP))[:, :, :HEAD_DIM], (D, 512))
    return _blocks([g["wg1"], g["wu1"], g["wd1"], d_in, g["wpa_t"], d_pb, g["wout"]])


def _lower_grads(g):
    return _blocks([g["wg0"], g["wu0"], g["wd0"]])


def _big_grads(g):
    return _lower_grads(g) + _upper_grads(g)


def kernel(x, meta_tokens, norm_gains, ffn_w_gate, ffn_w_up, ffn_w_down, w_in, gla_w2, gla_b2, gla_gn, q_norm, k_norm, w_pa, w_pb, b_merge, w_out, final_norm, loss_target, m_meta_tokens, m_norm_gains, m_ffn_w_gate, m_ffn_w_up, m_ffn_w_down, m_w_in, m_gla_w2, m_gla_b2, m_gla_gn, m_q_norm, m_k_norm, m_w_pa, m_w_pb, m_b_merge, m_w_out, m_final_norm, v_meta_tokens, v_norm_gains, v_ffn_w_gate, v_ffn_w_up, v_ffn_w_down, v_w_in, v_gla_w2, v_gla_b2, v_gla_gn, v_q_norm, v_k_norm, v_w_pa, v_w_pb, v_b_merge, v_w_out, v_final_norm):
    dev = _dev_index(lax.axis_index("x"), lax.axis_index("y"), lax.axis_index("c"))
    sh_g = _t(ffn_w_gate).astype(BF16)
    sh_u = _t(ffn_w_up).astype(BF16)
    sh_d = ffn_w_down.astype(BF16)
    sh_in = _t(w_in).astype(BF16)
    sh_pa = _t(w_pa).astype(BF16)
    sh_pb = _t(w_pb).astype(BF16)
    sh_out = w_out.astype(BF16)
    small = jnp.concatenate([jnp.reshape(t, (-1, 128)) for t in
                             (meta_tokens, norm_gains, gla_w2, gla_b2, b_merge)], axis=0)
    small = jnp.pad(small, ((0, 2), (0, 0)))
    def shards(l):
        return [sh_g[l], sh_u[l], sh_d[l], sh_in[l], sh_pa[l], sh_pb[l], sh_out[l]]

    w_leads = [1, 1, 1, 0, 0, 0, 0]
    *g0, g_small = _all_gather("gather_layer0", shards(0) + [small], w_leads + [0])
    started = {1: _gather_start("gather_start_1", shards(1), w_leads, g_small)}
    meta_full = jnp.reshape(jnp.transpose(g_small[:, 0:16], (1, 0, 2)), (NMETA, D)) + started[1][4][0, 0]
    gains_full = jnp.reshape(jnp.transpose(jnp.reshape(g_small[:, 16:28], (NDEV, DEPTH, 3, 128)), (1, 2, 0, 3)), (DEPTH, 3, D))
    w2_full = jnp.reshape(jnp.transpose(jnp.reshape(g_small[:, 28:60], (NDEV, DEPTH, 2, GLA_RANK, 32)), (1, 2, 3, 0, 4)),
                          (DEPTH, 2, GLA_RANK, 256))
    b2_full = jnp.reshape(jnp.transpose(jnp.reshape(g_small[:, 60:62], (NDEV, DEPTH, 2, 32)), (1, 2, 0, 3)), (DEPTH, 2, 256))
    bm_full = jnp.reshape(jnp.transpose(jnp.reshape(g_small[:, 62:70], (NDEV, DEPTH, 2, 128)), (1, 2, 0, 3)), (DEPTH, 2, D))

    def layer_weights(l, gathered, gains_l):
        return _layer_weights(*gathered, gains_l, w2_full[l], b2_full[l], bm_full[l], gla_gn[l], q_norm[l], k_norm[l])

    xl = x[0]
    lp = xl.shape[0] + NULL + NMETA
    tabs = _rope_tables(lp)
    h = jnp.concatenate([jnp.zeros((NULL, D), F32), meta_full, xl], axis=0)
    weights, saved = [], []
    z = _rmsnorm_fwd(h, jnp.reshape(gains_full[0, 0], (1, D)))
    for l in range(DEPTH):
        tok = jnp.zeros((), F32)
        if 1 <= l < DEPTH - 1:
            started[l + 1] = _gather_start(f"gather_start_{l + 1}", shards(l + 1), w_leads, h)
            tok = started[l + 1][4][0, 0]
        gathered = g0 if l == 0 else _gather_wait(f"gather_wait_{l}", started[l], w_leads, h)
        weights.append(layer_weights(l, gathered, gains_full[l] + tok))
        next_gain = jnp.reshape(gains_full[l + 1, 0], (1, D)) if l + 1 < DEPTH else None
        h, z, sv = _layer_fwd(h, z, weights[l], tabs, next_gain)
        saved.append(sv)
    loss, dh, dhb, d_final = _loss_head(h, loss_target[0], jnp.reshape(final_norm, (1, D)))
    loss = lax.psum(loss[0, 0], ("x", "y", "c"))

    grads, scattering = [None] * DEPTH, {}
    tok = jnp.zeros((), F32)
    for l in reversed(range(DEPTH)):
        w = dict(weights[l], gains=weights[l]["gains"] + tok)
        if l > 0:
            dh, dhb, grads[l] = _layer_bwd(dh, dhb, saved[l], w, tabs)
            scattering[l] = _scatter_start(f"scatter_start_{l}", _big_grads(grads[l]), dhb)
            tok = scattering[l][4][0, 0]
        else:
            dh, dhb, gm = _layer_bwd_upper(dh, dhb, saved[l], w, tabs)
            ups = _upper_grads(gm)
            pair = _pair_sum("rs_pair_up", ups, _exchange_sibling("rs_sibling_up", ups))
            scattering[l] = _chips_start(f"scatter_start_{l}", pair, dhb)
            w = dict(w, gains=w["gains"] + scattering[l][4][0, 0])
            dh, dhb, grads[l] = _layer_bwd_lower(dh, dhb, saved[l], w, gm)
    grad_x = dh[NULL + NMETA:][None]
    red = [None] * DEPTH
    for l in reversed(range(1, DEPTH)):
        red[l] = _scatter_wait(f"scatter_wait_{l}", scattering[l], dhb)
    red_lower = _reduce_scatter("0", _lower_grads(grads[0]))
    pair, recv = _chips_wait("scatter_wait_0", scattering[0], red_lower[0])
    red[0] = red_lower + _final_sum("rs_sum_up", pair, recv)
    g_gate = jnp.stack([jnp.stack([_t(red[l][0]), _t(red[l][3])]) for l in range(DEPTH)])
    g_up = jnp.stack([jnp.stack([_t(red[l][1]), _t(red[l][4])]) for l in range(DEPTH)])
    g_down = jnp.stack([jnp.stack([red[l][2], red[l][5]]) for l in range(DEPTH)])
    g_win = jnp.stack([_t(red[l][6]) for l in range(DEPTH)])
    g_wpa = jnp.stack([_t(red[l][7]) for l in range(DEPTH)])
    g_wpb = jnp.stack([_t(red[l][8]) for l in range(DEPTH)])
    g_wout = jnp.stack([red[l][9] for l in range(DEPTH)])

    d_meta = dh[NULL:NULL + NMETA]
    d_gains = jnp.stack([jnp.concatenate([grads[l]["gain0"], grads[l]["gain"], grads[l]["gain2"]], axis=0)
                         for l in range(DEPTH)])
    d_w2 = jnp.stack([jnp.reshape(jnp.reshape(grads[l]["w2p"], (2, HP, GLA_H, HP))[:, :GLA_RANK, :, :GLA_DK],
                                  (2, GLA_RANK, 256)) for l in range(DEPTH)])
    d_b2 = jnp.stack([jnp.reshape(jnp.reshape(grads[l]["b2p"], (2, GLA_H, HP))[:, :, :GLA_DK], (2, 256))
                      for l in range(DEPTH)])
    d_gn = jnp.stack([grads[l]["gn"][0] for l in range(DEPTH)])
    d_gq = jnp.stack([grads[l]["gq"][0, :HEAD_DIM] for l in range(DEPTH)])
    d_gk = jnp.stack([grads[l]["gk"][0, :HEAD_DIM] for l in range(DEPTH)])
    d_bm = jnp.stack([jnp.concatenate([grads[l]["bma"], grads[l]["bmb"]], axis=0) for l in range(DEPTH)])
    parts = [d_meta, d_gains, d_w2, d_b2, d_gn, d_gq, d_gk, d_bm, d_final[0]]
    sizes = [p.size for p in parts]
    flat = jnp.concatenate([jnp.reshape(p, (-1,)) for p in parts])
    flat = jnp.reshape(flat, (-1, 128))
    nrow = flat.shape[0]
    flat = jnp.pad(flat, ((0, (-nrow) % 8), (0, 0)))
    (g_flat,) = _all_gather("gather_small_grads", [flat], [0])
    tot = jnp.reshape(_sum_gathered(g_flat), (-1,))
    full, o = [], 0
    for p, s in zip(parts, sizes):
        full.append(jnp.reshape(tot[o:o + s], p.shape))
        o += s
    f_meta, f_gains, f_w2, f_b2, f_gn, f_gq, f_gk, f_bm, f_final = full

    def mine(t, width):
        return lax.dynamic_slice_in_dim(t, dev * width, width, axis=t.ndim - 1)

    g_small = dict(meta_tokens=mine(f_meta, 128), norm_gains=mine(f_gains, 128), gla_w2=mine(f_w2, 32),
                   gla_b2=mine(f_b2, 32), gla_gn=f_gn, q_norm=f_gq, k_norm=f_gk, b_merge=mine(f_bm, 128),
                   final_norm=f_final)
    gr = dict(g_small, ffn_w_gate=g_gate, ffn_w_up=g_up, ffn_w_down=g_down, w_in=g_win, w_pa=g_wpa, w_pb=g_wpb,
              w_out=g_wout)
    ws = dict(meta_tokens=meta_tokens, norm_gains=norm_gains, ffn_w_gate=ffn_w_gate, ffn_w_up=ffn_w_up,
              ffn_w_down=ffn_w_down, w_in=w_in, gla_w2=gla_w2, gla_b2=gla_b2, gla_gn=gla_gn, q_norm=q_norm,
              k_norm=k_norm, w_pa=w_pa, w_pb=w_pb, b_merge=b_merge, w_out=w_out, final_norm=final_norm)
    ms = dict(meta_tokens=m_meta_tokens, norm_gains=m_norm_gains, ffn_w_gate=m_ffn_w_gate, ffn_w_up=m_ffn_w_up,
              ffn_w_down=m_ffn_w_down, w_in=m_w_in, gla_w2=m_gla_w2, gla_b2=m_gla_b2, gla_gn=m_gla_gn, q_norm=m_q_norm,
              k_norm=m_k_norm, w_pa=m_w_pa, w_pb=m_w_pb, b_merge=m_b_merge, w_out=m_w_out, final_norm=m_final_norm)
    vs = dict(meta_tokens=v_meta_tokens, norm_gains=v_norm_gains, ffn_w_gate=v_ffn_w_gate, ffn_w_up=v_ffn_w_up,
              ffn_w_down=v_ffn_w_down, w_in=v_w_in, gla_w2=v_gla_w2, gla_b2=v_gla_b2, gla_gn=v_gla_gn, q_norm=v_q_norm,
              k_norm=v_k_norm, w_pa=v_w_pa, w_pb=v_w_pb, b_merge=v_b_merge, w_out=v_w_out, final_norm=v_final_norm)
    names = ["meta_tokens", "norm_gains", "ffn_w_gate", "ffn_w_up", "ffn_w_down", "w_in", "gla_w2", "gla_b2", "gla_gn",
             "q_norm", "k_norm", "w_pa", "w_pb", "b_merge", "w_out", "final_norm"]
    deltas, new_m, new_v = [], [], []
    for n in names:
        dlt, nm, nv = _adamw(ws[n], gr[n], ms[n], vs[n])
        deltas.append(dlt)
        new_m.append(nm)
        new_v.append(nv)
    return (loss, grad_x, *[gr[n] for n in names], *deltas, *new_m, *new_v)
```

```python
import functools
import math

import jax
import jax.numpy as jnp
import numpy as np
from jax import lax
from jax.experimental import pallas as pl
from jax.experimental.pallas import tpu as pltpu

F32 = jnp.float32
BF16 = jnp.bfloat16
MESH = pl.DeviceIdType.MESH
ANY = pl.BlockSpec(memory_space=pl.ANY)

NDEV = 8
D = 1024
DFF = 2816
DEPTH = 4
NMETA = 16
NULL = 112
GRID_W = 64
EPS = 1e-6
HP = 128
GLA_H = 4
GLA_DK = 64
GLA_RANK = 16
GLA_TAU = 16.0
CHUNK = 64
ATT_H = 8
ATT_KV = 2
ATT_G = ATT_H // ATT_KV
HEAD_DIM = 64
ROPE_THETA = 10000.0

IN_SIZES = (256, 256, 512, 512, 16, 16, 512, 128, 128, 1024, 1024)
IN_NAMES = ("qa", "ka", "va", "ra", "lrf", "lrb", "qb", "kb", "vb", "ga", "gb")
D_IN = sum(IN_SIZES)
P_ORDER = ("qb", "ga", "gb", "qa", "ka", "va", "ra", "kb", "vb", "lrf", "lrb")
P_WIDTH = dict(qb=1024, ga=1024, gb=1024, qa=512, ka=512, va=512, ra=512, kb=256, vb=256, lrf=128, lrb=128)
P_OFF = {}
_o = 0
for _n in P_ORDER:
    P_OFF[_n] = _o
    _o += P_WIDTH[_n]
D_INP = _o
P_HEADS = dict(qa=(4, 64), ka=(4, 64), qb=(8, 64), kb=(2, 64), vb=(2, 64), lrf=(1, 16), lrb=(1, 16))

ADAM_LR = 0.001
ADAM_B1 = 0.9
ADAM_B2 = 0.999
ADAM_EPS = 1e-08
ADAM_WD = 0.01
ADAM_STEP = 10

VMEM_BIG = 56 * 1024 * 1024
MXU_N = 256


def _cparams(vmem=None):
    return pltpu.CompilerParams(vmem_limit_bytes=vmem) if vmem else pltpu.CompilerParams()


def _pick(n, prefs):
    for p in prefs:
        if n % p == 0:
            return p
    return n


def _tm(lp):
    return _pick(lp, (528, 512, 256, 128))


_DN = {"nn": (((1,), (0,)), ((), ())), "nt": (((1,), (1,)), ((), ())), "tn": (((0,), (0,)), ((), ()))}


def _dot(a, b, mode="nn", precision=None):
    return lax.dot_general(a, b, _DN[mode], preferred_element_type=F32, precision=precision)


def _split(x):
    hi = x.astype(BF16)
    return hi, (x - hi.astype(F32)).astype(BF16)


def _dot_sel(t, x, mode="nn"):
    hi, lo = _split(x)
    return _dot(t, hi, mode) + _dot(t, lo, mode)


def _dot3(a, b, mode="nn"):
    ah, al = _split(a)
    bh, bl = _split(b)
    return _dot(ah, bh, mode) + (_dot(ah, bl, mode) + _dot(al, bh, mode))


def _sigmoid(x):
    return 0.5 * jnp.tanh(0.5 * x) + 0.5


def _mm(name, m, n, terms, outs, epilogue, extras=(), *, tm, tn, nk=1, nsub=1, i_outer=False, vmem=None):
    gm, gn = m // tm, n // tn
    assert gm * tm == m and gn * tn == n, (name, m, n, tm, tn)
    n_acc = 1 + max(t[3] for t in terms)

    def gmap(f):
        if i_outer:
            return lambda i, j, kk: f(i, j, kk)
        return lambda j, i, kk: f(i, j, kk)

    in_specs, args = [], []
    for a, b, mode, _, pa, pb in terms:
        kdim = a.shape[-2] if mode == "tn" else a.shape[-1]
        tk = kdim // nk
        assert tk * nk == kdim
        na, nb = (None,) * len(pa), (None,) * len(pb)
        if mode == "tn":
            in_specs.append(pl.BlockSpec(na + (tk, tm), gmap(lambda i, j, kk, pa=pa: pa + (kk, i))))
        else:
            in_specs.append(pl.BlockSpec(na + (tm, tk), gmap(lambda i, j, kk, pa=pa: pa + (i, kk))))
        if mode == "nt":
            in_specs.append(pl.BlockSpec(nb + (tn, tk), gmap(lambda i, j, kk, pb=pb: pb + (j, kk))))
        else:
            in_specs.append(pl.BlockSpec(nb + (tk, tn), gmap(lambda i, j, kk, pb=pb: pb + (kk, j))))
        args += [a, b]
    for arr, kind, off, pe in extras:
        ne = (None,) * len(pe)
        if kind == "mn":
            in_specs.append(pl.BlockSpec(ne + (tm, tn), gmap(lambda i, j, kk, off=off, pe=pe: pe + (i, j + off))))
        else:
            in_specs.append(pl.BlockSpec(ne + (1, tn), gmap(lambda i, j, kk, off=off, pe=pe: pe + (0, j + off))))
        args.append(arr)
    out_shape, out_specs = [], []
    for shape, dtype, kind, off, po in outs:
        no = (None,) * len(po)
        out_shape.append(jax.ShapeDtypeStruct(shape, dtype))
        if kind == "mn":
            out_specs.append(pl.BlockSpec(no + (tm, tn), gmap(lambda i, j, kk, off=off, po=po: po + (i, j + off))))
        else:
            assert not i_outer
            out_specs.append(pl.BlockSpec(no + (1, tn), gmap(lambda i, j, kk, off=off, po=po: po + (0, j + off))))
    n_t, n_e, n_o = len(terms), len(extras), len(outs)
    i_axis = 0 if i_outer else 1

    def body(*refs):
        ins = refs[: 2 * n_t]
        exs = refs[2 * n_t: 2 * n_t + n_e]
        ors = refs[2 * n_t + n_e: 2 * n_t + n_e + n_o]
        accs = refs[2 * n_t + n_e + n_o:]
        i = pl.program_id(i_axis)
        kk = pl.program_id(2)

        def partials(cs):
            part = [None] * n_acc
            for t, (_, _, mode, ai, _, _) in enumerate(terms):
                b_ref = ins[2 * t + 1]
                b_val = b_ref[cs, :] if mode == "nt" else b_ref[:, cs]
                p = _dot(ins[2 * t][...], b_val, mode)
                part[ai] = p if part[ai] is None else part[ai] + p
            return part

        def finish(vals, cs):
            res = epilogue(vals, [e[:, cs] for e in exs], i * tm)
            for (_, dtype, kind, _, _), o_ref, v in zip(outs, ors, res):
                if kind == "mn":
                    o_ref[:, cs] = v.astype(dtype)
                else:
                    @pl.when(i == 0)
                    def _():
                        o_ref[:, cs] = v.astype(dtype)

                    @pl.when(i != 0)
                    def _():
                        o_ref[:, cs] += v.astype(dtype)

        if nk == 1:
            w = tn // nsub
            for s in range(nsub):
                cs = slice(s * w, (s + 1) * w)
                finish(partials(cs), cs)
        else:
            part = partials(slice(None))
            @pl.when(kk == 0)
            def _():
                for a_ref, p in zip(accs, part):
                    a_ref[...] = p

            @pl.when(kk != 0)
            def _():
                for a_ref, p in zip(accs, part):
                    a_ref[...] += p

            @pl.when(kk == nk - 1)
            def _():
                finish([a_ref[...] for a_ref in accs], slice(None))

    scratch = [pltpu.VMEM((tm, tn), F32) for _ in range(n_acc)] if nk > 1 else []
    grid = (gm, gn, nk) if i_outer else (gn, gm, nk)
    res = pl.pallas_call(
        body, name=name, grid=grid, in_specs=in_specs, out_specs=out_specs, out_shape=out_shape,
        scratch_shapes=scratch, compiler_params=_cparams(vmem),
    )(*args)
    return res


def _term(a, b, mode, acc=0, pa=(), pb=()):
    return (a, b, mode, acc, tuple(pa), tuple(pb))


def _row_tile(lp):
    return _pick(lp, (384, 256, 128))


def _rmsnorm_fwd(h, gain):
    lp = h.shape[0]
    tr = _row_tile(lp)

    def body(h_ref, g_ref, z_ref):
        x = h_ref[...]
        r = lax.rsqrt(jnp.mean(x * x, axis=-1, keepdims=True) + EPS)
        z_ref[...] = (x * r * g_ref[...]).astype(BF16)

    return pl.pallas_call(
        body, name="rmsnorm_fwd", grid=(lp // tr,),
        in_specs=[pl.BlockSpec((tr, D), lambda i: (i, 0)), pl.BlockSpec((1, D), lambda i: (0, 0))],
        out_specs=pl.BlockSpec((tr, D), lambda i: (i, 0)),
        out_shape=jax.ShapeDtypeStruct((lp, D), BF16),
    )(h, gain)


def _loss_head(h, target, gain):
    lp = h.shape[0]
    tr = 128

    def body(h_ref, t_ref, g_ref, loss_ref, dh_ref, dhb_ref, dg_ref):
        i = pl.program_id(0)

        @pl.when(i == 0)
        def _():
            loss_ref[...] = jnp.zeros_like(loss_ref)
            dg_ref[...] = jnp.zeros_like(dg_ref)
            dh_ref[...] = jnp.zeros_like(dh_ref)
            dhb_ref[...] = jnp.zeros_like(dhb_ref)

        @pl.when(i != 0)
        def _():
            x = h_ref[...]
            g = g_ref[...]
            r = lax.rsqrt(jnp.mean(x * x, axis=-1, keepdims=True) + EPS)
            xh = x * r
            y = xh * g
            err = y - t_ref[...]
            loss_ref[...] += 0.5 * jnp.sum(jnp.sum(err * err, axis=-1, keepdims=True), axis=0, keepdims=True) / D
            dy = err * (1.0 / D)
            dg_ref[...] += jnp.sum(dy * xh, axis=0, keepdims=True)
            dxh = dy * g
            dx = r * (dxh - xh * jnp.mean(dxh * xh, axis=-1, keepdims=True))
            dh_ref[...] = dx
            dhb_ref[...] = dx.astype(BF16)

    row = pl.BlockSpec((tr, D), lambda i: (i, 0))
    vec = pl.BlockSpec((1, D), lambda i: (0, 0))
    return pl.pallas_call(
        body, name="loss_head", grid=(lp // tr,),
        in_specs=[row, pl.BlockSpec((tr, D), lambda i: (jnp.maximum(i - 1, 0), 0)), vec],
        out_specs=[pl.BlockSpec((1, 1), lambda i: (0, 0)), row, row, vec],
        out_shape=[jax.ShapeDtypeStruct((1, 1), F32), jax.ShapeDtypeStruct((lp, D), F32),
                   jax.ShapeDtypeStruct((lp, D), BF16), jax.ShapeDtypeStruct((1, D), F32)],
    )(h, target, gain)


def _silu_parts(g):
    s = _sigmoid(g)
    return g * s, s * (1.0 + g * (1.0 - s))


def _residual_norm_epi(scale, with_norm):
    def epi(accs, exs, row0):
        h2 = exs[0] + scale * accs[0]
        if not with_norm:
            return [h2]
        r = lax.rsqrt(jnp.mean(h2 * h2, axis=-1, keepdims=True) + EPS)
        return [h2, h2 * r * exs[1]]
    return epi


def _norm_bwd_epi(accs, exs, row0):
    dz = accs[0]
    x, res, g = exs
    r = lax.rsqrt(jnp.mean(x * x, axis=-1, keepdims=True) + EPS)
    xh = x * r
    dxh = dz * g
    dx = r * (dxh - xh * jnp.mean(dxh * xh, axis=-1, keepdims=True))
    rows = row0 + lax.broadcasted_iota(jnp.int32, (dz.shape[0], 1), 0)
    dh = jnp.where(rows >= NULL, res + dx, 0.0)
    return [dh, dh, jnp.sum(dz * xh, axis=0, keepdims=True)]


def _norm_bwd_outs(lp):
    return [((lp, D), F32, "mn", 0, ()), ((lp, D), BF16, "mn", 0, ()), ((1, D), F32, "nsum", 0, ())]


def _ffn_fwd(h, z, wg_t, wu_t, wd, pre, next_gain):
    lp = h.shape[0]
    tm = _tm(lp)

    def up_epi(accs, exs, row0):
        g, u = accs
        sg, _ = _silu_parts(g)
        return [g, u, sg * u]

    bshape = (lp, DFF)
    g_, u_, act = _mm("ffn_up", lp, DFF, [_term(z, wg_t, "nt", 0, (), pre), _term(z, wu_t, "nt", 1, (), pre)],
                      [(bshape, BF16, "mn", 0, ())] * 3, up_epi, tm=tm, tn=DFF, nsub=DFF // MXU_N, vmem=VMEM_BIG)

    with_norm = next_gain is not None
    res = _mm("ffn_down", lp, D, [_term(act, wd, "nn", 0, (), pre)],
              [((lp, D), F32, "mn", 0, ())] + ([((lp, D), BF16, "mn", 0, ())] if with_norm else []),
              _residual_norm_epi(0.5, with_norm),
              extras=[(h, "mn", 0, ())] + ([(next_gain, "n", 0, ())] if with_norm else []),
              tm=tm, tn=D, i_outer=True, vmem=VMEM_BIG)
    return res[0], (res[1] if with_norm else None), dict(h=h, z=z, g=g_, u=u_, act=act)


def _dw(name, a, b, m, n, scale=1.0):
    lp = a.shape[0]
    tm = _pick(m, (2944, 1408, 1024, 512, 256, 128))
    tn = _pick(n, (1024, 512, 256, 128))
    nk = lp // _pick(lp, (2112, 256, 128) if tm <= 1408 else (1056, 256, 128))

    def epi(accs, exs, row0):
        return [accs[0] * scale]

    (w,) = _mm(name, m, n, [_term(a, b, "tn")], [((m, n), BF16, "mn", 0, ())], epi, tm=tm, tn=tn, nk=nk,
               i_outer=True, vmem=VMEM_BIG)
    return w


def _ffn_bwd(dh, dhb, sv, gain, wg_t, wu_t, wd, pre):
    lp = dh.shape[0]
    tm = _tm(lp)

    def dact_epi(accs, exs, row0):
        g = exs[0].astype(F32)
        u = exs[1].astype(F32)
        da = 0.5 * accs[0]
        sg, dsg = _silu_parts(g)
        return [da * u * dsg, da * sg]

    dg_, du_ = _mm("ffn_dact", lp, DFF, [_term(dhb, wd, "nt", 0, (), pre)],
                   [((lp, DFF), BF16, "mn", 0, ())] * 2, dact_epi,
                   extras=[(sv["g"], "mn", 0, ()), (sv["u"], "mn", 0, ())], tm=tm, tn=DFF, nsub=DFF // MXU_N,
                   vmem=VMEM_BIG)
    d_wd = _dw("dw_down", sv["act"], dhb, DFF, D, 0.5)
    d_wg = _dw("dw_gate", dg_, sv["z"], DFF, D)
    d_wu = _dw("dw_up", du_, sv["z"], DFF, D)

    nk = 1
    dh2, dhb2, dgain = _mm("ffn_dz", lp, D, [_term(dg_, wg_t, "nn", 0, (), pre), _term(du_, wu_t, "nn", 0, (), pre)],
                           _norm_bwd_outs(lp), _norm_bwd_epi,
                           extras=[(sv["h"], "mn", 0, ()), (dh, "mn", 0, ()), (gain, "n", 0, ())],
                           tm=tm, tn=D, nk=nk, vmem=VMEM_BIG)
    return dh2, dhb2, dgain, d_wg, d_wu, d_wd


def _gla_gates(hin, w2p, b2p):
    lp = hin.shape[0]
    tr = _row_tile(lp)
    bf, bb = P_OFF["lrf"] // HP, P_OFF["lrb"] // HP

    def body(lf_ref, lb_ref, w_ref, b_ref, o_ref, c_ref):
        i = pl.program_id(0)
        rows = i * tr + lax.broadcasted_iota(jnp.int32, (tr, 1), 0)
        r = lax.broadcasted_iota(jnp.int32, (tr, tr), 0)
        c = lax.broadcasted_iota(jnp.int32, (tr, tr), 1)
        same = (r // CHUNK) == (c // CHUNK)
        for d, l_ref in enumerate((lf_ref, lb_ref)):
            logit = _dot3(l_ref[...], w_ref[d]) + b_ref[d]
            g = jnp.where(rows >= NULL, jax.nn.log_sigmoid(logit) * (1.0 / GLA_TAU), 0.0)
            o_ref[d] = g
            tmat = jnp.where(same & ((r >= c) if d == 0 else (r <= c)), 1.0, 0.0).astype(BF16)
            c_ref[d] = _dot_sel(tmat, g)

    spec = pl.BlockSpec((2, tr, 512), lambda i: (0, i, 0))
    return pl.pallas_call(
        body, name="gla_gates", grid=(lp // tr,),
        in_specs=[pl.BlockSpec((tr, HP), lambda i: (i, bf)), pl.BlockSpec((tr, HP), lambda i: (i, bb)),
                  pl.BlockSpec((2, HP, 512), lambda i: (0, 0, 0)), pl.BlockSpec((2, 1, 512), lambda i: (0, 0, 0))],
        out_specs=[spec, spec],
        out_shape=[jax.ShapeDtypeStruct((2, lp, 512), F32)] * 2,
    )(hin, hin, w2p, b2p)


def _gla_rows(lp):
    return _pick(lp, (384, 256, 128))


def _tri(d):
    r = lax.broadcasted_iota(jnp.int32, (CHUNK, CHUNK), 0)
    c = lax.broadcasted_iota(jnp.int32, (CHUNK, CHUNK), 1)
    return (r >= c) if d == 0 else (r <= c)


def _gla_fwd(hin, gates):
    lp = hin.shape[0]
    rb = _gla_rows(lp)
    nb = lp // rb
    cpb = rb // CHUNK
    nchunk = lp // CHUNK
    qo, ko, vo = P_OFF["qa"] // 512, P_OFF["ka"] // 512, P_OFF["va"] // 512
    scale = GLA_DK ** -0.5

    def body(qf, kf, vf, gf, qb, kb, vb_, gb, of, ob, sf, sb, st):
        @pl.when(pl.program_id(0) == 0)
        def _():
            st[...] = jnp.zeros_like(st)

        ins = ((qf, kf, vf, gf, of, sf), (qb, kb, vb_, gb, ob, sb))
        for ci in range(cpb):
            for d in range(2):
                q_ref, k_ref, v_ref, g_ref, o_ref, s_ref = ins[d]
                tri = _tri(d)
                c = ci if d == 0 else cpb - 1 - ci
                rows = slice(c * CHUNK, (c + 1) * CHUNK)
                for h in range(GLA_H):
                    sl = slice(h * HP, (h + 1) * HP)
                    q = q_ref[rows, sl] * scale
                    k = k_ref[rows, sl]
                    v = v_ref[rows, sl]
                    b = g_ref[rows, sl]
                    btot = b[CHUNK - 1:CHUNK] if d == 0 else b[0:1]
                    qd = (q * jnp.exp(b)).astype(BF16)
                    ki = (k * jnp.exp(-b)).astype(BF16)
                    ke = (k * jnp.exp(btot - b)).astype(BF16)
                    vb = v.astype(BF16)
                    att = jnp.where(tri, _dot(qd, ki, "nt"), 0.0)
                    s_prev = st[d, h]
                    o_ref[rows, sl] = _dot(att.astype(BF16), vb) + _dot(qd, s_prev.astype(BF16), "nt")
                    s_ref[h, c] = s_prev
                    st[d, h] = s_prev * jnp.exp(btot) + _dot(vb, ke, "tn")

    def specs(off):
        return (pl.BlockSpec((rb, 512), lambda b: (b, off)), pl.BlockSpec((rb, 512), lambda b: (nb - 1 - b, off)))

    (qf, qb), (kf, kb), (vf, vb2) = specs(qo), specs(ko), specs(vo)
    gf = pl.BlockSpec((None, rb, 512), lambda b: (0, b, 0))
    gb = pl.BlockSpec((None, rb, 512), lambda b: (1, nb - 1 - b, 0))
    of, ob = specs(0)
    sf = pl.BlockSpec((GLA_H, cpb, HP, HP), lambda b: (0, b, 0, 0))
    sb = pl.BlockSpec((GLA_H, cpb, HP, HP), lambda b: (0, nb - 1 - b, 0, 0))
    osh = jax.ShapeDtypeStruct((lp, GLA_H * HP), F32)
    ssh = jax.ShapeDtypeStruct((GLA_H, nchunk, HP, HP), F32)
    return pl.pallas_call(
        body, name="gla_fwd", grid=(nb,),
        in_specs=[qf, kf, vf, gf, qb, kb, vb2, gb], out_specs=[of, ob, sf, sb], out_shape=[osh, osh, ssh, ssh],
        scratch_shapes=[pltpu.VMEM((2, GLA_H, HP, HP), F32)], compiler_params=_cparams(VMEM_BIG),
    )(hin, hin, hin, gates, hin, hin, hin, gates)


def _gla_bwd(hin, gates, states, do):
    lp = hin.shape[0]
    rb = _gla_rows(lp)
    nb = lp // rb
    cpb = rb // CHUNK
    qo, ko, vo = P_OFF["qa"] // 512, P_OFF["ka"] // 512, P_OFF["va"] // 512
    scale = GLA_DK ** -0.5

    def body(qf, kf, vf, gf, sf, dof, qb, kb, vb_, gb, sb, dob,
             dqf, dkf, dvf, dgf, dqb, dkb, dvb, dgb, dst):
        @pl.when(pl.program_id(0) == 0)
        def _():
            dst[...] = jnp.zeros_like(dst)

        ins = ((qf, kf, vf, gf, sf, dof, dqf, dkf, dvf, dgf), (qb, kb, vb_, gb, sb, dob, dqb, dkb, dvb, dgb))
        for ci in range(cpb):
            for d in range(2):
                q_ref, k_ref, v_ref, g_ref, s_ref, do_ref, dq_ref, dk_ref, dv_ref, dg_ref = ins[d]
                tri, tri_t = _tri(d), _tri(1 - d)
                edge = lax.broadcasted_iota(jnp.int32, (CHUNK, 1), 0) == (CHUNK - 1 if d == 0 else 0)
                c = cpb - 1 - ci if d == 0 else ci
                rows = slice(c * CHUNK, (c + 1) * CHUNK)
                for h in range(GLA_H):
                    sl = slice(h * HP, (h + 1) * HP)
                    q = q_ref[rows, sl] * scale
                    k = k_ref[rows, sl]
                    v = v_ref[rows, sl]
                    dout = do_ref[rows, sl].astype(BF16)
                    b = g_ref[rows, sl]
                    btot = b[CHUNK - 1:CHUNK] if d == 0 else b[0:1]
                    e = jnp.exp(b)
                    ei = jnp.exp(-b)
                    et = jnp.exp(btot - b)
                    etot = jnp.exp(btot)
                    qd = q * e
                    ki = k * ei
                    ke = k * et
                    qdb, kib, keb, vb = qd.astype(BF16), ki.astype(BF16), ke.astype(BF16), v.astype(BF16)
                    att_t = jnp.where(tri_t, _dot(kib, qdb, "nt"), 0.0).astype(BF16)
                    d_att = jnp.where(tri, _dot(dout, vb, "nt"), 0.0).astype(BF16)
                    d_att_t = jnp.where(tri_t, _dot(vb, dout, "nt"), 0.0).astype(BF16)
                    s_prev = s_ref[h, c]
                    ds_t = dst[d, h]
                    ds_b = ds_t.astype(BF16)
                    dv = _dot(att_t, dout) + _dot(keb, ds_b, "nt")
                    d_qd = _dot(d_att, kib) + _dot(dout, s_prev.astype(BF16))
                    d_ki = _dot(d_att_t, qdb)
                    d_ke = _dot(vb, ds_b)
                    d_e = jnp.sum(s_prev * ds_t, axis=0, keepdims=True)
                    dst[d, h] = _dot(dout, qdb, "tn") + ds_t * etot
                    db = d_qd * qd - d_ki * ki - d_ke * ke
                    dbtot = jnp.sum(d_ke * ke, axis=0, keepdims=True) + d_e * etot
                    dq_ref[rows, sl] = d_qd * e * scale
                    dk_ref[rows, sl] = d_ki * ei + d_ke * et
                    dv_ref[rows, sl] = dv
                    dg_ref[rows, sl] = db + jnp.where(edge, dbtot, 0.0)

    def fw(off):
        return pl.BlockSpec((rb, 512), lambda b: (nb - 1 - b, off))

    def bw(off):
        return pl.BlockSpec((rb, 512), lambda b: (b, off))

    gf = pl.BlockSpec((None, rb, 512), lambda b: (0, nb - 1 - b, 0))
    gb = pl.BlockSpec((None, rb, 512), lambda b: (1, b, 0))
    sf = pl.BlockSpec((GLA_H, cpb, HP, HP), lambda b: (0, nb - 1 - b, 0, 0))
    sb = pl.BlockSpec((GLA_H, cpb, HP, HP), lambda b: (0, b, 0, 0))
    osh = jax.ShapeDtypeStruct((lp, GLA_H * HP), F32)
    res = pl.pallas_call(
        body, name="gla_bwd", grid=(nb,),
        in_specs=[fw(qo), fw(ko), fw(vo), gf, sf, fw(0), bw(qo), bw(ko), bw(vo), gb, sb, bw(0)],
        out_specs=[fw(0)] * 4 + [bw(0)] * 4, out_shape=[osh] * 8,
        scratch_shapes=[pltpu.VMEM((2, GLA_H, HP, HP), F32)], compiler_params=_cparams(VMEM_BIG),
    )(hin, hin, hin, gates, states[0], do, hin, hin, hin, gates, states[1], do)
    return res[:4], res[4:]


def _gla_out_fwd(o2, hin, gn):
    lp = hin.shape[0]
    tr = _row_tile(lp)
    ro = P_OFF["ra"] // 512

    def body(of_ref, ob_ref, r_ref, gn_ref, a_ref):
        r = r_ref[...]
        sr, _ = _silu_parts(r)
        for h in range(GLA_H):
            sl = slice(h * HP, (h + 1) * HP)
            o = of_ref[:, sl] + ob_ref[:, sl]
            rs = lax.rsqrt(jnp.mean(o * o, axis=-1, keepdims=True) + EPS)
            a_ref[:, sl] = (o * rs * gn_ref[:, sl] * sr[:, sl]).astype(BF16)

    row = pl.BlockSpec((tr, 512), lambda i: (i, 0))
    return pl.pallas_call(
        body, name="gla_out_fwd", grid=(lp // tr,),
        in_specs=[row, row, pl.BlockSpec((tr, 512), lambda i: (i, ro)), pl.BlockSpec((1, 512), lambda i: (0, 0))],
        out_specs=row,
        out_shape=jax.ShapeDtypeStruct((lp, 512), BF16),
    )(o2[0], o2[1], hin, gn)


def _gla_out_bwd(da, o2, hin, gn):
    lp = hin.shape[0]
    tr = _row_tile(lp)
    ro = P_OFF["ra"] // 512

    def body(da_ref, of_ref, ob_ref, r_ref, gn_ref, do_ref, dr_ref, dgn_ref):
        i = pl.program_id(0)
        r = r_ref[...]
        sr, dsr = _silu_parts(r)
        da_v = da_ref[...]
        parts = []
        for h in range(GLA_H):
            sl = slice(h * HP, (h + 1) * HP)
            o = of_ref[:, sl] + ob_ref[:, sl]
            rs = lax.rsqrt(jnp.mean(o * o, axis=-1, keepdims=True) + EPS)
            oh = o * rs
            gn_h = gn_ref[:, sl]
            dah = da_v[:, sl]
            dr_ref[:, sl] = (dah * oh * gn_h * dsr[:, sl]).astype(BF16)
            t = dah * sr[:, sl]
            parts.append(jnp.sum(t * oh, axis=0, keepdims=True))
            doh = t * gn_h
            do_ref[:, sl] = rs * (doh - oh * jnp.mean(doh * oh, axis=-1, keepdims=True))
        part = jnp.concatenate(parts, axis=1)

        @pl.when(i == 0)
        def _():
            dgn_ref[...] = part

        @pl.when(i != 0)
        def _():
            dgn_ref[...] += part

    row = pl.BlockSpec((tr, 512), lambda i: (i, 0))
    return pl.pallas_call(
        body, name="gla_out_bwd", grid=(lp // tr,),
        in_specs=[row, row, row, pl.BlockSpec((tr, 512), lambda i: (i, ro)), pl.BlockSpec((1, 512), lambda i: (0, 0))],
        out_specs=[row, row, pl.BlockSpec((1, 512), lambda i: (0, 0))],
        out_shape=[jax.ShapeDtypeStruct((lp, 512), F32), jax.ShapeDtypeStruct((lp, 512), BF16),
                   jax.ShapeDtypeStruct((1, 512), F32)],
    )(da, o2[0], o2[1], hin, gn)


def _gla_in_bwd(gf, gb, gates, hin, w2p, others):
    lp = hin.shape[0]
    tr = _row_tile(lp)
    bf, bb = P_OFF["lrf"] // HP, P_OFF["lrb"] // HP
    names = tuple(others)

    def seg(name):
        return slice(P_OFF[name], P_OFF[name] + P_WIDTH[name])

    def body(dqf_ref, dkf_ref, dvf_ref, dgf_ref, dqb_ref, dkb_ref, dvb_ref, dgb_ref, g_ref, lf_ref, lb_ref, w_ref,
             *rest):
        other_refs, (o_ref, dw_ref, db_ref) = rest[:len(names)], rest[len(names):]
        i = pl.program_id(0)
        for n, ref in zip(names, other_refs):
            o_ref[:, seg(n)] = ref[...].astype(BF16)
        o_ref[:, seg("qa")] = (dqf_ref[...] + dqb_ref[...]).astype(BF16)
        o_ref[:, seg("ka")] = (dkf_ref[...] + dkb_ref[...]).astype(BF16)
        o_ref[:, seg("va")] = (dvf_ref[...] + dvb_ref[...]).astype(BF16)
        olr_ref = o_ref.at[:, P_OFF["lrf"]:P_OFF["lrf"] + 2 * HP]
        rows = i * tr + lax.broadcasted_iota(jnp.int32, (tr, 1), 0)
        r = lax.broadcasted_iota(jnp.int32, (tr, tr), 0)
        c = lax.broadcasted_iota(jnp.int32, (tr, tr), 1)
        same = (r // CHUNK) == (c // CHUNK)
        for d, (l_ref, dg_ref) in enumerate(((lf_ref, dgf_ref), (lb_ref, dgb_ref))):
            tmat = jnp.where(same & ((r <= c) if d == 0 else (r >= c)), 1.0, 0.0).astype(BF16)
            dg = _dot_sel(tmat, dg_ref[...])
            sig_neg = 1.0 - jnp.exp(GLA_TAU * g_ref[d])
            dlogit = jnp.where(rows >= NULL, dg * (1.0 / GLA_TAU) * sig_neg, 0.0)
            olr_ref[:, d * HP:(d + 1) * HP] = _dot3(dlogit, w_ref[d], "nt").astype(BF16)
            dw = _dot3(l_ref[...], dlogit, "tn")
            dbias = jnp.sum(dlogit, axis=0, keepdims=True)

            @pl.when(i == 0)
            def _():
                dw_ref[d] = dw
                db_ref[d] = dbias

            @pl.when(i != 0)
            def _():
                dw_ref[d] += dw
                db_ref[d] += dbias

    two = pl.BlockSpec((2, tr, 512), lambda i: (0, i, 0))
    row = pl.BlockSpec((tr, 512), lambda i: (i, 0))
    return pl.pallas_call(
        body, name="gla_in_bwd", grid=(lp // tr,),
        in_specs=[row] * 8 + [two, pl.BlockSpec((tr, HP), lambda i: (i, bf)),
                  pl.BlockSpec((tr, HP), lambda i: (i, bb)), pl.BlockSpec((2, HP, 512), lambda i: (0, 0, 0))] +
                 [pl.BlockSpec((tr, P_WIDTH[n]), lambda i: (i, 0)) for n in names],
        out_specs=[pl.BlockSpec((tr, D_INP), lambda i: (i, 0)),
                   pl.BlockSpec((2, HP, 512), lambda i: (0, 0, 0)), pl.BlockSpec((2, 1, 512), lambda i: (0, 0, 0))],
        out_shape=[jax.ShapeDtypeStruct((lp, D_INP), BF16), jax.ShapeDtypeStruct((2, HP, 512), F32),
                   jax.ShapeDtypeStruct((2, 1, 512), F32)],
        compiler_params=_cparams(VMEM_BIG),
    )(*gf, *gb, gates, hin, hin, w2p, *[others[n] for n in names])


def _rope_tables(lp):
    n_tok = lp - NULL - NMETA
    rows = n_tok // GRID_W
    row = np.repeat(np.arange(rows), GRID_W).astype(np.float32)
    col = np.tile(np.arange(GRID_W), rows).astype(np.float32)
    inv = (ROPE_THETA ** (-np.arange(0, 32, 2, dtype=np.float32) / 32)).astype(np.float32)
    ang = np.concatenate([row[:, None] * inv, col[:, None] * inv], axis=-1)
    ang = np.concatenate([np.zeros((NULL + NMETA, 32), np.float32), ang], axis=0)
    cos, sin = np.cos(ang).astype(np.float32), np.sin(ang).astype(np.float32)
    z16 = np.zeros((lp, 16), np.float32)
    z64 = np.zeros((lp, 64), np.float32)
    c = np.concatenate([cos[:, :16], cos[:, :16], cos[:, 16:], cos[:, 16:], z64], axis=1)
    a = np.concatenate([-sin[:, :16], z16, -sin[:, 16:], z16, z64], axis=1)
    b = np.concatenate([z16, sin[:, :16], z16, sin[:, 16:], z64], axis=1)
    return jnp.asarray(c), jnp.asarray(a), jnp.asarray(b)


def _rope(x, c, a, b):
    return x * c + pltpu.roll(x, HP - 16, 1) * a + pltpu.roll(x, 16, 1) * b


def _rope_t(dx, c, a, b):
    return dx * c + pltpu.roll(dx * a, 16, 1) + pltpu.roll(dx * b, HP - 16, 1)


def _attn_prep(hin, gq, gk, tabs):
    lp = hin.shape[0]
    tr = _row_tile(lp)
    qo, ko, vo = P_OFF["qb"] // 1024, P_OFF["kb"] // 256, P_OFF["vb"] // 256

    def body(q_ref, k_ref, v_ref, gq_ref, gk_ref, c_ref, a_ref, b_ref, oq_ref, ok_ref, ov_ref):
        c, a, b = c_ref[...], a_ref[...], b_ref[...]
        for src, g_ref, dst, nh, sc in ((q_ref, gq_ref, oq_ref, ATT_H, Q_SCALE), (k_ref, gk_ref, ok_ref, ATT_KV, 1.0)):
            for h in range(nh):
                sl = slice(h * HP, (h + 1) * HP)
                x = src[:, sl]
                r = lax.rsqrt(jnp.sum(x * x, axis=-1, keepdims=True) * (1.0 / HEAD_DIM) + EPS)
                dst[:, sl] = (_rope(x * r * g_ref[...], c, a, b) * sc).astype(BF16)
        lane = lax.broadcasted_iota(jnp.int32, (1, ATT_KV * HP), 1)
        ov_ref[...] = jnp.where(lane % HP == HEAD_DIM, 1.0, v_ref[...]).astype(BF16)

    tab = pl.BlockSpec((tr, HP), lambda i: (i, 0))
    vec = pl.BlockSpec((1, HP), lambda i: (0, 0))
    return pl.pallas_call(
        body, name="attn_prep", grid=(lp // tr,),
        in_specs=[pl.BlockSpec((tr, 1024), lambda i: (i, qo)), pl.BlockSpec((tr, 256), lambda i: (i, ko)),
                  pl.BlockSpec((tr, 256), lambda i: (i, vo)), vec, vec, tab, tab, tab],
        out_specs=[pl.BlockSpec((tr, 1024), lambda i: (i, 0)), pl.BlockSpec((tr, 256), lambda i: (i, 0)),
                   pl.BlockSpec((tr, 256), lambda i: (i, 0))],
        out_shape=[jax.ShapeDtypeStruct((lp, 1024), BF16), jax.ShapeDtypeStruct((lp, 256), BF16),
                   jax.ShapeDtypeStruct((lp, 256), BF16)],
    )(hin, hin, hin, gq, gk, *tabs)


def _attn_prep_bwd(dqr, dkr, hin, gq, gk, tabs):
    lp = hin.shape[0]
    tr = _row_tile(lp)
    qo, ko = P_OFF["qb"] // 1024, P_OFF["kb"] // 256

    def body(dq_ref, dk_ref, q_ref, k_ref, gq_ref, gk_ref, c_ref, a_ref, b_ref, oq_ref, ok_ref, dgq_ref, dgk_ref):
        i = pl.program_id(0)
        c, a, b = c_ref[...], a_ref[...], b_ref[...]
        for src, dsrc, g_ref, dst, dg_ref, nh, sc in (
                (q_ref, dq_ref, gq_ref, oq_ref, dgq_ref, ATT_H, Q_SCALE),
                (k_ref, dk_ref, gk_ref, ok_ref, dgk_ref, ATT_KV, 1.0)):
            acc = jnp.zeros((1, HP), F32)
            for h in range(nh):
                sl = slice(h * HP, (h + 1) * HP)
                x = src[:, sl]
                r = lax.rsqrt(jnp.sum(x * x, axis=-1, keepdims=True) * (1.0 / HEAD_DIM) + EPS)
                xh = x * r
                dxn = _rope_t(dsrc[:, sl] * sc, c, a, b)
                acc = acc + jnp.sum(dxn * xh, axis=0, keepdims=True)
                dxh = dxn * g_ref[...]
                dx = r * (dxh - xh * (jnp.sum(dxh * xh, axis=-1, keepdims=True) * (1.0 / HEAD_DIM)))
                dst[:, sl] = dx.astype(BF16)

            @pl.when(i == 0)
            def _():
                dg_ref[...] = acc

            @pl.when(i != 0)
            def _():
                dg_ref[...] += acc

    tab = pl.BlockSpec((tr, HP), lambda i: (i, 0))
    vec = pl.BlockSpec((1, HP), lambda i: (0, 0))
    return pl.pallas_call(
        body, name="attn_prep_bwd", grid=(lp // tr,),
        in_specs=[pl.BlockSpec((tr, 1024), lambda i: (i, 0)), pl.BlockSpec((tr, 256), lambda i: (i, 0)),
                  pl.BlockSpec((tr, 1024), lambda i: (i, qo)), pl.BlockSpec((tr, 256), lambda i: (i, ko)),
                  vec, vec, tab, tab, tab],
        out_specs=[pl.BlockSpec((tr, 1024), lambda i: (i, 0)), pl.BlockSpec((tr, 256), lambda i: (i, 0)), vec, vec],
        out_shape=[jax.ShapeDtypeStruct((lp, 1024), BF16), jax.ShapeDtypeStruct((lp, 256), BF16),
                   jax.ShapeDtypeStruct((1, HP), F32), jax.ShapeDtypeStruct((1, HP), F32)],
    )(dqr, dkr, hin, hin, gq, gk, *tabs)


QB = 128
GH = 2
Q_SCALE = HEAD_DIM ** -0.5 * math.log2(math.e)
LN2 = math.log(2.0)


def _stack(ref, g0, n):
    return jnp.concatenate([ref[:, (g0 + g) * HP:(g0 + g + 1) * HP] for g in range(n)], axis=0)


def _attn_fwd(qr, kr, vb):
    lp = qr.shape[0]
    nq = lp // QB

    def body(q_ref, k_ref, v_ref, o_ref, lse_ref):
        qb = pl.program_id(1)
        keys = lax.broadcasted_iota(jnp.int32, (1, lp), 1)
        lane = lax.broadcasted_iota(jnp.int32, (1, HP), 1)
        rows = qb * QB + lax.broadcasted_iota(jnp.int32, (QB, 1), 0)
        for ch in range(ATT_G // GH):
            qs = _stack(q_ref, ch * GH, GH)
            s = _dot(qs, k_ref[...], "nt")
            s = jnp.where(keys >= NULL, s, -1e30)
            m = jnp.max(s, axis=-1, keepdims=True)
            p = jnp.exp2(s - m).astype(BF16)
            o_raw = _dot(p, v_ref[...])
            l = jnp.sum(jnp.where(lane == HEAD_DIM, o_raw, 0.0), axis=-1, keepdims=True)
            o = jnp.where(lane < HEAD_DIM, o_raw / l, 0.0)
            lse = m + jnp.log2(l)
            for g in range(GH):
                sl = slice((ch * GH + g) * HP, (ch * GH + g + 1) * HP)
                o_ref[:, sl] = jnp.where(rows >= NULL, o[g * QB:(g + 1) * QB], 0.0).astype(BF16)
                lse_ref[:, sl] = jnp.broadcast_to(lse[g * QB:(g + 1) * QB], (QB, HP))

    qspec = pl.BlockSpec((QB, ATT_G * HP), lambda kv, qb: (qb, kv))
    kspec = pl.BlockSpec((lp, HP), lambda kv, qb: (0, kv))
    return pl.pallas_call(
        body, name="attn_fwd", grid=(ATT_KV, nq),
        in_specs=[qspec, kspec, kspec], out_specs=[qspec, qspec],
        out_shape=[jax.ShapeDtypeStruct((lp, ATT_H * HP), BF16), jax.ShapeDtypeStruct((lp, ATT_H * HP), F32)],
        compiler_params=_cparams(VMEM_BIG),
    )(qr, kr, vb)


def _attn_bwd(qr, kr, vb, o, lse, do):
    lp = qr.shape[0]
    nq = lp // QB

    def body(q_ref, k_ref, v_ref, o_ref, lse_ref, do_ref, dq_ref, dk_ref, dv_ref):
        qb = pl.program_id(1)

        @pl.when(qb == 0)
        def _():
            dk_ref[...] = jnp.zeros_like(dk_ref)
            dv_ref[...] = jnp.zeros_like(dv_ref)

        keys = lax.broadcasted_iota(jnp.int32, (1, lp), 1)
        k = k_ref[...]
        dk_acc, dv_acc = None, None
        for ch in range(ATT_G // GH):
            g0 = ch * GH
            qs = _stack(q_ref, g0, GH)
            dos = _stack(do_ref, g0, GH)
            os_ = _stack(o_ref, g0, GH).astype(F32)
            lse_s = jnp.concatenate([lse_ref[:, (g0 + g) * HP:(g0 + g) * HP + 1] for g in range(GH)], axis=0)
            delta = jnp.sum(dos * os_, axis=-1, keepdims=True) * LN2
            s = _dot(qs, k, "nt")
            p = jnp.where(keys >= NULL, jnp.exp2(s - lse_s), 0.0)
            dob = dos.astype(BF16)
            dp = _dot((dos * LN2).astype(BF16), v_ref[...], "nt")
            ds = (p * (dp - delta)).astype(BF16)
            dq = _dot(ds, k)
            for g in range(GH):
                dq_ref[:, (g0 + g) * HP:(g0 + g + 1) * HP] = dq[g * QB:(g + 1) * QB]
            dv_c = _dot(p.astype(BF16), dob, "tn")
            dk_c = _dot(ds, qs, "tn")
            dv_acc = dv_c if dv_acc is None else dv_acc + dv_c
            dk_acc = dk_c if dk_acc is None else dk_acc + dk_c
        dv_ref[...] += dv_acc
        dk_ref[...] += dk_acc

    qspec = pl.BlockSpec((QB, ATT_G * HP), lambda kv, qb: (qb, kv))
    kspec = pl.BlockSpec((lp, HP), lambda kv, qb: (0, kv))
    return pl.pallas_call(
        body, name="attn_bwd", grid=(ATT_KV, nq),
        in_specs=[qspec, kspec, kspec, qspec, qspec, qspec], out_specs=[qspec, kspec, kspec],
        out_shape=[jax.ShapeDtypeStruct((lp, ATT_H * HP), F32), jax.ShapeDtypeStruct((lp, ATT_KV * HP), F32),
                   jax.ShapeDtypeStruct((lp, ATT_KV * HP), F32)],
        compiler_params=_cparams(VMEM_BIG),
    )(qr, kr, vb, o, lse, do)


def _mixer_fwd(h, z, wl, l, tabs, next_gain):
    lp = h.shape[0]
    tm = _tm(lp)

    def id_epi(accs, exs, row0):
        return [accs[0]]

    (hin,) = _mm("in_proj", lp, D_INP, [_term(z, wl["win_t"], "nt", 0, (), (l,))], [((lp, D_INP), F32, "mn", 0, ())],
                 id_epi, tm=tm, tn=D_INP // 2, vmem=VMEM_BIG)
    gates, cum = _gla_gates(hin, wl["w2p"][l], wl["b2p"][l])
    o_f, o_b, s_f, s_b = _gla_fwd(hin, cum)
    o2, states = (o_f, o_b), (s_f, s_b)
    a = _gla_out_fwd(o2, hin, wl["gn"][l])
    qr, kr, vb = _attn_prep(hin, wl["gq"][l], wl["gk"][l], tabs)
    b, lse = _attn_fwd(qr, kr, vb)

    def merge_epi(accs, exs, row0):
        pa, pb = accs
        ga, gb, bma, bmb = exs
        y = _sigmoid(ga + bma) * pa + _sigmoid(gb + bmb) * pb
        return [y, pa, pb]

    y, pa, pb = _mm("merge", lp, D, [_term(a, wl["wpa_t"], "nt", 0, (), (l,)), _term(b, wl["wpb_t"], "nt", 1, (), (l,))],
                    [((lp, D), BF16, "mn", 0, ())] * 3, merge_epi,
                    extras=[(hin, "mn", P_OFF["ga"] // D, ()), (hin, "mn", P_OFF["gb"] // D, ()),
                            (wl["bm"], "n", 0, (l, 0)), (wl["bm"], "n", 0, (l, 1))],
                    tm=tm, tn=D, nsub=D // MXU_N, i_outer=True, vmem=VMEM_BIG)

    h2, z2 = _mm("out_proj", lp, D, [_term(y, wl["wout"], "nn", 0, (), (l,))],
                 [((lp, D), F32, "mn", 0, ()), ((lp, D), BF16, "mn", 0, ())], _residual_norm_epi(1.0, True),
                 extras=[(h, "mn", 0, ()), (next_gain, "n", 0, ())], tm=tm, tn=D, i_outer=True, vmem=VMEM_BIG)
    sv = dict(h=h, z=z, hin=hin, gates=gates, cum=cum, o2=o2, states=states, a=a, qr=qr, kr=kr, vb=vb, b=b, lse=lse,
              y=y, pa=pa, pb=pb)
    return h2, z2, sv


def _mixer_bwd(dh, dhb, sv, gain, wl, l, tabs):
    lp = dh.shape[0]
    tm = _tm(lp)
    hin = sv["hin"]

    def merge_bwd_epi(accs, exs, row0):
        dy = accs[0]
        ga, gb, pa, pb, bma, bmb = exs
        sa = _sigmoid(ga + bma)
        sb = _sigmoid(gb + bmb)
        dga = dy * pa.astype(F32) * sa * (1.0 - sa)
        dgb = dy * pb.astype(F32) * sb * (1.0 - sb)
        return [dy * sa, dy * sb, dga, dgb, jnp.sum(dga, axis=0, keepdims=True), jnp.sum(dgb, axis=0, keepdims=True)]

    big = ((lp, D), BF16, "mn", 0, ())
    vec = ((1, D), F32, "nsum", 0, ())
    dpa, dpb, dga, dgb, dbma, dbmb = _mm(
        "merge_bwd", lp, D, [_term(dhb, wl["wout"], "nt", 0, (), (l,))], [big, big, big, big, vec, vec], merge_bwd_epi,
        extras=[(hin, "mn", P_OFF["ga"] // D, ()), (hin, "mn", P_OFF["gb"] // D, ()), (sv["pa"], "mn", 0, ()),
                (sv["pb"], "mn", 0, ()), (wl["bm"], "n", 0, (l, 0)), (wl["bm"], "n", 0, (l, 1))],
        tm=tm, tn=D, nsub=D // MXU_N, vmem=VMEM_BIG)
    d_wout = _dw("dw_out", sv["y"], dhb, D, D)
    d_wpa_t = _dw("dw_pa", dpa, sv["a"], D, 512)
    d_wpb_t = _dw("dw_pb", dpb, sv["b"], D, ATT_H * HP)

    def id_epi(accs, exs, row0):
        return [accs[0]]

    (da,) = _mm("d_a", lp, 512, [_term(dpa, wl["wpa_t"], "nn", 0, (), (l,))], [((lp, 512), F32, "mn", 0, ())], id_epi,
                tm=tm, tn=512, i_outer=True, vmem=VMEM_BIG)
    (db,) = _mm("d_b", lp, ATT_H * HP, [_term(dpb, wl["wpb_t"], "nn", 0, (), (l,))],
                [((lp, ATT_H * HP), F32, "mn", 0, ())], id_epi, tm=tm, tn=512, i_outer=True, vmem=VMEM_BIG)
    d_o, d_ra, d_gn = _gla_out_bwd(da, sv["o2"], hin, wl["gn"][l])
    g_fw, g_bw = _gla_bwd(hin, sv["cum"], sv["states"], d_o)
    dqr, dkr, dvb = _attn_bwd(sv["qr"], sv["kr"], sv["vb"], sv["b"], sv["lse"], db)
    d_qb, d_kb, d_gq, d_gk = _attn_prep_bwd(dqr, dkr, hin, wl["gq"][l], wl["gk"][l], tabs)
    dhin, d_w2p, d_b2p = _gla_in_bwd(g_fw, g_bw, sv["gates"], hin, wl["w2p"][l],
                                     dict(qb=d_qb, ga=dga, gb=dgb, ra=d_ra, kb=d_kb, vb=dvb))
    d_win_t = _dw("dw_in", dhin, sv["z"], D_INP, D)
    dh2, dhb2, dgain = _mm("in_proj_dz", lp, D, [_term(dhin, wl["win_t"], "nn", 0, (), (l,))], _norm_bwd_outs(lp),
                           _norm_bwd_epi, extras=[(sv["h"], "mn", 0, ()), (dh, "mn", 0, ()), (gain, "n", 0, ())],
                           tm=tm, tn=D, nk=2, vmem=VMEM_BIG)
    grads = dict(gain=dgain, wout=d_wout, wpa_t=d_wpa_t, wpb_t=d_wpb_t, win_t=d_win_t, gn=d_gn, w2p=d_w2p, b2p=d_b2p,
                 gq=d_gq, gk=d_gk, bma=dbma, bmb=dbmb)
    return dh2, dhb2, grads


def _mesh_pos():
    x, y, c = lax.axis_index("x"), lax.axis_index("y"), lax.axis_index("c")
    chips = [(1 - x, y), (x, 1 - y), (1 - x, 1 - y)]
    return x, y, c, chips


def _dev_index(x, y, c):
    return 4 * x + 2 * y + c


def _all_gather(name, shards, leads):
    nt = len(shards)

    def blk(ref, lead, idx):
        return ref.at[(slice(None),) * lead + (idx,)]

    def body(*refs):
        xs, outs = refs[:nt], refs[nt:2 * nt]
        send_sems, recv_sems, local_sems = refs[2 * nt:]
        x, y, c, chips = _mesh_pos()
        me, sibling = (x, y, c), (x, y, 1 - c)

        def copy(t, k, block, to, own=False):
            dst = blk(outs[t], leads[t], _dev_index(*block))
            return pltpu.make_async_remote_copy(
                src_ref=xs[t] if own else dst, dst_ref=dst, send_sem=send_sems.at[t, k], recv_sem=recv_sems.at[t, k],
                device_id=to, device_id_type=MESH)

        locals_ = [pltpu.make_async_copy(xs[t], blk(outs[t], leads[t], _dev_index(*me)), local_sems.at[t])
                   for t in range(nt)]
        for cp in locals_:
            cp.start()
        first = []
        for t in range(nt):
            first.append(copy(t, 0, me, sibling, own=True))
            first += [copy(t, 1 + j, me, (*chip, c), own=True) for j, chip in enumerate(chips)]
        for cp in first:
            cp.start()
        passed = []
        for j, chip in enumerate(chips):
            for t in range(nt):
                copy(t, 1 + j, (*chip, c), me).wait_recv()
                fw = copy(t, 4 + j, (*chip, c), sibling)
                fw.start()
                passed.append(fw)
        for t in range(nt):
            copy(t, 0, sibling, me).wait_recv()
        for j, chip in enumerate(chips):
            for t in range(nt):
                copy(t, 4 + j, (*chip, 1 - c), me).wait_recv()
        for cp in first + passed:
            cp.wait_send()
        for cp in locals_:
            cp.wait()

    out_shape = [jax.ShapeDtypeStruct(s.shape[:ld] + (NDEV,) + s.shape[ld:], s.dtype) for s, ld in zip(shards, leads)]
    return pl.pallas_call(
        body, name=name, in_specs=[ANY] * nt, out_specs=[ANY] * nt, out_shape=out_shape,
        scratch_shapes=[pltpu.SemaphoreType.DMA((nt, 7)), pltpu.SemaphoreType.DMA((nt, 7)),
                        pltpu.SemaphoreType.DMA((nt,))],
    )(*shards)


def _exchange_sibling(name, gs):
    nt = len(gs)

    def body(*refs):
        xs, outs = refs[:nt], refs[nt:2 * nt]
        send_sems, recv_sems = refs[2 * nt:]
        x, y, c, _ = _mesh_pos()
        sibling = (x, y, 1 - c)
        copies = []
        for t in range(nt):
            for ch in range(4):
                copies.append(pltpu.make_async_remote_copy(
                    src_ref=xs[t].at[2 * ch + (1 - c)], dst_ref=outs[t].at[ch],
                    send_sem=send_sems.at[t, ch], recv_sem=recv_sems.at[t, ch],
                    device_id=sibling, device_id_type=MESH))
        for cp in copies:
            cp.start()
        for cp in copies:
            cp.wait()

    out_shape = [jax.ShapeDtypeStruct((4,) + g.shape[1:], g.dtype) for g in gs]
    return pl.pallas_call(
        body, name=name, in_specs=[ANY] * nt, out_specs=[ANY] * nt, out_shape=out_shape,
        scratch_shapes=[pltpu.SemaphoreType.DMA((nt, 4)), pltpu.SemaphoreType.DMA((nt, 4))],
    )(*gs)


def _pair_sum(name, gs, recv):
    c = lax.axis_index("c")
    outs = []
    for t, (g, rv) in enumerate(zip(gs, recv)):
        _, r, cols = rv.shape

        def body(c_ref, g_ref, r_ref, o_ref):
            o_ref[...] = (g_ref[...].astype(F32) + r_ref[...].astype(F32)).astype(o_ref.dtype)

        outs.append(pl.pallas_call(
            body, name=f"{name}_{t}",
            grid_spec=pltpu.PrefetchScalarGridSpec(
                num_scalar_prefetch=1, grid=(4,),
                in_specs=[pl.BlockSpec((None, r, cols), lambda ch, cr: (2 * ch + cr[0], 0, 0)),
                          pl.BlockSpec((None, r, cols), lambda ch, cr: (ch, 0, 0))],
                out_specs=pl.BlockSpec((None, r, cols), lambda ch, cr: (ch, 0, 0))),
            out_shape=jax.ShapeDtypeStruct(rv.shape, rv.dtype),
        )(jnp.reshape(c, (1,)).astype(jnp.int32), g, rv))
    return outs


def _exchange_chips(name, ps):
    nt = len(ps)

    def body(*refs):
        xs, outs = refs[:nt], refs[nt:2 * nt]
        send_sems, recv_sems = refs[2 * nt:]
        x, y, c, chips = _mesh_pos()
        copies = []
        for t in range(nt):
            for j, (cx, cy) in enumerate(chips):
                copies.append(pltpu.make_async_remote_copy(
                    src_ref=xs[t].at[2 * cx + cy], dst_ref=outs[t].at[j],
                    send_sem=send_sems.at[t, j], recv_sem=recv_sems.at[t, j],
                    device_id=(cx, cy, c), device_id_type=MESH))
        for cp in copies:
            cp.start()
        for cp in copies:
            cp.wait()

    out_shape = [jax.ShapeDtypeStruct((3,) + p.shape[1:], p.dtype) for p in ps]
    return pl.pallas_call(
        body, name=name, in_specs=[ANY] * nt, out_specs=[ANY] * nt, out_shape=out_shape,
        scratch_shapes=[pltpu.SemaphoreType.DMA((nt, 3)), pltpu.SemaphoreType.DMA((nt, 3))],
    )(*ps)


def _final_sum(name, ps, recv, transposed):
    chip = 2 * lax.axis_index("x") + lax.axis_index("y")
    outs = []
    for t, (p, rv) in enumerate(zip(ps, recv)):
        _, r, cols = rv.shape
        tr_out = transposed[t]
        oshape = (cols, r) if tr_out else (r, cols)

        def body(c_ref, p_ref, r0_ref, r1_ref, r2_ref, o_ref):
            acc = ((p_ref[...].astype(F32) + r0_ref[...].astype(F32)) + r1_ref[...].astype(F32)) + r2_ref[...].astype(F32)
            o_ref[...] = acc.T if tr_out else acc

        outs.append(pl.pallas_call(
            body, name=f"{name}_{t}",
            grid_spec=pltpu.PrefetchScalarGridSpec(
                num_scalar_prefetch=1, grid=(1,),
                in_specs=[pl.BlockSpec((None, r, cols), lambda i, cr: (cr[0], 0, 0))] +
                         [pl.BlockSpec((None, r, cols), lambda i, cr, j=j: (j, 0, 0)) for j in range(3)],
                out_specs=pl.BlockSpec(oshape, lambda i, cr: (0, 0))),
            out_shape=jax.ShapeDtypeStruct(oshape, F32),
        )(jnp.reshape(chip, (1,)).astype(jnp.int32), p, rv, rv, rv))
    return outs


def _reduce_scatter(tag, gs, transposed):
    recv1 = _exchange_sibling(f"rs_sibling_{tag}", gs)
    ps = _pair_sum(f"rs_pair_{tag}", gs, recv1)
    recv2 = _exchange_chips(f"rs_chips_{tag}", ps)
    return _final_sum(f"rs_sum_{tag}", ps, recv2, transposed)


def _sum_gathered(g):
    _, r, cols = g.shape

    def body(g_ref, o_ref):
        acc = g_ref[0]
        for d in range(1, NDEV):
            acc = acc + g_ref[d]
        o_ref[...] = acc

    return pl.pallas_call(body, name="small_sum", out_shape=jax.ShapeDtypeStruct((r, cols), F32))(g)


HBM = pl.BlockSpec(memory_space=pltpu.HBM)
SEM = pl.BlockSpec(memory_space=pltpu.SEMAPHORE)
EFFECT = pltpu.SideEffectType.DATAFLOW_SIDE_EFFECTING
NREL = NDEV - 1


def _related(k):
    x, y, c = lax.axis_index("x"), lax.axis_index("y"), lax.axis_index("c")
    px = 1 - x if k & 4 else x
    py = 1 - y if k & 2 else y
    pc = 1 - c if k & 1 else c
    return (px, py, pc), _dev_index(px, py, pc)


def _in_hbm(a):
    return pltpu.with_memory_space_constraint(a, pltpu.HBM)


ALL_RELS = tuple(range(1, NDEV))
CHIP_RELS = (4, 2, 6)


def _split_copies(xs, lands, send_sems, recv_sems, src_of, dst_of, rels):
    copies = []
    for t in range(len(xs)):
        for q, k in enumerate(rels):
            peer, peer_idx = _related(k)
            copies.append(pltpu.make_async_remote_copy(
                src_ref=src_of(xs[t], t, peer_idx), dst_ref=dst_of(lands[t], t, q, peer_idx),
                send_sem=send_sems.at[t * len(rels) + q], recv_sem=recv_sems.at[t * len(rels) + q],
                device_id=peer, device_id_type=MESH))
    return copies


def _exchange_start(name, xs, lands, src_of, dst_of, after, rels=ALL_RELS):
    nt = len(xs)

    def body(*refs):
        x_refs, land_refs = refs[:nt], refs[nt:2 * nt]
        send_sems, recv_sems = refs[2 * nt + 1], refs[2 * nt + 2]
        token = refs[-1]
        for cp in _split_copies(x_refs, land_refs, send_sems, recv_sems, src_of, dst_of, rels):
            cp.start()
        token[...] = jnp.zeros_like(token)

    res = pl.pallas_call(
        body, name=name,
        out_shape=(pltpu.SemaphoreType.DMA((nt * len(rels),)), pltpu.SemaphoreType.DMA((nt * len(rels),)),
                   *[pltpu.HBM(a.shape, a.dtype) for a in xs], *[pltpu.HBM(a.shape, a.dtype) for a in lands],
                   jax.ShapeDtypeStruct((8, 128), F32)),
        in_specs=[HBM] * (2 * nt) + [ANY],
        out_specs=(SEM, SEM, *[HBM] * (2 * nt), pl.BlockSpec(memory_space=pltpu.VMEM)),
        input_output_aliases={i: 2 + i for i in range(2 * nt)},
        compiler_params=pltpu.CompilerParams(has_side_effects=EFFECT),
    )(*[_in_hbm(a) for a in xs], *[_in_hbm(a) for a in lands], after)
    return res[0], res[1], res[2:2 + nt], res[2 + nt:2 + 2 * nt], res[-1]


def _exchange_wait(name, send_sems, recv_sems, xs, lands, src_of, dst_of, after, rels=ALL_RELS):
    nt = len(xs)

    def body(*refs):
        x_refs, land_refs = refs[:nt], refs[nt:2 * nt]
        send_sems, recv_sems = refs[2 * nt], refs[2 * nt + 1]
        for cp in _split_copies(x_refs, land_refs, send_sems, recv_sems, src_of, dst_of, rels):
            cp.wait_send()
            cp.wait_recv()

    res = pl.pallas_call(
        body, name=name,
        out_shape=(*[pltpu.HBM(a.shape, a.dtype) for a in xs], *[pltpu.HBM(a.shape, a.dtype) for a in lands]),
        in_specs=[HBM] * (2 * nt) + [SEM, SEM, ANY], out_specs=tuple([HBM] * (2 * nt)),
        input_output_aliases={i: i for i in range(2 * nt)},
        compiler_params=pltpu.CompilerParams(has_side_effects=EFFECT),
    )(*xs, *lands, send_sems, recv_sems, after)
    return res[:nt], res[nt:]


def _gather_start(name, shards, leads, after):
    def src_of(x_ref, t, peer_idx):
        return x_ref

    def dst_of(land_ref, t, k, peer_idx):
        me = _dev_index(lax.axis_index("x"), lax.axis_index("y"), lax.axis_index("c"))
        return land_ref.at[(slice(None),) * leads[t] + (me,)]

    lands = [lax.empty(s.shape[:ld] + (NDEV,) + s.shape[ld:], s.dtype) for s, ld in zip(shards, leads)]
    return _exchange_start(name, shards, lands, src_of, dst_of, after)


def _gather_wait(name, started, leads, after):
    send_sems, recv_sems, shards, lands, _ = started

    def src_of(x_ref, t, peer_idx):
        return x_ref

    def dst_of(land_ref, t, k, peer_idx):
        return land_ref.at[(slice(None),) * leads[t] + (peer_idx,)]

    shards, lands = _exchange_wait(name, send_sems, recv_sems, shards, lands, src_of, dst_of, after)
    me = _dev_index(lax.axis_index("x"), lax.axis_index("y"), lax.axis_index("c"))
    return [lax.dynamic_update_index_in_dim(g, s, me, ld) for g, s, ld in zip(lands, shards, leads)]


def _scatter_src(x_ref, t, peer_idx):
    return x_ref.at[peer_idx]


def _scatter_dst(land_ref, t, q, peer_idx):
    return land_ref.at[q]


def _chips_src(x_ref, t, peer_idx):
    return x_ref.at[peer_idx // 2]


def _chips_start(name, ps, after):
    lands = [lax.empty((len(CHIP_RELS),) + p.shape[1:], p.dtype) for p in ps]
    return _exchange_start(name, ps, lands, _chips_src, _scatter_dst, after, CHIP_RELS)


def _chips_wait(name, started, after):
    send_sems, recv_sems, ps, lands, _ = started
    return _exchange_wait(name, send_sems, recv_sems, ps, lands, _chips_src, _scatter_dst, after, CHIP_RELS)


def _scatter_start(name, gs, after):
    lands = [lax.empty((NREL,) + g.shape[1:], g.dtype) for g in gs]
    return _exchange_start(name, gs, lands, _scatter_src, _scatter_dst, after)


def _scatter_wait(name, started, after, transposed):
    send_sems, recv_sems, gs, lands, _ = started
    gs, lands = _exchange_wait(name, send_sems, recv_sems, gs, lands, _scatter_src, _scatter_dst, after)
    me = _dev_index(lax.axis_index("x"), lax.axis_index("y"), lax.axis_index("c"))
    outs = []
    for t, (g, rv) in enumerate(zip(gs, lands)):
        _, r, cols = rv.shape
        tr_out = transposed[t]
        oshape = (cols, r) if tr_out else (r, cols)

        def body(c_ref, own_ref, rv_ref, o_ref):
            acc = own_ref[...].astype(F32)
            for k in range(NREL):
                acc = acc + rv_ref[k].astype(F32)
            o_ref[...] = acc.T if tr_out else acc

        outs.append(pl.pallas_call(
            body, name=f"{name}_sum_{t}",
            grid_spec=pltpu.PrefetchScalarGridSpec(
                num_scalar_prefetch=1, grid=(1,),
                in_specs=[pl.BlockSpec((None, r, cols), lambda i, cr: (cr[0], 0, 0)),
                          pl.BlockSpec((NREL, r, cols), lambda i, cr: (0, 0, 0))],
                out_specs=pl.BlockSpec(oshape, lambda i, cr: (0, 0))),
            out_shape=jax.ShapeDtypeStruct(oshape, F32), compiler_params=_cparams(VMEM_BIG),
        )(jnp.reshape(me, (1,)).astype(jnp.int32), g, rv))
    return outs


def _adamw(w, g, m, v):
    shape = w.shape
    cols = shape[-1]
    rows = math.prod(shape[:-1]) if len(shape) > 1 else 1
    w2, g2, m2, v2 = (jnp.reshape(t, (rows, cols)) for t in (w, g, m, v))
    tr = _pick(rows, (1024, 512, 256, 128)) if rows * cols > 65536 else rows
    c1 = 1.0 / (1.0 - ADAM_B1 ** ADAM_STEP)
    c2 = 1.0 / (1.0 - ADAM_B2 ** ADAM_STEP)

    def body(w_ref, g_ref, m_ref, v_ref, d_ref, nm_ref, nv_ref):
        gv = g_ref[...]
        nm = ADAM_B1 * m_ref[...] + (1.0 - ADAM_B1) * gv
        nv = ADAM_B2 * v_ref[...] + (1.0 - ADAM_B2) * (gv * gv)
        d_ref[...] = -ADAM_LR * ((nm * c1) / (jnp.sqrt(nv * c2) + ADAM_EPS) + ADAM_WD * w_ref[...])
        nm_ref[...] = nm
        nv_ref[...] = nv

    spec = pl.BlockSpec((tr, cols), lambda i: (i, 0))
    osh = jax.ShapeDtypeStruct((rows, cols), F32)
    d, nm, nv = pl.pallas_call(
        body, name="adamw", grid=(rows // tr,), in_specs=[spec] * 4, out_specs=[spec] * 3, out_shape=[osh] * 3,
        compiler_params=_cparams(VMEM_BIG),
    )(w2, g2, m2, v2)
    return jnp.reshape(d, shape), jnp.reshape(nm, shape), jnp.reshape(nv, shape)


def _pad_heads(w, name):
    if name not in P_HEADS:
        return w
    nh, real = P_HEADS[name]
    w = jnp.reshape(w, w.shape[:-2] + (nh, real, w.shape[-1]))
    w = jnp.pad(w, [(0, 0)] * (w.ndim - 2) + [(0, HP - real), (0, 0)])
    return jnp.reshape(w, w.shape[:-3] + (nh * HP, w.shape[-1]))


def _unpad_heads(w, name):
    if name not in P_HEADS:
        return w
    nh, real = P_HEADS[name]
    w = jnp.reshape(w, w.shape[:-2] + (nh, HP, w.shape[-1]))[..., :real, :]
    return jnp.reshape(w, w.shape[:-3] + (nh * real, w.shape[-1]))


def _win_pad(win_t):
    segs, o = {}, 0
    for n, s in zip(IN_NAMES, IN_SIZES):
        segs[n] = win_t[..., o:o + s, :]
        o += s
    return jnp.concatenate([_pad_heads(segs[n], n) for n in P_ORDER], axis=-2)


def _win_unpad(win_p):
    segs = {n: _unpad_heads(win_p[..., P_OFF[n]:P_OFF[n] + P_WIDTH[n], :], n) for n in P_ORDER}
    return jnp.concatenate([segs[n] for n in IN_NAMES], axis=-2)


def _t(w):
    return jnp.swapaxes(w, -1, -2)


def _layer_weights(g_g, g_u, g_d, g_in, g_pa, g_pb, g_out, gains, w2, b2, bm, gn, gq, gk):
    w2p = jnp.pad(jnp.reshape(w2, (2, GLA_RANK, GLA_H, GLA_DK)), ((0, 0), (0, HP - GLA_RANK), (0, 0), (0, HP - GLA_DK)))
    b2p = jnp.pad(jnp.reshape(b2, (2, 1, GLA_H, GLA_DK)), ((0, 0), (0, 0), (0, 0), (0, HP - GLA_DK)))
    wpb_t = jnp.pad(jnp.reshape(g_pb, (D, ATT_H, HEAD_DIM)), ((0, 0), (0, 0), (0, HP - HEAD_DIM)))
    return dict(
        gains=jnp.reshape(gains, (1, 3, 1, D)),
        wg_t=jnp.reshape(g_g, (1, 2, DFF, D)), wu_t=jnp.reshape(g_u, (1, 2, DFF, D)), wd=jnp.reshape(g_d, (1, 2, DFF, D)),
        win_t=_win_pad(jnp.reshape(g_in, (1, D_IN, D))), wpa_t=jnp.reshape(g_pa, (1, D, 512)),
        wpb_t=jnp.reshape(wpb_t, (1, D, ATT_H * HP)), wout=jnp.reshape(g_out, (1, D, D)),
        w2p=jnp.reshape(w2p, (1, 2, HP, GLA_H * HP)), b2p=jnp.reshape(b2p, (1, 2, 1, GLA_H * HP)),
        bm=jnp.reshape(bm, (1, 2, 1, D)), gn=jnp.reshape(gn, (1, 1, GLA_H * HP)),
        gq=jnp.pad(jnp.reshape(gq, (1, 1, HEAD_DIM)), ((0, 0), (0, 0), (0, HP - HEAD_DIM))),
        gk=jnp.pad(jnp.reshape(gk, (1, 1, HEAD_DIM)), ((0, 0), (0, 0), (0, HP - HEAD_DIM))))


def _layer_fwd(h, z, w, tabs, next_gain):
    h, z, s0 = _ffn_fwd(h, z, w["wg_t"], w["wu_t"], w["wd"], (0, 0), w["gains"][0, 1])
    h, z, s1 = _mixer_fwd(h, z, w, 0, tabs, w["gains"][0, 2])
    h, z, s2 = _ffn_fwd(h, z, w["wg_t"], w["wu_t"], w["wd"], (0, 1), next_gain)
    return h, z, (s0, s1, s2)


def _layer_bwd_upper(dh, dhb, saved, w, tabs):
    _, s1, s2 = saved
    dh, dhb, dg2, dwg1, dwu1, dwd1 = _ffn_bwd(dh, dhb, s2, w["gains"][0, 2], w["wg_t"], w["wu_t"], w["wd"], (0, 1))
    dh, dhb, gm = _mixer_bwd(dh, dhb, s1, w["gains"][0, 1], w, 0, tabs)
    gm.update(gain2=dg2, wg1=dwg1, wu1=dwu1, wd1=dwd1)
    return dh, dhb, gm


def _layer_bwd_lower(dh, dhb, saved, w, gm):
    dh, dhb, dg0, dwg0, dwu0, dwd0 = _ffn_bwd(dh, dhb, saved[0], w["gains"][0, 0], w["wg_t"], w["wu_t"], w["wd"], (0, 0))
    gm.update(gain0=dg0, wg0=dwg0, wu0=dwu0, wd0=dwd0)
    return dh, dhb, gm


def _layer_bwd(dh, dhb, saved, w, tabs):
    dh, dhb, gm = _layer_bwd_upper(dh, dhb, saved, w, tabs)
    return _layer_bwd_lower(dh, dhb, saved, w, gm)


def _blocks(ts):
    return [jnp.reshape(t, (NDEV, t.shape[0] // NDEV, t.shape[1])) for t in ts]


def _upper_grads(g):
    d_in = _win_unpad(g["win_t"])
    d_pb = jnp.reshape(jnp.reshape(g["wpb_t"], (D, ATT_H, HP))[:, :, :HEAD_DIM], (D, 512))
    return _blocks([g["wg1"], g["wu1"], g["wd1"], d_in, g["wpa_t"], d_pb, g["wout"]])


def _lower_grads(g):
    return _blocks([g["wg0"], g["wu0"], g["wd0"]])


def _big_grads(g):
    return _lower_grads(g) + _upper_grads(g)


def kernel(x, meta_tokens, norm_gains, ffn_w_gate, ffn_w_up, ffn_w_down, w_in, gla_w2, gla_b2, gla_gn, q_norm, k_norm, w_pa, w_pb, b_merge, w_out, final_norm, loss_target, m_meta_tokens, m_norm_gains, m_ffn_w_gate, m_ffn_w_up, m_ffn_w_down, m_w_in, m_gla_w2, m_gla_b2, m_gla_gn, m_q_norm, m_k_norm, m_w_pa, m_w_pb, m_b_merge, m_w_out, m_final_norm, v_meta_tokens, v_norm_gains, v_ffn_w_gate, v_ffn_w_up, v_ffn_w_down, v_w_in, v_gla_w2, v_gla_b2, v_gla_gn, v_q_norm, v_k_norm, v_w_pa, v_w_pb, v_b_merge, v_w_out, v_final_norm):
    dev = _dev_index(lax.axis_index("x"), lax.axis_index("y"), lax.axis_index("c"))
    sh_g = _t(ffn_w_gate).astype(BF16)
    sh_u = _t(ffn_w_up).astype(BF16)
    sh_d = ffn_w_down.astype(BF16)
    sh_in = _t(w_in).astype(BF16)
    sh_pa = _t(w_pa).astype(BF16)
    sh_pb = _t(w_pb).astype(BF16)
    sh_out = w_out.astype(BF16)
    small = jnp.concatenate([jnp.reshape(t, (-1, 128)) for t in
                             (meta_tokens, norm_gains, gla_w2, gla_b2, b_merge)], axis=0)
    small = jnp.pad(small, ((0, 2), (0, 0)))
    def shards(l):
        return [sh_g[l], sh_u[l], sh_d[l], sh_in[l], sh_pa[l], sh_pb[l], sh_out[l]]

    w_leads = [1, 1, 1, 0, 0, 0, 0]
    *g0, g_small = _all_gather("gather_layer0", shards(0) + [small], w_leads + [0])
    started = {1: _gather_start("gather_start_1", shards(1), w_leads, g_small)}
    meta_full = jnp.reshape(jnp.transpose(g_small[:, 0:16], (1, 0, 2)), (NMETA, D)) + started[1][4][0, 0]
    gains_full = jnp.reshape(jnp.transpose(jnp.reshape(g_small[:, 16:28], (NDEV, DEPTH, 3, 128)), (1, 2, 0, 3)), (DEPTH, 3, D))
    w2_full = jnp.reshape(jnp.transpose(jnp.reshape(g_small[:, 28:60], (NDEV, DEPTH, 2, GLA_RANK, 32)), (1, 2, 3, 0, 4)),
                          (DEPTH, 2, GLA_RANK, 256))
    b2_full = jnp.reshape(jnp.transpose(jnp.reshape(g_small[:, 60:62], (NDEV, DEPTH, 2, 32)), (1, 2, 0, 3)), (DEPTH, 2, 256))
    bm_full = jnp.reshape(jnp.transpose(jnp.reshape(g_small[:, 62:70], (NDEV, DEPTH, 2, 128)), (1, 2, 0, 3)), (DEPTH, 2, D))

    def layer_weights(l, gathered, gains_l):
        return _layer_weights(*gathered, gains_l, w2_full[l], b2_full[l], bm_full[l], gla_gn[l], q_norm[l], k_norm[l])

    xl = x[0]
    lp = xl.shape[0] + NULL + NMETA
    tabs = _rope_tables(lp)
    h = jnp.concatenate([jnp.zeros((NULL, D), F32), meta_full, xl], axis=0)
    weights, saved = [], []
    z = _rmsnorm_fwd(h, jnp.reshape(gains_full[0, 0], (1, D)))
    for l in range(DEPTH):
        tok = jnp.zeros((), F32)
        if 1 <= l < DEPTH - 1:
            started[l + 1] = _gather_start(f"gather_start_{l + 1}", shards(l + 1), w_leads, h)
            tok = started[l + 1][4][0, 0]
        gathered = g0 if l == 0 else _gather_wait(f"gather_wait_{l}", started[l], w_leads, h)
        weights.append(layer_weights(l, gathered, gains_full[l] + tok))
        next_gain = jnp.reshape(gains_full[l + 1, 0], (1, D)) if l + 1 < DEPTH else None
        h, z, sv = _layer_fwd(h, z, weights[l], tabs, next_gain)
        saved.append(sv)
    loss, dh, dhb, d_final = _loss_head(h, loss_target[0], jnp.reshape(final_norm, (1, D)))
    loss = lax.psum(loss[0, 0], ("x", "y", "c"))

    grads, scattering = [None] * DEPTH, {}
    tok = jnp.zeros((), F32)
    for l in reversed(range(DEPTH)):
        w = dict(weights[l], gains=weights[l]["gains"] + tok)
        if l > 0:
            dh, dhb, grads[l] = _layer_bwd(dh, dhb, saved[l], w, tabs)
            scattering[l] = _scatter_start(f"scatter_start_{l}", _big_grads(grads[l]), dhb)
            tok = scattering[l][4][0, 0]
        else:
            dh, dhb, gm = _layer_bwd_upper(dh, dhb, saved[l], w, tabs)
            ups = _upper_grads(gm)
            pair = _pair_sum("rs_pair_up", ups, _exchange_sibling("rs_sibling_up", ups))
            scattering[l] = _chips_start(f"scatter_start_{l}", pair, dhb)
            dhb = dhb + scattering[l][4][0, 0].astype(BF16)
            dh, dhb, grads[l] = _layer_bwd_lower(dh, dhb, saved[l], w, gm)
    grad_x = dh[NULL + NMETA:][None]
    t_lower, t_upper = [True, True, False], [True, True, False, True, True, True, False]
    red = [None] * DEPTH
    for l in reversed(range(1, DEPTH)):
        red[l] = _scatter_wait(f"scatter_wait_{l}", scattering[l], dhb, t_lower + t_upper)
    red_lower = _reduce_scatter("0", _lower_grads(grads[0]), t_lower)
    pair, recv = _chips_wait("scatter_wait_0", scattering[0], red_lower[0])
    red[0] = red_lower + _final_sum("rs_sum_up", pair, recv, t_upper)
    g_gate = jnp.stack([jnp.stack([red[l][0], red[l][3]]) for l in range(DEPTH)])
    g_up = jnp.stack([jnp.stack([red[l][1], red[l][4]]) for l in range(DEPTH)])
    g_down = jnp.stack([jnp.stack([red[l][2], red[l][5]]) for l in range(DEPTH)])
    g_win = jnp.stack([red[l][6] for l in range(DEPTH)])
    g_wpa = jnp.stack([red[l][7] for l in range(DEPTH)])
    g_wpb = jnp.stack([red[l][8] for l in range(DEPTH)])
    g_wout = jnp.stack([red[l][9] for l in range(DEPTH)])

    d_meta = dh[NULL:NULL + NMETA]
    d_gains = jnp.stack([jnp.concatenate([grads[l]["gain0"], grads[l]["gain"], grads[l]["gain2"]], axis=0)
                         for l in range(DEPTH)])
    d_w2 = jnp.stack([jnp.reshape(jnp.reshape(grads[l]["w2p"], (2, HP, GLA_H, HP))[:, :GLA_RANK, :, :GLA_DK],
                                  (2, GLA_RANK, 256)) for l in range(DEPTH)])
    d_b2 = jnp.stack([jnp.reshape(jnp.reshape(grads[l]["b2p"], (2, GLA_H, HP))[:, :, :GLA_DK], (2, 256))
                      for l in range(DEPTH)])
    d_gn = jnp.stack([grads[l]["gn"][0] for l in range(DEPTH)])
    d_gq = jnp.stack([grads[l]["gq"][0, :HEAD_DIM] for l in range(DEPTH)])
    d_gk = jnp.stack([grads[l]["gk"][0, :HEAD_DIM] for l in range(DEPTH)])
    d_bm = jnp.stack([jnp.concatenate([grads[l]["bma"], grads[l]["bmb"]], axis=0) for l in range(DEPTH)])
    parts = [d_meta, d_gains, d_w2, d_b2, d_gn, d_gq, d_gk, d_bm, d_final[0]]
    sizes = [p.size for p in parts]
    flat = jnp.concatenate([jnp.reshape(p, (-1,)) for p in parts])
    flat = jnp.reshape(flat, (-1, 128))
    nrow = flat.shape[0]
    flat = jnp.pad(flat, ((0, (-nrow) % 8), (0, 0)))
    (g_flat,) = _all_gather("gather_small_grads", [flat], [0])
    tot = jnp.reshape(_sum_gathered(g_flat), (-1,))
    full, o = [], 0
    for p, s in zip(parts, sizes):
        full.append(jnp.reshape(tot[o:o + s], p.shape))
        o += s
    f_meta, f_gains, f_w2, f_b2, f_gn, f_gq, f_gk, f_bm, f_final = full

    def mine(t, width):
        return lax.dynamic_slice_in_dim(t, dev * width, width, axis=t.ndim - 1)

    g_small = dict(meta_tokens=mine(f_meta, 128), norm_gains=mine(f_gains, 128), gla_w2=mine(f_w2, 32),
                   gla_b2=mine(f_b2, 32), gla_gn=f_gn, q_norm=f_gq, k_norm=f_gk, b_merge=mine(f_bm, 128),
                   final_norm=f_final)
    gr = dict(g_small, ffn_w_gate=g_gate, ffn_w_up=g_up, ffn_w_down=g_down, w_in=g_win, w_pa=g_wpa, w_pb=g_wpb,
              w_out=g_wout)
    ws = dict(meta_tokens=meta_tokens, norm_gains=norm_gains, ffn_w_gate=ffn_w_gate, ffn_w_up=ffn_w_up,
              ffn_w_down=ffn_w_down, w_in=w_in, gla_w2=gla_w2, gla_b2=gla_b2, gla_gn=gla_gn, q_norm=q_norm,
              k_norm=k_norm, w_pa=w_pa, w_pb=w_pb, b_merge=b_merge, w_out=w_out, final_norm=final_norm)
    ms = dict(meta_tokens=m_meta_tokens, norm_gains=m_norm_gains, ffn_w_gate=m_ffn_w_gate, ffn_w_up=m_ffn_w_up,
              ffn_w_down=m_ffn_w_down, w_in=m_w_in, gla_w2=m_gla_w2, gla_b2=m_gla_b2, gla_gn=m_gla_gn, q_norm=m_q_norm,
              k_norm=m_k_norm, w_pa=m_w_pa, w_pb=m_w_pb, b_merge=m_b_merge, w_out=m_w_out, final_norm=m_final_norm)
    vs = dict(meta_tokens=v_meta_tokens, norm_gains=v_norm_gains, ffn_w_gate=v_ffn_w_gate, ffn_w_up=v_ffn_w_up,
              ffn_w_down=v_ffn_w_down, w_in=v_w_in, gla_w2=v_gla_w2, gla_b2=v_gla_b2, gla_gn=v_gla_gn, q_norm=v_q_norm,
              k_norm=v_k_norm, w_pa=v_w_pa, w_pb=v_w_pb, b_merge=v_b_merge, w_out=v_w_out, final_norm=v_final_norm)
    names = ["meta_tokens", "norm_gains", "ffn_w_gate", "ffn_w_up", "ffn_w_down", "w_in", "gla_w2", "gla_b2", "gla_gn",
             "q_norm", "k_norm", "w_pa", "w_pb", "b_merge", "w_out", "final_norm"]
    deltas, new_m, new_v = [], [], []
    for n in names:
        dlt, nm, nv = _adamw(ws[n], gr[n], ms[n], vs[n])
        deltas.append(dlt)
        new_m.append(nm)
        new_v.append(nv)
    return (loss, grad_x, *[gr[n] for n in names], *deltas, *new_m, *new_v)
```

```python
import functools
import math

import jax
import jax.numpy as jnp
import numpy as np
from jax import lax
from jax.experimental import pallas as pl
from jax.experimental.pallas import tpu as pltpu

F32 = jnp.float32
BF16 = jnp.bfloat16
MESH = pl.DeviceIdType.MESH
ANY = pl.BlockSpec(memory_space=pl.ANY)

NDEV = 8
D = 1024
DFF = 2816
DEPTH = 4
NMETA = 16
NULL = 112
GRID_W = 64
EPS = 1e-6
HP = 128
GLA_H = 4
GLA_DK = 64
GLA_RANK = 16
GLA_TAU = 16.0
CHUNK = 64
ATT_H = 8
ATT_KV = 2
ATT_G = ATT_H // ATT_KV
HEAD_DIM = 64
ROPE_THETA = 10000.0

IN_SIZES = (256, 256, 512, 512, 16, 16, 512, 128, 128, 1024, 1024)
IN_NAMES = ("qa", "ka", "va", "ra", "lrf", "lrb", "qb", "kb", "vb", "ga", "gb")
D_IN = sum(IN_SIZES)
P_ORDER = ("qb", "ga", "gb", "qa", "ka", "va", "ra", "kb", "vb", "lrf", "lrb")
P_WIDTH = dict(qb=1024, ga=1024, gb=1024, qa=512, ka=512, va=512, ra=512, kb=256, vb=256, lrf=128, lrb=128)
P_OFF = {}
_o = 0
for _n in P_ORDER:
    P_OFF[_n] = _o
    _o += P_WIDTH[_n]
D_INP = _o
P_HEADS = dict(qa=(4, 64), ka=(4, 64), qb=(8, 64), kb=(2, 64), vb=(2, 64), lrf=(1, 16), lrb=(1, 16))

ADAM_LR = 0.001
ADAM_B1 = 0.9
ADAM_B2 = 0.999
ADAM_EPS = 1e-08
ADAM_WD = 0.01
ADAM_STEP = 10

VMEM_BIG = 56 * 1024 * 1024
MXU_N = 256


def _cparams(vmem=None):
    return pltpu.CompilerParams(vmem_limit_bytes=vmem) if vmem else pltpu.CompilerParams()


def _pick(n, prefs):
    for p in prefs:
        if n % p == 0:
            return p
    return n


def _tm(lp):
    return _pick(lp, (528, 512, 256, 128))


_DN = {"nn": (((1,), (0,)), ((), ())), "nt": (((1,), (1,)), ((), ())), "tn": (((0,), (0,)), ((), ()))}


def _dot(a, b, mode="nn", precision=None):
    return lax.dot_general(a, b, _DN[mode], preferred_element_type=F32, precision=precision)


def _split(x):
    hi = x.astype(BF16)
    return hi, (x - hi.astype(F32)).astype(BF16)


def _dot_sel(t, x, mode="nn"):
    hi, lo = _split(x)
    return _dot(t, hi, mode) + _dot(t, lo, mode)


def _dot3(a, b, mode="nn"):
    ah, al = _split(a)
    bh, bl = _split(b)
    return _dot(ah, bh, mode) + (_dot(ah, bl, mode) + _dot(al, bh, mode))


def _sigmoid(x):
    return 0.5 * jnp.tanh(0.5 * x) + 0.5


def _mm(name, m, n, terms, outs, epilogue, extras=(), *, tm, tn, nk=1, nsub=1, i_outer=False, vmem=None):
    gm, gn = m // tm, n // tn
    assert gm * tm == m and gn * tn == n, (name, m, n, tm, tn)
    n_acc = 1 + max(t[3] for t in terms)

    def gmap(f):
        if i_outer:
            return lambda i, j, kk: f(i, j, kk)
        return lambda j, i, kk: f(i, j, kk)

    in_specs, args = [], []
    for a, b, mode, _, pa, pb in terms:
        kdim = a.shape[-2] if mode == "tn" else a.shape[-1]
        tk = kdim // nk
        assert tk * nk == kdim
        na, nb = (None,) * len(pa), (None,) * len(pb)
        if mode == "tn":
            in_specs.append(pl.BlockSpec(na + (tk, tm), gmap(lambda i, j, kk, pa=pa: pa + (kk, i))))
        else:
            in_specs.append(pl.BlockSpec(na + (tm, tk), gmap(lambda i, j, kk, pa=pa: pa + (i, kk))))
        if mode == "nt":
            in_specs.append(pl.BlockSpec(nb + (tn, tk), gmap(lambda i, j, kk, pb=pb: pb + (j, kk))))
        else:
            in_specs.append(pl.BlockSpec(nb + (tk, tn), gmap(lambda i, j, kk, pb=pb: pb + (kk, j))))
        args += [a, b]
    for arr, kind, off, pe in extras:
        ne = (None,) * len(pe)
        if kind == "mn":
            in_specs.append(pl.BlockSpec(ne + (tm, tn), gmap(lambda i, j, kk, off=off, pe=pe: pe + (i, j + off))))
        else:
            in_specs.append(pl.BlockSpec(ne + (1, tn), gmap(lambda i, j, kk, off=off, pe=pe: pe + (0, j + off))))
        args.append(arr)
    out_shape, out_specs = [], []
    for shape, dtype, kind, off, po in outs:
        no = (None,) * len(po)
        out_shape.append(jax.ShapeDtypeStruct(shape, dtype))
        if kind == "mn":
            out_specs.append(pl.BlockSpec(no + (tm, tn), gmap(lambda i, j, kk, off=off, po=po: po + (i, j + off))))
        else:
            assert not i_outer
            out_specs.append(pl.BlockSpec(no + (1, tn), gmap(lambda i, j, kk, off=off, po=po: po + (0, j + off))))
    n_t, n_e, n_o = len(terms), len(extras), len(outs)
    i_axis = 0 if i_outer else 1

    def body(*refs):
        ins = refs[: 2 * n_t]
        exs = refs[2 * n_t: 2 * n_t + n_e]
        ors = refs[2 * n_t + n_e: 2 * n_t + n_e + n_o]
        accs = refs[2 * n_t + n_e + n_o:]
        i = pl.program_id(i_axis)
        kk = pl.program_id(2)

        def partials(cs):
            part = [None] * n_acc
            for t, (_, _, mode, ai, _, _) in enumerate(terms):
                b_ref = ins[2 * t + 1]
                b_val = b_ref[cs, :] if mode == "nt" else b_ref[:, cs]
                p = _dot(ins[2 * t][...], b_val, mode)
                part[ai] = p if part[ai] is None else part[ai] + p
            return part

        def finish(vals, cs):
            res = epilogue(vals, [e[:, cs] for e in exs], i * tm)
            for (_, dtype, kind, _, _), o_ref, v in zip(outs, ors, res):
                if kind == "mn":
                    o_ref[:, cs] = v.astype(dtype)
                else:
                    @pl.when(i == 0)
                    def _():
                        o_ref[:, cs] = v.astype(dtype)

                    @pl.when(i != 0)
                    def _():
                        o_ref[:, cs] += v.astype(dtype)

        if nk == 1:
            w = tn // nsub
            for s in range(nsub):
                cs = slice(s * w, (s + 1) * w)
                finish(partials(cs), cs)
        else:
            part = partials(slice(None))
            @pl.when(kk == 0)
            def _():
                for a_ref, p in zip(accs, part):
                    a_ref[...] = p

            @pl.when(kk != 0)
            def _():
                for a_ref, p in zip(accs, part):
                    a_ref[...] += p

            @pl.when(kk == nk - 1)
            def _():
                finish([a_ref[...] for a_ref in accs], slice(None))

    scratch = [pltpu.VMEM((tm, tn), F32) for _ in range(n_acc)] if nk > 1 else []
    grid = (gm, gn, nk) if i_outer else (gn, gm, nk)
    res = pl.pallas_call(
        body, name=name, grid=grid, in_specs=in_specs, out_specs=out_specs, out_shape=out_shape,
        scratch_shapes=scratch, compiler_params=_cparams(vmem),
    )(*args)
    return res


def _term(a, b, mode, acc=0, pa=(), pb=()):
    return (a, b, mode, acc, tuple(pa), tuple(pb))


def _row_tile(lp):
    return _pick(lp, (384, 256, 128))


def _rmsnorm_fwd(h, gain):
    lp = h.shape[0]
    tr = _row_tile(lp)

    def body(h_ref, g_ref, z_ref):
        x = h_ref[...]
        r = lax.rsqrt(jnp.mean(x * x, axis=-1, keepdims=True) + EPS)
        z_ref[...] = (x * r * g_ref[...]).astype(BF16)

    return pl.pallas_call(
        body, name="rmsnorm_fwd", grid=(lp // tr,),
        in_specs=[pl.BlockSpec((tr, D), lambda i: (i, 0)), pl.BlockSpec((1, D), lambda i: (0, 0))],
        out_specs=pl.BlockSpec((tr, D), lambda i: (i, 0)),
        out_shape=jax.ShapeDtypeStruct((lp, D), BF16),
    )(h, gain)


def _loss_head(h, target, gain):
    lp = h.shape[0]
    tr = 128

    def body(h_ref, t_ref, g_ref, loss_ref, dh_ref, dhb_ref, dg_ref):
        i = pl.program_id(0)

        @pl.when(i == 0)
        def _():
            loss_ref[...] = jnp.zeros_like(loss_ref)
            dg_ref[...] = jnp.zeros_like(dg_ref)
            dh_ref[...] = jnp.zeros_like(dh_ref)
            dhb_ref[...] = jnp.zeros_like(dhb_ref)

        @pl.when(i != 0)
        def _():
            x = h_ref[...]
            g = g_ref[...]
            r = lax.rsqrt(jnp.mean(x * x, axis=-1, keepdims=True) + EPS)
            xh = x * r
            y = xh * g
            err = y - t_ref[...]
            loss_ref[...] += 0.5 * jnp.sum(jnp.sum(err * err, axis=-1, keepdims=True), axis=0, keepdims=True) / D
            dy = err * (1.0 / D)
            dg_ref[...] += jnp.sum(dy * xh, axis=0, keepdims=True)
            dxh = dy * g
            dx = r * (dxh - xh * jnp.mean(dxh * xh, axis=-1, keepdims=True))
            dh_ref[...] = dx
            dhb_ref[...] = dx.astype(BF16)

    row = pl.BlockSpec((tr, D), lambda i: (i, 0))
    vec = pl.BlockSpec((1, D), lambda i: (0, 0))
    return pl.pallas_call(
        body, name="loss_head", grid=(lp // tr,),
        in_specs=[row, pl.BlockSpec((tr, D), lambda i: (jnp.maximum(i - 1, 0), 0)), vec],
        out_specs=[pl.BlockSpec((1, 1), lambda i: (0, 0)), row, row, vec],
        out_shape=[jax.ShapeDtypeStruct((1, 1), F32), jax.ShapeDtypeStruct((lp, D), F32),
                   jax.ShapeDtypeStruct((lp, D), BF16), jax.ShapeDtypeStruct((1, D), F32)],
    )(h, target, gain)


def _silu_parts(g):
    s = _sigmoid(g)
    return g * s, s * (1.0 + g * (1.0 - s))


def _residual_norm_epi(scale, with_norm):
    def epi(accs, exs, row0):
        h2 = exs[0] + scale * accs[0]
        if not with_norm:
            return [h2]
        r = lax.rsqrt(jnp.mean(h2 * h2, axis=-1, keepdims=True) + EPS)
        return [h2, h2 * r * exs[1]]
    return epi


def _norm_bwd_epi(accs, exs, row0):
    dz = accs[0]
    x, res, g = exs
    r = lax.rsqrt(jnp.mean(x * x, axis=-1, keepdims=True) + EPS)
    xh = x * r
    dxh = dz * g
    dx = r * (dxh - xh * jnp.mean(dxh * xh, axis=-1, keepdims=True))
    rows = row0 + lax.broadcasted_iota(jnp.int32, (dz.shape[0], 1), 0)
    dh = jnp.where(rows >= NULL, res + dx, 0.0)
    return [dh, dh, jnp.sum(dz * xh, axis=0, keepdims=True)]


def _norm_bwd_outs(lp):
    return [((lp, D), F32, "mn", 0, ()), ((lp, D), BF16, "mn", 0, ()), ((1, D), F32, "nsum", 0, ())]


def _ffn_fwd(h, z, wg_t, wu_t, wd, pre, next_gain):
    lp = h.shape[0]
    tm = _tm(lp)

    def up_epi(accs, exs, row0):
        g, u = accs
        sg, dsg = _silu_parts(g)
        return [u, sg, dsg, sg * u]

    bshape = (lp, DFF)
    u_, sg_, dsg_, act = _mm("ffn_up", lp, DFF, [_term(z, wg_t, "nt", 0, (), pre), _term(z, wu_t, "nt", 1, (), pre)],
                             [(bshape, BF16, "mn", 0, ())] * 4, up_epi, tm=tm, tn=DFF, nsub=DFF // MXU_N,
                             vmem=VMEM_BIG)

    with_norm = next_gain is not None
    res = _mm("ffn_down", lp, D, [_term(act, wd, "nn", 0, (), pre)],
              [((lp, D), F32, "mn", 0, ())] + ([((lp, D), BF16, "mn", 0, ())] if with_norm else []),
              _residual_norm_epi(0.5, with_norm),
              extras=[(h, "mn", 0, ())] + ([(next_gain, "n", 0, ())] if with_norm else []),
              tm=tm, tn=D, i_outer=True, vmem=VMEM_BIG)
    return res[0], (res[1] if with_norm else None), dict(h=h, z=z, u=u_, sg=sg_, dsg=dsg_, act=act)


def _dw(name, a, b, m, n, scale=1.0):
    lp = a.shape[0]
    tm = _pick(m, (2944, 1408, 1024, 512, 256, 128))
    tn = _pick(n, (1024, 512, 256, 128))
    nk = lp // _pick(lp, (2112, 256, 128) if tm <= 1408 else (1056, 256, 128))

    def epi(accs, exs, row0):
        return [accs[0] * scale]

    (w,) = _mm(name, m, n, [_term(a, b, "tn")], [((m, n), BF16, "mn", 0, ())], epi, tm=tm, tn=tn, nk=nk,
               i_outer=True, vmem=VMEM_BIG)
    return w


def _ffn_bwd(dh, dhb, sv, gain, wg_t, wu_t, wd, pre):
    lp = dh.shape[0]
    tm = _tm(lp)

    def dact_epi(accs, exs, row0):
        u, sg, dsg = (e.astype(F32) for e in exs)
        da = 0.5 * accs[0]
        return [da * u * dsg, da * sg]

    dg_, du_ = _mm("ffn_dact", lp, DFF, [_term(dhb, wd, "nt", 0, (), pre)],
                   [((lp, DFF), BF16, "mn", 0, ())] * 2, dact_epi,
                   extras=[(sv["u"], "mn", 0, ()), (sv["sg"], "mn", 0, ()), (sv["dsg"], "mn", 0, ())],
                   tm=tm, tn=DFF, nsub=DFF // MXU_N, vmem=VMEM_BIG)
    d_wd = _dw("dw_down", sv["act"], dhb, DFF, D, 0.5)
    d_wg = _dw("dw_gate", dg_, sv["z"], DFF, D)
    d_wu = _dw("dw_up", du_, sv["z"], DFF, D)

    nk = 1
    dh2, dhb2, dgain = _mm("ffn_dz", lp, D, [_term(dg_, wg_t, "nn", 0, (), pre), _term(du_, wu_t, "nn", 0, (), pre)],
                           _norm_bwd_outs(lp), _norm_bwd_epi,
                           extras=[(sv["h"], "mn", 0, ()), (dh, "mn", 0, ()), (gain, "n", 0, ())],
                           tm=tm, tn=D, nk=nk, vmem=VMEM_BIG)
    return dh2, dhb2, dgain, d_wg, d_wu, d_wd


def _gla_gates(hin, w2p, b2p):
    lp = hin.shape[0]
    tr = _row_tile(lp)
    bf, bb = P_OFF["lrf"] // HP, P_OFF["lrb"] // HP

    def body(lf_ref, lb_ref, w_ref, b_ref, o_ref, c_ref):
        i = pl.program_id(0)
        rows = i * tr + lax.broadcasted_iota(jnp.int32, (tr, 1), 0)
        r = lax.broadcasted_iota(jnp.int32, (tr, tr), 0)
        c = lax.broadcasted_iota(jnp.int32, (tr, tr), 1)
        same = (r // CHUNK) == (c // CHUNK)
        for d, l_ref in enumerate((lf_ref, lb_ref)):
            logit = _dot3(l_ref[...], w_ref[d]) + b_ref[d]
            g = jnp.where(rows >= NULL, jax.nn.log_sigmoid(logit) * (1.0 / GLA_TAU), 0.0)
            o_ref[d] = g
            tmat = jnp.where(same & ((r >= c) if d == 0 else (r <= c)), 1.0, 0.0).astype(BF16)
            c_ref[d] = _dot_sel(tmat, g)

    spec = pl.BlockSpec((2, tr, 512), lambda i: (0, i, 0))
    return pl.pallas_call(
        body, name="gla_gates", grid=(lp // tr,),
        in_specs=[pl.BlockSpec((tr, HP), lambda i: (i, bf)), pl.BlockSpec((tr, HP), lambda i: (i, bb)),
                  pl.BlockSpec((2, HP, 512), lambda i: (0, 0, 0)), pl.BlockSpec((2, 1, 512), lambda i: (0, 0, 0))],
        out_specs=[spec, spec],
        out_shape=[jax.ShapeDtypeStruct((2, lp, 512), F32)] * 2,
    )(hin, hin, w2p, b2p)


def _gla_rows(lp):
    return _pick(lp, (384, 256, 128))


def _tri(d):
    r = lax.broadcasted_iota(jnp.int32, (CHUNK, CHUNK), 0)
    c = lax.broadcasted_iota(jnp.int32, (CHUNK, CHUNK), 1)
    return (r >= c) if d == 0 else (r <= c)


def _gla_fwd(hin, gates):
    lp = hin.shape[0]
    rb = _gla_rows(lp)
    nb = lp // rb
    cpb = rb // CHUNK
    nchunk = lp // CHUNK
    qo, ko, vo = P_OFF["qa"] // 512, P_OFF["ka"] // 512, P_OFF["va"] // 512
    scale = GLA_DK ** -0.5

    def body(qf, kf, vf, gf, qb, kb, vb_, gb, of, ob, sf, sb, st):
        @pl.when(pl.program_id(0) == 0)
        def _():
            st[...] = jnp.zeros_like(st)

        ins = ((qf, kf, vf, gf, of, sf), (qb, kb, vb_, gb, ob, sb))
        for ci in range(cpb):
            for d in range(2):
                q_ref, k_ref, v_ref, g_ref, o_ref, s_ref = ins[d]
                tri = _tri(d)
                c = ci if d == 0 else cpb - 1 - ci
                rows = slice(c * CHUNK, (c + 1) * CHUNK)
                for h in range(GLA_H):
                    sl = slice(h * HP, (h + 1) * HP)
                    q = q_ref[rows, sl] * scale
                    k = k_ref[rows, sl]
                    v = v_ref[rows, sl]
                    b = g_ref[rows, sl]
                    btot = b[CHUNK - 1:CHUNK] if d == 0 else b[0:1]
                    qd = (q * jnp.exp(b)).astype(BF16)
                    ki = (k * jnp.exp(-b)).astype(BF16)
                    ke = (k * jnp.exp(btot - b)).astype(BF16)
                    vb = v.astype(BF16)
                    att = jnp.where(tri, _dot(qd, ki, "nt"), 0.0)
                    s_prev = st[d, h]
                    o_ref[rows, sl] = _dot(att.astype(BF16), vb) + _dot(qd, s_prev.astype(BF16), "nt")
                    s_ref[h, c] = s_prev
                    st[d, h] = s_prev * jnp.exp(btot) + _dot(vb, ke, "tn")

    def specs(off):
        return (pl.BlockSpec((rb, 512), lambda b: (b, off)), pl.BlockSpec((rb, 512), lambda b: (nb - 1 - b, off)))

    (qf, qb), (kf, kb), (vf, vb2) = specs(qo), specs(ko), specs(vo)
    gf = pl.BlockSpec((None, rb, 512), lambda b: (0, b, 0))
    gb = pl.BlockSpec((None, rb, 512), lambda b: (1, nb - 1 - b, 0))
    of, ob = specs(0)
    sf = pl.BlockSpec((GLA_H, cpb, HP, HP), lambda b: (0, b, 0, 0))
    sb = pl.BlockSpec((GLA_H, cpb, HP, HP), lambda b: (0, nb - 1 - b, 0, 0))
    osh = jax.ShapeDtypeStruct((lp, GLA_H * HP), F32)
    ssh = jax.ShapeDtypeStruct((GLA_H, nchunk, HP, HP), F32)
    return pl.pallas_call(
        body, name="gla_fwd", grid=(nb,),
        in_specs=[qf, kf, vf, gf, qb, kb, vb2, gb], out_specs=[of, ob, sf, sb], out_shape=[osh, osh, ssh, ssh],
        scratch_shapes=[pltpu.VMEM((2, GLA_H, HP, HP), F32)], compiler_params=_cparams(VMEM_BIG),
    )(hin, hin, hin, gates, hin, hin, hin, gates)


def _gla_bwd(hin, gates, states, do):
    lp = hin.shape[0]
    rb = _gla_rows(lp)
    nb = lp // rb
    cpb = rb // CHUNK
    qo, ko, vo = P_OFF["qa"] // 512, P_OFF["ka"] // 512, P_OFF["va"] // 512
    scale = GLA_DK ** -0.5

    def body(qf, kf, vf, gf, sf, dof, qb, kb, vb_, gb, sb, dob,
             dqf, dkf, dvf, dgf, dqb, dkb, dvb, dgb, dst):
        @pl.when(pl.program_id(0) == 0)
        def _():
            dst[...] = jnp.zeros_like(dst)

        ins = ((qf, kf, vf, gf, sf, dof, dqf, dkf, dvf, dgf), (qb, kb, vb_, gb, sb, dob, dqb, dkb, dvb, dgb))
        for ci in range(cpb):
            for d in range(2):
                q_ref, k_ref, v_ref, g_ref, s_ref, do_ref, dq_ref, dk_ref, dv_ref, dg_ref = ins[d]
                tri, tri_t = _tri(d), _tri(1 - d)
                edge = lax.broadcasted_iota(jnp.int32, (CHUNK, 1), 0) == (CHUNK - 1 if d == 0 else 0)
                c = cpb - 1 - ci if d == 0 else ci
                rows = slice(c * CHUNK, (c + 1) * CHUNK)
                for h in range(GLA_H):
                    sl = slice(h * HP, (h + 1) * HP)
                    q = q_ref[rows, sl] * scale
                    k = k_ref[rows, sl]
                    v = v_ref[rows, sl]
                    dout = do_ref[rows, sl].astype(BF16)
                    b = g_ref[rows, sl]
                    btot = b[CHUNK - 1:CHUNK] if d == 0 else b[0:1]
                    e = jnp.exp(b)
                    ei = jnp.exp(-b)
                    et = jnp.exp(btot - b)
                    etot = jnp.exp(btot)
                    qd = q * e
                    ki = k * ei
                    ke = k * et
                    qdb, kib, keb, vb = qd.astype(BF16), ki.astype(BF16), ke.astype(BF16), v.astype(BF16)
                    att_t = jnp.where(tri_t, _dot(kib, qdb, "nt"), 0.0).astype(BF16)
                    d_att = jnp.where(tri, _dot(dout, vb, "nt"), 0.0).astype(BF16)
                    d_att_t = jnp.where(tri_t, _dot(vb, dout, "nt"), 0.0).astype(BF16)
                    s_prev = s_ref[h, c]
                    ds_t = dst[d, h]
                    ds_b = ds_t.astype(BF16)
                    dv = _dot(att_t, dout) + _dot(keb, ds_b, "nt")
                    d_qd = _dot(d_att, kib) + _dot(dout, s_prev.astype(BF16))
                    d_ki = _dot(d_att_t, qdb)
                    d_ke = _dot(vb, ds_b)
                    d_e = jnp.sum(s_prev * ds_t, axis=0, keepdims=True)
                    dst[d, h] = _dot(dout, qdb, "tn") + ds_t * etot
                    db = d_qd * qd - d_ki * ki - d_ke * ke
                    dbtot = jnp.sum(d_ke * ke, axis=0, keepdims=True) + d_e * etot
                    dq_ref[rows, sl] = d_qd * e * scale
                    dk_ref[rows, sl] = d_ki * ei + d_ke * et
                    dv_ref[rows, sl] = dv
                    dg_ref[rows, sl] = db + jnp.where(edge, dbtot, 0.0)

    def fw(off):
        return pl.BlockSpec((rb, 512), lambda b: (nb - 1 - b, off))

    def bw(off):
        return pl.BlockSpec((rb, 512), lambda b: (b, off))

    gf = pl.BlockSpec((None, rb, 512), lambda b: (0, nb - 1 - b, 0))
    gb = pl.BlockSpec((None, rb, 512), lambda b: (1, b, 0))
    sf = pl.BlockSpec((GLA_H, cpb, HP, HP), lambda b: (0, nb - 1 - b, 0, 0))
    sb = pl.BlockSpec((GLA_H, cpb, HP, HP), lambda b: (0, b, 0, 0))
    osh = jax.ShapeDtypeStruct((lp, GLA_H * HP), F32)
    res = pl.pallas_call(
        body, name="gla_bwd", grid=(nb,),
        in_specs=[fw(qo), fw(ko), fw(vo), gf, sf, fw(0), bw(qo), bw(ko), bw(vo), gb, sb, bw(0)],
        out_specs=[fw(0)] * 4 + [bw(0)] * 4, out_shape=[osh] * 8,
        scratch_shapes=[pltpu.VMEM((2, GLA_H, HP, HP), F32)], compiler_params=_cparams(VMEM_BIG),
    )(hin, hin, hin, gates, states[0], do, hin, hin, hin, gates, states[1], do)
    return res[:4], res[4:]


def _gla_out_fwd(o2, hin, gn):
    lp = hin.shape[0]
    tr = _row_tile(lp)
    ro = P_OFF["ra"] // 512

    def body(of_ref, ob_ref, r_ref, gn_ref, a_ref):
        r = r_ref[...]
        sr, _ = _silu_parts(r)
        for h in range(GLA_H):
            sl = slice(h * HP, (h + 1) * HP)
            o = of_ref[:, sl] + ob_ref[:, sl]
            rs = lax.rsqrt(jnp.mean(o * o, axis=-1, keepdims=True) + EPS)
            a_ref[:, sl] = (o * rs * gn_ref[:, sl] * sr[:, sl]).astype(BF16)

    row = pl.BlockSpec((tr, 512), lambda i: (i, 0))
    return pl.pallas_call(
        body, name="gla_out_fwd", grid=(lp // tr,),
        in_specs=[row, row, pl.BlockSpec((tr, 512), lambda i: (i, ro)), pl.BlockSpec((1, 512), lambda i: (0, 0))],
        out_specs=row,
        out_shape=jax.ShapeDtypeStruct((lp, 512), BF16),
    )(o2[0], o2[1], hin, gn)


def _gla_out_bwd(da, o2, hin, gn):
    lp = hin.shape[0]
    tr = _row_tile(lp)
    ro = P_OFF["ra"] // 512

    def body(da_ref, of_ref, ob_ref, r_ref, gn_ref, do_ref, dr_ref, dgn_ref):
        i = pl.program_id(0)
        r = r_ref[...]
        sr, dsr = _silu_parts(r)
        da_v = da_ref[...]
        parts = []
        for h in range(GLA_H):
            sl = slice(h * HP, (h + 1) * HP)
            o = of_ref[:, sl] + ob_ref[:, sl]
            rs = lax.rsqrt(jnp.mean(o * o, axis=-1, keepdims=True) + EPS)
            oh = o * rs
            gn_h = gn_ref[:, sl]
            dah = da_v[:, sl]
            dr_ref[:, sl] = (dah * oh * gn_h * dsr[:, sl]).astype(BF16)
            t = dah * sr[:, sl]
            parts.append(jnp.sum(t * oh, axis=0, keepdims=True))
            doh = t * gn_h
            do_ref[:, sl] = rs * (doh - oh * jnp.mean(doh * oh, axis=-1, keepdims=True))
        part = jnp.concatenate(parts, axis=1)

        @pl.when(i == 0)
        def _():
            dgn_ref[...] = part

        @pl.when(i != 0)
        def _():
            dgn_ref[...] += part

    row = pl.BlockSpec((tr, 512), lambda i: (i, 0))
    return pl.pallas_call(
        body, name="gla_out_bwd", grid=(lp // tr,),
        in_specs=[row, row, row, pl.BlockSpec((tr, 512), lambda i: (i, ro)), pl.BlockSpec((1, 512), lambda i: (0, 0))],
        out_specs=[row, row, pl.BlockSpec((1, 512), lambda i: (0, 0))],
        out_shape=[jax.ShapeDtypeStruct((lp, 512), F32), jax.ShapeDtypeStruct((lp, 512), BF16),
                   jax.ShapeDtypeStruct((1, 512), F32)],
    )(da, o2[0], o2[1], hin, gn)


def _gla_in_bwd(gf, gb, gates, hin, w2p, others):
    lp = hin.shape[0]
    tr = _row_tile(lp)
    bf, bb = P_OFF["lrf"] // HP, P_OFF["lrb"] // HP
    names = tuple(others)

    def seg(name):
        return slice(P_OFF[name], P_OFF[name] + P_WIDTH[name])

    def body(dqf_ref, dkf_ref, dvf_ref, dgf_ref, dqb_ref, dkb_ref, dvb_ref, dgb_ref, g_ref, lf_ref, lb_ref, w_ref,
             *rest):
        other_refs, (o_ref, dw_ref, db_ref) = rest[:len(names)], rest[len(names):]
        i = pl.program_id(0)
        for n, ref in zip(names, other_refs):
            o_ref[:, seg(n)] = ref[...].astype(BF16)
        o_ref[:, seg("qa")] = (dqf_ref[...] + dqb_ref[...]).astype(BF16)
        o_ref[:, seg("ka")] = (dkf_ref[...] + dkb_ref[...]).astype(BF16)
        o_ref[:, seg("va")] = (dvf_ref[...] + dvb_ref[...]).astype(BF16)
        olr_ref = o_ref.at[:, P_OFF["lrf"]:P_OFF["lrf"] + 2 * HP]
        rows = i * tr + lax.broadcasted_iota(jnp.int32, (tr, 1), 0)
        r = lax.broadcasted_iota(jnp.int32, (tr, tr), 0)
        c = lax.broadcasted_iota(jnp.int32, (tr, tr), 1)
        same = (r // CHUNK) == (c // CHUNK)
        for d, (l_ref, dg_ref) in enumerate(((lf_ref, dgf_ref), (lb_ref, dgb_ref))):
            tmat = jnp.where(same & ((r <= c) if d == 0 else (r >= c)), 1.0, 0.0).astype(BF16)
            dg = _dot_sel(tmat, dg_ref[...])
            sig_neg = 1.0 - jnp.exp(GLA_TAU * g_ref[d])
            dlogit = jnp.where(rows >= NULL, dg * (1.0 / GLA_TAU) * sig_neg, 0.0)
            olr_ref[:, d * HP:(d + 1) * HP] = _dot3(dlogit, w_ref[d], "nt").astype(BF16)
            dw = _dot3(l_ref[...], dlogit, "tn")
            dbias = jnp.sum(dlogit, axis=0, keepdims=True)

            @pl.when(i == 0)
            def _():
                dw_ref[d] = dw
                db_ref[d] = dbias

            @pl.when(i != 0)
            def _():
                dw_ref[d] += dw
                db_ref[d] += dbias

    two = pl.BlockSpec((2, tr, 512), lambda i: (0, i, 0))
    row = pl.BlockSpec((tr, 512), lambda i: (i, 0))
    return pl.pallas_call(
        body, name="gla_in_bwd", grid=(lp // tr,),
        in_specs=[row] * 8 + [two, pl.BlockSpec((tr, HP), lambda i: (i, bf)),
                  pl.BlockSpec((tr, HP), lambda i: (i, bb)), pl.BlockSpec((2, HP, 512), lambda i: (0, 0, 0))] +
                 [pl.BlockSpec((tr, P_WIDTH[n]), lambda i: (i, 0)) for n in names],
        out_specs=[pl.BlockSpec((tr, D_INP), lambda i: (i, 0)),
                   pl.BlockSpec((2, HP, 512), lambda i: (0, 0, 0)), pl.BlockSpec((2, 1, 512), lambda i: (0, 0, 0))],
        out_shape=[jax.ShapeDtypeStruct((lp, D_INP), BF16), jax.ShapeDtypeStruct((2, HP, 512), F32),
                   jax.ShapeDtypeStruct((2, 1, 512), F32)],
        compiler_params=_cparams(VMEM_BIG),
    )(*gf, *gb, gates, hin, hin, w2p, *[others[n] for n in names])


def _rope_tables(lp):
    n_tok = lp - NULL - NMETA
    rows = n_tok // GRID_W
    row = np.repeat(np.arange(rows), GRID_W).astype(np.float32)
    col = np.tile(np.arange(GRID_W), rows).astype(np.float32)
    inv = (ROPE_THETA ** (-np.arange(0, 32, 2, dtype=np.float32) / 32)).astype(np.float32)
    ang = np.concatenate([row[:, None] * inv, col[:, None] * inv], axis=-1)
    ang = np.concatenate([np.zeros((NULL + NMETA, 32), np.float32), ang], axis=0)
    cos, sin = np.cos(ang).astype(np.float32), np.sin(ang).astype(np.float32)
    z16 = np.zeros((lp, 16), np.float32)
    z64 = np.zeros((lp, 64), np.float32)
    c = np.concatenate([cos[:, :16], cos[:, :16], cos[:, 16:], cos[:, 16:], z64], axis=1)
    a = np.concatenate([-sin[:, :16], z16, -sin[:, 16:], z16, z64], axis=1)
    b = np.concatenate([z16, sin[:, :16], z16, sin[:, 16:], z64], axis=1)
    return jnp.asarray(c), jnp.asarray(a), jnp.asarray(b)


def _rope(x, c, a, b):
    return x * c + pltpu.roll(x, HP - 16, 1) * a + pltpu.roll(x, 16, 1) * b


def _rope_t(dx, c, a, b):
    return dx * c + pltpu.roll(dx * a, 16, 1) + pltpu.roll(dx * b, HP - 16, 1)


def _attn_prep(hin, gq, gk, tabs):
    lp = hin.shape[0]
    tr = _row_tile(lp)
    qo, ko, vo = P_OFF["qb"] // 1024, P_OFF["kb"] // 256, P_OFF["vb"] // 256

    def body(q_ref, k_ref, v_ref, gq_ref, gk_ref, c_ref, a_ref, b_ref, oq_ref, ok_ref, ov_ref):
        c, a, b = c_ref[...], a_ref[...], b_ref[...]
        for src, g_ref, dst, nh, sc in ((q_ref, gq_ref, oq_ref, ATT_H, Q_SCALE), (k_ref, gk_ref, ok_ref, ATT_KV, 1.0)):
            for h in range(nh):
                sl = slice(h * HP, (h + 1) * HP)
                x = src[:, sl]
                r = lax.rsqrt(jnp.sum(x * x, axis=-1, keepdims=True) * (1.0 / HEAD_DIM) + EPS)
                dst[:, sl] = (_rope(x * r * g_ref[...], c, a, b) * sc).astype(BF16)
        lane = lax.broadcasted_iota(jnp.int32, (1, ATT_KV * HP), 1)
        ov_ref[...] = jnp.where(lane % HP == HEAD_DIM, 1.0, v_ref[...]).astype(BF16)

    tab = pl.BlockSpec((tr, HP), lambda i: (i, 0))
    vec = pl.BlockSpec((1, HP), lambda i: (0, 0))
    return pl.pallas_call(
        body, name="attn_prep", grid=(lp // tr,),
        in_specs=[pl.BlockSpec((tr, 1024), lambda i: (i, qo)), pl.BlockSpec((tr, 256), lambda i: (i, ko)),
                  pl.BlockSpec((tr, 256), lambda i: (i, vo)), vec, vec, tab, tab, tab],
        out_specs=[pl.BlockSpec((tr, 1024), lambda i: (i, 0)), pl.BlockSpec((tr, 256), lambda i: (i, 0)),
                   pl.BlockSpec((tr, 256), lambda i: (i, 0))],
        out_shape=[jax.ShapeDtypeStruct((lp, 1024), BF16), jax.ShapeDtypeStruct((lp, 256), BF16),
                   jax.ShapeDtypeStruct((lp, 256), BF16)],
    )(hin, hin, hin, gq, gk, *tabs)


def _attn_prep_bwd(dqr, dkr, hin, gq, gk, tabs):
    lp = hin.shape[0]
    tr = _row_tile(lp)
    qo, ko = P_OFF["qb"] // 1024, P_OFF["kb"] // 256

    def body(dq_ref, dk_ref, q_ref, k_ref, gq_ref, gk_ref, c_ref, a_ref, b_ref, oq_ref, ok_ref, dgq_ref, dgk_ref):
        i = pl.program_id(0)
        c, a, b = c_ref[...], a_ref[...], b_ref[...]
        for src, dsrc, g_ref, dst, dg_ref, nh, sc in (
                (q_ref, dq_ref, gq_ref, oq_ref, dgq_ref, ATT_H, Q_SCALE),
                (k_ref, dk_ref, gk_ref, ok_ref, dgk_ref, ATT_KV, 1.0)):
            acc = jnp.zeros((1, HP), F32)
            for h in range(nh):
                sl = slice(h * HP, (h + 1) * HP)
                x = src[:, sl]
                r = lax.rsqrt(jnp.sum(x * x, axis=-1, keepdims=True) * (1.0 / HEAD_DIM) + EPS)
                xh = x * r
                dxn = _rope_t(dsrc[:, sl] * sc, c, a, b)
                acc = acc + jnp.sum(dxn * xh, axis=0, keepdims=True)
                dxh = dxn * g_ref[...]
                dx = r * (dxh - xh * (jnp.sum(dxh * xh, axis=-1, keepdims=True) * (1.0 / HEAD_DIM)))
                dst[:, sl] = dx.astype(BF16)

            @pl.when(i == 0)
            def _():
                dg_ref[...] = acc

            @pl.when(i != 0)
            def _():
                dg_ref[...] += acc

    tab = pl.BlockSpec((tr, HP), lambda i: (i, 0))
    vec = pl.BlockSpec((1, HP), lambda i: (0, 0))
    return pl.pallas_call(
        body, name="attn_prep_bwd", grid=(lp // tr,),
        in_specs=[pl.BlockSpec((tr, 1024), lambda i: (i, 0)), pl.BlockSpec((tr, 256), lambda i: (i, 0)),
                  pl.BlockSpec((tr, 1024), lambda i: (i, qo)), pl.BlockSpec((tr, 256), lambda i: (i, ko)),
                  vec, vec, tab, tab, tab],
        out_specs=[pl.BlockSpec((tr, 1024), lambda i: (i, 0)), pl.BlockSpec((tr, 256), lambda i: (i, 0)), vec, vec],
        out_shape=[jax.ShapeDtypeStruct((lp, 1024), BF16), jax.ShapeDtypeStruct((lp, 256), BF16),
                   jax.ShapeDtypeStruct((1, HP), F32), jax.ShapeDtypeStruct((1, HP), F32)],
    )(dqr, dkr, hin, hin, gq, gk, *tabs)


QB = 128
GH = 2
Q_SCALE = HEAD_DIM ** -0.5 * math.log2(math.e)
LN2 = math.log(2.0)


def _stack(ref, g0, n):
    return jnp.concatenate([ref[:, (g0 + g) * HP:(g0 + g + 1) * HP] for g in range(n)], axis=0)


def _attn_fwd(qr, kr, vb):
    lp = qr.shape[0]
    nq = lp // QB

    def body(q_ref, k_ref, v_ref, o_ref, lse_ref):
        qb = pl.program_id(1)
        keys = lax.broadcasted_iota(jnp.int32, (1, lp), 1)
        lane = lax.broadcasted_iota(jnp.int32, (1, HP), 1)
        rows = qb * QB + lax.broadcasted_iota(jnp.int32, (QB, 1), 0)
        for ch in range(ATT_G // GH):
            qs = _stack(q_ref, ch * GH, GH)
            s = _dot(qs, k_ref[...], "nt")
            s = jnp.where(keys >= NULL, s, -1e30)
            m = jnp.max(s, axis=-1, keepdims=True)
            p = jnp.exp2(s - m).astype(BF16)
            o_raw = _dot(p, v_ref[...])
            l = jnp.sum(jnp.where(lane == HEAD_DIM, o_raw, 0.0), axis=-1, keepdims=True)
            o = jnp.where(lane < HEAD_DIM, o_raw / l, 0.0)
            lse = m + jnp.log2(l)
            for g in range(GH):
                sl = slice((ch * GH + g) * HP, (ch * GH + g + 1) * HP)
                o_ref[:, sl] = jnp.where(rows >= NULL, o[g * QB:(g + 1) * QB], 0.0).astype(BF16)
                lse_ref[:, sl] = jnp.broadcast_to(lse[g * QB:(g + 1) * QB], (QB, HP))

    qspec = pl.BlockSpec((QB, ATT_G * HP), lambda kv, qb: (qb, kv))
    kspec = pl.BlockSpec((lp, HP), lambda kv, qb: (0, kv))
    return pl.pallas_call(
        body, name="attn_fwd", grid=(ATT_KV, nq),
        in_specs=[qspec, kspec, kspec], out_specs=[qspec, qspec],
        out_shape=[jax.ShapeDtypeStruct((lp, ATT_H * HP), BF16), jax.ShapeDtypeStruct((lp, ATT_H * HP), F32)],
        compiler_params=_cparams(VMEM_BIG),
    )(qr, kr, vb)


def _attn_bwd(qr, kr, vb, o, lse, do):
    lp = qr.shape[0]
    nq = lp // QB

    def body(q_ref, k_ref, v_ref, o_ref, lse_ref, do_ref, dq_ref, dk_ref, dv_ref):
        qb = pl.program_id(1)

        @pl.when(qb == 0)
        def _():
            dk_ref[...] = jnp.zeros_like(dk_ref)
            dv_ref[...] = jnp.zeros_like(dv_ref)

        keys = lax.broadcasted_iota(jnp.int32, (1, lp), 1)
        k = k_ref[...]
        dk_acc, dv_acc = None, None
        for ch in range(ATT_G // GH):
            g0 = ch * GH
            qs = _stack(q_ref, g0, GH)
            dos = _stack(do_ref, g0, GH)
            os_ = _stack(o_ref, g0, GH).astype(F32)
            lse_s = jnp.concatenate([lse_ref[:, (g0 + g) * HP:(g0 + g) * HP + 1] for g in range(GH)], axis=0)
            delta = jnp.sum(dos * os_, axis=-1, keepdims=True) * LN2
            s = _dot(qs, k, "nt")
            p = jnp.where(keys >= NULL, jnp.exp2(s - lse_s), 0.0)
            dob = dos.astype(BF16)
            dp = _dot((dos * LN2).astype(BF16), v_ref[...], "nt")
            ds = (p * (dp - delta)).astype(BF16)
            dq = _dot(ds, k)
            for g in range(GH):
                dq_ref[:, (g0 + g) * HP:(g0 + g + 1) * HP] = dq[g * QB:(g + 1) * QB]
            dv_c = _dot(p.astype(BF16), dob, "tn")
            dk_c = _dot(ds, qs, "tn")
            dv_acc = dv_c if dv_acc is None else dv_acc + dv_c
            dk_acc = dk_c if dk_acc is None else dk_acc + dk_c
        dv_ref[...] += dv_acc
        dk_ref[...] += dk_acc

    qspec = pl.BlockSpec((QB, ATT_G * HP), lambda kv, qb: (qb, kv))
    kspec = pl.BlockSpec((lp, HP), lambda kv, qb: (0, kv))
    return pl.pallas_call(
        body, name="attn_bwd", grid=(ATT_KV, nq),
        in_specs=[qspec, kspec, kspec, qspec, qspec, qspec], out_specs=[qspec, kspec, kspec],
        out_shape=[jax.ShapeDtypeStruct((lp, ATT_H * HP), F32), jax.ShapeDtypeStruct((lp, ATT_KV * HP), F32),
                   jax.ShapeDtypeStruct((lp, ATT_KV * HP), F32)],
        compiler_params=_cparams(VMEM_BIG),
    )(qr, kr, vb, o, lse, do)


def _mixer_fwd(h, z, wl, l, tabs, next_gain):
    lp = h.shape[0]
    tm = _tm(lp)

    def id_epi(accs, exs, row0):
        return [accs[0]]

    (hin,) = _mm("in_proj", lp, D_INP, [_term(z, wl["win_t"], "nt", 0, (), (l,))], [((lp, D_INP), F32, "mn", 0, ())],
                 id_epi, tm=tm, tn=D_INP // 2, vmem=VMEM_BIG)
    gates, cum = _gla_gates(hin, wl["w2p"][l], wl["b2p"][l])
    o_f, o_b, s_f, s_b = _gla_fwd(hin, cum)
    o2, states = (o_f, o_b), (s_f, s_b)
    a = _gla_out_fwd(o2, hin, wl["gn"][l])
    qr, kr, vb = _attn_prep(hin, wl["gq"][l], wl["gk"][l], tabs)
    b, lse = _attn_fwd(qr, kr, vb)

    def merge_epi(accs, exs, row0):
        pa, pb = accs
        ga, gb, bma, bmb = exs
        y = _sigmoid(ga + bma) * pa + _sigmoid(gb + bmb) * pb
        return [y, pa, pb]

    y, pa, pb = _mm("merge", lp, D, [_term(a, wl["wpa_t"], "nt", 0, (), (l,)), _term(b, wl["wpb_t"], "nt", 1, (), (l,))],
                    [((lp, D), BF16, "mn", 0, ())] * 3, merge_epi,
                    extras=[(hin, "mn", P_OFF["ga"] // D, ()), (hin, "mn", P_OFF["gb"] // D, ()),
                            (wl["bm"], "n", 0, (l, 0)), (wl["bm"], "n", 0, (l, 1))],
                    tm=tm, tn=D, nsub=D // MXU_N, i_outer=True, vmem=VMEM_BIG)

    h2, z2 = _mm("out_proj", lp, D, [_term(y, wl["wout"], "nn", 0, (), (l,))],
                 [((lp, D), F32, "mn", 0, ()), ((lp, D), BF16, "mn", 0, ())], _residual_norm_epi(1.0, True),
                 extras=[(h, "mn", 0, ()), (next_gain, "n", 0, ())], tm=tm, tn=D, i_outer=True, vmem=VMEM_BIG)
    sv = dict(h=h, z=z, hin=hin, gates=gates, cum=cum, o2=o2, states=states, a=a, qr=qr, kr=kr, vb=vb, b=b, lse=lse,
              y=y, pa=pa, pb=pb)
    return h2, z2, sv


def _mixer_bwd(dh, dhb, sv, gain, wl, l, tabs):
    lp = dh.shape[0]
    tm = _tm(lp)
    hin = sv["hin"]

    def merge_bwd_epi(accs, exs, row0):
        dy = accs[0]
        ga, gb, pa, pb, bma, bmb = exs
        sa = _sigmoid(ga + bma)
        sb = _sigmoid(gb + bmb)
        dga = dy * pa.astype(F32) * sa * (1.0 - sa)
        dgb = dy * pb.astype(F32) * sb * (1.0 - sb)
        return [dy * sa, dy * sb, dga, dgb, jnp.sum(dga, axis=0, keepdims=True), jnp.sum(dgb, axis=0, keepdims=True)]

    big = ((lp, D), BF16, "mn", 0, ())
    vec = ((1, D), F32, "nsum", 0, ())
    dpa, dpb, dga, dgb, dbma, dbmb = _mm(
        "merge_bwd", lp, D, [_term(dhb, wl["wout"], "nt", 0, (), (l,))], [big, big, big, big, vec, vec], merge_bwd_epi,
        extras=[(hin, "mn", P_OFF["ga"] // D, ()), (hin, "mn", P_OFF["gb"] // D, ()), (sv["pa"], "mn", 0, ()),
                (sv["pb"], "mn", 0, ()), (wl["bm"], "n", 0, (l, 0)), (wl["bm"], "n", 0, (l, 1))],
        tm=tm, tn=D, nsub=D // MXU_N, vmem=VMEM_BIG)
    d_wout = _dw("dw_out", sv["y"], dhb, D, D)
    d_wpa_t = _dw("dw_pa", dpa, sv["a"], D, 512)
    d_wpb_t = _dw("dw_pb", dpb, sv["b"], D, ATT_H * HP)

    def id_epi(accs, exs, row0):
        return [accs[0]]

    (da,) = _mm("d_a", lp, 512, [_term(dpa, wl["wpa_t"], "nn", 0, (), (l,))], [((lp, 512), F32, "mn", 0, ())], id_epi,
                tm=tm, tn=512, i_outer=True, vmem=VMEM_BIG)
    (db,) = _mm("d_b", lp, ATT_H * HP, [_term(dpb, wl["wpb_t"], "nn", 0, (), (l,))],
                [((lp, ATT_H * HP), F32, "mn", 0, ())], id_epi, tm=tm, tn=512, i_outer=True, vmem=VMEM_BIG)
    d_o, d_ra, d_gn = _gla_out_bwd(da, sv["o2"], hin, wl["gn"][l])
    g_fw, g_bw = _gla_bwd(hin, sv["cum"], sv["states"], d_o)
    dqr, dkr, dvb = _attn_bwd(sv["qr"], sv["kr"], sv["vb"], sv["b"], sv["lse"], db)
    d_qb, d_kb, d_gq, d_gk = _attn_prep_bwd(dqr, dkr, hin, wl["gq"][l], wl["gk"][l], tabs)
    dhin, d_w2p, d_b2p = _gla_in_bwd(g_fw, g_bw, sv["gates"], hin, wl["w2p"][l],
                                     dict(qb=d_qb, ga=dga, gb=dgb, ra=d_ra, kb=d_kb, vb=dvb))
    d_win_t = _dw("dw_in", dhin, sv["z"], D_INP, D)
    dh2, dhb2, dgain = _mm("in_proj_dz", lp, D, [_term(dhin, wl["win_t"], "nn", 0, (), (l,))], _norm_bwd_outs(lp),
                           _norm_bwd_epi, extras=[(sv["h"], "mn", 0, ()), (dh, "mn", 0, ()), (gain, "n", 0, ())],
                           tm=tm, tn=D, nk=2, vmem=VMEM_BIG)
    grads = dict(gain=dgain, wout=d_wout, wpa_t=d_wpa_t, wpb_t=d_wpb_t, win_t=d_win_t, gn=d_gn, w2p=d_w2p, b2p=d_b2p,
                 gq=d_gq, gk=d_gk, bma=dbma, bmb=dbmb)
    return dh2, dhb2, grads


def _mesh_pos():
    x, y, c = lax.axis_index("x"), lax.axis_index("y"), lax.axis_index("c")
    chips = [(1 - x, y), (x, 1 - y), (1 - x, 1 - y)]
    return x, y, c, chips


def _dev_index(x, y, c):
    return 4 * x + 2 * y + c


def _all_gather(name, shards, leads):
    nt = len(shards)

    def blk(ref, lead, idx):
        return ref.at[(slice(None),) * lead + (idx,)]

    def body(*refs):
        xs, outs = refs[:nt], refs[nt:2 * nt]
        send_sems, recv_sems, local_sems = refs[2 * nt:]
        x, y, c, chips = _mesh_pos()
        me, sibling = (x, y, c), (x, y, 1 - c)

        def copy(t, k, block, to, own=False):
            dst = blk(outs[t], leads[t], _dev_index(*block))
            return pltpu.make_async_remote_copy(
                src_ref=xs[t] if own else dst, dst_ref=dst, send_sem=send_sems.at[t, k], recv_sem=recv_sems.at[t, k],
                device_id=to, device_id_type=MESH)

        locals_ = [pltpu.make_async_copy(xs[t], blk(outs[t], leads[t], _dev_index(*me)), local_sems.at[t])
                   for t in range(nt)]
        for cp in locals_:
            cp.start()
        first = []
        for t in range(nt):
            first.append(copy(t, 0, me, sibling, own=True))
            first += [copy(t, 1 + j, me, (*chip, c), own=True) for j, chip in enumerate(chips)]
        for cp in first:
            cp.start()
        passed = []
        for j, chip in enumerate(chips):
            for t in range(nt):
                copy(t, 1 + j, (*chip, c), me).wait_recv()
                fw = copy(t, 4 + j, (*chip, c), sibling)
                fw.start()
                passed.append(fw)
        for t in range(nt):
            copy(t, 0, sibling, me).wait_recv()
        for j, chip in enumerate(chips):
            for t in range(nt):
                copy(t, 4 + j, (*chip, 1 - c), me).wait_recv()
        for cp in first + passed:
            cp.wait_send()
        for cp in locals_:
            cp.wait()

    out_shape = [jax.ShapeDtypeStruct(s.shape[:ld] + (NDEV,) + s.shape[ld:], s.dtype) for s, ld in zip(shards, leads)]
    return pl.pallas_call(
        body, name=name, in_specs=[ANY] * nt, out_specs=[ANY] * nt, out_shape=out_shape,
        scratch_shapes=[pltpu.SemaphoreType.DMA((nt, 7)), pltpu.SemaphoreType.DMA((nt, 7)),
                        pltpu.SemaphoreType.DMA((nt,))],
    )(*shards)


def _exchange_sibling(name, gs):
    nt = len(gs)

    def body(*refs):
        xs, outs = refs[:nt], refs[nt:2 * nt]
        send_sems, recv_sems = refs[2 * nt:]
        x, y, c, _ = _mesh_pos()
        sibling = (x, y, 1 - c)
        copies = []
        for t in range(nt):
            for ch in range(4):
                copies.append(pltpu.make_async_remote_copy(
                    src_ref=xs[t].at[2 * ch + (1 - c)], dst_ref=outs[t].at[ch],
                    send_sem=send_sems.at[t, ch], recv_sem=recv_sems.at[t, ch],
                    device_id=sibling, device_id_type=MESH))
        for cp in copies:
            cp.start()
        for cp in copies:
            cp.wait()

    out_shape = [jax.ShapeDtypeStruct((4,) + g.shape[1:], g.dtype) for g in gs]
    return pl.pallas_call(
        body, name=name, in_specs=[ANY] * nt, out_specs=[ANY] * nt, out_shape=out_shape,
        scratch_shapes=[pltpu.SemaphoreType.DMA((nt, 4)), pltpu.SemaphoreType.DMA((nt, 4))],
    )(*gs)


def _pair_sum(name, gs, recv):
    c = lax.axis_index("c")
    outs = []
    for t, (g, rv) in enumerate(zip(gs, recv)):
        _, r, cols = rv.shape

        def body(c_ref, g_ref, r_ref, o_ref):
            o_ref[...] = (g_ref[...].astype(F32) + r_ref[...].astype(F32)).astype(o_ref.dtype)

        outs.append(pl.pallas_call(
            body, name=f"{name}_{t}",
            grid_spec=pltpu.PrefetchScalarGridSpec(
                num_scalar_prefetch=1, grid=(4,),
                in_specs=[pl.BlockSpec((None, r, cols), lambda ch, cr: (2 * ch + cr[0], 0, 0)),
                          pl.BlockSpec((None, r, cols), lambda ch, cr: (ch, 0, 0))],
                out_specs=pl.BlockSpec((None, r, cols), lambda ch, cr: (ch, 0, 0))),
            out_shape=jax.ShapeDtypeStruct(rv.shape, rv.dtype),
        )(jnp.reshape(c, (1,)).astype(jnp.int32), g, rv))
    return outs


def _exchange_chips(name, ps):
    nt = len(ps)

    def body(*refs):
        xs, outs = refs[:nt], refs[nt:2 * nt]
        send_sems, recv_sems = refs[2 * nt:]
        x, y, c, chips = _mesh_pos()
        copies = []
        for t in range(nt):
            for j, (cx, cy) in enumerate(chips):
                copies.append(pltpu.make_async_remote_copy(
                    src_ref=xs[t].at[2 * cx + cy], dst_ref=outs[t].at[j],
                    send_sem=send_sems.at[t, j], recv_sem=recv_sems.at[t, j],
                    device_id=(cx, cy, c), device_id_type=MESH))
        for cp in copies:
            cp.start()
        for cp in copies:
            cp.wait()

    out_shape = [jax.ShapeDtypeStruct((3,) + p.shape[1:], p.dtype) for p in ps]
    return pl.pallas_call(
        body, name=name, in_specs=[ANY] * nt, out_specs=[ANY] * nt, out_shape=out_shape,
        scratch_shapes=[pltpu.SemaphoreType.DMA((nt, 3)), pltpu.SemaphoreType.DMA((nt, 3))],
    )(*ps)


def _final_sum(name, ps, recv, transposed):
    chip = 2 * lax.axis_index("x") + lax.axis_index("y")
    outs = []
    for t, (p, rv) in enumerate(zip(ps, recv)):
        _, r, cols = rv.shape
        tr_out = transposed[t]
        oshape = (cols, r) if tr_out else (r, cols)

        def body(c_ref, p_ref, r0_ref, r1_ref, r2_ref, o_ref):
            acc = ((p_ref[...].astype(F32) + r0_ref[...].astype(F32)) + r1_ref[...].astype(F32)) + r2_ref[...].astype(F32)
            o_ref[...] = acc.T if tr_out else acc

        outs.append(pl.pallas_call(
            body, name=f"{name}_{t}",
            grid_spec=pltpu.PrefetchScalarGridSpec(
                num_scalar_prefetch=1, grid=(1,),
                in_specs=[pl.BlockSpec((None, r, cols), lambda i, cr: (cr[0], 0, 0))] +
                         [pl.BlockSpec((None, r, cols), lambda i, cr, j=j: (j, 0, 0)) for j in range(3)],
                out_specs=pl.BlockSpec(oshape, lambda i, cr: (0, 0))),
            out_shape=jax.ShapeDtypeStruct(oshape, F32),
        )(jnp.reshape(chip, (1,)).astype(jnp.int32), p, rv, rv, rv))
    return outs


def _reduce_scatter(tag, gs, transposed):
    recv1 = _exchange_sibling(f"rs_sibling_{tag}", gs)
    ps = _pair_sum(f"rs_pair_{tag}", gs, recv1)
    recv2 = _exchange_chips(f"rs_chips_{tag}", ps)
    return _final_sum(f"rs_sum_{tag}", ps, recv2, transposed)


def _sum_gathered(g):
    _, r, cols = g.shape

    def body(g_ref, o_ref):
        acc = g_ref[0]
        for d in range(1, NDEV):
            acc = acc + g_ref[d]
        o_ref[...] = acc

    return pl.pallas_call(body, name="small_sum", out_shape=jax.ShapeDtypeStruct((r, cols), F32))(g)


HBM = pl.BlockSpec(memory_space=pltpu.HBM)
SEM = pl.BlockSpec(memory_space=pltpu.SEMAPHORE)
EFFECT = pltpu.SideEffectType.DATAFLOW_SIDE_EFFECTING
NREL = NDEV - 1


def _related(k):
    x, y, c = lax.axis_index("x"), lax.axis_index("y"), lax.axis_index("c")
    px = 1 - x if k & 4 else x
    py = 1 - y if k & 2 else y
    pc = 1 - c if k & 1 else c
    return (px, py, pc), _dev_index(px, py, pc)


def _in_hbm(a):
    return pltpu.with_memory_space_constraint(a, pltpu.HBM)


ALL_RELS = tuple(range(1, NDEV))
CHIP_RELS = (4, 2, 6)


def _split_copies(xs, lands, send_sems, recv_sems, src_of, dst_of, rels):
    copies = []
    for t in range(len(xs)):
        for q, k in enumerate(rels):
            peer, peer_idx = _related(k)
            copies.append(pltpu.make_async_remote_copy(
                src_ref=src_of(xs[t], t, peer_idx), dst_ref=dst_of(lands[t], t, q, peer_idx),
                send_sem=send_sems.at[t * len(rels) + q], recv_sem=recv_sems.at[t * len(rels) + q],
                device_id=peer, device_id_type=MESH))
    return copies


def _exchange_start(name, xs, lands, src_of, dst_of, after, rels=ALL_RELS):
    nt = len(xs)

    def body(*refs):
        x_refs, land_refs = refs[:nt], refs[nt:2 * nt]
        send_sems, recv_sems = refs[2 * nt + 1], refs[2 * nt + 2]
        token = refs[-1]
        for cp in _split_copies(x_refs, land_refs, send_sems, recv_sems, src_of, dst_of, rels):
            cp.start()
        token[...] = jnp.zeros_like(token)

    res = pl.pallas_call(
        body, name=name,
        out_shape=(pltpu.SemaphoreType.DMA((nt * len(rels),)), pltpu.SemaphoreType.DMA((nt * len(rels),)),
                   *[pltpu.HBM(a.shape, a.dtype) for a in xs], *[pltpu.HBM(a.shape, a.dtype) for a in lands],
                   jax.ShapeDtypeStruct((8, 128), F32)),
        in_specs=[HBM] * (2 * nt) + [ANY],
        out_specs=(SEM, SEM, *[HBM] * (2 * nt), pl.BlockSpec(memory_space=pltpu.VMEM)),
        input_output_aliases={i: 2 + i for i in range(2 * nt)},
        compiler_params=pltpu.CompilerParams(has_side_effects=EFFECT),
    )(*[_in_hbm(a) for a in xs], *[_in_hbm(a) for a in lands], after)
    return res[0], res[1], res[2:2 + nt], res[2 + nt:2 + 2 * nt], res[-1]


def _exchange_wait(name, send_sems, recv_sems, xs, lands, src_of, dst_of, after, rels=ALL_RELS):
    nt = len(xs)

    def body(*refs):
        x_refs, land_refs = refs[:nt], refs[nt:2 * nt]
        send_sems, recv_sems = refs[2 * nt], refs[2 * nt + 1]
        for cp in _split_copies(x_refs, land_refs, send_sems, recv_sems, src_of, dst_of, rels):
            cp.wait_send()
            cp.wait_recv()

    res = pl.pallas_call(
        body, name=name,
        out_shape=(*[pltpu.HBM(a.shape, a.dtype) for a in xs], *[pltpu.HBM(a.shape, a.dtype) for a in lands]),
        in_specs=[HBM] * (2 * nt) + [SEM, SEM, ANY], out_specs=tuple([HBM] * (2 * nt)),
        input_output_aliases={i: i for i in range(2 * nt)},
        compiler_params=pltpu.CompilerParams(has_side_effects=EFFECT),
    )(*xs, *lands, send_sems, recv_sems, after)
    return res[:nt], res[nt:]


def _gather_start(name, shards, leads, after):
    def src_of(x_ref, t, peer_idx):
        return x_ref

    def dst_of(land_ref, t, k, peer_idx):
        me = _dev_index(lax.axis_index("x"), lax.axis_index("y"), lax.axis_index("c"))
        return land_ref.at[(slice(None),) * leads[t] + (me,)]

    lands = [lax.empty(s.shape[:ld] + (NDEV,) + s.shape[ld:], s.dtype) for s, ld in zip(shards, leads)]
    return _exchange_start(name, shards, lands, src_of, dst_of, after)


def _gather_wait(name, started, leads, after):
    send_sems, recv_sems, shards, lands, _ = started

    def src_of(x_ref, t, peer_idx):
        return x_ref

    def dst_of(land_ref, t, k, peer_idx):
        return land_ref.at[(slice(None),) * leads[t] + (peer_idx,)]

    shards, lands = _exchange_wait(name, send_sems, recv_sems, shards, lands, src_of, dst_of, after)
    me = _dev_index(lax.axis_index("x"), lax.axis_index("y"), lax.axis_index("c"))
    return [lax.dynamic_update_index_in_dim(g, s, me, ld) for g, s, ld in zip(lands, shards, leads)]


def _scatter_src(x_ref, t, peer_idx):
    return x_ref.at[peer_idx]


def _scatter_dst(land_ref, t, q, peer_idx):
    return land_ref.at[q]


def _chips_src(x_ref, t, peer_idx):
    return x_ref.at[peer_idx // 2]


def _chips_start(name, ps, after):
    lands = [lax.empty((len(CHIP_RELS),) + p.shape[1:], p.dtype) for p in ps]
    return _exchange_start(name, ps, lands, _chips_src, _scatter_dst, after, CHIP_RELS)


def _chips_wait(name, started, after):
    send_sems, recv_sems, ps, lands, _ = started
    return _exchange_wait(name, send_sems, recv_sems, ps, lands, _chips_src, _scatter_dst, after, CHIP_RELS)


def _scatter_start(name, gs, after):
    lands = [lax.empty((NREL,) + g.shape[1:], g.dtype) for g in gs]
    return _exchange_start(name, gs, lands, _scatter_src, _scatter_dst, after)


def _scatter_wait(name, started, after, transposed):
    send_sems, recv_sems, gs, lands, _ = started
    gs, lands = _exchange_wait(name, send_sems, recv_sems, gs, lands, _scatter_src, _scatter_dst, after)
    me = _dev_index(lax.axis_index("x"), lax.axis_index("y"), lax.axis_index("c"))
    outs = []
    for t, (g, rv) in enumerate(zip(gs, lands)):
        _, r, cols = rv.shape
        tr_out = transposed[t]
        oshape = (cols, r) if tr_out else (r, cols)

        def body(c_ref, own_ref, rv_ref, o_ref):
            acc = own_ref[...].astype(F32)
            for k in range(NREL):
                acc = acc + rv_ref[k].astype(F32)
            o_ref[...] = acc.T if tr_out else acc

        outs.append(pl.pallas_call(
            body, name=f"{name}_sum_{t}",
            grid_spec=pltpu.PrefetchScalarGridSpec(
                num_scalar_prefetch=1, grid=(1,),
                in_specs=[pl.BlockSpec((None, r, cols), lambda i, cr: (cr[0], 0, 0)),
                          pl.BlockSpec((NREL, r, cols), lambda i, cr: (0, 0, 0))],
                out_specs=pl.BlockSpec(oshape, lambda i, cr: (0, 0))),
            out_shape=jax.ShapeDtypeStruct(oshape, F32), compiler_params=_cparams(VMEM_BIG),
        )(jnp.reshape(me, (1,)).astype(jnp.int32), g, rv))
    return outs


def _adamw(w, g, m, v):
    shape = w.shape
    c1 = 1.0 / (1.0 - ADAM_B1 ** ADAM_STEP)
    c2 = 1.0 / (1.0 - ADAM_B2 ** ADAM_STEP)
    big = w.size > 65536 and w.ndim >= 3
    if big:
        lead, rows, cols = shape[:-2], shape[-2], shape[-1]
        tr = _pick(rows, (1024, 512, 256, 128))
        w2, g2, m2, v2 = w, g, m, v
        grid = lead + (rows // tr,)
        spec = pl.BlockSpec((None,) * len(lead) + (tr, cols), lambda *idx: idx + (0,))
        osh = jax.ShapeDtypeStruct(shape, F32)
    else:
        cols = shape[-1]
        rows = math.prod(shape[:-1]) if len(shape) > 1 else 1
        w2, g2, m2, v2 = (jnp.reshape(t, (rows, cols)) for t in (w, g, m, v))
        grid = (1,)
        spec = pl.BlockSpec((rows, cols), lambda i: (0, 0))
        osh = jax.ShapeDtypeStruct((rows, cols), F32)

    def body(w_ref, g_ref, m_ref, v_ref, d_ref, nm_ref, nv_ref):
        gv = g_ref[...]
        nm = ADAM_B1 * m_ref[...] + (1.0 - ADAM_B1) * gv
        nv = ADAM_B2 * v_ref[...] + (1.0 - ADAM_B2) * (gv * gv)
        d_ref[...] = -ADAM_LR * ((nm * c1) / (jnp.sqrt(nv * c2) + ADAM_EPS) + ADAM_WD * w_ref[...])
        nm_ref[...] = nm
        nv_ref[...] = nv

    d, nm, nv = pl.pallas_call(
        body, name="adamw", grid=grid, in_specs=[spec] * 4, out_specs=[spec] * 3, out_shape=[osh] * 3,
        compiler_params=_cparams(VMEM_BIG),
    )(w2, g2, m2, v2)
    return jnp.reshape(d, shape), jnp.reshape(nm, shape), jnp.reshape(nv, shape)


def _pad_heads(w, name):
    if name not in P_HEADS:
        return w
    nh, real = P_HEADS[name]
    w = jnp.reshape(w, w.shape[:-2] + (nh, real, w.shape[-1]))
    w = jnp.pad(w, [(0, 0)] * (w.ndim - 2) + [(0, HP - real), (0, 0)])
    return jnp.reshape(w, w.shape[:-3] + (nh * HP, w.shape[-1]))


def _unpad_heads(w, name):
    if name not in P_HEADS:
        return w
    nh, real = P_HEADS[name]
    w = jnp.reshape(w, w.shape[:-2] + (nh, HP, w.shape[-1]))[..., :real, :]
    return jnp.reshape(w, w.shape[:-3] + (nh * real, w.shape[-1]))


def _win_pad(win_t):
    segs, o = {}, 0
    for n, s in zip(IN_NAMES, IN_SIZES):
        segs[n] = win_t[..., o:o + s, :]
        o += s
    return jnp.concatenate([_pad_heads(segs[n], n) for n in P_ORDER], axis=-2)


def _win_unpad(win_p):
    segs = {n: _unpad_heads(win_p[..., P_OFF[n]:P_OFF[n] + P_WIDTH[n], :], n) for n in P_ORDER}
    return jnp.concatenate([segs[n] for n in IN_NAMES], axis=-2)


def _t(w):
    return jnp.swapaxes(w, -1, -2)


def _layer_weights(g_g, g_u, g_d, g_in, g_pa, g_pb, g_out, gains, w2, b2, bm, gn, gq, gk):
    w2p = jnp.pad(jnp.reshape(w2, (2, GLA_RANK, GLA_H, GLA_DK)), ((0, 0), (0, HP - GLA_RANK), (0, 0), (0, HP - GLA_DK)))
    b2p = jnp.pad(jnp.reshape(b2, (2, 1, GLA_H, GLA_DK)), ((0, 0), (0, 0), (0, 0), (0, HP - GLA_DK)))
    wpb_t = jnp.pad(jnp.reshape(g_pb, (D, ATT_H, HEAD_DIM)), ((0, 0), (0, 0), (0, HP - HEAD_DIM)))
    return dict(
        gains=jnp.reshape(gains, (1, 3, 1, D)),
        wg_t=jnp.reshape(g_g, (1, 2, DFF, D)), wu_t=jnp.reshape(g_u, (1, 2, DFF, D)), wd=jnp.reshape(g_d, (1, 2, DFF, D)),
        win_t=_win_pad(jnp.reshape(g_in, (1, D_IN, D))), wpa_t=jnp.reshape(g_pa, (1, D, 512)),
        wpb_t=jnp.reshape(wpb_t, (1, D, ATT_H * HP)), wout=jnp.reshape(g_out, (1, D, D)),
        w2p=jnp.reshape(w2p, (1, 2, HP, GLA_H * HP)), b2p=jnp.reshape(b2p, (1, 2, 1, GLA_H * HP)),
        bm=jnp.reshape(bm, (1, 2, 1, D)), gn=jnp.reshape(gn, (1, 1, GLA_H * HP)),
        gq=jnp.pad(jnp.reshape(gq, (1, 1, HEAD_DIM)), ((0, 0), (0, 0), (0, HP - HEAD_DIM))),
        gk=jnp.pad(jnp.reshape(gk, (1, 1, HEAD_DIM)), ((0, 0), (0, 0), (0, HP - HEAD_DIM))))


def _layer_fwd(h, z, w, tabs, next_gain):
    h, z, s0 = _ffn_fwd(h, z, w["wg_t"], w["wu_t"], w["wd"], (0, 0), w["gains"][0, 1])
    h, z, s1 = _mixer_fwd(h, z, w, 0, tabs, w["gains"][0, 2])
    h, z, s2 = _ffn_fwd(h, z, w["wg_t"], w["wu_t"], w["wd"], (0, 1), next_gain)
    return h, z, (s0, s1, s2)


def _layer_bwd_upper(dh, dhb, saved, w, tabs):
    _, s1, s2 = saved
    dh, dhb, dg2, dwg1, dwu1, dwd1 = _ffn_bwd(dh, dhb, s2, w["gains"][0, 2], w["wg_t"], w["wu_t"], w["wd"], (0, 1))
    dh, dhb, gm = _mixer_bwd(dh, dhb, s1, w["gains"][0, 1], w, 0, tabs)
    gm.update(gain2=dg2, wg1=dwg1, wu1=dwu1, wd1=dwd1)
    return dh, dhb, gm


def _layer_bwd_lower(dh, dhb, saved, w, gm):
    dh, dhb, dg0, dwg0, dwu0, dwd0 = _ffn_bwd(dh, dhb, saved[0], w["gains"][0, 0], w["wg_t"], w["wu_t"], w["wd"], (0, 0))
    gm.update(gain0=dg0, wg0=dwg0, wu0=dwu0, wd0=dwd0)
    return dh, dhb, gm


def _layer_bwd(dh, dhb, saved, w, tabs):
    dh, dhb, gm = _layer_bwd_upper(dh, dhb, saved, w, tabs)
    return _layer_bwd_lower(dh, dhb, saved, w, gm)


def _blocks(ts):
    return [jnp.reshape(t, (NDEV, t.shape[0] // NDEV, t.shape[1])) for t in ts]


def _upper_grads(g):
    d_in = _win_unpad(g["win_t"])
    d_pb = jnp.reshape(jnp.reshape(g["wpb_t"], (D, ATT_H, HP))[:, :, :HEAD_DIM], (D, 512))
    return _blocks([g["wg1"], g["wu1"], g["wd1"], d_in, g["wpa_t"], d_pb, g["wout"]])


def _lower_grads(g):
    return _blocks([g["wg0"], g["wu0"], g["wd0"]])


def _big_grads(g):
    return _lower_grads(g) + _upper_grads(g)


def kernel(x, meta_tokens, norm_gains, ffn_w_gate, ffn_w_up, ffn_w_down, w_in, gla_w2, gla_b2, gla_gn, q_norm, k_norm, w_pa, w_pb, b_merge, w_out, final_norm, loss_target, m_meta_tokens, m_norm_gains, m_ffn_w_gate, m_ffn_w_up, m_ffn_w_down, m_w_in, m_gla_w2, m_gla_b2, m_gla_gn, m_q_norm, m_k_norm, m_w_pa, m_w_pb, m_b_merge, m_w_out, m_final_norm, v_meta_tokens, v_norm_gains, v_ffn_w_gate, v_ffn_w_up, v_ffn_w_down, v_w_in, v_gla_w2, v_gla_b2, v_gla_gn, v_q_norm, v_k_norm, v_w_pa, v_w_pb, v_b_merge, v_w_out, v_final_norm):
    dev = _dev_index(lax.axis_index("x"), lax.axis_index("y"), lax.axis_index("c"))
    sh_g = _t(ffn_w_gate).astype(BF16)
    sh_u = _t(ffn_w_up).astype(BF16)
    sh_d = ffn_w_down.astype(BF16)
    sh_in = _t(w_in).astype(BF16)
    sh_pa = _t(w_pa).astype(BF16)
    sh_pb = _t(w_pb).astype(BF16)
    sh_out = w_out.astype(BF16)
    small = jnp.concatenate([jnp.reshape(t, (-1, 128)) for t in
                             (meta_tokens, norm_gains, gla_w2, gla_b2, b_merge)], axis=0)
    small = jnp.pad(small, ((0, 2), (0, 0)))
    def shards(l):
        return [sh_g[l], sh_u[l], sh_d[l], sh_in[l], sh_pa[l], sh_pb[l], sh_out[l]]

    w_leads = [1, 1, 1, 0, 0, 0, 0]
    *g0, g_small = _all_gather("gather_layer0", shards(0) + [small], w_leads + [0])
    started = {1: _gather_start("gather_start_1", shards(1), w_leads, g_small)}
    meta_full = jnp.reshape(jnp.transpose(g_small[:, 0:16], (1, 0, 2)), (NMETA, D)) + started[1][4][0, 0]
    gains_full = jnp.reshape(jnp.transpose(jnp.reshape(g_small[:, 16:28], (NDEV, DEPTH, 3, 128)), (1, 2, 0, 3)), (DEPTH, 3, D))
    w2_full = jnp.reshape(jnp.transpose(jnp.reshape(g_small[:, 28:60], (NDEV, DEPTH, 2, GLA_RANK, 32)), (1, 2, 3, 0, 4)),
                          (DEPTH, 2, GLA_RANK, 256))
    b2_full = jnp.reshape(jnp.transpose(jnp.reshape(g_small[:, 60:62], (NDEV, DEPTH, 2, 32)), (1, 2, 0, 3)), (DEPTH, 2, 256))
    bm_full = jnp.reshape(jnp.transpose(jnp.reshape(g_small[:, 62:70], (NDEV, DEPTH, 2, 128)), (1, 2, 0, 3)), (DEPTH, 2, D))

    def layer_weights(l, gathered, gains_l):
        return _layer_weights(*gathered, gains_l, w2_full[l], b2_full[l], bm_full[l], gla_gn[l], q_norm[l], k_norm[l])

    xl = x[0]
    lp = xl.shape[0] + NULL + NMETA
    tabs = _rope_tables(lp)
    h = jnp.concatenate([jnp.zeros((NULL, D), F32), meta_full, xl], axis=0)
    weights, saved = [], []
    z = _rmsnorm_fwd(h, jnp.reshape(gains_full[0, 0], (1, D)))
    for l in range(DEPTH):
        tok = jnp.zeros((), F32)
        if 1 <= l < DEPTH - 1:
            started[l + 1] = _gather_start(f"gather_start_{l + 1}", shards(l + 1), w_leads, h)
            tok = started[l + 1][4][0, 0]
        gathered = g0 if l == 0 else _gather_wait(f"gather_wait_{l}", started[l], w_leads, h)
        weights.append(layer_weights(l, gathered, gains_full[l] + tok))
        next_gain = jnp.reshape(gains_full[l + 1, 0], (1, D)) if l + 1 < DEPTH else None
        h, z, sv = _layer_fwd(h, z, weights[l], tabs, next_gain)
        saved.append(sv)
    loss, dh, dhb, d_final = _loss_head(h, loss_target[0], jnp.reshape(final_norm, (1, D)))
    loss = lax.psum(loss[0, 0], ("x", "y", "c"))

    grads, scattering = [None] * DEPTH, {}
    tok = jnp.zeros((), F32)
    for l in reversed(range(DEPTH)):
        w = dict(weights[l], gains=weights[l]["gains"] + tok)
        if l > 0:
            dh, dhb, grads[l] = _layer_bwd(dh, dhb, saved[l], w, tabs)
            scattering[l] = _scatter_start(f"scatter_start_{l}", _big_grads(grads[l]), dhb)
            tok = scattering[l][4][0, 0]
        else:
            dh, dhb, gm = _layer_bwd_upper(dh, dhb, saved[l], w, tabs)
            ups = _upper_grads(gm)
            pair = _pair_sum("rs_pair_up", ups, _exchange_sibling("rs_sibling_up", ups))
            scattering[l] = _chips_start(f"scatter_start_{l}", pair, dhb)
            dhb = dhb + scattering[l][4][0, 0].astype(BF16)
            dh, dhb, grads[l] = _layer_bwd_lower(dh, dhb, saved[l], w, gm)
    grad_x = dh[NULL + NMETA:][None]
    t_lower, t_upper = [True, True, False], [True, True, False, True, True, True, False]
    red = [None] * DEPTH
    for l in reversed(range(1, DEPTH)):
        red[l] = _scatter_wait(f"scatter_wait_{l}", scattering[l], dhb, t_lower + t_upper)
    red_lower = _reduce_scatter("0", _lower_grads(grads[0]), t_lower)
    pair, recv = _chips_wait("scatter_wait_0", scattering[0], red_lower[0])
    red[0] = red_lower + _final_sum("rs_sum_up", pair, recv, t_upper)
    g_gate = jnp.stack([jnp.stack([red[l][0], red[l][3]]) for l in range(DEPTH)])
    g_up = jnp.stack([jnp.stack([red[l][1], red[l][4]]) for l in range(DEPTH)])
    g_down = jnp.stack([jnp.stack([red[l][2], red[l][5]]) for l in range(DEPTH)])
    g_win = jnp.stack([red[l][6] for l in range(DEPTH)])
    g_wpa = jnp.stack([red[l][7] for l in range(DEPTH)])
    g_wpb = jnp.stack([red[l][8] for l in range(DEPTH)])
    g_wout = jnp.stack([red[l][9] for l in range(DEPTH)])

    d_meta = dh[NULL:NULL + NMETA]
    d_gains = jnp.stack([jnp.concatenate([grads[l]["gain0"], grads[l]["gain"], grads[l]["gain2"]], axis=0)
                         for l in range(DEPTH)])
    d_w2 = jnp.stack([jnp.reshape(jnp.reshape(grads[l]["w2p"], (2, HP, GLA_H, HP))[:, :GLA_RANK, :, :GLA_DK],
                                  (2, GLA_RANK, 256)) for l in range(DEPTH)])
    d_b2 = jnp.stack([jnp.reshape(jnp.reshape(grads[l]["b2p"], (2, GLA_H, HP))[:, :, :GLA_DK], (2, 256))
                      for l in range(DEPTH)])
    d_gn = jnp.stack([grads[l]["gn"][0] for l in range(DEPTH)])
    d_gq = jnp.stack([grads[l]["gq"][0, :HEAD_DIM] for l in range(DEPTH)])
    d_gk = jnp.stack([grads[l]["gk"][0, :HEAD_DIM] for l in range(DEPTH)])
    d_bm = jnp.stack([jnp.concatenate([grads[l]["bma"], grads[l]["bmb"]], axis=0) for l in range(DEPTH)])
    parts = [d_meta, d_gains, d_w2, d_b2, d_gn, d_gq, d_gk, d_bm, d_final[0]]
    sizes = [p.size for p in parts]
    flat = jnp.concatenate([jnp.reshape(p, (-1,)) for p in parts])
    flat = jnp.reshape(flat, (-1, 128))
    nrow = flat.shape[0]
    flat = jnp.pad(flat, ((0, (-nrow) % 8), (0, 0)))
    (g_flat,) = _all_gather("gather_small_grads", [flat], [0])
    tot = jnp.reshape(_sum_gathered(g_flat), (-1,))
    full, o = [], 0
    for p, s in zip(parts, sizes):
        full.append(jnp.reshape(tot[o:o + s], p.shape))
        o += s
    f_meta, f_gains, f_w2, f_b2, f_gn, f_gq, f_gk, f_bm, f_final = full

    def mine(t, width):
        return lax.dynamic_slice_in_dim(t, dev * width, width, axis=t.ndim - 1)

    g_small = dict(meta_tokens=mine(f_meta, 128), norm_gains=mine(f_gains, 128), gla_w2=mine(f_w2, 32),
                   gla_b2=mine(f_b2, 32), gla_gn=f_gn, q_norm=f_gq, k_norm=f_gk, b_merge=mine(f_bm, 128),
                   final_norm=f_final)
    gr = dict(g_small, ffn_w_gate=g_gate, ffn_w_up=g_up, ffn_w_down=g_down, w_in=g_win, w_pa=g_wpa, w_pb=g_wpb,
              w_out=g_wout)
    ws = dict(meta_tokens=meta_tokens, norm_gains=norm_gains, ffn_w_gate=ffn_w_gate, ffn_w_up=ffn_w_up,
              ffn_w_down=ffn_w_down, w_in=w_in, gla_w2=gla_w2, gla_b2=gla_b2, gla_gn=gla_gn, q_norm=q_norm,
              k_norm=k_norm, w_pa=w_pa, w_pb=w_pb, b_merge=b_merge, w_out=w_out, final_norm=final_norm)
    ms = dict(meta_tokens=m_meta_tokens, norm_gains=m_norm_gains, ffn_w_gate=m_ffn_w_gate, ffn_w_up=m_ffn_w_up,
              ffn_w_down=m_ffn_w_down, w_in=m_w_in, gla_w2=m_gla_w2, gla_b2=m_gla_b2, gla_gn=m_gla_gn, q_norm=m_q_norm,
              k_norm=m_k_norm, w_pa=m_w_pa, w_pb=m_w_pb, b_merge=m_b_merge, w_out=m_w_out, final_norm=m_final_norm)
    vs = dict(meta_tokens=v_meta_tokens, norm_gains=v_norm_gains, ffn_w_gate=v_ffn_w_gate, ffn_w_up=v_ffn_w_up,
              ffn_w_down=v_ffn_w_down, w_in=v_w_in, gla_w2=v_gla_w2, gla_b2=v_gla_b2, gla_gn=v_gla_gn, q_norm=v_q_norm,
              k_norm=v_k_norm, w_pa=v_w_pa, w_pb=v_w_pb, b_merge=v_b_merge, w_out=v_w_out, final_norm=v_final_norm)
    names = ["meta_tokens", "norm_gains", "ffn_w_gate", "ffn_w_up", "ffn_w_down", "w_in", "gla_w2", "gla_b2", "gla_gn",
             "q_norm", "k_norm", "w_pa", "w_pb", "b_merge", "w_out", "final_norm"]
    deltas, new_m, new_v = [], [], []
    for n in names:
        dlt, nm, nv = _adamw(ws[n], gr[n], ms[n], vs[n])
        deltas.append(dlt)
        new_m.append(nm)
        new_v.append(nv)
    return (loss, grad_x, *[gr[n] for n in names], *deltas, *new_m, *new_v)
```

```python
import functools
import math

import jax
import jax.numpy as jnp
import numpy as np
from jax import lax
from jax.experimental import pallas as pl
from jax.experimental.pallas import tpu as pltpu

F32 = jnp.float32
BF16 = jnp.bfloat16
MESH = pl.DeviceIdType.MESH
ANY = pl.BlockSpec(memory_space=pl.ANY)

NDEV = 8
D = 1024
DFF = 2816
DEPTH = 4
NMETA = 16
NULL = 112
GRID_W = 64
EPS = 1e-6
HP = 128
GLA_H = 4
GLA_DK = 64
GLA_RANK = 16
GLA_TAU = 16.0
CHUNK = 64
ATT_H = 8
ATT_KV = 2
ATT_G = ATT_H // ATT_KV
HEAD_DIM = 64
ROPE_THETA = 10000.0

IN_SIZES = (256, 256, 512, 512, 16, 16, 512, 128, 128, 1024, 1024)
IN_NAMES = ("qa", "ka", "va", "ra", "lrf", "lrb", "qb", "kb", "vb", "ga", "gb")
D_IN = sum(IN_SIZES)
P_ORDER = ("qb", "ga", "gb", "qa", "ka", "va", "ra", "kb", "vb", "lrf", "lrb")
P_WIDTH = dict(qb=1024, ga=1024, gb=1024, qa=512, ka=512, va=512, ra=512, kb=256, vb=256, lrf=128, lrb=128)
P_OFF = {}
_o = 0
for _n in P_ORDER:
    P_OFF[_n] = _o
    _o += P_WIDTH[_n]
D_INP = _o
P_HEADS = dict(qa=(4, 64), ka=(4, 64), qb=(8, 64), kb=(2, 64), vb=(2, 64), lrf=(1, 16), lrb=(1, 16))

ADAM_LR = 0.001
ADAM_B1 = 0.9
ADAM_B2 = 0.999
ADAM_EPS = 1e-08
ADAM_WD = 0.01
ADAM_STEP = 10

VMEM_BIG = 56 * 1024 * 1024
MXU_N = 256


def _cparams(vmem=None):
    return pltpu.CompilerParams(vmem_limit_bytes=vmem) if vmem else pltpu.CompilerParams()


def _pick(n, prefs):
    for p in prefs:
        if n % p == 0:
            return p
    return n


def _tm(lp):
    return _pick(lp, (528, 512, 256, 128))


_DN = {"nn": (((1,), (0,)), ((), ())), "nt": (((1,), (1,)), ((), ())), "tn": (((0,), (0,)), ((), ()))}


def _dot(a, b, mode="nn", precision=None):
    return lax.dot_general(a, b, _DN[mode], preferred_element_type=F32, precision=precision)


def _split(x):
    hi = x.astype(BF16)
    return hi, (x - hi.astype(F32)).astype(BF16)


def _dot_sel(t, x, mode="nn"):
    hi, lo = _split(x)
    return _dot(t, hi, mode) + _dot(t, lo, mode)


def _dot3(a, b, mode="nn"):
    ah, al = _split(a)
    bh, bl = _split(b)
    return _dot(ah, bh, mode) + (_dot(ah, bl, mode) + _dot(al, bh, mode))


def _sigmoid(x):
    return 0.5 * jnp.tanh(0.5 * x) + 0.5


def _mm(name, m, n, terms, outs, epilogue, extras=(), *, tm, tn, nk=1, nsub=1, i_outer=False, vmem=None):
    gm, gn = m // tm, n // tn
    assert gm * tm == m and gn * tn == n, (name, m, n, tm, tn)
    n_acc = 1 + max(t[3] for t in terms)

    def gmap(f):
        if i_outer:
            return lambda i, j, kk: f(i, j, kk)
        return lambda j, i, kk: f(i, j, kk)

    in_specs, args = [], []
    for a, b, mode, _, pa, pb in terms:
        kdim = a.shape[-2] if mode == "tn" else a.shape[-1]
        tk = kdim // nk
        assert tk * nk == kdim
        na, nb = (None,) * len(pa), (None,) * len(pb)
        if mode == "tn":
            in_specs.append(pl.BlockSpec(na + (tk, tm), gmap(lambda i, j, kk, pa=pa: pa + (kk, i))))
        else:
            in_specs.append(pl.BlockSpec(na + (tm, tk), gmap(lambda i, j, kk, pa=pa: pa + (i, kk))))
        if mode == "nt":
            in_specs.append(pl.BlockSpec(nb + (tn, tk), gmap(lambda i, j, kk, pb=pb: pb + (j, kk))))
        else:
            in_specs.append(pl.BlockSpec(nb + (tk, tn), gmap(lambda i, j, kk, pb=pb: pb + (kk, j))))
        args += [a, b]
    for arr, kind, off, pe in extras:
        ne = (None,) * len(pe)
        if kind == "mn":
            in_specs.append(pl.BlockSpec(ne + (tm, tn), gmap(lambda i, j, kk, off=off, pe=pe: pe + (i, j + off))))
        else:
            in_specs.append(pl.BlockSpec(ne + (1, tn), gmap(lambda i, j, kk, off=off, pe=pe: pe + (0, j + off))))
        args.append(arr)
    out_shape, out_specs = [], []
    for shape, dtype, kind, off, po in outs:
        no = (None,) * len(po)
        out_shape.append(jax.ShapeDtypeStruct(shape, dtype))
        if kind == "mn":
            out_specs.append(pl.BlockSpec(no + (tm, tn), gmap(lambda i, j, kk, off=off, po=po: po + (i, j + off))))
        else:
            assert not i_outer
            out_specs.append(pl.BlockSpec(no + (1, tn), gmap(lambda i, j, kk, off=off, po=po: po + (0, j + off))))
    n_t, n_e, n_o = len(terms), len(extras), len(outs)
    i_axis = 0 if i_outer else 1

    def body(*refs):
        ins = refs[: 2 * n_t]
        exs = refs[2 * n_t: 2 * n_t + n_e]
        ors = refs[2 * n_t + n_e: 2 * n_t + n_e + n_o]
        accs = refs[2 * n_t + n_e + n_o:]
        i = pl.program_id(i_axis)
        kk = pl.program_id(2)

        def partials(cs):
            part = [None] * n_acc
            for t, (_, _, mode, ai, _, _) in enumerate(terms):
                b_ref = ins[2 * t + 1]
                b_val = b_ref[cs, :] if mode == "nt" else b_ref[:, cs]
                p = _dot(ins[2 * t][...], b_val, mode)
                part[ai] = p if part[ai] is None else part[ai] + p
            return part

        def finish(vals, cs):
            res = epilogue(vals, [e[:, cs] for e in exs], i * tm)
            for (_, dtype, kind, _, _), o_ref, v in zip(outs, ors, res):
                if kind == "mn":
                    o_ref[:, cs] = v.astype(dtype)
                else:
                    @pl.when(i == 0)
                    def _():
                        o_ref[:, cs] = v.astype(dtype)

                    @pl.when(i != 0)
                    def _():
                        o_ref[:, cs] += v.astype(dtype)

        if nk == 1:
            w = tn // nsub
            for s in range(nsub):
                cs = slice(s * w, (s + 1) * w)
                finish(partials(cs), cs)
        else:
            part = partials(slice(None))
            @pl.when(kk == 0)
            def _():
                for a_ref, p in zip(accs, part):
                    a_ref[...] = p

            @pl.when(kk != 0)
            def _():
                for a_ref, p in zip(accs, part):
                    a_ref[...] += p

            @pl.when(kk == nk - 1)
            def _():
                finish([a_ref[...] for a_ref in accs], slice(None))

    scratch = [pltpu.VMEM((tm, tn), F32) for _ in range(n_acc)] if nk > 1 else []
    grid = (gm, gn, nk) if i_outer else (gn, gm, nk)
    res = pl.pallas_call(
        body, name=name, grid=grid, in_specs=in_specs, out_specs=out_specs, out_shape=out_shape,
        scratch_shapes=scratch, compiler_params=_cparams(vmem),
    )(*args)
    return res


def _term(a, b, mode, acc=0, pa=(), pb=()):
    return (a, b, mode, acc, tuple(pa), tuple(pb))


def _row_tile(lp):
    return _pick(lp, (384, 256, 128))


def _rmsnorm_fwd(h, gain):
    lp = h.shape[0]
    tr = _row_tile(lp)

    def body(h_ref, g_ref, z_ref):
        x = h_ref[...]
        r = lax.rsqrt(jnp.mean(x * x, axis=-1, keepdims=True) + EPS)
        z_ref[...] = (x * r * g_ref[...]).astype(BF16)

    return pl.pallas_call(
        body, name="rmsnorm_fwd", grid=(lp // tr,),
        in_specs=[pl.BlockSpec((tr, D), lambda i: (i, 0)), pl.BlockSpec((1, D), lambda i: (0, 0))],
        out_specs=pl.BlockSpec((tr, D), lambda i: (i, 0)),
        out_shape=jax.ShapeDtypeStruct((lp, D), BF16),
    )(h, gain)


def _loss_head(h, target, gain):
    lp = h.shape[0]
    tr = 128

    def body(h_ref, t_ref, g_ref, loss_ref, dh_ref, dhb_ref, dg_ref):
        i = pl.program_id(0)

        @pl.when(i == 0)
        def _():
            loss_ref[...] = jnp.zeros_like(loss_ref)
            dg_ref[...] = jnp.zeros_like(dg_ref)
            dh_ref[...] = jnp.zeros_like(dh_ref)
            dhb_ref[...] = jnp.zeros_like(dhb_ref)

        @pl.when(i != 0)
        def _():
            x = h_ref[...]
            g = g_ref[...]
            r = lax.rsqrt(jnp.mean(x * x, axis=-1, keepdims=True) + EPS)
            xh = x * r
            y = xh * g
            err = y - t_ref[...]
            loss_ref[...] += 0.5 * jnp.sum(jnp.sum(err * err, axis=-1, keepdims=True), axis=0, keepdims=True) / D
            dy = err * (1.0 / D)
            dg_ref[...] += jnp.sum(dy * xh, axis=0, keepdims=True)
            dxh = dy * g
            dx = r * (dxh - xh * jnp.mean(dxh * xh, axis=-1, keepdims=True))
            dh_ref[...] = dx
            dhb_ref[...] = dx.astype(BF16)

    row = pl.BlockSpec((tr, D), lambda i: (i, 0))
    vec = pl.BlockSpec((1, D), lambda i: (0, 0))
    return pl.pallas_call(
        body, name="loss_head", grid=(lp // tr,),
        in_specs=[row, pl.BlockSpec((tr, D), lambda i: (jnp.maximum(i - 1, 0), 0)), vec],
        out_specs=[pl.BlockSpec((1, 1), lambda i: (0, 0)), row, row, vec],
        out_shape=[jax.ShapeDtypeStruct((1, 1), F32), jax.ShapeDtypeStruct((lp, D), F32),
                   jax.ShapeDtypeStruct((lp, D), BF16), jax.ShapeDtypeStruct((1, D), F32)],
    )(h, target, gain)


def _silu_parts(g):
    s = _sigmoid(g)
    return g * s, s * (1.0 + g * (1.0 - s))


def _residual_norm_epi(scale, with_norm):
    def epi(accs, exs, row0):
        h2 = exs[0] + scale * accs[0]
        if not with_norm:
            return [h2]
        r = lax.rsqrt(jnp.mean(h2 * h2, axis=-1, keepdims=True) + EPS)
        return [h2, h2 * r * exs[1]]
    return epi


def _norm_bwd_epi(accs, exs, row0):
    dz = accs[0]
    x, res, g = exs
    r = lax.rsqrt(jnp.mean(x * x, axis=-1, keepdims=True) + EPS)
    xh = x * r
    dxh = dz * g
    dx = r * (dxh - xh * jnp.mean(dxh * xh, axis=-1, keepdims=True))
    rows = row0 + lax.broadcasted_iota(jnp.int32, (dz.shape[0], 1), 0)
    dh = jnp.where(rows >= NULL, res + dx, 0.0)
    return [dh, dh, jnp.sum(dz * xh, axis=0, keepdims=True)]


def _norm_bwd_outs(lp):
    return [((lp, D), F32, "mn", 0, ()), ((lp, D), BF16, "mn", 0, ()), ((1, D), F32, "nsum", 0, ())]


def _ffn_fwd(h, z, wg_t, wu_t, wd, pre, next_gain):
    lp = h.shape[0]
    tm = _tm(lp)

    def up_epi(accs, exs, row0):
        g, u = accs
        sg, _ = _silu_parts(g)
        return [g, u, sg * u]

    bshape = (lp, DFF)
    g_, u_, act = _mm("ffn_up", lp, DFF, [_term(z, wg_t, "nt", 0, (), pre), _term(z, wu_t, "nt", 1, (), pre)],
                      [(bshape, BF16, "mn", 0, ())] * 3, up_epi, tm=tm, tn=DFF, nsub=DFF // MXU_N, vmem=VMEM_BIG)

    with_norm = next_gain is not None
    res = _mm("ffn_down", lp, D, [_term(act, wd, "nn", 0, (), pre)],
              [((lp, D), F32, "mn", 0, ())] + ([((lp, D), BF16, "mn", 0, ())] if with_norm else []),
              _residual_norm_epi(0.5, with_norm),
              extras=[(h, "mn", 0, ())] + ([(next_gain, "n", 0, ())] if with_norm else []),
              tm=tm, tn=D, i_outer=True, vmem=VMEM_BIG)
    return res[0], (res[1] if with_norm else None), dict(h=h, z=z, g=g_, u=u_, act=act)


def _dw(name, a, b, m, n, scale=1.0):
    lp = a.shape[0]
    tm = _pick(m, (2944, 1408, 1024, 512, 256, 128))
    tn = _pick(n, (1024, 512, 256, 128))
    nk = lp // _pick(lp, (2112, 256, 128) if tm <= 1408 else (1056, 256, 128))

    def epi(accs, exs, row0):
        return [accs[0] * scale]

    (w,) = _mm(name, m, n, [_term(a, b, "tn")], [((m, n), BF16, "mn", 0, ())], epi, tm=tm, tn=tn, nk=nk,
               i_outer=True, vmem=VMEM_BIG)
    return w


def _ffn_bwd(dh, dhb, sv, gain, wg_t, wu_t, wd, pre):
    lp = dh.shape[0]
    tm = _tm(lp)

    def dact_epi(accs, exs, row0):
        g = exs[0].astype(F32)
        u = exs[1].astype(F32)
        da = 0.5 * accs[0]
        sg, dsg = _silu_parts(g)
        return [da * u * dsg, da * sg]

    dg_, du_ = _mm("ffn_dact", lp, DFF, [_term(dhb, wd, "nt", 0, (), pre)],
                   [((lp, DFF), BF16, "mn", 0, ())] * 2, dact_epi,
                   extras=[(sv["g"], "mn", 0, ()), (sv["u"], "mn", 0, ())], tm=tm, tn=DFF, nsub=DFF // MXU_N,
                   vmem=VMEM_BIG)
    d_wd = _dw("dw_down", sv["act"], dhb, DFF, D, 0.5)
    d_wg = _dw("dw_gate", dg_, sv["z"], DFF, D)
    d_wu = _dw("dw_up", du_, sv["z"], DFF, D)

    nk = 1
    dh2, dhb2, dgain = _mm("ffn_dz", lp, D, [_term(dg_, wg_t, "nn", 0, (), pre), _term(du_, wu_t, "nn", 0, (), pre)],
                           _norm_bwd_outs(lp), _norm_bwd_epi,
                           extras=[(sv["h"], "mn", 0, ()), (dh, "mn", 0, ()), (gain, "n", 0, ())],
                           tm=tm, tn=D, nk=nk, vmem=VMEM_BIG)
    return dh2, dhb2, dgain, d_wg, d_wu, d_wd


def _gla_gates(hin, w2p, b2p):
    lp = hin.shape[0]
    tr = _row_tile(lp)
    bf, bb = P_OFF["lrf"] // HP, P_OFF["lrb"] // HP

    def body(lf_ref, lb_ref, w_ref, b_ref, o_ref, c_ref):
        i = pl.program_id(0)
        rows = i * tr + lax.broadcasted_iota(jnp.int32, (tr, 1), 0)
        r = lax.broadcasted_iota(jnp.int32, (tr, tr), 0)
        c = lax.broadcasted_iota(jnp.int32, (tr, tr), 1)
        same = (r // CHUNK) == (c // CHUNK)
        for d, l_ref in enumerate((lf_ref, lb_ref)):
            logit = _dot3(l_ref[...], w_ref[d]) + b_ref[d]
            g = jnp.where(rows >= NULL, jax.nn.log_sigmoid(logit) * (1.0 / GLA_TAU), 0.0)
            o_ref[d] = g
            tmat = jnp.where(same & ((r >= c) if d == 0 else (r <= c)), 1.0, 0.0).astype(BF16)
            c_ref[d] = _dot_sel(tmat, g)

    spec = pl.BlockSpec((2, tr, 512), lambda i: (0, i, 0))
    return pl.pallas_call(
        body, name="gla_gates", grid=(lp // tr,),
        in_specs=[pl.BlockSpec((tr, HP), lambda i: (i, bf)), pl.BlockSpec((tr, HP), lambda i: (i, bb)),
                  pl.BlockSpec((2, HP, 512), lambda i: (0, 0, 0)), pl.BlockSpec((2, 1, 512), lambda i: (0, 0, 0))],
        out_specs=[spec, spec],
        out_shape=[jax.ShapeDtypeStruct((2, lp, 512), F32)] * 2,
    )(hin, hin, w2p, b2p)


def _gla_rows(lp):
    return _pick(lp, (384, 256, 128))


def _tri(d):
    r = lax.broadcasted_iota(jnp.int32, (CHUNK, CHUNK), 0)
    c = lax.broadcasted_iota(jnp.int32, (CHUNK, CHUNK), 1)
    return (r >= c) if d == 0 else (r <= c)


def _gla_fwd(hin, gates):
    lp = hin.shape[0]
    rb = _gla_rows(lp)
    nb = lp // rb
    cpb = rb // CHUNK
    nchunk = lp // CHUNK
    qo, ko, vo = P_OFF["qa"] // 512, P_OFF["ka"] // 512, P_OFF["va"] // 512
    scale = GLA_DK ** -0.5

    def body(qf, kf, vf, gf, qb, kb, vb_, gb, of, ob, sf, sb, st):
        @pl.when(pl.program_id(0) == 0)
        def _():
            st[...] = jnp.zeros_like(st)

        ins = ((qf, kf, vf, gf, of, sf), (qb, kb, vb_, gb, ob, sb))
        for ci in range(cpb):
            for d in range(2):
                q_ref, k_ref, v_ref, g_ref, o_ref, s_ref = ins[d]
                tri = _tri(d)
                c = ci if d == 0 else cpb - 1 - ci
                rows = slice(c * CHUNK, (c + 1) * CHUNK)
                for h in range(GLA_H):
                    sl = slice(h * HP, (h + 1) * HP)
                    q = q_ref[rows, sl] * scale
                    k = k_ref[rows, sl]
                    v = v_ref[rows, sl]
                    b = g_ref[rows, sl]
                    btot = b[CHUNK - 1:CHUNK] if d == 0 else b[0:1]
                    qd = (q * jnp.exp(b)).astype(BF16)
                    ki = (k * jnp.exp(-b)).astype(BF16)
                    ke = (k * jnp.exp(btot - b)).astype(BF16)
                    vb = v.astype(BF16)
                    att = jnp.where(tri, _dot(qd, ki, "nt"), 0.0)
                    s_prev = st[d, h]
                    o_ref[rows, sl] = _dot(att.astype(BF16), vb) + _dot(qd, s_prev.astype(BF16), "nt")
                    s_ref[h, c] = s_prev
                    st[d, h] = s_prev * jnp.exp(btot) + _dot(vb, ke, "tn")

    def specs(off):
        return (pl.BlockSpec((rb, 512), lambda b: (b, off)), pl.BlockSpec((rb, 512), lambda b: (nb - 1 - b, off)))

    (qf, qb), (kf, kb), (vf, vb2) = specs(qo), specs(ko), specs(vo)
    gf = pl.BlockSpec((None, rb, 512), lambda b: (0, b, 0))
    gb = pl.BlockSpec((None, rb, 512), lambda b: (1, nb - 1 - b, 0))
    of, ob = specs(0)
    sf = pl.BlockSpec((GLA_H, cpb, HP, HP), lambda b: (0, b, 0, 0))
    sb = pl.BlockSpec((GLA_H, cpb, HP, HP), lambda b: (0, nb - 1 - b, 0, 0))
    osh = jax.ShapeDtypeStruct((lp, GLA_H * HP), F32)
    ssh = jax.ShapeDtypeStruct((GLA_H, nchunk, HP, HP), F32)
    return pl.pallas_call(
        body, name="gla_fwd", grid=(nb,),
        in_specs=[qf, kf, vf, gf, qb, kb, vb2, gb], out_specs=[of, ob, sf, sb], out_shape=[osh, osh, ssh, ssh],
        scratch_shapes=[pltpu.VMEM((2, GLA_H, HP, HP), F32)], compiler_params=_cparams(VMEM_BIG),
    )(hin, hin, hin, gates, hin, hin, hin, gates)


def _gla_bwd(hin, gates, states, do):
    lp = hin.shape[0]
    rb = _gla_rows(lp)
    nb = lp // rb
    cpb = rb // CHUNK
    qo, ko, vo = P_OFF["qa"] // 512, P_OFF["ka"] // 512, P_OFF["va"] // 512
    scale = GLA_DK ** -0.5

    def body(qf, kf, vf, gf, sf, dof, qb, kb, vb_, gb, sb, dob,
             dqf, dkf, dvf, dgf, dqb, dkb, dvb, dgb, dst):
        @pl.when(pl.program_id(0) == 0)
        def _():
            dst[...] = jnp.zeros_like(dst)

        ins = ((qf, kf, vf, gf, sf, dof, dqf, dkf, dvf, dgf), (qb, kb, vb_, gb, sb, dob, dqb, dkb, dvb, dgb))
        for ci in range(cpb):
            for d in range(2):
                q_ref, k_ref, v_ref, g_ref, s_ref, do_ref, dq_ref, dk_ref, dv_ref, dg_ref = ins[d]
                tri, tri_t = _tri(d), _tri(1 - d)
                edge = lax.broadcasted_iota(jnp.int32, (CHUNK, 1), 0) == (CHUNK - 1 if d == 0 else 0)
                c = cpb - 1 - ci if d == 0 else ci
                rows = slice(c * CHUNK, (c + 1) * CHUNK)
                for h in range(GLA_H):
                    sl = slice(h * HP, (h + 1) * HP)
                    q = q_ref[rows, sl] * scale
                    k = k_ref[rows, sl]
                    v = v_ref[rows, sl]
                    dout = do_ref[rows, sl].astype(BF16)
                    b = g_ref[rows, sl]
                    btot = b[CHUNK - 1:CHUNK] if d == 0 else b[0:1]
                    e = jnp.exp(b)
                    ei = jnp.exp(-b)
                    et = jnp.exp(btot - b)
                    etot = jnp.exp(btot)
                    qd = q * e
                    ki = k * ei
                    ke = k * et
                    qdb, kib, keb, vb = qd.astype(BF16), ki.astype(BF16), ke.astype(BF16), v.astype(BF16)
                    att_t = jnp.where(tri_t, _dot(kib, qdb, "nt"), 0.0).astype(BF16)
                    d_att = jnp.where(tri, _dot(dout, vb, "nt"), 0.0).astype(BF16)
                    d_att_t = jnp.where(tri_t, _dot(vb, dout, "nt"), 0.0).astype(BF16)
                    s_prev = s_ref[h, c]
                    ds_t = dst[d, h]
                    ds_b = ds_t.astype(BF16)
                    dv = _dot(att_t, dout) + _dot(keb, ds_b, "nt")
                    d_qd = _dot(d_att, kib) + _dot(dout, s_prev.astype(BF16))
                    d_ki = _dot(d_att_t, qdb)
                    d_ke = _dot(vb, ds_b)
                    d_e = jnp.sum(s_prev * ds_t, axis=0, keepdims=True)
                    dst[d, h] = _dot(dout, qdb, "tn") + ds_t * etot
                    db = d_qd * qd - d_ki * ki - d_ke * ke
                    dbtot = jnp.sum(d_ke * ke, axis=0, keepdims=True) + d_e * etot
                    dq_ref[rows, sl] = d_qd * e * scale
                    dk_ref[rows, sl] = d_ki * ei + d_ke * et
                    dv_ref[rows, sl] = dv
                    dg_ref[rows, sl] = db + jnp.where(edge, dbtot, 0.0)

    def fw(off):
        return pl.BlockSpec((rb, 512), lambda b: (nb - 1 - b, off))

    def bw(off):
        return pl.BlockSpec((rb, 512), lambda b: (b, off))

    gf = pl.BlockSpec((None, rb, 512), lambda b: (0, nb - 1 - b, 0))
    gb = pl.BlockSpec((None, rb, 512), lambda b: (1, b, 0))
    sf = pl.BlockSpec((GLA_H, cpb, HP, HP), lambda b: (0, nb - 1 - b, 0, 0))
    sb = pl.BlockSpec((GLA_H, cpb, HP, HP), lambda b: (0, b, 0, 0))
    osh = jax.ShapeDtypeStruct((lp, GLA_H * HP), F32)
    res = pl.pallas_call(
        body, name="gla_bwd", grid=(nb,),
        in_specs=[fw(qo), fw(ko), fw(vo), gf, sf, fw(0), bw(qo), bw(ko), bw(vo), gb, sb, bw(0)],
        out_specs=[fw(0)] * 4 + [bw(0)] * 4, out_shape=[osh] * 8,
        scratch_shapes=[pltpu.VMEM((2, GLA_H, HP, HP), F32)], compiler_params=_cparams(VMEM_BIG),
    )(hin, hin, hin, gates, states[0], do, hin, hin, hin, gates, states[1], do)
    return res[:4], res[4:]


def _gla_out_fwd(o2, hin, gn):
    lp = hin.shape[0]
    tr = _row_tile(lp)
    ro = P_OFF["ra"] // 512

    def body(of_ref, ob_ref, r_ref, gn_ref, a_ref):
        r = r_ref[...]
        sr, _ = _silu_parts(r)
        for h in range(GLA_H):
            sl = slice(h * HP, (h + 1) * HP)
            o = of_ref[:, sl] + ob_ref[:, sl]
            rs = lax.rsqrt(jnp.mean(o * o, axis=-1, keepdims=True) + EPS)
            a_ref[:, sl] = (o * rs * gn_ref[:, sl] * sr[:, sl]).astype(BF16)

    row = pl.BlockSpec((tr, 512), lambda i: (i, 0))
    return pl.pallas_call(
        body, name="gla_out_fwd", grid=(lp // tr,),
        in_specs=[row, row, pl.BlockSpec((tr, 512), lambda i: (i, ro)), pl.BlockSpec((1, 512), lambda i: (0, 0))],
        out_specs=row,
        out_shape=jax.ShapeDtypeStruct((lp, 512), BF16),
    )(o2[0], o2[1], hin, gn)


def _gla_out_bwd(da, o2, hin, gn):
    lp = hin.shape[0]
    tr = _row_tile(lp)
    ro = P_OFF["ra"] // 512

    def body(da_ref, of_ref, ob_ref, r_ref, gn_ref, do_ref, dr_ref, dgn_ref):
        i = pl.program_id(0)
        r = r_ref[...]
        sr, dsr = _silu_parts(r)
        da_v = da_ref[...]
        parts = []
        for h in range(GLA_H):
            sl = slice(h * HP, (h + 1) * HP)
            o = of_ref[:, sl] + ob_ref[:, sl]
            rs = lax.rsqrt(jnp.mean(o * o, axis=-1, keepdims=True) + EPS)
            oh = o * rs
            gn_h = gn_ref[:, sl]
            dah = da_v[:, sl]
            dr_ref[:, sl] = (dah * oh * gn_h * dsr[:, sl]).astype(BF16)
            t = dah * sr[:, sl]
            parts.append(jnp.sum(t * oh, axis=0, keepdims=True))
            doh = t * gn_h
            do_ref[:, sl] = rs * (doh - oh * jnp.mean(doh * oh, axis=-1, keepdims=True))
        part = jnp.concatenate(parts, axis=1)

        @pl.when(i == 0)
        def _():
            dgn_ref[...] = part

        @pl.when(i != 0)
        def _():
            dgn_ref[...] += part

    row = pl.BlockSpec((tr, 512), lambda i: (i, 0))
    return pl.pallas_call(
        body, name="gla_out_bwd", grid=(lp // tr,),
        in_specs=[row, row, row, pl.BlockSpec((tr, 512), lambda i: (i, ro)), pl.BlockSpec((1, 512), lambda i: (0, 0))],
        out_specs=[row, row, pl.BlockSpec((1, 512), lambda i: (0, 0))],
        out_shape=[jax.ShapeDtypeStruct((lp, 512), F32), jax.ShapeDtypeStruct((lp, 512), BF16),
                   jax.ShapeDtypeStruct((1, 512), F32)],
    )(da, o2[0], o2[1], hin, gn)


def _gla_in_bwd(gf, gb, gates, hin, w2p, others):
    lp = hin.shape[0]
    tr = _row_tile(lp)
    bf, bb = P_OFF["lrf"] // HP, P_OFF["lrb"] // HP
    names = tuple(others)

    def seg(name):
        return slice(P_OFF[name], P_OFF[name] + P_WIDTH[name])

    def body(dqf_ref, dkf_ref, dvf_ref, dgf_ref, dqb_ref, dkb_ref, dvb_ref, dgb_ref, g_ref, lf_ref, lb_ref, w_ref,
             *rest):
        other_refs, (o_ref, dw_ref, db_ref) = rest[:len(names)], rest[len(names):]
        i = pl.program_id(0)
        for n, ref in zip(names, other_refs):
            o_ref[:, seg(n)] = ref[...].astype(BF16)
        o_ref[:, seg("qa")] = (dqf_ref[...] + dqb_ref[...]).astype(BF16)
        o_ref[:, seg("ka")] = (dkf_ref[...] + dkb_ref[...]).astype(BF16)
        o_ref[:, seg("va")] = (dvf_ref[...] + dvb_ref[...]).astype(BF16)
        olr_ref = o_ref.at[:, P_OFF["lrf"]:P_OFF["lrf"] + 2 * HP]
        rows = i * tr + lax.broadcasted_iota(jnp.int32, (tr, 1), 0)
        r = lax.broadcasted_iota(jnp.int32, (tr, tr), 0)
        c = lax.broadcasted_iota(jnp.int32, (tr, tr), 1)
        same = (r // CHUNK) == (c // CHUNK)
        for d, (l_ref, dg_ref) in enumerate(((lf_ref, dgf_ref), (lb_ref, dgb_ref))):
            tmat = jnp.where(same & ((r <= c) if d == 0 else (r >= c)), 1.0, 0.0).astype(BF16)
            dg = _dot_sel(tmat, dg_ref[...])
            sig_neg = 1.0 - jnp.exp(GLA_TAU * g_ref[d])
            dlogit = jnp.where(rows >= NULL, dg * (1.0 / GLA_TAU) * sig_neg, 0.0)
            olr_ref[:, d * HP:(d + 1) * HP] = _dot3(dlogit, w_ref[d], "nt").astype(BF16)
            dw = _dot3(l_ref[...], dlogit, "tn")
            dbias = jnp.sum(dlogit, axis=0, keepdims=True)

            @pl.when(i == 0)
            def _():
                dw_ref[d] = dw
                db_ref[d] = dbias

            @pl.when(i != 0)
            def _():
                dw_ref[d] += dw
                db_ref[d] += dbias

    two = pl.BlockSpec((2, tr, 512), lambda i: (0, i, 0))
    row = pl.BlockSpec((tr, 512), lambda i: (i, 0))
    return pl.pallas_call(
        body, name="gla_in_bwd", grid=(lp // tr,),
        in_specs=[row] * 8 + [two, pl.BlockSpec((tr, HP), lambda i: (i, bf)),
                  pl.BlockSpec((tr, HP), lambda i: (i, bb)), pl.BlockSpec((2, HP, 512), lambda i: (0, 0, 0))] +
                 [pl.BlockSpec((tr, P_WIDTH[n]), lambda i: (i, 0)) for n in names],
        out_specs=[pl.BlockSpec((tr, D_INP), lambda i: (i, 0)),
                   pl.BlockSpec((2, HP, 512), lambda i: (0, 0, 0)), pl.BlockSpec((2, 1, 512), lambda i: (0, 0, 0))],
        out_shape=[jax.ShapeDtypeStruct((lp, D_INP), BF16), jax.ShapeDtypeStruct((2, HP, 512), F32),
                   jax.ShapeDtypeStruct((2, 1, 512), F32)],
        compiler_params=_cparams(VMEM_BIG),
    )(*gf, *gb, gates, hin, hin, w2p, *[others[n] for n in names])


def _rope_tables(lp):
    n_tok = lp - NULL - NMETA
    rows = n_tok // GRID_W
    row = np.repeat(np.arange(rows), GRID_W).astype(np.float32)
    col = np.tile(np.arange(GRID_W), rows).astype(np.float32)
    inv = (ROPE_THETA ** (-np.arange(0, 32, 2, dtype=np.float32) / 32)).astype(np.float32)
    ang = np.concatenate([row[:, None] * inv, col[:, None] * inv], axis=-1)
    ang = np.concatenate([np.zeros((NULL + NMETA, 32), np.float32), ang], axis=0)
    cos, sin = np.cos(ang).astype(np.float32), np.sin(ang).astype(np.float32)
    z16 = np.zeros((lp, 16), np.float32)
    z64 = np.zeros((lp, 64), np.float32)
    c = np.concatenate([cos[:, :16], cos[:, :16], cos[:, 16:], cos[:, 16:], z64], axis=1)
    a = np.concatenate([-sin[:, :16], z16, -sin[:, 16:], z16, z64], axis=1)
    b = np.concatenate([z16, sin[:, :16], z16, sin[:, 16:], z64], axis=1)
    return jnp.asarray(c), jnp.asarray(a), jnp.asarray(b)


def _rope(x, c, a, b):
    return x * c + pltpu.roll(x, HP - 16, 1) * a + pltpu.roll(x, 16, 1) * b


def _rope_t(dx, c, a, b):
    return dx * c + pltpu.roll(dx * a, 16, 1) + pltpu.roll(dx * b, HP - 16, 1)


def _attn_prep(hin, gq, gk, tabs):
    lp = hin.shape[0]
    tr = _row_tile(lp)
    qo, ko, vo = P_OFF["qb"] // 1024, P_OFF["kb"] // 256, P_OFF["vb"] // 256

    def body(q_ref, k_ref, v_ref, gq_ref, gk_ref, c_ref, a_ref, b_ref, oq_ref, ok_ref, ov_ref):
        c, a, b = c_ref[...], a_ref[...], b_ref[...]
        for src, g_ref, dst, nh, sc in ((q_ref, gq_ref, oq_ref, ATT_H, Q_SCALE), (k_ref, gk_ref, ok_ref, ATT_KV, 1.0)):
            for h in range(nh):
                sl = slice(h * HP, (h + 1) * HP)
                x = src[:, sl]
                r = lax.rsqrt(jnp.sum(x * x, axis=-1, keepdims=True) * (1.0 / HEAD_DIM) + EPS)
                dst[:, sl] = (_rope(x * r * g_ref[...], c, a, b) * sc).astype(BF16)
        lane = lax.broadcasted_iota(jnp.int32, (1, ATT_KV * HP), 1)
        ov_ref[...] = jnp.where(lane % HP == HEAD_DIM, 1.0, v_ref[...]).astype(BF16)

    tab = pl.BlockSpec((tr, HP), lambda i: (i, 0))
    vec = pl.BlockSpec((1, HP), lambda i: (0, 0))
    return pl.pallas_call(
        body, name="attn_prep", grid=(lp // tr,),
        in_specs=[pl.BlockSpec((tr, 1024), lambda i: (i, qo)), pl.BlockSpec((tr, 256), lambda i: (i, ko)),
                  pl.BlockSpec((tr, 256), lambda i: (i, vo)), vec, vec, tab, tab, tab],
        out_specs=[pl.BlockSpec((tr, 1024), lambda i: (i, 0)), pl.BlockSpec((tr, 256), lambda i: (i, 0)),
                   pl.BlockSpec((tr, 256), lambda i: (i, 0))],
        out_shape=[jax.ShapeDtypeStruct((lp, 1024), BF16), jax.ShapeDtypeStruct((lp, 256), BF16),
                   jax.ShapeDtypeStruct((lp, 256), BF16)],
    )(hin, hin, hin, gq, gk, *tabs)


def _attn_prep_bwd(dqr, dkr, hin, gq, gk, tabs):
    lp = hin.shape[0]
    tr = _row_tile(lp)
    qo, ko = P_OFF["qb"] // 1024, P_OFF["kb"] // 256

    def body(dq_ref, dk_ref, q_ref, k_ref, gq_ref, gk_ref, c_ref, a_ref, b_ref, oq_ref, ok_ref, dgq_ref, dgk_ref):
        i = pl.program_id(0)
        c, a, b = c_ref[...], a_ref[...], b_ref[...]
        for src, dsrc, g_ref, dst, dg_ref, nh, sc in (
                (q_ref, dq_ref, gq_ref, oq_ref, dgq_ref, ATT_H, Q_SCALE),
                (k_ref, dk_ref, gk_ref, ok_ref, dgk_ref, ATT_KV, 1.0)):
            acc = jnp.zeros((1, HP), F32)
            for h in range(nh):
                sl = slice(h * HP, (h + 1) * HP)
                x = src[:, sl]
                r = lax.rsqrt(jnp.sum(x * x, axis=-1, keepdims=True) * (1.0 / HEAD_DIM) + EPS)
                xh = x * r
                dxn = _rope_t(dsrc[:, sl] * sc, c, a, b)
                acc = acc + jnp.sum(dxn * xh, axis=0, keepdims=True)
                dxh = dxn * g_ref[...]
                dx = r * (dxh - xh * (jnp.sum(dxh * xh, axis=-1, keepdims=True) * (1.0 / HEAD_DIM)))
                dst[:, sl] = dx.astype(BF16)

            @pl.when(i == 0)
            def _():
                dg_ref[...] = acc

            @pl.when(i != 0)
            def _():
                dg_ref[...] += acc

    tab = pl.BlockSpec((tr, HP), lambda i: (i, 0))
    vec = pl.BlockSpec((1, HP), lambda i: (0, 0))
    return pl.pallas_call(
        body, name="attn_prep_bwd", grid=(lp // tr,),
        in_specs=[pl.BlockSpec((tr, 1024), lambda i: (i, 0)), pl.BlockSpec((tr, 256), lambda i: (i, 0)),
                  pl.BlockSpec((tr, 1024), lambda i: (i, qo)), pl.BlockSpec((tr, 256), lambda i: (i, ko)),
                  vec, vec, tab, tab, tab],
        out_specs=[pl.BlockSpec((tr, 1024), lambda i: (i, 0)), pl.BlockSpec((tr, 256), lambda i: (i, 0)), vec, vec],
        out_shape=[jax.ShapeDtypeStruct((lp, 1024), BF16), jax.ShapeDtypeStruct((lp, 256), BF16),
                   jax.ShapeDtypeStruct((1, HP), F32), jax.ShapeDtypeStruct((1, HP), F32)],
    )(dqr, dkr, hin, hin, gq, gk, *tabs)


QB = 128
GH = 2
Q_SCALE = HEAD_DIM ** -0.5 * math.log2(math.e)
LN2 = math.log(2.0)


def _stack(ref, g0, n):
    return jnp.concatenate([ref[:, (g0 + g) * HP:(g0 + g + 1) * HP] for g in range(n)], axis=0)


def _attn_fwd(qr, kr, vb):
    lp = qr.shape[0]
    nq = lp // QB

    def body(q_ref, k_ref, v_ref, o_ref, lse_ref):
        qb = pl.program_id(1)
        keys = lax.broadcasted_iota(jnp.int32, (1, lp), 1)
        lane = lax.broadcasted_iota(jnp.int32, (1, HP), 1)
        rows = qb * QB + lax.broadcasted_iota(jnp.int32, (QB, 1), 0)
        for ch in range(ATT_G // GH):
            qs = _stack(q_ref, ch * GH, GH)
            s = _dot(qs, k_ref[...], "nt")
            s = jnp.where(keys >= NULL, s, -1e30)
            m = jnp.max(s, axis=-1, keepdims=True)
            p = jnp.exp2(s - m).astype(BF16)
            o_raw = _dot(p, v_ref[...])
            l = jnp.sum(jnp.where(lane == HEAD_DIM, o_raw, 0.0), axis=-1, keepdims=True)
            o = jnp.where(lane < HEAD_DIM, o_raw / l, 0.0)
            lse = m + jnp.log2(l)
            for g in range(GH):
                sl = slice((ch * GH + g) * HP, (ch * GH + g + 1) * HP)
                o_ref[:, sl] = jnp.where(rows >= NULL, o[g * QB:(g + 1) * QB], 0.0).astype(BF16)
                lse_ref[:, sl] = jnp.broadcast_to(lse[g * QB:(g + 1) * QB], (QB, HP))

    qspec = pl.BlockSpec((QB, ATT_G * HP), lambda kv, qb: (qb, kv))
    kspec = pl.BlockSpec((lp, HP), lambda kv, qb: (0, kv))
    return pl.pallas_call(
        body, name="attn_fwd", grid=(ATT_KV, nq),
        in_specs=[qspec, kspec, kspec], out_specs=[qspec, qspec],
        out_shape=[jax.ShapeDtypeStruct((lp, ATT_H * HP), BF16), jax.ShapeDtypeStruct((lp, ATT_H * HP), F32)],
        compiler_params=_cparams(VMEM_BIG),
    )(qr, kr, vb)


def _attn_bwd(qr, kr, vb, o, lse, do):
    lp = qr.shape[0]
    nq = lp // QB

    def body(q_ref, k_ref, v_ref, o_ref, lse_ref, do_ref, dq_ref, dk_ref, dv_ref):
        qb = pl.program_id(1)

        @pl.when(qb == 0)
        def _():
            dk_ref[...] = jnp.zeros_like(dk_ref)
            dv_ref[...] = jnp.zeros_like(dv_ref)

        keys = lax.broadcasted_iota(jnp.int32, (1, lp), 1)
        k = k_ref[...]
        dk_acc, dv_acc = None, None
        for ch in range(ATT_G // GH):
            g0 = ch * GH
            qs = _stack(q_ref, g0, GH)
            dos = _stack(do_ref, g0, GH)
            os_ = _stack(o_ref, g0, GH).astype(F32)
            lse_s = jnp.concatenate([lse_ref[:, (g0 + g) * HP:(g0 + g) * HP + 1] for g in range(GH)], axis=0)
            delta = jnp.sum(dos * os_, axis=-1, keepdims=True) * LN2
            s = _dot(qs, k, "nt")
            p = jnp.where(keys >= NULL, jnp.exp2(s - lse_s), 0.0)
            dob = dos.astype(BF16)
            dp = _dot((dos * LN2).astype(BF16), v_ref[...], "nt")
            ds = (p * (dp - delta)).astype(BF16)
            dq = _dot(ds, k)
            for g in range(GH):
                dq_ref[:, (g0 + g) * HP:(g0 + g + 1) * HP] = dq[g * QB:(g + 1) * QB]
            dv_c = _dot(p.astype(BF16), dob, "tn")
            dk_c = _dot(ds, qs, "tn")
            dv_acc = dv_c if dv_acc is None else dv_acc + dv_c
            dk_acc = dk_c if dk_acc is None else dk_acc + dk_c
        dv_ref[...] += dv_acc
        dk_ref[...] += dk_acc

    qspec = pl.BlockSpec((QB, ATT_G * HP), lambda kv, qb: (qb, kv))
    kspec = pl.BlockSpec((lp, HP), lambda kv, qb: (0, kv))
    return pl.pallas_call(
        body, name="attn_bwd", grid=(ATT_KV, nq),
        in_specs=[qspec, kspec, kspec, qspec, qspec, qspec], out_specs=[qspec, kspec, kspec],
        out_shape=[jax.ShapeDtypeStruct((lp, ATT_H * HP), F32), jax.ShapeDtypeStruct((lp, ATT_KV * HP), F32),
                   jax.ShapeDtypeStruct((lp, ATT_KV * HP), F32)],
        compiler_params=_cparams(VMEM_BIG),
    )(qr, kr, vb, o, lse, do)


def _mixer_fwd(h, z, wl, l, tabs, next_gain):
    lp = h.shape[0]
    tm = _tm(lp)

    def id_epi(accs, exs, row0):
        return [accs[0]]

    (hin,) = _mm("in_proj", lp, D_INP, [_term(z, wl["win_t"], "nt", 0, (), (l,))], [((lp, D_INP), F32, "mn", 0, ())],
                 id_epi, tm=tm, tn=D_INP // 2, vmem=VMEM_BIG)
    gates, cum = _gla_gates(hin, wl["w2p"][l], wl["b2p"][l])
    o_f, o_b, s_f, s_b = _gla_fwd(hin, cum)
    o2, states = (o_f, o_b), (s_f, s_b)
    a = _gla_out_fwd(o2, hin, wl["gn"][l])
    qr, kr, vb = _attn_prep(hin, wl["gq"][l], wl["gk"][l], tabs)
    b, lse = _attn_fwd(qr, kr, vb)

    def merge_epi(accs, exs, row0):
        pa, pb = accs
        ga, gb, bma, bmb = exs
        y = _sigmoid(ga + bma) * pa + _sigmoid(gb + bmb) * pb
        return [y, pa, pb]

    y, pa, pb = _mm("merge", lp, D, [_term(a, wl["wpa_t"], "nt", 0, (), (l,)), _term(b, wl["wpb_t"], "nt", 1, (), (l,))],
                    [((lp, D), BF16, "mn", 0, ())] * 3, merge_epi,
                    extras=[(hin, "mn", P_OFF["ga"] // D, ()), (hin, "mn", P_OFF["gb"] // D, ()),
                            (wl["bm"], "n", 0, (l, 0)), (wl["bm"], "n", 0, (l, 1))],
                    tm=tm, tn=D, nsub=D // MXU_N, i_outer=True, vmem=VMEM_BIG)

    h2, z2 = _mm("out_proj", lp, D, [_term(y, wl["wout"], "nn", 0, (), (l,))],
                 [((lp, D), F32, "mn", 0, ()), ((lp, D), BF16, "mn", 0, ())], _residual_norm_epi(1.0, True),
                 extras=[(h, "mn", 0, ()), (next_gain, "n", 0, ())], tm=tm, tn=D, i_outer=True, vmem=VMEM_BIG)
    sv = dict(h=h, z=z, hin=hin, gates=gates, cum=cum, o2=o2, states=states, a=a, qr=qr, kr=kr, vb=vb, b=b, lse=lse,
              y=y, pa=pa, pb=pb)
    return h2, z2, sv


def _mixer_bwd(dh, dhb, sv, gain, wl, l, tabs):
    lp = dh.shape[0]
    tm = _tm(lp)
    hin = sv["hin"]

    def merge_bwd_epi(accs, exs, row0):
        dy = accs[0]
        ga, gb, pa, pb, bma, bmb = exs
        sa = _sigmoid(ga + bma)
        sb = _sigmoid(gb + bmb)
        dga = dy * pa.astype(F32) * sa * (1.0 - sa)
        dgb = dy * pb.astype(F32) * sb * (1.0 - sb)
        return [dy * sa, dy * sb, dga, dgb, jnp.sum(dga, axis=0, keepdims=True), jnp.sum(dgb, axis=0, keepdims=True)]

    big = ((lp, D), BF16, "mn", 0, ())
    vec = ((1, D), F32, "nsum", 0, ())
    dpa, dpb, dga, dgb, dbma, dbmb = _mm(
        "merge_bwd", lp, D, [_term(dhb, wl["wout"], "nt", 0, (), (l,))], [big, big, big, big, vec, vec], merge_bwd_epi,
        extras=[(hin, "mn", P_OFF["ga"] // D, ()), (hin, "mn", P_OFF["gb"] // D, ()), (sv["pa"], "mn", 0, ()),
                (sv["pb"], "mn", 0, ()), (wl["bm"], "n", 0, (l, 0)), (wl["bm"], "n", 0, (l, 1))],
        tm=tm, tn=D, nsub=D // MXU_N, vmem=VMEM_BIG)
    d_wout = _dw("dw_out", sv["y"], dhb, D, D)
    d_wpa_t = _dw("dw_pa", dpa, sv["a"], D, 512)
    d_wpb_t = _dw("dw_pb", dpb, sv["b"], D, ATT_H * HP)

    def id_epi(accs, exs, row0):
        return [accs[0]]

    (da,) = _mm("d_a", lp, 512, [_term(dpa, wl["wpa_t"], "nn", 0, (), (l,))], [((lp, 512), F32, "mn", 0, ())], id_epi,
                tm=tm, tn=512, i_outer=True, vmem=VMEM_BIG)
    (db,) = _mm("d_b", lp, ATT_H * HP, [_term(dpb, wl["wpb_t"], "nn", 0, (), (l,))],
                [((lp, ATT_H * HP), F32, "mn", 0, ())], id_epi, tm=tm, tn=512, i_outer=True, vmem=VMEM_BIG)
    d_o, d_ra, d_gn = _gla_out_bwd(da, sv["o2"], hin, wl["gn"][l])
    g_fw, g_bw = _gla_bwd(hin, sv["cum"], sv["states"], d_o)
    dqr, dkr, dvb = _attn_bwd(sv["qr"], sv["kr"], sv["vb"], sv["b"], sv["lse"], db)
    d_qb, d_kb, d_gq, d_gk = _attn_prep_bwd(dqr, dkr, hin, wl["gq"][l], wl["gk"][l], tabs)
    dhin, d_w2p, d_b2p = _gla_in_bwd(g_fw, g_bw, sv["gates"], hin, wl["w2p"][l],
                                     dict(qb=d_qb, ga=dga, gb=dgb, ra=d_ra, kb=d_kb, vb=dvb))
    d_win_t = _dw("dw_in", dhin, sv["z"], D_INP, D)
    dh2, dhb2, dgain = _mm("in_proj_dz", lp, D, [_term(dhin, wl["win_t"], "nn", 0, (), (l,))], _norm_bwd_outs(lp),
                           _norm_bwd_epi, extras=[(sv["h"], "mn", 0, ()), (dh, "mn", 0, ()), (gain, "n", 0, ())],
                           tm=tm, tn=D, nk=2, vmem=VMEM_BIG)
    grads = dict(gain=dgain, wout=d_wout, wpa_t=d_wpa_t, wpb_t=d_wpb_t, win_t=d_win_t, gn=d_gn, w2p=d_w2p, b2p=d_b2p,
                 gq=d_gq, gk=d_gk, bma=dbma, bmb=dbmb)
    return dh2, dhb2, grads


def _mesh_pos():
    x, y, c = lax.axis_index("x"), lax.axis_index("y"), lax.axis_index("c")
    chips = [(1 - x, y), (x, 1 - y), (1 - x, 1 - y)]
    return x, y, c, chips


def _dev_index(x, y, c):
    return 4 * x + 2 * y + c


def _all_gather(name, shards, leads):
    nt = len(shards)

    def blk(ref, lead, idx):
        return ref.at[(slice(None),) * lead + (idx,)]

    def body(*refs):
        xs, outs = refs[:nt], refs[nt:2 * nt]
        send_sems, recv_sems, local_sems = refs[2 * nt:]
        x, y, c, chips = _mesh_pos()
        me, sibling = (x, y, c), (x, y, 1 - c)

        def copy(t, k, block, to, own=False):
            dst = blk(outs[t], leads[t], _dev_index(*block))
            return pltpu.make_async_remote_copy(
                src_ref=xs[t] if own else dst, dst_ref=dst, send_sem=send_sems.at[t, k], recv_sem=recv_sems.at[t, k],
                device_id=to, device_id_type=MESH)

        locals_ = [pltpu.make_async_copy(xs[t], blk(outs[t], leads[t], _dev_index(*me)), local_sems.at[t])
                   for t in range(nt)]
        for cp in locals_:
            cp.start()
        first = []
        for t in range(nt):
            first.append(copy(t, 0, me, sibling, own=True))
            first += [copy(t, 1 + j, me, (*chip, c), own=True) for j, chip in enumerate(chips)]
        for cp in first:
            cp.start()
        passed = []
        for j, chip in enumerate(chips):
            for t in range(nt):
                copy(t, 1 + j, (*chip, c), me).wait_recv()
                fw = copy(t, 4 + j, (*chip, c), sibling)
                fw.start()
                passed.append(fw)
        for t in range(nt):
            copy(t, 0, sibling, me).wait_recv()
        for j, chip in enumerate(chips):
            for t in range(nt):
                copy(t, 4 + j, (*chip, 1 - c), me).wait_recv()
        for cp in first + passed:
            cp.wait_send()
        for cp in locals_:
            cp.wait()

    out_shape = [jax.ShapeDtypeStruct(s.shape[:ld] + (NDEV,) + s.shape[ld:], s.dtype) for s, ld in zip(shards, leads)]
    return pl.pallas_call(
        body, name=name, in_specs=[ANY] * nt, out_specs=[ANY] * nt, out_shape=out_shape,
        scratch_shapes=[pltpu.SemaphoreType.DMA((nt, 7)), pltpu.SemaphoreType.DMA((nt, 7)),
                        pltpu.SemaphoreType.DMA((nt,))],
    )(*shards)


def _exchange_sibling(name, gs):
    nt = len(gs)

    def body(*refs):
        xs, outs = refs[:nt], refs[nt:2 * nt]
        send_sems, recv_sems = refs[2 * nt:]
        x, y, c, _ = _mesh_pos()
        sibling = (x, y, 1 - c)
        copies = []
        for t in range(nt):
            for ch in range(4):
                copies.append(pltpu.make_async_remote_copy(
                    src_ref=xs[t].at[2 * ch + (1 - c)], dst_ref=outs[t].at[ch],
                    send_sem=send_sems.at[t, ch], recv_sem=recv_sems.at[t, ch],
                    device_id=sibling, device_id_type=MESH))
        for cp in copies:
            cp.start()
        for cp in copies:
            cp.wait()

    out_shape = [jax.ShapeDtypeStruct((4,) + g.shape[1:], g.dtype) for g in gs]
    return pl.pallas_call(
        body, name=name, in_specs=[ANY] * nt, out_specs=[ANY] * nt, out_shape=out_shape,
        scratch_shapes=[pltpu.SemaphoreType.DMA((nt, 4)), pltpu.SemaphoreType.DMA((nt, 4))],
    )(*gs)


def _pair_sum(name, gs, recv):
    c = lax.axis_index("c")
    outs = []
    for t, (g, rv) in enumerate(zip(gs, recv)):
        _, r, cols = rv.shape

        def body(c_ref, g_ref, r_ref, o_ref):
            o_ref[...] = (g_ref[...].astype(F32) + r_ref[...].astype(F32)).astype(o_ref.dtype)

        outs.append(pl.pallas_call(
            body, name=f"{name}_{t}",
            grid_spec=pltpu.PrefetchScalarGridSpec(
                num_scalar_prefetch=1, grid=(4,),
                in_specs=[pl.BlockSpec((None, r, cols), lambda ch, cr: (2 * ch + cr[0], 0, 0)),
                          pl.BlockSpec((None, r, cols), lambda ch, cr: (ch, 0, 0))],
                out_specs=pl.BlockSpec((None, r, cols), lambda ch, cr: (ch, 0, 0))),
            out_shape=jax.ShapeDtypeStruct(rv.shape, rv.dtype),
        )(jnp.reshape(c, (1,)).astype(jnp.int32), g, rv))
    return outs


def _final_sum(name, ps, recv, transposed):
    chip = 2 * lax.axis_index("x") + lax.axis_index("y")
    outs = []
    for t, (p, rv) in enumerate(zip(ps, recv)):
        _, r, cols = rv.shape
        tr_out = transposed[t]
        oshape = (cols, r) if tr_out else (r, cols)

        def body(c_ref, p_ref, r0_ref, r1_ref, r2_ref, o_ref):
            acc = ((p_ref[...].astype(F32) + r0_ref[...].astype(F32)) + r1_ref[...].astype(F32)) + r2_ref[...].astype(F32)
            o_ref[...] = acc.T if tr_out else acc

        outs.append(pl.pallas_call(
            body, name=f"{name}_{t}",
            grid_spec=pltpu.PrefetchScalarGridSpec(
                num_scalar_prefetch=1, grid=(1,),
                in_specs=[pl.BlockSpec((None, r, cols), lambda i, cr: (cr[0], 0, 0))] +
                         [pl.BlockSpec((None, r, cols), lambda i, cr, j=j: (j, 0, 0)) for j in range(3)],
                out_specs=pl.BlockSpec(oshape, lambda i, cr: (0, 0))),
            out_shape=jax.ShapeDtypeStruct(oshape, F32),
        )(jnp.reshape(chip, (1,)).astype(jnp.int32), p, rv, rv, rv))
    return outs


def _sum_gathered(g):
    _, r, cols = g.shape

    def body(g_ref, o_ref):
        acc = g_ref[0]
        for d in range(1, NDEV):
            acc = acc + g_ref[d]
        o_ref[...] = acc

    return pl.pallas_call(body, name="small_sum", out_shape=jax.ShapeDtypeStruct((r, cols), F32))(g)


HBM = pl.BlockSpec(memory_space=pltpu.HBM)
SEM = pl.BlockSpec(memory_space=pltpu.SEMAPHORE)
EFFECT = pltpu.SideEffectType.DATAFLOW_SIDE_EFFECTING
NREL = NDEV - 1


def _related(k):
    x, y, c = lax.axis_index("x"), lax.axis_index("y"), lax.axis_index("c")
    px = 1 - x if k & 4 else x
    py = 1 - y if k & 2 else y
    pc = 1 - c if k & 1 else c
    return (px, py, pc), _dev_index(px, py, pc)


def _in_hbm(a):
    return pltpu.with_memory_space_constraint(a, pltpu.HBM)


ALL_RELS = tuple(range(1, NDEV))
CHIP_RELS = (4, 2, 6)


def _split_copies(xs, lands, send_sems, recv_sems, src_of, dst_of, rels):
    copies = []
    for t in range(len(xs)):
        for q, k in enumerate(rels):
            peer, peer_idx = _related(k)
            copies.append(pltpu.make_async_remote_copy(
                src_ref=src_of(xs[t], t, peer_idx), dst_ref=dst_of(lands[t], t, q, peer_idx),
                send_sem=send_sems.at[t * len(rels) + q], recv_sem=recv_sems.at[t * len(rels) + q],
                device_id=peer, device_id_type=MESH))
    return copies


def _exchange_start(name, xs, lands, src_of, dst_of, after, rels=ALL_RELS):
    nt = len(xs)

    def body(*refs):
        x_refs, land_refs = refs[:nt], refs[nt:2 * nt]
        send_sems, recv_sems = refs[2 * nt + 1], refs[2 * nt + 2]
        token = refs[-1]
        for cp in _split_copies(x_refs, land_refs, send_sems, recv_sems, src_of, dst_of, rels):
            cp.start()
        token[...] = jnp.zeros_like(token)

    res = pl.pallas_call(
        body, name=name,
        out_shape=(pltpu.SemaphoreType.DMA((nt * len(rels),)), pltpu.SemaphoreType.DMA((nt * len(rels),)),
                   *[pltpu.HBM(a.shape, a.dtype) for a in xs], *[pltpu.HBM(a.shape, a.dtype) for a in lands],
                   jax.ShapeDtypeStruct((8, 128), F32)),
        in_specs=[HBM] * (2 * nt) + [ANY],
        out_specs=(SEM, SEM, *[HBM] * (2 * nt), pl.BlockSpec(memory_space=pltpu.VMEM)),
        input_output_aliases={i: 2 + i for i in range(2 * nt)},
        compiler_params=pltpu.CompilerParams(has_side_effects=EFFECT),
    )(*[_in_hbm(a) for a in xs], *[_in_hbm(a) for a in lands], after)
    return res[0], res[1], res[2:2 + nt], res[2 + nt:2 + 2 * nt], res[-1]


def _exchange_wait(name, send_sems, recv_sems, xs, lands, src_of, dst_of, after, rels=ALL_RELS):
    nt = len(xs)

    def body(*refs):
        x_refs, land_refs = refs[:nt], refs[nt:2 * nt]
        send_sems, recv_sems = refs[2 * nt], refs[2 * nt + 1]
        for cp in _split_copies(x_refs, land_refs, send_sems, recv_sems, src_of, dst_of, rels):
            cp.wait_send()
            cp.wait_recv()

    res = pl.pallas_call(
        body, name=name,
        out_shape=(*[pltpu.HBM(a.shape, a.dtype) for a in xs], *[pltpu.HBM(a.shape, a.dtype) for a in lands]),
        in_specs=[HBM] * (2 * nt) + [SEM, SEM, ANY], out_specs=tuple([HBM] * (2 * nt)),
        input_output_aliases={i: i for i in range(2 * nt)},
        compiler_params=pltpu.CompilerParams(has_side_effects=EFFECT),
    )(*xs, *lands, send_sems, recv_sems, after)
    return res[:nt], res[nt:]


def _gather_start(name, shards, leads, after, rels=ALL_RELS):
    def src_of(x_ref, t, peer_idx):
        return x_ref

    def dst_of(land_ref, t, k, peer_idx):
        me = _dev_index(lax.axis_index("x"), lax.axis_index("y"), lax.axis_index("c"))
        return land_ref.at[(slice(None),) * leads[t] + (me,)]

    lands = [lax.empty(s.shape[:ld] + (NDEV,) + s.shape[ld:], s.dtype) for s, ld in zip(shards, leads)]
    return _exchange_start(name, shards, lands, src_of, dst_of, after, rels)


def _gather_wait(name, started, leads, after, rels=ALL_RELS):
    send_sems, recv_sems, shards, lands, _ = started

    def src_of(x_ref, t, peer_idx):
        return x_ref

    def dst_of(land_ref, t, k, peer_idx):
        return land_ref.at[(slice(None),) * leads[t] + (peer_idx,)]

    shards, lands = _exchange_wait(name, send_sems, recv_sems, shards, lands, src_of, dst_of, after, rels)
    me = _dev_index(lax.axis_index("x"), lax.axis_index("y"), lax.axis_index("c"))
    return [lax.dynamic_update_index_in_dim(g, s, me, ld) for g, s, ld in zip(lands, shards, leads)]


SIBLING_AND_CHIPS = (1,) + CHIP_RELS


def _forward_to_sibling(name, gathered, leads):
    nt = len(gathered)

    def body(*refs):
        ins, outs = refs[:nt], refs[nt:2 * nt]
        send_sems, recv_sems = refs[2 * nt:]
        x, y, c, chips = _mesh_pos()
        copies, arrivals = [], []
        for t in range(nt):
            for j, chip in enumerate(chips):
                def block(core):
                    return outs[t].at[(slice(None),) * leads[t] + (_dev_index(*chip, core),)]
                copies.append(pltpu.make_async_remote_copy(
                    src_ref=block(c), dst_ref=block(c), send_sem=send_sems.at[t, j], recv_sem=recv_sems.at[t, j],
                    device_id=(x, y, 1 - c), device_id_type=MESH))
                arrivals.append(pltpu.make_async_remote_copy(
                    src_ref=block(1 - c), dst_ref=block(1 - c), send_sem=send_sems.at[t, j], recv_sem=recv_sems.at[t, j],
                    device_id=(x, y, 1 - c), device_id_type=MESH))
        for cp in copies:
            cp.start()
        for cp in arrivals:
            cp.wait_recv()
        for cp in copies:
            cp.wait_send()

    return pl.pallas_call(
        body, name=name, in_specs=[ANY] * nt, out_specs=[ANY] * nt,
        out_shape=[jax.ShapeDtypeStruct(g.shape, g.dtype) for g in gathered],
        input_output_aliases={t: t for t in range(nt)},
        scratch_shapes=[pltpu.SemaphoreType.DMA((nt, 3)), pltpu.SemaphoreType.DMA((nt, 3))],
    )(*gathered)


def _scatter_src(x_ref, t, peer_idx):
    return x_ref.at[peer_idx]


def _scatter_dst(land_ref, t, q, peer_idx):
    return land_ref.at[q]


def _chips_src(x_ref, t, peer_idx):
    return x_ref.at[peer_idx // 2]


def _chips_start(name, ps, after):
    lands = [lax.empty((len(CHIP_RELS),) + p.shape[1:], p.dtype) for p in ps]
    return _exchange_start(name, ps, lands, _chips_src, _scatter_dst, after, CHIP_RELS)


def _chips_wait(name, started, after):
    send_sems, recv_sems, ps, lands, _ = started
    return _exchange_wait(name, send_sems, recv_sems, ps, lands, _chips_src, _scatter_dst, after, CHIP_RELS)


def _scatter_start(name, gs, after):
    lands = [lax.empty((NREL,) + g.shape[1:], g.dtype) for g in gs]
    return _exchange_start(name, gs, lands, _scatter_src, _scatter_dst, after)


def _scatter_wait(name, started, after, transposed):
    send_sems, recv_sems, gs, lands, _ = started
    gs, lands = _exchange_wait(name, send_sems, recv_sems, gs, lands, _scatter_src, _scatter_dst, after)
    me = _dev_index(lax.axis_index("x"), lax.axis_index("y"), lax.axis_index("c"))
    outs = []
    for t, (g, rv) in enumerate(zip(gs, lands)):
        _, r, cols = rv.shape
        tr_out = transposed[t]
        oshape = (cols, r) if tr_out else (r, cols)

        def body(c_ref, own_ref, rv_ref, o_ref):
            acc = own_ref[...].astype(F32)
            for k in range(NREL):
                acc = acc + rv_ref[k].astype(F32)
            o_ref[...] = acc.T if tr_out else acc

        outs.append(pl.pallas_call(
            body, name=f"{name}_sum_{t}",
            grid_spec=pltpu.PrefetchScalarGridSpec(
                num_scalar_prefetch=1, grid=(1,),
                in_specs=[pl.BlockSpec((None, r, cols), lambda i, cr: (cr[0], 0, 0)),
                          pl.BlockSpec((NREL, r, cols), lambda i, cr: (0, 0, 0))],
                out_specs=pl.BlockSpec(oshape, lambda i, cr: (0, 0))),
            out_shape=jax.ShapeDtypeStruct(oshape, F32), compiler_params=_cparams(VMEM_BIG),
        )(jnp.reshape(me, (1,)).astype(jnp.int32), g, rv))
    return outs


def _adamw(w, g, m, v):
    shape = w.shape
    cols = shape[-1]
    rows = math.prod(shape[:-1]) if len(shape) > 1 else 1
    w2, g2, m2, v2 = (jnp.reshape(t, (rows, cols)) for t in (w, g, m, v))
    tr = _pick(rows, (1024, 512, 256, 128)) if rows * cols > 65536 else rows
    c1 = 1.0 / (1.0 - ADAM_B1 ** ADAM_STEP)
    c2 = 1.0 / (1.0 - ADAM_B2 ** ADAM_STEP)

    def body(w_ref, g_ref, m_ref, v_ref, d_ref, nm_ref, nv_ref):
        gv = g_ref[...]
        nm = ADAM_B1 * m_ref[...] + (1.0 - ADAM_B1) * gv
        nv = ADAM_B2 * v_ref[...] + (1.0 - ADAM_B2) * (gv * gv)
        d_ref[...] = -ADAM_LR * ((nm * c1) / (jnp.sqrt(nv * c2) + ADAM_EPS) + ADAM_WD * w_ref[...])
        nm_ref[...] = nm
        nv_ref[...] = nv

    spec = pl.BlockSpec((tr, cols), lambda i: (i, 0))
    osh = jax.ShapeDtypeStruct((rows, cols), F32)
    d, nm, nv = pl.pallas_call(
        body, name="adamw", grid=(rows // tr,), in_specs=[spec] * 4, out_specs=[spec] * 3, out_shape=[osh] * 3,
        compiler_params=_cparams(VMEM_BIG),
    )(w2, g2, m2, v2)
    return jnp.reshape(d, shape), jnp.reshape(nm, shape), jnp.reshape(nv, shape)


def _pad_heads(w, name):
    if name not in P_HEADS:
        return w
    nh, real = P_HEADS[name]
    w = jnp.reshape(w, w.shape[:-2] + (nh, real, w.shape[-1]))
    w = jnp.pad(w, [(0, 0)] * (w.ndim - 2) + [(0, HP - real), (0, 0)])
    return jnp.reshape(w, w.shape[:-3] + (nh * HP, w.shape[-1]))


def _unpad_heads(w, name):
    if name not in P_HEADS:
        return w
    nh, real = P_HEADS[name]
    w = jnp.reshape(w, w.shape[:-2] + (nh, HP, w.shape[-1]))[..., :real, :]
    return jnp.reshape(w, w.shape[:-3] + (nh * real, w.shape[-1]))


def _win_pad(win_t):
    segs, o = {}, 0
    for n, s in zip(IN_NAMES, IN_SIZES):
        segs[n] = win_t[..., o:o + s, :]
        o += s
    return jnp.concatenate([_pad_heads(segs[n], n) for n in P_ORDER], axis=-2)


def _win_unpad(win_p):
    segs = {n: _unpad_heads(win_p[..., P_OFF[n]:P_OFF[n] + P_WIDTH[n], :], n) for n in P_ORDER}
    return jnp.concatenate([segs[n] for n in IN_NAMES], axis=-2)


def _t(w):
    return jnp.swapaxes(w, -1, -2)


def _ffn_stacked(g_g, g_u, g_d):
    wg, wu, wd = (jnp.reshape(g, (2, DFF, D)) for g in (g_g, g_u, g_d))
    return [(wg, wu, wd, (j,)) for j in range(2)]


def _ffn_single(g_g, g_u, g_d):
    return tuple(jnp.reshape(g, (DFF, D)) for g in (g_g, g_u, g_d)) + ((),)


def _layer_weights(ffn, g_in, g_pa, g_pb, g_out, gains, w2, b2, bm, gn, gq, gk):
    w2p = jnp.pad(jnp.reshape(w2, (2, GLA_RANK, GLA_H, GLA_DK)), ((0, 0), (0, HP - GLA_RANK), (0, 0), (0, HP - GLA_DK)))
    b2p = jnp.pad(jnp.reshape(b2, (2, 1, GLA_H, GLA_DK)), ((0, 0), (0, 0), (0, 0), (0, HP - GLA_DK)))
    wpb_t = jnp.pad(jnp.reshape(g_pb, (D, ATT_H, HEAD_DIM)), ((0, 0), (0, 0), (0, HP - HEAD_DIM)))
    return dict(
        gains=jnp.reshape(gains, (1, 3, 1, D)), ffn=ffn,
        win_t=_win_pad(jnp.reshape(g_in, (1, D_IN, D))), wpa_t=jnp.reshape(g_pa, (1, D, 512)),
        wpb_t=jnp.reshape(wpb_t, (1, D, ATT_H * HP)), wout=jnp.reshape(g_out, (1, D, D)),
        w2p=jnp.reshape(w2p, (1, 2, HP, GLA_H * HP)), b2p=jnp.reshape(b2p, (1, 2, 1, GLA_H * HP)),
        bm=jnp.reshape(bm, (1, 2, 1, D)), gn=jnp.reshape(gn, (1, 1, GLA_H * HP)),
        gq=jnp.pad(jnp.reshape(gq, (1, 1, HEAD_DIM)), ((0, 0), (0, 0), (0, HP - HEAD_DIM))),
        gk=jnp.pad(jnp.reshape(gk, (1, 1, HEAD_DIM)), ((0, 0), (0, 0), (0, HP - HEAD_DIM))))


def _layer_fwd_lower(h, z, ffn0, gain1):
    return _ffn_fwd(h, z, *ffn0, gain1)


def _layer_fwd_upper(h, z, s0, w, tabs, next_gain):
    h, z, s1 = _mixer_fwd(h, z, w, 0, tabs, w["gains"][0, 2])
    h, z, s2 = _ffn_fwd(h, z, *w["ffn"][1], next_gain)
    return h, z, (s0, s1, s2)


def _layer_fwd(h, z, w, tabs, next_gain):
    h, z, s0 = _layer_fwd_lower(h, z, w["ffn"][0], w["gains"][0, 1])
    return _layer_fwd_upper(h, z, s0, w, tabs, next_gain)


def _layer_bwd_upper(dh, dhb, saved, w, tabs):
    _, s1, s2 = saved
    dh, dhb, dg2, dwg1, dwu1, dwd1 = _ffn_bwd(dh, dhb, s2, w["gains"][0, 2], *w["ffn"][1])
    dh, dhb, gm = _mixer_bwd(dh, dhb, s1, w["gains"][0, 1], w, 0, tabs)
    gm.update(gain2=dg2, wg1=dwg1, wu1=dwu1, wd1=dwd1)
    return dh, dhb, gm


def _layer_bwd_lower(dh, dhb, saved, w, gm):
    dh, dhb, dg0, dwg0, dwu0, dwd0 = _ffn_bwd(dh, dhb, saved[0], w["gains"][0, 0], *w["ffn"][0])
    gm.update(gain0=dg0, wg0=dwg0, wu0=dwu0, wd0=dwd0)
    return dh, dhb, gm


def _layer_bwd(dh, dhb, saved, w, tabs):
    dh, dhb, gm = _layer_bwd_upper(dh, dhb, saved, w, tabs)
    return _layer_bwd_lower(dh, dhb, saved, w, gm)


def _blocks(ts):
    return [jnp.reshape(t, (NDEV, t.shape[0] // NDEV, t.shape[1])) for t in ts]


def _upper_grads(g):
    d_in = _win_unpad(g["win_t"])
    d_pb = jnp.reshape(jnp.reshape(g["wpb_t"], (D, ATT_H, HP))[:, :, :HEAD_DIM], (D, 512))
    return _blocks([g["wg1"], g["wu1"], g["wd1"], d_in, g["wpa_t"], d_pb, g["wout"]])


def _lower_grads(g):
    return _blocks([g["wg0"], g["wu0"], g["wd0"]])


def _big_grads(g):
    return _lower_grads(g) + _upper_grads(g)


def kernel(x, meta_tokens, norm_gains, ffn_w_gate, ffn_w_up, ffn_w_down, w_in, gla_w2, gla_b2, gla_gn, q_norm, k_norm, w_pa, w_pb, b_merge, w_out, final_norm, loss_target, m_meta_tokens, m_norm_gains, m_ffn_w_gate, m_ffn_w_up, m_ffn_w_down, m_w_in, m_gla_w2, m_gla_b2, m_gla_gn, m_q_norm, m_k_norm, m_w_pa, m_w_pb, m_b_merge, m_w_out, m_final_norm, v_meta_tokens, v_norm_gains, v_ffn_w_gate, v_ffn_w_up, v_ffn_w_down, v_w_in, v_gla_w2, v_gla_b2, v_gla_gn, v_q_norm, v_k_norm, v_w_pa, v_w_pb, v_b_merge, v_w_out, v_final_norm):
    dev = _dev_index(lax.axis_index("x"), lax.axis_index("y"), lax.axis_index("c"))
    sh_g = _t(ffn_w_gate).astype(BF16)
    sh_u = _t(ffn_w_up).astype(BF16)
    sh_d = ffn_w_down.astype(BF16)
    sh_in = _t(w_in).astype(BF16)
    sh_pa = _t(w_pa).astype(BF16)
    sh_pb = _t(w_pb).astype(BF16)
    sh_out = w_out.astype(BF16)
    small = jnp.concatenate([jnp.reshape(t, (-1, 128)) for t in
                             (meta_tokens, norm_gains, gla_w2, gla_b2, b_merge)], axis=0)
    small = jnp.pad(small, ((0, 2), (0, 0)))
    def shards(l):
        return [sh_g[l], sh_u[l], sh_d[l], sh_in[l], sh_pa[l], sh_pb[l], sh_out[l]]

    w_leads = [1, 1, 1, 0, 0, 0, 0]
    *g0_ffn0, g_small = _all_gather("gather_layer0", [sh_g[0, 0], sh_u[0, 0], sh_d[0, 0], small], [0, 0, 0, 0])
    rest0 = [sh_g[0, 1], sh_u[0, 1], sh_d[0, 1], sh_in[0], sh_pa[0], sh_pb[0], sh_out[0]]
    rest_leads = [0] * len(rest0)
    started0 = _gather_start("gather_start_0", rest0, rest_leads, g_small, SIBLING_AND_CHIPS)
    meta_full = jnp.reshape(jnp.transpose(g_small[:, 0:16], (1, 0, 2)), (NMETA, D)) + started0[4][0, 0]
    gains_full = jnp.reshape(jnp.transpose(jnp.reshape(g_small[:, 16:28], (NDEV, DEPTH, 3, 128)), (1, 2, 0, 3)), (DEPTH, 3, D))
    w2_full = jnp.reshape(jnp.transpose(jnp.reshape(g_small[:, 28:60], (NDEV, DEPTH, 2, GLA_RANK, 32)), (1, 2, 3, 0, 4)),
                          (DEPTH, 2, GLA_RANK, 256))
    b2_full = jnp.reshape(jnp.transpose(jnp.reshape(g_small[:, 60:62], (NDEV, DEPTH, 2, 32)), (1, 2, 0, 3)), (DEPTH, 2, 256))
    bm_full = jnp.reshape(jnp.transpose(jnp.reshape(g_small[:, 62:70], (NDEV, DEPTH, 2, 128)), (1, 2, 0, 3)), (DEPTH, 2, D))

    def layer_weights(l, ffn, others, gains_l):
        return _layer_weights(ffn, *others, gains_l, w2_full[l], b2_full[l], bm_full[l], gla_gn[l], q_norm[l], k_norm[l])

    xl = x[0]
    lp = xl.shape[0] + NULL + NMETA
    tabs = _rope_tables(lp)
    h = jnp.concatenate([jnp.zeros((NULL, D), F32), meta_full, xl], axis=0)
    weights, saved, started = [], [], {}
    z = _rmsnorm_fwd(h, jnp.reshape(gains_full[0, 0], (1, D)))
    for l in range(DEPTH):
        next_gain = jnp.reshape(gains_full[l + 1, 0], (1, D)) if l + 1 < DEPTH else None
        if l == 0:
            ffn0 = _ffn_single(*g0_ffn0)
            h, z, s0 = _layer_fwd_lower(h, z, ffn0, jnp.reshape(gains_full[0, 1], (1, D)))
            rest = _forward_to_sibling("gather_forward_0", _gather_wait("gather_wait_0", started0, rest_leads, h,
                                                                         SIBLING_AND_CHIPS), rest_leads)
            started[1] = _gather_start("gather_start_1", shards(1), w_leads, rest[0])
            weights.append(layer_weights(0, [ffn0, _ffn_single(*rest[:3])], rest[3:], gains_full[0]))
            z = z + started[1][4][0, 0].astype(BF16)
            h, z, sv = _layer_fwd_upper(h, z, s0, weights[0], tabs, next_gain)
        else:
            tok = jnp.zeros((), F32)
            if l < DEPTH - 1:
                started[l + 1] = _gather_start(f"gather_start_{l + 1}", shards(l + 1), w_leads, h)
                tok = started[l + 1][4][0, 0]
            gathered = _gather_wait(f"gather_wait_{l}", started[l], w_leads, h)
            weights.append(layer_weights(l, _ffn_stacked(*gathered[:3]), gathered[3:], gains_full[l] + tok))
            h, z, sv = _layer_fwd(h, z, weights[l], tabs, next_gain)
        saved.append(sv)
    loss, dh, dhb, d_final = _loss_head(h, loss_target[0], jnp.reshape(final_norm, (1, D)))
    loss = lax.psum(loss[0, 0], ("x", "y", "c"))

    grads, scattering = [None] * DEPTH, {}
    tok = jnp.zeros((), F32)
    for l in reversed(range(DEPTH)):
        w = dict(weights[l], gains=weights[l]["gains"] + tok)
        if l > 0:
            dh, dhb, grads[l] = _layer_bwd(dh, dhb, saved[l], w, tabs)
            scattering[l] = _scatter_start(f"scatter_start_{l}", _big_grads(grads[l]), dhb)
            tok = scattering[l][4][0, 0]
        else:
            dh, dhb, gm = _layer_bwd_upper(dh, dhb, saved[l], w, tabs)
            ups = _upper_grads(gm)
            pair = _pair_sum("rs_pair_up", ups, _exchange_sibling("rs_sibling_up", ups))
            scattering[l] = _chips_start(f"scatter_start_{l}", pair, dhb)
            dhb = dhb + scattering[l][4][0, 0].astype(BF16)
            dh, dhb, grads[l] = _layer_bwd_lower(dh, dhb, saved[l], w, gm)
    grad_x = dh[NULL + NMETA:][None]
    t_lower, t_upper = [True, True, False], [True, True, False, True, True, True, False]
    lows = _lower_grads(grads[0])
    pair_lo = _pair_sum("rs_pair_lo", lows, _exchange_sibling("rs_sibling_lo", lows))
    started_lo = _chips_start("scatter_start_lo", pair_lo, dhb)
    red = [None] * DEPTH
    for l in reversed(range(1, DEPTH)):
        red[l] = _scatter_wait(f"scatter_wait_{l}", scattering[l], started_lo[4], t_lower + t_upper)
    pair, recv = _chips_wait("scatter_wait_0", scattering[0], red[1][-1])
    red_upper = _final_sum("rs_sum_up", pair, recv, t_upper)
    pair_lo, recv_lo = _chips_wait("scatter_wait_lo", started_lo, red_upper[-1])
    red[0] = _final_sum("rs_sum_lo", pair_lo, recv_lo, t_lower) + red_upper
    g_gate = jnp.stack([jnp.stack([red[l][0], red[l][3]]) for l in range(DEPTH)])
    g_up = jnp.stack([jnp.stack([red[l][1], red[l][4]]) for l in range(DEPTH)])
    g_down = jnp.stack([jnp.stack([red[l][2], red[l][5]]) for l in range(DEPTH)])
    g_win = jnp.stack([red[l][6] for l in range(DEPTH)])
    g_wpa = jnp.stack([red[l][7] for l in range(DEPTH)])
    g_wpb = jnp.stack([red[l][8] for l in range(DEPTH)])
    g_wout = jnp.stack([red[l][9] for l in range(DEPTH)])

    d_meta = dh[NULL:NULL + NMETA]
    d_gains = jnp.stack([jnp.concatenate([grads[l]["gain0"], grads[l]["gain"], grads[l]["gain2"]], axis=0)
                         for l in range(DEPTH)])
    d_w2 = jnp.stack([jnp.reshape(jnp.reshape(grads[l]["w2p"], (2, HP, GLA_H, HP))[:, :GLA_RANK, :, :GLA_DK],
                                  (2, GLA_RANK, 256)) for l in range(DEPTH)])
    d_b2 = jnp.stack([jnp.reshape(jnp.reshape(grads[l]["b2p"], (2, GLA_H, HP))[:, :, :GLA_DK], (2, 256))
                      for l in range(DEPTH)])
    d_gn = jnp.stack([grads[l]["gn"][0] for l in range(DEPTH)])
    d_gq = jnp.stack([grads[l]["gq"][0, :HEAD_DIM] for l in range(DEPTH)])
    d_gk = jnp.stack([grads[l]["gk"][0, :HEAD_DIM] for l in range(DEPTH)])
    d_bm = jnp.stack([jnp.concatenate([grads[l]["bma"], grads[l]["bmb"]], axis=0) for l in range(DEPTH)])
    parts = [d_meta, d_gains, d_w2, d_b2, d_gn, d_gq, d_gk, d_bm, d_final[0]]
    sizes = [p.size for p in parts]
    flat = jnp.concatenate([jnp.reshape(p, (-1,)) for p in parts])
    flat = jnp.reshape(flat, (-1, 128))
    nrow = flat.shape[0]
    flat = jnp.pad(flat, ((0, (-nrow) % 8), (0, 0)))
    (g_flat,) = _all_gather("gather_small_grads", [flat], [0])
    tot = jnp.reshape(_sum_gathered(g_flat), (-1,))
    full, o = [], 0
    for p, s in zip(parts, sizes):
        full.append(jnp.reshape(tot[o:o + s], p.shape))
        o += s
    f_meta, f_gains, f_w2, f_b2, f_gn, f_gq, f_gk, f_bm, f_final = full

    def mine(t, width):
        return lax.dynamic_slice_in_dim(t, dev * width, width, axis=t.ndim - 1)

    g_small = dict(meta_tokens=mine(f_meta, 128), norm_gains=mine(f_gains, 128), gla_w2=mine(f_w2, 32),
                   gla_b2=mine(f_b2, 32), gla_gn=f_gn, q_norm=f_gq, k_norm=f_gk, b_merge=mine(f_bm, 128),
                   final_norm=f_final)
    gr = dict(g_small, ffn_w_gate=g_gate, ffn_w_up=g_up, ffn_w_down=g_down, w_in=g_win, w_pa=g_wpa, w_pb=g_wpb,
              w_out=g_wout)
    ws = dict(meta_tokens=meta_tokens, norm_gains=norm_gains, ffn_w_gate=ffn_w_gate, ffn_w_up=ffn_w_up,
              ffn_w_down=ffn_w_down, w_in=w_in, gla_w2=gla_w2, gla_b2=gla_b2, gla_gn=gla_gn, q_norm=q_norm,
              k_norm=k_norm, w_pa=w_pa, w_pb=w_pb, b_merge=b_merge, w_out=w_out, final_norm=final_norm)
    ms = dict(meta_tokens=m_meta_tokens, norm_gains=m_norm_gains, ffn_w_gate=m_ffn_w_gate, ffn_w_up=m_ffn_w_up,
              ffn_w_down=m_ffn_w_down, w_in=m_w_in, gla_w2=m_gla_w2, gla_b2=m_gla_b2, gla_gn=m_gla_gn, q_norm=m_q_norm,
              k_norm=m_k_norm, w_pa=m_w_pa, w_pb=m_w_pb, b_merge=m_b_merge, w_out=m_w_out, final_norm=m_final_norm)
    vs = dict(meta_tokens=v_meta_tokens, norm_gains=v_norm_gains, ffn_w_gate=v_ffn_w_gate, ffn_w_up=v_ffn_w_up,
              ffn_w_down=v_ffn_w_down, w_in=v_w_in, gla_w2=v_gla_w2, gla_b2=v_gla_b2, gla_gn=v_gla_gn, q_norm=v_q_norm,
              k_norm=v_k_norm, w_pa=v_w_pa, w_pb=v_w_pb, b_merge=v_b_merge, w_out=v_w_out, final_norm=v_final_norm)
    names = ["meta_tokens", "norm_gains", "ffn_w_gate", "ffn_w_up", "ffn_w_down", "w_in", "gla_w2", "gla_b2", "gla_gn",
             "q_norm", "k_norm", "w_pa", "w_pb", "b_merge", "w_out", "final_norm"]
    deltas, new_m, new_v = [], [], []
    for n in names:
        dlt, nm, nv = _adamw(ws[n], gr[n], ms[n], vs[n])
        deltas.append(dlt)
        new_m.append(nm)
        new_v.append(nv)
    return (loss, grad_x, *[gr[n] for n in names], *deltas, *new_m, *new_v)
```

```python
import functools
import math

import jax
import jax.numpy as jnp
import numpy as np
from jax import lax
from jax.experimental import pallas as pl
from jax.experimental.pallas import tpu as pltpu

F32 = jnp.float32
BF16 = jnp.bfloat16
MESH = pl.DeviceIdType.MESH
ANY = pl.BlockSpec(memory_space=pl.ANY)

NDEV = 8
D = 1024
DFF = 2816
DEPTH = 4
NMETA = 16
NULL = 112
GRID_W = 64
EPS = 1e-6
HP = 128
GLA_H = 4
GLA_DK = 64
GLA_RANK = 16
GLA_TAU = 16.0
CHUNK = 64
ATT_H = 8
ATT_KV = 2
ATT_G = ATT_H // ATT_KV
HEAD_DIM = 64
ROPE_THETA = 10000.0

IN_SIZES = (256, 256, 512, 512, 16, 16, 512, 128, 128, 1024, 1024)
IN_NAMES = ("qa", "ka", "va", "ra", "lrf", "lrb", "qb", "kb", "vb", "ga", "gb")
D_IN = sum(IN_SIZES)
P_ORDER = ("qb", "ga", "gb", "qa", "ka", "va", "ra", "kb", "vb", "lrf", "lrb")
P_WIDTH = dict(qb=1024, ga=1024, gb=1024, qa=512, ka=512, va=512, ra=512, kb=256, vb=256, lrf=128, lrb=128)
P_OFF = {}
_o = 0
for _n in P_ORDER:
    P_OFF[_n] = _o
    _o += P_WIDTH[_n]
D_INP = _o
P_HEADS = dict(qa=(4, 64), ka=(4, 64), qb=(8, 64), kb=(2, 64), vb=(2, 64), lrf=(1, 16), lrb=(1, 16))

ADAM_LR = 0.001
ADAM_B1 = 0.9
ADAM_B2 = 0.999
ADAM_EPS = 1e-08
ADAM_WD = 0.01
ADAM_STEP = 10

VMEM_BIG = 56 * 1024 * 1024
MXU_N = 256


def _cparams(vmem=None):
    return pltpu.CompilerParams(vmem_limit_bytes=vmem) if vmem else pltpu.CompilerParams()


def _pick(n, prefs):
    for p in prefs:
        if n % p == 0:
            return p
    return n


def _tm(lp):
    return _pick(lp, (528, 512, 256, 128))


_DN = {"nn": (((1,), (0,)), ((), ())), "nt": (((1,), (1,)), ((), ())), "tn": (((0,), (0,)), ((), ()))}


def _dot(a, b, mode="nn", precision=None):
    return lax.dot_general(a, b, _DN[mode], preferred_element_type=F32, precision=precision)


def _split(x):
    hi = x.astype(BF16)
    return hi, (x - hi.astype(F32)).astype(BF16)


def _dot_sel(t, x, mode="nn"):
    hi, lo = _split(x)
    return _dot(t, hi, mode) + _dot(t, lo, mode)


def _dot3(a, b, mode="nn"):
    ah, al = _split(a)
    bh, bl = _split(b)
    return _dot(ah, bh, mode) + (_dot(ah, bl, mode) + _dot(al, bh, mode))


def _sigmoid(x):
    return 0.5 * jnp.tanh(0.5 * x) + 0.5


def _mm(name, m, n, terms, outs, epilogue, extras=(), *, tm, tn, nk=1, nsub=1, i_outer=False, vmem=None):
    gm, gn = m // tm, n // tn
    assert gm * tm == m and gn * tn == n, (name, m, n, tm, tn)
    n_acc = 1 + max(t[3] for t in terms)

    def gmap(f):
        if i_outer:
            return lambda i, j, kk: f(i, j, kk)
        return lambda j, i, kk: f(i, j, kk)

    in_specs, args = [], []
    for a, b, mode, _, pa, pb in terms:
        kdim = a.shape[-2] if mode == "tn" else a.shape[-1]
        tk = kdim // nk
        assert tk * nk == kdim
        na, nb = (None,) * len(pa), (None,) * len(pb)
        if mode == "tn":
            in_specs.append(pl.BlockSpec(na + (tk, tm), gmap(lambda i, j, kk, pa=pa: pa + (kk, i))))
        else:
            in_specs.append(pl.BlockSpec(na + (tm, tk), gmap(lambda i, j, kk, pa=pa: pa + (i, kk))))
        if mode == "nt":
            in_specs.append(pl.BlockSpec(nb + (tn, tk), gmap(lambda i, j, kk, pb=pb: pb + (j, kk))))
        else:
            in_specs.append(pl.BlockSpec(nb + (tk, tn), gmap(lambda i, j, kk, pb=pb: pb + (kk, j))))
        args += [a, b]
    for arr, kind, off, pe in extras:
        ne = (None,) * len(pe)
        if kind == "mn":
            in_specs.append(pl.BlockSpec(ne + (tm, tn), gmap(lambda i, j, kk, off=off, pe=pe: pe + (i, j + off))))
        else:
            in_specs.append(pl.BlockSpec(ne + (1, tn), gmap(lambda i, j, kk, off=off, pe=pe: pe + (0, j + off))))
        args.append(arr)
    out_shape, out_specs = [], []
    for shape, dtype, kind, off, po in outs:
        no = (None,) * len(po)
        out_shape.append(jax.ShapeDtypeStruct(shape, dtype))
        if kind == "mn":
            out_specs.append(pl.BlockSpec(no + (tm, tn), gmap(lambda i, j, kk, off=off, po=po: po + (i, j + off))))
        else:
            assert not i_outer
            out_specs.append(pl.BlockSpec(no + (1, tn), gmap(lambda i, j, kk, off=off, po=po: po + (0, j + off))))
    n_t, n_e, n_o = len(terms), len(extras), len(outs)
    i_axis = 0 if i_outer else 1

    def body(*refs):
        ins = refs[: 2 * n_t]
        exs = refs[2 * n_t: 2 * n_t + n_e]
        ors = refs[2 * n_t + n_e: 2 * n_t + n_e + n_o]
        accs = refs[2 * n_t + n_e + n_o:]
        i = pl.program_id(i_axis)
        kk = pl.program_id(2)

        def partials(cs):
            part = [None] * n_acc
            for t, (_, _, mode, ai, _, _) in enumerate(terms):
                b_ref = ins[2 * t + 1]
                b_val = b_ref[cs, :] if mode == "nt" else b_ref[:, cs]
                p = _dot(ins[2 * t][...], b_val, mode)
                part[ai] = p if part[ai] is None else part[ai] + p
            return part

        def finish(vals, cs):
            res = epilogue(vals, [e[:, cs] for e in exs], i * tm)
            for (_, dtype, kind, _, _), o_ref, v in zip(outs, ors, res):
                if kind == "mn":
                    o_ref[:, cs] = v.astype(dtype)
                else:
                    @pl.when(i == 0)
                    def _():
                        o_ref[:, cs] = v.astype(dtype)

                    @pl.when(i != 0)
                    def _():
                        o_ref[:, cs] += v.astype(dtype)

        if nk == 1:
            w = tn // nsub
            for s in range(nsub):
                cs = slice(s * w, (s + 1) * w)
                finish(partials(cs), cs)
        else:
            part = partials(slice(None))
            @pl.when(kk == 0)
            def _():
                for a_ref, p in zip(accs, part):
                    a_ref[...] = p

            @pl.when(kk != 0)
            def _():
                for a_ref, p in zip(accs, part):
                    a_ref[...] += p

            @pl.when(kk == nk - 1)
            def _():
                finish([a_ref[...] for a_ref in accs], slice(None))

    scratch = [pltpu.VMEM((tm, tn), F32) for _ in range(n_acc)] if nk > 1 else []
    grid = (gm, gn, nk) if i_outer else (gn, gm, nk)
    res = pl.pallas_call(
        body, name=name, grid=grid, in_specs=in_specs, out_specs=out_specs, out_shape=out_shape,
        scratch_shapes=scratch, compiler_params=_cparams(vmem),
    )(*args)
    return res


def _term(a, b, mode, acc=0, pa=(), pb=()):
    return (a, b, mode, acc, tuple(pa), tuple(pb))


def _row_tile(lp):
    return _pick(lp, (384, 256, 128))


def _rmsnorm_fwd(h, gain):
    lp = h.shape[0]
    tr = _row_tile(lp)

    def body(h_ref, g_ref, z_ref):
        x = h_ref[...]
        r = lax.rsqrt(jnp.mean(x * x, axis=-1, keepdims=True) + EPS)
        z_ref[...] = (x * r * g_ref[...]).astype(BF16)

    return pl.pallas_call(
        body, name="rmsnorm_fwd", grid=(lp // tr,),
        in_specs=[pl.BlockSpec((tr, D), lambda i: (i, 0)), pl.BlockSpec((1, D), lambda i: (0, 0))],
        out_specs=pl.BlockSpec((tr, D), lambda i: (i, 0)),
        out_shape=jax.ShapeDtypeStruct((lp, D), BF16),
    )(h, gain)


def _loss_head(h, target, gain):
    lp = h.shape[0]
    tr = 128

    def body(h_ref, t_ref, g_ref, loss_ref, dh_ref, dhb_ref, dg_ref):
        i = pl.program_id(0)

        @pl.when(i == 0)
        def _():
            loss_ref[...] = jnp.zeros_like(loss_ref)
            dg_ref[...] = jnp.zeros_like(dg_ref)
            dh_ref[...] = jnp.zeros_like(dh_ref)
            dhb_ref[...] = jnp.zeros_like(dhb_ref)

        @pl.when(i != 0)
        def _():
            x = h_ref[...]
            g = g_ref[...]
            r = lax.rsqrt(jnp.mean(x * x, axis=-1, keepdims=True) + EPS)
            xh = x * r
            y = xh * g
            err = y - t_ref[...]
            loss_ref[...] += 0.5 * jnp.sum(jnp.sum(err * err, axis=-1, keepdims=True), axis=0, keepdims=True) / D
            dy = err * (1.0 / D)
            dg_ref[...] += jnp.sum(dy * xh, axis=0, keepdims=True)
            dxh = dy * g
            dx = r * (dxh - xh * jnp.mean(dxh * xh, axis=-1, keepdims=True))
            dh_ref[...] = dx
            dhb_ref[...] = dx.astype(BF16)

    row = pl.BlockSpec((tr, D), lambda i: (i, 0))
    vec = pl.BlockSpec((1, D), lambda i: (0, 0))
    return pl.pallas_call(
        body, name="loss_head", grid=(lp // tr,),
        in_specs=[row, pl.BlockSpec((tr, D), lambda i: (jnp.maximum(i - 1, 0), 0)), vec],
        out_specs=[pl.BlockSpec((1, 1), lambda i: (0, 0)), row, row, vec],
        out_shape=[jax.ShapeDtypeStruct((1, 1), F32), jax.ShapeDtypeStruct((lp, D), F32),
                   jax.ShapeDtypeStruct((lp, D), BF16), jax.ShapeDtypeStruct((1, D), F32)],
    )(h, target, gain)


def _silu_parts(g):
    s = _sigmoid(g)
    return g * s, s * (1.0 + g * (1.0 - s))


def _residual_norm_epi(scale, with_norm):
    def epi(accs, exs, row0):
        h2 = exs[0] + scale * accs[0]
        if not with_norm:
            return [h2]
        r = lax.rsqrt(jnp.mean(h2 * h2, axis=-1, keepdims=True) + EPS)
        return [h2, h2 * r * exs[1]]
    return epi


def _norm_bwd_epi(accs, exs, row0):
    dz = accs[0]
    x, res, g = exs
    r = lax.rsqrt(jnp.mean(x * x, axis=-1, keepdims=True) + EPS)
    xh = x * r
    dxh = dz * g
    dx = r * (dxh - xh * jnp.mean(dxh * xh, axis=-1, keepdims=True))
    rows = row0 + lax.broadcasted_iota(jnp.int32, (dz.shape[0], 1), 0)
    dh = jnp.where(rows >= NULL, res + dx, 0.0)
    return [dh, dh, jnp.sum(dz * xh, axis=0, keepdims=True)]


def _norm_bwd_outs(lp):
    return [((lp, D), F32, "mn", 0, ()), ((lp, D), BF16, "mn", 0, ()), ((1, D), F32, "nsum", 0, ())]


def _ffn_fwd(h, z, wg_t, wu_t, wd, pre, next_gain):
    lp = h.shape[0]
    tm = _tm(lp)

    def up_epi(accs, exs, row0):
        g, u = accs
        sg, _ = _silu_parts(g)
        return [g, u, sg * u]

    bshape = (lp, DFF)
    g_, u_, act = _mm("ffn_up", lp, DFF, [_term(z, wg_t, "nt", 0, (), pre), _term(z, wu_t, "nt", 1, (), pre)],
                      [(bshape, BF16, "mn", 0, ())] * 3, up_epi, tm=tm, tn=DFF, nsub=DFF // MXU_N, vmem=VMEM_BIG)

    with_norm = next_gain is not None
    res = _mm("ffn_down", lp, D, [_term(act, wd, "nn", 0, (), pre)],
              [((lp, D), F32, "mn", 0, ())] + ([((lp, D), BF16, "mn", 0, ())] if with_norm else []),
              _residual_norm_epi(0.5, with_norm),
              extras=[(h, "mn", 0, ())] + ([(next_gain, "n", 0, ())] if with_norm else []),
              tm=tm, tn=D, i_outer=True, vmem=VMEM_BIG)
    return res[0], (res[1] if with_norm else None), dict(h=h, z=z, g=g_, u=u_, act=act)


def _dw(name, a, b, m, n, scale=1.0):
    lp = a.shape[0]
    tm = _pick(m, (2944, 1408, 1024, 512, 256, 128))
    tn = _pick(n, (1024, 512, 256, 128))
    nk = lp // _pick(lp, (2112, 256, 128) if tm <= 1408 else (1056, 256, 128))

    def epi(accs, exs, row0):
        return [accs[0] * scale]

    (w,) = _mm(name, m, n, [_term(a, b, "tn")], [((m, n), BF16, "mn", 0, ())], epi, tm=tm, tn=tn, nk=nk,
               i_outer=True, vmem=VMEM_BIG)
    return w


def _ffn_bwd(dh, dhb, sv, gain, wg_t, wu_t, wd, pre):
    lp = dh.shape[0]
    tm = _tm(lp)

    def dact_epi(accs, exs, row0):
        g = exs[0].astype(F32)
        u = exs[1].astype(F32)
        da = 0.5 * accs[0]
        sg, dsg = _silu_parts(g)
        return [da * u * dsg, da * sg]

    dg_, du_ = _mm("ffn_dact", lp, DFF, [_term(dhb, wd, "nt", 0, (), pre)],
                   [((lp, DFF), BF16, "mn", 0, ())] * 2, dact_epi,
                   extras=[(sv["g"], "mn", 0, ()), (sv["u"], "mn", 0, ())], tm=tm, tn=DFF, nsub=DFF // MXU_N,
                   vmem=VMEM_BIG)
    d_wd = _dw("dw_down", sv["act"], dhb, DFF, D, 0.5)
    d_wg = _dw("dw_gate", dg_, sv["z"], DFF, D)
    d_wu = _dw("dw_up", du_, sv["z"], DFF, D)

    nk = 1
    dh2, dhb2, dgain = _mm("ffn_dz", lp, D, [_term(dg_, wg_t, "nn", 0, (), pre), _term(du_, wu_t, "nn", 0, (), pre)],
                           _norm_bwd_outs(lp), _norm_bwd_epi,
                           extras=[(sv["h"], "mn", 0, ()), (dh, "mn", 0, ()), (gain, "n", 0, ())],
                           tm=tm, tn=D, nk=nk, vmem=VMEM_BIG)
    return dh2, dhb2, dgain, d_wg, d_wu, d_wd


def _gla_gates(hin, w2p, b2p):
    lp = hin.shape[0]
    tr = _row_tile(lp)
    bf, bb = P_OFF["lrf"] // HP, P_OFF["lrb"] // HP

    def body(lf_ref, lb_ref, w_ref, b_ref, o_ref, c_ref):
        i = pl.program_id(0)
        rows = i * tr + lax.broadcasted_iota(jnp.int32, (tr, 1), 0)
        r = lax.broadcasted_iota(jnp.int32, (tr, tr), 0)
        c = lax.broadcasted_iota(jnp.int32, (tr, tr), 1)
        same = (r // CHUNK) == (c // CHUNK)
        for d, l_ref in enumerate((lf_ref, lb_ref)):
            logit = _dot3(l_ref[...].astype(F32), w_ref[d]) + b_ref[d]
            g = jnp.where(rows >= NULL, jax.nn.log_sigmoid(logit) * (1.0 / GLA_TAU), 0.0)
            o_ref[d] = g
            tmat = jnp.where(same & ((r >= c) if d == 0 else (r <= c)), 1.0, 0.0).astype(BF16)
            c_ref[d] = _dot_sel(tmat, g)

    spec = pl.BlockSpec((2, tr, 512), lambda i: (0, i, 0))
    return pl.pallas_call(
        body, name="gla_gates", grid=(lp // tr,),
        in_specs=[pl.BlockSpec((tr, HP), lambda i: (i, bf)), pl.BlockSpec((tr, HP), lambda i: (i, bb)),
                  pl.BlockSpec((2, HP, 512), lambda i: (0, 0, 0)), pl.BlockSpec((2, 1, 512), lambda i: (0, 0, 0))],
        out_specs=[spec, spec],
        out_shape=[jax.ShapeDtypeStruct((2, lp, 512), F32)] * 2,
    )(hin, hin, w2p, b2p)


def _gla_rows(lp):
    return _pick(lp, (384, 256, 128))


def _tri(d):
    r = lax.broadcasted_iota(jnp.int32, (CHUNK, CHUNK), 0)
    c = lax.broadcasted_iota(jnp.int32, (CHUNK, CHUNK), 1)
    return (r >= c) if d == 0 else (r <= c)


def _gla_fwd(hin, gates):
    lp = hin.shape[0]
    rb = _gla_rows(lp)
    nb = lp // rb
    cpb = rb // CHUNK
    nchunk = lp // CHUNK
    qo, ko, vo = P_OFF["qa"] // 512, P_OFF["ka"] // 512, P_OFF["va"] // 512
    scale = GLA_DK ** -0.5

    def body(qf, kf, vf, gf, qb, kb, vb_, gb, of, ob, sf, sb, st):
        @pl.when(pl.program_id(0) == 0)
        def _():
            st[...] = jnp.zeros_like(st)

        ins = ((qf, kf, vf, gf, of, sf), (qb, kb, vb_, gb, ob, sb))
        for ci in range(cpb):
            for d in range(2):
                q_ref, k_ref, v_ref, g_ref, o_ref, s_ref = ins[d]
                tri = _tri(d)
                c = ci if d == 0 else cpb - 1 - ci
                rows = slice(c * CHUNK, (c + 1) * CHUNK)
                for h in range(GLA_H):
                    sl = slice(h * HP, (h + 1) * HP)
                    q = q_ref[rows, sl].astype(F32) * scale
                    k = k_ref[rows, sl].astype(F32)
                    v = v_ref[rows, sl].astype(F32)
                    b = g_ref[rows, sl]
                    btot = b[CHUNK - 1:CHUNK] if d == 0 else b[0:1]
                    qd = (q * jnp.exp(b)).astype(BF16)
                    ki = (k * jnp.exp(-b)).astype(BF16)
                    ke = (k * jnp.exp(btot - b)).astype(BF16)
                    vb = v.astype(BF16)
                    att = jnp.where(tri, _dot(qd, ki, "nt"), 0.0)
                    s_prev = st[d, h]
                    o_ref[rows, sl] = _dot(att.astype(BF16), vb) + _dot(qd, s_prev.astype(BF16), "nt")
                    s_ref[h, c] = s_prev
                    st[d, h] = s_prev * jnp.exp(btot) + _dot(vb, ke, "tn")

    def specs(off):
        return (pl.BlockSpec((rb, 512), lambda b: (b, off)), pl.BlockSpec((rb, 512), lambda b: (nb - 1 - b, off)))

    (qf, qb), (kf, kb), (vf, vb2) = specs(qo), specs(ko), specs(vo)
    gf = pl.BlockSpec((None, rb, 512), lambda b: (0, b, 0))
    gb = pl.BlockSpec((None, rb, 512), lambda b: (1, nb - 1 - b, 0))
    of, ob = specs(0)
    sf = pl.BlockSpec((GLA_H, cpb, HP, HP), lambda b: (0, b, 0, 0))
    sb = pl.BlockSpec((GLA_H, cpb, HP, HP), lambda b: (0, nb - 1 - b, 0, 0))
    osh = jax.ShapeDtypeStruct((lp, GLA_H * HP), F32)
    ssh = jax.ShapeDtypeStruct((GLA_H, nchunk, HP, HP), F32)
    return pl.pallas_call(
        body, name="gla_fwd", grid=(nb,),
        in_specs=[qf, kf, vf, gf, qb, kb, vb2, gb], out_specs=[of, ob, sf, sb], out_shape=[osh, osh, ssh, ssh],
        scratch_shapes=[pltpu.VMEM((2, GLA_H, HP, HP), F32)], compiler_params=_cparams(VMEM_BIG),
    )(hin, hin, hin, gates, hin, hin, hin, gates)


def _gla_bwd(hin, gates, states, do):
    lp = hin.shape[0]
    rb = _gla_rows(lp)
    nb = lp // rb
    cpb = rb // CHUNK
    qo, ko, vo = P_OFF["qa"] // 512, P_OFF["ka"] // 512, P_OFF["va"] // 512
    scale = GLA_DK ** -0.5

    def body(qf, kf, vf, gf, sf, dof, qb, kb, vb_, gb, sb, dob,
             dqf, dkf, dvf, dgf, dqb, dkb, dvb, dgb, dst):
        @pl.when(pl.program_id(0) == 0)
        def _():
            dst[...] = jnp.zeros_like(dst)

        ins = ((qf, kf, vf, gf, sf, dof, dqf, dkf, dvf, dgf), (qb, kb, vb_, gb, sb, dob, dqb, dkb, dvb, dgb))
        for ci in range(cpb):
            for d in range(2):
                q_ref, k_ref, v_ref, g_ref, s_ref, do_ref, dq_ref, dk_ref, dv_ref, dg_ref = ins[d]
                tri, tri_t = _tri(d), _tri(1 - d)
                edge = lax.broadcasted_iota(jnp.int32, (CHUNK, 1), 0) == (CHUNK - 1 if d == 0 else 0)
                c = cpb - 1 - ci if d == 0 else ci
                rows = slice(c * CHUNK, (c + 1) * CHUNK)
                for h in range(GLA_H):
                    sl = slice(h * HP, (h + 1) * HP)
                    q = q_ref[rows, sl].astype(F32) * scale
                    k = k_ref[rows, sl].astype(F32)
                    v = v_ref[rows, sl].astype(F32)
                    dout = do_ref[rows, sl].astype(BF16)
                    b = g_ref[rows, sl]
                    btot = b[CHUNK - 1:CHUNK] if d == 0 else b[0:1]
                    e = jnp.exp(b)
                    ei = jnp.exp(-b)
                    et = jnp.exp(btot - b)
                    etot = jnp.exp(btot)
                    qd = q * e
                    ki = k * ei
                    ke = k * et
                    qdb, kib, keb, vb = qd.astype(BF16), ki.astype(BF16), ke.astype(BF16), v.astype(BF16)
                    att_t = jnp.where(tri_t, _dot(kib, qdb, "nt"), 0.0).astype(BF16)
                    d_att = jnp.where(tri, _dot(dout, vb, "nt"), 0.0).astype(BF16)
                    d_att_t = jnp.where(tri_t, _dot(vb, dout, "nt"), 0.0).astype(BF16)
                    s_prev = s_ref[h, c]
                    ds_t = dst[d, h]
                    ds_b = ds_t.astype(BF16)
                    dv = _dot(att_t, dout) + _dot(keb, ds_b, "nt")
                    d_qd = _dot(d_att, kib) + _dot(dout, s_prev.astype(BF16))
                    d_ki = _dot(d_att_t, qdb)
                    d_ke = _dot(vb, ds_b)
                    d_e = jnp.sum(s_prev * ds_t, axis=0, keepdims=True)
                    dst[d, h] = _dot(dout, qdb, "tn") + ds_t * etot
                    db = d_qd * qd - d_ki * ki - d_ke * ke
                    dbtot = jnp.sum(d_ke * ke, axis=0, keepdims=True) + d_e * etot
                    dq_ref[rows, sl] = (d_qd * e * scale).astype(BF16)
                    dk_ref[rows, sl] = (d_ki * ei + d_ke * et).astype(BF16)
                    dv_ref[rows, sl] = dv.astype(BF16)
                    dg_ref[rows, sl] = db + jnp.where(edge, dbtot, 0.0)

    def fw(off):
        return pl.BlockSpec((rb, 512), lambda b: (nb - 1 - b, off))

    def bw(off):
        return pl.BlockSpec((rb, 512), lambda b: (b, off))

    gf = pl.BlockSpec((None, rb, 512), lambda b: (0, nb - 1 - b, 0))
    gb = pl.BlockSpec((None, rb, 512), lambda b: (1, b, 0))
    sf = pl.BlockSpec((GLA_H, cpb, HP, HP), lambda b: (0, nb - 1 - b, 0, 0))
    sb = pl.BlockSpec((GLA_H, cpb, HP, HP), lambda b: (0, b, 0, 0))
    osh = jax.ShapeDtypeStruct((lp, GLA_H * HP), F32)
    osh_b = jax.ShapeDtypeStruct((lp, GLA_H * HP), BF16)
    res = pl.pallas_call(
        body, name="gla_bwd", grid=(nb,),
        in_specs=[fw(qo), fw(ko), fw(vo), gf, sf, fw(0), bw(qo), bw(ko), bw(vo), gb, sb, bw(0)],
        out_specs=[fw(0)] * 4 + [bw(0)] * 4, out_shape=[osh_b, osh_b, osh_b, osh] * 2,
        scratch_shapes=[pltpu.VMEM((2, GLA_H, HP, HP), F32)], compiler_params=_cparams(VMEM_BIG),
    )(hin, hin, hin, gates, states[0], do, hin, hin, hin, gates, states[1], do)
    return res[:4], res[4:]


def _gla_out_fwd(o2, hin, gn):
    lp = hin.shape[0]
    tr = _row_tile(lp)
    ro = P_OFF["ra"] // 512

    def body(of_ref, ob_ref, r_ref, gn_ref, a_ref):
        r = r_ref[...].astype(F32)
        sr, _ = _silu_parts(r)
        for h in range(GLA_H):
            sl = slice(h * HP, (h + 1) * HP)
            o = of_ref[:, sl] + ob_ref[:, sl]
            rs = lax.rsqrt(jnp.mean(o * o, axis=-1, keepdims=True) + EPS)
            a_ref[:, sl] = (o * rs * gn_ref[:, sl] * sr[:, sl]).astype(BF16)

    row = pl.BlockSpec((tr, 512), lambda i: (i, 0))
    return pl.pallas_call(
        body, name="gla_out_fwd", grid=(lp // tr,),
        in_specs=[row, row, pl.BlockSpec((tr, 512), lambda i: (i, ro)), pl.BlockSpec((1, 512), lambda i: (0, 0))],
        out_specs=row,
        out_shape=jax.ShapeDtypeStruct((lp, 512), BF16),
    )(o2[0], o2[1], hin, gn)


def _gla_out_bwd(da, o2, hin, gn):
    lp = hin.shape[0]
    tr = _row_tile(lp)
    ro = P_OFF["ra"] // 512

    def body(da_ref, of_ref, ob_ref, r_ref, gn_ref, do_ref, dr_ref, dgn_ref):
        i = pl.program_id(0)
        r = r_ref[...].astype(F32)
        sr, dsr = _silu_parts(r)
        da_v = da_ref[...]
        parts = []
        for h in range(GLA_H):
            sl = slice(h * HP, (h + 1) * HP)
            o = of_ref[:, sl] + ob_ref[:, sl]
            rs = lax.rsqrt(jnp.mean(o * o, axis=-1, keepdims=True) + EPS)
            oh = o * rs
            gn_h = gn_ref[:, sl]
            dah = da_v[:, sl]
            dr_ref[:, sl] = (dah * oh * gn_h * dsr[:, sl]).astype(BF16)
            t = dah * sr[:, sl]
            parts.append(jnp.sum(t * oh, axis=0, keepdims=True))
            doh = t * gn_h
            do_ref[:, sl] = rs * (doh - oh * jnp.mean(doh * oh, axis=-1, keepdims=True))
        part = jnp.concatenate(parts, axis=1)

        @pl.when(i == 0)
        def _():
            dgn_ref[...] = part

        @pl.when(i != 0)
        def _():
            dgn_ref[...] += part

    row = pl.BlockSpec((tr, 512), lambda i: (i, 0))
    return pl.pallas_call(
        body, name="gla_out_bwd", grid=(lp // tr,),
        in_specs=[row, row, row, pl.BlockSpec((tr, 512), lambda i: (i, ro)), pl.BlockSpec((1, 512), lambda i: (0, 0))],
        out_specs=[row, row, pl.BlockSpec((1, 512), lambda i: (0, 0))],
        out_shape=[jax.ShapeDtypeStruct((lp, 512), F32), jax.ShapeDtypeStruct((lp, 512), BF16),
                   jax.ShapeDtypeStruct((1, 512), F32)],
    )(da, o2[0], o2[1], hin, gn)


def _gla_in_bwd(gf, gb, gates, hin, w2p, others):
    lp = hin.shape[0]
    tr = _row_tile(lp)
    bf, bb = P_OFF["lrf"] // HP, P_OFF["lrb"] // HP
    names = tuple(others)

    def seg(name):
        return slice(P_OFF[name], P_OFF[name] + P_WIDTH[name])

    def body(dqf_ref, dkf_ref, dvf_ref, dgf_ref, dqb_ref, dkb_ref, dvb_ref, dgb_ref, g_ref, lf_ref, lb_ref, w_ref,
             *rest):
        other_refs, (o_ref, dw_ref, db_ref) = rest[:len(names)], rest[len(names):]
        i = pl.program_id(0)
        for n, ref in zip(names, other_refs):
            o_ref[:, seg(n)] = ref[...].astype(BF16)
        o_ref[:, seg("qa")] = (dqf_ref[...].astype(F32) + dqb_ref[...].astype(F32)).astype(BF16)
        o_ref[:, seg("ka")] = (dkf_ref[...].astype(F32) + dkb_ref[...].astype(F32)).astype(BF16)
        o_ref[:, seg("va")] = (dvf_ref[...].astype(F32) + dvb_ref[...].astype(F32)).astype(BF16)
        olr_ref = o_ref.at[:, P_OFF["lrf"]:P_OFF["lrf"] + 2 * HP]
        rows = i * tr + lax.broadcasted_iota(jnp.int32, (tr, 1), 0)
        r = lax.broadcasted_iota(jnp.int32, (tr, tr), 0)
        c = lax.broadcasted_iota(jnp.int32, (tr, tr), 1)
        same = (r // CHUNK) == (c // CHUNK)
        for d, (l_ref, dg_ref) in enumerate(((lf_ref, dgf_ref), (lb_ref, dgb_ref))):
            tmat = jnp.where(same & ((r <= c) if d == 0 else (r >= c)), 1.0, 0.0).astype(BF16)
            dg = _dot_sel(tmat, dg_ref[...])
            sig_neg = 1.0 - jnp.exp(GLA_TAU * g_ref[d])
            dlogit = jnp.where(rows >= NULL, dg * (1.0 / GLA_TAU) * sig_neg, 0.0)
            olr_ref[:, d * HP:(d + 1) * HP] = _dot3(dlogit, w_ref[d], "nt").astype(BF16)
            dw = _dot3(l_ref[...].astype(F32), dlogit, "tn")
            dbias = jnp.sum(dlogit, axis=0, keepdims=True)

            @pl.when(i == 0)
            def _():
                dw_ref[d] = dw
                db_ref[d] = dbias

            @pl.when(i != 0)
            def _():
                dw_ref[d] += dw
                db_ref[d] += dbias

    two = pl.BlockSpec((2, tr, 512), lambda i: (0, i, 0))
    row = pl.BlockSpec((tr, 512), lambda i: (i, 0))
    return pl.pallas_call(
        body, name="gla_in_bwd", grid=(lp // tr,),
        in_specs=[row] * 8 + [two, pl.BlockSpec((tr, HP), lambda i: (i, bf)),
                  pl.BlockSpec((tr, HP), lambda i: (i, bb)), pl.BlockSpec((2, HP, 512), lambda i: (0, 0, 0))] +
                 [pl.BlockSpec((tr, P_WIDTH[n]), lambda i: (i, 0)) for n in names],
        out_specs=[pl.BlockSpec((tr, D_INP), lambda i: (i, 0)),
                   pl.BlockSpec((2, HP, 512), lambda i: (0, 0, 0)), pl.BlockSpec((2, 1, 512), lambda i: (0, 0, 0))],
        out_shape=[jax.ShapeDtypeStruct((lp, D_INP), BF16), jax.ShapeDtypeStruct((2, HP, 512), F32),
                   jax.ShapeDtypeStruct((2, 1, 512), F32)],
        compiler_params=_cparams(VMEM_BIG),
    )(*gf, *gb, gates, hin, hin, w2p, *[others[n] for n in names])


def _rope_tables(lp):
    n_tok = lp - NULL - NMETA
    rows = n_tok // GRID_W
    row = np.repeat(np.arange(rows), GRID_W).astype(np.float32)
    col = np.tile(np.arange(GRID_W), rows).astype(np.float32)
    inv = (ROPE_THETA ** (-np.arange(0, 32, 2, dtype=np.float32) / 32)).astype(np.float32)
    ang = np.concatenate([row[:, None] * inv, col[:, None] * inv], axis=-1)
    ang = np.concatenate([np.zeros((NULL + NMETA, 32), np.float32), ang], axis=0)
    cos, sin = np.cos(ang).astype(np.float32), np.sin(ang).astype(np.float32)
    z16 = np.zeros((lp, 16), np.float32)
    z64 = np.zeros((lp, 64), np.float32)
    c = np.concatenate([cos[:, :16], cos[:, :16], cos[:, 16:], cos[:, 16:], z64], axis=1)
    a = np.concatenate([-sin[:, :16], z16, -sin[:, 16:], z16, z64], axis=1)
    b = np.concatenate([z16, sin[:, :16], z16, sin[:, 16:], z64], axis=1)
    return jnp.asarray(c), jnp.asarray(a), jnp.asarray(b)


def _rope(x, c, a, b):
    return x * c + pltpu.roll(x, HP - 16, 1) * a + pltpu.roll(x, 16, 1) * b


def _rope_t(dx, c, a, b):
    return dx * c + pltpu.roll(dx * a, 16, 1) + pltpu.roll(dx * b, HP - 16, 1)


def _attn_prep(hin, gq, gk, tabs):
    lp = hin.shape[0]
    tr = _row_tile(lp)
    qo, ko, vo = P_OFF["qb"] // 1024, P_OFF["kb"] // 256, P_OFF["vb"] // 256

    def body(q_ref, k_ref, v_ref, gq_ref, gk_ref, c_ref, a_ref, b_ref, oq_ref, ok_ref, ov_ref):
        c, a, b = c_ref[...], a_ref[...], b_ref[...]
        for src, g_ref, dst, nh, sc in ((q_ref, gq_ref, oq_ref, ATT_H, Q_SCALE), (k_ref, gk_ref, ok_ref, ATT_KV, 1.0)):
            for h in range(nh):
                sl = slice(h * HP, (h + 1) * HP)
                x = src[:, sl].astype(F32)
                r = lax.rsqrt(jnp.sum(x * x, axis=-1, keepdims=True) * (1.0 / HEAD_DIM) + EPS)
                dst[:, sl] = (_rope(x * r * g_ref[...], c, a, b) * sc).astype(BF16)
        lane = lax.broadcasted_iota(jnp.int32, (1, ATT_KV * HP), 1)
        ov_ref[...] = jnp.where(lane % HP == HEAD_DIM, 1.0, v_ref[...]).astype(BF16)

    tab = pl.BlockSpec((tr, HP), lambda i: (i, 0))
    vec = pl.BlockSpec((1, HP), lambda i: (0, 0))
    return pl.pallas_call(
        body, name="attn_prep", grid=(lp // tr,),
        in_specs=[pl.BlockSpec((tr, 1024), lambda i: (i, qo)), pl.BlockSpec((tr, 256), lambda i: (i, ko)),
                  pl.BlockSpec((tr, 256), lambda i: (i, vo)), vec, vec, tab, tab, tab],
        out_specs=[pl.BlockSpec((tr, 1024), lambda i: (i, 0)), pl.BlockSpec((tr, 256), lambda i: (i, 0)),
                   pl.BlockSpec((tr, 256), lambda i: (i, 0))],
        out_shape=[jax.ShapeDtypeStruct((lp, 1024), BF16), jax.ShapeDtypeStruct((lp, 256), BF16),
                   jax.ShapeDtypeStruct((lp, 256), BF16)],
    )(hin, hin, hin, gq, gk, *tabs)


def _attn_prep_bwd(dqr, dkr, hin, gq, gk, tabs):
    lp = hin.shape[0]
    tr = _row_tile(lp)
    qo, ko = P_OFF["qb"] // 1024, P_OFF["kb"] // 256

    def body(dq_ref, dk_ref, q_ref, k_ref, gq_ref, gk_ref, c_ref, a_ref, b_ref, oq_ref, ok_ref, dgq_ref, dgk_ref):
        i = pl.program_id(0)
        c, a, b = c_ref[...], a_ref[...], b_ref[...]
        for src, dsrc, g_ref, dst, dg_ref, nh, sc in (
                (q_ref, dq_ref, gq_ref, oq_ref, dgq_ref, ATT_H, Q_SCALE),
                (k_ref, dk_ref, gk_ref, ok_ref, dgk_ref, ATT_KV, 1.0)):
            acc = jnp.zeros((1, HP), F32)
            for h in range(nh):
                sl = slice(h * HP, (h + 1) * HP)
                x = src[:, sl].astype(F32)
                r = lax.rsqrt(jnp.sum(x * x, axis=-1, keepdims=True) * (1.0 / HEAD_DIM) + EPS)
                xh = x * r
                dxn = _rope_t(dsrc[:, sl] * sc, c, a, b)
                acc = acc + jnp.sum(dxn * xh, axis=0, keepdims=True)
                dxh = dxn * g_ref[...]
                dx = r * (dxh - xh * (jnp.sum(dxh * xh, axis=-1, keepdims=True) * (1.0 / HEAD_DIM)))
                dst[:, sl] = dx.astype(BF16)

            @pl.when(i == 0)
            def _():
                dg_ref[...] = acc

            @pl.when(i != 0)
            def _():
                dg_ref[...] += acc

    tab = pl.BlockSpec((tr, HP), lambda i: (i, 0))
    vec = pl.BlockSpec((1, HP), lambda i: (0, 0))
    return pl.pallas_call(
        body, name="attn_prep_bwd", grid=(lp // tr,),
        in_specs=[pl.BlockSpec((tr, 1024), lambda i: (i, 0)), pl.BlockSpec((tr, 256), lambda i: (i, 0)),
                  pl.BlockSpec((tr, 1024), lambda i: (i, qo)), pl.BlockSpec((tr, 256), lambda i: (i, ko)),
                  vec, vec, tab, tab, tab],
        out_specs=[pl.BlockSpec((tr, 1024), lambda i: (i, 0)), pl.BlockSpec((tr, 256), lambda i: (i, 0)), vec, vec],
        out_shape=[jax.ShapeDtypeStruct((lp, 1024), BF16), jax.ShapeDtypeStruct((lp, 256), BF16),
                   jax.ShapeDtypeStruct((1, HP), F32), jax.ShapeDtypeStruct((1, HP), F32)],
    )(dqr, dkr, hin, hin, gq, gk, *tabs)


QB = 128
GH = 2
Q_SCALE = HEAD_DIM ** -0.5 * math.log2(math.e)
LN2 = math.log(2.0)


def _stack(ref, g0, n):
    return jnp.concatenate([ref[:, (g0 + g) * HP:(g0 + g + 1) * HP] for g in range(n)], axis=0)


def _attn_fwd(qr, kr, vb):
    lp = qr.shape[0]
    nq = lp // QB

    def body(q_ref, k_ref, v_ref, o_ref, lse_ref):
        qb = pl.program_id(1)
        keys = lax.broadcasted_iota(jnp.int32, (1, lp), 1)
        lane = lax.broadcasted_iota(jnp.int32, (1, HP), 1)
        rows = qb * QB + lax.broadcasted_iota(jnp.int32, (QB, 1), 0)
        for ch in range(ATT_G // GH):
            qs = _stack(q_ref, ch * GH, GH)
            s = _dot(qs, k_ref[...], "nt")
            s = jnp.where(keys >= NULL, s, -1e30)
            m = jnp.max(s, axis=-1, keepdims=True)
            p = jnp.exp2(s - m).astype(BF16)
            o_raw = _dot(p, v_ref[...])
            l = jnp.sum(jnp.where(lane == HEAD_DIM, o_raw, 0.0), axis=-1, keepdims=True)
            o = jnp.where(lane < HEAD_DIM, o_raw / l, 0.0)
            lse = m + jnp.log2(l)
            for g in range(GH):
                sl = slice((ch * GH + g) * HP, (ch * GH + g + 1) * HP)
                o_ref[:, sl] = jnp.where(rows >= NULL, o[g * QB:(g + 1) * QB], 0.0).astype(BF16)
                lse_ref[:, sl] = jnp.broadcast_to(lse[g * QB:(g + 1) * QB], (QB, HP))

    qspec = pl.BlockSpec((QB, ATT_G * HP), lambda kv, qb: (qb, kv))
    kspec = pl.BlockSpec((lp, HP), lambda kv, qb: (0, kv))
    return pl.pallas_call(
        body, name="attn_fwd", grid=(ATT_KV, nq),
        in_specs=[qspec, kspec, kspec], out_specs=[qspec, qspec],
        out_shape=[jax.ShapeDtypeStruct((lp, ATT_H * HP), BF16), jax.ShapeDtypeStruct((lp, ATT_H * HP), F32)],
        compiler_params=_cparams(VMEM_BIG),
    )(qr, kr, vb)


def _attn_bwd(qr, kr, vb, o, lse, do):
    lp = qr.shape[0]
    nq = lp // QB

    def body(q_ref, k_ref, v_ref, o_ref, lse_ref, do_ref, dq_ref, dk_ref, dv_ref):
        qb = pl.program_id(1)

        @pl.when(qb == 0)
        def _():
            dk_ref[...] = jnp.zeros_like(dk_ref)
            dv_ref[...] = jnp.zeros_like(dv_ref)

        keys = lax.broadcasted_iota(jnp.int32, (1, lp), 1)
        k = k_ref[...]
        dk_acc, dv_acc = None, None
        for ch in range(ATT_G // GH):
            g0 = ch * GH
            qs = _stack(q_ref, g0, GH)
            dos = _stack(do_ref, g0, GH)
            os_ = _stack(o_ref, g0, GH).astype(F32)
            lse_s = jnp.concatenate([lse_ref[:, (g0 + g) * HP:(g0 + g) * HP + 1] for g in range(GH)], axis=0)
            delta = jnp.sum(dos * os_, axis=-1, keepdims=True) * LN2
            s = _dot(qs, k, "nt")
            p = jnp.where(keys >= NULL, jnp.exp2(s - lse_s), 0.0)
            dob = dos.astype(BF16)
            dp = _dot((dos * LN2).astype(BF16), v_ref[...], "nt")
            ds = (p * (dp - delta)).astype(BF16)
            dq = _dot(ds, k)
            for g in range(GH):
                dq_ref[:, (g0 + g) * HP:(g0 + g + 1) * HP] = dq[g * QB:(g + 1) * QB]
            dv_c = _dot(p.astype(BF16), dob, "tn")
            dk_c = _dot(ds, qs, "tn")
            dv_acc = dv_c if dv_acc is None else dv_acc + dv_c
            dk_acc = dk_c if dk_acc is None else dk_acc + dk_c
        dv_ref[...] += dv_acc
        dk_ref[...] += dk_acc

    qspec = pl.BlockSpec((QB, ATT_G * HP), lambda kv, qb: (qb, kv))
    kspec = pl.BlockSpec((lp, HP), lambda kv, qb: (0, kv))
    return pl.pallas_call(
        body, name="attn_bwd", grid=(ATT_KV, nq),
        in_specs=[qspec, kspec, kspec, qspec, qspec, qspec], out_specs=[qspec, kspec, kspec],
        out_shape=[jax.ShapeDtypeStruct((lp, ATT_H * HP), F32), jax.ShapeDtypeStruct((lp, ATT_KV * HP), F32),
                   jax.ShapeDtypeStruct((lp, ATT_KV * HP), F32)],
        compiler_params=_cparams(VMEM_BIG),
    )(qr, kr, vb, o, lse, do)


def _mixer_fwd(h, z, wl, l, tabs, next_gain):
    lp = h.shape[0]
    tm = _tm(lp)

    def id_epi(accs, exs, row0):
        return [accs[0]]

    (hin,) = _mm("in_proj", lp, D_INP, [_term(z, wl["win_t"], "nt", 0, (), (l,))], [((lp, D_INP), BF16, "mn", 0, ())],
                 id_epi, tm=tm, tn=D_INP // 2, vmem=VMEM_BIG)
    gates, cum = _gla_gates(hin, wl["w2p"][l], wl["b2p"][l])
    o_f, o_b, s_f, s_b = _gla_fwd(hin, cum)
    o2, states = (o_f, o_b), (s_f, s_b)
    a = _gla_out_fwd(o2, hin, wl["gn"][l])
    qr, kr, vb = _attn_prep(hin, wl["gq"][l], wl["gk"][l], tabs)
    b, lse = _attn_fwd(qr, kr, vb)

    def merge_epi(accs, exs, row0):
        pa, pb = accs
        ga, gb, bma, bmb = exs
        y = _sigmoid(ga + bma) * pa + _sigmoid(gb + bmb) * pb
        return [y, pa, pb]

    y, pa, pb = _mm("merge", lp, D, [_term(a, wl["wpa_t"], "nt", 0, (), (l,)), _term(b, wl["wpb_t"], "nt", 1, (), (l,))],
                    [((lp, D), BF16, "mn", 0, ())] * 3, merge_epi,
                    extras=[(hin, "mn", P_OFF["ga"] // D, ()), (hin, "mn", P_OFF["gb"] // D, ()),
                            (wl["bm"], "n", 0, (l, 0)), (wl["bm"], "n", 0, (l, 1))],
                    tm=tm, tn=D, nsub=D // MXU_N, i_outer=True, vmem=VMEM_BIG)

    h2, z2 = _mm("out_proj", lp, D, [_term(y, wl["wout"], "nn", 0, (), (l,))],
                 [((lp, D), F32, "mn", 0, ()), ((lp, D), BF16, "mn", 0, ())], _residual_norm_epi(1.0, True),
                 extras=[(h, "mn", 0, ()), (next_gain, "n", 0, ())], tm=tm, tn=D, i_outer=True, vmem=VMEM_BIG)
    sv = dict(h=h, z=z, hin=hin, gates=gates, cum=cum, o2=o2, states=states, a=a, qr=qr, kr=kr, vb=vb, b=b, lse=lse,
              y=y, pa=pa, pb=pb)
    return h2, z2, sv


def _mixer_bwd(dh, dhb, sv, gain, wl, l, tabs):
    lp = dh.shape[0]
    tm = _tm(lp)
    hin = sv["hin"]

    def merge_bwd_epi(accs, exs, row0):
        dy = accs[0]
        ga, gb, pa, pb, bma, bmb = exs
        sa = _sigmoid(ga + bma)
        sb = _sigmoid(gb + bmb)
        dga = dy * pa.astype(F32) * sa * (1.0 - sa)
        dgb = dy * pb.astype(F32) * sb * (1.0 - sb)
        return [dy * sa, dy * sb, dga, dgb, jnp.sum(dga, axis=0, keepdims=True), jnp.sum(dgb, axis=0, keepdims=True)]

    big = ((lp, D), BF16, "mn", 0, ())
    vec = ((1, D), F32, "nsum", 0, ())
    dpa, dpb, dga, dgb, dbma, dbmb = _mm(
        "merge_bwd", lp, D, [_term(dhb, wl["wout"], "nt", 0, (), (l,))], [big, big, big, big, vec, vec], merge_bwd_epi,
        extras=[(hin, "mn", P_OFF["ga"] // D, ()), (hin, "mn", P_OFF["gb"] // D, ()), (sv["pa"], "mn", 0, ()),
                (sv["pb"], "mn", 0, ()), (wl["bm"], "n", 0, (l, 0)), (wl["bm"], "n", 0, (l, 1))],
        tm=tm, tn=D, nsub=D // MXU_N, vmem=VMEM_BIG)
    d_wout = _dw("dw_out", sv["y"], dhb, D, D)
    d_wpa_t = _dw("dw_pa", dpa, sv["a"], D, 512)
    d_wpb_t = _dw("dw_pb", dpb, sv["b"], D, ATT_H * HP)

    def id_epi(accs, exs, row0):
        return [accs[0]]

    (da,) = _mm("d_a", lp, 512, [_term(dpa, wl["wpa_t"], "nn", 0, (), (l,))], [((lp, 512), F32, "mn", 0, ())], id_epi,
                tm=tm, tn=512, i_outer=True, vmem=VMEM_BIG)
    (db,) = _mm("d_b", lp, ATT_H * HP, [_term(dpb, wl["wpb_t"], "nn", 0, (), (l,))],
                [((lp, ATT_H * HP), F32, "mn", 0, ())], id_epi, tm=tm, tn=512, i_outer=True, vmem=VMEM_BIG)
    d_o, d_ra, d_gn = _gla_out_bwd(da, sv["o2"], hin, wl["gn"][l])
    g_fw, g_bw = _gla_bwd(hin, sv["cum"], sv["states"], d_o)
    dqr, dkr, dvb = _attn_bwd(sv["qr"], sv["kr"], sv["vb"], sv["b"], sv["lse"], db)
    d_qb, d_kb, d_gq, d_gk = _attn_prep_bwd(dqr, dkr, hin, wl["gq"][l], wl["gk"][l], tabs)
    dhin, d_w2p, d_b2p = _gla_in_bwd(g_fw, g_bw, sv["gates"], hin, wl["w2p"][l],
                                     dict(qb=d_qb, ga=dga, gb=dgb, ra=d_ra, kb=d_kb, vb=dvb))
    d_win_t = _dw("dw_in", dhin, sv["z"], D_INP, D)
    dh2, dhb2, dgain = _mm("in_proj_dz", lp, D, [_term(dhin, wl["win_t"], "nn", 0, (), (l,))], _norm_bwd_outs(lp),
                           _norm_bwd_epi, extras=[(sv["h"], "mn", 0, ()), (dh, "mn", 0, ()), (gain, "n", 0, ())],
                           tm=tm, tn=D, nk=2, vmem=VMEM_BIG)
    grads = dict(gain=dgain, wout=d_wout, wpa_t=d_wpa_t, wpb_t=d_wpb_t, win_t=d_win_t, gn=d_gn, w2p=d_w2p, b2p=d_b2p,
                 gq=d_gq, gk=d_gk, bma=dbma, bmb=dbmb)
    return dh2, dhb2, grads


def _mesh_pos():
    x, y, c = lax.axis_index("x"), lax.axis_index("y"), lax.axis_index("c")
    chips = [(1 - x, y), (x, 1 - y), (1 - x, 1 - y)]
    return x, y, c, chips


def _dev_index(x, y, c):
    return 4 * x + 2 * y + c


def _all_gather(name, shards, leads):
    nt = len(shards)

    def blk(ref, lead, idx):
        return ref.at[(slice(None),) * lead + (idx,)]

    def body(*refs):
        xs, outs = refs[:nt], refs[nt:2 * nt]
        send_sems, recv_sems, local_sems = refs[2 * nt:]
        x, y, c, chips = _mesh_pos()
        me, sibling = (x, y, c), (x, y, 1 - c)

        def copy(t, k, block, to, own=False):
            dst = blk(outs[t], leads[t], _dev_index(*block))
            return pltpu.make_async_remote_copy(
                src_ref=xs[t] if own else dst, dst_ref=dst, send_sem=send_sems.at[t, k], recv_sem=recv_sems.at[t, k],
                device_id=to, device_id_type=MESH)

        locals_ = [pltpu.make_async_copy(xs[t], blk(outs[t], leads[t], _dev_index(*me)), local_sems.at[t])
                   for t in range(nt)]
        for cp in locals_:
            cp.start()
        first = []
        for t in range(nt):
            first.append(copy(t, 0, me, sibling, own=True))
            first += [copy(t, 1 + j, me, (*chip, c), own=True) for j, chip in enumerate(chips)]
        for cp in first:
            cp.start()
        passed = []
        for j, chip in enumerate(chips):
            for t in range(nt):
                copy(t, 1 + j, (*chip, c), me).wait_recv()
                fw = copy(t, 4 + j, (*chip, c), sibling)
                fw.start()
                passed.append(fw)
        for t in range(nt):
            copy(t, 0, sibling, me).wait_recv()
        for j, chip in enumerate(chips):
            for t in range(nt):
                copy(t, 4 + j, (*chip, 1 - c), me).wait_recv()
        for cp in first + passed:
            cp.wait_send()
        for cp in locals_:
            cp.wait()

    out_shape = [jax.ShapeDtypeStruct(s.shape[:ld] + (NDEV,) + s.shape[ld:], s.dtype) for s, ld in zip(shards, leads)]
    return pl.pallas_call(
        body, name=name, in_specs=[ANY] * nt, out_specs=[ANY] * nt, out_shape=out_shape,
        scratch_shapes=[pltpu.SemaphoreType.DMA((nt, 7)), pltpu.SemaphoreType.DMA((nt, 7)),
                        pltpu.SemaphoreType.DMA((nt,))],
    )(*shards)


def _exchange_sibling(name, gs):
    nt = len(gs)

    def body(*refs):
        xs, outs = refs[:nt], refs[nt:2 * nt]
        send_sems, recv_sems = refs[2 * nt:]
        x, y, c, _ = _mesh_pos()
        sibling = (x, y, 1 - c)
        copies = []
        for t in range(nt):
            for ch in range(4):
                copies.append(pltpu.make_async_remote_copy(
                    src_ref=xs[t].at[2 * ch + (1 - c)], dst_ref=outs[t].at[ch],
                    send_sem=send_sems.at[t, ch], recv_sem=recv_sems.at[t, ch],
                    device_id=sibling, device_id_type=MESH))
        for cp in copies:
            cp.start()
        for cp in copies:
            cp.wait()

    out_shape = [jax.ShapeDtypeStruct((4,) + g.shape[1:], g.dtype) for g in gs]
    return pl.pallas_call(
        body, name=name, in_specs=[ANY] * nt, out_specs=[ANY] * nt, out_shape=out_shape,
        scratch_shapes=[pltpu.SemaphoreType.DMA((nt, 4)), pltpu.SemaphoreType.DMA((nt, 4))],
    )(*gs)


def _pair_sum(name, gs, recv):
    c = lax.axis_index("c")
    outs = []
    for t, (g, rv) in enumerate(zip(gs, recv)):
        _, r, cols = rv.shape

        def body(c_ref, g_ref, r_ref, o_ref):
            o_ref[...] = (g_ref[...].astype(F32) + r_ref[...].astype(F32)).astype(o_ref.dtype)

        outs.append(pl.pallas_call(
            body, name=f"{name}_{t}",
            grid_spec=pltpu.PrefetchScalarGridSpec(
                num_scalar_prefetch=1, grid=(4,),
                in_specs=[pl.BlockSpec((None, r, cols), lambda ch, cr: (2 * ch + cr[0], 0, 0)),
                          pl.BlockSpec((None, r, cols), lambda ch, cr: (ch, 0, 0))],
                out_specs=pl.BlockSpec((None, r, cols), lambda ch, cr: (ch, 0, 0))),
            out_shape=jax.ShapeDtypeStruct(rv.shape, rv.dtype),
        )(jnp.reshape(c, (1,)).astype(jnp.int32), g, rv))
    return outs


def _final_sum(name, ps, recv, transposed):
    chip = 2 * lax.axis_index("x") + lax.axis_index("y")
    outs = []
    for t, (p, rv) in enumerate(zip(ps, recv)):
        _, r, cols = rv.shape
        tr_out = transposed[t]
        oshape = (cols, r) if tr_out else (r, cols)

        def body(c_ref, p_ref, r0_ref, r1_ref, r2_ref, o_ref):
            acc = ((p_ref[...].astype(F32) + r0_ref[...].astype(F32)) + r1_ref[...].astype(F32)) + r2_ref[...].astype(F32)
            o_ref[...] = acc.T if tr_out else acc

        outs.append(pl.pallas_call(
            body, name=f"{name}_{t}",
            grid_spec=pltpu.PrefetchScalarGridSpec(
                num_scalar_prefetch=1, grid=(1,),
                in_specs=[pl.BlockSpec((None, r, cols), lambda i, cr: (cr[0], 0, 0))] +
                         [pl.BlockSpec((None, r, cols), lambda i, cr, j=j: (j, 0, 0)) for j in range(3)],
                out_specs=pl.BlockSpec(oshape, lambda i, cr: (0, 0))),
            out_shape=jax.ShapeDtypeStruct(oshape, F32),
        )(jnp.reshape(chip, (1,)).astype(jnp.int32), p, rv, rv, rv))
    return outs


def _sum_gathered(g):
    _, r, cols = g.shape

    def body(g_ref, o_ref):
        acc = g_ref[0]
        for d in range(1, NDEV):
            acc = acc + g_ref[d]
        o_ref[...] = acc

    return pl.pallas_call(body, name="small_sum", out_shape=jax.ShapeDtypeStruct((r, cols), F32))(g)


HBM = pl.BlockSpec(memory_space=pltpu.HBM)
SEM = pl.BlockSpec(memory_space=pltpu.SEMAPHORE)
EFFECT = pltpu.SideEffectType.DATAFLOW_SIDE_EFFECTING
NREL = NDEV - 1


def _related(k):
    x, y, c = lax.axis_index("x"), lax.axis_index("y"), lax.axis_index("c")
    px = 1 - x if k & 4 else x
    py = 1 - y if k & 2 else y
    pc = 1 - c if k & 1 else c
    return (px, py, pc), _dev_index(px, py, pc)


def _in_hbm(a):
    return pltpu.with_memory_space_constraint(a, pltpu.HBM)


ALL_RELS = tuple(range(1, NDEV))
CHIP_RELS = (4, 2, 6)


def _split_copies(xs, lands, send_sems, recv_sems, src_of, dst_of, rels):
    copies = []
    for t in range(len(xs)):
        for q, k in enumerate(rels):
            peer, peer_idx = _related(k)
            copies.append(pltpu.make_async_remote_copy(
                src_ref=src_of(xs[t], t, peer_idx), dst_ref=dst_of(lands[t], t, q, peer_idx),
                send_sem=send_sems.at[t * len(rels) + q], recv_sem=recv_sems.at[t * len(rels) + q],
                device_id=peer, device_id_type=MESH))
    return copies


def _exchange_start(name, xs, lands, src_of, dst_of, after, rels=ALL_RELS):
    nt = len(xs)

    def body(*refs):
        x_refs, land_refs = refs[:nt], refs[nt:2 * nt]
        send_sems, recv_sems = refs[2 * nt + 1], refs[2 * nt + 2]
        token = refs[-1]
        for cp in _split_copies(x_refs, land_refs, send_sems, recv_sems, src_of, dst_of, rels):
            cp.start()
        token[...] = jnp.zeros_like(token)

    res = pl.pallas_call(
        body, name=name,
        out_shape=(pltpu.SemaphoreType.DMA((nt * len(rels),)), pltpu.SemaphoreType.DMA((nt * len(rels),)),
                   *[pltpu.HBM(a.shape, a.dtype) for a in xs], *[pltpu.HBM(a.shape, a.dtype) for a in lands],
                   jax.ShapeDtypeStruct((8, 128), F32)),
        in_specs=[HBM] * (2 * nt) + [ANY],
        out_specs=(SEM, SEM, *[HBM] * (2 * nt), pl.BlockSpec(memory_space=pltpu.VMEM)),
        input_output_aliases={i: 2 + i for i in range(2 * nt)},
        compiler_params=pltpu.CompilerParams(has_side_effects=EFFECT),
    )(*[_in_hbm(a) for a in xs], *[_in_hbm(a) for a in lands], after)
    return res[0], res[1], res[2:2 + nt], res[2 + nt:2 + 2 * nt], res[-1]


def _exchange_wait(name, send_sems, recv_sems, xs, lands, src_of, dst_of, after, rels=ALL_RELS):
    nt = len(xs)

    def body(*refs):
        x_refs, land_refs = refs[:nt], refs[nt:2 * nt]
        send_sems, recv_sems = refs[2 * nt], refs[2 * nt + 1]
        for cp in _split_copies(x_refs, land_refs, send_sems, recv_sems, src_of, dst_of, rels):
            cp.wait_send()
            cp.wait_recv()

    res = pl.pallas_call(
        body, name=name,
        out_shape=(*[pltpu.HBM(a.shape, a.dtype) for a in xs], *[pltpu.HBM(a.shape, a.dtype) for a in lands]),
        in_specs=[HBM] * (2 * nt) + [SEM, SEM, ANY], out_specs=tuple([HBM] * (2 * nt)),
        input_output_aliases={i: i for i in range(2 * nt)},
        compiler_params=pltpu.CompilerParams(has_side_effects=EFFECT),
    )(*xs, *lands, send_sems, recv_sems, after)
    return res[:nt], res[nt:]


def _gather_start(name, shards, leads, after, rels=ALL_RELS):
    def src_of(x_ref, t, peer_idx):
        return x_ref

    def dst_of(land_ref, t, k, peer_idx):
        me = _dev_index(lax.axis_index("x"), lax.axis_index("y"), lax.axis_index("c"))
        return land_ref.at[(slice(None),) * leads[t] + (me,)]

    lands = [lax.empty(s.shape[:ld] + (NDEV,) + s.shape[ld:], s.dtype) for s, ld in zip(shards, leads)]
    return _exchange_start(name, shards, lands, src_of, dst_of, after, rels)


def _gather_wait(name, started, leads, after, rels=ALL_RELS):
    send_sems, recv_sems, shards, lands, _ = started

    def src_of(x_ref, t, peer_idx):
        return x_ref

    def dst_of(land_ref, t, k, peer_idx):
        return land_ref.at[(slice(None),) * leads[t] + (peer_idx,)]

    shards, lands = _exchange_wait(name, send_sems, recv_sems, shards, lands, src_of, dst_of, after, rels)
    me = _dev_index(lax.axis_index("x"), lax.axis_index("y"), lax.axis_index("c"))
    return [lax.dynamic_update_index_in_dim(g, s, me, ld) for g, s, ld in zip(lands, shards, leads)]


SIBLING_AND_CHIPS = (1,) + CHIP_RELS


def _forward_to_sibling(name, gathered, leads):
    nt = len(gathered)

    def body(*refs):
        ins, outs = refs[:nt], refs[nt:2 * nt]
        send_sems, recv_sems = refs[2 * nt:]
        x, y, c, chips = _mesh_pos()
        copies, arrivals = [], []
        for t in range(nt):
            for j, chip in enumerate(chips):
                def block(core):
                    return outs[t].at[(slice(None),) * leads[t] + (_dev_index(*chip, core),)]
                copies.append(pltpu.make_async_remote_copy(
                    src_ref=block(c), dst_ref=block(c), send_sem=send_sems.at[t, j], recv_sem=recv_sems.at[t, j],
                    device_id=(x, y, 1 - c), device_id_type=MESH))
                arrivals.append(pltpu.make_async_remote_copy(
                    src_ref=block(1 - c), dst_ref=block(1 - c), send_sem=send_sems.at[t, j], recv_sem=recv_sems.at[t, j],
                    device_id=(x, y, 1 - c), device_id_type=MESH))
        for cp in copies:
            cp.start()
        for cp in arrivals:
            cp.wait_recv()
        for cp in copies:
            cp.wait_send()

    return pl.pallas_call(
        body, name=name, in_specs=[ANY] * nt, out_specs=[ANY] * nt,
        out_shape=[jax.ShapeDtypeStruct(g.shape, g.dtype) for g in gathered],
        input_output_aliases={t: t for t in range(nt)},
        scratch_shapes=[pltpu.SemaphoreType.DMA((nt, 3)), pltpu.SemaphoreType.DMA((nt, 3))],
    )(*gathered)


def _scatter_src(x_ref, t, peer_idx):
    return x_ref.at[peer_idx]


def _scatter_dst(land_ref, t, q, peer_idx):
    return land_ref.at[q]


def _chips_src(x_ref, t, peer_idx):
    return x_ref.at[peer_idx // 2]


def _chips_start(name, ps, after):
    lands = [lax.empty((len(CHIP_RELS),) + p.shape[1:], p.dtype) for p in ps]
    return _exchange_start(name, ps, lands, _chips_src, _scatter_dst, after, CHIP_RELS)


def _chips_wait(name, started, after):
    send_sems, recv_sems, ps, lands, _ = started
    return _exchange_wait(name, send_sems, recv_sems, ps, lands, _chips_src, _scatter_dst, after, CHIP_RELS)


def _scatter_start(name, gs, after):
    lands = [lax.empty((NREL,) + g.shape[1:], g.dtype) for g in gs]
    return _exchange_start(name, gs, lands, _scatter_src, _scatter_dst, after)


def _scatter_wait(name, started, after, transposed):
    send_sems, recv_sems, gs, lands, _ = started
    gs, lands = _exchange_wait(name, send_sems, recv_sems, gs, lands, _scatter_src, _scatter_dst, after)
    me = _dev_index(lax.axis_index("x"), lax.axis_index("y"), lax.axis_index("c"))
    outs = []
    for t, (g, rv) in enumerate(zip(gs, lands)):
        _, r, cols = rv.shape
        tr_out = transposed[t]
        oshape = (cols, r) if tr_out else (r, cols)

        def body(c_ref, own_ref, rv_ref, o_ref):
            acc = own_ref[...].astype(F32)
            for k in range(NREL):
                acc = acc + rv_ref[k].astype(F32)
            o_ref[...] = acc.T if tr_out else acc

        outs.append(pl.pallas_call(
            body, name=f"{name}_sum_{t}",
            grid_spec=pltpu.PrefetchScalarGridSpec(
                num_scalar_prefetch=1, grid=(1,),
                in_specs=[pl.BlockSpec((None, r, cols), lambda i, cr: (cr[0], 0, 0)),
                          pl.BlockSpec((NREL, r, cols), lambda i, cr: (0, 0, 0))],
                out_specs=pl.BlockSpec(oshape, lambda i, cr: (0, 0))),
            out_shape=jax.ShapeDtypeStruct(oshape, F32), compiler_params=_cparams(VMEM_BIG),
        )(jnp.reshape(me, (1,)).astype(jnp.int32), g, rv))
    return outs


def _adamw(w, g, m, v):
    shape = w.shape
    cols = shape[-1]
    rows = math.prod(shape[:-1]) if len(shape) > 1 else 1
    w2, g2, m2, v2 = (jnp.reshape(t, (rows, cols)) for t in (w, g, m, v))
    tr = _pick(rows, (1024, 512, 256, 128)) if rows * cols > 65536 else rows
    c1 = 1.0 / (1.0 - ADAM_B1 ** ADAM_STEP)
    c2 = 1.0 / (1.0 - ADAM_B2 ** ADAM_STEP)

    def body(w_ref, g_ref, m_ref, v_ref, d_ref, nm_ref, nv_ref):
        gv = g_ref[...]
        nm = ADAM_B1 * m_ref[...] + (1.0 - ADAM_B1) * gv
        nv = ADAM_B2 * v_ref[...] + (1.0 - ADAM_B2) * (gv * gv)
        d_ref[...] = -ADAM_LR * ((nm * c1) / (jnp.sqrt(nv * c2) + ADAM_EPS) + ADAM_WD * w_ref[...])
        nm_ref[...] = nm
        nv_ref[...] = nv

    spec = pl.BlockSpec((tr, cols), lambda i: (i, 0))
    osh = jax.ShapeDtypeStruct((rows, cols), F32)
    d, nm, nv = pl.pallas_call(
        body, name="adamw", grid=(rows // tr,), in_specs=[spec] * 4, out_specs=[spec] * 3, out_shape=[osh] * 3,
        compiler_params=_cparams(VMEM_BIG),
    )(w2, g2, m2, v2)
    return jnp.reshape(d, shape), jnp.reshape(nm, shape), jnp.reshape(nv, shape)


def _pad_heads(w, name):
    if name not in P_HEADS:
        return w
    nh, real = P_HEADS[name]
    w = jnp.reshape(w, w.shape[:-2] + (nh, real, w.shape[-1]))
    w = jnp.pad(w, [(0, 0)] * (w.ndim - 2) + [(0, HP - real), (0, 0)])
    return jnp.reshape(w, w.shape[:-3] + (nh * HP, w.shape[-1]))


def _unpad_heads(w, name):
    if name not in P_HEADS:
        return w
    nh, real = P_HEADS[name]
    w = jnp.reshape(w, w.shape[:-2] + (nh, HP, w.shape[-1]))[..., :real, :]
    return jnp.reshape(w, w.shape[:-3] + (nh * real, w.shape[-1]))


def _win_pad(win_t):
    segs, o = {}, 0
    for n, s in zip(IN_NAMES, IN_SIZES):
        segs[n] = win_t[..., o:o + s, :]
        o += s
    return jnp.concatenate([_pad_heads(segs[n], n) for n in P_ORDER], axis=-2)


def _win_unpad(win_p):
    segs = {n: _unpad_heads(win_p[..., P_OFF[n]:P_OFF[n] + P_WIDTH[n], :], n) for n in P_ORDER}
    return jnp.concatenate([segs[n] for n in IN_NAMES], axis=-2)


def _t(w):
    return jnp.swapaxes(w, -1, -2)


def _ffn_stacked(g_g, g_u, g_d):
    wg, wu, wd = (jnp.reshape(g, (2, DFF, D)) for g in (g_g, g_u, g_d))
    return [(wg, wu, wd, (j,)) for j in range(2)]


def _ffn_single(g_g, g_u, g_d):
    return tuple(jnp.reshape(g, (DFF, D)) for g in (g_g, g_u, g_d)) + ((),)


def _layer_weights(ffn, g_in, g_pa, g_pb, g_out, gains, w2, b2, bm, gn, gq, gk):
    w2p = jnp.pad(jnp.reshape(w2, (2, GLA_RANK, GLA_H, GLA_DK)), ((0, 0), (0, HP - GLA_RANK), (0, 0), (0, HP - GLA_DK)))
    b2p = jnp.pad(jnp.reshape(b2, (2, 1, GLA_H, GLA_DK)), ((0, 0), (0, 0), (0, 0), (0, HP - GLA_DK)))
    wpb_t = jnp.pad(jnp.reshape(g_pb, (D, ATT_H, HEAD_DIM)), ((0, 0), (0, 0), (0, HP - HEAD_DIM)))
    return dict(
        gains=jnp.reshape(gains, (1, 3, 1, D)), ffn=ffn,
        win_t=_win_pad(jnp.reshape(g_in, (1, D_IN, D))), wpa_t=jnp.reshape(g_pa, (1, D, 512)),
        wpb_t=jnp.reshape(wpb_t, (1, D, ATT_H * HP)), wout=jnp.reshape(g_out, (1, D, D)),
        w2p=jnp.reshape(w2p, (1, 2, HP, GLA_H * HP)), b2p=jnp.reshape(b2p, (1, 2, 1, GLA_H * HP)),
        bm=jnp.reshape(bm, (1, 2, 1, D)), gn=jnp.reshape(gn, (1, 1, GLA_H * HP)),
        gq=jnp.pad(jnp.reshape(gq, (1, 1, HEAD_DIM)), ((0, 0), (0, 0), (0, HP - HEAD_DIM))),
        gk=jnp.pad(jnp.reshape(gk, (1, 1, HEAD_DIM)), ((0, 0), (0, 0), (0, HP - HEAD_DIM))))


def _layer_fwd_lower(h, z, ffn0, gain1):
    return _ffn_fwd(h, z, *ffn0, gain1)


def _layer_fwd_upper(h, z, s0, w, tabs, next_gain):
    h, z, s1 = _mixer_fwd(h, z, w, 0, tabs, w["gains"][0, 2])
    h, z, s2 = _ffn_fwd(h, z, *w["ffn"][1], next_gain)
    return h, z, (s0, s1, s2)


def _layer_fwd(h, z, w, tabs, next_gain):
    h, z, s0 = _layer_fwd_lower(h, z, w["ffn"][0], w["gains"][0, 1])
    return _layer_fwd_upper(h, z, s0, w, tabs, next_gain)


def _layer_bwd_upper(dh, dhb, saved, w, tabs):
    _, s1, s2 = saved
    dh, dhb, dg2, dwg1, dwu1, dwd1 = _ffn_bwd(dh, dhb, s2, w["gains"][0, 2], *w["ffn"][1])
    dh, dhb, gm = _mixer_bwd(dh, dhb, s1, w["gains"][0, 1], w, 0, tabs)
    gm.update(gain2=dg2, wg1=dwg1, wu1=dwu1, wd1=dwd1)
    return dh, dhb, gm


def _layer_bwd_lower(dh, dhb, saved, w, gm):
    dh, dhb, dg0, dwg0, dwu0, dwd0 = _ffn_bwd(dh, dhb, saved[0], w["gains"][0, 0], *w["ffn"][0])
    gm.update(gain0=dg0, wg0=dwg0, wu0=dwu0, wd0=dwd0)
    return dh, dhb, gm


def _layer_bwd(dh, dhb, saved, w, tabs):
    dh, dhb, gm = _layer_bwd_upper(dh, dhb, saved, w, tabs)
    return _layer_bwd_lower(dh, dhb, saved, w, gm)


def _blocks(ts):
    return [jnp.reshape(t, (NDEV, t.shape[0] // NDEV, t.shape[1])) for t in ts]


def _upper_grads(g):
    d_in = _win_unpad(g["win_t"])
    d_pb = jnp.reshape(jnp.reshape(g["wpb_t"], (D, ATT_H, HP))[:, :, :HEAD_DIM], (D, 512))
    return _blocks([g["wg1"], g["wu1"], g["wd1"], d_in, g["wpa_t"], d_pb, g["wout"]])


def _lower_grads(g):
    return _blocks([g["wg0"], g["wu0"], g["wd0"]])


def _big_grads(g):
    return _lower_grads(g) + _upper_grads(g)


def kernel(x, meta_tokens, norm_gains, ffn_w_gate, ffn_w_up, ffn_w_down, w_in, gla_w2, gla_b2, gla_gn, q_norm, k_norm, w_pa, w_pb, b_merge, w_out, final_norm, loss_target, m_meta_tokens, m_norm_gains, m_ffn_w_gate, m_ffn_w_up, m_ffn_w_down, m_w_in, m_gla_w2, m_gla_b2, m_gla_gn, m_q_norm, m_k_norm, m_w_pa, m_w_pb, m_b_merge, m_w_out, m_final_norm, v_meta_tokens, v_norm_gains, v_ffn_w_gate, v_ffn_w_up, v_ffn_w_down, v_w_in, v_gla_w2, v_gla_b2, v_gla_gn, v_q_norm, v_k_norm, v_w_pa, v_w_pb, v_b_merge, v_w_out, v_final_norm):
    dev = _dev_index(lax.axis_index("x"), lax.axis_index("y"), lax.axis_index("c"))
    sh_g = _t(ffn_w_gate).astype(BF16)
    sh_u = _t(ffn_w_up).astype(BF16)
    sh_d = ffn_w_down.astype(BF16)
    sh_in = _t(w_in).astype(BF16)
    sh_pa = _t(w_pa).astype(BF16)
    sh_pb = _t(w_pb).astype(BF16)
    sh_out = w_out.astype(BF16)
    small = jnp.concatenate([jnp.reshape(t, (-1, 128)) for t in
                             (meta_tokens, norm_gains, gla_w2, gla_b2, b_merge)], axis=0)
    small = jnp.pad(small, ((0, 2), (0, 0)))
    def shards(l):
        return [sh_g[l], sh_u[l], sh_d[l], sh_in[l], sh_pa[l], sh_pb[l], sh_out[l]]

    w_leads = [1, 1, 1, 0, 0, 0, 0]
    *g0_ffn0, g_small = _all_gather("gather_layer0", [sh_g[0, 0], sh_u[0, 0], sh_d[0, 0], small], [0, 0, 0, 0])
    rest0 = [sh_g[0, 1], sh_u[0, 1], sh_d[0, 1], sh_in[0], sh_pa[0], sh_pb[0], sh_out[0]]
    rest_leads = [0] * len(rest0)
    started0 = _gather_start("gather_start_0", rest0, rest_leads, g_small, SIBLING_AND_CHIPS)
    meta_full = jnp.reshape(jnp.transpose(g_small[:, 0:16], (1, 0, 2)), (NMETA, D)) + started0[4][0, 0]
    gains_full = jnp.reshape(jnp.transpose(jnp.reshape(g_small[:, 16:28], (NDEV, DEPTH, 3, 128)), (1, 2, 0, 3)), (DEPTH, 3, D))
    w2_full = jnp.reshape(jnp.transpose(jnp.reshape(g_small[:, 28:60], (NDEV, DEPTH, 2, GLA_RANK, 32)), (1, 2, 3, 0, 4)),
                          (DEPTH, 2, GLA_RANK, 256))
    b2_full = jnp.reshape(jnp.transpose(jnp.reshape(g_small[:, 60:62], (NDEV, DEPTH, 2, 32)), (1, 2, 0, 3)), (DEPTH, 2, 256))
    bm_full = jnp.reshape(jnp.transpose(jnp.reshape(g_small[:, 62:70], (NDEV, DEPTH, 2, 128)), (1, 2, 0, 3)), (DEPTH, 2, D))

    def layer_weights(l, ffn, others, gains_l):
        return _layer_weights(ffn, *others, gains_l, w2_full[l], b2_full[l], bm_full[l], gla_gn[l], q_norm[l], k_norm[l])

    xl = x[0]
    lp = xl.shape[0] + NULL + NMETA
    tabs = _rope_tables(lp)
    h = jnp.concatenate([jnp.zeros((NULL, D), F32), meta_full, xl], axis=0)
    weights, saved, started = [], [], {}
    z = _rmsnorm_fwd(h, jnp.reshape(gains_full[0, 0], (1, D)))
    for l in range(DEPTH):
        next_gain = jnp.reshape(gains_full[l + 1, 0], (1, D)) if l + 1 < DEPTH else None
        if l == 0:
            ffn0 = _ffn_single(*g0_ffn0)
            h, z, s0 = _layer_fwd_lower(h, z, ffn0, jnp.reshape(gains_full[0, 1], (1, D)))
            rest = _forward_to_sibling("gather_forward_0", _gather_wait("gather_wait_0", started0, rest_leads, h,
                                                                         SIBLING_AND_CHIPS), rest_leads)
            started[1] = _gather_start("gather_start_1", shards(1), w_leads, rest[0])
            weights.append(layer_weights(0, [ffn0, _ffn_single(*rest[:3])], rest[3:], gains_full[0]))
            z = z + started[1][4][0, 0].astype(BF16)
            h, z, sv = _layer_fwd_upper(h, z, s0, weights[0], tabs, next_gain)
        else:
            tok = jnp.zeros((), F32)
            if l < DEPTH - 1:
                started[l + 1] = _gather_start(f"gather_start_{l + 1}", shards(l + 1), w_leads, h)
                tok = started[l + 1][4][0, 0]
            gathered = _gather_wait(f"gather_wait_{l}", started[l], w_leads, h)
            weights.append(layer_weights(l, _ffn_stacked(*gathered[:3]), gathered[3:], gains_full[l] + tok))
            h, z, sv = _layer_fwd(h, z, weights[l], tabs, next_gain)
        saved.append(sv)
    loss, dh, dhb, d_final = _loss_head(h, loss_target[0], jnp.reshape(final_norm, (1, D)))
    loss = lax.psum(loss[0, 0], ("x", "y", "c"))

    grads, scattering = [None] * DEPTH, {}
    tok = jnp.zeros((), F32)
    for l in reversed(range(DEPTH)):
        w = dict(weights[l], gains=weights[l]["gains"] + tok)
        if l > 0:
            dh, dhb, grads[l] = _layer_bwd(dh, dhb, saved[l], w, tabs)
            scattering[l] = _scatter_start(f"scatter_start_{l}", _big_grads(grads[l]), dhb)
            tok = scattering[l][4][0, 0]
        else:
            dh, dhb, gm = _layer_bwd_upper(dh, dhb, saved[l], w, tabs)
            ups = _upper_grads(gm)
            pair = _pair_sum("rs_pair_up", ups, _exchange_sibling("rs_sibling_up", ups))
            scattering[l] = _chips_start(f"scatter_start_{l}", pair, dhb)
            dhb = dhb + scattering[l][4][0, 0].astype(BF16)
            dh, dhb, grads[l] = _layer_bwd_lower(dh, dhb, saved[l], w, gm)
    grad_x = dh[NULL + NMETA:][None]
    t_lower, t_upper = [True, True, False], [True, True, False, True, True, True, False]
    lows = _lower_grads(grads[0])
    pair_lo = _pair_sum("rs_pair_lo", lows, _exchange_sibling("rs_sibling_lo", lows))
    started_lo = _chips_start("scatter_start_lo", pair_lo, dhb)
    red = [None] * DEPTH
    for l in reversed(range(1, DEPTH)):
        red[l] = _scatter_wait(f"scatter_wait_{l}", scattering[l], started_lo[4], t_lower + t_upper)
    pair, recv = _chips_wait("scatter_wait_0", scattering[0], red[1][-1])
    red_upper = _final_sum("rs_sum_up", pair, recv, t_upper)
    pair_lo, recv_lo = _chips_wait("scatter_wait_lo", started_lo, red_upper[-1])
    red[0] = _final_sum("rs_sum_lo", pair_lo, recv_lo, t_lower) + red_upper
    g_gate = jnp.stack([jnp.stack([red[l][0], red[l][3]]) for l in range(DEPTH)])
    g_up = jnp.stack([jnp.stack([red[l][1], red[l][4]]) for l in range(DEPTH)])
    g_down = jnp.stack([jnp.stack([red[l][2], red[l][5]]) for l in range(DEPTH)])
    g_win = jnp.stack([red[l][6] for l in range(DEPTH)])
    g_wpa = jnp.stack([red[l][7] for l in range(DEPTH)])
    g_wpb = jnp.stack([red[l][8] for l in range(DEPTH)])
    g_wout = jnp.stack([red[l][9] for l in range(DEPTH)])

    d_meta = dh[NULL:NULL + NMETA]
    d_gains = jnp.stack([jnp.concatenate([grads[l]["gain0"], grads[l]["gain"], grads[l]["gain2"]], axis=0)
                         for l in range(DEPTH)])
    d_w2 = jnp.stack([jnp.reshape(jnp.reshape(grads[l]["w2p"], (2, HP, GLA_H, HP))[:, :GLA_RANK, :, :GLA_DK],
                                  (2, GLA_RANK, 256)) for l in range(DEPTH)])
    d_b2 = jnp.stack([jnp.reshape(jnp.reshape(grads[l]["b2p"], (2, GLA_H, HP))[:, :, :GLA_DK], (2, 256))
                      for l in range(DEPTH)])
    d_gn = jnp.stack([grads[l]["gn"][0] for l in range(DEPTH)])
    d_gq = jnp.stack([grads[l]["gq"][0, :HEAD_DIM] for l in range(DEPTH)])
    d_gk = jnp.stack([grads[l]["gk"][0, :HEAD_DIM] for l in range(DEPTH)])
    d_bm = jnp.stack([jnp.concatenate([grads[l]["bma"], grads[l]["bmb"]], axis=0) for l in range(DEPTH)])
    parts = [d_meta, d_gains, d_w2, d_b2, d_gn, d_gq, d_gk, d_bm, d_final[0]]
    sizes = [p.size for p in parts]
    flat = jnp.concatenate([jnp.reshape(p, (-1,)) for p in parts])
    flat = jnp.reshape(flat, (-1, 128))
    nrow = flat.shape[0]
    flat = jnp.pad(flat, ((0, (-nrow) % 8), (0, 0)))
    (g_flat,) = _all_gather("gather_small_grads", [flat], [0])
    tot = jnp.reshape(_sum_gathered(g_flat), (-1,))
    full, o = [], 0
    for p, s in zip(parts, sizes):
        full.append(jnp.reshape(tot[o:o + s], p.shape))
        o += s
    f_meta, f_gains, f_w2, f_b2, f_gn, f_gq, f_gk, f_bm, f_final = full

    def mine(t, width):
        return lax.dynamic_slice_in_dim(t, dev * width, width, axis=t.ndim - 1)

    g_small = dict(meta_tokens=mine(f_meta, 128), norm_gains=mine(f_gains, 128), gla_w2=mine(f_w2, 32),
                   gla_b2=mine(f_b2, 32), gla_gn=f_gn, q_norm=f_gq, k_norm=f_gk, b_merge=mine(f_bm, 128),
                   final_norm=f_final)
    gr = dict(g_small, ffn_w_gate=g_gate, ffn_w_up=g_up, ffn_w_down=g_down, w_in=g_win, w_pa=g_wpa, w_pb=g_wpb,
              w_out=g_wout)
    ws = dict(meta_tokens=meta_tokens, norm_gains=norm_gains, ffn_w_gate=ffn_w_gate, ffn_w_up=ffn_w_up,
              ffn_w_down=ffn_w_down, w_in=w_in, gla_w2=gla_w2, gla_b2=gla_b2, gla_gn=gla_gn, q_norm=q_norm,
              k_norm=k_norm, w_pa=w_pa, w_pb=w_pb, b_merge=b_merge, w_out=w_out, final_norm=final_norm)
    ms = dict(meta_tokens=m_meta_tokens, norm_gains=m_norm_gains, ffn_w_gate=m_ffn_w_gate, ffn_w_up=m_ffn_w_up,
              ffn_w_down=m_ffn_w_down, w_in=m_w_in, gla_w2=m_gla_w2, gla_b2=m_gla_b2, gla_gn=m_gla_gn, q_norm=m_q_norm,
              k_norm=m_k_norm, w_pa=m_w_pa, w_pb=m_w_pb, b_merge=m_b_merge, w_out=m_w_out, final_norm=m_final_norm)
    vs = dict(meta_tokens=v_meta_tokens, norm_gains=v_norm_gains, ffn_w_gate=v_ffn_w_gate, ffn_w_up=v_ffn_w_up,
              ffn_w_down=v_ffn_w_down, w_in=v_w_in, gla_w2=v_gla_w2, gla_b2=v_gla_b2, gla_gn=v_gla_gn, q_norm=v_q_norm,
              k_norm=v_k_norm, w_pa=v_w_pa, w_pb=v_w_pb, b_merge=v_b_merge, w_out=v_w_out, final_norm=v_final_norm)
    names = ["meta_tokens", "norm_gains", "ffn_w_gate", "ffn_w_up", "ffn_w_down", "w_in", "gla_w2", "gla_b2", "gla_gn",
             "q_norm", "k_norm", "w_pa", "w_pb", "b_merge", "w_out", "final_norm"]
    deltas, new_m, new_v = [], [], []
    for n in names:
        dlt, nm, nv = _adamw(ws[n], gr[n], ms[n], vs[n])
        deltas.append(dlt)
        new_m.append(nm)
        new_v.append(nv)
    return (loss, grad_x, *[gr[n] for n in names], *deltas, *new_m, *new_v)
```

```python
import functools
import math

import jax
import jax.numpy as jnp
import numpy as np
from jax import lax
from jax.experimental import pallas as pl
from jax.experimental.pallas import tpu as pltpu

F32 = jnp.float32
BF16 = jnp.bfloat16
MESH = pl.DeviceIdType.MESH
ANY = pl.BlockSpec(memory_space=pl.ANY)

NDEV = 8
D = 1024
DFF = 2816
DEPTH = 4
NMETA = 16
NULL = 112
GRID_W = 64
EPS = 1e-6
HP = 128
GLA_H = 4
GLA_DK = 64
GLA_RANK = 16
GLA_TAU = 16.0
CHUNK = 64
ATT_H = 8
ATT_KV = 2
ATT_G = ATT_H // ATT_KV
HEAD_DIM = 64
ROPE_THETA = 10000.0

IN_SIZES = (256, 256, 512, 512, 16, 16, 512, 128, 128, 1024, 1024)
IN_NAMES = ("qa", "ka", "va", "ra", "lrf", "lrb", "qb", "kb", "vb", "ga", "gb")
D_IN = sum(IN_SIZES)
P_ORDER = ("qb", "ga", "gb", "qa", "ka", "va", "ra", "kb", "vb", "lrf", "lrb")
P_WIDTH = dict(qb=1024, ga=1024, gb=1024, qa=512, ka=512, va=512, ra=512, kb=256, vb=256, lrf=128, lrb=128)
P_OFF = {}
_o = 0
for _n in P_ORDER:
    P_OFF[_n] = _o
    _o += P_WIDTH[_n]
D_INP = _o
P_HEADS = dict(qa=(4, 64), ka=(4, 64), qb=(8, 64), kb=(2, 64), vb=(2, 64), lrf=(1, 16), lrb=(1, 16))

ADAM_LR = 0.001
ADAM_B1 = 0.9
ADAM_B2 = 0.999
ADAM_EPS = 1e-08
ADAM_WD = 0.01
ADAM_STEP = 10

VMEM_BIG = 58 * 1024 * 1024
MXU_N = 256


def _cparams(vmem=None):
    return pltpu.CompilerParams(vmem_limit_bytes=vmem) if vmem else pltpu.CompilerParams()


def _pick(n, prefs):
    for p in prefs:
        if n % p == 0:
            return p
    return n


def _tm(lp):
    return _pick(lp, (528, 512, 256, 128))


_DN = {"nn": (((1,), (0,)), ((), ())), "nt": (((1,), (1,)), ((), ())), "tn": (((0,), (0,)), ((), ()))}


def _dot(a, b, mode="nn", precision=None):
    return lax.dot_general(a, b, _DN[mode], preferred_element_type=F32, precision=precision)


def _split(x):
    hi = x.astype(BF16)
    return hi, (x - hi.astype(F32)).astype(BF16)


def _dot_sel(t, x, mode="nn"):
    hi, lo = _split(x)
    return _dot(t, hi, mode) + _dot(t, lo, mode)


def _dot3(a, b, mode="nn"):
    ah, al = _split(a)
    bh, bl = _split(b)
    return _dot(ah, bh, mode) + (_dot(ah, bl, mode) + _dot(al, bh, mode))


def _sigmoid(x):
    return 0.5 * jnp.tanh(0.5 * x) + 0.5


def _mm(name, m, n, terms, outs, epilogue, extras=(), *, tm, tn, nk=1, nsub=1, i_outer=False, vmem=None):
    gm, gn = m // tm, n // tn
    assert gm * tm == m and gn * tn == n, (name, m, n, tm, tn)
    n_acc = 1 + max(t[3] for t in terms)

    def gmap(f):
        if i_outer:
            return lambda i, j, kk: f(i, j, kk)
        return lambda j, i, kk: f(i, j, kk)

    in_specs, args = [], []
    for a, b, mode, _, pa, pb in terms:
        kdim = a.shape[-2] if mode == "tn" else a.shape[-1]
        tk = kdim // nk
        assert tk * nk == kdim
        na, nb = (None,) * len(pa), (None,) * len(pb)
        if mode == "tn":
            in_specs.append(pl.BlockSpec(na + (tk, tm), gmap(lambda i, j, kk, pa=pa: pa + (kk, i))))
        else:
            in_specs.append(pl.BlockSpec(na + (tm, tk), gmap(lambda i, j, kk, pa=pa: pa + (i, kk))))
        if mode == "nt":
            in_specs.append(pl.BlockSpec(nb + (tn, tk), gmap(lambda i, j, kk, pb=pb: pb + (j, kk))))
        else:
            in_specs.append(pl.BlockSpec(nb + (tk, tn), gmap(lambda i, j, kk, pb=pb: pb + (kk, j))))
        args += [a, b]
    for arr, kind, off, pe in extras:
        ne = (None,) * len(pe)
        if kind == "mn":
            in_specs.append(pl.BlockSpec(ne + (tm, tn), gmap(lambda i, j, kk, off=off, pe=pe: pe + (i, j + off))))
        else:
            in_specs.append(pl.BlockSpec(ne + (1, tn), gmap(lambda i, j, kk, off=off, pe=pe: pe + (0, j + off))))
        args.append(arr)
    out_shape, out_specs = [], []
    for shape, dtype, kind, off, po in outs:
        no = (None,) * len(po)
        out_shape.append(jax.ShapeDtypeStruct(shape, dtype))
        if kind == "mn":
            out_specs.append(pl.BlockSpec(no + (tm, tn), gmap(lambda i, j, kk, off=off, po=po: po + (i, j + off))))
        else:
            assert not i_outer
            out_specs.append(pl.BlockSpec(no + (1, tn), gmap(lambda i, j, kk, off=off, po=po: po + (0, j + off))))
    n_t, n_e, n_o = len(terms), len(extras), len(outs)
    i_axis = 0 if i_outer else 1

    def body(*refs):
        ins = refs[: 2 * n_t]
        exs = refs[2 * n_t: 2 * n_t + n_e]
        ors = refs[2 * n_t + n_e: 2 * n_t + n_e + n_o]
        accs = refs[2 * n_t + n_e + n_o:]
        i = pl.program_id(i_axis)
        kk = pl.program_id(2)

        def partials(cs):
            part = [None] * n_acc
            for t, (_, _, mode, ai, _, _) in enumerate(terms):
                b_ref = ins[2 * t + 1]
                b_val = b_ref[cs, :] if mode == "nt" else b_ref[:, cs]
                p = _dot(ins[2 * t][...], b_val, mode)
                part[ai] = p if part[ai] is None else part[ai] + p
            return part

        def finish(vals, cs):
            res = epilogue(vals, [e[:, cs] for e in exs], i * tm)
            for (_, dtype, kind, _, _), o_ref, v in zip(outs, ors, res):
                if kind == "mn":
                    o_ref[:, cs] = v.astype(dtype)
                else:
                    @pl.when(i == 0)
                    def _():
                        o_ref[:, cs] = v.astype(dtype)

                    @pl.when(i != 0)
                    def _():
                        o_ref[:, cs] += v.astype(dtype)

        if nk == 1:
            w = tn // nsub
            for s in range(nsub):
                cs = slice(s * w, (s + 1) * w)
                finish(partials(cs), cs)
        else:
            part = partials(slice(None))
            @pl.when(kk == 0)
            def _():
                for a_ref, p in zip(accs, part):
                    a_ref[...] = p

            @pl.when(kk != 0)
            def _():
                for a_ref, p in zip(accs, part):
                    a_ref[...] += p

            @pl.when(kk == nk - 1)
            def _():
                finish([a_ref[...] for a_ref in accs], slice(None))

    scratch = [pltpu.VMEM((tm, tn), F32) for _ in range(n_acc)] if nk > 1 else []
    grid = (gm, gn, nk) if i_outer else (gn, gm, nk)
    res = pl.pallas_call(
        body, name=name, grid=grid, in_specs=in_specs, out_specs=out_specs, out_shape=out_shape,
        scratch_shapes=scratch, compiler_params=_cparams(vmem),
    )(*args)
    return res


def _term(a, b, mode, acc=0, pa=(), pb=()):
    return (a, b, mode, acc, tuple(pa), tuple(pb))


def _row_tile(lp):
    return _pick(lp, (384, 256, 128))


def _rmsnorm_fwd(h, gain):
    lp = h.shape[0]
    tr = _row_tile(lp)

    def body(h_ref, g_ref, z_ref):
        x = h_ref[...]
        r = lax.rsqrt(jnp.mean(x * x, axis=-1, keepdims=True) + EPS)
        z_ref[...] = (x * r * g_ref[...]).astype(BF16)

    return pl.pallas_call(
        body, name="rmsnorm_fwd", grid=(lp // tr,),
        in_specs=[pl.BlockSpec((tr, D), lambda i: (i, 0)), pl.BlockSpec((1, D), lambda i: (0, 0))],
        out_specs=pl.BlockSpec((tr, D), lambda i: (i, 0)),
        out_shape=jax.ShapeDtypeStruct((lp, D), BF16),
    )(h, gain)


def _loss_head(h, target, gain):
    lp = h.shape[0]
    tr = 128

    def body(h_ref, t_ref, g_ref, loss_ref, dh_ref, dhb_ref, dg_ref):
        i = pl.program_id(0)

        @pl.when(i == 0)
        def _():
            loss_ref[...] = jnp.zeros_like(loss_ref)
            dg_ref[...] = jnp.zeros_like(dg_ref)
            dh_ref[...] = jnp.zeros_like(dh_ref)
            dhb_ref[...] = jnp.zeros_like(dhb_ref)

        @pl.when(i != 0)
        def _():
            x = h_ref[...]
            g = g_ref[...]
            r = lax.rsqrt(jnp.mean(x * x, axis=-1, keepdims=True) + EPS)
            xh = x * r
            y = xh * g
            err = y - t_ref[...]
            loss_ref[...] += 0.5 * jnp.sum(jnp.sum(err * err, axis=-1, keepdims=True), axis=0, keepdims=True) / D
            dy = err * (1.0 / D)
            dg_ref[...] += jnp.sum(dy * xh, axis=0, keepdims=True)
            dxh = dy * g
            dx = r * (dxh - xh * jnp.mean(dxh * xh, axis=-1, keepdims=True))
            dh_ref[...] = dx
            dhb_ref[...] = dx.astype(BF16)

    row = pl.BlockSpec((tr, D), lambda i: (i, 0))
    vec = pl.BlockSpec((1, D), lambda i: (0, 0))
    return pl.pallas_call(
        body, name="loss_head", grid=(lp // tr,),
        in_specs=[row, pl.BlockSpec((tr, D), lambda i: (jnp.maximum(i - 1, 0), 0)), vec],
        out_specs=[pl.BlockSpec((1, 1), lambda i: (0, 0)), row, row, vec],
        out_shape=[jax.ShapeDtypeStruct((1, 1), F32), jax.ShapeDtypeStruct((lp, D), F32),
                   jax.ShapeDtypeStruct((lp, D), BF16), jax.ShapeDtypeStruct((1, D), F32)],
    )(h, target, gain)


def _silu_parts(g):
    s = _sigmoid(g)
    return g * s, s * (1.0 + g * (1.0 - s))


def _residual_norm_epi(scale, with_norm):
    def epi(accs, exs, row0):
        h2 = exs[0] + scale * accs[0]
        if not with_norm:
            return [h2]
        r = lax.rsqrt(jnp.mean(h2 * h2, axis=-1, keepdims=True) + EPS)
        return [h2, h2 * r * exs[1]]
    return epi


def _norm_bwd_epi(accs, exs, row0):
    dz = accs[0]
    x, res, g = exs
    r = lax.rsqrt(jnp.mean(x * x, axis=-1, keepdims=True) + EPS)
    xh = x * r
    dxh = dz * g
    dx = r * (dxh - xh * jnp.mean(dxh * xh, axis=-1, keepdims=True))
    rows = row0 + lax.broadcasted_iota(jnp.int32, (dz.shape[0], 1), 0)
    dh = jnp.where(rows >= NULL, res + dx, 0.0)
    return [dh, dh, jnp.sum(dz * xh, axis=0, keepdims=True)]


def _norm_bwd_outs(lp):
    return [((lp, D), F32, "mn", 0, ()), ((lp, D), BF16, "mn", 0, ()), ((1, D), F32, "nsum", 0, ())]


def _ffn_fwd(h, z, wg_t, wu_t, wd, pre, next_gain):
    lp = h.shape[0]
    tm = _tm(lp)

    def up_epi(accs, exs, row0):
        g, u = accs
        sg, _ = _silu_parts(g)
        return [g, u, sg * u]

    bshape = (lp, DFF)
    g_, u_, act = _mm("ffn_up", lp, DFF, [_term(z, wg_t, "nt", 0, (), pre), _term(z, wu_t, "nt", 1, (), pre)],
                      [(bshape, BF16, "mn", 0, ())] * 3, up_epi, tm=tm, tn=DFF, nsub=DFF // MXU_N, vmem=VMEM_BIG)

    with_norm = next_gain is not None
    res = _mm("ffn_down", lp, D, [_term(act, wd, "nn", 0, (), pre)],
              [((lp, D), F32, "mn", 0, ())] + ([((lp, D), BF16, "mn", 0, ())] if with_norm else []),
              _residual_norm_epi(0.5, with_norm),
              extras=[(h, "mn", 0, ())] + ([(next_gain, "n", 0, ())] if with_norm else []),
              tm=tm, tn=D, i_outer=True, vmem=VMEM_BIG)
    return res[0], (res[1] if with_norm else None), dict(h=h, z=z, g=g_, u=u_, act=act)


def _dw(name, a, b, m, n, scale=1.0):
    lp = a.shape[0]
    tm = _pick(m, (2944, 1408, 1024, 512, 256, 128))
    tn = _pick(n, (1024, 512, 256, 128)) if tm <= 1408 else _pick(n, (512, 256, 128))
    nk = lp // _pick(lp, (2112, 256, 128))

    def epi(accs, exs, row0):
        return [accs[0] * scale]

    (w,) = _mm(name, m, n, [_term(a, b, "tn")], [((m, n), BF16, "mn", 0, ())], epi, tm=tm, tn=tn, nk=nk,
               i_outer=True, vmem=VMEM_BIG)
    return w


def _ffn_bwd(dh, dhb, sv, gain, wg_t, wu_t, wd, pre):
    lp = dh.shape[0]
    tm = _tm(lp)

    def dact_epi(accs, exs, row0):
        g = exs[0].astype(F32)
        u = exs[1].astype(F32)
        da = 0.5 * accs[0]
        sg, dsg = _silu_parts(g)
        return [da * u * dsg, da * sg]

    dg_, du_ = _mm("ffn_dact", lp, DFF, [_term(dhb, wd, "nt", 0, (), pre)],
                   [((lp, DFF), BF16, "mn", 0, ())] * 2, dact_epi,
                   extras=[(sv["g"], "mn", 0, ()), (sv["u"], "mn", 0, ())], tm=_pick(lp, (1056, 512, 256, 128)),
                   tn=DFF // 2, vmem=VMEM_BIG)
    d_wd = _dw("dw_down", sv["act"], dhb, DFF, D, 0.5)
    d_wg = _dw("dw_gate", dg_, sv["z"], DFF, D)
    d_wu = _dw("dw_up", du_, sv["z"], DFF, D)

    nk = 1
    dh2, dhb2, dgain = _mm("ffn_dz", lp, D, [_term(dg_, wg_t, "nn", 0, (), pre), _term(du_, wu_t, "nn", 0, (), pre)],
                           _norm_bwd_outs(lp), _norm_bwd_epi,
                           extras=[(sv["h"], "mn", 0, ()), (dh, "mn", 0, ()), (gain, "n", 0, ())],
                           tm=tm, tn=D, nk=nk, vmem=VMEM_BIG)
    return dh2, dhb2, dgain, d_wg, d_wu, d_wd


def _gla_gates(hin, w2p, b2p):
    lp = hin.shape[0]
    tr = _row_tile(lp)
    bf, bb = P_OFF["lrf"] // HP, P_OFF["lrb"] // HP

    def body(lf_ref, lb_ref, w_ref, b_ref, o_ref, c_ref):
        i = pl.program_id(0)
        rows = i * tr + lax.broadcasted_iota(jnp.int32, (tr, 1), 0)
        r = lax.broadcasted_iota(jnp.int32, (tr, tr), 0)
        c = lax.broadcasted_iota(jnp.int32, (tr, tr), 1)
        same = (r // CHUNK) == (c // CHUNK)
        for d, l_ref in enumerate((lf_ref, lb_ref)):
            logit = _dot3(l_ref[...].astype(F32), w_ref[d]) + b_ref[d]
            g = jnp.where(rows >= NULL, jax.nn.log_sigmoid(logit) * (1.0 / GLA_TAU), 0.0)
            o_ref[d] = g
            tmat = jnp.where(same & ((r >= c) if d == 0 else (r <= c)), 1.0, 0.0).astype(BF16)
            c_ref[d] = _dot_sel(tmat, g)

    spec = pl.BlockSpec((2, tr, 512), lambda i: (0, i, 0))
    return pl.pallas_call(
        body, name="gla_gates", grid=(lp // tr,),
        in_specs=[pl.BlockSpec((tr, HP), lambda i: (i, bf)), pl.BlockSpec((tr, HP), lambda i: (i, bb)),
                  pl.BlockSpec((2, HP, 512), lambda i: (0, 0, 0)), pl.BlockSpec((2, 1, 512), lambda i: (0, 0, 0))],
        out_specs=[spec, spec],
        out_shape=[jax.ShapeDtypeStruct((2, lp, 512), F32)] * 2,
    )(hin, hin, w2p, b2p)


def _gla_rows(lp):
    return _pick(lp, (384, 256, 128))


def _tri(d):
    r = lax.broadcasted_iota(jnp.int32, (CHUNK, CHUNK), 0)
    c = lax.broadcasted_iota(jnp.int32, (CHUNK, CHUNK), 1)
    return (r >= c) if d == 0 else (r <= c)


def _gla_fwd(hin, gates):
    lp = hin.shape[0]
    rb = _gla_rows(lp)
    nb = lp // rb
    cpb = rb // CHUNK
    nchunk = lp // CHUNK
    qo, ko, vo = P_OFF["qa"] // 512, P_OFF["ka"] // 512, P_OFF["va"] // 512
    scale = GLA_DK ** -0.5

    def body(qf, kf, vf, gf, qb, kb, vb_, gb, of, ob, sf, sb, st):
        @pl.when(pl.program_id(0) == 0)
        def _():
            st[...] = jnp.zeros_like(st)

        ins = ((qf, kf, vf, gf, of, sf), (qb, kb, vb_, gb, ob, sb))
        for ci in range(cpb):
            for d in range(2):
                q_ref, k_ref, v_ref, g_ref, o_ref, s_ref = ins[d]
                tri = _tri(d)
                c = ci if d == 0 else cpb - 1 - ci
                rows = slice(c * CHUNK, (c + 1) * CHUNK)
                for h in range(GLA_H):
                    sl = slice(h * HP, (h + 1) * HP)
                    q = q_ref[rows, sl].astype(F32) * scale
                    k = k_ref[rows, sl].astype(F32)
                    v = v_ref[rows, sl].astype(F32)
                    b = g_ref[rows, sl]
                    btot = b[CHUNK - 1:CHUNK] if d == 0 else b[0:1]
                    qd = (q * jnp.exp(b)).astype(BF16)
                    ki = (k * jnp.exp(-b)).astype(BF16)
                    ke = (k * jnp.exp(btot - b)).astype(BF16)
                    vb = v.astype(BF16)
                    att = jnp.where(tri, _dot(qd, ki, "nt"), 0.0)
                    s_prev = st[d, h]
                    o_ref[rows, sl] = _dot(att.astype(BF16), vb) + _dot(qd, s_prev.astype(BF16), "nt")
                    s_ref[h, c] = s_prev
                    st[d, h] = s_prev * jnp.exp(btot) + _dot(vb, ke, "tn")

    def specs(off):
        return (pl.BlockSpec((rb, 512), lambda b: (b, off)), pl.BlockSpec((rb, 512), lambda b: (nb - 1 - b, off)))

    (qf, qb), (kf, kb), (vf, vb2) = specs(qo), specs(ko), specs(vo)
    gf = pl.BlockSpec((None, rb, 512), lambda b: (0, b, 0))
    gb = pl.BlockSpec((None, rb, 512), lambda b: (1, nb - 1 - b, 0))
    of, ob = specs(0)
    sf = pl.BlockSpec((GLA_H, cpb, HP, HP), lambda b: (0, b, 0, 0))
    sb = pl.BlockSpec((GLA_H, cpb, HP, HP), lambda b: (0, nb - 1 - b, 0, 0))
    osh = jax.ShapeDtypeStruct((lp, GLA_H * HP), F32)
    ssh = jax.ShapeDtypeStruct((GLA_H, nchunk, HP, HP), F32)
    return pl.pallas_call(
        body, name="gla_fwd", grid=(nb,),
        in_specs=[qf, kf, vf, gf, qb, kb, vb2, gb], out_specs=[of, ob, sf, sb], out_shape=[osh, osh, ssh, ssh],
        scratch_shapes=[pltpu.VMEM((2, GLA_H, HP, HP), F32)], compiler_params=_cparams(VMEM_BIG),
    )(hin, hin, hin, gates, hin, hin, hin, gates)


def _gla_bwd(hin, gates, states, do):
    lp = hin.shape[0]
    rb = 2 * CHUNK
    nb = lp // rb
    cpb = rb // CHUNK
    qo, ko, vo = P_OFF["qa"] // 512, P_OFF["ka"] // 512, P_OFF["va"] // 512
    scale = GLA_DK ** -0.5

    def body(qf, kf, vf, gf, sf, dof, qb, kb, vb_, gb, sb, dob,
             dqf, dkf, dvf, dgf, dqb, dkb, dvb, dgb, dst):
        @pl.when(pl.program_id(0) == 0)
        def _():
            dst[...] = jnp.zeros_like(dst)

        ins = ((qf, kf, vf, gf, sf, dof, dqf, dkf, dvf, dgf), (qb, kb, vb_, gb, sb, dob, dqb, dkb, dvb, dgb))
        for ci in range(cpb):
            for d in range(2):
                q_ref, k_ref, v_ref, g_ref, s_ref, do_ref, dq_ref, dk_ref, dv_ref, dg_ref = ins[d]
                tri, tri_t = _tri(d), _tri(1 - d)
                edge = lax.broadcasted_iota(jnp.int32, (CHUNK, 1), 0) == (CHUNK - 1 if d == 0 else 0)
                c = cpb - 1 - ci if d == 0 else ci
                rows = slice(c * CHUNK, (c + 1) * CHUNK)
                for h in range(GLA_H):
                    sl = slice(h * HP, (h + 1) * HP)
                    q = q_ref[rows, sl].astype(F32) * scale
                    k = k_ref[rows, sl].astype(F32)
                    v = v_ref[rows, sl].astype(F32)
                    dout = do_ref[rows, sl].astype(BF16)
                    b = g_ref[rows, sl]
                    btot = b[CHUNK - 1:CHUNK] if d == 0 else b[0:1]
                    e = jnp.exp(b)
                    ei = jnp.exp(-b)
                    et = jnp.exp(btot - b)
                    etot = jnp.exp(btot)
                    qd = q * e
                    ki = k * ei
                    ke = k * et
                    qdb, kib, keb, vb = qd.astype(BF16), ki.astype(BF16), ke.astype(BF16), v.astype(BF16)
                    att_t = jnp.where(tri_t, _dot(kib, qdb, "nt"), 0.0).astype(BF16)
                    d_att = jnp.where(tri, _dot(dout, vb, "nt"), 0.0).astype(BF16)
                    d_att_t = jnp.where(tri_t, _dot(vb, dout, "nt"), 0.0).astype(BF16)
                    s_prev = s_ref[h, c]
                    ds_t = dst[d, h]
                    ds_b = ds_t.astype(BF16)
                    dv = _dot(att_t, dout) + _dot(keb, ds_b, "nt")
                    d_qd = _dot(d_att, kib) + _dot(dout, s_prev.astype(BF16))
                    d_ki = _dot(d_att_t, qdb)
                    d_ke = _dot(vb, ds_b)
                    d_e = jnp.sum(s_prev * ds_t, axis=0, keepdims=True)
                    dst[d, h] = _dot(dout, qdb, "tn") + ds_t * etot
                    db = d_qd * qd - d_ki * ki - d_ke * ke
                    dbtot = jnp.sum(d_ke * ke, axis=0, keepdims=True) + d_e * etot
                    dq_ref[rows, sl] = (d_qd * e * scale).astype(BF16)
                    dk_ref[rows, sl] = (d_ki * ei + d_ke * et).astype(BF16)
                    dv_ref[rows, sl] = dv.astype(BF16)
                    dg_ref[rows, sl] = db + jnp.where(edge, dbtot, 0.0)

    def fw(off):
        return pl.BlockSpec((rb, 512), lambda b: (nb - 1 - b, off))

    def bw(off):
        return pl.BlockSpec((rb, 512), lambda b: (b, off))

    gf = pl.BlockSpec((None, rb, 512), lambda b: (0, nb - 1 - b, 0))
    gb = pl.BlockSpec((None, rb, 512), lambda b: (1, b, 0))
    sf = pl.BlockSpec((GLA_H, cpb, HP, HP), lambda b: (0, nb - 1 - b, 0, 0))
    sb = pl.BlockSpec((GLA_H, cpb, HP, HP), lambda b: (0, b, 0, 0))
    osh = jax.ShapeDtypeStruct((lp, GLA_H * HP), F32)
    osh_b = jax.ShapeDtypeStruct((lp, GLA_H * HP), BF16)
    res = pl.pallas_call(
        body, name="gla_bwd", grid=(nb,),
        in_specs=[fw(qo), fw(ko), fw(vo), gf, sf, fw(0), bw(qo), bw(ko), bw(vo), gb, sb, bw(0)],
        out_specs=[fw(0)] * 4 + [bw(0)] * 4, out_shape=[osh_b, osh_b, osh_b, osh] * 2,
        scratch_shapes=[pltpu.VMEM((2, GLA_H, HP, HP), F32)], compiler_params=_cparams(VMEM_BIG),
    )(hin, hin, hin, gates, states[0], do, hin, hin, hin, gates, states[1], do)
    return res[:4], res[4:]


def _gla_out_fwd(o2, hin, gn):
    lp = hin.shape[0]
    tr = _row_tile(lp)
    ro = P_OFF["ra"] // 512

    def body(of_ref, ob_ref, r_ref, gn_ref, a_ref):
        r = r_ref[...].astype(F32)
        sr, _ = _silu_parts(r)
        for h in range(GLA_H):
            sl = slice(h * HP, (h + 1) * HP)
            o = of_ref[:, sl] + ob_ref[:, sl]
            rs = lax.rsqrt(jnp.mean(o * o, axis=-1, keepdims=True) + EPS)
            a_ref[:, sl] = (o * rs * gn_ref[:, sl] * sr[:, sl]).astype(BF16)

    row = pl.BlockSpec((tr, 512), lambda i: (i, 0))
    return pl.pallas_call(
        body, name="gla_out_fwd", grid=(lp // tr,),
        in_specs=[row, row, pl.BlockSpec((tr, 512), lambda i: (i, ro)), pl.BlockSpec((1, 512), lambda i: (0, 0))],
        out_specs=row,
        out_shape=jax.ShapeDtypeStruct((lp, 512), BF16),
    )(o2[0], o2[1], hin, gn)


def _gla_out_bwd(da, o2, hin, gn):
    lp = hin.shape[0]
    tr = _row_tile(lp)
    ro = P_OFF["ra"] // 512

    def body(da_ref, of_ref, ob_ref, r_ref, gn_ref, do_ref, dr_ref, dgn_ref):
        i = pl.program_id(0)
        r = r_ref[...].astype(F32)
        sr, dsr = _silu_parts(r)
        da_v = da_ref[...]
        parts = []
        for h in range(GLA_H):
            sl = slice(h * HP, (h + 1) * HP)
            o = of_ref[:, sl] + ob_ref[:, sl]
            rs = lax.rsqrt(jnp.mean(o * o, axis=-1, keepdims=True) + EPS)
            oh = o * rs
            gn_h = gn_ref[:, sl]
            dah = da_v[:, sl]
            dr_ref[:, sl] = (dah * oh * gn_h * dsr[:, sl]).astype(BF16)
            t = dah * sr[:, sl]
            parts.append(jnp.sum(t * oh, axis=0, keepdims=True))
            doh = t * gn_h
            do_ref[:, sl] = rs * (doh - oh * jnp.mean(doh * oh, axis=-1, keepdims=True))
        part = jnp.concatenate(parts, axis=1)

        @pl.when(i == 0)
        def _():
            dgn_ref[...] = part

        @pl.when(i != 0)
        def _():
            dgn_ref[...] += part

    row = pl.BlockSpec((tr, 512), lambda i: (i, 0))
    return pl.pallas_call(
        body, name="gla_out_bwd", grid=(lp // tr,),
        in_specs=[row, row, row, pl.BlockSpec((tr, 512), lambda i: (i, ro)), pl.BlockSpec((1, 512), lambda i: (0, 0))],
        out_specs=[row, row, pl.BlockSpec((1, 512), lambda i: (0, 0))],
        out_shape=[jax.ShapeDtypeStruct((lp, 512), F32), jax.ShapeDtypeStruct((lp, 512), BF16),
                   jax.ShapeDtypeStruct((1, 512), F32)],
    )(da, o2[0], o2[1], hin, gn)


def _gla_in_bwd(gf, gb, gates, hin, w2p, others):
    lp = hin.shape[0]
    tr = _row_tile(lp)
    bf, bb = P_OFF["lrf"] // HP, P_OFF["lrb"] // HP
    names = tuple(others)

    def seg(name):
        return slice(P_OFF[name], P_OFF[name] + P_WIDTH[name])

    def body(dqf_ref, dkf_ref, dvf_ref, dgf_ref, dqb_ref, dkb_ref, dvb_ref, dgb_ref, g_ref, lf_ref, lb_ref, w_ref,
             *rest):
        other_refs, (o_ref, dw_ref, db_ref) = rest[:len(names)], rest[len(names):]
        i = pl.program_id(0)
        for n, ref in zip(names, other_refs):
            o_ref[:, seg(n)] = ref[...].astype(BF16)
        o_ref[:, seg("qa")] = (dqf_ref[...].astype(F32) + dqb_ref[...].astype(F32)).astype(BF16)
        o_ref[:, seg("ka")] = (dkf_ref[...].astype(F32) + dkb_ref[...].astype(F32)).astype(BF16)
        o_ref[:, seg("va")] = (dvf_ref[...].astype(F32) + dvb_ref[...].astype(F32)).astype(BF16)
        olr_ref = o_ref.at[:, P_OFF["lrf"]:P_OFF["lrf"] + 2 * HP]
        rows = i * tr + lax.broadcasted_iota(jnp.int32, (tr, 1), 0)
        r = lax.broadcasted_iota(jnp.int32, (tr, tr), 0)
        c = lax.broadcasted_iota(jnp.int32, (tr, tr), 1)
        same = (r // CHUNK) == (c // CHUNK)
        for d, (l_ref, dg_ref) in enumerate(((lf_ref, dgf_ref), (lb_ref, dgb_ref))):
            tmat = jnp.where(same & ((r <= c) if d == 0 else (r >= c)), 1.0, 0.0).astype(BF16)
            dg = _dot_sel(tmat, dg_ref[...])
            sig_neg = 1.0 - jnp.exp(GLA_TAU * g_ref[d])
            dlogit = jnp.where(rows >= NULL, dg * (1.0 / GLA_TAU) * sig_neg, 0.0)
            olr_ref[:, d * HP:(d + 1) * HP] = _dot3(dlogit, w_ref[d], "nt").astype(BF16)
            dw = _dot3(l_ref[...].astype(F32), dlogit, "tn")
            dbias = jnp.sum(dlogit, axis=0, keepdims=True)

            @pl.when(i == 0)
            def _():
                dw_ref[d] = dw
                db_ref[d] = dbias

            @pl.when(i != 0)
            def _():
                dw_ref[d] += dw
                db_ref[d] += dbias

    two = pl.BlockSpec((2, tr, 512), lambda i: (0, i, 0))
    row = pl.BlockSpec((tr, 512), lambda i: (i, 0))
    return pl.pallas_call(
        body, name="gla_in_bwd", grid=(lp // tr,),
        in_specs=[row] * 8 + [two, pl.BlockSpec((tr, HP), lambda i: (i, bf)),
                  pl.BlockSpec((tr, HP), lambda i: (i, bb)), pl.BlockSpec((2, HP, 512), lambda i: (0, 0, 0))] +
                 [pl.BlockSpec((tr, P_WIDTH[n]), lambda i: (i, 0)) for n in names],
        out_specs=[pl.BlockSpec((tr, D_INP), lambda i: (i, 0)),
                   pl.BlockSpec((2, HP, 512), lambda i: (0, 0, 0)), pl.BlockSpec((2, 1, 512), lambda i: (0, 0, 0))],
        out_shape=[jax.ShapeDtypeStruct((lp, D_INP), BF16), jax.ShapeDtypeStruct((2, HP, 512), F32),
                   jax.ShapeDtypeStruct((2, 1, 512), F32)],
        compiler_params=_cparams(VMEM_BIG),
    )(*gf, *gb, gates, hin, hin, w2p, *[others[n] for n in names])


def _rope_tables(lp):
    n_tok = lp - NULL - NMETA
    rows = n_tok // GRID_W
    row = np.repeat(np.arange(rows), GRID_W).astype(np.float32)
    col = np.tile(np.arange(GRID_W), rows).astype(np.float32)
    inv = (ROPE_THETA ** (-np.arange(0, 32, 2, dtype=np.float32) / 32)).astype(np.float32)
    ang = np.concatenate([row[:, None] * inv, col[:, None] * inv], axis=-1)
    ang = np.concatenate([np.zeros((NULL + NMETA, 32), np.float32), ang], axis=0)
    cos, sin = np.cos(ang).astype(np.float32), np.sin(ang).astype(np.float32)
    z16 = np.zeros((lp, 16), np.float32)
    z64 = np.zeros((lp, 64), np.float32)
    c = np.concatenate([cos[:, :16], cos[:, :16], cos[:, 16:], cos[:, 16:], z64], axis=1)
    a = np.concatenate([-sin[:, :16], z16, -sin[:, 16:], z16, z64], axis=1)
    b = np.concatenate([z16, sin[:, :16], z16, sin[:, 16:], z64], axis=1)
    return jnp.asarray(c), jnp.asarray(a), jnp.asarray(b)


def _rope(x, c, a, b):
    return x * c + pltpu.roll(x, HP - 16, 1) * a + pltpu.roll(x, 16, 1) * b


def _rope_t(dx, c, a, b):
    return dx * c + pltpu.roll(dx * a, 16, 1) + pltpu.roll(dx * b, HP - 16, 1)


def _attn_prep(hin, gq, gk, tabs):
    lp = hin.shape[0]
    tr = _row_tile(lp)
    qo, ko, vo = P_OFF["qb"] // 1024, P_OFF["kb"] // 256, P_OFF["vb"] // 256

    def body(q_ref, k_ref, v_ref, gq_ref, gk_ref, c_ref, a_ref, b_ref, oq_ref, ok_ref, ov_ref):
        c, a, b = c_ref[...], a_ref[...], b_ref[...]
        for src, g_ref, dst, nh, sc in ((q_ref, gq_ref, oq_ref, ATT_H, Q_SCALE), (k_ref, gk_ref, ok_ref, ATT_KV, 1.0)):
            for h in range(nh):
                sl = slice(h * HP, (h + 1) * HP)
                x = src[:, sl].astype(F32)
                r = lax.rsqrt(jnp.sum(x * x, axis=-1, keepdims=True) * (1.0 / HEAD_DIM) + EPS)
                dst[:, sl] = (_rope(x * r * g_ref[...], c, a, b) * sc).astype(BF16)
        lane = lax.broadcasted_iota(jnp.int32, (1, ATT_KV * HP), 1)
        ov_ref[...] = jnp.where(lane % HP == HEAD_DIM, 1.0, v_ref[...]).astype(BF16)

    tab = pl.BlockSpec((tr, HP), lambda i: (i, 0))
    vec = pl.BlockSpec((1, HP), lambda i: (0, 0))
    return pl.pallas_call(
        body, name="attn_prep", grid=(lp // tr,),
        in_specs=[pl.BlockSpec((tr, 1024), lambda i: (i, qo)), pl.BlockSpec((tr, 256), lambda i: (i, ko)),
                  pl.BlockSpec((tr, 256), lambda i: (i, vo)), vec, vec, tab, tab, tab],
        out_specs=[pl.BlockSpec((tr, 1024), lambda i: (i, 0)), pl.BlockSpec((tr, 256), lambda i: (i, 0)),
                   pl.BlockSpec((tr, 256), lambda i: (i, 0))],
        out_shape=[jax.ShapeDtypeStruct((lp, 1024), BF16), jax.ShapeDtypeStruct((lp, 256), BF16),
                   jax.ShapeDtypeStruct((lp, 256), BF16)],
    )(hin, hin, hin, gq, gk, *tabs)


def _attn_prep_bwd(dqr, dkr, hin, gq, gk, tabs):
    lp = hin.shape[0]
    tr = _row_tile(lp)
    qo, ko = P_OFF["qb"] // 1024, P_OFF["kb"] // 256

    def body(dq_ref, dk_ref, q_ref, k_ref, gq_ref, gk_ref, c_ref, a_ref, b_ref, oq_ref, ok_ref, dgq_ref, dgk_ref):
        i = pl.program_id(0)
        c, a, b = c_ref[...], a_ref[...], b_ref[...]
        for src, dsrc, g_ref, dst, dg_ref, nh, sc in (
                (q_ref, dq_ref, gq_ref, oq_ref, dgq_ref, ATT_H, Q_SCALE),
                (k_ref, dk_ref, gk_ref, ok_ref, dgk_ref, ATT_KV, 1.0)):
            acc = jnp.zeros((1, HP), F32)
            for h in range(nh):
                sl = slice(h * HP, (h + 1) * HP)
                x = src[:, sl].astype(F32)
                r = lax.rsqrt(jnp.sum(x * x, axis=-1, keepdims=True) * (1.0 / HEAD_DIM) + EPS)
                xh = x * r
                dxn = _rope_t(dsrc[:, sl] * sc, c, a, b)
                acc = acc + jnp.sum(dxn * xh, axis=0, keepdims=True)
                dxh = dxn * g_ref[...]
                dx = r * (dxh - xh * (jnp.sum(dxh * xh, axis=-1, keepdims=True) * (1.0 / HEAD_DIM)))
                dst[:, sl] = dx.astype(BF16)

            @pl.when(i == 0)
            def _():
                dg_ref[...] = acc

            @pl.when(i != 0)
            def _():
                dg_ref[...] += acc

    tab = pl.BlockSpec((tr, HP), lambda i: (i, 0))
    vec = pl.BlockSpec((1, HP), lambda i: (0, 0))
    return pl.pallas_call(
        body, name="attn_prep_bwd", grid=(lp // tr,),
        in_specs=[pl.BlockSpec((tr, 1024), lambda i: (i, 0)), pl.BlockSpec((tr, 256), lambda i: (i, 0)),
                  pl.BlockSpec((tr, 1024), lambda i: (i, qo)), pl.BlockSpec((tr, 256), lambda i: (i, ko)),
                  vec, vec, tab, tab, tab],
        out_specs=[pl.BlockSpec((tr, 1024), lambda i: (i, 0)), pl.BlockSpec((tr, 256), lambda i: (i, 0)), vec, vec],
        out_shape=[jax.ShapeDtypeStruct((lp, 1024), BF16), jax.ShapeDtypeStruct((lp, 256), BF16),
                   jax.ShapeDtypeStruct((1, HP), F32), jax.ShapeDtypeStruct((1, HP), F32)],
    )(dqr, dkr, hin, hin, gq, gk, *tabs)


QB = 128
GH = 2
Q_SCALE = HEAD_DIM ** -0.5 * math.log2(math.e)
LN2 = math.log(2.0)


def _stack(ref, g0, n):
    return jnp.concatenate([ref[:, (g0 + g) * HP:(g0 + g + 1) * HP] for g in range(n)], axis=0)


def _attn_fwd(qr, kr, vb):
    lp = qr.shape[0]
    nq = lp // QB

    def body(q_ref, k_ref, v_ref, o_ref, lse_ref):
        qb = pl.program_id(1)
        keys = lax.broadcasted_iota(jnp.int32, (1, lp), 1)
        lane = lax.broadcasted_iota(jnp.int32, (1, HP), 1)
        rows = qb * QB + lax.broadcasted_iota(jnp.int32, (QB, 1), 0)
        for ch in range(ATT_G // GH):
            qs = _stack(q_ref, ch * GH, GH)
            s = _dot(qs, k_ref[...], "nt")
            s = jnp.where(keys >= NULL, s, -1e30)
            m = jnp.max(s, axis=-1, keepdims=True)
            p = jnp.exp2(s - m).astype(BF16)
            o_raw = _dot(p, v_ref[...])
            l = jnp.sum(jnp.where(lane == HEAD_DIM, o_raw, 0.0), axis=-1, keepdims=True)
            o = jnp.where(lane < HEAD_DIM, o_raw / l, 0.0)
            lse = m + jnp.log2(l)
            for g in range(GH):
                sl = slice((ch * GH + g) * HP, (ch * GH + g + 1) * HP)
                o_ref[:, sl] = jnp.where(rows >= NULL, o[g * QB:(g + 1) * QB], 0.0).astype(BF16)
                lse_ref[:, sl] = jnp.broadcast_to(lse[g * QB:(g + 1) * QB], (QB, HP))

    qspec = pl.BlockSpec((QB, ATT_G * HP), lambda kv, qb: (qb, kv))
    kspec = pl.BlockSpec((lp, HP), lambda kv, qb: (0, kv))
    return pl.pallas_call(
        body, name="attn_fwd", grid=(ATT_KV, nq),
        in_specs=[qspec, kspec, kspec], out_specs=[qspec, qspec],
        out_shape=[jax.ShapeDtypeStruct((lp, ATT_H * HP), BF16), jax.ShapeDtypeStruct((lp, ATT_H * HP), F32)],
        compiler_params=_cparams(VMEM_BIG),
    )(qr, kr, vb)


def _attn_bwd(qr, kr, vb, o, lse, do):
    lp = qr.shape[0]
    nq = lp // QB

    def body(q_ref, k_ref, v_ref, o_ref, lse_ref, do_ref, dq_ref, dk_ref, dv_ref):
        qb = pl.program_id(1)

        @pl.when(qb == 0)
        def _():
            dk_ref[...] = jnp.zeros_like(dk_ref)
            dv_ref[...] = jnp.zeros_like(dv_ref)

        keys = lax.broadcasted_iota(jnp.int32, (1, lp), 1)
        k = k_ref[...]
        dk_acc, dv_acc = None, None
        for ch in range(ATT_G // GH):
            g0 = ch * GH
            qs = _stack(q_ref, g0, GH)
            dos = _stack(do_ref, g0, GH)
            os_ = _stack(o_ref, g0, GH).astype(F32)
            lse_s = jnp.concatenate([lse_ref[:, (g0 + g) * HP:(g0 + g) * HP + 1] for g in range(GH)], axis=0)
            delta = jnp.sum(dos * os_, axis=-1, keepdims=True) * LN2
            s = _dot(qs, k, "nt")
            p = jnp.where(keys >= NULL, jnp.exp2(s - lse_s), 0.0)
            dob = dos.astype(BF16)
            dp = _dot((dos * LN2).astype(BF16), v_ref[...], "nt")
            ds = (p * (dp - delta)).astype(BF16)
            dq = _dot(ds, k)
            for g in range(GH):
                dq_ref[:, (g0 + g) * HP:(g0 + g + 1) * HP] = dq[g * QB:(g + 1) * QB]
            dv_c = _dot(p.astype(BF16), dob, "tn")
            dk_c = _dot(ds, qs, "tn")
            dv_acc = dv_c if dv_acc is None else dv_acc + dv_c
            dk_acc = dk_c if dk_acc is None else dk_acc + dk_c
        dv_ref[...] += dv_acc
        dk_ref[...] += dk_acc

    qspec = pl.BlockSpec((QB, ATT_G * HP), lambda kv, qb: (qb, kv))
    kspec = pl.BlockSpec((lp, HP), lambda kv, qb: (0, kv))
    return pl.pallas_call(
        body, name="attn_bwd", grid=(ATT_KV, nq),
        in_specs=[qspec, kspec, kspec, qspec, qspec, qspec], out_specs=[qspec, kspec, kspec],
        out_shape=[jax.ShapeDtypeStruct((lp, ATT_H * HP), F32), jax.ShapeDtypeStruct((lp, ATT_KV * HP), F32),
                   jax.ShapeDtypeStruct((lp, ATT_KV * HP), F32)],
        compiler_params=_cparams(VMEM_BIG),
    )(qr, kr, vb, o, lse, do)


def _mixer_fwd(h, z, wl, l, tabs, next_gain):
    lp = h.shape[0]
    tm = _tm(lp)

    def id_epi(accs, exs, row0):
        return [accs[0]]

    (hin,) = _mm("in_proj", lp, D_INP, [_term(z, wl["win_t"], "nt", 0, (), (l,))], [((lp, D_INP), BF16, "mn", 0, ())],
                 id_epi, tm=tm, tn=D_INP // 2, vmem=VMEM_BIG)
    gates, cum = _gla_gates(hin, wl["w2p"][l], wl["b2p"][l])
    o_f, o_b, s_f, s_b = _gla_fwd(hin, cum)
    o2, states = (o_f, o_b), (s_f, s_b)
    a = _gla_out_fwd(o2, hin, wl["gn"][l])
    qr, kr, vb = _attn_prep(hin, wl["gq"][l], wl["gk"][l], tabs)
    b, lse = _attn_fwd(qr, kr, vb)

    def merge_epi(accs, exs, row0):
        pa, pb = accs
        ga, gb, bma, bmb = exs
        y = _sigmoid(ga + bma) * pa + _sigmoid(gb + bmb) * pb
        return [y, pa, pb]

    y, pa, pb = _mm("merge", lp, D, [_term(a, wl["wpa_t"], "nt", 0, (), (l,)), _term(b, wl["wpb_t"], "nt", 1, (), (l,))],
                    [((lp, D), BF16, "mn", 0, ())] * 3, merge_epi,
                    extras=[(hin, "mn", P_OFF["ga"] // D, ()), (hin, "mn", P_OFF["gb"] // D, ()),
                            (wl["bm"], "n", 0, (l, 0)), (wl["bm"], "n", 0, (l, 1))],
                    tm=tm, tn=D, nsub=D // MXU_N, i_outer=True, vmem=VMEM_BIG)

    h2, z2 = _mm("out_proj", lp, D, [_term(y, wl["wout"], "nn", 0, (), (l,))],
                 [((lp, D), F32, "mn", 0, ()), ((lp, D), BF16, "mn", 0, ())], _residual_norm_epi(1.0, True),
                 extras=[(h, "mn", 0, ()), (next_gain, "n", 0, ())], tm=tm, tn=D, i_outer=True, vmem=VMEM_BIG)
    sv = dict(h=h, z=z, hin=hin, gates=gates, cum=cum, o2=o2, states=states, a=a, qr=qr, kr=kr, vb=vb, b=b, lse=lse,
              y=y, pa=pa, pb=pb)
    return h2, z2, sv


def _mixer_bwd(dh, dhb, sv, gain, wl, l, tabs):
    lp = dh.shape[0]
    tm = _tm(lp)
    hin = sv["hin"]

    def merge_bwd_epi(accs, exs, row0):
        dy = accs[0]
        ga, gb, pa, pb, bma, bmb = exs
        sa = _sigmoid(ga + bma)
        sb = _sigmoid(gb + bmb)
        dga = dy * pa.astype(F32) * sa * (1.0 - sa)
        dgb = dy * pb.astype(F32) * sb * (1.0 - sb)
        return [dy * sa, dy * sb, dga, dgb, jnp.sum(dga, axis=0, keepdims=True), jnp.sum(dgb, axis=0, keepdims=True)]

    big = ((lp, D), BF16, "mn", 0, ())
    vec = ((1, D), F32, "nsum", 0, ())
    dpa, dpb, dga, dgb, dbma, dbmb = _mm(
        "merge_bwd", lp, D, [_term(dhb, wl["wout"], "nt", 0, (), (l,))], [big, big, big, big, vec, vec], merge_bwd_epi,
        extras=[(hin, "mn", P_OFF["ga"] // D, ()), (hin, "mn", P_OFF["gb"] // D, ()), (sv["pa"], "mn", 0, ()),
                (sv["pb"], "mn", 0, ()), (wl["bm"], "n", 0, (l, 0)), (wl["bm"], "n", 0, (l, 1))],
        tm=tm, tn=D, nsub=D // MXU_N, vmem=VMEM_BIG)
    d_wout = _dw("dw_out", sv["y"], dhb, D, D)
    d_wpa_t = _dw("dw_pa", dpa, sv["a"], D, 512)
    d_wpb_t = _dw("dw_pb", dpb, sv["b"], D, ATT_H * HP)

    def id_epi(accs, exs, row0):
        return [accs[0]]

    (da,) = _mm("d_a", lp, 512, [_term(dpa, wl["wpa_t"], "nn", 0, (), (l,))], [((lp, 512), F32, "mn", 0, ())], id_epi,
                tm=tm, tn=512, i_outer=True, vmem=VMEM_BIG)
    (db,) = _mm("d_b", lp, ATT_H * HP, [_term(dpb, wl["wpb_t"], "nn", 0, (), (l,))],
                [((lp, ATT_H * HP), F32, "mn", 0, ())], id_epi, tm=tm, tn=512, i_outer=True, vmem=VMEM_BIG)
    d_o, d_ra, d_gn = _gla_out_bwd(da, sv["o2"], hin, wl["gn"][l])
    g_fw, g_bw = _gla_bwd(hin, sv["cum"], sv["states"], d_o)
    dqr, dkr, dvb = _attn_bwd(sv["qr"], sv["kr"], sv["vb"], sv["b"], sv["lse"], db)
    d_qb, d_kb, d_gq, d_gk = _attn_prep_bwd(dqr, dkr, hin, wl["gq"][l], wl["gk"][l], tabs)
    dhin, d_w2p, d_b2p = _gla_in_bwd(g_fw, g_bw, sv["gates"], hin, wl["w2p"][l],
                                     dict(qb=d_qb, ga=dga, gb=dgb, ra=d_ra, kb=d_kb, vb=dvb))
    d_win_t = _dw("dw_in", dhin, sv["z"], D_INP, D)
    dh2, dhb2, dgain = _mm("in_proj_dz", lp, D, [_term(dhin, wl["win_t"], "nn", 0, (), (l,))], _norm_bwd_outs(lp),
                           _norm_bwd_epi, extras=[(sv["h"], "mn", 0, ()), (dh, "mn", 0, ()), (gain, "n", 0, ())],
                           tm=tm, tn=D, nk=1, vmem=VMEM_BIG)
    grads = dict(gain=dgain, wout=d_wout, wpa_t=d_wpa_t, wpb_t=d_wpb_t, win_t=d_win_t, gn=d_gn, w2p=d_w2p, b2p=d_b2p,
                 gq=d_gq, gk=d_gk, bma=dbma, bmb=dbmb)
    return dh2, dhb2, grads


def _mesh_pos():
    x, y, c = lax.axis_index("x"), lax.axis_index("y"), lax.axis_index("c")
    chips = [(1 - x, y), (x, 1 - y), (1 - x, 1 - y)]
    return x, y, c, chips


def _dev_index(x, y, c):
    return 4 * x + 2 * y + c


def _all_gather(name, shards, leads):
    nt = len(shards)

    def blk(ref, lead, idx):
        return ref.at[(slice(None),) * lead + (idx,)]

    def body(*refs):
        xs, outs = refs[:nt], refs[nt:2 * nt]
        send_sems, recv_sems, local_sems = refs[2 * nt:]
        x, y, c, chips = _mesh_pos()
        me, sibling = (x, y, c), (x, y, 1 - c)

        def copy(t, k, block, to, own=False):
            dst = blk(outs[t], leads[t], _dev_index(*block))
            return pltpu.make_async_remote_copy(
                src_ref=xs[t] if own else dst, dst_ref=dst, send_sem=send_sems.at[t, k], recv_sem=recv_sems.at[t, k],
                device_id=to, device_id_type=MESH)

        locals_ = [pltpu.make_async_copy(xs[t], blk(outs[t], leads[t], _dev_index(*me)), local_sems.at[t])
                   for t in range(nt)]
        for cp in locals_:
            cp.start()
        first = []
        for t in range(nt):
            first.append(copy(t, 0, me, sibling, own=True))
            first += [copy(t, 1 + j, me, (*chip, c), own=True) for j, chip in enumerate(chips)]
        for cp in first:
            cp.start()
        passed = []
        for j, chip in enumerate(chips):
            for t in range(nt):
                copy(t, 1 + j, (*chip, c), me).wait_recv()
                fw = copy(t, 4 + j, (*chip, c), sibling)
                fw.start()
                passed.append(fw)
        for t in range(nt):
            copy(t, 0, sibling, me).wait_recv()
        for j, chip in enumerate(chips):
            for t in range(nt):
                copy(t, 4 + j, (*chip, 1 - c), me).wait_recv()
        for cp in first + passed:
            cp.wait_send()
        for cp in locals_:
            cp.wait()

    out_shape = [jax.ShapeDtypeStruct(s.shape[:ld] + (NDEV,) + s.shape[ld:], s.dtype) for s, ld in zip(shards, leads)]
    return pl.pallas_call(
        body, name=name, in_specs=[ANY] * nt, out_specs=[ANY] * nt, out_shape=out_shape,
        scratch_shapes=[pltpu.SemaphoreType.DMA((nt, 7)), pltpu.SemaphoreType.DMA((nt, 7)),
                        pltpu.SemaphoreType.DMA((nt,))],
    )(*shards)


def _exchange_sibling(name, gs):
    nt = len(gs)

    def body(*refs):
        xs, outs = refs[:nt], refs[nt:2 * nt]
        send_sems, recv_sems = refs[2 * nt:]
        x, y, c, _ = _mesh_pos()
        sibling = (x, y, 1 - c)
        copies = []
        for t in range(nt):
            for ch in range(4):
                copies.append(pltpu.make_async_remote_copy(
                    src_ref=xs[t].at[2 * ch + (1 - c)], dst_ref=outs[t].at[ch],
                    send_sem=send_sems.at[t, ch], recv_sem=recv_sems.at[t, ch],
                    device_id=sibling, device_id_type=MESH))
        for cp in copies:
            cp.start()
        for cp in copies:
            cp.wait()

    out_shape = [jax.ShapeDtypeStruct((4,) + g.shape[1:], g.dtype) for g in gs]
    return pl.pallas_call(
        body, name=name, in_specs=[ANY] * nt, out_specs=[ANY] * nt, out_shape=out_shape,
        scratch_shapes=[pltpu.SemaphoreType.DMA((nt, 4)), pltpu.SemaphoreType.DMA((nt, 4))],
    )(*gs)


def _pair_sum(name, gs, recv):
    c = lax.axis_index("c")
    outs = []
    for t, (g, rv) in enumerate(zip(gs, recv)):
        _, r, cols = rv.shape

        def body(c_ref, g_ref, r_ref, o_ref):
            o_ref[...] = (g_ref[...].astype(F32) + r_ref[...].astype(F32)).astype(o_ref.dtype)

        outs.append(pl.pallas_call(
            body, name=f"{name}_{t}",
            grid_spec=pltpu.PrefetchScalarGridSpec(
                num_scalar_prefetch=1, grid=(4,),
                in_specs=[pl.BlockSpec((None, r, cols), lambda ch, cr: (2 * ch + cr[0], 0, 0)),
                          pl.BlockSpec((None, r, cols), lambda ch, cr: (ch, 0, 0))],
                out_specs=pl.BlockSpec((None, r, cols), lambda ch, cr: (ch, 0, 0))),
            out_shape=jax.ShapeDtypeStruct(rv.shape, rv.dtype),
        )(jnp.reshape(c, (1,)).astype(jnp.int32), g, rv))
    return outs


def _final_sum(name, ps, recv, transposed):
    chip = 2 * lax.axis_index("x") + lax.axis_index("y")
    outs = []
    for t, (p, rv) in enumerate(zip(ps, recv)):
        _, r, cols = rv.shape
        tr_out = transposed[t]
        oshape = (cols, r) if tr_out else (r, cols)

        def body(c_ref, p_ref, r0_ref, r1_ref, r2_ref, o_ref):
            acc = ((p_ref[...].astype(F32) + r0_ref[...].astype(F32)) + r1_ref[...].astype(F32)) + r2_ref[...].astype(F32)
            o_ref[...] = acc.T if tr_out else acc

        outs.append(pl.pallas_call(
            body, name=f"{name}_{t}",
            grid_spec=pltpu.PrefetchScalarGridSpec(
                num_scalar_prefetch=1, grid=(1,),
                in_specs=[pl.BlockSpec((None, r, cols), lambda i, cr: (cr[0], 0, 0))] +
                         [pl.BlockSpec((None, r, cols), lambda i, cr, j=j: (j, 0, 0)) for j in range(3)],
                out_specs=pl.BlockSpec(oshape, lambda i, cr: (0, 0))),
            out_shape=jax.ShapeDtypeStruct(oshape, F32),
        )(jnp.reshape(chip, (1,)).astype(jnp.int32), p, rv, rv, rv))
    return outs


def _sum_gathered(g):
    _, r, cols = g.shape

    def body(g_ref, o_ref):
        acc = g_ref[0]
        for d in range(1, NDEV):
            acc = acc + g_ref[d]
        o_ref[...] = acc

    return pl.pallas_call(body, name="small_sum", out_shape=jax.ShapeDtypeStruct((r, cols), F32))(g)


HBM = pl.BlockSpec(memory_space=pltpu.HBM)
SEM = pl.BlockSpec(memory_space=pltpu.SEMAPHORE)
EFFECT = pltpu.SideEffectType.DATAFLOW_SIDE_EFFECTING
NREL = NDEV - 1


def _related(k):
    x, y, c = lax.axis_index("x"), lax.axis_index("y"), lax.axis_index("c")
    px = 1 - x if k & 4 else x
    py = 1 - y if k & 2 else y
    pc = 1 - c if k & 1 else c
    return (px, py, pc), _dev_index(px, py, pc)


def _in_hbm(a):
    return pltpu.with_memory_space_constraint(a, pltpu.HBM)


ALL_RELS = tuple(range(1, NDEV))
CHIP_RELS = (4, 2, 6)


def _split_copies(xs, lands, send_sems, recv_sems, src_of, dst_of, rels):
    copies = []
    for t in range(len(xs)):
        for q, k in enumerate(rels):
            peer, peer_idx = _related(k)
            copies.append(pltpu.make_async_remote_copy(
                src_ref=src_of(xs[t], t, peer_idx), dst_ref=dst_of(lands[t], t, q, peer_idx),
                send_sem=send_sems.at[t * len(rels) + q], recv_sem=recv_sems.at[t * len(rels) + q],
                device_id=peer, device_id_type=MESH))
    return copies


def _exchange_start(name, xs, lands, src_of, dst_of, after, rels=ALL_RELS):
    nt = len(xs)

    def body(*refs):
        x_refs, land_refs = refs[:nt], refs[nt:2 * nt]
        send_sems, recv_sems = refs[2 * nt + 1], refs[2 * nt + 2]
        token = refs[-1]
        for cp in _split_copies(x_refs, land_refs, send_sems, recv_sems, src_of, dst_of, rels):
            cp.start()
        token[...] = jnp.zeros_like(token)

    res = pl.pallas_call(
        body, name=name,
        out_shape=(pltpu.SemaphoreType.DMA((nt * len(rels),)), pltpu.SemaphoreType.DMA((nt * len(rels),)),
                   *[pltpu.HBM(a.shape, a.dtype) for a in xs], *[pltpu.HBM(a.shape, a.dtype) for a in lands],
                   jax.ShapeDtypeStruct((8, 128), F32)),
        in_specs=[HBM] * (2 * nt) + [ANY],
        out_specs=(SEM, SEM, *[HBM] * (2 * nt), pl.BlockSpec(memory_space=pltpu.VMEM)),
        input_output_aliases={i: 2 + i for i in range(2 * nt)},
        compiler_params=pltpu.CompilerParams(has_side_effects=EFFECT),
    )(*[_in_hbm(a) for a in xs], *[_in_hbm(a) for a in lands], after)
    return res[0], res[1], res[2:2 + nt], res[2 + nt:2 + 2 * nt], res[-1]


def _exchange_wait(name, send_sems, recv_sems, xs, lands, src_of, dst_of, after, rels=ALL_RELS):
    nt = len(xs)

    def body(*refs):
        x_refs, land_refs = refs[:nt], refs[nt:2 * nt]
        send_sems, recv_sems = refs[2 * nt], refs[2 * nt + 1]
        for cp in _split_copies(x_refs, land_refs, send_sems, recv_sems, src_of, dst_of, rels):
            cp.wait_send()
            cp.wait_recv()

    res = pl.pallas_call(
        body, name=name,
        out_shape=(*[pltpu.HBM(a.shape, a.dtype) for a in xs], *[pltpu.HBM(a.shape, a.dtype) for a in lands]),
        in_specs=[HBM] * (2 * nt) + [SEM, SEM, ANY], out_specs=tuple([HBM] * (2 * nt)),
        input_output_aliases={i: i for i in range(2 * nt)},
        compiler_params=pltpu.CompilerParams(has_side_effects=EFFECT),
    )(*xs, *lands, send_sems, recv_sems, after)
    return res[:nt], res[nt:]


def _gather_start(name, shards, leads, after, rels=ALL_RELS):
    def src_of(x_ref, t, peer_idx):
        return x_ref

    def dst_of(land_ref, t, k, peer_idx):
        me = _dev_index(lax.axis_index("x"), lax.axis_index("y"), lax.axis_index("c"))
        return land_ref.at[(slice(None),) * leads[t] + (me,)]

    lands = [lax.empty(s.shape[:ld] + (NDEV,) + s.shape[ld:], s.dtype) for s, ld in zip(shards, leads)]
    return _exchange_start(name, shards, lands, src_of, dst_of, after, rels)


def _gather_wait(name, started, leads, after, rels=ALL_RELS):
    send_sems, recv_sems, shards, lands, _ = started

    def src_of(x_ref, t, peer_idx):
        return x_ref

    def dst_of(land_ref, t, k, peer_idx):
        return land_ref.at[(slice(None),) * leads[t] + (peer_idx,)]

    shards, lands = _exchange_wait(name, send_sems, recv_sems, shards, lands, src_of, dst_of, after, rels)
    me = _dev_index(lax.axis_index("x"), lax.axis_index("y"), lax.axis_index("c"))
    return [lax.dynamic_update_index_in_dim(g, s, me, ld) for g, s, ld in zip(lands, shards, leads)]


SIBLING_AND_CHIPS = (1,) + CHIP_RELS


def _forward_to_sibling(name, gathered, leads):
    nt = len(gathered)

    def body(*refs):
        ins, outs = refs[:nt], refs[nt:2 * nt]
        send_sems, recv_sems = refs[2 * nt:]
        x, y, c, chips = _mesh_pos()
        copies, arrivals = [], []
        for t in range(nt):
            for j, chip in enumerate(chips):
                def block(core):
                    return outs[t].at[(slice(None),) * leads[t] + (_dev_index(*chip, core),)]
                copies.append(pltpu.make_async_remote_copy(
                    src_ref=block(c), dst_ref=block(c), send_sem=send_sems.at[t, j], recv_sem=recv_sems.at[t, j],
                    device_id=(x, y, 1 - c), device_id_type=MESH))
                arrivals.append(pltpu.make_async_remote_copy(
                    src_ref=block(1 - c), dst_ref=block(1 - c), send_sem=send_sems.at[t, j], recv_sem=recv_sems.at[t, j],
                    device_id=(x, y, 1 - c), device_id_type=MESH))
        for cp in copies:
            cp.start()
        for cp in arrivals:
            cp.wait_recv()
        for cp in copies:
            cp.wait_send()

    return pl.pallas_call(
        body, name=name, in_specs=[ANY] * nt, out_specs=[ANY] * nt,
        out_shape=[jax.ShapeDtypeStruct(g.shape, g.dtype) for g in gathered],
        input_output_aliases={t: t for t in range(nt)},
        scratch_shapes=[pltpu.SemaphoreType.DMA((nt, 3)), pltpu.SemaphoreType.DMA((nt, 3))],
    )(*gathered)


def _scatter_src(x_ref, t, peer_idx):
    return x_ref.at[peer_idx]


def _scatter_dst(land_ref, t, q, peer_idx):
    return land_ref.at[q]


def _chips_src(x_ref, t, peer_idx):
    return x_ref.at[peer_idx // 2]


def _chips_start(name, ps, after):
    lands = [lax.empty((len(CHIP_RELS),) + p.shape[1:], p.dtype) for p in ps]
    return _exchange_start(name, ps, lands, _chips_src, _scatter_dst, after, CHIP_RELS)


def _chips_wait(name, started, after):
    send_sems, recv_sems, ps, lands, _ = started
    return _exchange_wait(name, send_sems, recv_sems, ps, lands, _chips_src, _scatter_dst, after, CHIP_RELS)


def _scatter_start(name, gs, after):
    lands = [lax.empty((NREL,) + g.shape[1:], g.dtype) for g in gs]
    return _exchange_start(name, gs, lands, _scatter_src, _scatter_dst, after)


def _scatter_wait(name, started, after, transposed):
    send_sems, recv_sems, gs, lands, _ = started
    gs, lands = _exchange_wait(name, send_sems, recv_sems, gs, lands, _scatter_src, _scatter_dst, after)
    me = _dev_index(lax.axis_index("x"), lax.axis_index("y"), lax.axis_index("c"))
    outs = []
    for t, (g, rv) in enumerate(zip(gs, lands)):
        _, r, cols = rv.shape
        tr_out = transposed[t]
        oshape = (cols, r) if tr_out else (r, cols)

        def body(c_ref, own_ref, rv_ref, o_ref):
            acc = own_ref[...].astype(F32)
            for k in range(NREL):
                acc = acc + rv_ref[k].astype(F32)
            o_ref[...] = acc.T if tr_out else acc

        outs.append(pl.pallas_call(
            body, name=f"{name}_sum_{t}",
            grid_spec=pltpu.PrefetchScalarGridSpec(
                num_scalar_prefetch=1, grid=(1,),
                in_specs=[pl.BlockSpec((None, r, cols), lambda i, cr: (cr[0], 0, 0)),
                          pl.BlockSpec((NREL, r, cols), lambda i, cr: (0, 0, 0))],
                out_specs=pl.BlockSpec(oshape, lambda i, cr: (0, 0))),
            out_shape=jax.ShapeDtypeStruct(oshape, F32), compiler_params=_cparams(VMEM_BIG),
        )(jnp.reshape(me, (1,)).astype(jnp.int32), g, rv))
    return outs


def _adamw(w, g, m, v):
    shape = w.shape
    cols = shape[-1]
    rows = math.prod(shape[:-1]) if len(shape) > 1 else 1
    w2, g2, m2, v2 = (jnp.reshape(t, (rows, cols)) for t in (w, g, m, v))
    tr = _pick(rows, (1024, 512, 256, 128)) if rows * cols > 65536 else rows
    c1 = 1.0 / (1.0 - ADAM_B1 ** ADAM_STEP)
    c2 = 1.0 / (1.0 - ADAM_B2 ** ADAM_STEP)

    def body(w_ref, g_ref, m_ref, v_ref, d_ref, nm_ref, nv_ref):
        gv = g_ref[...]
        nm = ADAM_B1 * m_ref[...] + (1.0 - ADAM_B1) * gv
        nv = ADAM_B2 * v_ref[...] + (1.0 - ADAM_B2) * (gv * gv)
        d_ref[...] = -ADAM_LR * ((nm * c1) / (jnp.sqrt(nv * c2) + ADAM_EPS) + ADAM_WD * w_ref[...])
        nm_ref[...] = nm
        nv_ref[...] = nv

    spec = pl.BlockSpec((tr, cols), lambda i: (i, 0))
    osh = jax.ShapeDtypeStruct((rows, cols), F32)
    d, nm, nv = pl.pallas_call(
        body, name="adamw", grid=(rows // tr,), in_specs=[spec] * 4, out_specs=[spec] * 3, out_shape=[osh] * 3,
        compiler_params=_cparams(VMEM_BIG),
    )(w2, g2, m2, v2)
    return jnp.reshape(d, shape), jnp.reshape(nm, shape), jnp.reshape(nv, shape)


def _pad_heads(w, name):
    if name not in P_HEADS:
        return w
    nh, real = P_HEADS[name]
    w = jnp.reshape(w, w.shape[:-2] + (nh, real, w.shape[-1]))
    w = jnp.pad(w, [(0, 0)] * (w.ndim - 2) + [(0, HP - real), (0, 0)])
    return jnp.reshape(w, w.shape[:-3] + (nh * HP, w.shape[-1]))


def _unpad_heads(w, name):
    if name not in P_HEADS:
        return w
    nh, real = P_HEADS[name]
    w = jnp.reshape(w, w.shape[:-2] + (nh, HP, w.shape[-1]))[..., :real, :]
    return jnp.reshape(w, w.shape[:-3] + (nh * real, w.shape[-1]))


def _win_pad(win_t):
    segs, o = {}, 0
    for n, s in zip(IN_NAMES, IN_SIZES):
        segs[n] = win_t[..., o:o + s, :]
        o += s
    return jnp.concatenate([_pad_heads(segs[n], n) for n in P_ORDER], axis=-2)


def _win_unpad(win_p):
    segs = {n: _unpad_heads(win_p[..., P_OFF[n]:P_OFF[n] + P_WIDTH[n], :], n) for n in P_ORDER}
    return jnp.concatenate([segs[n] for n in IN_NAMES], axis=-2)


def _t(w):
    return jnp.swapaxes(w, -1, -2)


def _ffn_stacked(g_g, g_u, g_d):
    wg, wu, wd = (jnp.reshape(g, (2, DFF, D)) for g in (g_g, g_u, g_d))
    return [(wg, wu, wd, (j,)) for j in range(2)]


def _ffn_single(g_g, g_u, g_d):
    return tuple(jnp.reshape(g, (DFF, D)) for g in (g_g, g_u, g_d)) + ((),)


def _layer_weights(ffn, g_in, g_pa, g_pb, g_out, gains, w2, b2, bm, gn, gq, gk):
    w2p = jnp.pad(jnp.reshape(w2, (2, GLA_RANK, GLA_H, GLA_DK)), ((0, 0), (0, HP - GLA_RANK), (0, 0), (0, HP - GLA_DK)))
    b2p = jnp.pad(jnp.reshape(b2, (2, 1, GLA_H, GLA_DK)), ((0, 0), (0, 0), (0, 0), (0, HP - GLA_DK)))
    wpb_t = jnp.pad(jnp.reshape(g_pb, (D, ATT_H, HEAD_DIM)), ((0, 0), (0, 0), (0, HP - HEAD_DIM)))
    return dict(
        gains=jnp.reshape(gains, (1, 3, 1, D)), ffn=ffn,
        win_t=_win_pad(jnp.reshape(g_in, (1, D_IN, D))), wpa_t=jnp.reshape(g_pa, (1, D, 512)),
        wpb_t=jnp.reshape(wpb_t, (1, D, ATT_H * HP)), wout=jnp.reshape(g_out, (1, D, D)),
        w2p=jnp.reshape(w2p, (1, 2, HP, GLA_H * HP)), b2p=jnp.reshape(b2p, (1, 2, 1, GLA_H * HP)),
        bm=jnp.reshape(bm, (1, 2, 1, D)), gn=jnp.reshape(gn, (1, 1, GLA_H * HP)),
        gq=jnp.pad(jnp.reshape(gq, (1, 1, HEAD_DIM)), ((0, 0), (0, 0), (0, HP - HEAD_DIM))),
        gk=jnp.pad(jnp.reshape(gk, (1, 1, HEAD_DIM)), ((0, 0), (0, 0), (0, HP - HEAD_DIM))))


def _layer_fwd_lower(h, z, ffn0, gain1):
    return _ffn_fwd(h, z, *ffn0, gain1)


def _layer_fwd_upper(h, z, s0, w, tabs, next_gain):
    h, z, s1 = _mixer_fwd(h, z, w, 0, tabs, w["gains"][0, 2])
    h, z, s2 = _ffn_fwd(h, z, *w["ffn"][1], next_gain)
    return h, z, (s0, s1, s2)


def _layer_fwd(h, z, w, tabs, next_gain):
    h, z, s0 = _layer_fwd_lower(h, z, w["ffn"][0], w["gains"][0, 1])
    return _layer_fwd_upper(h, z, s0, w, tabs, next_gain)


def _layer_bwd_upper(dh, dhb, saved, w, tabs):
    _, s1, s2 = saved
    dh, dhb, dg2, dwg1, dwu1, dwd1 = _ffn_bwd(dh, dhb, s2, w["gains"][0, 2], *w["ffn"][1])
    dh, dhb, gm = _mixer_bwd(dh, dhb, s1, w["gains"][0, 1], w, 0, tabs)
    gm.update(gain2=dg2, wg1=dwg1, wu1=dwu1, wd1=dwd1)
    return dh, dhb, gm


def _layer_bwd_lower(dh, dhb, saved, w, gm):
    dh, dhb, dg0, dwg0, dwu0, dwd0 = _ffn_bwd(dh, dhb, saved[0], w["gains"][0, 0], *w["ffn"][0])
    gm.update(gain0=dg0, wg0=dwg0, wu0=dwu0, wd0=dwd0)
    return dh, dhb, gm


def _layer_bwd(dh, dhb, saved, w, tabs):
    dh, dhb, gm = _layer_bwd_upper(dh, dhb, saved, w, tabs)
    return _layer_bwd_lower(dh, dhb, saved, w, gm)


def _blocks(ts):
    return [jnp.reshape(t, (NDEV, t.shape[0] // NDEV, t.shape[1])) for t in ts]


def _upper_grads(g):
    d_in = _win_unpad(g["win_t"])
    d_pb = jnp.reshape(jnp.reshape(g["wpb_t"], (D, ATT_H, HP))[:, :, :HEAD_DIM], (D, 512))
    return _blocks([g["wg1"], g["wu1"], g["wd1"], d_in, g["wpa_t"], d_pb, g["wout"]])


def _lower_grads(g):
    return _blocks([g["wg0"], g["wu0"], g["wd0"]])


def _big_grads(g):
    return _lower_grads(g) + _upper_grads(g)


def kernel(x, meta_tokens, norm_gains, ffn_w_gate, ffn_w_up, ffn_w_down, w_in, gla_w2, gla_b2, gla_gn, q_norm, k_norm, w_pa, w_pb, b_merge, w_out, final_norm, loss_target, m_meta_tokens, m_norm_gains, m_ffn_w_gate, m_ffn_w_up, m_ffn_w_down, m_w_in, m_gla_w2, m_gla_b2, m_gla_gn, m_q_norm, m_k_norm, m_w_pa, m_w_pb, m_b_merge, m_w_out, m_final_norm, v_meta_tokens, v_norm_gains, v_ffn_w_gate, v_ffn_w_up, v_ffn_w_down, v_w_in, v_gla_w2, v_gla_b2, v_gla_gn, v_q_norm, v_k_norm, v_w_pa, v_w_pb, v_b_merge, v_w_out, v_final_norm):
    dev = _dev_index(lax.axis_index("x"), lax.axis_index("y"), lax.axis_index("c"))
    sh_g = _t(ffn_w_gate).astype(BF16)
    sh_u = _t(ffn_w_up).astype(BF16)
    sh_d = ffn_w_down.astype(BF16)
    sh_in = _t(w_in).astype(BF16)
    sh_pa = _t(w_pa).astype(BF16)
    sh_pb = _t(w_pb).astype(BF16)
    sh_out = w_out.astype(BF16)
    small = jnp.concatenate([jnp.reshape(t, (-1, 128)) for t in
                             (meta_tokens, norm_gains, gla_w2, gla_b2, b_merge)], axis=0)
    small = jnp.pad(small, ((0, 2), (0, 0)))
    def shards(l):
        return [sh_g[l], sh_u[l], sh_d[l], sh_in[l], sh_pa[l], sh_pb[l], sh_out[l]]

    w_leads = [1, 1, 1, 0, 0, 0, 0]
    *g0_ffn0, g_small = _all_gather("gather_layer0", [sh_g[0, 0], sh_u[0, 0], sh_d[0, 0], small], [0, 0, 0, 0])
    rest0 = [sh_g[0, 1], sh_u[0, 1], sh_d[0, 1], sh_in[0], sh_pa[0], sh_pb[0], sh_out[0]]
    rest_leads = [0] * len(rest0)
    started0 = _gather_start("gather_start_0", rest0, rest_leads, g_small, SIBLING_AND_CHIPS)
    meta_full = jnp.reshape(jnp.transpose(g_small[:, 0:16], (1, 0, 2)), (NMETA, D)) + started0[4][0, 0]
    gains_full = jnp.reshape(jnp.transpose(jnp.reshape(g_small[:, 16:28], (NDEV, DEPTH, 3, 128)), (1, 2, 0, 3)), (DEPTH, 3, D))
    w2_full = jnp.reshape(jnp.transpose(jnp.reshape(g_small[:, 28:60], (NDEV, DEPTH, 2, GLA_RANK, 32)), (1, 2, 3, 0, 4)),
                          (DEPTH, 2, GLA_RANK, 256))
    b2_full = jnp.reshape(jnp.transpose(jnp.reshape(g_small[:, 60:62], (NDEV, DEPTH, 2, 32)), (1, 2, 0, 3)), (DEPTH, 2, 256))
    bm_full = jnp.reshape(jnp.transpose(jnp.reshape(g_small[:, 62:70], (NDEV, DEPTH, 2, 128)), (1, 2, 0, 3)), (DEPTH, 2, D))

    def layer_weights(l, ffn, others, gains_l):
        return _layer_weights(ffn, *others, gains_l, w2_full[l], b2_full[l], bm_full[l], gla_gn[l], q_norm[l], k_norm[l])

    xl = x[0]
    lp = xl.shape[0] + NULL + NMETA
    tabs = _rope_tables(lp)
    h = jnp.concatenate([jnp.zeros((NULL, D), F32), meta_full, xl], axis=0)
    weights, saved, started = [], [], {}
    z = _rmsnorm_fwd(h, jnp.reshape(gains_full[0, 0], (1, D)))
    for l in range(DEPTH):
        next_gain = jnp.reshape(gains_full[l + 1, 0], (1, D)) if l + 1 < DEPTH else None
        if l == 0:
            ffn0 = _ffn_single(*g0_ffn0)
            h, z, s0 = _layer_fwd_lower(h, z, ffn0, jnp.reshape(gains_full[0, 1], (1, D)))
            rest = _forward_to_sibling("gather_forward_0", _gather_wait("gather_wait_0", started0, rest_leads, h,
                                                                         SIBLING_AND_CHIPS), rest_leads)
            started[1] = _gather_start("gather_start_1", shards(1), w_leads, rest[0])
            weights.append(layer_weights(0, [ffn0, _ffn_single(*rest[:3])], rest[3:], gains_full[0]))
            z = z + started[1][4][0, 0].astype(BF16)
            h, z, sv = _layer_fwd_upper(h, z, s0, weights[0], tabs, next_gain)
        else:
            tok = jnp.zeros((), F32)
            if l < DEPTH - 1:
                started[l + 1] = _gather_start(f"gather_start_{l + 1}", shards(l + 1), w_leads, h)
                tok = started[l + 1][4][0, 0]
            gathered = _gather_wait(f"gather_wait_{l}", started[l], w_leads, h)
            weights.append(layer_weights(l, _ffn_stacked(*gathered[:3]), gathered[3:], gains_full[l] + tok))
            h, z, sv = _layer_fwd(h, z, weights[l], tabs, next_gain)
        saved.append(sv)
    loss, dh, dhb, d_final = _loss_head(h, loss_target[0], jnp.reshape(final_norm, (1, D)))
    loss = lax.psum(loss[0, 0], ("x", "y", "c"))

    grads, scattering = [None] * DEPTH, {}
    tok = jnp.zeros((), F32)
    for l in reversed(range(DEPTH)):
        w = dict(weights[l], gains=weights[l]["gains"] + tok)
        if l > 0:
            dh, dhb, grads[l] = _layer_bwd(dh, dhb, saved[l], w, tabs)
            scattering[l] = _scatter_start(f"scatter_start_{l}", _big_grads(grads[l]), dhb)
            tok = scattering[l][4][0, 0]
        else:
            dh, dhb, gm = _layer_bwd_upper(dh, dhb, saved[l], w, tabs)
            ups = _upper_grads(gm)
            pair = _pair_sum("rs_pair_up", ups, _exchange_sibling("rs_sibling_up", ups))
            scattering[l] = _chips_start(f"scatter_start_{l}", pair, dhb)
            dhb = dhb + scattering[l][4][0, 0].astype(BF16)
            dh, dhb, grads[l] = _layer_bwd_lower(dh, dhb, saved[l], w, gm)
    grad_x = dh[NULL + NMETA:][None]
    t_lower, t_upper = [True, True, False], [True, True, False, True, True, True, False]
    lows = _lower_grads(grads[0])
    pair_lo = _pair_sum("rs_pair_lo", lows, _exchange_sibling("rs_sibling_lo", lows))
    started_lo = _chips_start("scatter_start_lo", pair_lo, dhb)
    red = [None] * DEPTH
    for l in reversed(range(1, DEPTH)):
        red[l] = _scatter_wait(f"scatter_wait_{l}", scattering[l], started_lo[4], t_lower + t_upper)
    pair, recv = _chips_wait("scatter_wait_0", scattering[0], red[1][-1])
    red_upper = _final_sum("rs_sum_up", pair, recv, t_upper)
    pair_lo, recv_lo = _chips_wait("scatter_wait_lo", started_lo, red_upper[-1])
    red[0] = _final_sum("rs_sum_lo", pair_lo, recv_lo, t_lower) + red_upper
    g_gate = jnp.stack([jnp.stack([red[l][0], red[l][3]]) for l in range(DEPTH)])
    g_up = jnp.stack([jnp.stack([red[l][1], red[l][4]]) for l in range(DEPTH)])
    g_down = jnp.stack([jnp.stack([red[l][2], red[l][5]]) for l in range(DEPTH)])
    g_win = jnp.stack([red[l][6] for l in range(DEPTH)])
    g_wpa = jnp.stack([red[l][7] for l in range(DEPTH)])
    g_wpb = jnp.stack([red[l][8] for l in range(DEPTH)])
    g_wout = jnp.stack([red[l][9] for l in range(DEPTH)])

    d_meta = dh[NULL:NULL + NMETA]
    d_gains = jnp.stack([jnp.concatenate([grads[l]["gain0"], grads[l]["gain"], grads[l]["gain2"]], axis=0)
                         for l in range(DEPTH)])
    d_w2 = jnp.stack([jnp.reshape(jnp.reshape(grads[l]["w2p"], (2, HP, GLA_H, HP))[:, :GLA_RANK, :, :GLA_DK],
                                  (2, GLA_RANK, 256)) for l in range(DEPTH)])
    d_b2 = jnp.stack([jnp.reshape(jnp.reshape(grads[l]["b2p"], (2, GLA_H, HP))[:, :, :GLA_DK], (2, 256))
                      for l in range(DEPTH)])
    d_gn = jnp.stack([grads[l]["gn"][0] for l in range(DEPTH)])
    d_gq = jnp.stack([grads[l]["gq"][0, :HEAD_DIM] for l in range(DEPTH)])
    d_gk = jnp.stack([grads[l]["gk"][0, :HEAD_DIM] for l in range(DEPTH)])
    d_bm = jnp.stack([jnp.concatenate([grads[l]["bma"], grads[l]["bmb"]], axis=0) for l in range(DEPTH)])
    parts = [d_meta, d_gains, d_w2, d_b2, d_gn, d_gq, d_gk, d_bm, d_final[0]]
    sizes = [p.size for p in parts]
    flat = jnp.concatenate([jnp.reshape(p, (-1,)) for p in parts])
    flat = jnp.reshape(flat, (-1, 128))
    nrow = flat.shape[0]
    flat = jnp.pad(flat, ((0, (-nrow) % 8), (0, 0)))
    (g_flat,) = _all_gather("gather_small_grads", [flat], [0])
    tot = jnp.reshape(_sum_gathered(g_flat), (-1,))
    full, o = [], 0
    for p, s in zip(parts, sizes):
        full.append(jnp.reshape(tot[o:o + s], p.shape))
        o += s
    f_meta, f_gains, f_w2, f_b2, f_gn, f_gq, f_gk, f_bm, f_final = full

    def mine(t, width):
        return lax.dynamic_slice_in_dim(t, dev * width, width, axis=t.ndim - 1)

    g_small = dict(meta_tokens=mine(f_meta, 128), norm_gains=mine(f_gains, 128), gla_w2=mine(f_w2, 32),
                   gla_b2=mine(f_b2, 32), gla_gn=f_gn, q_norm=f_gq, k_norm=f_gk, b_merge=mine(f_bm, 128),
                   final_norm=f_final)
    gr = dict(g_small, ffn_w_gate=g_gate, ffn_w_up=g_up, ffn_w_down=g_down, w_in=g_win, w_pa=g_wpa, w_pb=g_wpb,
              w_out=g_wout)
    ws = dict(meta_tokens=meta_tokens, norm_gains=norm_gains, ffn_w_gate=ffn_w_gate, ffn_w_up=ffn_w_up,
              ffn_w_down=ffn_w_down, w_in=w_in, gla_w2=gla_w2, gla_b2=gla_b2, gla_gn=gla_gn, q_norm=q_norm,
              k_norm=k_norm, w_pa=w_pa, w_pb=w_pb, b_merge=b_merge, w_out=w_out, final_norm=final_norm)
    ms = dict(meta_tokens=m_meta_tokens, norm_gains=m_norm_gains, ffn_w_gate=m_ffn_w_gate, ffn_w_up=m_ffn_w_up,
              ffn_w_down=m_ffn_w_down, w_in=m_w_in, gla_w2=m_gla_w2, gla_b2=m_gla_b2, gla_gn=m_gla_gn, q_norm=m_q_norm,
              k_norm=m_k_norm, w_pa=m_w_pa, w_pb=m_w_pb, b_merge=m_b_merge, w_out=m_w_out, final_norm=m_final_norm)
    vs = dict(meta_tokens=v_meta_tokens, norm_gains=v_norm_gains, ffn_w_gate=v_ffn_w_gate, ffn_w_up=v_ffn_w_up,
              ffn_w_down=v_ffn_w_down, w_in=v_w_in, gla_w2=v_gla_w2, gla_b2=v_gla_b2, gla_gn=v_gla_gn, q_norm=v_q_norm,
              k_norm=v_k_norm, w_pa=v_w_pa, w_pb=v_w_pb, b_merge=v_b_merge, w_out=v_w_out, final_norm=v_final_norm)
    names = ["meta_tokens", "norm_gains", "ffn_w_gate", "ffn_w_up", "ffn_w_down", "w_in", "gla_w2", "gla_b2", "gla_gn",
             "q_norm", "k_norm", "w_pa", "w_pb", "b_merge", "w_out", "final_norm"]
    deltas, new_m, new_v = [], [], []
    for n in names:
        dlt, nm, nv = _adamw(ws[n], gr[n], ms[n], vs[n])
        deltas.append(dlt)
        new_m.append(nm)
        new_v.append(nv)
    return (loss, grad_x, *[gr[n] for n in names], *deltas, *new_m, *new_v)
```

```python
import functools
import math

import jax
import jax.numpy as jnp
import numpy as np
from jax import lax
from jax.experimental import pallas as pl
from jax.experimental.pallas import tpu as pltpu

F32 = jnp.float32
BF16 = jnp.bfloat16
MESH = pl.DeviceIdType.MESH
ANY = pl.BlockSpec(memory_space=pl.ANY)

NDEV = 8
D = 1024
DFF = 2816
DEPTH = 4
NMETA = 16
NULL = 112
GRID_W = 64
EPS = 1e-6
HP = 128
GLA_H = 4
GLA_DK = 64
GLA_RANK = 16
GLA_TAU = 16.0
CHUNK = 64
ATT_H = 8
ATT_KV = 2
ATT_G = ATT_H // ATT_KV
HEAD_DIM = 64
ROPE_THETA = 10000.0

IN_SIZES = (256, 256, 512, 512, 16, 16, 512, 128, 128, 1024, 1024)
IN_NAMES = ("qa", "ka", "va", "ra", "lrf", "lrb", "qb", "kb", "vb", "ga", "gb")
D_IN = sum(IN_SIZES)
P_ORDER = ("qb", "ga", "gb", "qa", "ka", "va", "ra", "kb", "vb", "lrf", "lrb")
P_WIDTH = dict(qb=1024, ga=1024, gb=1024, qa=512, ka=512, va=512, ra=512, kb=256, vb=256, lrf=128, lrb=128)
P_OFF = {}
_o = 0
for _n in P_ORDER:
    P_OFF[_n] = _o
    _o += P_WIDTH[_n]
D_INP = _o
P_HEADS = dict(qa=(4, 64), ka=(4, 64), qb=(8, 64), kb=(2, 64), vb=(2, 64), lrf=(1, 16), lrb=(1, 16))

ADAM_LR = 0.001
ADAM_B1 = 0.9
ADAM_B2 = 0.999
ADAM_EPS = 1e-08
ADAM_WD = 0.01
ADAM_STEP = 10

VMEM_BIG = 58 * 1024 * 1024
MXU_N = 256


def _cparams(vmem=None):
    return pltpu.CompilerParams(vmem_limit_bytes=vmem) if vmem else pltpu.CompilerParams()


def _pick(n, prefs):
    for p in prefs:
        if n % p == 0:
            return p
    return n


def _tm(lp):
    return _pick(lp, (528, 512, 256, 128))


def _tm_wide(lp):
    return _pick(lp, (1056, 512, 256, 128))


_DN = {"nn": (((1,), (0,)), ((), ())), "nt": (((1,), (1,)), ((), ())), "tn": (((0,), (0,)), ((), ()))}


def _dot(a, b, mode="nn", precision=None):
    return lax.dot_general(a, b, _DN[mode], preferred_element_type=F32, precision=precision)


def _split(x):
    hi = x.astype(BF16)
    return hi, (x - hi.astype(F32)).astype(BF16)


def _dot_sel(t, x, mode="nn"):
    hi, lo = _split(x)
    return _dot(t, hi, mode) + _dot(t, lo, mode)


def _dot3(a, b, mode="nn"):
    ah, al = _split(a)
    bh, bl = _split(b)
    return _dot(ah, bh, mode) + (_dot(ah, bl, mode) + _dot(al, bh, mode))


def _sigmoid(x):
    return 0.5 * jnp.tanh(0.5 * x) + 0.5


def _mm(name, m, n, terms, outs, epilogue, extras=(), *, tm, tn, nk=1, nsub=1, i_outer=False, vmem=None):
    gm, gn = m // tm, n // tn
    assert gm * tm == m and gn * tn == n, (name, m, n, tm, tn)
    n_acc = 1 + max(t[3] for t in terms)

    def gmap(f):
        if i_outer:
            return lambda i, j, kk: f(i, j, kk)
        return lambda j, i, kk: f(i, j, kk)

    in_specs, args = [], []
    for a, b, mode, _, pa, pb in terms:
        kdim = a.shape[-2] if mode == "tn" else a.shape[-1]
        tk = kdim // nk
        assert tk * nk == kdim
        na, nb = (None,) * len(pa), (None,) * len(pb)
        if mode == "tn":
            in_specs.append(pl.BlockSpec(na + (tk, tm), gmap(lambda i, j, kk, pa=pa: pa + (kk, i))))
        else:
            in_specs.append(pl.BlockSpec(na + (tm, tk), gmap(lambda i, j, kk, pa=pa: pa + (i, kk))))
        if mode == "nt":
            in_specs.append(pl.BlockSpec(nb + (tn, tk), gmap(lambda i, j, kk, pb=pb: pb + (j, kk))))
        else:
            in_specs.append(pl.BlockSpec(nb + (tk, tn), gmap(lambda i, j, kk, pb=pb: pb + (kk, j))))
        args += [a, b]
    for arr, kind, off, pe in extras:
        ne = (None,) * len(pe)
        if kind == "mn":
            in_specs.append(pl.BlockSpec(ne + (tm, tn), gmap(lambda i, j, kk, off=off, pe=pe: pe + (i, j + off))))
        else:
            in_specs.append(pl.BlockSpec(ne + (1, tn), gmap(lambda i, j, kk, off=off, pe=pe: pe + (0, j + off))))
        args.append(arr)
    out_shape, out_specs = [], []
    for shape, dtype, kind, off, po in outs:
        no = (None,) * len(po)
        out_shape.append(jax.ShapeDtypeStruct(shape, dtype))
        if kind == "mn":
            out_specs.append(pl.BlockSpec(no + (tm, tn), gmap(lambda i, j, kk, off=off, po=po: po + (i, j + off))))
        else:
            assert not i_outer
            out_specs.append(pl.BlockSpec(no + (1, tn), gmap(lambda i, j, kk, off=off, po=po: po + (0, j + off))))
    n_t, n_e, n_o = len(terms), len(extras), len(outs)
    i_axis = 0 if i_outer else 1

    def body(*refs):
        ins = refs[: 2 * n_t]
        exs = refs[2 * n_t: 2 * n_t + n_e]
        ors = refs[2 * n_t + n_e: 2 * n_t + n_e + n_o]
        accs = refs[2 * n_t + n_e + n_o:]
        i = pl.program_id(i_axis)
        kk = pl.program_id(2)

        def partials(cs):
            part = [None] * n_acc
            for t, (_, _, mode, ai, _, _) in enumerate(terms):
                b_ref = ins[2 * t + 1]
                b_val = b_ref[cs, :] if mode == "nt" else b_ref[:, cs]
                p = _dot(ins[2 * t][...], b_val, mode)
                part[ai] = p if part[ai] is None else part[ai] + p
            return part

        def finish(vals, cs):
            res = epilogue(vals, [e[:, cs] for e in exs], i * tm)
            for (_, dtype, kind, _, _), o_ref, v in zip(outs, ors, res):
                if kind == "mn":
                    o_ref[:, cs] = v.astype(dtype)
                else:
                    @pl.when(i == 0)
                    def _():
                        o_ref[:, cs] = v.astype(dtype)

                    @pl.when(i != 0)
                    def _():
                        o_ref[:, cs] += v.astype(dtype)

        if nk == 1:
            w = tn // nsub
            for s in range(nsub):
                cs = slice(s * w, (s + 1) * w)
                finish(partials(cs), cs)
        else:
            part = partials(slice(None))
            @pl.when(kk == 0)
            def _():
                for a_ref, p in zip(accs, part):
                    a_ref[...] = p

            @pl.when(kk != 0)
            def _():
                for a_ref, p in zip(accs, part):
                    a_ref[...] += p

            @pl.when(kk == nk - 1)
            def _():
                finish([a_ref[...] for a_ref in accs], slice(None))

    scratch = [pltpu.VMEM((tm, tn), F32) for _ in range(n_acc)] if nk > 1 else []
    grid = (gm, gn, nk) if i_outer else (gn, gm, nk)
    res = pl.pallas_call(
        body, name=name, grid=grid, in_specs=in_specs, out_specs=out_specs, out_shape=out_shape,
        scratch_shapes=scratch, compiler_params=_cparams(vmem),
    )(*args)
    return res


def _term(a, b, mode, acc=0, pa=(), pb=()):
    return (a, b, mode, acc, tuple(pa), tuple(pb))


def _row_tile(lp):
    return _pick(lp, (384, 256, 128))


def _rmsnorm_fwd(h, gain):
    lp = h.shape[0]
    tr = _row_tile(lp)

    def body(h_ref, g_ref, z_ref):
        x = h_ref[...]
        r = lax.rsqrt(jnp.mean(x * x, axis=-1, keepdims=True) + EPS)
        z_ref[...] = (x * r * g_ref[...]).astype(BF16)

    return pl.pallas_call(
        body, name="rmsnorm_fwd", grid=(lp // tr,),
        in_specs=[pl.BlockSpec((tr, D), lambda i: (i, 0)), pl.BlockSpec((1, D), lambda i: (0, 0))],
        out_specs=pl.BlockSpec((tr, D), lambda i: (i, 0)),
        out_shape=jax.ShapeDtypeStruct((lp, D), BF16),
    )(h, gain)


def _loss_head(h, target, gain):
    lp = h.shape[0]
    tr = 128

    def body(h_ref, t_ref, g_ref, loss_ref, dh_ref, dhb_ref, dg_ref):
        i = pl.program_id(0)

        @pl.when(i == 0)
        def _():
            loss_ref[...] = jnp.zeros_like(loss_ref)
            dg_ref[...] = jnp.zeros_like(dg_ref)
            dh_ref[...] = jnp.zeros_like(dh_ref)
            dhb_ref[...] = jnp.zeros_like(dhb_ref)

        @pl.when(i != 0)
        def _():
            x = h_ref[...]
            g = g_ref[...]
            r = lax.rsqrt(jnp.mean(x * x, axis=-1, keepdims=True) + EPS)
            xh = x * r
            y = xh * g
            err = y - t_ref[...]
            loss_ref[...] += 0.5 * jnp.sum(jnp.sum(err * err, axis=-1, keepdims=True), axis=0, keepdims=True) / D
            dy = err * (1.0 / D)
            dg_ref[...] += jnp.sum(dy * xh, axis=0, keepdims=True)
            dxh = dy * g
            dx = r * (dxh - xh * jnp.mean(dxh * xh, axis=-1, keepdims=True))
            dh_ref[...] = dx
            dhb_ref[...] = dx.astype(BF16)

    row = pl.BlockSpec((tr, D), lambda i: (i, 0))
    vec = pl.BlockSpec((1, D), lambda i: (0, 0))
    return pl.pallas_call(
        body, name="loss_head", grid=(lp // tr,),
        in_specs=[row, pl.BlockSpec((tr, D), lambda i: (jnp.maximum(i - 1, 0), 0)), vec],
        out_specs=[pl.BlockSpec((1, 1), lambda i: (0, 0)), row, row, vec],
        out_shape=[jax.ShapeDtypeStruct((1, 1), F32), jax.ShapeDtypeStruct((lp, D), F32),
                   jax.ShapeDtypeStruct((lp, D), BF16), jax.ShapeDtypeStruct((1, D), F32)],
    )(h, target, gain)


def _silu_parts(g):
    s = _sigmoid(g)
    return g * s, s * (1.0 + g * (1.0 - s))


def _residual_norm_epi(scale, with_norm):
    def epi(accs, exs, row0):
        h2 = exs[0] + scale * accs[0]
        if not with_norm:
            return [h2]
        r = lax.rsqrt(jnp.mean(h2 * h2, axis=-1, keepdims=True) + EPS)
        return [h2, h2 * r * exs[1]]
    return epi


def _norm_bwd_epi(accs, exs, row0):
    dz = accs[0]
    x, res, g = exs
    r = lax.rsqrt(jnp.mean(x * x, axis=-1, keepdims=True) + EPS)
    xh = x * r
    dxh = dz * g
    dx = r * (dxh - xh * jnp.mean(dxh * xh, axis=-1, keepdims=True))
    rows = row0 + lax.broadcasted_iota(jnp.int32, (dz.shape[0], 1), 0)
    dh = jnp.where(rows >= NULL, res + dx, 0.0)
    return [dh, dh, jnp.sum(dz * xh, axis=0, keepdims=True)]


def _norm_bwd_outs(lp):
    return [((lp, D), F32, "mn", 0, ()), ((lp, D), BF16, "mn", 0, ()), ((1, D), F32, "nsum", 0, ())]


def _ffn_fwd(h, z, wg_t, wu_t, wd, pre, next_gain):
    lp = h.shape[0]
    tm = _tm(lp)

    def up_epi(accs, exs, row0):
        g, u = accs
        sg, _ = _silu_parts(g)
        return [g, u, sg * u]

    bshape = (lp, DFF)
    g_, u_, act = _mm("ffn_up", lp, DFF, [_term(z, wg_t, "nt", 0, (), pre), _term(z, wu_t, "nt", 1, (), pre)],
                      [(bshape, BF16, "mn", 0, ())] * 3, up_epi, tm=tm, tn=DFF, nsub=DFF // MXU_N, vmem=VMEM_BIG)

    with_norm = next_gain is not None
    res = _mm("ffn_down", lp, D, [_term(act, wd, "nn", 0, (), pre)],
              [((lp, D), F32, "mn", 0, ())] + ([((lp, D), BF16, "mn", 0, ())] if with_norm else []),
              _residual_norm_epi(0.5, with_norm),
              extras=[(h, "mn", 0, ())] + ([(next_gain, "n", 0, ())] if with_norm else []),
              tm=_tm_wide(lp), tn=D, i_outer=True, vmem=VMEM_BIG)
    return res[0], (res[1] if with_norm else None), dict(h=h, z=z, g=g_, u=u_, act=act)


def _dw(name, a, b, m, n, scale=1.0):
    lp = a.shape[0]
    tm = _pick(m, (2944, 1408, 1024, 512, 256, 128))
    tn = _pick(n, (1024, 512, 256, 128)) if tm <= 1408 else _pick(n, (512, 256, 128))
    nk = lp // _pick(lp, (2112, 256, 128))

    def epi(accs, exs, row0):
        return [accs[0] * scale]

    (w,) = _mm(name, m, n, [_term(a, b, "tn")], [((m, n), BF16, "mn", 0, ())], epi, tm=tm, tn=tn, nk=nk,
               i_outer=True, vmem=VMEM_BIG)
    return w


def _ffn_bwd(dh, dhb, sv, gain, wg_t, wu_t, wd, pre):
    lp = dh.shape[0]
    tm = _tm(lp)

    def dact_epi(accs, exs, row0):
        g = exs[0].astype(F32)
        u = exs[1].astype(F32)
        da = 0.5 * accs[0]
        sg, dsg = _silu_parts(g)
        return [da * u * dsg, da * sg]

    dg_, du_ = _mm("ffn_dact", lp, DFF, [_term(dhb, wd, "nt", 0, (), pre)],
                   [((lp, DFF), BF16, "mn", 0, ())] * 2, dact_epi,
                   extras=[(sv["g"], "mn", 0, ()), (sv["u"], "mn", 0, ())], tm=tm, tn=DFF, nsub=DFF // MXU_N,
                   vmem=VMEM_BIG)
    d_wd = _dw("dw_down", sv["act"], dhb, DFF, D, 0.5)
    d_wg = _dw("dw_gate", dg_, sv["z"], DFF, D)
    d_wu = _dw("dw_up", du_, sv["z"], DFF, D)

    nk = 1
    dh2, dhb2, dgain = _mm("ffn_dz", lp, D, [_term(dg_, wg_t, "nn", 0, (), pre), _term(du_, wu_t, "nn", 0, (), pre)],
                           _norm_bwd_outs(lp), _norm_bwd_epi,
                           extras=[(sv["h"], "mn", 0, ()), (dh, "mn", 0, ()), (gain, "n", 0, ())],
                           tm=tm, tn=D, nk=nk, vmem=VMEM_BIG)
    return dh2, dhb2, dgain, d_wg, d_wu, d_wd


def _gla_gates(hin, w2p, b2p):
    lp = hin.shape[0]
    tr = _row_tile(lp)
    bf, bb = P_OFF["lrf"] // HP, P_OFF["lrb"] // HP

    def body(lf_ref, lb_ref, w_ref, b_ref, o_ref, c_ref):
        i = pl.program_id(0)
        rows = i * tr + lax.broadcasted_iota(jnp.int32, (tr, 1), 0)
        r = lax.broadcasted_iota(jnp.int32, (tr, tr), 0)
        c = lax.broadcasted_iota(jnp.int32, (tr, tr), 1)
        same = (r // CHUNK) == (c // CHUNK)
        for d, l_ref in enumerate((lf_ref, lb_ref)):
            logit = _dot3(l_ref[...].astype(F32), w_ref[d]) + b_ref[d]
            g = jnp.where(rows >= NULL, jax.nn.log_sigmoid(logit) * (1.0 / GLA_TAU), 0.0)
            o_ref[d] = g
            tmat = jnp.where(same & ((r >= c) if d == 0 else (r <= c)), 1.0, 0.0).astype(BF16)
            c_ref[d] = _dot_sel(tmat, g)

    spec = pl.BlockSpec((2, tr, 512), lambda i: (0, i, 0))
    return pl.pallas_call(
        body, name="gla_gates", grid=(lp // tr,),
        in_specs=[pl.BlockSpec((tr, HP), lambda i: (i, bf)), pl.BlockSpec((tr, HP), lambda i: (i, bb)),
                  pl.BlockSpec((2, HP, 512), lambda i: (0, 0, 0)), pl.BlockSpec((2, 1, 512), lambda i: (0, 0, 0))],
        out_specs=[spec, spec],
        out_shape=[jax.ShapeDtypeStruct((2, lp, 512), F32)] * 2,
    )(hin, hin, w2p, b2p)


def _gla_rows(lp):
    return _pick(lp, (384, 256, 128))


def _tri(d):
    r = lax.broadcasted_iota(jnp.int32, (CHUNK, CHUNK), 0)
    c = lax.broadcasted_iota(jnp.int32, (CHUNK, CHUNK), 1)
    return (r >= c) if d == 0 else (r <= c)


def _gla_fwd(hin, gates):
    lp = hin.shape[0]
    rb = _gla_rows(lp)
    nb = lp // rb
    cpb = rb // CHUNK
    nchunk = lp // CHUNK
    qo, ko, vo = P_OFF["qa"] // 512, P_OFF["ka"] // 512, P_OFF["va"] // 512
    scale = GLA_DK ** -0.5

    def body(qf, kf, vf, gf, qb, kb, vb_, gb, of, ob, sf, sb, st):
        @pl.when(pl.program_id(0) == 0)
        def _():
            st[...] = jnp.zeros_like(st)

        ins = ((qf, kf, vf, gf, of, sf), (qb, kb, vb_, gb, ob, sb))
        for ci in range(cpb):
            for d in range(2):
                q_ref, k_ref, v_ref, g_ref, o_ref, s_ref = ins[d]
                tri = _tri(d)
                c = ci if d == 0 else cpb - 1 - ci
                rows = slice(c * CHUNK, (c + 1) * CHUNK)
                for h in range(GLA_H):
                    sl = slice(h * HP, (h + 1) * HP)
                    q = q_ref[rows, sl].astype(F32) * scale
                    k = k_ref[rows, sl].astype(F32)
                    v = v_ref[rows, sl].astype(F32)
                    b = g_ref[rows, sl]
                    btot = b[CHUNK - 1:CHUNK] if d == 0 else b[0:1]
                    qd = (q * jnp.exp(b)).astype(BF16)
                    ki = (k * jnp.exp(-b)).astype(BF16)
                    ke = (k * jnp.exp(btot - b)).astype(BF16)
                    vb = v.astype(BF16)
                    att = jnp.where(tri, _dot(qd, ki, "nt"), 0.0)
                    s_prev = st[d, h]
                    o_ref[rows, sl] = _dot(att.astype(BF16), vb) + _dot(qd, s_prev.astype(BF16), "nt")
                    s_ref[h, c] = s_prev
                    st[d, h] = s_prev * jnp.exp(btot) + _dot(vb, ke, "tn")

    def specs(off):
        return (pl.BlockSpec((rb, 512), lambda b: (b, off)), pl.BlockSpec((rb, 512), lambda b: (nb - 1 - b, off)))

    (qf, qb), (kf, kb), (vf, vb2) = specs(qo), specs(ko), specs(vo)
    gf = pl.BlockSpec((None, rb, 512), lambda b: (0, b, 0))
    gb = pl.BlockSpec((None, rb, 512), lambda b: (1, nb - 1 - b, 0))
    of, ob = specs(0)
    sf = pl.BlockSpec((GLA_H, cpb, HP, HP), lambda b: (0, b, 0, 0))
    sb = pl.BlockSpec((GLA_H, cpb, HP, HP), lambda b: (0, nb - 1 - b, 0, 0))
    osh = jax.ShapeDtypeStruct((lp, GLA_H * HP), F32)
    ssh = jax.ShapeDtypeStruct((GLA_H, nchunk, HP, HP), F32)
    return pl.pallas_call(
        body, name="gla_fwd", grid=(nb,),
        in_specs=[qf, kf, vf, gf, qb, kb, vb2, gb], out_specs=[of, ob, sf, sb], out_shape=[osh, osh, ssh, ssh],
        scratch_shapes=[pltpu.VMEM((2, GLA_H, HP, HP), F32)], compiler_params=_cparams(VMEM_BIG),
    )(hin, hin, hin, gates, hin, hin, hin, gates)


def _gla_bwd(hin, gates, states, do):
    lp = hin.shape[0]
    rb = _gla_rows(lp)
    nb = lp // rb
    cpb = rb // CHUNK
    qo, ko, vo = P_OFF["qa"] // 512, P_OFF["ka"] // 512, P_OFF["va"] // 512
    scale = GLA_DK ** -0.5

    def body(qf, kf, vf, gf, sf, dof, qb, kb, vb_, gb, sb, dob,
             dqf, dkf, dvf, dgf, dqb, dkb, dvb, dgb, dst):
        @pl.when(pl.program_id(0) == 0)
        def _():
            dst[...] = jnp.zeros_like(dst)

        ins = ((qf, kf, vf, gf, sf, dof, dqf, dkf, dvf, dgf), (qb, kb, vb_, gb, sb, dob, dqb, dkb, dvb, dgb))
        for ci in range(cpb):
            for d in range(2):
                q_ref, k_ref, v_ref, g_ref, s_ref, do_ref, dq_ref, dk_ref, dv_ref, dg_ref = ins[d]
                tri, tri_t = _tri(d), _tri(1 - d)
                edge = lax.broadcasted_iota(jnp.int32, (CHUNK, 1), 0) == (CHUNK - 1 if d == 0 else 0)
                c = cpb - 1 - ci if d == 0 else ci
                rows = slice(c * CHUNK, (c + 1) * CHUNK)
                for h in range(GLA_H):
                    sl = slice(h * HP, (h + 1) * HP)
                    q = q_ref[rows, sl].astype(F32) * scale
                    k = k_ref[rows, sl].astype(F32)
                    v = v_ref[rows, sl].astype(F32)
                    dout = do_ref[rows, sl].astype(BF16)
                    b = g_ref[rows, sl]
                    btot = b[CHUNK - 1:CHUNK] if d == 0 else b[0:1]
                    e = jnp.exp(b)
                    ei = jnp.exp(-b)
                    et = jnp.exp(btot - b)
                    etot = jnp.exp(btot)
                    qd = q * e
                    ki = k * ei
                    ke = k * et
                    qdb, kib, keb, vb = qd.astype(BF16), ki.astype(BF16), ke.astype(BF16), v.astype(BF16)
                    att_t = jnp.where(tri_t, _dot(kib, qdb, "nt"), 0.0).astype(BF16)
                    d_att = jnp.where(tri, _dot(dout, vb, "nt"), 0.0).astype(BF16)
                    d_att_t = jnp.where(tri_t, _dot(vb, dout, "nt"), 0.0).astype(BF16)
                    s_prev = s_ref[h, c]
                    ds_t = dst[d, h]
                    ds_b = ds_t.astype(BF16)
                    dv = _dot(att_t, dout) + _dot(keb, ds_b, "nt")
                    d_qd = _dot(d_att, kib) + _dot(dout, s_prev.astype(BF16))
                    d_ki = _dot(d_att_t, qdb)
                    d_ke = _dot(vb, ds_b)
                    d_e = jnp.sum(s_prev * ds_t, axis=0, keepdims=True)
                    dst[d, h] = _dot(dout, qdb, "tn") + ds_t * etot
                    db = d_qd * qd - d_ki * ki - d_ke * ke
                    dbtot = jnp.sum(d_ke * ke, axis=0, keepdims=True) + d_e * etot
                    dq_ref[rows, sl] = (d_qd * e * scale).astype(BF16)
                    dk_ref[rows, sl] = (d_ki * ei + d_ke * et).astype(BF16)
                    dv_ref[rows, sl] = dv.astype(BF16)
                    dg_ref[rows, sl] = db + jnp.where(edge, dbtot, 0.0)

    def fw(off):
        return pl.BlockSpec((rb, 512), lambda b: (nb - 1 - b, off))

    def bw(off):
        return pl.BlockSpec((rb, 512), lambda b: (b, off))

    gf = pl.BlockSpec((None, rb, 512), lambda b: (0, nb - 1 - b, 0))
    gb = pl.BlockSpec((None, rb, 512), lambda b: (1, b, 0))
    sf = pl.BlockSpec((GLA_H, cpb, HP, HP), lambda b: (0, nb - 1 - b, 0, 0))
    sb = pl.BlockSpec((GLA_H, cpb, HP, HP), lambda b: (0, b, 0, 0))
    osh = jax.ShapeDtypeStruct((lp, GLA_H * HP), F32)
    osh_b = jax.ShapeDtypeStruct((lp, GLA_H * HP), BF16)
    res = pl.pallas_call(
        body, name="gla_bwd", grid=(nb,),
        in_specs=[fw(qo), fw(ko), fw(vo), gf, sf, fw(0), bw(qo), bw(ko), bw(vo), gb, sb, bw(0)],
        out_specs=[fw(0)] * 4 + [bw(0)] * 4, out_shape=[osh_b, osh_b, osh_b, osh] * 2,
        scratch_shapes=[pltpu.VMEM((2, GLA_H, HP, HP), F32)], compiler_params=_cparams(VMEM_BIG),
    )(hin, hin, hin, gates, states[0], do, hin, hin, hin, gates, states[1], do)
    return res[:4], res[4:]


def _gla_out_fwd(o2, hin, gn):
    lp = hin.shape[0]
    tr = _row_tile(lp)
    ro = P_OFF["ra"] // 512

    def body(of_ref, ob_ref, r_ref, gn_ref, a_ref):
        r = r_ref[...].astype(F32)
        sr, _ = _silu_parts(r)
        for h in range(GLA_H):
            sl = slice(h * HP, (h + 1) * HP)
            o = of_ref[:, sl] + ob_ref[:, sl]
            rs = lax.rsqrt(jnp.mean(o * o, axis=-1, keepdims=True) + EPS)
            a_ref[:, sl] = (o * rs * gn_ref[:, sl] * sr[:, sl]).astype(BF16)

    row = pl.BlockSpec((tr, 512), lambda i: (i, 0))
    return pl.pallas_call(
        body, name="gla_out_fwd", grid=(lp // tr,),
        in_specs=[row, row, pl.BlockSpec((tr, 512), lambda i: (i, ro)), pl.BlockSpec((1, 512), lambda i: (0, 0))],
        out_specs=row,
        out_shape=jax.ShapeDtypeStruct((lp, 512), BF16),
    )(o2[0], o2[1], hin, gn)


def _gla_out_bwd(da, o2, hin, gn):
    lp = hin.shape[0]
    tr = _row_tile(lp)
    ro = P_OFF["ra"] // 512

    def body(da_ref, of_ref, ob_ref, r_ref, gn_ref, do_ref, dr_ref, dgn_ref):
        i = pl.program_id(0)
        r = r_ref[...].astype(F32)
        sr, dsr = _silu_parts(r)
        da_v = da_ref[...]
        parts = []
        for h in range(GLA_H):
            sl = slice(h * HP, (h + 1) * HP)
            o = of_ref[:, sl] + ob_ref[:, sl]
            rs = lax.rsqrt(jnp.mean(o * o, axis=-1, keepdims=True) + EPS)
            oh = o * rs
            gn_h = gn_ref[:, sl]
            dah = da_v[:, sl]
            dr_ref[:, sl] = (dah * oh * gn_h * dsr[:, sl]).astype(BF16)
            t = dah * sr[:, sl]
            parts.append(jnp.sum(t * oh, axis=0, keepdims=True))
            doh = t * gn_h
            do_ref[:, sl] = rs * (doh - oh * jnp.mean(doh * oh, axis=-1, keepdims=True))
        part = jnp.concatenate(parts, axis=1)

        @pl.when(i == 0)
        def _():
            dgn_ref[...] = part

        @pl.when(i != 0)
        def _():
            dgn_ref[...] += part

    row = pl.BlockSpec((tr, 512), lambda i: (i, 0))
    return pl.pallas_call(
        body, name="gla_out_bwd", grid=(lp // tr,),
        in_specs=[row, row, row, pl.BlockSpec((tr, 512), lambda i: (i, ro)), pl.BlockSpec((1, 512), lambda i: (0, 0))],
        out_specs=[row, row, pl.BlockSpec((1, 512), lambda i: (0, 0))],
        out_shape=[jax.ShapeDtypeStruct((lp, 512), F32), jax.ShapeDtypeStruct((lp, 512), BF16),
                   jax.ShapeDtypeStruct((1, 512), F32)],
    )(da, o2[0], o2[1], hin, gn)


def _gla_in_bwd(gf, gb, gates, hin, w2p, others):
    lp = hin.shape[0]
    tr = _row_tile(lp)
    bf, bb = P_OFF["lrf"] // HP, P_OFF["lrb"] // HP
    names = tuple(others)

    def seg(name):
        return slice(P_OFF[name], P_OFF[name] + P_WIDTH[name])

    def body(dqf_ref, dkf_ref, dvf_ref, dgf_ref, dqb_ref, dkb_ref, dvb_ref, dgb_ref, g_ref, lf_ref, lb_ref, w_ref,
             *rest):
        other_refs, (o_ref, dw_ref, db_ref) = rest[:len(names)], rest[len(names):]
        i = pl.program_id(0)
        for n, ref in zip(names, other_refs):
            o_ref[:, seg(n)] = ref[...].astype(BF16)
        o_ref[:, seg("qa")] = (dqf_ref[...].astype(F32) + dqb_ref[...].astype(F32)).astype(BF16)
        o_ref[:, seg("ka")] = (dkf_ref[...].astype(F32) + dkb_ref[...].astype(F32)).astype(BF16)
        o_ref[:, seg("va")] = (dvf_ref[...].astype(F32) + dvb_ref[...].astype(F32)).astype(BF16)
        olr_ref = o_ref.at[:, P_OFF["lrf"]:P_OFF["lrf"] + 2 * HP]
        rows = i * tr + lax.broadcasted_iota(jnp.int32, (tr, 1), 0)
        r = lax.broadcasted_iota(jnp.int32, (tr, tr), 0)
        c = lax.broadcasted_iota(jnp.int32, (tr, tr), 1)
        same = (r // CHUNK) == (c // CHUNK)
        for d, (l_ref, dg_ref) in enumerate(((lf_ref, dgf_ref), (lb_ref, dgb_ref))):
            tmat = jnp.where(same & ((r <= c) if d == 0 else (r >= c)), 1.0, 0.0).astype(BF16)
            dg = _dot_sel(tmat, dg_ref[...])
            sig_neg = 1.0 - jnp.exp(GLA_TAU * g_ref[d])
            dlogit = jnp.where(rows >= NULL, dg * (1.0 / GLA_TAU) * sig_neg, 0.0)
            olr_ref[:, d * HP:(d + 1) * HP] = _dot3(dlogit, w_ref[d], "nt").astype(BF16)
            dw = _dot3(l_ref[...].astype(F32), dlogit, "tn")
            dbias = jnp.sum(dlogit, axis=0, keepdims=True)

            @pl.when(i == 0)
            def _():
                dw_ref[d] = dw
                db_ref[d] = dbias

            @pl.when(i != 0)
            def _():
                dw_ref[d] += dw
                db_ref[d] += dbias

    two = pl.BlockSpec((2, tr, 512), lambda i: (0, i, 0))
    row = pl.BlockSpec((tr, 512), lambda i: (i, 0))
    return pl.pallas_call(
        body, name="gla_in_bwd", grid=(lp // tr,),
        in_specs=[row] * 8 + [two, pl.BlockSpec((tr, HP), lambda i: (i, bf)),
                  pl.BlockSpec((tr, HP), lambda i: (i, bb)), pl.BlockSpec((2, HP, 512), lambda i: (0, 0, 0))] +
                 [pl.BlockSpec((tr, P_WIDTH[n]), lambda i: (i, 0)) for n in names],
        out_specs=[pl.BlockSpec((tr, D_INP), lambda i: (i, 0)),
                   pl.BlockSpec((2, HP, 512), lambda i: (0, 0, 0)), pl.BlockSpec((2, 1, 512), lambda i: (0, 0, 0))],
        out_shape=[jax.ShapeDtypeStruct((lp, D_INP), BF16), jax.ShapeDtypeStruct((2, HP, 512), F32),
                   jax.ShapeDtypeStruct((2, 1, 512), F32)],
        compiler_params=_cparams(VMEM_BIG),
    )(*gf, *gb, gates, hin, hin, w2p, *[others[n] for n in names])


def _rope_tables(lp):
    n_tok = lp - NULL - NMETA
    rows = n_tok // GRID_W
    row = np.repeat(np.arange(rows), GRID_W).astype(np.float32)
    col = np.tile(np.arange(GRID_W), rows).astype(np.float32)
    inv = (ROPE_THETA ** (-np.arange(0, 32, 2, dtype=np.float32) / 32)).astype(np.float32)
    ang = np.concatenate([row[:, None] * inv, col[:, None] * inv], axis=-1)
    ang = np.concatenate([np.zeros((NULL + NMETA, 32), np.float32), ang], axis=0)
    cos, sin = np.cos(ang).astype(np.float32), np.sin(ang).astype(np.float32)
    z16 = np.zeros((lp, 16), np.float32)
    z64 = np.zeros((lp, 64), np.float32)
    c = np.concatenate([cos[:, :16], cos[:, :16], cos[:, 16:], cos[:, 16:], z64], axis=1)
    a = np.concatenate([-sin[:, :16], z16, -sin[:, 16:], z16, z64], axis=1)
    b = np.concatenate([z16, sin[:, :16], z16, sin[:, 16:], z64], axis=1)
    return jnp.asarray(c), jnp.asarray(a), jnp.asarray(b)


def _rope(x, c, a, b):
    return x * c + pltpu.roll(x, HP - 16, 1) * a + pltpu.roll(x, 16, 1) * b


def _rope_t(dx, c, a, b):
    return dx * c + pltpu.roll(dx * a, 16, 1) + pltpu.roll(dx * b, HP - 16, 1)


def _attn_prep(hin, gq, gk, tabs):
    lp = hin.shape[0]
    tr = _row_tile(lp)
    qo, ko, vo = P_OFF["qb"] // 1024, P_OFF["kb"] // 256, P_OFF["vb"] // 256

    def body(q_ref, k_ref, v_ref, gq_ref, gk_ref, c_ref, a_ref, b_ref, oq_ref, ok_ref, ov_ref):
        c, a, b = c_ref[...], a_ref[...], b_ref[...]
        for src, g_ref, dst, nh, sc in ((q_ref, gq_ref, oq_ref, ATT_H, Q_SCALE), (k_ref, gk_ref, ok_ref, ATT_KV, 1.0)):
            for h in range(nh):
                sl = slice(h * HP, (h + 1) * HP)
                x = src[:, sl].astype(F32)
                r = lax.rsqrt(jnp.sum(x * x, axis=-1, keepdims=True) * (1.0 / HEAD_DIM) + EPS)
                dst[:, sl] = (_rope(x * r * g_ref[...], c, a, b) * sc).astype(BF16)
        lane = lax.broadcasted_iota(jnp.int32, (1, ATT_KV * HP), 1)
        ov_ref[...] = jnp.where(lane % HP == HEAD_DIM, 1.0, v_ref[...]).astype(BF16)

    tab = pl.BlockSpec((tr, HP), lambda i: (i, 0))
    vec = pl.BlockSpec((1, HP), lambda i: (0, 0))
    return pl.pallas_call(
        body, name="attn_prep", grid=(lp // tr,),
        in_specs=[pl.BlockSpec((tr, 1024), lambda i: (i, qo)), pl.BlockSpec((tr, 256), lambda i: (i, ko)),
                  pl.BlockSpec((tr, 256), lambda i: (i, vo)), vec, vec, tab, tab, tab],
        out_specs=[pl.BlockSpec((tr, 1024), lambda i: (i, 0)), pl.BlockSpec((tr, 256), lambda i: (i, 0)),
                   pl.BlockSpec((tr, 256), lambda i: (i, 0))],
        out_shape=[jax.ShapeDtypeStruct((lp, 1024), BF16), jax.ShapeDtypeStruct((lp, 256), BF16),
                   jax.ShapeDtypeStruct((lp, 256), BF16)],
    )(hin, hin, hin, gq, gk, *tabs)


def _attn_prep_bwd(dqr, dkr, hin, gq, gk, tabs):
    lp = hin.shape[0]
    tr = _row_tile(lp)
    qo, ko = P_OFF["qb"] // 1024, P_OFF["kb"] // 256

    def body(dq_ref, dk_ref, q_ref, k_ref, gq_ref, gk_ref, c_ref, a_ref, b_ref, oq_ref, ok_ref, dgq_ref, dgk_ref):
        i = pl.program_id(0)
        c, a, b = c_ref[...], a_ref[...], b_ref[...]
        for src, dsrc, g_ref, dst, dg_ref, nh, sc in (
                (q_ref, dq_ref, gq_ref, oq_ref, dgq_ref, ATT_H, Q_SCALE),
                (k_ref, dk_ref, gk_ref, ok_ref, dgk_ref, ATT_KV, 1.0)):
            acc = jnp.zeros((1, HP), F32)
            for h in range(nh):
                sl = slice(h * HP, (h + 1) * HP)
                x = src[:, sl].astype(F32)
                r = lax.rsqrt(jnp.sum(x * x, axis=-1, keepdims=True) * (1.0 / HEAD_DIM) + EPS)
                xh = x * r
                dxn = _rope_t(dsrc[:, sl] * sc, c, a, b)
                acc = acc + jnp.sum(dxn * xh, axis=0, keepdims=True)
                dxh = dxn * g_ref[...]
                dx = r * (dxh - xh * (jnp.sum(dxh * xh, axis=-1, keepdims=True) * (1.0 / HEAD_DIM)))
                dst[:, sl] = dx.astype(BF16)

            @pl.when(i == 0)
            def _():
                dg_ref[...] = acc

            @pl.when(i != 0)
            def _():
                dg_ref[...] += acc

    tab = pl.BlockSpec((tr, HP), lambda i: (i, 0))
    vec = pl.BlockSpec((1, HP), lambda i: (0, 0))
    return pl.pallas_call(
        body, name="attn_prep_bwd", grid=(lp // tr,),
        in_specs=[pl.BlockSpec((tr, 1024), lambda i: (i, 0)), pl.BlockSpec((tr, 256), lambda i: (i, 0)),
                  pl.BlockSpec((tr, 1024), lambda i: (i, qo)), pl.BlockSpec((tr, 256), lambda i: (i, ko)),
                  vec, vec, tab, tab, tab],
        out_specs=[pl.BlockSpec((tr, 1024), lambda i: (i, 0)), pl.BlockSpec((tr, 256), lambda i: (i, 0)), vec, vec],
        out_shape=[jax.ShapeDtypeStruct((lp, 1024), BF16), jax.ShapeDtypeStruct((lp, 256), BF16),
                   jax.ShapeDtypeStruct((1, HP), F32), jax.ShapeDtypeStruct((1, HP), F32)],
    )(dqr, dkr, hin, hin, gq, gk, *tabs)


QB = 128
GH = 2
Q_SCALE = HEAD_DIM ** -0.5 * math.log2(math.e)
LN2 = math.log(2.0)


def _stack(ref, g0, n):
    return jnp.concatenate([ref[:, (g0 + g) * HP:(g0 + g + 1) * HP] for g in range(n)], axis=0)


def _attn_fwd(qr, kr, vb):
    lp = qr.shape[0]
    nq = lp // QB

    def body(q_ref, k_ref, v_ref, o_ref, lse_ref):
        qb = pl.program_id(1)
        keys = lax.broadcasted_iota(jnp.int32, (1, lp), 1)
        lane = lax.broadcasted_iota(jnp.int32, (1, HP), 1)
        rows = qb * QB + lax.broadcasted_iota(jnp.int32, (QB, 1), 0)
        for ch in range(ATT_G // GH):
            qs = _stack(q_ref, ch * GH, GH)
            s = _dot(qs, k_ref[...], "nt")
            s = jnp.where(keys >= NULL, s, -1e30)
            m = jnp.max(s, axis=-1, keepdims=True)
            p = jnp.exp2(s - m).astype(BF16)
            o_raw = _dot(p, v_ref[...])
            l = jnp.sum(jnp.where(lane == HEAD_DIM, o_raw, 0.0), axis=-1, keepdims=True)
            o = jnp.where(lane < HEAD_DIM, o_raw / l, 0.0)
            lse = m + jnp.log2(l)
            for g in range(GH):
                sl = slice((ch * GH + g) * HP, (ch * GH + g + 1) * HP)
                o_ref[:, sl] = jnp.where(rows >= NULL, o[g * QB:(g + 1) * QB], 0.0).astype(BF16)
                lse_ref[:, sl] = jnp.broadcast_to(lse[g * QB:(g + 1) * QB], (QB, HP))

    qspec = pl.BlockSpec((QB, ATT_G * HP), lambda kv, qb: (qb, kv))
    kspec = pl.BlockSpec((lp, HP), lambda kv, qb: (0, kv))
    return pl.pallas_call(
        body, name="attn_fwd", grid=(ATT_KV, nq),
        in_specs=[qspec, kspec, kspec], out_specs=[qspec, qspec],
        out_shape=[jax.ShapeDtypeStruct((lp, ATT_H * HP), BF16), jax.ShapeDtypeStruct((lp, ATT_H * HP), F32)],
        compiler_params=_cparams(VMEM_BIG),
    )(qr, kr, vb)


def _attn_bwd(qr, kr, vb, o, lse, do):
    lp = qr.shape[0]
    nq = lp // QB

    def body(q_ref, k_ref, v_ref, o_ref, lse_ref, do_ref, dq_ref, dk_ref, dv_ref):
        qb = pl.program_id(1)

        @pl.when(qb == 0)
        def _():
            dk_ref[...] = jnp.zeros_like(dk_ref)
            dv_ref[...] = jnp.zeros_like(dv_ref)

        keys = lax.broadcasted_iota(jnp.int32, (1, lp), 1)
        k = k_ref[...]
        dk_acc, dv_acc = None, None
        for ch in range(ATT_G // GH):
            g0 = ch * GH
            qs = _stack(q_ref, g0, GH)
            dos = _stack(do_ref, g0, GH)
            os_ = _stack(o_ref, g0, GH).astype(F32)
            lse_s = jnp.concatenate([lse_ref[:, (g0 + g) * HP:(g0 + g) * HP + 1] for g in range(GH)], axis=0)
            delta = jnp.sum(dos * os_, axis=-1, keepdims=True) * LN2
            s = _dot(qs, k, "nt")
            p = jnp.where(keys >= NULL, jnp.exp2(s - lse_s), 0.0)
            dob = dos.astype(BF16)
            dp = _dot((dos * LN2).astype(BF16), v_ref[...], "nt")
            ds = (p * (dp - delta)).astype(BF16)
            dq = _dot(ds, k)
            for g in range(GH):
                dq_ref[:, (g0 + g) * HP:(g0 + g + 1) * HP] = dq[g * QB:(g + 1) * QB]
            dv_c = _dot(p.astype(BF16), dob, "tn")
            dk_c = _dot(ds, qs, "tn")
            dv_acc = dv_c if dv_acc is None else dv_acc + dv_c
            dk_acc = dk_c if dk_acc is None else dk_acc + dk_c
        dv_ref[...] += dv_acc
        dk_ref[...] += dk_acc

    qspec = pl.BlockSpec((QB, ATT_G * HP), lambda kv, qb: (qb, kv))
    kspec = pl.BlockSpec((lp, HP), lambda kv, qb: (0, kv))
    return pl.pallas_call(
        body, name="attn_bwd", grid=(ATT_KV, nq),
        in_specs=[qspec, kspec, kspec, qspec, qspec, qspec], out_specs=[qspec, kspec, kspec],
        out_shape=[jax.ShapeDtypeStruct((lp, ATT_H * HP), F32), jax.ShapeDtypeStruct((lp, ATT_KV * HP), F32),
                   jax.ShapeDtypeStruct((lp, ATT_KV * HP), F32)],
        compiler_params=_cparams(VMEM_BIG),
    )(qr, kr, vb, o, lse, do)


def _mixer_fwd(h, z, wl, l, tabs, next_gain):
    lp = h.shape[0]
    tm = _tm(lp)

    def id_epi(accs, exs, row0):
        return [accs[0]]

    (hin,) = _mm("in_proj", lp, D_INP, [_term(z, wl["win_t"], "nt", 0, (), (l,))], [((lp, D_INP), BF16, "mn", 0, ())],
                 id_epi, tm=_tm_wide(lp), tn=D_INP // 2, vmem=VMEM_BIG)
    gates, cum = _gla_gates(hin, wl["w2p"][l], wl["b2p"][l])
    o_f, o_b, s_f, s_b = _gla_fwd(hin, cum)
    o2, states = (o_f, o_b), (s_f, s_b)
    a = _gla_out_fwd(o2, hin, wl["gn"][l])
    qr, kr, vb = _attn_prep(hin, wl["gq"][l], wl["gk"][l], tabs)
    b, lse = _attn_fwd(qr, kr, vb)

    def merge_epi(accs, exs, row0):
        pa, pb = accs
        ga, gb, bma, bmb = exs
        y = _sigmoid(ga + bma) * pa + _sigmoid(gb + bmb) * pb
        return [y, pa, pb]

    y, pa, pb = _mm("merge", lp, D, [_term(a, wl["wpa_t"], "nt", 0, (), (l,)), _term(b, wl["wpb_t"], "nt", 1, (), (l,))],
                    [((lp, D), BF16, "mn", 0, ())] * 3, merge_epi,
                    extras=[(hin, "mn", P_OFF["ga"] // D, ()), (hin, "mn", P_OFF["gb"] // D, ()),
                            (wl["bm"], "n", 0, (l, 0)), (wl["bm"], "n", 0, (l, 1))],
                    tm=_tm_wide(lp), tn=D, nsub=D // MXU_N, i_outer=True, vmem=VMEM_BIG)

    h2, z2 = _mm("out_proj", lp, D, [_term(y, wl["wout"], "nn", 0, (), (l,))],
                 [((lp, D), F32, "mn", 0, ()), ((lp, D), BF16, "mn", 0, ())], _residual_norm_epi(1.0, True),
                 extras=[(h, "mn", 0, ()), (next_gain, "n", 0, ())], tm=_tm_wide(lp), tn=D, i_outer=True, vmem=VMEM_BIG)
    sv = dict(h=h, z=z, hin=hin, gates=gates, cum=cum, o2=o2, states=states, a=a, qr=qr, kr=kr, vb=vb, b=b, lse=lse,
              y=y, pa=pa, pb=pb)
    return h2, z2, sv


def _mixer_bwd(dh, dhb, sv, gain, wl, l, tabs):
    lp = dh.shape[0]
    tm = _tm(lp)
    hin = sv["hin"]

    def merge_bwd_epi(accs, exs, row0):
        dy = accs[0]
        ga, gb, pa, pb, bma, bmb = exs
        sa = _sigmoid(ga + bma)
        sb = _sigmoid(gb + bmb)
        dga = dy * pa.astype(F32) * sa * (1.0 - sa)
        dgb = dy * pb.astype(F32) * sb * (1.0 - sb)
        return [dy * sa, dy * sb, dga, dgb, jnp.sum(dga, axis=0, keepdims=True), jnp.sum(dgb, axis=0, keepdims=True)]

    big = ((lp, D), BF16, "mn", 0, ())
    vec = ((1, D), F32, "nsum", 0, ())
    dpa, dpb, dga, dgb, dbma, dbmb = _mm(
        "merge_bwd", lp, D, [_term(dhb, wl["wout"], "nt", 0, (), (l,))], [big, big, big, big, vec, vec], merge_bwd_epi,
        extras=[(hin, "mn", P_OFF["ga"] // D, ()), (hin, "mn", P_OFF["gb"] // D, ()), (sv["pa"], "mn", 0, ()),
                (sv["pb"], "mn", 0, ()), (wl["bm"], "n", 0, (l, 0)), (wl["bm"], "n", 0, (l, 1))],
        tm=_tm_wide(lp), tn=D, nsub=D // MXU_N, vmem=VMEM_BIG)
    d_wout = _dw("dw_out", sv["y"], dhb, D, D)
    d_wpa_t = _dw("dw_pa", dpa, sv["a"], D, 512)
    d_wpb_t = _dw("dw_pb", dpb, sv["b"], D, ATT_H * HP)

    def id_epi(accs, exs, row0):
        return [accs[0]]

    (da,) = _mm("d_a", lp, 512, [_term(dpa, wl["wpa_t"], "nn", 0, (), (l,))], [((lp, 512), F32, "mn", 0, ())], id_epi,
                tm=_tm_wide(lp), tn=512, i_outer=True, vmem=VMEM_BIG)
    (db,) = _mm("d_b", lp, ATT_H * HP, [_term(dpb, wl["wpb_t"], "nn", 0, (), (l,))],
                [((lp, ATT_H * HP), F32, "mn", 0, ())], id_epi, tm=_tm_wide(lp), tn=512, i_outer=True, vmem=VMEM_BIG)
    d_o, d_ra, d_gn = _gla_out_bwd(da, sv["o2"], hin, wl["gn"][l])
    g_fw, g_bw = _gla_bwd(hin, sv["cum"], sv["states"], d_o)
    dqr, dkr, dvb = _attn_bwd(sv["qr"], sv["kr"], sv["vb"], sv["b"], sv["lse"], db)
    d_qb, d_kb, d_gq, d_gk = _attn_prep_bwd(dqr, dkr, hin, wl["gq"][l], wl["gk"][l], tabs)
    dhin, d_w2p, d_b2p = _gla_in_bwd(g_fw, g_bw, sv["gates"], hin, wl["w2p"][l],
                                     dict(qb=d_qb, ga=dga, gb=dgb, ra=d_ra, kb=d_kb, vb=dvb))
    d_win_t = _dw("dw_in", dhin, sv["z"], D_INP, D)
    dh2, dhb2, dgain = _mm("in_proj_dz", lp, D, [_term(dhin, wl["win_t"], "nn", 0, (), (l,))], _norm_bwd_outs(lp),
                           _norm_bwd_epi, extras=[(sv["h"], "mn", 0, ()), (dh, "mn", 0, ()), (gain, "n", 0, ())],
                           tm=tm, tn=D, nk=1, vmem=VMEM_BIG)
    grads = dict(gain=dgain, wout=d_wout, wpa_t=d_wpa_t, wpb_t=d_wpb_t, win_t=d_win_t, gn=d_gn, w2p=d_w2p, b2p=d_b2p,
                 gq=d_gq, gk=d_gk, bma=dbma, bmb=dbmb)
    return dh2, dhb2, grads


def _mesh_pos():
    x, y, c = lax.axis_index("x"), lax.axis_index("y"), lax.axis_index("c")
    chips = [(1 - x, y), (x, 1 - y), (1 - x, 1 - y)]
    return x, y, c, chips


def _dev_index(x, y, c):
    return 4 * x + 2 * y + c


def _all_gather(name, shards, leads):
    nt = len(shards)

    def blk(ref, lead, idx):
        return ref.at[(slice(None),) * lead + (idx,)]

    def body(*refs):
        xs, outs = refs[:nt], refs[nt:2 * nt]
        send_sems, recv_sems, local_sems = refs[2 * nt:]
        x, y, c, chips = _mesh_pos()
        me, sibling = (x, y, c), (x, y, 1 - c)

        def copy(t, k, block, to, own=False):
            dst = blk(outs[t], leads[t], _dev_index(*block))
            return pltpu.make_async_remote_copy(
                src_ref=xs[t] if own else dst, dst_ref=dst, send_sem=send_sems.at[t, k], recv_sem=recv_sems.at[t, k],
                device_id=to, device_id_type=MESH)

        locals_ = [pltpu.make_async_copy(xs[t], blk(outs[t], leads[t], _dev_index(*me)), local_sems.at[t])
                   for t in range(nt)]
        for cp in locals_:
            cp.start()
        first = []
        for t in range(nt):
            first.append(copy(t, 0, me, sibling, own=True))
            first += [copy(t, 1 + j, me, (*chip, c), own=True) for j, chip in enumerate(chips)]
        for cp in first:
            cp.start()
        passed = []
        for j, chip in enumerate(chips):
            for t in range(nt):
                copy(t, 1 + j, (*chip, c), me).wait_recv()
                fw = copy(t, 4 + j, (*chip, c), sibling)
                fw.start()
                passed.append(fw)
        for t in range(nt):
            copy(t, 0, sibling, me).wait_recv()
        for j, chip in enumerate(chips):
            for t in range(nt):
                copy(t, 4 + j, (*chip, 1 - c), me).wait_recv()
        for cp in first + passed:
            cp.wait_send()
        for cp in locals_:
            cp.wait()

    out_shape = [jax.ShapeDtypeStruct(s.shape[:ld] + (NDEV,) + s.shape[ld:], s.dtype) for s, ld in zip(shards, leads)]
    return pl.pallas_call(
        body, name=name, in_specs=[ANY] * nt, out_specs=[ANY] * nt, out_shape=out_shape,
        scratch_shapes=[pltpu.SemaphoreType.DMA((nt, 7)), pltpu.SemaphoreType.DMA((nt, 7)),
                        pltpu.SemaphoreType.DMA((nt,))],
    )(*shards)


def _exchange_sibling(name, gs):
    nt = len(gs)

    def body(*refs):
        xs, outs = refs[:nt], refs[nt:2 * nt]
        send_sems, recv_sems = refs[2 * nt:]
        x, y, c, _ = _mesh_pos()
        sibling = (x, y, 1 - c)
        copies = []
        for t in range(nt):
            for ch in range(4):
                copies.append(pltpu.make_async_remote_copy(
                    src_ref=xs[t].at[2 * ch + (1 - c)], dst_ref=outs[t].at[ch],
                    send_sem=send_sems.at[t, ch], recv_sem=recv_sems.at[t, ch],
                    device_id=sibling, device_id_type=MESH))
        for cp in copies:
            cp.start()
        for cp in copies:
            cp.wait()

    out_shape = [jax.ShapeDtypeStruct((4,) + g.shape[1:], g.dtype) for g in gs]
    return pl.pallas_call(
        body, name=name, in_specs=[ANY] * nt, out_specs=[ANY] * nt, out_shape=out_shape,
        scratch_shapes=[pltpu.SemaphoreType.DMA((nt, 4)), pltpu.SemaphoreType.DMA((nt, 4))],
    )(*gs)


def _pair_sum(name, gs, recv):
    c = lax.axis_index("c")
    outs = []
    for t, (g, rv) in enumerate(zip(gs, recv)):
        _, r, cols = rv.shape

        def body(c_ref, g_ref, r_ref, o_ref):
            o_ref[...] = (g_ref[...].astype(F32) + r_ref[...].astype(F32)).astype(o_ref.dtype)

        outs.append(pl.pallas_call(
            body, name=f"{name}_{t}",
            grid_spec=pltpu.PrefetchScalarGridSpec(
                num_scalar_prefetch=1, grid=(4,),
                in_specs=[pl.BlockSpec((None, r, cols), lambda ch, cr: (2 * ch + cr[0], 0, 0)),
                          pl.BlockSpec((None, r, cols), lambda ch, cr: (ch, 0, 0))],
                out_specs=pl.BlockSpec((None, r, cols), lambda ch, cr: (ch, 0, 0))),
            out_shape=jax.ShapeDtypeStruct(rv.shape, rv.dtype),
        )(jnp.reshape(c, (1,)).astype(jnp.int32), g, rv))
    return outs


def _final_sum(name, ps, recv, transposed):
    chip = 2 * lax.axis_index("x") + lax.axis_index("y")
    outs = []
    for t, (p, rv) in enumerate(zip(ps, recv)):
        _, r, cols = rv.shape
        tr_out = transposed[t]
        oshape = (cols, r) if tr_out else (r, cols)

        def body(c_ref, p_ref, r0_ref, r1_ref, r2_ref, o_ref):
            acc = ((p_ref[...].astype(F32) + r0_ref[...].astype(F32)) + r1_ref[...].astype(F32)) + r2_ref[...].astype(F32)
            o_ref[...] = acc.T if tr_out else acc

        outs.append(pl.pallas_call(
            body, name=f"{name}_{t}",
            grid_spec=pltpu.PrefetchScalarGridSpec(
                num_scalar_prefetch=1, grid=(1,),
                in_specs=[pl.BlockSpec((None, r, cols), lambda i, cr: (cr[0], 0, 0))] +
                         [pl.BlockSpec((None, r, cols), lambda i, cr, j=j: (j, 0, 0)) for j in range(3)],
                out_specs=pl.BlockSpec(oshape, lambda i, cr: (0, 0))),
            out_shape=jax.ShapeDtypeStruct(oshape, F32),
        )(jnp.reshape(chip, (1,)).astype(jnp.int32), p, rv, rv, rv))
    return outs


def _sum_gathered(g):
    _, r, cols = g.shape

    def body(g_ref, o_ref):
        acc = g_ref[0]
        for d in range(1, NDEV):
            acc = acc + g_ref[d]
        o_ref[...] = acc

    return pl.pallas_call(body, name="small_sum", out_shape=jax.ShapeDtypeStruct((r, cols), F32))(g)


HBM = pl.BlockSpec(memory_space=pltpu.HBM)
SEM = pl.BlockSpec(memory_space=pltpu.SEMAPHORE)
EFFECT = pltpu.SideEffectType.DATAFLOW_SIDE_EFFECTING
NREL = NDEV - 1


def _related(k):
    x, y, c = lax.axis_index("x"), lax.axis_index("y"), lax.axis_index("c")
    px = 1 - x if k & 4 else x
    py = 1 - y if k & 2 else y
    pc = 1 - c if k & 1 else c
    return (px, py, pc), _dev_index(px, py, pc)


def _in_hbm(a):
    return pltpu.with_memory_space_constraint(a, pltpu.HBM)


ALL_RELS = tuple(range(1, NDEV))
CHIP_RELS = (4, 2, 6)


def _split_copies(xs, lands, send_sems, recv_sems, src_of, dst_of, rels):
    copies = []
    for t in range(len(xs)):
        for q, k in enumerate(rels):
            peer, peer_idx = _related(k)
            copies.append(pltpu.make_async_remote_copy(
                src_ref=src_of(xs[t], t, peer_idx), dst_ref=dst_of(lands[t], t, q, peer_idx),
                send_sem=send_sems.at[t * len(rels) + q], recv_sem=recv_sems.at[t * len(rels) + q],
                device_id=peer, device_id_type=MESH))
    return copies


def _exchange_start(name, xs, lands, src_of, dst_of, after, rels=ALL_RELS):
    nt = len(xs)

    def body(*refs):
        x_refs, land_refs = refs[:nt], refs[nt:2 * nt]
        send_sems, recv_sems = refs[2 * nt + 1], refs[2 * nt + 2]
        token = refs[-1]
        for cp in _split_copies(x_refs, land_refs, send_sems, recv_sems, src_of, dst_of, rels):
            cp.start()
        token[...] = jnp.zeros_like(token)

    res = pl.pallas_call(
        body, name=name,
        out_shape=(pltpu.SemaphoreType.DMA((nt * len(rels),)), pltpu.SemaphoreType.DMA((nt * len(rels),)),
                   *[pltpu.HBM(a.shape, a.dtype) for a in xs], *[pltpu.HBM(a.shape, a.dtype) for a in lands],
                   jax.ShapeDtypeStruct((8, 128), F32)),
        in_specs=[HBM] * (2 * nt) + [ANY],
        out_specs=(SEM, SEM, *[HBM] * (2 * nt), pl.BlockSpec(memory_space=pltpu.VMEM)),
        input_output_aliases={i: 2 + i for i in range(2 * nt)},
        compiler_params=pltpu.CompilerParams(has_side_effects=EFFECT),
    )(*[_in_hbm(a) for a in xs], *[_in_hbm(a) for a in lands], after)
    return res[0], res[1], res[2:2 + nt], res[2 + nt:2 + 2 * nt], res[-1]


def _exchange_wait(name, send_sems, recv_sems, xs, lands, src_of, dst_of, after, rels=ALL_RELS):
    nt = len(xs)

    def body(*refs):
        x_refs, land_refs = refs[:nt], refs[nt:2 * nt]
        send_sems, recv_sems = refs[2 * nt], refs[2 * nt + 1]
        for cp in _split_copies(x_refs, land_refs, send_sems, recv_sems, src_of, dst_of, rels):
            cp.wait_send()
            cp.wait_recv()

    res = pl.pallas_call(
        body, name=name,
        out_shape=(*[pltpu.HBM(a.shape, a.dtype) for a in xs], *[pltpu.HBM(a.shape, a.dtype) for a in lands]),
        in_specs=[HBM] * (2 * nt) + [SEM, SEM, ANY], out_specs=tuple([HBM] * (2 * nt)),
        input_output_aliases={i: i for i in range(2 * nt)},
        compiler_params=pltpu.CompilerParams(has_side_effects=EFFECT),
    )(*xs, *lands, send_sems, recv_sems, after)
    return res[:nt], res[nt:]


def _gather_start(name, shards, leads, after, rels=ALL_RELS):
    def src_of(x_ref, t, peer_idx):
        return x_ref

    def dst_of(land_ref, t, k, peer_idx):
        me = _dev_index(lax.axis_index("x"), lax.axis_index("y"), lax.axis_index("c"))
        return land_ref.at[(slice(None),) * leads[t] + (me,)]

    lands = [lax.empty(s.shape[:ld] + (NDEV,) + s.shape[ld:], s.dtype) for s, ld in zip(shards, leads)]
    return _exchange_start(name, shards, lands, src_of, dst_of, after, rels)


def _gather_wait(name, started, leads, after, rels=ALL_RELS):
    send_sems, recv_sems, shards, lands, _ = started

    def src_of(x_ref, t, peer_idx):
        return x_ref

    def dst_of(land_ref, t, k, peer_idx):
        return land_ref.at[(slice(None),) * leads[t] + (peer_idx,)]

    shards, lands = _exchange_wait(name, send_sems, recv_sems, shards, lands, src_of, dst_of, after, rels)
    me = _dev_index(lax.axis_index("x"), lax.axis_index("y"), lax.axis_index("c"))
    return [lax.dynamic_update_index_in_dim(g, s, me, ld) for g, s, ld in zip(lands, shards, leads)]


SIBLING_AND_CHIPS = (1,) + CHIP_RELS


def _forward_to_sibling(name, gathered, leads):
    nt = len(gathered)

    def body(*refs):
        ins, outs = refs[:nt], refs[nt:2 * nt]
        send_sems, recv_sems = refs[2 * nt:]
        x, y, c, chips = _mesh_pos()
        copies, arrivals = [], []
        for t in range(nt):
            for j, chip in enumerate(chips):
                def block(core):
                    return outs[t].at[(slice(None),) * leads[t] + (_dev_index(*chip, core),)]
                copies.append(pltpu.make_async_remote_copy(
                    src_ref=block(c), dst_ref=block(c), send_sem=send_sems.at[t, j], recv_sem=recv_sems.at[t, j],
                    device_id=(x, y, 1 - c), device_id_type=MESH))
                arrivals.append(pltpu.make_async_remote_copy(
                    src_ref=block(1 - c), dst_ref=block(1 - c), send_sem=send_sems.at[t, j], recv_sem=recv_sems.at[t, j],
                    device_id=(x, y, 1 - c), device_id_type=MESH))
        for cp in copies:
            cp.start()
        for cp in arrivals:
            cp.wait_recv()
        for cp in copies:
            cp.wait_send()

    return pl.pallas_call(
        body, name=name, in_specs=[ANY] * nt, out_specs=[ANY] * nt,
        out_shape=[jax.ShapeDtypeStruct(g.shape, g.dtype) for g in gathered],
        input_output_aliases={t: t for t in range(nt)},
        scratch_shapes=[pltpu.SemaphoreType.DMA((nt, 3)), pltpu.SemaphoreType.DMA((nt, 3))],
    )(*gathered)


def _scatter_src(x_ref, t, peer_idx):
    return x_ref.at[peer_idx]


def _scatter_dst(land_ref, t, q, peer_idx):
    return land_ref.at[q]


def _chips_src(x_ref, t, peer_idx):
    return x_ref.at[peer_idx // 2]


def _chips_start(name, ps, after):
    lands = [lax.empty((len(CHIP_RELS),) + p.shape[1:], p.dtype) for p in ps]
    return _exchange_start(name, ps, lands, _chips_src, _scatter_dst, after, CHIP_RELS)


def _chips_wait(name, started, after):
    send_sems, recv_sems, ps, lands, _ = started
    return _exchange_wait(name, send_sems, recv_sems, ps, lands, _chips_src, _scatter_dst, after, CHIP_RELS)


def _scatter_start(name, gs, after):
    lands = [lax.empty((NREL,) + g.shape[1:], g.dtype) for g in gs]
    return _exchange_start(name, gs, lands, _scatter_src, _scatter_dst, after)


def _scatter_wait(name, started, after, transposed):
    send_sems, recv_sems, gs, lands, _ = started
    gs, lands = _exchange_wait(name, send_sems, recv_sems, gs, lands, _scatter_src, _scatter_dst, after)
    me = _dev_index(lax.axis_index("x"), lax.axis_index("y"), lax.axis_index("c"))
    outs = []
    for t, (g, rv) in enumerate(zip(gs, lands)):
        _, r, cols = rv.shape
        tr_out = transposed[t]
        oshape = (cols, r) if tr_out else (r, cols)

        def body(c_ref, own_ref, rv_ref, o_ref):
            acc = own_ref[...].astype(F32)
            for k in range(NREL):
                acc = acc + rv_ref[k].astype(F32)
            o_ref[...] = acc.T if tr_out else acc

        outs.append(pl.pallas_call(
            body, name=f"{name}_sum_{t}",
            grid_spec=pltpu.PrefetchScalarGridSpec(
                num_scalar_prefetch=1, grid=(1,),
                in_specs=[pl.BlockSpec((None, r, cols), lambda i, cr: (cr[0], 0, 0)),
                          pl.BlockSpec((NREL, r, cols), lambda i, cr: (0, 0, 0))],
                out_specs=pl.BlockSpec(oshape, lambda i, cr: (0, 0))),
            out_shape=jax.ShapeDtypeStruct(oshape, F32), compiler_params=_cparams(VMEM_BIG),
        )(jnp.reshape(me, (1,)).astype(jnp.int32), g, rv))
    return outs


def _adamw(w, g, m, v):
    shape = w.shape
    cols = shape[-1]
    rows = math.prod(shape[:-1]) if len(shape) > 1 else 1
    w2, g2, m2, v2 = (jnp.reshape(t, (rows, cols)) for t in (w, g, m, v))
    tr = _pick(rows, (1024, 512, 256, 128)) if rows * cols > 65536 else rows
    c1 = 1.0 / (1.0 - ADAM_B1 ** ADAM_STEP)
    c2 = 1.0 / (1.0 - ADAM_B2 ** ADAM_STEP)

    def body(w_ref, g_ref, m_ref, v_ref, d_ref, nm_ref, nv_ref):
        gv = g_ref[...]
        nm = ADAM_B1 * m_ref[...] + (1.0 - ADAM_B1) * gv
        nv = ADAM_B2 * v_ref[...] + (1.0 - ADAM_B2) * (gv * gv)
        d_ref[...] = -ADAM_LR * ((nm * c1) / (jnp.sqrt(nv * c2) + ADAM_EPS) + ADAM_WD * w_ref[...])
        nm_ref[...] = nm
        nv_ref[...] = nv

    spec = pl.BlockSpec((tr, cols), lambda i: (i, 0))
    osh = jax.ShapeDtypeStruct((rows, cols), F32)
    d, nm, nv = pl.pallas_call(
        body, name="adamw", grid=(rows // tr,), in_specs=[spec] * 4, out_specs=[spec] * 3, out_shape=[osh] * 3,
        compiler_params=_cparams(VMEM_BIG),
    )(w2, g2, m2, v2)
    return jnp.reshape(d, shape), jnp.reshape(nm, shape), jnp.reshape(nv, shape)


def _pad_heads(w, name):
    if name not in P_HEADS:
        return w
    nh, real = P_HEADS[name]
    w = jnp.reshape(w, w.shape[:-2] + (nh, real, w.shape[-1]))
    w = jnp.pad(w, [(0, 0)] * (w.ndim - 2) + [(0, HP - real), (0, 0)])
    return jnp.reshape(w, w.shape[:-3] + (nh * HP, w.shape[-1]))


def _unpad_heads(w, name):
    if name not in P_HEADS:
        return w
    nh, real = P_HEADS[name]
    w = jnp.reshape(w, w.shape[:-2] + (nh, HP, w.shape[-1]))[..., :real, :]
    return jnp.reshape(w, w.shape[:-3] + (nh * real, w.shape[-1]))


def _win_pad(win_t):
    segs, o = {}, 0
    for n, s in zip(IN_NAMES, IN_SIZES):
        segs[n] = win_t[..., o:o + s, :]
        o += s
    return jnp.concatenate([_pad_heads(segs[n], n) for n in P_ORDER], axis=-2)


def _win_unpad(win_p):
    segs = {n: _unpad_heads(win_p[..., P_OFF[n]:P_OFF[n] + P_WIDTH[n], :], n) for n in P_ORDER}
    return jnp.concatenate([segs[n] for n in IN_NAMES], axis=-2)


def _t(w):
    return jnp.swapaxes(w, -1, -2)


def _ffn_stacked(g_g, g_u, g_d):
    wg, wu, wd = (jnp.reshape(g, (2, DFF, D)) for g in (g_g, g_u, g_d))
    return [(wg, wu, wd, (j,)) for j in range(2)]


def _ffn_single(g_g, g_u, g_d):
    return tuple(jnp.reshape(g, (DFF, D)) for g in (g_g, g_u, g_d)) + ((),)


def _layer_weights(ffn, g_in, g_pa, g_pb, g_out, gains, w2, b2, bm, gn, gq, gk):
    w2p = jnp.pad(jnp.reshape(w2, (2, GLA_RANK, GLA_H, GLA_DK)), ((0, 0), (0, HP - GLA_RANK), (0, 0), (0, HP - GLA_DK)))
    b2p = jnp.pad(jnp.reshape(b2, (2, 1, GLA_H, GLA_DK)), ((0, 0), (0, 0), (0, 0), (0, HP - GLA_DK)))
    wpb_t = jnp.pad(jnp.reshape(g_pb, (D, ATT_H, HEAD_DIM)), ((0, 0), (0, 0), (0, HP - HEAD_DIM)))
    return dict(
        gains=jnp.reshape(gains, (1, 3, 1, D)), ffn=ffn,
        win_t=_win_pad(jnp.reshape(g_in, (1, D_IN, D))), wpa_t=jnp.reshape(g_pa, (1, D, 512)),
        wpb_t=jnp.reshape(wpb_t, (1, D, ATT_H * HP)), wout=jnp.reshape(g_out, (1, D, D)),
        w2p=jnp.reshape(w2p, (1, 2, HP, GLA_H * HP)), b2p=jnp.reshape(b2p, (1, 2, 1, GLA_H * HP)),
        bm=jnp.reshape(bm, (1, 2, 1, D)), gn=jnp.reshape(gn, (1, 1, GLA_H * HP)),
        gq=jnp.pad(jnp.reshape(gq, (1, 1, HEAD_DIM)), ((0, 0), (0, 0), (0, HP - HEAD_DIM))),
        gk=jnp.pad(jnp.reshape(gk, (1, 1, HEAD_DIM)), ((0, 0), (0, 0), (0, HP - HEAD_DIM))))


def _layer_fwd_lower(h, z, ffn0, gain1):
    return _ffn_fwd(h, z, *ffn0, gain1)


def _layer_fwd_upper(h, z, s0, w, tabs, next_gain):
    h, z, s1 = _mixer_fwd(h, z, w, 0, tabs, w["gains"][0, 2])
    h, z, s2 = _ffn_fwd(h, z, *w["ffn"][1], next_gain)
    return h, z, (s0, s1, s2)


def _layer_fwd(h, z, w, tabs, next_gain):
    h, z, s0 = _layer_fwd_lower(h, z, w["ffn"][0], w["gains"][0, 1])
    return _layer_fwd_upper(h, z, s0, w, tabs, next_gain)


def _layer_bwd_upper(dh, dhb, saved, w, tabs):
    _, s1, s2 = saved
    dh, dhb, dg2, dwg1, dwu1, dwd1 = _ffn_bwd(dh, dhb, s2, w["gains"][0, 2], *w["ffn"][1])
    dh, dhb, gm = _mixer_bwd(dh, dhb, s1, w["gains"][0, 1], w, 0, tabs)
    gm.update(gain2=dg2, wg1=dwg1, wu1=dwu1, wd1=dwd1)
    return dh, dhb, gm


def _layer_bwd_lower(dh, dhb, saved, w, gm):
    dh, dhb, dg0, dwg0, dwu0, dwd0 = _ffn_bwd(dh, dhb, saved[0], w["gains"][0, 0], *w["ffn"][0])
    gm.update(gain0=dg0, wg0=dwg0, wu0=dwu0, wd0=dwd0)
    return dh, dhb, gm


def _layer_bwd(dh, dhb, saved, w, tabs):
    dh, dhb, gm = _layer_bwd_upper(dh, dhb, saved, w, tabs)
    return _layer_bwd_lower(dh, dhb, saved, w, gm)


def _blocks(ts):
    return [jnp.reshape(t, (NDEV, t.shape[0] // NDEV, t.shape[1])) for t in ts]


def _upper_grads(g):
    d_in = _win_unpad(g["win_t"])
    d_pb = jnp.reshape(jnp.reshape(g["wpb_t"], (D, ATT_H, HP))[:, :, :HEAD_DIM], (D, 512))
    return _blocks([g["wg1"], g["wu1"], g["wd1"], d_in, g["wpa_t"], d_pb, g["wout"]])


def _lower_grads(g):
    return _blocks([g["wg0"], g["wu0"], g["wd0"]])


def _big_grads(g):
    return _lower_grads(g) + _upper_grads(g)


def kernel(x, meta_tokens, norm_gains, ffn_w_gate, ffn_w_up, ffn_w_down, w_in, gla_w2, gla_b2, gla_gn, q_norm, k_norm, w_pa, w_pb, b_merge, w_out, final_norm, loss_target, m_meta_tokens, m_norm_gains, m_ffn_w_gate, m_ffn_w_up, m_ffn_w_down, m_w_in, m_gla_w2, m_gla_b2, m_gla_gn, m_q_norm, m_k_norm, m_w_pa, m_w_pb, m_b_merge, m_w_out, m_final_norm, v_meta_tokens, v_norm_gains, v_ffn_w_gate, v_ffn_w_up, v_ffn_w_down, v_w_in, v_gla_w2, v_gla_b2, v_gla_gn, v_q_norm, v_k_norm, v_w_pa, v_w_pb, v_b_merge, v_w_out, v_final_norm):
    dev = _dev_index(lax.axis_index("x"), lax.axis_index("y"), lax.axis_index("c"))
    sh_g = _t(ffn_w_gate).astype(BF16)
    sh_u = _t(ffn_w_up).astype(BF16)
    sh_d = ffn_w_down.astype(BF16)
    sh_in = _t(w_in).astype(BF16)
    sh_pa = _t(w_pa).astype(BF16)
    sh_pb = _t(w_pb).astype(BF16)
    sh_out = w_out.astype(BF16)
    small = jnp.concatenate([jnp.reshape(t, (-1, 128)) for t in
                             (meta_tokens, norm_gains, gla_w2, gla_b2, b_merge)], axis=0)
    small = jnp.pad(small, ((0, 2), (0, 0)))
    def shards(l):
        return [sh_g[l], sh_u[l], sh_d[l], sh_in[l], sh_pa[l], sh_pb[l], sh_out[l]]

    w_leads = [1, 1, 1, 0, 0, 0, 0]
    *g0_ffn0, g_small = _all_gather("gather_layer0", [sh_g[0, 0], sh_u[0, 0], sh_d[0, 0], small], [0, 0, 0, 0])
    rest0 = [sh_g[0, 1], sh_u[0, 1], sh_d[0, 1], sh_in[0], sh_pa[0], sh_pb[0], sh_out[0]]
    rest_leads = [0] * len(rest0)
    started0 = _gather_start("gather_start_0", rest0, rest_leads, g_small, SIBLING_AND_CHIPS)
    meta_full = jnp.reshape(jnp.transpose(g_small[:, 0:16], (1, 0, 2)), (NMETA, D)) + started0[4][0, 0]
    gains_full = jnp.reshape(jnp.transpose(jnp.reshape(g_small[:, 16:28], (NDEV, DEPTH, 3, 128)), (1, 2, 0, 3)), (DEPTH, 3, D))
    w2_full = jnp.reshape(jnp.transpose(jnp.reshape(g_small[:, 28:60], (NDEV, DEPTH, 2, GLA_RANK, 32)), (1, 2, 3, 0, 4)),
                          (DEPTH, 2, GLA_RANK, 256))
    b2_full = jnp.reshape(jnp.transpose(jnp.reshape(g_small[:, 60:62], (NDEV, DEPTH, 2, 32)), (1, 2, 0, 3)), (DEPTH, 2, 256))
    bm_full = jnp.reshape(jnp.transpose(jnp.reshape(g_small[:, 62:70], (NDEV, DEPTH, 2, 128)), (1, 2, 0, 3)), (DEPTH, 2, D))

    def layer_weights(l, ffn, others, gains_l):
        return _layer_weights(ffn, *others, gains_l, w2_full[l], b2_full[l], bm_full[l], gla_gn[l], q_norm[l], k_norm[l])

    xl = x[0]
    lp = xl.shape[0] + NULL + NMETA
    tabs = _rope_tables(lp)
    h = jnp.concatenate([jnp.zeros((NULL, D), F32), meta_full, xl], axis=0)
    weights, saved, started = [], [], {}
    z = _rmsnorm_fwd(h, jnp.reshape(gains_full[0, 0], (1, D)))
    for l in range(DEPTH):
        next_gain = jnp.reshape(gains_full[l + 1, 0], (1, D)) if l + 1 < DEPTH else None
        if l == 0:
            ffn0 = _ffn_single(*g0_ffn0)
            h, z, s0 = _layer_fwd_lower(h, z, ffn0, jnp.reshape(gains_full[0, 1], (1, D)))
            rest = _forward_to_sibling("gather_forward_0", _gather_wait("gather_wait_0", started0, rest_leads, h,
                                                                         SIBLING_AND_CHIPS), rest_leads)
            started[1] = _gather_start("gather_start_1", shards(1), w_leads, rest[0])
            weights.append(layer_weights(0, [ffn0, _ffn_single(*rest[:3])], rest[3:], gains_full[0]))
            z = z + started[1][4][0, 0].astype(BF16)
            h, z, sv = _layer_fwd_upper(h, z, s0, weights[0], tabs, next_gain)
        else:
            tok = jnp.zeros((), F32)
            if l < DEPTH - 1:
                started[l + 1] = _gather_start(f"gather_start_{l + 1}", shards(l + 1), w_leads, h)
                tok = started[l + 1][4][0, 0]
            gathered = _gather_wait(f"gather_wait_{l}", started[l], w_leads, h)
            weights.append(layer_weights(l, _ffn_stacked(*gathered[:3]), gathered[3:], gains_full[l] + tok))
            h, z, sv = _layer_fwd(h, z, weights[l], tabs, next_gain)
        saved.append(sv)
    loss, dh, dhb, d_final = _loss_head(h, loss_target[0], jnp.reshape(final_norm, (1, D)))
    loss = lax.psum(loss[0, 0], ("x", "y", "c"))

    grads, scattering = [None] * DEPTH, {}
    tok = jnp.zeros((), F32)
    for l in reversed(range(DEPTH)):
        w = dict(weights[l], gains=weights[l]["gains"] + tok)
        if l > 0:
            dh, dhb, grads[l] = _layer_bwd(dh, dhb, saved[l], w, tabs)
            scattering[l] = _scatter_start(f"scatter_start_{l}", _big_grads(grads[l]), dhb)
            tok = scattering[l][4][0, 0]
        else:
            dh, dhb, gm = _layer_bwd_upper(dh, dhb, saved[l], w, tabs)
            ups = _upper_grads(gm)
            pair = _pair_sum("rs_pair_up", ups, _exchange_sibling("rs_sibling_up", ups))
            scattering[l] = _chips_start(f"scatter_start_{l}", pair, dhb)
            dhb = dhb + scattering[l][4][0, 0].astype(BF16)
            dh, dhb, grads[l] = _layer_bwd_lower(dh, dhb, saved[l], w, gm)
    grad_x = dh[NULL + NMETA:][None]
    t_lower, t_upper = [True, True, False], [True, True, False, True, True, True, False]
    lows = _lower_grads(grads[0])
    pair_lo = _pair_sum("rs_pair_lo", lows, _exchange_sibling("rs_sibling_lo", lows))
    started_lo = _chips_start("scatter_start_lo", pair_lo, dhb)
    red = [None] * DEPTH
    for l in reversed(range(1, DEPTH)):
        red[l] = _scatter_wait(f"scatter_wait_{l}", scattering[l], started_lo[4], t_lower + t_upper)
    pair, recv = _chips_wait("scatter_wait_0", scattering[0], red[1][-1])
    red_upper = _final_sum("rs_sum_up", pair, recv, t_upper)
    pair_lo, recv_lo = _chips_wait("scatter_wait_lo", started_lo, red_upper[-1])
    red[0] = _final_sum("rs_sum_lo", pair_lo, recv_lo, t_lower) + red_upper
    g_gate = jnp.stack([jnp.stack([red[l][0], red[l][3]]) for l in range(DEPTH)])
    g_up = jnp.stack([jnp.stack([red[l][1], red[l][4]]) for l in range(DEPTH)])
    g_down = jnp.stack([jnp.stack([red[l][2], red[l][5]]) for l in range(DEPTH)])
    g_win = jnp.stack([red[l][6] for l in range(DEPTH)])
    g_wpa = jnp.stack([red[l][7] for l in range(DEPTH)])
    g_wpb = jnp.stack([red[l][8] for l in range(DEPTH)])
    g_wout = jnp.stack([red[l][9] for l in range(DEPTH)])

    d_meta = dh[NULL:NULL + NMETA]
    d_gains = jnp.stack([jnp.concatenate([grads[l]["gain0"], grads[l]["gain"], grads[l]["gain2"]], axis=0)
                         for l in range(DEPTH)])
    d_w2 = jnp.stack([jnp.reshape(jnp.reshape(grads[l]["w2p"], (2, HP, GLA_H, HP))[:, :GLA_RANK, :, :GLA_DK],
                                  (2, GLA_RANK, 256)) for l in range(DEPTH)])
    d_b2 = jnp.stack([jnp.reshape(jnp.reshape(grads[l]["b2p"], (2, GLA_H, HP))[:, :, :GLA_DK], (2, 256))
                      for l in range(DEPTH)])
    d_gn = jnp.stack([grads[l]["gn"][0] for l in range(DEPTH)])
    d_gq = jnp.stack([grads[l]["gq"][0, :HEAD_DIM] for l in range(DEPTH)])
    d_gk = jnp.stack([grads[l]["gk"][0, :HEAD_DIM] for l in range(DEPTH)])
    d_bm = jnp.stack([jnp.concatenate([grads[l]["bma"], grads[l]["bmb"]], axis=0) for l in range(DEPTH)])
    parts = [d_meta, d_gains, d_w2, d_b2, d_gn, d_gq, d_gk, d_bm, d_final[0]]
    sizes = [p.size for p in parts]
    flat = jnp.concatenate([jnp.reshape(p, (-1,)) for p in parts])
    flat = jnp.reshape(flat, (-1, 128))
    nrow = flat.shape[0]
    flat = jnp.pad(flat, ((0, (-nrow) % 8), (0, 0)))
    (g_flat,) = _all_gather("gather_small_grads", [flat], [0])
    tot = jnp.reshape(_sum_gathered(g_flat), (-1,))
    full, o = [], 0
    for p, s in zip(parts, sizes):
        full.append(jnp.reshape(tot[o:o + s], p.shape))
        o += s
    f_meta, f_gains, f_w2, f_b2, f_gn, f_gq, f_gk, f_bm, f_final = full

    def mine(t, width):
        return lax.dynamic_slice_in_dim(t, dev * width, width, axis=t.ndim - 1)

    g_small = dict(meta_tokens=mine(f_meta, 128), norm_gains=mine(f_gains, 128), gla_w2=mine(f_w2, 32),
                   gla_b2=mine(f_b2, 32), gla_gn=f_gn, q_norm=f_gq, k_norm=f_gk, b_merge=mine(f_bm, 128),
                   final_norm=f_final)
    gr = dict(g_small, ffn_w_gate=g_gate, ffn_w_up=g_up, ffn_w_down=g_down, w_in=g_win, w_pa=g_wpa, w_pb=g_wpb,
              w_out=g_wout)
    ws = dict(meta_tokens=meta_tokens, norm_gains=norm_gains, ffn_w_gate=ffn_w_gate, ffn_w_up=ffn_w_up,
              ffn_w_down=ffn_w_down, w_in=w_in, gla_w2=gla_w2, gla_b2=gla_b2, gla_gn=gla_gn, q_norm=q_norm,
              k_norm=k_norm, w_pa=w_pa, w_pb=w_pb, b_merge=b_merge, w_out=w_out, final_norm=final_norm)
    ms = dict(meta_tokens=m_meta_tokens, norm_gains=m_norm_gains, ffn_w_gate=m_ffn_w_gate, ffn_w_up=m_ffn_w_up,
              ffn_w_down=m_ffn_w_down, w_in=m_w_in, gla_w2=m_gla_w2, gla_b2=m_gla_b2, gla_gn=m_gla_gn, q_norm=m_q_norm,
              k_norm=m_k_norm, w_pa=m_w_pa, w_pb=m_w_pb, b_merge=m_b_merge, w_out=m_w_out, final_norm=m_final_norm)
    vs = dict(meta_tokens=v_meta_tokens, norm_gains=v_norm_gains, ffn_w_gate=v_ffn_w_gate, ffn_w_up=v_ffn_w_up,
              ffn_w_down=v_ffn_w_down, w_in=v_w_in, gla_w2=v_gla_w2, gla_b2=v_gla_b2, gla_gn=v_gla_gn, q_norm=v_q_norm,
              k_norm=v_k_norm, w_pa=v_w_pa, w_pb=v_w_pb, b_merge=v_b_merge, w_out=v_w_out, final_norm=v_final_norm)
    names = ["meta_tokens", "norm_gains", "ffn_w_gate", "ffn_w_up", "ffn_w_down", "w_in", "gla_w2", "gla_b2", "gla_gn",
             "q_norm", "k_norm", "w_pa", "w_pb", "b_merge", "w_out", "final_norm"]
    deltas, new_m, new_v = [], [], []
    for n in names:
        dlt, nm, nv = _adamw(ws[n], gr[n], ms[n], vs[n])
        deltas.append(dlt)
        new_m.append(nm)
        new_v.append(nv)
    return (loss, grad_x, *[gr[n] for n in names], *deltas, *new_m, *new_v)
```

```python
import functools
import math

import jax
import jax.numpy as jnp
import numpy as np
from jax import lax
from jax.experimental import pallas as pl
from jax.experimental.pallas import tpu as pltpu

F32 = jnp.float32
BF16 = jnp.bfloat16
MESH = pl.DeviceIdType.MESH
ANY = pl.BlockSpec(memory_space=pl.ANY)

NDEV = 8
D = 1024
DFF = 2816
DEPTH = 4
NMETA = 16
NULL = 112
GRID_W = 64
EPS = 1e-6
HP = 128
GLA_H = 4
GLA_DK = 64
GLA_RANK = 16
GLA_TAU = 16.0
CHUNK = 64
ATT_H = 8
ATT_KV = 2
ATT_G = ATT_H // ATT_KV
HEAD_DIM = 64
ROPE_THETA = 10000.0

IN_SIZES = (256, 256, 512, 512, 16, 16, 512, 128, 128, 1024, 1024)
IN_NAMES = ("qa", "ka", "va", "ra", "lrf", "lrb", "qb", "kb", "vb", "ga", "gb")
D_IN = sum(IN_SIZES)
P_ORDER = ("qb", "ga", "gb", "qa", "ka", "va", "ra", "kb", "vb", "lrf", "lrb")
P_WIDTH = dict(qb=1024, ga=1024, gb=1024, qa=512, ka=512, va=512, ra=512, kb=256, vb=256, lrf=128, lrb=128)
P_OFF = {}
_o = 0
for _n in P_ORDER:
    P_OFF[_n] = _o
    _o += P_WIDTH[_n]
D_INP = _o
P_HEADS = dict(qa=(4, 64), ka=(4, 64), qb=(8, 64), kb=(2, 64), vb=(2, 64), lrf=(1, 16), lrb=(1, 16))

ADAM_LR = 0.001
ADAM_B1 = 0.9
ADAM_B2 = 0.999
ADAM_EPS = 1e-08
ADAM_WD = 0.01
ADAM_STEP = 10

VMEM_BIG = 58 * 1024 * 1024
MXU_N = 256


def _cparams(vmem=None):
    return pltpu.CompilerParams(vmem_limit_bytes=vmem) if vmem else pltpu.CompilerParams()


def _pick(n, prefs):
    for p in prefs:
        if n % p == 0:
            return p
    return n


def _tm(lp):
    return _pick(lp, (528, 512, 256, 128))


def _tm_wide(lp):
    return _pick(lp, (1056, 512, 256, 128))


_DN = {"nn": (((1,), (0,)), ((), ())), "nt": (((1,), (1,)), ((), ())), "tn": (((0,), (0,)), ((), ()))}


def _dot(a, b, mode="nn", precision=None):
    return lax.dot_general(a, b, _DN[mode], preferred_element_type=F32, precision=precision)


def _split(x):
    hi = x.astype(BF16)
    return hi, (x - hi.astype(F32)).astype(BF16)


def _dot_sel(t, x, mode="nn"):
    hi, lo = _split(x)
    return _dot(t, hi, mode) + _dot(t, lo, mode)


def _dot3(a, b, mode="nn"):
    ah, al = _split(a)
    bh, bl = _split(b)
    return _dot(ah, bh, mode) + (_dot(ah, bl, mode) + _dot(al, bh, mode))


def _sigmoid(x):
    return 0.5 * jnp.tanh(0.5 * x) + 0.5


def _mm(name, m, n, terms, outs, epilogue, extras=(), *, tm, tn, nk=1, nsub=1, i_outer=False, vmem=None):
    gm, gn = m // tm, n // tn
    assert gm * tm == m and gn * tn == n, (name, m, n, tm, tn)
    n_acc = 1 + max(t[3] for t in terms)

    def gmap(f):
        if i_outer:
            return lambda i, j, kk: f(i, j, kk)
        return lambda j, i, kk: f(i, j, kk)

    in_specs, args = [], []
    for a, b, mode, _, pa, pb in terms:
        kdim = a.shape[-2] if mode == "tn" else a.shape[-1]
        tk = kdim // nk
        assert tk * nk == kdim
        na, nb = (None,) * len(pa), (None,) * len(pb)
        if mode == "tn":
            in_specs.append(pl.BlockSpec(na + (tk, tm), gmap(lambda i, j, kk, pa=pa: pa + (kk, i))))
        else:
            in_specs.append(pl.BlockSpec(na + (tm, tk), gmap(lambda i, j, kk, pa=pa: pa + (i, kk))))
        if mode == "nt":
            in_specs.append(pl.BlockSpec(nb + (tn, tk), gmap(lambda i, j, kk, pb=pb: pb + (j, kk))))
        else:
            in_specs.append(pl.BlockSpec(nb + (tk, tn), gmap(lambda i, j, kk, pb=pb: pb + (kk, j))))
        args += [a, b]
    for arr, kind, off, pe in extras:
        ne = (None,) * len(pe)
        if kind == "mn":
            in_specs.append(pl.BlockSpec(ne + (tm, tn), gmap(lambda i, j, kk, off=off, pe=pe: pe + (i, j + off))))
        else:
            in_specs.append(pl.BlockSpec(ne + (1, tn), gmap(lambda i, j, kk, off=off, pe=pe: pe + (0, j + off))))
        args.append(arr)
    out_shape, out_specs = [], []
    for shape, dtype, kind, off, po in outs:
        no = (None,) * len(po)
        out_shape.append(jax.ShapeDtypeStruct(shape, dtype))
        if kind == "mn":
            out_specs.append(pl.BlockSpec(no + (tm, tn), gmap(lambda i, j, kk, off=off, po=po: po + (i, j + off))))
        else:
            assert not i_outer
            out_specs.append(pl.BlockSpec(no + (1, tn), gmap(lambda i, j, kk, off=off, po=po: po + (0, j + off))))
    n_t, n_e, n_o = len(terms), len(extras), len(outs)
    i_axis = 0 if i_outer else 1

    def body(*refs):
        ins = refs[: 2 * n_t]
        exs = refs[2 * n_t: 2 * n_t + n_e]
        ors = refs[2 * n_t + n_e: 2 * n_t + n_e + n_o]
        accs = refs[2 * n_t + n_e + n_o:]
        i = pl.program_id(i_axis)
        kk = pl.program_id(2)

        def partials(cs):
            part = [None] * n_acc
            for t, (_, _, mode, ai, _, _) in enumerate(terms):
                b_ref = ins[2 * t + 1]
                b_val = b_ref[cs, :] if mode == "nt" else b_ref[:, cs]
                p = _dot(ins[2 * t][...], b_val, mode)
                part[ai] = p if part[ai] is None else part[ai] + p
            return part

        def finish(vals, cs):
            res = epilogue(vals, [e[:, cs] for e in exs], i * tm)
            for (_, dtype, kind, _, _), o_ref, v in zip(outs, ors, res):
                if kind == "mn":
                    o_ref[:, cs] = v.astype(dtype)
                else:
                    @pl.when(i == 0)
                    def _():
                        o_ref[:, cs] = v.astype(dtype)

                    @pl.when(i != 0)
                    def _():
                        o_ref[:, cs] += v.astype(dtype)

        if nk == 1:
            w = tn // nsub
            for s in range(nsub):
                cs = slice(s * w, (s + 1) * w)
                finish(partials(cs), cs)
        else:
            part = partials(slice(None))
            @pl.when(kk == 0)
            def _():
                for a_ref, p in zip(accs, part):
                    a_ref[...] = p

            @pl.when(kk != 0)
            def _():
                for a_ref, p in zip(accs, part):
                    a_ref[...] += p

            @pl.when(kk == nk - 1)
            def _():
                finish([a_ref[...] for a_ref in accs], slice(None))

    scratch = [pltpu.VMEM((tm, tn), F32) for _ in range(n_acc)] if nk > 1 else []
    grid = (gm, gn, nk) if i_outer else (gn, gm, nk)
    res = pl.pallas_call(
        body, name=name, grid=grid, in_specs=in_specs, out_specs=out_specs, out_shape=out_shape,
        scratch_shapes=scratch, compiler_params=_cparams(vmem),
    )(*args)
    return res


def _term(a, b, mode, acc=0, pa=(), pb=()):
    return (a, b, mode, acc, tuple(pa), tuple(pb))


def _row_tile(lp):
    return _pick(lp, (384, 256, 128))


def _rmsnorm_fwd(h, gain):
    lp = h.shape[0]
    tr = _row_tile(lp)

    def body(h_ref, g_ref, z_ref):
        x = h_ref[...]
        r = lax.rsqrt(jnp.mean(x * x, axis=-1, keepdims=True) + EPS)
        z_ref[...] = (x * r * g_ref[...]).astype(BF16)

    return pl.pallas_call(
        body, name="rmsnorm_fwd", grid=(lp // tr,),
        in_specs=[pl.BlockSpec((tr, D), lambda i: (i, 0)), pl.BlockSpec((1, D), lambda i: (0, 0))],
        out_specs=pl.BlockSpec((tr, D), lambda i: (i, 0)),
        out_shape=jax.ShapeDtypeStruct((lp, D), BF16),
    )(h, gain)


def _loss_head(h, target, gain):
    lp = h.shape[0]
    tr = 128

    def body(h_ref, t_ref, g_ref, loss_ref, dh_ref, dhb_ref, dg_ref):
        i = pl.program_id(0)

        @pl.when(i == 0)
        def _():
            loss_ref[...] = jnp.zeros_like(loss_ref)
            dg_ref[...] = jnp.zeros_like(dg_ref)
            dh_ref[...] = jnp.zeros_like(dh_ref)
            dhb_ref[...] = jnp.zeros_like(dhb_ref)

        @pl.when(i != 0)
        def _():
            x = h_ref[...]
            g = g_ref[...]
            r = lax.rsqrt(jnp.mean(x * x, axis=-1, keepdims=True) + EPS)
            xh = x * r
            y = xh * g
            err = y - t_ref[...]
            loss_ref[...] += 0.5 * jnp.sum(jnp.sum(err * err, axis=-1, keepdims=True), axis=0, keepdims=True) / D
            dy = err * (1.0 / D)
            dg_ref[...] += jnp.sum(dy * xh, axis=0, keepdims=True)
            dxh = dy * g
            dx = r * (dxh - xh * jnp.mean(dxh * xh, axis=-1, keepdims=True))
            dh_ref[...] = dx
            dhb_ref[...] = dx.astype(BF16)

    row = pl.BlockSpec((tr, D), lambda i: (i, 0))
    vec = pl.BlockSpec((1, D), lambda i: (0, 0))
    return pl.pallas_call(
        body, name="loss_head", grid=(lp // tr,),
        in_specs=[row, pl.BlockSpec((tr, D), lambda i: (jnp.maximum(i - 1, 0), 0)), vec],
        out_specs=[pl.BlockSpec((1, 1), lambda i: (0, 0)), row, row, vec],
        out_shape=[jax.ShapeDtypeStruct((1, 1), F32), jax.ShapeDtypeStruct((lp, D), F32),
                   jax.ShapeDtypeStruct((lp, D), BF16), jax.ShapeDtypeStruct((1, D), F32)],
    )(h, target, gain)


def _silu_parts(g):
    s = _sigmoid(g)
    return g * s, s * (1.0 + g * (1.0 - s))


def _residual_norm_epi(scale, with_norm):
    def epi(accs, exs, row0):
        h2 = exs[0] + scale * accs[0]
        if not with_norm:
            return [h2]
        r = lax.rsqrt(jnp.mean(h2 * h2, axis=-1, keepdims=True) + EPS)
        return [h2, h2 * r * exs[1]]
    return epi


def _norm_bwd_epi(accs, exs, row0):
    dz = accs[0]
    x, res, g = exs
    r = lax.rsqrt(jnp.mean(x * x, axis=-1, keepdims=True) + EPS)
    xh = x * r
    dxh = dz * g
    dx = r * (dxh - xh * jnp.mean(dxh * xh, axis=-1, keepdims=True))
    rows = row0 + lax.broadcasted_iota(jnp.int32, (dz.shape[0], 1), 0)
    dh = jnp.where(rows >= NULL, res + dx, 0.0)
    return [dh, dh, jnp.sum(dz * xh, axis=0, keepdims=True)]


def _norm_bwd_outs(lp):
    return [((lp, D), F32, "mn", 0, ()), ((lp, D), BF16, "mn", 0, ()), ((1, D), F32, "nsum", 0, ())]


def _ffn_fwd(h, z, wg_t, wu_t, wd, pre, next_gain):
    lp = h.shape[0]
    tm = _tm(lp)

    def up_epi(accs, exs, row0):
        g, u = accs
        sg, _ = _silu_parts(g)
        return [g, u, sg * u]

    bshape = (lp, DFF)
    g_, u_, act = _mm("ffn_up", lp, DFF, [_term(z, wg_t, "nt", 0, (), pre), _term(z, wu_t, "nt", 1, (), pre)],
                      [(bshape, BF16, "mn", 0, ())] * 3, up_epi, tm=tm, tn=DFF, nsub=DFF // MXU_N, vmem=VMEM_BIG)

    with_norm = next_gain is not None
    res = _mm("ffn_down", lp, D, [_term(act, wd, "nn", 0, (), pre)],
              [((lp, D), F32, "mn", 0, ())] + ([((lp, D), BF16, "mn", 0, ())] if with_norm else []),
              _residual_norm_epi(0.5, with_norm),
              extras=[(h, "mn", 0, ())] + ([(next_gain, "n", 0, ())] if with_norm else []),
              tm=tm, tn=D, i_outer=True, vmem=VMEM_BIG)
    return res[0], (res[1] if with_norm else None), dict(h=h, z=z, g=g_, u=u_, act=act)


def _dw(name, a, b, m, n, scale=1.0):
    lp = a.shape[0]
    tm = _pick(m, (2944, 1408, 1024, 512, 256, 128))
    tn = _pick(n, (1024, 512, 256, 128)) if tm <= 1408 else _pick(n, (512, 256, 128))
    nk = lp // _pick(lp, (2112, 256, 128))

    def epi(accs, exs, row0):
        return [accs[0] * scale]

    (w,) = _mm(name, m, n, [_term(a, b, "tn")], [((m, n), BF16, "mn", 0, ())], epi, tm=tm, tn=tn, nk=nk,
               i_outer=True, vmem=VMEM_BIG)
    return w


def _ffn_bwd(dh, dhb, sv, gain, wg_t, wu_t, wd, pre):
    lp = dh.shape[0]
    tm = _tm(lp)

    def dact_epi(accs, exs, row0):
        g = exs[0].astype(F32)
        u = exs[1].astype(F32)
        da = 0.5 * accs[0]
        sg, dsg = _silu_parts(g)
        return [da * u * dsg, da * sg]

    dg_, du_ = _mm("ffn_dact", lp, DFF, [_term(dhb, wd, "nt", 0, (), pre)],
                   [((lp, DFF), BF16, "mn", 0, ())] * 2, dact_epi,
                   extras=[(sv["g"], "mn", 0, ()), (sv["u"], "mn", 0, ())], tm=tm, tn=DFF, nsub=DFF // MXU_N,
                   vmem=VMEM_BIG)
    d_wd = _dw("dw_down", sv["act"], dhb, DFF, D, 0.5)
    d_wg = _dw("dw_gate", dg_, sv["z"], DFF, D)
    d_wu = _dw("dw_up", du_, sv["z"], DFF, D)

    nk = 1
    dh2, dhb2, dgain = _mm("ffn_dz", lp, D, [_term(dg_, wg_t, "nn", 0, (), pre), _term(du_, wu_t, "nn", 0, (), pre)],
                           _norm_bwd_outs(lp), _norm_bwd_epi,
                           extras=[(sv["h"], "mn", 0, ()), (dh, "mn", 0, ()), (gain, "n", 0, ())],
                           tm=tm, tn=D, nk=nk, vmem=VMEM_BIG)
    return dh2, dhb2, dgain, d_wg, d_wu, d_wd


def _gla_gates(hin, w2p, b2p):
    lp = hin.shape[0]
    tr = _row_tile(lp)
    bf, bb = P_OFF["lrf"] // HP, P_OFF["lrb"] // HP

    def body(lf_ref, lb_ref, w_ref, b_ref, o_ref, c_ref):
        i = pl.program_id(0)
        rows = i * tr + lax.broadcasted_iota(jnp.int32, (tr, 1), 0)
        r = lax.broadcasted_iota(jnp.int32, (tr, tr), 0)
        c = lax.broadcasted_iota(jnp.int32, (tr, tr), 1)
        same = (r // CHUNK) == (c // CHUNK)
        for d, l_ref in enumerate((lf_ref, lb_ref)):
            logit = _dot3(l_ref[...].astype(F32), w_ref[d]) + b_ref[d]
            g = jnp.where(rows >= NULL, jax.nn.log_sigmoid(logit) * (1.0 / GLA_TAU), 0.0)
            o_ref[d] = g
            tmat = jnp.where(same & ((r >= c) if d == 0 else (r <= c)), 1.0, 0.0).astype(BF16)
            c_ref[d] = _dot_sel(tmat, g)

    spec = pl.BlockSpec((2, tr, 512), lambda i: (0, i, 0))
    return pl.pallas_call(
        body, name="gla_gates", grid=(lp // tr,),
        in_specs=[pl.BlockSpec((tr, HP), lambda i: (i, bf)), pl.BlockSpec((tr, HP), lambda i: (i, bb)),
                  pl.BlockSpec((2, HP, 512), lambda i: (0, 0, 0)), pl.BlockSpec((2, 1, 512), lambda i: (0, 0, 0))],
        out_specs=[spec, spec],
        out_shape=[jax.ShapeDtypeStruct((2, lp, 512), F32)] * 2,
    )(hin, hin, w2p, b2p)


def _gla_rows(lp):
    return _pick(lp, (384, 256, 128))


def _tri(d):
    r = lax.broadcasted_iota(jnp.int32, (CHUNK, CHUNK), 0)
    c = lax.broadcasted_iota(jnp.int32, (CHUNK, CHUNK), 1)
    return (r >= c) if d == 0 else (r <= c)


def _gla_fwd(hin, gates):
    lp = hin.shape[0]
    rb = _gla_rows(lp)
    nb = lp // rb
    cpb = rb // CHUNK
    nchunk = lp // CHUNK
    qo, ko, vo = P_OFF["qa"] // 512, P_OFF["ka"] // 512, P_OFF["va"] // 512
    scale = GLA_DK ** -0.5

    def body(qf, kf, vf, gf, qb, kb, vb_, gb, of, ob, sf, sb, st):
        @pl.when(pl.program_id(0) == 0)
        def _():
            st[...] = jnp.zeros_like(st)

        ins = ((qf, kf, vf, gf, of, sf), (qb, kb, vb_, gb, ob, sb))
        for ci in range(cpb):
            for d in range(2):
                q_ref, k_ref, v_ref, g_ref, o_ref, s_ref = ins[d]
                tri = _tri(d)
                c = ci if d == 0 else cpb - 1 - ci
                rows = slice(c * CHUNK, (c + 1) * CHUNK)
                for h in range(GLA_H):
                    sl = slice(h * HP, (h + 1) * HP)
                    q = q_ref[rows, sl].astype(F32) * scale
                    k = k_ref[rows, sl].astype(F32)
                    v = v_ref[rows, sl].astype(F32)
                    b = g_ref[rows, sl]
                    btot = b[CHUNK - 1:CHUNK] if d == 0 else b[0:1]
                    qd = (q * jnp.exp(b)).astype(BF16)
                    ki = (k * jnp.exp(-b)).astype(BF16)
                    ke = (k * jnp.exp(btot - b)).astype(BF16)
                    vb = v.astype(BF16)
                    att = jnp.where(tri, _dot(qd, ki, "nt"), 0.0)
                    s_prev = st[d, h]
                    o_ref[rows, sl] = _dot(att.astype(BF16), vb) + _dot(qd, s_prev.astype(BF16), "nt")
                    s_ref[h, c] = s_prev
                    st[d, h] = s_prev * jnp.exp(btot) + _dot(vb, ke, "tn")

    def specs(off):
        return (pl.BlockSpec((rb, 512), lambda b: (b, off)), pl.BlockSpec((rb, 512), lambda b: (nb - 1 - b, off)))

    (qf, qb), (kf, kb), (vf, vb2) = specs(qo), specs(ko), specs(vo)
    gf = pl.BlockSpec((None, rb, 512), lambda b: (0, b, 0))
    gb = pl.BlockSpec((None, rb, 512), lambda b: (1, nb - 1 - b, 0))
    of, ob = specs(0)
    sf = pl.BlockSpec((GLA_H, cpb, HP, HP), lambda b: (0, b, 0, 0))
    sb = pl.BlockSpec((GLA_H, cpb, HP, HP), lambda b: (0, nb - 1 - b, 0, 0))
    osh = jax.ShapeDtypeStruct((lp, GLA_H * HP), F32)
    ssh = jax.ShapeDtypeStruct((GLA_H, nchunk, HP, HP), F32)
    return pl.pallas_call(
        body, name="gla_fwd", grid=(nb,),
        in_specs=[qf, kf, vf, gf, qb, kb, vb2, gb], out_specs=[of, ob, sf, sb], out_shape=[osh, osh, ssh, ssh],
        scratch_shapes=[pltpu.VMEM((2, GLA_H, HP, HP), F32)], compiler_params=_cparams(VMEM_BIG),
    )(hin, hin, hin, gates, hin, hin, hin, gates)


def _gla_bwd(hin, gates, states, do):
    lp = hin.shape[0]
    rb = _gla_rows(lp)
    nb = lp // rb
    cpb = rb // CHUNK
    qo, ko, vo = P_OFF["qa"] // 512, P_OFF["ka"] // 512, P_OFF["va"] // 512
    scale = GLA_DK ** -0.5

    def body(qf, kf, vf, gf, sf, dof, qb, kb, vb_, gb, sb, dob,
             dqf, dkf, dvf, dgf, dqb, dkb, dvb, dgb, dst):
        @pl.when(pl.program_id(0) == 0)
        def _():
            dst[...] = jnp.zeros_like(dst)

        ins = ((qf, kf, vf, gf, sf, dof, dqf, dkf, dvf, dgf), (qb, kb, vb_, gb, sb, dob, dqb, dkb, dvb, dgb))
        for ci in range(cpb):
            for d in range(2):
                q_ref, k_ref, v_ref, g_ref, s_ref, do_ref, dq_ref, dk_ref, dv_ref, dg_ref = ins[d]
                tri, tri_t = _tri(d), _tri(1 - d)
                edge = lax.broadcasted_iota(jnp.int32, (CHUNK, 1), 0) == (CHUNK - 1 if d == 0 else 0)
                c = cpb - 1 - ci if d == 0 else ci
                rows = slice(c * CHUNK, (c + 1) * CHUNK)
                for h in range(GLA_H):
                    sl = slice(h * HP, (h + 1) * HP)
                    q = q_ref[rows, sl].astype(F32) * scale
                    k = k_ref[rows, sl].astype(F32)
                    v = v_ref[rows, sl].astype(F32)
                    dout = do_ref[rows, sl].astype(BF16)
                    b = g_ref[rows, sl]
                    btot = b[CHUNK - 1:CHUNK] if d == 0 else b[0:1]
                    e = jnp.exp(b)
                    ei = jnp.exp(-b)
                    et = jnp.exp(btot - b)
                    etot = jnp.exp(btot)
                    qd = q * e
                    ki = k * ei
                    ke = k * et
                    qdb, kib, keb, vb = qd.astype(BF16), ki.astype(BF16), ke.astype(BF16), v.astype(BF16)
                    att_t = jnp.where(tri_t, _dot(kib, qdb, "nt"), 0.0).astype(BF16)
                    d_att = jnp.where(tri, _dot(dout, vb, "nt"), 0.0).astype(BF16)
                    d_att_t = jnp.where(tri_t, _dot(vb, dout, "nt"), 0.0).astype(BF16)
                    s_prev = s_ref[h, c]
                    ds_t = dst[d, h]
                    ds_b = ds_t.astype(BF16)
                    dv = _dot(att_t, dout) + _dot(keb, ds_b, "nt")
                    d_qd = _dot(d_att, kib) + _dot(dout, s_prev.astype(BF16))
                    d_ki = _dot(d_att_t, qdb)
                    d_ke = _dot(vb, ds_b)
                    d_e = jnp.sum(s_prev * ds_t, axis=0, keepdims=True)
                    dst[d, h] = _dot(dout, qdb, "tn") + ds_t * etot
                    db = d_qd * qd - d_ki * ki - d_ke * ke
                    dbtot = jnp.sum(d_ke * ke, axis=0, keepdims=True) + d_e * etot
                    dq_ref[rows, sl] = (d_qd * e * scale).astype(BF16)
                    dk_ref[rows, sl] = (d_ki * ei + d_ke * et).astype(BF16)
                    dv_ref[rows, sl] = dv.astype(BF16)
                    dg_ref[rows, sl] = db + jnp.where(edge, dbtot, 0.0)

    def fw(off):
        return pl.BlockSpec((rb, 512), lambda b: (nb - 1 - b, off))

    def bw(off):
        return pl.BlockSpec((rb, 512), lambda b: (b, off))

    gf = pl.BlockSpec((None, rb, 512), lambda b: (0, nb - 1 - b, 0))
    gb = pl.BlockSpec((None, rb, 512), lambda b: (1, b, 0))
    sf = pl.BlockSpec((GLA_H, cpb, HP, HP), lambda b: (0, nb - 1 - b, 0, 0))
    sb = pl.BlockSpec((GLA_H, cpb, HP, HP), lambda b: (0, b, 0, 0))
    osh = jax.ShapeDtypeStruct((lp, GLA_H * HP), F32)
    osh_b = jax.ShapeDtypeStruct((lp, GLA_H * HP), BF16)
    res = pl.pallas_call(
        body, name="gla_bwd", grid=(nb,),
        in_specs=[fw(qo), fw(ko), fw(vo), gf, sf, fw(0), bw(qo), bw(ko), bw(vo), gb, sb, bw(0)],
        out_specs=[fw(0)] * 4 + [bw(0)] * 4, out_shape=[osh_b, osh_b, osh_b, osh] * 2,
        scratch_shapes=[pltpu.VMEM((2, GLA_H, HP, HP), F32)], compiler_params=_cparams(VMEM_BIG),
    )(hin, hin, hin, gates, states[0], do, hin, hin, hin, gates, states[1], do)
    return res[:4], res[4:]


def _gla_out_fwd(o2, hin, gn):
    lp = hin.shape[0]
    tr = _row_tile(lp)
    ro = P_OFF["ra"] // 512

    def body(of_ref, ob_ref, r_ref, gn_ref, a_ref):
        r = r_ref[...].astype(F32)
        sr, _ = _silu_parts(r)
        for h in range(GLA_H):
            sl = slice(h * HP, (h + 1) * HP)
            o = of_ref[:, sl] + ob_ref[:, sl]
            rs = lax.rsqrt(jnp.mean(o * o, axis=-1, keepdims=True) + EPS)
            a_ref[:, sl] = (o * rs * gn_ref[:, sl] * sr[:, sl]).astype(BF16)

    row = pl.BlockSpec((tr, 512), lambda i: (i, 0))
    return pl.pallas_call(
        body, name="gla_out_fwd", grid=(lp // tr,),
        in_specs=[row, row, pl.BlockSpec((tr, 512), lambda i: (i, ro)), pl.BlockSpec((1, 512), lambda i: (0, 0))],
        out_specs=row,
        out_shape=jax.ShapeDtypeStruct((lp, 512), BF16),
    )(o2[0], o2[1], hin, gn)


def _gla_out_bwd(da, o2, hin, gn):
    lp = hin.shape[0]
    tr = _row_tile(lp)
    ro = P_OFF["ra"] // 512

    def body(da_ref, of_ref, ob_ref, r_ref, gn_ref, do_ref, dr_ref, dgn_ref):
        i = pl.program_id(0)
        r = r_ref[...].astype(F32)
        sr, dsr = _silu_parts(r)
        da_v = da_ref[...]
        parts = []
        for h in range(GLA_H):
            sl = slice(h * HP, (h + 1) * HP)
            o = of_ref[:, sl] + ob_ref[:, sl]
            rs = lax.rsqrt(jnp.mean(o * o, axis=-1, keepdims=True) + EPS)
            oh = o * rs
            gn_h = gn_ref[:, sl]
            dah = da_v[:, sl]
            dr_ref[:, sl] = (dah * oh * gn_h * dsr[:, sl]).astype(BF16)
            t = dah * sr[:, sl]
            parts.append(jnp.sum(t * oh, axis=0, keepdims=True))
            doh = t * gn_h
            do_ref[:, sl] = rs * (doh - oh * jnp.mean(doh * oh, axis=-1, keepdims=True))
        part = jnp.concatenate(parts, axis=1)

        @pl.when(i == 0)
        def _():
            dgn_ref[...] = part

        @pl.when(i != 0)
        def _():
            dgn_ref[...] += part

    row = pl.BlockSpec((tr, 512), lambda i: (i, 0))
    return pl.pallas_call(
        body, name="gla_out_bwd", grid=(lp // tr,),
        in_specs=[row, row, row, pl.BlockSpec((tr, 512), lambda i: (i, ro)), pl.BlockSpec((1, 512), lambda i: (0, 0))],
        out_specs=[row, row, pl.BlockSpec((1, 512), lambda i: (0, 0))],
        out_shape=[jax.ShapeDtypeStruct((lp, 512), F32), jax.ShapeDtypeStruct((lp, 512), BF16),
                   jax.ShapeDtypeStruct((1, 512), F32)],
    )(da, o2[0], o2[1], hin, gn)


def _gla_in_bwd(gf, gb, gates, hin, w2p, others):
    lp = hin.shape[0]
    tr = _row_tile(lp)
    bf, bb = P_OFF["lrf"] // HP, P_OFF["lrb"] // HP
    names = tuple(others)

    def seg(name):
        return slice(P_OFF[name], P_OFF[name] + P_WIDTH[name])

    def body(dqf_ref, dkf_ref, dvf_ref, dgf_ref, dqb_ref, dkb_ref, dvb_ref, dgb_ref, g_ref, lf_ref, lb_ref, w_ref,
             *rest):
        other_refs, (o_ref, dw_ref, db_ref) = rest[:len(names)], rest[len(names):]
        i = pl.program_id(0)
        for n, ref in zip(names, other_refs):
            o_ref[:, seg(n)] = ref[...].astype(BF16)
        o_ref[:, seg("qa")] = (dqf_ref[...].astype(F32) + dqb_ref[...].astype(F32)).astype(BF16)
        o_ref[:, seg("ka")] = (dkf_ref[...].astype(F32) + dkb_ref[...].astype(F32)).astype(BF16)
        o_ref[:, seg("va")] = (dvf_ref[...].astype(F32) + dvb_ref[...].astype(F32)).astype(BF16)
        olr_ref = o_ref.at[:, P_OFF["lrf"]:P_OFF["lrf"] + 2 * HP]
        rows = i * tr + lax.broadcasted_iota(jnp.int32, (tr, 1), 0)
        r = lax.broadcasted_iota(jnp.int32, (tr, tr), 0)
        c = lax.broadcasted_iota(jnp.int32, (tr, tr), 1)
        same = (r // CHUNK) == (c // CHUNK)
        for d, (l_ref, dg_ref) in enumerate(((lf_ref, dgf_ref), (lb_ref, dgb_ref))):
            tmat = jnp.where(same & ((r <= c) if d == 0 else (r >= c)), 1.0, 0.0).astype(BF16)
            dg = _dot_sel(tmat, dg_ref[...])
            sig_neg = 1.0 - jnp.exp(GLA_TAU * g_ref[d])
            dlogit = jnp.where(rows >= NULL, dg * (1.0 / GLA_TAU) * sig_neg, 0.0)
            olr_ref[:, d * HP:(d + 1) * HP] = _dot3(dlogit, w_ref[d], "nt").astype(BF16)
            dw = _dot3(l_ref[...].astype(F32), dlogit, "tn")
            dbias = jnp.sum(dlogit, axis=0, keepdims=True)

            @pl.when(i == 0)
            def _():
                dw_ref[d] = dw
                db_ref[d] = dbias

            @pl.when(i != 0)
            def _():
                dw_ref[d] += dw
                db_ref[d] += dbias

    two = pl.BlockSpec((2, tr, 512), lambda i: (0, i, 0))
    row = pl.BlockSpec((tr, 512), lambda i: (i, 0))
    return pl.pallas_call(
        body, name="gla_in_bwd", grid=(lp // tr,),
        in_specs=[row] * 8 + [two, pl.BlockSpec((tr, HP), lambda i: (i, bf)),
                  pl.BlockSpec((tr, HP), lambda i: (i, bb)), pl.BlockSpec((2, HP, 512), lambda i: (0, 0, 0))] +
                 [pl.BlockSpec((tr, P_WIDTH[n]), lambda i: (i, 0)) for n in names],
        out_specs=[pl.BlockSpec((tr, D_INP), lambda i: (i, 0)),
                   pl.BlockSpec((2, HP, 512), lambda i: (0, 0, 0)), pl.BlockSpec((2, 1, 512), lambda i: (0, 0, 0))],
        out_shape=[jax.ShapeDtypeStruct((lp, D_INP), BF16), jax.ShapeDtypeStruct((2, HP, 512), F32),
                   jax.ShapeDtypeStruct((2, 1, 512), F32)],
        compiler_params=_cparams(VMEM_BIG),
    )(*gf, *gb, gates, hin, hin, w2p, *[others[n] for n in names])


def _rope_tables(lp):
    n_tok = lp - NULL - NMETA
    rows = n_tok // GRID_W
    row = np.repeat(np.arange(rows), GRID_W).astype(np.float32)
    col = np.tile(np.arange(GRID_W), rows).astype(np.float32)
    inv = (ROPE_THETA ** (-np.arange(0, 32, 2, dtype=np.float32) / 32)).astype(np.float32)
    ang = np.concatenate([row[:, None] * inv, col[:, None] * inv], axis=-1)
    ang = np.concatenate([np.zeros((NULL + NMETA, 32), np.float32), ang], axis=0)
    cos, sin = np.cos(ang).astype(np.float32), np.sin(ang).astype(np.float32)
    z16 = np.zeros((lp, 16), np.float32)
    z64 = np.zeros((lp, 64), np.float32)
    c = np.concatenate([cos[:, :16], cos[:, :16], cos[:, 16:], cos[:, 16:], z64], axis=1)
    a = np.concatenate([-sin[:, :16], z16, -sin[:, 16:], z16, z64], axis=1)
    b = np.concatenate([z16, sin[:, :16], z16, sin[:, 16:], z64], axis=1)
    return jnp.asarray(c), jnp.asarray(a), jnp.asarray(b)


def _rope(x, c, a, b):
    return x * c + pltpu.roll(x, HP - 16, 1) * a + pltpu.roll(x, 16, 1) * b


def _rope_t(dx, c, a, b):
    return dx * c + pltpu.roll(dx * a, 16, 1) + pltpu.roll(dx * b, HP - 16, 1)


def _attn_prep(hin, gq, gk, tabs):
    lp = hin.shape[0]
    tr = _row_tile(lp)
    qo, ko, vo = P_OFF["qb"] // 1024, P_OFF["kb"] // 256, P_OFF["vb"] // 256

    def body(q_ref, k_ref, v_ref, gq_ref, gk_ref, c_ref, a_ref, b_ref, oq_ref, ok_ref, ov_ref):
        c, a, b = c_ref[...], a_ref[...], b_ref[...]
        for src, g_ref, dst, nh, sc in ((q_ref, gq_ref, oq_ref, ATT_H, Q_SCALE), (k_ref, gk_ref, ok_ref, ATT_KV, 1.0)):
            for h in range(nh):
                sl = slice(h * HP, (h + 1) * HP)
                x = src[:, sl].astype(F32)
                r = lax.rsqrt(jnp.sum(x * x, axis=-1, keepdims=True) * (1.0 / HEAD_DIM) + EPS)
                dst[:, sl] = (_rope(x * r * g_ref[...], c, a, b) * sc).astype(BF16)
        lane = lax.broadcasted_iota(jnp.int32, (1, ATT_KV * HP), 1)
        ov_ref[...] = jnp.where(lane % HP == HEAD_DIM, 1.0, v_ref[...]).astype(BF16)

    tab = pl.BlockSpec((tr, HP), lambda i: (i, 0))
    vec = pl.BlockSpec((1, HP), lambda i: (0, 0))
    return pl.pallas_call(
        body, name="attn_prep", grid=(lp // tr,),
        in_specs=[pl.BlockSpec((tr, 1024), lambda i: (i, qo)), pl.BlockSpec((tr, 256), lambda i: (i, ko)),
                  pl.BlockSpec((tr, 256), lambda i: (i, vo)), vec, vec, tab, tab, tab],
        out_specs=[pl.BlockSpec((tr, 1024), lambda i: (i, 0)), pl.BlockSpec((tr, 256), lambda i: (i, 0)),
                   pl.BlockSpec((tr, 256), lambda i: (i, 0))],
        out_shape=[jax.ShapeDtypeStruct((lp, 1024), BF16), jax.ShapeDtypeStruct((lp, 256), BF16),
                   jax.ShapeDtypeStruct((lp, 256), BF16)],
    )(hin, hin, hin, gq, gk, *tabs)


def _attn_prep_bwd(dqr, dkr, hin, gq, gk, tabs):
    lp = hin.shape[0]
    tr = _row_tile(lp)
    qo, ko = P_OFF["qb"] // 1024, P_OFF["kb"] // 256

    def body(dq_ref, dk_ref, q_ref, k_ref, gq_ref, gk_ref, c_ref, a_ref, b_ref, oq_ref, ok_ref, dgq_ref, dgk_ref):
        i = pl.program_id(0)
        c, a, b = c_ref[...], a_ref[...], b_ref[...]
        for src, dsrc, g_ref, dst, dg_ref, nh, sc in (
                (q_ref, dq_ref, gq_ref, oq_ref, dgq_ref, ATT_H, Q_SCALE),
                (k_ref, dk_ref, gk_ref, ok_ref, dgk_ref, ATT_KV, 1.0)):
            acc = jnp.zeros((1, HP), F32)
            for h in range(nh):
                sl = slice(h * HP, (h + 1) * HP)
                x = src[:, sl].astype(F32)
                r = lax.rsqrt(jnp.sum(x * x, axis=-1, keepdims=True) * (1.0 / HEAD_DIM) + EPS)
                xh = x * r
                dxn = _rope_t(dsrc[:, sl] * sc, c, a, b)
                acc = acc + jnp.sum(dxn * xh, axis=0, keepdims=True)
                dxh = dxn * g_ref[...]
                dx = r * (dxh - xh * (jnp.sum(dxh * xh, axis=-1, keepdims=True) * (1.0 / HEAD_DIM)))
                dst[:, sl] = dx.astype(BF16)

            @pl.when(i == 0)
            def _():
                dg_ref[...] = acc

            @pl.when(i != 0)
            def _():
                dg_ref[...] += acc

    tab = pl.BlockSpec((tr, HP), lambda i: (i, 0))
    vec = pl.BlockSpec((1, HP), lambda i: (0, 0))
    return pl.pallas_call(
        body, name="attn_prep_bwd", grid=(lp // tr,),
        in_specs=[pl.BlockSpec((tr, 1024), lambda i: (i, 0)), pl.BlockSpec((tr, 256), lambda i: (i, 0)),
                  pl.BlockSpec((tr, 1024), lambda i: (i, qo)), pl.BlockSpec((tr, 256), lambda i: (i, ko)),
                  vec, vec, tab, tab, tab],
        out_specs=[pl.BlockSpec((tr, 1024), lambda i: (i, 0)), pl.BlockSpec((tr, 256), lambda i: (i, 0)), vec, vec],
        out_shape=[jax.ShapeDtypeStruct((lp, 1024), BF16), jax.ShapeDtypeStruct((lp, 256), BF16),
                   jax.ShapeDtypeStruct((1, HP), F32), jax.ShapeDtypeStruct((1, HP), F32)],
    )(dqr, dkr, hin, hin, gq, gk, *tabs)


QB = 128
GH = 2
Q_SCALE = HEAD_DIM ** -0.5 * math.log2(math.e)
LN2 = math.log(2.0)


def _stack(ref, g0, n):
    return jnp.concatenate([ref[:, (g0 + g) * HP:(g0 + g + 1) * HP] for g in range(n)], axis=0)


def _attn_fwd(qr, kr, vb):
    lp = qr.shape[0]
    nq = lp // QB

    def body(q_ref, k_ref, v_ref, o_ref, lse_ref):
        qb = pl.program_id(1)
        keys = lax.broadcasted_iota(jnp.int32, (1, lp), 1)
        lane = lax.broadcasted_iota(jnp.int32, (1, HP), 1)
        rows = qb * QB + lax.broadcasted_iota(jnp.int32, (QB, 1), 0)
        for ch in range(ATT_G // GH):
            qs = _stack(q_ref, ch * GH, GH)
            s = _dot(qs, k_ref[...], "nt")
            s = jnp.where(keys >= NULL, s, -1e30)
            m = jnp.max(s, axis=-1, keepdims=True)
            p = jnp.exp2(s - m).astype(BF16)
            o_raw = _dot(p, v_ref[...])
            l = jnp.sum(jnp.where(lane == HEAD_DIM, o_raw, 0.0), axis=-1, keepdims=True)
            o = jnp.where(lane < HEAD_DIM, o_raw / l, 0.0)
            lse = m + jnp.log2(l)
            for g in range(GH):
                sl = slice((ch * GH + g) * HP, (ch * GH + g + 1) * HP)
                o_ref[:, sl] = jnp.where(rows >= NULL, o[g * QB:(g + 1) * QB], 0.0).astype(BF16)
                lse_ref[:, sl] = jnp.broadcast_to(lse[g * QB:(g + 1) * QB], (QB, HP))

    qspec = pl.BlockSpec((QB, ATT_G * HP), lambda kv, qb: (qb, kv))
    kspec = pl.BlockSpec((lp, HP), lambda kv, qb: (0, kv))
    return pl.pallas_call(
        body, name="attn_fwd", grid=(ATT_KV, nq),
        in_specs=[qspec, kspec, kspec], out_specs=[qspec, qspec],
        out_shape=[jax.ShapeDtypeStruct((lp, ATT_H * HP), BF16), jax.ShapeDtypeStruct((lp, ATT_H * HP), F32)],
        compiler_params=_cparams(VMEM_BIG),
    )(qr, kr, vb)


def _attn_bwd(qr, kr, vb, o, lse, do):
    lp = qr.shape[0]
    nq = lp // QB

    def body(q_ref, k_ref, v_ref, o_ref, lse_ref, do_ref, dq_ref, dk_ref, dv_ref):
        qb = pl.program_id(1)

        @pl.when(qb == 0)
        def _():
            dk_ref[...] = jnp.zeros_like(dk_ref)
            dv_ref[...] = jnp.zeros_like(dv_ref)

        keys = lax.broadcasted_iota(jnp.int32, (1, lp), 1)
        k = k_ref[...]
        dk_acc, dv_acc = None, None
        for ch in range(ATT_G // GH):
            g0 = ch * GH
            qs = _stack(q_ref, g0, GH)
            dos = _stack(do_ref, g0, GH)
            os_ = _stack(o_ref, g0, GH).astype(F32)
            lse_s = jnp.concatenate([lse_ref[:, (g0 + g) * HP:(g0 + g) * HP + 1] for g in range(GH)], axis=0)
            delta = jnp.sum(dos * os_, axis=-1, keepdims=True) * LN2
            s = _dot(qs, k, "nt")
            p = jnp.where(keys >= NULL, jnp.exp2(s - lse_s), 0.0)
            dob = dos.astype(BF16)
            dp = _dot((dos * LN2).astype(BF16), v_ref[...], "nt")
            ds = (p * (dp - delta)).astype(BF16)
            dq = _dot(ds, k)
            for g in range(GH):
                dq_ref[:, (g0 + g) * HP:(g0 + g + 1) * HP] = dq[g * QB:(g + 1) * QB]
            dv_c = _dot(p.astype(BF16), dob, "tn")
            dk_c = _dot(ds, qs, "tn")
            dv_acc = dv_c if dv_acc is None else dv_acc + dv_c
            dk_acc = dk_c if dk_acc is None else dk_acc + dk_c
        dv_ref[...] += dv_acc
        dk_ref[...] += dk_acc

    qspec = pl.BlockSpec((QB, ATT_G * HP), lambda kv, qb: (qb, kv))
    kspec = pl.BlockSpec((lp, HP), lambda kv, qb: (0, kv))
    return pl.pallas_call(
        body, name="attn_bwd", grid=(ATT_KV, nq),
        in_specs=[qspec, kspec, kspec, qspec, qspec, qspec], out_specs=[qspec, kspec, kspec],
        out_shape=[jax.ShapeDtypeStruct((lp, ATT_H * HP), F32), jax.ShapeDtypeStruct((lp, ATT_KV * HP), F32),
                   jax.ShapeDtypeStruct((lp, ATT_KV * HP), F32)],
        compiler_params=_cparams(VMEM_BIG),
    )(qr, kr, vb, o, lse, do)


def _mixer_fwd(h, z, wl, l, tabs, next_gain):
    lp = h.shape[0]
    tm = _tm(lp)

    def id_epi(accs, exs, row0):
        return [accs[0]]

    (hin,) = _mm("in_proj", lp, D_INP, [_term(z, wl["win_t"], "nt", 0, (), (l,))], [((lp, D_INP), BF16, "mn", 0, ())],
                 id_epi, tm=_tm_wide(lp), tn=D_INP // 2, vmem=VMEM_BIG)
    gates, cum = _gla_gates(hin, wl["w2p"][l], wl["b2p"][l])
    o_f, o_b, s_f, s_b = _gla_fwd(hin, cum)
    o2, states = (o_f, o_b), (s_f, s_b)
    a = _gla_out_fwd(o2, hin, wl["gn"][l])
    qr, kr, vb = _attn_prep(hin, wl["gq"][l], wl["gk"][l], tabs)
    b, lse = _attn_fwd(qr, kr, vb)

    def merge_epi(accs, exs, row0):
        pa, pb = accs
        ga, gb, bma, bmb = exs
        y = _sigmoid(ga + bma) * pa + _sigmoid(gb + bmb) * pb
        return [y, pa, pb]

    y, pa, pb = _mm("merge", lp, D, [_term(a, wl["wpa_t"], "nt", 0, (), (l,)), _term(b, wl["wpb_t"], "nt", 1, (), (l,))],
                    [((lp, D), BF16, "mn", 0, ())] * 3, merge_epi,
                    extras=[(hin, "mn", P_OFF["ga"] // D, ()), (hin, "mn", P_OFF["gb"] // D, ()),
                            (wl["bm"], "n", 0, (l, 0)), (wl["bm"], "n", 0, (l, 1))],
                    tm=_tm_wide(lp), tn=D, nsub=D // MXU_N, i_outer=True, vmem=VMEM_BIG)

    h2, z2 = _mm("out_proj", lp, D, [_term(y, wl["wout"], "nn", 0, (), (l,))],
                 [((lp, D), F32, "mn", 0, ()), ((lp, D), BF16, "mn", 0, ())], _residual_norm_epi(1.0, True),
                 extras=[(h, "mn", 0, ()), (next_gain, "n", 0, ())], tm=_tm_wide(lp), tn=D, i_outer=True, vmem=VMEM_BIG)
    sv = dict(h=h, z=z, hin=hin, gates=gates, cum=cum, o2=o2, states=states, a=a, qr=qr, kr=kr, vb=vb, b=b, lse=lse,
              y=y, pa=pa, pb=pb)
    return h2, z2, sv


def _mixer_bwd(dh, dhb, sv, gain, wl, l, tabs):
    lp = dh.shape[0]
    tm = _tm(lp)
    hin = sv["hin"]

    def merge_bwd_epi(accs, exs, row0):
        dy = accs[0]
        ga, gb, pa, pb, bma, bmb = exs
        sa = _sigmoid(ga + bma)
        sb = _sigmoid(gb + bmb)
        dga = dy * pa.astype(F32) * sa * (1.0 - sa)
        dgb = dy * pb.astype(F32) * sb * (1.0 - sb)
        return [dy * sa, dy * sb, dga, dgb, jnp.sum(dga, axis=0, keepdims=True), jnp.sum(dgb, axis=0, keepdims=True)]

    big = ((lp, D), BF16, "mn", 0, ())
    vec = ((1, D), F32, "nsum", 0, ())
    dpa, dpb, dga, dgb, dbma, dbmb = _mm(
        "merge_bwd", lp, D, [_term(dhb, wl["wout"], "nt", 0, (), (l,))], [big, big, big, big, vec, vec], merge_bwd_epi,
        extras=[(hin, "mn", P_OFF["ga"] // D, ()), (hin, "mn", P_OFF["gb"] // D, ()), (sv["pa"], "mn", 0, ()),
                (sv["pb"], "mn", 0, ()), (wl["bm"], "n", 0, (l, 0)), (wl["bm"], "n", 0, (l, 1))],
        tm=tm, tn=D, nsub=D // MXU_N, vmem=VMEM_BIG)
    d_wout = _dw("dw_out", sv["y"], dhb, D, D)
    d_wpa_t = _dw("dw_pa", dpa, sv["a"], D, 512)
    d_wpb_t = _dw("dw_pb", dpb, sv["b"], D, ATT_H * HP)

    def id_epi(accs, exs, row0):
        return [accs[0]]

    (da,) = _mm("d_a", lp, 512, [_term(dpa, wl["wpa_t"], "nn", 0, (), (l,))], [((lp, 512), F32, "mn", 0, ())], id_epi,
                tm=_tm_wide(lp), tn=512, i_outer=True, vmem=VMEM_BIG)
    (db,) = _mm("d_b", lp, ATT_H * HP, [_term(dpb, wl["wpb_t"], "nn", 0, (), (l,))],
                [((lp, ATT_H * HP), F32, "mn", 0, ())], id_epi, tm=_tm_wide(lp), tn=512, i_outer=True, vmem=VMEM_BIG)
    d_o, d_ra, d_gn = _gla_out_bwd(da, sv["o2"], hin, wl["gn"][l])
    g_fw, g_bw = _gla_bwd(hin, sv["cum"], sv["states"], d_o)
    dqr, dkr, dvb = _attn_bwd(sv["qr"], sv["kr"], sv["vb"], sv["b"], sv["lse"], db)
    d_qb, d_kb, d_gq, d_gk = _attn_prep_bwd(dqr, dkr, hin, wl["gq"][l], wl["gk"][l], tabs)
    dhin, d_w2p, d_b2p = _gla_in_bwd(g_fw, g_bw, sv["gates"], hin, wl["w2p"][l],
                                     dict(qb=d_qb, ga=dga, gb=dgb, ra=d_ra, kb=d_kb, vb=dvb))
    d_win_t = _dw("dw_in", dhin, sv["z"], D_INP, D)
    dh2, dhb2, dgain = _mm("in_proj_dz", lp, D, [_term(dhin, wl["win_t"], "nn", 0, (), (l,))], _norm_bwd_outs(lp),
                           _norm_bwd_epi, extras=[(sv["h"], "mn", 0, ()), (dh, "mn", 0, ()), (gain, "n", 0, ())],
                           tm=tm, tn=D, nk=1, vmem=VMEM_BIG)
    grads = dict(gain=dgain, wout=d_wout, wpa_t=d_wpa_t, wpb_t=d_wpb_t, win_t=d_win_t, gn=d_gn, w2p=d_w2p, b2p=d_b2p,
                 gq=d_gq, gk=d_gk, bma=dbma, bmb=dbmb)
    return dh2, dhb2, grads


def _mesh_pos():
    x, y, c = lax.axis_index("x"), lax.axis_index("y"), lax.axis_index("c")
    chips = [(1 - x, y), (x, 1 - y), (1 - x, 1 - y)]
    return x, y, c, chips


def _dev_index(x, y, c):
    return 4 * x + 2 * y + c


def _all_gather(name, shards, leads):
    nt = len(shards)

    def blk(ref, lead, idx):
        return ref.at[(slice(None),) * lead + (idx,)]

    def body(*refs):
        xs, outs = refs[:nt], refs[nt:2 * nt]
        send_sems, recv_sems, local_sems = refs[2 * nt:]
        x, y, c, chips = _mesh_pos()
        me, sibling = (x, y, c), (x, y, 1 - c)

        def copy(t, k, block, to, own=False):
            dst = blk(outs[t], leads[t], _dev_index(*block))
            return pltpu.make_async_remote_copy(
                src_ref=xs[t] if own else dst, dst_ref=dst, send_sem=send_sems.at[t, k], recv_sem=recv_sems.at[t, k],
                device_id=to, device_id_type=MESH)

        locals_ = [pltpu.make_async_copy(xs[t], blk(outs[t], leads[t], _dev_index(*me)), local_sems.at[t])
                   for t in range(nt)]
        for cp in locals_:
            cp.start()
        first = []
        for t in range(nt):
            first.append(copy(t, 0, me, sibling, own=True))
            first += [copy(t, 1 + j, me, (*chip, c), own=True) for j, chip in enumerate(chips)]
        for cp in first:
            cp.start()
        passed = []
        for j, chip in enumerate(chips):
            for t in range(nt):
                copy(t, 1 + j, (*chip, c), me).wait_recv()
                fw = copy(t, 4 + j, (*chip, c), sibling)
                fw.start()
                passed.append(fw)
        for t in range(nt):
            copy(t, 0, sibling, me).wait_recv()
        for j, chip in enumerate(chips):
            for t in range(nt):
                copy(t, 4 + j, (*chip, 1 - c), me).wait_recv()
        for cp in first + passed:
            cp.wait_send()
        for cp in locals_:
            cp.wait()

    out_shape = [jax.ShapeDtypeStruct(s.shape[:ld] + (NDEV,) + s.shape[ld:], s.dtype) for s, ld in zip(shards, leads)]
    return pl.pallas_call(
        body, name=name, in_specs=[ANY] * nt, out_specs=[ANY] * nt, out_shape=out_shape,
        scratch_shapes=[pltpu.SemaphoreType.DMA((nt, 7)), pltpu.SemaphoreType.DMA((nt, 7)),
                        pltpu.SemaphoreType.DMA((nt,))],
    )(*shards)


def _exchange_sibling(name, gs):
    nt = len(gs)

    def body(*refs):
        xs, outs = refs[:nt], refs[nt:2 * nt]
        send_sems, recv_sems = refs[2 * nt:]
        x, y, c, _ = _mesh_pos()
        sibling = (x, y, 1 - c)
        copies = []
        for t in range(nt):
            for ch in range(4):
                copies.append(pltpu.make_async_remote_copy(
                    src_ref=xs[t].at[2 * ch + (1 - c)], dst_ref=outs[t].at[ch],
                    send_sem=send_sems.at[t, ch], recv_sem=recv_sems.at[t, ch],
                    device_id=sibling, device_id_type=MESH))
        for cp in copies:
            cp.start()
        for cp in copies:
            cp.wait()

    out_shape = [jax.ShapeDtypeStruct((4,) + g.shape[1:], g.dtype) for g in gs]
    return pl.pallas_call(
        body, name=name, in_specs=[ANY] * nt, out_specs=[ANY] * nt, out_shape=out_shape,
        scratch_shapes=[pltpu.SemaphoreType.DMA((nt, 4)), pltpu.SemaphoreType.DMA((nt, 4))],
    )(*gs)


def _pair_sum(name, gs, recv):
    c = lax.axis_index("c")
    outs = []
    for t, (g, rv) in enumerate(zip(gs, recv)):
        _, r, cols = rv.shape

        def body(c_ref, g_ref, r_ref, o_ref):
            o_ref[...] = (g_ref[...].astype(F32) + r_ref[...].astype(F32)).astype(o_ref.dtype)

        outs.append(pl.pallas_call(
            body, name=f"{name}_{t}",
            grid_spec=pltpu.PrefetchScalarGridSpec(
                num_scalar_prefetch=1, grid=(4,),
                in_specs=[pl.BlockSpec((None, r, cols), lambda ch, cr: (2 * ch + cr[0], 0, 0)),
                          pl.BlockSpec((None, r, cols), lambda ch, cr: (ch, 0, 0))],
                out_specs=pl.BlockSpec((None, r, cols), lambda ch, cr: (ch, 0, 0))),
            out_shape=jax.ShapeDtypeStruct(rv.shape, rv.dtype),
        )(jnp.reshape(c, (1,)).astype(jnp.int32), g, rv))
    return outs


def _final_sum(name, ps, recv, transposed):
    chip = 2 * lax.axis_index("x") + lax.axis_index("y")
    outs = []
    for t, (p, rv) in enumerate(zip(ps, recv)):
        _, r, cols = rv.shape
        tr_out = transposed[t]
        oshape = (cols, r) if tr_out else (r, cols)

        def body(c_ref, p_ref, r0_ref, r1_ref, r2_ref, o_ref):
            acc = ((p_ref[...].astype(F32) + r0_ref[...].astype(F32)) + r1_ref[...].astype(F32)) + r2_ref[...].astype(F32)
            o_ref[...] = acc.T if tr_out else acc

        outs.append(pl.pallas_call(
            body, name=f"{name}_{t}",
            grid_spec=pltpu.PrefetchScalarGridSpec(
                num_scalar_prefetch=1, grid=(1,),
                in_specs=[pl.BlockSpec((None, r, cols), lambda i, cr: (cr[0], 0, 0))] +
                         [pl.BlockSpec((None, r, cols), lambda i, cr, j=j: (j, 0, 0)) for j in range(3)],
                out_specs=pl.BlockSpec(oshape, lambda i, cr: (0, 0))),
            out_shape=jax.ShapeDtypeStruct(oshape, F32),
        )(jnp.reshape(chip, (1,)).astype(jnp.int32), p, rv, rv, rv))
    return outs


def _sum_gathered(g):
    _, r, cols = g.shape

    def body(g_ref, o_ref):
        acc = g_ref[0]
        for d in range(1, NDEV):
            acc = acc + g_ref[d]
        o_ref[...] = acc

    return pl.pallas_call(body, name="small_sum", out_shape=jax.ShapeDtypeStruct((r, cols), F32))(g)


HBM = pl.BlockSpec(memory_space=pltpu.HBM)
SEM = pl.BlockSpec(memory_space=pltpu.SEMAPHORE)
EFFECT = pltpu.SideEffectType.DATAFLOW_SIDE_EFFECTING
NREL = NDEV - 1


def _related(k):
    x, y, c = lax.axis_index("x"), lax.axis_index("y"), lax.axis_index("c")
    px = 1 - x if k & 4 else x
    py = 1 - y if k & 2 else y
    pc = 1 - c if k & 1 else c
    return (px, py, pc), _dev_index(px, py, pc)


def _in_hbm(a):
    return pltpu.with_memory_space_constraint(a, pltpu.HBM)


ALL_RELS = tuple(range(1, NDEV))
CHIP_RELS = (4, 2, 6)


def _split_copies(xs, lands, send_sems, recv_sems, src_of, dst_of, rels):
    copies = []
    for t in range(len(xs)):
        for q, k in enumerate(rels):
            peer, peer_idx = _related(k)
            copies.append(pltpu.make_async_remote_copy(
                src_ref=src_of(xs[t], t, peer_idx), dst_ref=dst_of(lands[t], t, q, peer_idx),
                send_sem=send_sems.at[t * len(rels) + q], recv_sem=recv_sems.at[t * len(rels) + q],
                device_id=peer, device_id_type=MESH))
    return copies


def _exchange_start(name, xs, lands, src_of, dst_of, after, rels=ALL_RELS):
    nt = len(xs)

    def body(*refs):
        x_refs, land_refs = refs[:nt], refs[nt:2 * nt]
        send_sems, recv_sems = refs[2 * nt + 1], refs[2 * nt + 2]
        token = refs[-1]
        for cp in _split_copies(x_refs, land_refs, send_sems, recv_sems, src_of, dst_of, rels):
            cp.start()
        token[...] = jnp.zeros_like(token)

    res = pl.pallas_call(
        body, name=name,
        out_shape=(pltpu.SemaphoreType.DMA((nt * len(rels),)), pltpu.SemaphoreType.DMA((nt * len(rels),)),
                   *[pltpu.HBM(a.shape, a.dtype) for a in xs], *[pltpu.HBM(a.shape, a.dtype) for a in lands],
                   jax.ShapeDtypeStruct((8, 128), F32)),
        in_specs=[HBM] * (2 * nt) + [ANY],
        out_specs=(SEM, SEM, *[HBM] * (2 * nt), pl.BlockSpec(memory_space=pltpu.VMEM)),
        input_output_aliases={i: 2 + i for i in range(2 * nt)},
        compiler_params=pltpu.CompilerParams(has_side_effects=EFFECT),
    )(*[_in_hbm(a) for a in xs], *[_in_hbm(a) for a in lands], after)
    return res[0], res[1], res[2:2 + nt], res[2 + nt:2 + 2 * nt], res[-1]


def _exchange_wait(name, send_sems, recv_sems, xs, lands, src_of, dst_of, after, rels=ALL_RELS):
    nt = len(xs)

    def body(*refs):
        x_refs, land_refs = refs[:nt], refs[nt:2 * nt]
        send_sems, recv_sems = refs[2 * nt], refs[2 * nt + 1]
        for cp in _split_copies(x_refs, land_refs, send_sems, recv_sems, src_of, dst_of, rels):
            cp.wait_send()
            cp.wait_recv()

    res = pl.pallas_call(
        body, name=name,
        out_shape=(*[pltpu.HBM(a.shape, a.dtype) for a in xs], *[pltpu.HBM(a.shape, a.dtype) for a in lands]),
        in_specs=[HBM] * (2 * nt) + [SEM, SEM, ANY], out_specs=tuple([HBM] * (2 * nt)),
        input_output_aliases={i: i for i in range(2 * nt)},
        compiler_params=pltpu.CompilerParams(has_side_effects=EFFECT),
    )(*xs, *lands, send_sems, recv_sems, after)
    return res[:nt], res[nt:]


def _gather_start(name, shards, leads, after, rels=ALL_RELS):
    def src_of(x_ref, t, peer_idx):
        return x_ref

    def dst_of(land_ref, t, k, peer_idx):
        me = _dev_index(lax.axis_index("x"), lax.axis_index("y"), lax.axis_index("c"))
        return land_ref.at[(slice(None),) * leads[t] + (me,)]

    lands = [lax.empty(s.shape[:ld] + (NDEV,) + s.shape[ld:], s.dtype) for s, ld in zip(shards, leads)]
    return _exchange_start(name, shards, lands, src_of, dst_of, after, rels)


def _gather_wait(name, started, leads, after, rels=ALL_RELS):
    send_sems, recv_sems, shards, lands, _ = started

    def src_of(x_ref, t, peer_idx):
        return x_ref

    def dst_of(land_ref, t, k, peer_idx):
        return land_ref.at[(slice(None),) * leads[t] + (peer_idx,)]

    shards, lands = _exchange_wait(name, send_sems, recv_sems, shards, lands, src_of, dst_of, after, rels)
    me = _dev_index(lax.axis_index("x"), lax.axis_index("y"), lax.axis_index("c"))
    return [lax.dynamic_update_index_in_dim(g, s, me, ld) for g, s, ld in zip(lands, shards, leads)]


SIBLING_AND_CHIPS = (1,) + CHIP_RELS


def _forward_to_sibling(name, gathered, leads):
    nt = len(gathered)

    def body(*refs):
        ins, outs = refs[:nt], refs[nt:2 * nt]
        send_sems, recv_sems = refs[2 * nt:]
        x, y, c, chips = _mesh_pos()
        copies, arrivals = [], []
        for t in range(nt):
            for j, chip in enumerate(chips):
                def block(core):
                    return outs[t].at[(slice(None),) * leads[t] + (_dev_index(*chip, core),)]
                copies.append(pltpu.make_async_remote_copy(
                    src_ref=block(c), dst_ref=block(c), send_sem=send_sems.at[t, j], recv_sem=recv_sems.at[t, j],
                    device_id=(x, y, 1 - c), device_id_type=MESH))
                arrivals.append(pltpu.make_async_remote_copy(
                    src_ref=block(1 - c), dst_ref=block(1 - c), send_sem=send_sems.at[t, j], recv_sem=recv_sems.at[t, j],
                    device_id=(x, y, 1 - c), device_id_type=MESH))
        for cp in copies:
            cp.start()
        for cp in arrivals:
            cp.wait_recv()
        for cp in copies:
            cp.wait_send()

    return pl.pallas_call(
        body, name=name, in_specs=[ANY] * nt, out_specs=[ANY] * nt,
        out_shape=[jax.ShapeDtypeStruct(g.shape, g.dtype) for g in gathered],
        input_output_aliases={t: t for t in range(nt)},
        scratch_shapes=[pltpu.SemaphoreType.DMA((nt, 3)), pltpu.SemaphoreType.DMA((nt, 3))],
    )(*gathered)


def _scatter_src(x_ref, t, peer_idx):
    return x_ref.at[peer_idx]


def _scatter_dst(land_ref, t, q, peer_idx):
    return land_ref.at[q]


def _chips_src(x_ref, t, peer_idx):
    return x_ref.at[peer_idx // 2]


def _chips_start(name, ps, after):
    lands = [lax.empty((len(CHIP_RELS),) + p.shape[1:], p.dtype) for p in ps]
    return _exchange_start(name, ps, lands, _chips_src, _scatter_dst, after, CHIP_RELS)


def _chips_wait(name, started, after):
    send_sems, recv_sems, ps, lands, _ = started
    return _exchange_wait(name, send_sems, recv_sems, ps, lands, _chips_src, _scatter_dst, after, CHIP_RELS)


def _scatter_start(name, gs, after):
    lands = [lax.empty((NREL,) + g.shape[1:], g.dtype) for g in gs]
    return _exchange_start(name, gs, lands, _scatter_src, _scatter_dst, after)


def _scatter_wait(name, started, after, transposed):
    send_sems, recv_sems, gs, lands, _ = started
    gs, lands = _exchange_wait(name, send_sems, recv_sems, gs, lands, _scatter_src, _scatter_dst, after)
    me = _dev_index(lax.axis_index("x"), lax.axis_index("y"), lax.axis_index("c"))
    outs = []
    for t, (g, rv) in enumerate(zip(gs, lands)):
        _, r, cols = rv.shape
        tr_out = transposed[t]
        oshape = (cols, r) if tr_out else (r, cols)

        def body(c_ref, own_ref, rv_ref, o_ref):
            acc = own_ref[...].astype(F32)
            for k in range(NREL):
                acc = acc + rv_ref[k].astype(F32)
            o_ref[...] = acc.T if tr_out else acc

        outs.append(pl.pallas_call(
            body, name=f"{name}_sum_{t}",
            grid_spec=pltpu.PrefetchScalarGridSpec(
                num_scalar_prefetch=1, grid=(1,),
                in_specs=[pl.BlockSpec((None, r, cols), lambda i, cr: (cr[0], 0, 0)),
                          pl.BlockSpec((NREL, r, cols), lambda i, cr: (0, 0, 0))],
                out_specs=pl.BlockSpec(oshape, lambda i, cr: (0, 0))),
            out_shape=jax.ShapeDtypeStruct(oshape, F32), compiler_params=_cparams(VMEM_BIG),
        )(jnp.reshape(me, (1,)).astype(jnp.int32), g, rv))
    return outs


def _adamw(w, g, m, v):
    shape = w.shape
    cols = shape[-1]
    rows = math.prod(shape[:-1]) if len(shape) > 1 else 1
    w2, g2, m2, v2 = (jnp.reshape(t, (rows, cols)) for t in (w, g, m, v))
    tr = _pick(rows, (1024, 512, 256, 128)) if rows * cols > 65536 else rows
    c1 = 1.0 / (1.0 - ADAM_B1 ** ADAM_STEP)
    c2 = 1.0 / (1.0 - ADAM_B2 ** ADAM_STEP)

    def body(w_ref, g_ref, m_ref, v_ref, d_ref, nm_ref, nv_ref):
        gv = g_ref[...]
        nm = ADAM_B1 * m_ref[...] + (1.0 - ADAM_B1) * gv
        nv = ADAM_B2 * v_ref[...] + (1.0 - ADAM_B2) * (gv * gv)
        d_ref[...] = -ADAM_LR * ((nm * c1) / (jnp.sqrt(nv * c2) + ADAM_EPS) + ADAM_WD * w_ref[...])
        nm_ref[...] = nm
        nv_ref[...] = nv

    spec = pl.BlockSpec((tr, cols), lambda i: (i, 0))
    osh = jax.ShapeDtypeStruct((rows, cols), F32)
    d, nm, nv = pl.pallas_call(
        body, name="adamw", grid=(rows // tr,), in_specs=[spec] * 4, out_specs=[spec] * 3, out_shape=[osh] * 3,
        compiler_params=_cparams(VMEM_BIG),
    )(w2, g2, m2, v2)
    return jnp.reshape(d, shape), jnp.reshape(nm, shape), jnp.reshape(nv, shape)


def _pad_heads(w, name):
    if name not in P_HEADS:
        return w
    nh, real = P_HEADS[name]
    w = jnp.reshape(w, w.shape[:-2] + (nh, real, w.shape[-1]))
    w = jnp.pad(w, [(0, 0)] * (w.ndim - 2) + [(0, HP - real), (0, 0)])
    return jnp.reshape(w, w.shape[:-3] + (nh * HP, w.shape[-1]))


def _unpad_heads(w, name):
    if name not in P_HEADS:
        return w
    nh, real = P_HEADS[name]
    w = jnp.reshape(w, w.shape[:-2] + (nh, HP, w.shape[-1]))[..., :real, :]
    return jnp.reshape(w, w.shape[:-3] + (nh * real, w.shape[-1]))


def _win_pad(win_t):
    segs, o = {}, 0
    for n, s in zip(IN_NAMES, IN_SIZES):
        segs[n] = win_t[..., o:o + s, :]
        o += s
    return jnp.concatenate([_pad_heads(segs[n], n) for n in P_ORDER], axis=-2)


def _win_unpad(win_p):
    segs = {n: _unpad_heads(win_p[..., P_OFF[n]:P_OFF[n] + P_WIDTH[n], :], n) for n in P_ORDER}
    return jnp.concatenate([segs[n] for n in IN_NAMES], axis=-2)


def _t(w):
    return jnp.swapaxes(w, -1, -2)


def _ffn_stacked(g_g, g_u, g_d):
    wg, wu, wd = (jnp.reshape(g, (2, DFF, D)) for g in (g_g, g_u, g_d))
    return [(wg, wu, wd, (j,)) for j in range(2)]


def _ffn_single(g_g, g_u, g_d):
    return tuple(jnp.reshape(g, (DFF, D)) for g in (g_g, g_u, g_d)) + ((),)


def _layer_weights(ffn, g_in, g_pa, g_pb, g_out, gains, w2, b2, bm, gn, gq, gk):
    w2p = jnp.pad(jnp.reshape(w2, (2, GLA_RANK, GLA_H, GLA_DK)), ((0, 0), (0, HP - GLA_RANK), (0, 0), (0, HP - GLA_DK)))
    b2p = jnp.pad(jnp.reshape(b2, (2, 1, GLA_H, GLA_DK)), ((0, 0), (0, 0), (0, 0), (0, HP - GLA_DK)))
    wpb_t = jnp.pad(jnp.reshape(g_pb, (D, ATT_H, HEAD_DIM)), ((0, 0), (0, 0), (0, HP - HEAD_DIM)))
    return dict(
        gains=jnp.reshape(gains, (1, 3, 1, D)), ffn=ffn,
        win_t=_win_pad(jnp.reshape(g_in, (1, D_IN, D))), wpa_t=jnp.reshape(g_pa, (1, D, 512)),
        wpb_t=jnp.reshape(wpb_t, (1, D, ATT_H * HP)), wout=jnp.reshape(g_out, (1, D, D)),
        w2p=jnp.reshape(w2p, (1, 2, HP, GLA_H * HP)), b2p=jnp.reshape(b2p, (1, 2, 1, GLA_H * HP)),
        bm=jnp.reshape(bm, (1, 2, 1, D)), gn=jnp.reshape(gn, (1, 1, GLA_H * HP)),
        gq=jnp.pad(jnp.reshape(gq, (1, 1, HEAD_DIM)), ((0, 0), (0, 0), (0, HP - HEAD_DIM))),
        gk=jnp.pad(jnp.reshape(gk, (1, 1, HEAD_DIM)), ((0, 0), (0, 0), (0, HP - HEAD_DIM))))


def _layer_fwd_lower(h, z, ffn0, gain1):
    return _ffn_fwd(h, z, *ffn0, gain1)


def _layer_fwd_upper(h, z, s0, w, tabs, next_gain):
    h, z, s1 = _mixer_fwd(h, z, w, 0, tabs, w["gains"][0, 2])
    h, z, s2 = _ffn_fwd(h, z, *w["ffn"][1], next_gain)
    return h, z, (s0, s1, s2)


def _layer_fwd(h, z, w, tabs, next_gain):
    h, z, s0 = _layer_fwd_lower(h, z, w["ffn"][0], w["gains"][0, 1])
    return _layer_fwd_upper(h, z, s0, w, tabs, next_gain)


def _layer_bwd_upper(dh, dhb, saved, w, tabs):
    _, s1, s2 = saved
    dh, dhb, dg2, dwg1, dwu1, dwd1 = _ffn_bwd(dh, dhb, s2, w["gains"][0, 2], *w["ffn"][1])
    dh, dhb, gm = _mixer_bwd(dh, dhb, s1, w["gains"][0, 1], w, 0, tabs)
    gm.update(gain2=dg2, wg1=dwg1, wu1=dwu1, wd1=dwd1)
    return dh, dhb, gm


def _layer_bwd_lower(dh, dhb, saved, w, gm):
    dh, dhb, dg0, dwg0, dwu0, dwd0 = _ffn_bwd(dh, dhb, saved[0], w["gains"][0, 0], *w["ffn"][0])
    gm.update(gain0=dg0, wg0=dwg0, wu0=dwu0, wd0=dwd0)
    return dh, dhb, gm


def _layer_bwd(dh, dhb, saved, w, tabs):
    dh, dhb, gm = _layer_bwd_upper(dh, dhb, saved, w, tabs)
    return _layer_bwd_lower(dh, dhb, saved, w, gm)


def _blocks(ts):
    return [jnp.reshape(t, (NDEV, t.shape[0] // NDEV, t.shape[1])) for t in ts]


def _upper_grads(g):
    d_in = _win_unpad(g["win_t"])
    d_pb = jnp.reshape(jnp.reshape(g["wpb_t"], (D, ATT_H, HP))[:, :, :HEAD_DIM], (D, 512))
    return _blocks([g["wg1"], g["wu1"], g["wd1"], d_in, g["wpa_t"], d_pb, g["wout"]])


def _lower_grads(g):
    return _blocks([g["wg0"], g["wu0"], g["wd0"]])


def _big_grads(g):
    return _lower_grads(g) + _upper_grads(g)


def kernel(x, meta_tokens, norm_gains, ffn_w_gate, ffn_w_up, ffn_w_down, w_in, gla_w2, gla_b2, gla_gn, q_norm, k_norm, w_pa, w_pb, b_merge, w_out, final_norm, loss_target, m_meta_tokens, m_norm_gains, m_ffn_w_gate, m_ffn_w_up, m_ffn_w_down, m_w_in, m_gla_w2, m_gla_b2, m_gla_gn, m_q_norm, m_k_norm, m_w_pa, m_w_pb, m_b_merge, m_w_out, m_final_norm, v_meta_tokens, v_norm_gains, v_ffn_w_gate, v_ffn_w_up, v_ffn_w_down, v_w_in, v_gla_w2, v_gla_b2, v_gla_gn, v_q_norm, v_k_norm, v_w_pa, v_w_pb, v_b_merge, v_w_out, v_final_norm):
    dev = _dev_index(lax.axis_index("x"), lax.axis_index("y"), lax.axis_index("c"))
    sh_g = _t(ffn_w_gate).astype(BF16)
    sh_u = _t(ffn_w_up).astype(BF16)
    sh_d = ffn_w_down.astype(BF16)
    sh_in = _t(w_in).astype(BF16)
    sh_pa = _t(w_pa).astype(BF16)
    sh_pb = _t(w_pb).astype(BF16)
    sh_out = w_out.astype(BF16)
    small = jnp.concatenate([jnp.reshape(t, (-1, 128)) for t in
                             (meta_tokens, norm_gains, gla_w2, gla_b2, b_merge)], axis=0)
    small = jnp.pad(small, ((0, 2), (0, 0)))
    def shards(l):
        return [sh_g[l], sh_u[l], sh_d[l], sh_in[l], sh_pa[l], sh_pb[l], sh_out[l]]

    w_leads = [1, 1, 1, 0, 0, 0, 0]
    *g0_ffn0, g_small = _all_gather("gather_layer0", [sh_g[0, 0], sh_u[0, 0], sh_d[0, 0], small], [0, 0, 0, 0])
    rest0 = [sh_g[0, 1], sh_u[0, 1], sh_d[0, 1], sh_in[0], sh_pa[0], sh_pb[0], sh_out[0]]
    rest_leads = [0] * len(rest0)
    started0 = _gather_start("gather_start_0", rest0, rest_leads, g_small, SIBLING_AND_CHIPS)
    meta_full = jnp.reshape(jnp.transpose(g_small[:, 0:16], (1, 0, 2)), (NMETA, D)) + started0[4][0, 0]
    gains_full = jnp.reshape(jnp.transpose(jnp.reshape(g_small[:, 16:28], (NDEV, DEPTH, 3, 128)), (1, 2, 0, 3)), (DEPTH, 3, D))
    w2_full = jnp.reshape(jnp.transpose(jnp.reshape(g_small[:, 28:60], (NDEV, DEPTH, 2, GLA_RANK, 32)), (1, 2, 3, 0, 4)),
                          (DEPTH, 2, GLA_RANK, 256))
    b2_full = jnp.reshape(jnp.transpose(jnp.reshape(g_small[:, 60:62], (NDEV, DEPTH, 2, 32)), (1, 2, 0, 3)), (DEPTH, 2, 256))
    bm_full = jnp.reshape(jnp.transpose(jnp.reshape(g_small[:, 62:70], (NDEV, DEPTH, 2, 128)), (1, 2, 0, 3)), (DEPTH, 2, D))

    def layer_weights(l, ffn, others, gains_l):
        return _layer_weights(ffn, *others, gains_l, w2_full[l], b2_full[l], bm_full[l], gla_gn[l], q_norm[l], k_norm[l])

    xl = x[0]
    lp = xl.shape[0] + NULL + NMETA
    tabs = _rope_tables(lp)
    h = jnp.concatenate([jnp.zeros((NULL, D), F32), meta_full, xl], axis=0)
    weights, saved, started = [], [], {}
    z = _rmsnorm_fwd(h, jnp.reshape(gains_full[0, 0], (1, D)))
    for l in range(DEPTH):
        next_gain = jnp.reshape(gains_full[l + 1, 0], (1, D)) if l + 1 < DEPTH else None
        if l == 0:
            ffn0 = _ffn_single(*g0_ffn0)
            h, z, s0 = _layer_fwd_lower(h, z, ffn0, jnp.reshape(gains_full[0, 1], (1, D)))
            rest = _forward_to_sibling("gather_forward_0", _gather_wait("gather_wait_0", started0, rest_leads, h,
                                                                         SIBLING_AND_CHIPS), rest_leads)
            started[1] = _gather_start("gather_start_1", shards(1), w_leads, rest[0])
            weights.append(layer_weights(0, [ffn0, _ffn_single(*rest[:3])], rest[3:], gains_full[0]))
            z = z + started[1][4][0, 0].astype(BF16)
            h, z, sv = _layer_fwd_upper(h, z, s0, weights[0], tabs, next_gain)
        else:
            tok = jnp.zeros((), F32)
            if l < DEPTH - 1:
                started[l + 1] = _gather_start(f"gather_start_{l + 1}", shards(l + 1), w_leads, h)
                tok = started[l + 1][4][0, 0]
            gathered = _gather_wait(f"gather_wait_{l}", started[l], w_leads, h)
            weights.append(layer_weights(l, _ffn_stacked(*gathered[:3]), gathered[3:], gains_full[l] + tok))
            h, z, sv = _layer_fwd(h, z, weights[l], tabs, next_gain)
        saved.append(sv)
    loss, dh, dhb, d_final = _loss_head(h, loss_target[0], jnp.reshape(final_norm, (1, D)))
    loss = lax.psum(loss[0, 0], ("x", "y", "c"))

    grads, scattering = [None] * DEPTH, {}
    tok = jnp.zeros((), F32)
    for l in reversed(range(DEPTH)):
        w = dict(weights[l], gains=weights[l]["gains"] + tok)
        if l > 0:
            dh, dhb, grads[l] = _layer_bwd(dh, dhb, saved[l], w, tabs)
            scattering[l] = _scatter_start(f"scatter_start_{l}", _big_grads(grads[l]), dhb)
            tok = scattering[l][4][0, 0]
        else:
            dh, dhb, gm = _layer_bwd_upper(dh, dhb, saved[l], w, tabs)
            ups = _upper_grads(gm)
            pair = _pair_sum("rs_pair_up", ups, _exchange_sibling("rs_sibling_up", ups))
            scattering[l] = _chips_start(f"scatter_start_{l}", pair, dhb)
            dhb = dhb + scattering[l][4][0, 0].astype(BF16)
            dh, dhb, grads[l] = _layer_bwd_lower(dh, dhb, saved[l], w, gm)
    grad_x = dh[NULL + NMETA:][None]
    t_lower, t_upper = [False, False, False], [False, False, False, True, True, True, False]
    lows = _lower_grads(grads[0])
    pair_lo = _pair_sum("rs_pair_lo", lows, _exchange_sibling("rs_sibling_lo", lows))
    started_lo = _chips_start("scatter_start_lo", pair_lo, dhb)
    red = [None] * DEPTH
    for l in reversed(range(1, DEPTH)):
        red[l] = _scatter_wait(f"scatter_wait_{l}", scattering[l], started_lo[4], t_lower + t_upper)
    pair, recv = _chips_wait("scatter_wait_0", scattering[0], red[1][-1])
    red_upper = _final_sum("rs_sum_up", pair, recv, t_upper)
    pair_lo, recv_lo = _chips_wait("scatter_wait_lo", started_lo, red_upper[-1])
    red[0] = _final_sum("rs_sum_lo", pair_lo, recv_lo, t_lower) + red_upper
    g_gate = jnp.stack([jnp.stack([red[l][0], red[l][3]]) for l in range(DEPTH)])
    g_up = jnp.stack([jnp.stack([red[l][1], red[l][4]]) for l in range(DEPTH)])
    g_down = jnp.stack([jnp.stack([red[l][2], red[l][5]]) for l in range(DEPTH)])
    g_win = jnp.stack([red[l][6] for l in range(DEPTH)])
    g_wpa = jnp.stack([red[l][7] for l in range(DEPTH)])
    g_wpb = jnp.stack([red[l][8] for l in range(DEPTH)])
    g_wout = jnp.stack([red[l][9] for l in range(DEPTH)])

    d_meta = dh[NULL:NULL + NMETA]
    d_gains = jnp.stack([jnp.concatenate([grads[l]["gain0"], grads[l]["gain"], grads[l]["gain2"]], axis=0)
                         for l in range(DEPTH)])
    d_w2 = jnp.stack([jnp.reshape(jnp.reshape(grads[l]["w2p"], (2, HP, GLA_H, HP))[:, :GLA_RANK, :, :GLA_DK],
                                  (2, GLA_RANK, 256)) for l in range(DEPTH)])
    d_b2 = jnp.stack([jnp.reshape(jnp.reshape(grads[l]["b2p"], (2, GLA_H, HP))[:, :, :GLA_DK], (2, 256))
                      for l in range(DEPTH)])
    d_gn = jnp.stack([grads[l]["gn"][0] for l in range(DEPTH)])
    d_gq = jnp.stack([grads[l]["gq"][0, :HEAD_DIM] for l in range(DEPTH)])
    d_gk = jnp.stack([grads[l]["gk"][0, :HEAD_DIM] for l in range(DEPTH)])
    d_bm = jnp.stack([jnp.concatenate([grads[l]["bma"], grads[l]["bmb"]], axis=0) for l in range(DEPTH)])
    parts = [d_meta, d_gains, d_w2, d_b2, d_gn, d_gq, d_gk, d_bm, d_final[0]]
    sizes = [p.size for p in parts]
    flat = jnp.concatenate([jnp.reshape(p, (-1,)) for p in parts])
    flat = jnp.reshape(flat, (-1, 128))
    nrow = flat.shape[0]
    flat = jnp.pad(flat, ((0, (-nrow) % 8), (0, 0)))
    (g_flat,) = _all_gather("gather_small_grads", [flat], [0])
    tot = jnp.reshape(_sum_gathered(g_flat), (-1,))
    full, o = [], 0
    for p, s in zip(parts, sizes):
        full.append(jnp.reshape(tot[o:o + s], p.shape))
        o += s
    f_meta, f_gains, f_w2, f_b2, f_gn, f_gq, f_gk, f_bm, f_final = full

    def mine(t, width):
        return lax.dynamic_slice_in_dim(t, dev * width, width, axis=t.ndim - 1)

    g_small = dict(meta_tokens=mine(f_meta, 128), norm_gains=mine(f_gains, 128), gla_w2=mine(f_w2, 32),
                   gla_b2=mine(f_b2, 32), gla_gn=f_gn, q_norm=f_gq, k_norm=f_gk, b_merge=mine(f_bm, 128),
                   final_norm=f_final)
    gr = dict(g_small, ffn_w_gate=g_gate, ffn_w_up=g_up, ffn_w_down=g_down, w_in=g_win, w_pa=g_wpa, w_pb=g_wpb,
              w_out=g_wout)
    ws = dict(meta_tokens=meta_tokens, norm_gains=norm_gains, ffn_w_gate=ffn_w_gate, ffn_w_up=ffn_w_up,
              ffn_w_down=ffn_w_down, w_in=w_in, gla_w2=gla_w2, gla_b2=gla_b2, gla_gn=gla_gn, q_norm=q_norm,
              k_norm=k_norm, w_pa=w_pa, w_pb=w_pb, b_merge=b_merge, w_out=w_out, final_norm=final_norm)
    ms = dict(meta_tokens=m_meta_tokens, norm_gains=m_norm_gains, ffn_w_gate=m_ffn_w_gate, ffn_w_up=m_ffn_w_up,
              ffn_w_down=m_ffn_w_down, w_in=m_w_in, gla_w2=m_gla_w2, gla_b2=m_gla_b2, gla_gn=m_gla_gn, q_norm=m_q_norm,
              k_norm=m_k_norm, w_pa=m_w_pa, w_pb=m_w_pb, b_merge=m_b_merge, w_out=m_w_out, final_norm=m_final_norm)
    vs = dict(meta_tokens=v_meta_tokens, norm_gains=v_norm_gains, ffn_w_gate=v_ffn_w_gate, ffn_w_up=v_ffn_w_up,
              ffn_w_down=v_ffn_w_down, w_in=v_w_in, gla_w2=v_gla_w2, gla_b2=v_gla_b2, gla_gn=v_gla_gn, q_norm=v_q_norm,
              k_norm=v_k_norm, w_pa=v_w_pa, w_pb=v_w_pb, b_merge=v_b_merge, w_out=v_w_out, final_norm=v_final_norm)
    names = ["meta_tokens", "norm_gains", "ffn_w_gate", "ffn_w_up", "ffn_w_down", "w_in", "gla_w2", "gla_b2", "gla_gn",
             "q_norm", "k_norm", "w_pa", "w_pb", "b_merge", "w_out", "final_norm"]
    deltas, new_m, new_v = [], [], []
    for n in names:
        if n in ("ffn_w_gate", "ffn_w_up"):
            dlt, nm, nv = (_t(r) for r in _adamw(_t(ws[n]), gr[n], _t(ms[n]), _t(vs[n])))
            gr[n] = _t(gr[n])
        else:
            dlt, nm, nv = _adamw(ws[n], gr[n], ms[n], vs[n])
        deltas.append(dlt)
        new_m.append(nm)
        new_v.append(nv)
    return (loss, grad_x, *[gr[n] for n in names], *deltas, *new_m, *new_v)
```

```python
import math

import jax
import jax.numpy as jnp
import numpy as np
from jax import lax
from jax.experimental import pallas as pl
from jax.experimental.pallas import tpu as pltpu

F32 = jnp.float32
BF16 = jnp.bfloat16
MESH = pl.DeviceIdType.MESH
ANY = pl.BlockSpec(memory_space=pl.ANY)

NDEV = 8
D = 1024
DFF = 2816
DEPTH = 4
NMETA = 16
NULL = 112
GRID_W = 64
EPS = 1e-6
HP = 128
GLA_H = 4
GLA_DK = 64
GLA_RANK = 16
GLA_TAU = 16.0
CHUNK = 64
ATT_H = 8
ATT_KV = 2
ATT_G = ATT_H // ATT_KV
HEAD_DIM = 64
ROPE_THETA = 10000.0

IN_SIZES = (256, 256, 512, 512, 16, 16, 512, 128, 128, 1024, 1024)
IN_NAMES = ("qa", "ka", "va", "ra", "lrf", "lrb", "qb", "kb", "vb", "ga", "gb")
D_IN = sum(IN_SIZES)
P_ORDER = ("qb", "ga", "gb", "qa", "ka", "va", "ra", "kb", "vb", "lrf", "lrb")
P_WIDTH = dict(qb=1024, ga=1024, gb=1024, qa=512, ka=512, va=512, ra=512, kb=256, vb=256, lrf=128, lrb=128)
P_OFF = {}
_o = 0
for _n in P_ORDER:
    P_OFF[_n] = _o
    _o += P_WIDTH[_n]
D_INP = _o
P_HEADS = dict(qa=(4, 64), ka=(4, 64), qb=(8, 64), kb=(2, 64), vb=(2, 64), lrf=(1, 16), lrb=(1, 16))

ADAM_LR = 0.001
ADAM_B1 = 0.9
ADAM_B2 = 0.999
ADAM_EPS = 1e-08
ADAM_WD = 0.01
ADAM_STEP = 10

VMEM_BIG = 58 * 1024 * 1024
MXU_N = 256


def _cparams(vmem=None):
    return pltpu.CompilerParams(vmem_limit_bytes=vmem) if vmem else pltpu.CompilerParams()


def _pick(n, prefs):
    for p in prefs:
        if n % p == 0:
            return p
    return n


def _tm(lp):
    return _pick(lp, (528, 512, 256, 128))


def _tm_wide(lp):
    return _pick(lp, (1056, 512, 256, 128))


_DN = {"nn": (((1,), (0,)), ((), ())), "nt": (((1,), (1,)), ((), ())), "tn": (((0,), (0,)), ((), ()))}


def _dot(a, b, mode="nn", precision=None):
    return lax.dot_general(a, b, _DN[mode], preferred_element_type=F32, precision=precision)


def _split(x):
    hi = x.astype(BF16)
    return hi, (x - hi.astype(F32)).astype(BF16)


def _dot_sel(t, x, mode="nn"):
    hi, lo = _split(x)
    return _dot(t, hi, mode) + _dot(t, lo, mode)


def _dot3(a, b, mode="nn"):
    ah, al = _split(a)
    bh, bl = _split(b)
    return _dot(ah, bh, mode) + (_dot(ah, bl, mode) + _dot(al, bh, mode))


def _sigmoid(x):
    return 0.5 * jnp.tanh(0.5 * x) + 0.5


def _mm(name, m, n, terms, outs, epilogue, extras=(), *, tm, tn, nk=1, nsub=1, i_outer=False, vmem=None):
    gm, gn = m // tm, n // tn
    assert gm * tm == m and gn * tn == n, (name, m, n, tm, tn)
    n_acc = 1 + max(t[3] for t in terms)

    def gmap(f):
        if i_outer:
            return lambda i, j, kk: f(i, j, kk)
        return lambda j, i, kk: f(i, j, kk)

    in_specs, args = [], []
    for a, b, mode, _, pa, pb in terms:
        kdim = a.shape[-2] if mode == "tn" else a.shape[-1]
        tk = kdim // nk
        assert tk * nk == kdim
        na, nb = (None,) * len(pa), (None,) * len(pb)
        if mode == "tn":
            in_specs.append(pl.BlockSpec(na + (tk, tm), gmap(lambda i, j, kk, pa=pa: pa + (kk, i))))
        else:
            in_specs.append(pl.BlockSpec(na + (tm, tk), gmap(lambda i, j, kk, pa=pa: pa + (i, kk))))
        if mode == "nt":
            in_specs.append(pl.BlockSpec(nb + (tn, tk), gmap(lambda i, j, kk, pb=pb: pb + (j, kk))))
        else:
            in_specs.append(pl.BlockSpec(nb + (tk, tn), gmap(lambda i, j, kk, pb=pb: pb + (kk, j))))
        args += [a, b]
    for arr, kind, off, pe in extras:
        ne = (None,) * len(pe)
        if kind == "mn":
            in_specs.append(pl.BlockSpec(ne + (tm, tn), gmap(lambda i, j, kk, off=off, pe=pe: pe + (i, j + off))))
        else:
            in_specs.append(pl.BlockSpec(ne + (1, tn), gmap(lambda i, j, kk, off=off, pe=pe: pe + (0, j + off))))
        args.append(arr)
    out_shape, out_specs = [], []
    for shape, dtype, kind, off, po in outs:
        no = (None,) * len(po)
        out_shape.append(jax.ShapeDtypeStruct(shape, dtype))
        if kind == "mn":
            out_specs.append(pl.BlockSpec(no + (tm, tn), gmap(lambda i, j, kk, off=off, po=po: po + (i, j + off))))
        else:
            assert not i_outer
            out_specs.append(pl.BlockSpec(no + (1, tn), gmap(lambda i, j, kk, off=off, po=po: po + (0, j + off))))
    n_t, n_e, n_o = len(terms), len(extras), len(outs)
    i_axis = 0 if i_outer else 1

    def body(*refs):
        ins = refs[: 2 * n_t]
        exs = refs[2 * n_t: 2 * n_t + n_e]
        ors = refs[2 * n_t + n_e: 2 * n_t + n_e + n_o]
        accs = refs[2 * n_t + n_e + n_o:]
        i = pl.program_id(i_axis)
        kk = pl.program_id(2)

        def partials(cs):
            part = [None] * n_acc
            for t, (_, _, mode, ai, _, _) in enumerate(terms):
                b_ref = ins[2 * t + 1]
                b_val = b_ref[cs, :] if mode == "nt" else b_ref[:, cs]
                p = _dot(ins[2 * t][...], b_val, mode)
                part[ai] = p if part[ai] is None else part[ai] + p
            return part

        def finish(vals, cs):
            res = epilogue(vals, [e[:, cs] for e in exs], i * tm)
            for (_, dtype, kind, _, _), o_ref, v in zip(outs, ors, res):
                if kind == "mn":
                    o_ref[:, cs] = v.astype(dtype)
                else:
                    @pl.when(i == 0)
                    def _():
                        o_ref[:, cs] = v.astype(dtype)

                    @pl.when(i != 0)
                    def _():
                        o_ref[:, cs] += v.astype(dtype)

        if nk == 1:
            w = tn // nsub
            for s in range(nsub):
                cs = slice(s * w, (s + 1) * w)
                finish(partials(cs), cs)
        else:
            part = partials(slice(None))
            @pl.when(kk == 0)
            def _():
                for a_ref, p in zip(accs, part):
                    a_ref[...] = p

            @pl.when(kk != 0)
            def _():
                for a_ref, p in zip(accs, part):
                    a_ref[...] += p

            @pl.when(kk == nk - 1)
            def _():
                finish([a_ref[...] for a_ref in accs], slice(None))

    scratch = [pltpu.VMEM((tm, tn), F32) for _ in range(n_acc)] if nk > 1 else []
    grid = (gm, gn, nk) if i_outer else (gn, gm, nk)
    res = pl.pallas_call(
        body, name=name, grid=grid, in_specs=in_specs, out_specs=out_specs, out_shape=out_shape,
        scratch_shapes=scratch, compiler_params=_cparams(vmem),
    )(*args)
    return res


def _term(a, b, mode, acc=0, pa=(), pb=()):
    return (a, b, mode, acc, tuple(pa), tuple(pb))


def _row_tile(lp):
    return _pick(lp, (384, 256, 128))


def _rmsnorm_fwd(h, gain):
    lp = h.shape[0]
    tr = _row_tile(lp)

    def body(h_ref, g_ref, z_ref):
        x = h_ref[...]
        r = lax.rsqrt(jnp.mean(x * x, axis=-1, keepdims=True) + EPS)
        z_ref[...] = (x * r * g_ref[...]).astype(BF16)

    return pl.pallas_call(
        body, name="rmsnorm_fwd", grid=(lp // tr,),
        in_specs=[pl.BlockSpec((tr, D), lambda i: (i, 0)), pl.BlockSpec((1, D), lambda i: (0, 0))],
        out_specs=pl.BlockSpec((tr, D), lambda i: (i, 0)),
        out_shape=jax.ShapeDtypeStruct((lp, D), BF16),
    )(h, gain)


def _loss_head(h, target, gain):
    lp = h.shape[0]
    tr = 128

    def body(h_ref, t_ref, g_ref, loss_ref, dh_ref, dhb_ref, dg_ref):
        i = pl.program_id(0)

        @pl.when(i == 0)
        def _():
            loss_ref[...] = jnp.zeros_like(loss_ref)
            dg_ref[...] = jnp.zeros_like(dg_ref)
            dh_ref[...] = jnp.zeros_like(dh_ref)
            dhb_ref[...] = jnp.zeros_like(dhb_ref)

        @pl.when(i != 0)
        def _():
            x = h_ref[...]
            g = g_ref[...]
            r = lax.rsqrt(jnp.mean(x * x, axis=-1, keepdims=True) + EPS)
            xh = x * r
            y = xh * g
            err = y - t_ref[...]
            loss_ref[...] += 0.5 * jnp.sum(jnp.sum(err * err, axis=-1, keepdims=True), axis=0, keepdims=True) / D
            dy = err * (1.0 / D)
            dg_ref[...] += jnp.sum(dy * xh, axis=0, keepdims=True)
            dxh = dy * g
            dx = r * (dxh - xh * jnp.mean(dxh * xh, axis=-1, keepdims=True))
            dh_ref[...] = dx
            dhb_ref[...] = dx.astype(BF16)

    row = pl.BlockSpec((tr, D), lambda i: (i, 0))
    vec = pl.BlockSpec((1, D), lambda i: (0, 0))
    return pl.pallas_call(
        body, name="loss_head", grid=(lp // tr,),
        in_specs=[row, pl.BlockSpec((tr, D), lambda i: (jnp.maximum(i - 1, 0), 0)), vec],
        out_specs=[pl.BlockSpec((1, 1), lambda i: (0, 0)), row, row, vec],
        out_shape=[jax.ShapeDtypeStruct((1, 1), F32), jax.ShapeDtypeStruct((lp, D), F32),
                   jax.ShapeDtypeStruct((lp, D), BF16), jax.ShapeDtypeStruct((1, D), F32)],
    )(h, target, gain)


def _silu_parts(g):
    s = _sigmoid(g)
    return g * s, s * (1.0 + g * (1.0 - s))


def _residual_norm_epi(scale, with_norm):
    def epi(accs, exs, row0):
        h2 = exs[0] + scale * accs[0]
        if not with_norm:
            return [h2]
        r = lax.rsqrt(jnp.mean(h2 * h2, axis=-1, keepdims=True) + EPS)
        return [h2, h2 * r * exs[1]]
    return epi


def _norm_bwd_epi(accs, exs, row0):
    dz = accs[0]
    x, res, g = exs
    r = lax.rsqrt(jnp.mean(x * x, axis=-1, keepdims=True) + EPS)
    xh = x * r
    dxh = dz * g
    dx = r * (dxh - xh * jnp.mean(dxh * xh, axis=-1, keepdims=True))
    rows = row0 + lax.broadcasted_iota(jnp.int32, (dz.shape[0], 1), 0)
    dh = jnp.where(rows >= NULL, res + dx, 0.0)
    return [dh, dh, jnp.sum(dz * xh, axis=0, keepdims=True)]


def _norm_bwd_outs(lp):
    return [((lp, D), F32, "mn", 0, ()), ((lp, D), BF16, "mn", 0, ()), ((1, D), F32, "nsum", 0, ())]


def _ffn_fwd(h, z, wg_t, wu_t, wd, pre, next_gain):
    lp = h.shape[0]
    tm = _tm(lp)

    def up_epi(accs, exs, row0):
        g, u = accs
        sg, _ = _silu_parts(g)
        return [g, u, sg * u]

    bshape = (lp, DFF)
    g_, u_, act = _mm("ffn_up", lp, DFF, [_term(z, wg_t, "nt", 0, (), pre), _term(z, wu_t, "nt", 1, (), pre)],
                      [(bshape, BF16, "mn", 0, ())] * 3, up_epi, tm=tm, tn=DFF, nsub=DFF // MXU_N, vmem=VMEM_BIG)

    with_norm = next_gain is not None
    res = _mm("ffn_down", lp, D, [_term(act, wd, "nn", 0, (), pre)],
              [((lp, D), F32, "mn", 0, ())] + ([((lp, D), BF16, "mn", 0, ())] if with_norm else []),
              _residual_norm_epi(0.5, with_norm),
              extras=[(h, "mn", 0, ())] + ([(next_gain, "n", 0, ())] if with_norm else []),
              tm=tm, tn=D, i_outer=True, vmem=VMEM_BIG)
    return res[0], (res[1] if with_norm else None), dict(h=h, z=z, g=g_, u=u_, act=act)


def _dw(name, a, b, m, n, scale=1.0):
    lp = a.shape[0]
    tm = _pick(m, (2944, 1408, 1024, 512, 256, 128))
    tn = _pick(n, (1024, 512, 256, 128)) if tm <= 1408 else _pick(n, (512, 256, 128))
    nk = lp // _pick(lp, (2112, 256, 128))

    def epi(accs, exs, row0):
        return [accs[0] * scale]

    (w,) = _mm(name, m, n, [_term(a, b, "tn")], [((m, n), BF16, "mn", 0, ())], epi, tm=tm, tn=tn, nk=nk,
               i_outer=True, vmem=VMEM_BIG)
    return w


def _ffn_bwd(dh, dhb, sv, gain, wg_t, wu_t, wd, pre):
    lp = dh.shape[0]
    tm = _tm(lp)

    def dact_epi(accs, exs, row0):
        g = exs[0].astype(F32)
        u = exs[1].astype(F32)
        da = 0.5 * accs[0]
        sg, dsg = _silu_parts(g)
        return [da * u * dsg, da * sg]

    dg_, du_ = _mm("ffn_dact", lp, DFF, [_term(dhb, wd, "nt", 0, (), pre)],
                   [((lp, DFF), BF16, "mn", 0, ())] * 2, dact_epi,
                   extras=[(sv["g"], "mn", 0, ()), (sv["u"], "mn", 0, ())], tm=tm, tn=DFF, nsub=DFF // MXU_N,
                   vmem=VMEM_BIG)
    d_wd = _dw("dw_down", sv["act"], dhb, DFF, D, 0.5)
    d_wg = _dw("dw_gate", dg_, sv["z"], DFF, D)
    d_wu = _dw("dw_up", du_, sv["z"], DFF, D)

    nk = 1
    dh2, dhb2, dgain = _mm("ffn_dz", lp, D, [_term(dg_, wg_t, "nn", 0, (), pre), _term(du_, wu_t, "nn", 0, (), pre)],
                           _norm_bwd_outs(lp), _norm_bwd_epi,
                           extras=[(sv["h"], "mn", 0, ()), (dh, "mn", 0, ()), (gain, "n", 0, ())],
                           tm=tm, tn=D, nk=nk, vmem=VMEM_BIG)
    return dh2, dhb2, dgain, d_wg, d_wu, d_wd


def _gla_gates(hin, w2p, b2p):
    lp = hin.shape[0]
    tr = _row_tile(lp)
    bf, bb = P_OFF["lrf"] // HP, P_OFF["lrb"] // HP

    def body(lf_ref, lb_ref, w_ref, b_ref, o_ref, c_ref):
        i = pl.program_id(0)
        rows = i * tr + lax.broadcasted_iota(jnp.int32, (tr, 1), 0)
        r = lax.broadcasted_iota(jnp.int32, (tr, tr), 0)
        c = lax.broadcasted_iota(jnp.int32, (tr, tr), 1)
        same = (r // CHUNK) == (c // CHUNK)
        for d, l_ref in enumerate((lf_ref, lb_ref)):
            logit = _dot3(l_ref[...].astype(F32), w_ref[d]) + b_ref[d]
            g = jnp.where(rows >= NULL, jax.nn.log_sigmoid(logit) * (1.0 / GLA_TAU), 0.0)
            o_ref[d] = g
            tmat = jnp.where(same & ((r >= c) if d == 0 else (r <= c)), 1.0, 0.0).astype(BF16)
            c_ref[d] = _dot_sel(tmat, g)

    spec = pl.BlockSpec((2, tr, 512), lambda i: (0, i, 0))
    return pl.pallas_call(
        body, name="gla_gates", grid=(lp // tr,),
        in_specs=[pl.BlockSpec((tr, HP), lambda i: (i, bf)), pl.BlockSpec((tr, HP), lambda i: (i, bb)),
                  pl.BlockSpec((2, HP, 512), lambda i: (0, 0, 0)), pl.BlockSpec((2, 1, 512), lambda i: (0, 0, 0))],
        out_specs=[spec, spec],
        out_shape=[jax.ShapeDtypeStruct((2, lp, 512), F32)] * 2,
    )(hin, hin, w2p, b2p)


def _gla_rows(lp):
    return _pick(lp, (384, 256, 128))


def _tri(d):
    r = lax.broadcasted_iota(jnp.int32, (CHUNK, CHUNK), 0)
    c = lax.broadcasted_iota(jnp.int32, (CHUNK, CHUNK), 1)
    return (r >= c) if d == 0 else (r <= c)


def _gla_fwd(hin, gates):
    lp = hin.shape[0]
    rb = _gla_rows(lp)
    nb = lp // rb
    cpb = rb // CHUNK
    nchunk = lp // CHUNK
    qo, ko, vo = P_OFF["qa"] // 512, P_OFF["ka"] // 512, P_OFF["va"] // 512
    scale = GLA_DK ** -0.5

    def body(qf, kf, vf, gf, qb, kb, vb_, gb, of, ob, sf, sb, st):
        @pl.when(pl.program_id(0) == 0)
        def _():
            st[...] = jnp.zeros_like(st)

        ins = ((qf, kf, vf, gf, of, sf), (qb, kb, vb_, gb, ob, sb))
        for ci in range(cpb):
            for d in range(2):
                q_ref, k_ref, v_ref, g_ref, o_ref, s_ref = ins[d]
                tri = _tri(d)
                c = ci if d == 0 else cpb - 1 - ci
                rows = slice(c * CHUNK, (c + 1) * CHUNK)
                for h in range(GLA_H):
                    sl = slice(h * HP, (h + 1) * HP)
                    q = q_ref[rows, sl].astype(F32) * scale
                    k = k_ref[rows, sl].astype(F32)
                    v = v_ref[rows, sl].astype(F32)
                    b = g_ref[rows, sl]
                    btot = b[CHUNK - 1:CHUNK] if d == 0 else b[0:1]
                    qd = (q * jnp.exp(b)).astype(BF16)
                    ki = (k * jnp.exp(-b)).astype(BF16)
                    ke = (k * jnp.exp(btot - b)).astype(BF16)
                    vb = v.astype(BF16)
                    att = jnp.where(tri, _dot(qd, ki, "nt"), 0.0)
                    s_prev = st[d, h]
                    o_ref[rows, sl] = _dot(att.astype(BF16), vb) + _dot(qd, s_prev.astype(BF16), "nt")
                    s_ref[h, c] = s_prev
                    st[d, h] = s_prev * jnp.exp(btot) + _dot(vb, ke, "tn")

    def specs(off):
        return (pl.BlockSpec((rb, 512), lambda b: (b, off)), pl.BlockSpec((rb, 512), lambda b: (nb - 1 - b, off)))

    (qf, qb), (kf, kb), (vf, vb2) = specs(qo), specs(ko), specs(vo)
    gf = pl.BlockSpec((None, rb, 512), lambda b: (0, b, 0))
    gb = pl.BlockSpec((None, rb, 512), lambda b: (1, nb - 1 - b, 0))
    of, ob = specs(0)
    sf = pl.BlockSpec((GLA_H, cpb, HP, HP), lambda b: (0, b, 0, 0))
    sb = pl.BlockSpec((GLA_H, cpb, HP, HP), lambda b: (0, nb - 1 - b, 0, 0))
    osh = jax.ShapeDtypeStruct((lp, GLA_H * HP), F32)
    ssh = jax.ShapeDtypeStruct((GLA_H, nchunk, HP, HP), F32)
    return pl.pallas_call(
        body, name="gla_fwd", grid=(nb,),
        in_specs=[qf, kf, vf, gf, qb, kb, vb2, gb], out_specs=[of, ob, sf, sb], out_shape=[osh, osh, ssh, ssh],
        scratch_shapes=[pltpu.VMEM((2, GLA_H, HP, HP), F32)], compiler_params=_cparams(VMEM_BIG),
    )(hin, hin, hin, gates, hin, hin, hin, gates)


def _gla_bwd(hin, gates, states, do):
    lp = hin.shape[0]
    rb = _gla_rows(lp)
    nb = lp // rb
    cpb = rb // CHUNK
    qo, ko, vo = P_OFF["qa"] // 512, P_OFF["ka"] // 512, P_OFF["va"] // 512
    scale = GLA_DK ** -0.5

    def body(qf, kf, vf, gf, sf, dof, qb, kb, vb_, gb, sb, dob,
             dqf, dkf, dvf, dgf, dqb, dkb, dvb, dgb, dst):
        @pl.when(pl.program_id(0) == 0)
        def _():
            dst[...] = jnp.zeros_like(dst)

        ins = ((qf, kf, vf, gf, sf, dof, dqf, dkf, dvf, dgf), (qb, kb, vb_, gb, sb, dob, dqb, dkb, dvb, dgb))
        for ci in range(cpb):
            for d in range(2):
                q_ref, k_ref, v_ref, g_ref, s_ref, do_ref, dq_ref, dk_ref, dv_ref, dg_ref = ins[d]
                tri, tri_t = _tri(d), _tri(1 - d)
                edge = lax.broadcasted_iota(jnp.int32, (CHUNK, 1), 0) == (CHUNK - 1 if d == 0 else 0)
                c = cpb - 1 - ci if d == 0 else ci
                rows = slice(c * CHUNK, (c + 1) * CHUNK)
                for h in range(GLA_H):
                    sl = slice(h * HP, (h + 1) * HP)
                    q = q_ref[rows, sl].astype(F32) * scale
                    k = k_ref[rows, sl].astype(F32)
                    v = v_ref[rows, sl].astype(F32)
                    dout = do_ref[rows, sl].astype(BF16)
                    b = g_ref[rows, sl]
                    btot = b[CHUNK - 1:CHUNK] if d == 0 else b[0:1]
                    e = jnp.exp(b)
                    ei = jnp.exp(-b)
                    et = jnp.exp(btot - b)
                    etot = jnp.exp(btot)
                    qd = q * e
                    ki = k * ei
                    ke = k * et
                    qdb, kib, keb, vb = qd.astype(BF16), ki.astype(BF16), ke.astype(BF16), v.astype(BF16)
                    att_t = jnp.where(tri_t, _dot(kib, qdb, "nt"), 0.0).astype(BF16)
                    d_att = jnp.where(tri, _dot(dout, vb, "nt"), 0.0).astype(BF16)
                    d_att_t = jnp.where(tri_t, _dot(vb, dout, "nt"), 0.0).astype(BF16)
                    s_prev = s_ref[h, c]
                    ds_t = dst[d, h]
                    ds_b = ds_t.astype(BF16)
                    dv = _dot(att_t, dout) + _dot(keb, ds_b, "nt")
                    d_qd = _dot(d_att, kib) + _dot(dout, s_prev.astype(BF16))
                    d_ki = _dot(d_att_t, qdb)
                    d_ke = _dot(vb, ds_b)
                    d_e = jnp.sum(s_prev * ds_t, axis=0, keepdims=True)
                    dst[d, h] = _dot(dout, qdb, "tn") + ds_t * etot
                    db = d_qd * qd - d_ki * ki - d_ke * ke
                    dbtot = jnp.sum(d_ke * ke, axis=0, keepdims=True) + d_e * etot
                    dq_ref[rows, sl] = (d_qd * e * scale).astype(BF16)
                    dk_ref[rows, sl] = (d_ki * ei + d_ke * et).astype(BF16)
                    dv_ref[rows, sl] = dv.astype(BF16)
                    dg_ref[rows, sl] = db + jnp.where(edge, dbtot, 0.0)

    def fw(off):
        return pl.BlockSpec((rb, 512), lambda b: (nb - 1 - b, off))

    def bw(off):
        return pl.BlockSpec((rb, 512), lambda b: (b, off))

    gf = pl.BlockSpec((None, rb, 512), lambda b: (0, nb - 1 - b, 0))
    gb = pl.BlockSpec((None, rb, 512), lambda b: (1, b, 0))
    sf = pl.BlockSpec((GLA_H, cpb, HP, HP), lambda b: (0, nb - 1 - b, 0, 0))
    sb = pl.BlockSpec((GLA_H, cpb, HP, HP), lambda b: (0, b, 0, 0))
    osh = jax.ShapeDtypeStruct((lp, GLA_H * HP), F32)
    osh_b = jax.ShapeDtypeStruct((lp, GLA_H * HP), BF16)
    res = pl.pallas_call(
        body, name="gla_bwd", grid=(nb,),
        in_specs=[fw(qo), fw(ko), fw(vo), gf, sf, fw(0), bw(qo), bw(ko), bw(vo), gb, sb, bw(0)],
        out_specs=[fw(0)] * 4 + [bw(0)] * 4, out_shape=[osh_b, osh_b, osh_b, osh] * 2,
        scratch_shapes=[pltpu.VMEM((2, GLA_H, HP, HP), F32)], compiler_params=_cparams(VMEM_BIG),
    )(hin, hin, hin, gates, states[0], do, hin, hin, hin, gates, states[1], do)
    return res[:4], res[4:]


def _gla_out_fwd(o2, hin, gn):
    lp = hin.shape[0]
    tr = _row_tile(lp)
    ro = P_OFF["ra"] // 512

    def body(of_ref, ob_ref, r_ref, gn_ref, a_ref):
        r = r_ref[...].astype(F32)
        sr, _ = _silu_parts(r)
        for h in range(GLA_H):
            sl = slice(h * HP, (h + 1) * HP)
            o = of_ref[:, sl] + ob_ref[:, sl]
            rs = lax.rsqrt(jnp.mean(o * o, axis=-1, keepdims=True) + EPS)
            a_ref[:, sl] = (o * rs * gn_ref[:, sl] * sr[:, sl]).astype(BF16)

    row = pl.BlockSpec((tr, 512), lambda i: (i, 0))
    return pl.pallas_call(
        body, name="gla_out_fwd", grid=(lp // tr,),
        in_specs=[row, row, pl.BlockSpec((tr, 512), lambda i: (i, ro)), pl.BlockSpec((1, 512), lambda i: (0, 0))],
        out_specs=row,
        out_shape=jax.ShapeDtypeStruct((lp, 512), BF16),
    )(o2[0], o2[1], hin, gn)


def _gla_out_bwd(da, o2, hin, gn):
    lp = hin.shape[0]
    tr = _row_tile(lp)
    ro = P_OFF["ra"] // 512

    def body(da_ref, of_ref, ob_ref, r_ref, gn_ref, do_ref, dr_ref, dgn_ref):
        i = pl.program_id(0)
        r = r_ref[...].astype(F32)
        sr, dsr = _silu_parts(r)
        da_v = da_ref[...]
        parts = []
        for h in range(GLA_H):
            sl = slice(h * HP, (h + 1) * HP)
            o = of_ref[:, sl] + ob_ref[:, sl]
            rs = lax.rsqrt(jnp.mean(o * o, axis=-1, keepdims=True) + EPS)
            oh = o * rs
            gn_h = gn_ref[:, sl]
            dah = da_v[:, sl]
            dr_ref[:, sl] = (dah * oh * gn_h * dsr[:, sl]).astype(BF16)
            t = dah * sr[:, sl]
            parts.append(jnp.sum(t * oh, axis=0, keepdims=True))
            doh = t * gn_h
            do_ref[:, sl] = rs * (doh - oh * jnp.mean(doh * oh, axis=-1, keepdims=True))
        part = jnp.concatenate(parts, axis=1)

        @pl.when(i == 0)
        def _():
            dgn_ref[...] = part

        @pl.when(i != 0)
        def _():
            dgn_ref[...] += part

    row = pl.BlockSpec((tr, 512), lambda i: (i, 0))
    return pl.pallas_call(
        body, name="gla_out_bwd", grid=(lp // tr,),
        in_specs=[row, row, row, pl.BlockSpec((tr, 512), lambda i: (i, ro)), pl.BlockSpec((1, 512), lambda i: (0, 0))],
        out_specs=[row, row, pl.BlockSpec((1, 512), lambda i: (0, 0))],
        out_shape=[jax.ShapeDtypeStruct((lp, 512), F32), jax.ShapeDtypeStruct((lp, 512), BF16),
                   jax.ShapeDtypeStruct((1, 512), F32)],
    )(da, o2[0], o2[1], hin, gn)


def _gla_in_bwd(gf, gb, gates, hin, w2p, others):
    lp = hin.shape[0]
    tr = _row_tile(lp)
    bf, bb = P_OFF["lrf"] // HP, P_OFF["lrb"] // HP
    names = tuple(others)

    def seg(name):
        return slice(P_OFF[name], P_OFF[name] + P_WIDTH[name])

    def body(dqf_ref, dkf_ref, dvf_ref, dgf_ref, dqb_ref, dkb_ref, dvb_ref, dgb_ref, g_ref, lf_ref, lb_ref, w_ref,
             *rest):
        other_refs, (o_ref, dw_ref, db_ref) = rest[:len(names)], rest[len(names):]
        i = pl.program_id(0)
        for n, ref in zip(names, other_refs):
            o_ref[:, seg(n)] = ref[...].astype(BF16)
        o_ref[:, seg("qa")] = (dqf_ref[...].astype(F32) + dqb_ref[...].astype(F32)).astype(BF16)
        o_ref[:, seg("ka")] = (dkf_ref[...].astype(F32) + dkb_ref[...].astype(F32)).astype(BF16)
        o_ref[:, seg("va")] = (dvf_ref[...].astype(F32) + dvb_ref[...].astype(F32)).astype(BF16)
        olr_ref = o_ref.at[:, P_OFF["lrf"]:P_OFF["lrf"] + 2 * HP]
        rows = i * tr + lax.broadcasted_iota(jnp.int32, (tr, 1), 0)
        r = lax.broadcasted_iota(jnp.int32, (tr, tr), 0)
        c = lax.broadcasted_iota(jnp.int32, (tr, tr), 1)
        same = (r // CHUNK) == (c // CHUNK)
        for d, (l_ref, dg_ref) in enumerate(((lf_ref, dgf_ref), (lb_ref, dgb_ref))):
            tmat = jnp.where(same & ((r <= c) if d == 0 else (r >= c)), 1.0, 0.0).astype(BF16)
            dg = _dot_sel(tmat, dg_ref[...])
            sig_neg = 1.0 - jnp.exp(GLA_TAU * g_ref[d])
            dlogit = jnp.where(rows >= NULL, dg * (1.0 / GLA_TAU) * sig_neg, 0.0)
            olr_ref[:, d * HP:(d + 1) * HP] = _dot3(dlogit, w_ref[d], "nt").astype(BF16)
            dw = _dot3(l_ref[...].astype(F32), dlogit, "tn")
            dbias = jnp.sum(dlogit, axis=0, keepdims=True)

            @pl.when(i == 0)
            def _():
                dw_ref[d] = dw
                db_ref[d] = dbias

            @pl.when(i != 0)
            def _():
                dw_ref[d] += dw
                db_ref[d] += dbias

    two = pl.BlockSpec((2, tr, 512), lambda i: (0, i, 0))
    row = pl.BlockSpec((tr, 512), lambda i: (i, 0))
    return pl.pallas_call(
        body, name="gla_in_bwd", grid=(lp // tr,),
        in_specs=[row] * 8 + [two, pl.BlockSpec((tr, HP), lambda i: (i, bf)),
                  pl.BlockSpec((tr, HP), lambda i: (i, bb)), pl.BlockSpec((2, HP, 512), lambda i: (0, 0, 0))] +
                 [pl.BlockSpec((tr, P_WIDTH[n]), lambda i: (i, 0)) for n in names],
        out_specs=[pl.BlockSpec((tr, D_INP), lambda i: (i, 0)),
                   pl.BlockSpec((2, HP, 512), lambda i: (0, 0, 0)), pl.BlockSpec((2, 1, 512), lambda i: (0, 0, 0))],
        out_shape=[jax.ShapeDtypeStruct((lp, D_INP), BF16), jax.ShapeDtypeStruct((2, HP, 512), F32),
                   jax.ShapeDtypeStruct((2, 1, 512), F32)],
        compiler_params=_cparams(VMEM_BIG),
    )(*gf, *gb, gates, hin, hin, w2p, *[others[n] for n in names])


def _rope_tables(lp):
    n_tok = lp - NULL - NMETA
    rows = n_tok // GRID_W
    row = np.repeat(np.arange(rows), GRID_W).astype(np.float32)
    col = np.tile(np.arange(GRID_W), rows).astype(np.float32)
    inv = (ROPE_THETA ** (-np.arange(0, 32, 2, dtype=np.float32) / 32)).astype(np.float32)
    ang = np.concatenate([row[:, None] * inv, col[:, None] * inv], axis=-1)
    ang = np.concatenate([np.zeros((NULL + NMETA, 32), np.float32), ang], axis=0)
    cos, sin = np.cos(ang).astype(np.float32), np.sin(ang).astype(np.float32)
    z16 = np.zeros((lp, 16), np.float32)
    z64 = np.zeros((lp, 64), np.float32)
    c = np.concatenate([cos[:, :16], cos[:, :16], cos[:, 16:], cos[:, 16:], z64], axis=1)
    a = np.concatenate([-sin[:, :16], z16, -sin[:, 16:], z16, z64], axis=1)
    b = np.concatenate([z16, sin[:, :16], z16, sin[:, 16:], z64], axis=1)
    return jnp.asarray(c), jnp.asarray(a), jnp.asarray(b)


def _rope(x, c, a, b):
    return x * c + pltpu.roll(x, HP - 16, 1) * a + pltpu.roll(x, 16, 1) * b


def _rope_t(dx, c, a, b):
    return dx * c + pltpu.roll(dx * a, 16, 1) + pltpu.roll(dx * b, HP - 16, 1)


def _attn_prep(hin, gq, gk, tabs):
    lp = hin.shape[0]
    tr = _row_tile(lp)
    qo, ko, vo = P_OFF["qb"] // 1024, P_OFF["kb"] // 256, P_OFF["vb"] // 256

    def body(q_ref, k_ref, v_ref, gq_ref, gk_ref, c_ref, a_ref, b_ref, oq_ref, ok_ref, ov_ref):
        c, a, b = c_ref[...], a_ref[...], b_ref[...]
        for src, g_ref, dst, nh, sc in ((q_ref, gq_ref, oq_ref, ATT_H, Q_SCALE), (k_ref, gk_ref, ok_ref, ATT_KV, 1.0)):
            for h in range(nh):
                sl = slice(h * HP, (h + 1) * HP)
                x = src[:, sl].astype(F32)
                r = lax.rsqrt(jnp.sum(x * x, axis=-1, keepdims=True) * (1.0 / HEAD_DIM) + EPS)
                dst[:, sl] = (_rope(x * r * g_ref[...], c, a, b) * sc).astype(BF16)
        lane = lax.broadcasted_iota(jnp.int32, (1, ATT_KV * HP), 1)
        ov_ref[...] = jnp.where(lane % HP == HEAD_DIM, 1.0, v_ref[...]).astype(BF16)

    tab = pl.BlockSpec((tr, HP), lambda i: (i, 0))
    vec = pl.BlockSpec((1, HP), lambda i: (0, 0))
    return pl.pallas_call(
        body, name="attn_prep", grid=(lp // tr,),
        in_specs=[pl.BlockSpec((tr, 1024), lambda i: (i, qo)), pl.BlockSpec((tr, 256), lambda i: (i, ko)),
                  pl.BlockSpec((tr, 256), lambda i: (i, vo)), vec, vec, tab, tab, tab],
        out_specs=[pl.BlockSpec((tr, 1024), lambda i: (i, 0)), pl.BlockSpec((tr, 256), lambda i: (i, 0)),
                   pl.BlockSpec((tr, 256), lambda i: (i, 0))],
        out_shape=[jax.ShapeDtypeStruct((lp, 1024), BF16), jax.ShapeDtypeStruct((lp, 256), BF16),
                   jax.ShapeDtypeStruct((lp, 256), BF16)],
    )(hin, hin, hin, gq, gk, *tabs)


def _attn_prep_bwd(dqr, dkr, hin, gq, gk, tabs):
    lp = hin.shape[0]
    tr = _row_tile(lp)
    qo, ko = P_OFF["qb"] // 1024, P_OFF["kb"] // 256

    def body(dq_ref, dk_ref, q_ref, k_ref, gq_ref, gk_ref, c_ref, a_ref, b_ref, oq_ref, ok_ref, dgq_ref, dgk_ref):
        i = pl.program_id(0)
        c, a, b = c_ref[...], a_ref[...], b_ref[...]
        for src, dsrc, g_ref, dst, dg_ref, nh, sc in (
                (q_ref, dq_ref, gq_ref, oq_ref, dgq_ref, ATT_H, Q_SCALE),
                (k_ref, dk_ref, gk_ref, ok_ref, dgk_ref, ATT_KV, 1.0)):
            acc = jnp.zeros((1, HP), F32)
            for h in range(nh):
                sl = slice(h * HP, (h + 1) * HP)
                x = src[:, sl].astype(F32)
                r = lax.rsqrt(jnp.sum(x * x, axis=-1, keepdims=True) * (1.0 / HEAD_DIM) + EPS)
                xh = x * r
                dxn = _rope_t(dsrc[:, sl] * sc, c, a, b)
                acc = acc + jnp.sum(dxn * xh, axis=0, keepdims=True)
                dxh = dxn * g_ref[...]
                dx = r * (dxh - xh * (jnp.sum(dxh * xh, axis=-1, keepdims=True) * (1.0 / HEAD_DIM)))
                dst[:, sl] = dx.astype(BF16)

            @pl.when(i == 0)
            def _():
                dg_ref[...] = acc

            @pl.when(i != 0)
            def _():
                dg_ref[...] += acc

    tab = pl.BlockSpec((tr, HP), lambda i: (i, 0))
    vec = pl.BlockSpec((1, HP), lambda i: (0, 0))
    return pl.pallas_call(
        body, name="attn_prep_bwd", grid=(lp // tr,),
        in_specs=[pl.BlockSpec((tr, 1024), lambda i: (i, 0)), pl.BlockSpec((tr, 256), lambda i: (i, 0)),
                  pl.BlockSpec((tr, 1024), lambda i: (i, qo)), pl.BlockSpec((tr, 256), lambda i: (i, ko)),
                  vec, vec, tab, tab, tab],
        out_specs=[pl.BlockSpec((tr, 1024), lambda i: (i, 0)), pl.BlockSpec((tr, 256), lambda i: (i, 0)), vec, vec],
        out_shape=[jax.ShapeDtypeStruct((lp, 1024), BF16), jax.ShapeDtypeStruct((lp, 256), BF16),
                   jax.ShapeDtypeStruct((1, HP), F32), jax.ShapeDtypeStruct((1, HP), F32)],
    )(dqr, dkr, hin, hin, gq, gk, *tabs)


QB = 128
GH = 2
Q_SCALE = HEAD_DIM ** -0.5 * math.log2(math.e)
LN2 = math.log(2.0)


def _stack(ref, g0, n):
    return jnp.concatenate([ref[:, (g0 + g) * HP:(g0 + g + 1) * HP] for g in range(n)], axis=0)


def _attn_fwd(qr, kr, vb):
    lp = qr.shape[0]
    nq = lp // QB

    def body(q_ref, k_ref, v_ref, o_ref, lse_ref):
        qb = pl.program_id(1)
        keys = lax.broadcasted_iota(jnp.int32, (1, lp), 1)
        lane = lax.broadcasted_iota(jnp.int32, (1, HP), 1)
        rows = qb * QB + lax.broadcasted_iota(jnp.int32, (QB, 1), 0)
        for ch in range(ATT_G // GH):
            qs = _stack(q_ref, ch * GH, GH)
            s = _dot(qs, k_ref[...], "nt")
            s = jnp.where(keys >= NULL, s, -1e30)
            m = jnp.max(s, axis=-1, keepdims=True)
            p = jnp.exp2(s - m).astype(BF16)
            o_raw = _dot(p, v_ref[...])
            l = jnp.sum(jnp.where(lane == HEAD_DIM, o_raw, 0.0), axis=-1, keepdims=True)
            o = jnp.where(lane < HEAD_DIM, o_raw / l, 0.0)
            lse = m + jnp.log2(l)
            for g in range(GH):
                sl = slice((ch * GH + g) * HP, (ch * GH + g + 1) * HP)
                o_ref[:, sl] = jnp.where(rows >= NULL, o[g * QB:(g + 1) * QB], 0.0).astype(BF16)
                lse_ref[:, sl] = jnp.broadcast_to(lse[g * QB:(g + 1) * QB], (QB, HP))

    qspec = pl.BlockSpec((QB, ATT_G * HP), lambda kv, qb: (qb, kv))
    kspec = pl.BlockSpec((lp, HP), lambda kv, qb: (0, kv))
    return pl.pallas_call(
        body, name="attn_fwd", grid=(ATT_KV, nq),
        in_specs=[qspec, kspec, kspec], out_specs=[qspec, qspec],
        out_shape=[jax.ShapeDtypeStruct((lp, ATT_H * HP), BF16), jax.ShapeDtypeStruct((lp, ATT_H * HP), F32)],
        compiler_params=_cparams(VMEM_BIG),
    )(qr, kr, vb)


def _attn_bwd(qr, kr, vb, o, lse, do):
    lp = qr.shape[0]
    nq = lp // QB

    def body(q_ref, k_ref, v_ref, o_ref, lse_ref, do_ref, dq_ref, dk_ref, dv_ref):
        qb = pl.program_id(1)

        @pl.when(qb == 0)
        def _():
            dk_ref[...] = jnp.zeros_like(dk_ref)
            dv_ref[...] = jnp.zeros_like(dv_ref)

        keys = lax.broadcasted_iota(jnp.int32, (1, lp), 1)
        k = k_ref[...]
        dk_acc, dv_acc = None, None
        for ch in range(ATT_G // GH):
            g0 = ch * GH
            qs = _stack(q_ref, g0, GH)
            dos = _stack(do_ref, g0, GH)
            os_ = _stack(o_ref, g0, GH).astype(F32)
            lse_s = jnp.concatenate([lse_ref[:, (g0 + g) * HP:(g0 + g) * HP + 1] for g in range(GH)], axis=0)
            delta = jnp.sum(dos * os_, axis=-1, keepdims=True) * LN2
            s = _dot(qs, k, "nt")
            p = jnp.where(keys >= NULL, jnp.exp2(s - lse_s), 0.0)
            dob = dos.astype(BF16)
            dp = _dot((dos * LN2).astype(BF16), v_ref[...], "nt")
            ds = (p * (dp - delta)).astype(BF16)
            dq = _dot(ds, k)
            for g in range(GH):
                dq_ref[:, (g0 + g) * HP:(g0 + g + 1) * HP] = dq[g * QB:(g + 1) * QB]
            dv_c = _dot(p.astype(BF16), dob, "tn")
            dk_c = _dot(ds, qs, "tn")
            dv_acc = dv_c if dv_acc is None else dv_acc + dv_c
            dk_acc = dk_c if dk_acc is None else dk_acc + dk_c
        dv_ref[...] += dv_acc
        dk_ref[...] += dk_acc

    qspec = pl.BlockSpec((QB, ATT_G * HP), lambda kv, qb: (qb, kv))
    kspec = pl.BlockSpec((lp, HP), lambda kv, qb: (0, kv))
    return pl.pallas_call(
        body, name="attn_bwd", grid=(ATT_KV, nq),
        in_specs=[qspec, kspec, kspec, qspec, qspec, qspec], out_specs=[qspec, kspec, kspec],
        out_shape=[jax.ShapeDtypeStruct((lp, ATT_H * HP), F32), jax.ShapeDtypeStruct((lp, ATT_KV * HP), F32),
                   jax.ShapeDtypeStruct((lp, ATT_KV * HP), F32)],
        compiler_params=_cparams(VMEM_BIG),
    )(qr, kr, vb, o, lse, do)


def _mixer_fwd(h, z, wl, l, tabs, next_gain):
    lp = h.shape[0]
    tm = _tm(lp)

    def id_epi(accs, exs, row0):
        return [accs[0]]

    (hin,) = _mm("in_proj", lp, D_INP, [_term(z, wl["win_t"], "nt", 0, (), (l,))], [((lp, D_INP), BF16, "mn", 0, ())],
                 id_epi, tm=_tm_wide(lp), tn=D_INP // 2, vmem=VMEM_BIG)
    gates, cum = _gla_gates(hin, wl["w2p"][l], wl["b2p"][l])
    o_f, o_b, s_f, s_b = _gla_fwd(hin, cum)
    o2, states = (o_f, o_b), (s_f, s_b)
    a = _gla_out_fwd(o2, hin, wl["gn"][l])
    qr, kr, vb = _attn_prep(hin, wl["gq"][l], wl["gk"][l], tabs)
    b, lse = _attn_fwd(qr, kr, vb)

    def merge_epi(accs, exs, row0):
        pa, pb = accs
        ga, gb, bma, bmb = exs
        y = _sigmoid(ga + bma) * pa + _sigmoid(gb + bmb) * pb
        return [y, pa, pb]

    y, pa, pb = _mm("merge", lp, D, [_term(a, wl["wpa_t"], "nt", 0, (), (l,)), _term(b, wl["wpb_t"], "nt", 1, (), (l,))],
                    [((lp, D), BF16, "mn", 0, ())] * 3, merge_epi,
                    extras=[(hin, "mn", P_OFF["ga"] // D, ()), (hin, "mn", P_OFF["gb"] // D, ()),
                            (wl["bm"], "n", 0, (l, 0)), (wl["bm"], "n", 0, (l, 1))],
                    tm=_tm_wide(lp), tn=D, nsub=D // MXU_N, i_outer=True, vmem=VMEM_BIG)

    h2, z2 = _mm("out_proj", lp, D, [_term(y, wl["wout"], "nn", 0, (), (l,))],
                 [((lp, D), F32, "mn", 0, ()), ((lp, D), BF16, "mn", 0, ())], _residual_norm_epi(1.0, True),
                 extras=[(h, "mn", 0, ()), (next_gain, "n", 0, ())], tm=_tm_wide(lp), tn=D, i_outer=True, vmem=VMEM_BIG)
    sv = dict(h=h, z=z, hin=hin, gates=gates, cum=cum, o2=o2, states=states, a=a, qr=qr, kr=kr, vb=vb, b=b, lse=lse,
              y=y, pa=pa, pb=pb)
    return h2, z2, sv


def _mixer_bwd(dh, dhb, sv, gain, wl, l, tabs):
    lp = dh.shape[0]
    tm = _tm(lp)
    hin = sv["hin"]

    def merge_bwd_epi(accs, exs, row0):
        dy = accs[0]
        ga, gb, pa, pb, bma, bmb = exs
        sa = _sigmoid(ga + bma)
        sb = _sigmoid(gb + bmb)
        dga = dy * pa.astype(F32) * sa * (1.0 - sa)
        dgb = dy * pb.astype(F32) * sb * (1.0 - sb)
        return [dy * sa, dy * sb, dga, dgb, jnp.sum(dga, axis=0, keepdims=True), jnp.sum(dgb, axis=0, keepdims=True)]

    big = ((lp, D), BF16, "mn", 0, ())
    vec = ((1, D), F32, "nsum", 0, ())
    dpa, dpb, dga, dgb, dbma, dbmb = _mm(
        "merge_bwd", lp, D, [_term(dhb, wl["wout"], "nt", 0, (), (l,))], [big, big, big, big, vec, vec], merge_bwd_epi,
        extras=[(hin, "mn", P_OFF["ga"] // D, ()), (hin, "mn", P_OFF["gb"] // D, ()), (sv["pa"], "mn", 0, ()),
                (sv["pb"], "mn", 0, ()), (wl["bm"], "n", 0, (l, 0)), (wl["bm"], "n", 0, (l, 1))],
        tm=tm, tn=D, nsub=D // MXU_N, vmem=VMEM_BIG)
    d_wout = _dw("dw_out", sv["y"], dhb, D, D)
    d_wpa_t = _dw("dw_pa", dpa, sv["a"], D, 512)
    d_wpb_t = _dw("dw_pb", dpb, sv["b"], D, ATT_H * HP)

    def id_epi(accs, exs, row0):
        return [accs[0]]

    (da,) = _mm("d_a", lp, 512, [_term(dpa, wl["wpa_t"], "nn", 0, (), (l,))], [((lp, 512), F32, "mn", 0, ())], id_epi,
                tm=_tm_wide(lp), tn=512, i_outer=True, vmem=VMEM_BIG)
    (db,) = _mm("d_b", lp, ATT_H * HP, [_term(dpb, wl["wpb_t"], "nn", 0, (), (l,))],
                [((lp, ATT_H * HP), F32, "mn", 0, ())], id_epi, tm=_tm_wide(lp), tn=512, i_outer=True, vmem=VMEM_BIG)
    d_o, d_ra, d_gn = _gla_out_bwd(da, sv["o2"], hin, wl["gn"][l])
    g_fw, g_bw = _gla_bwd(hin, sv["cum"], sv["states"], d_o)
    dqr, dkr, dvb = _attn_bwd(sv["qr"], sv["kr"], sv["vb"], sv["b"], sv["lse"], db)
    d_qb, d_kb, d_gq, d_gk = _attn_prep_bwd(dqr, dkr, hin, wl["gq"][l], wl["gk"][l], tabs)
    dhin, d_w2p, d_b2p = _gla_in_bwd(g_fw, g_bw, sv["gates"], hin, wl["w2p"][l],
                                     dict(qb=d_qb, ga=dga, gb=dgb, ra=d_ra, kb=d_kb, vb=dvb))
    d_win_t = _dw("dw_in", dhin, sv["z"], D_INP, D)
    dh2, dhb2, dgain = _mm("in_proj_dz", lp, D, [_term(dhin, wl["win_t"], "nn", 0, (), (l,))], _norm_bwd_outs(lp),
                           _norm_bwd_epi, extras=[(sv["h"], "mn", 0, ()), (dh, "mn", 0, ()), (gain, "n", 0, ())],
                           tm=tm, tn=D, nk=1, vmem=VMEM_BIG)
    grads = dict(gain=dgain, wout=d_wout, wpa_t=d_wpa_t, wpb_t=d_wpb_t, win_t=d_win_t, gn=d_gn, w2p=d_w2p, b2p=d_b2p,
                 gq=d_gq, gk=d_gk, bma=dbma, bmb=dbmb)
    return dh2, dhb2, grads


def _mesh_pos():
    x, y, c = lax.axis_index("x"), lax.axis_index("y"), lax.axis_index("c")
    chips = [(1 - x, y), (x, 1 - y), (1 - x, 1 - y)]
    return x, y, c, chips


def _dev_index(x, y, c):
    return 4 * x + 2 * y + c


def _all_gather(name, shards, leads):
    nt = len(shards)

    def blk(ref, lead, idx):
        return ref.at[(slice(None),) * lead + (idx,)]

    def body(*refs):
        xs, outs = refs[:nt], refs[nt:2 * nt]
        send_sems, recv_sems, local_sems = refs[2 * nt:]
        x, y, c, chips = _mesh_pos()
        me, sibling = (x, y, c), (x, y, 1 - c)

        def copy(t, k, block, to, own=False):
            dst = blk(outs[t], leads[t], _dev_index(*block))
            return pltpu.make_async_remote_copy(
                src_ref=xs[t] if own else dst, dst_ref=dst, send_sem=send_sems.at[t, k], recv_sem=recv_sems.at[t, k],
                device_id=to, device_id_type=MESH)

        locals_ = [pltpu.make_async_copy(xs[t], blk(outs[t], leads[t], _dev_index(*me)), local_sems.at[t])
                   for t in range(nt)]
        for cp in locals_:
            cp.start()
        first = []
        for t in range(nt):
            first.append(copy(t, 0, me, sibling, own=True))
            first += [copy(t, 1 + j, me, (*chip, c), own=True) for j, chip in enumerate(chips)]
        for cp in first:
            cp.start()
        passed = []
        for j, chip in enumerate(chips):
            for t in range(nt):
                copy(t, 1 + j, (*chip, c), me).wait_recv()
                fw = copy(t, 4 + j, (*chip, c), sibling)
                fw.start()
                passed.append(fw)
        for t in range(nt):
            copy(t, 0, sibling, me).wait_recv()
        for j, chip in enumerate(chips):
            for t in range(nt):
                copy(t, 4 + j, (*chip, 1 - c), me).wait_recv()
        for cp in first + passed:
            cp.wait_send()
        for cp in locals_:
            cp.wait()

    out_shape = [jax.ShapeDtypeStruct(s.shape[:ld] + (NDEV,) + s.shape[ld:], s.dtype) for s, ld in zip(shards, leads)]
    return pl.pallas_call(
        body, name=name, in_specs=[ANY] * nt, out_specs=[ANY] * nt, out_shape=out_shape,
        scratch_shapes=[pltpu.SemaphoreType.DMA((nt, 7)), pltpu.SemaphoreType.DMA((nt, 7)),
                        pltpu.SemaphoreType.DMA((nt,))],
    )(*shards)


def _exchange_sibling(name, gs):
    nt = len(gs)

    def body(*refs):
        xs, outs = refs[:nt], refs[nt:2 * nt]
        send_sems, recv_sems = refs[2 * nt:]
        x, y, c, _ = _mesh_pos()
        sibling = (x, y, 1 - c)
        copies = []
        for t in range(nt):
            for ch in range(4):
                copies.append(pltpu.make_async_remote_copy(
                    src_ref=xs[t].at[2 * ch + (1 - c)], dst_ref=outs[t].at[ch],
                    send_sem=send_sems.at[t, ch], recv_sem=recv_sems.at[t, ch],
                    device_id=sibling, device_id_type=MESH))
        for cp in copies:
            cp.start()
        for cp in copies:
            cp.wait()

    out_shape = [jax.ShapeDtypeStruct((4,) + g.shape[1:], g.dtype) for g in gs]
    return pl.pallas_call(
        body, name=name, in_specs=[ANY] * nt, out_specs=[ANY] * nt, out_shape=out_shape,
        scratch_shapes=[pltpu.SemaphoreType.DMA((nt, 4)), pltpu.SemaphoreType.DMA((nt, 4))],
    )(*gs)


def _pair_sum(name, gs, recv):
    c = lax.axis_index("c")
    outs = []
    for t, (g, rv) in enumerate(zip(gs, recv)):
        _, r, cols = rv.shape

        def body(c_ref, g_ref, r_ref, o_ref):
            o_ref[...] = (g_ref[...].astype(F32) + r_ref[...].astype(F32)).astype(o_ref.dtype)

        outs.append(pl.pallas_call(
            body, name=f"{name}_{t}",
            grid_spec=pltpu.PrefetchScalarGridSpec(
                num_scalar_prefetch=1, grid=(4,),
                in_specs=[pl.BlockSpec((None, r, cols), lambda ch, cr: (2 * ch + cr[0], 0, 0)),
                          pl.BlockSpec((None, r, cols), lambda ch, cr: (ch, 0, 0))],
                out_specs=pl.BlockSpec((None, r, cols), lambda ch, cr: (ch, 0, 0))),
            out_shape=jax.ShapeDtypeStruct(rv.shape, rv.dtype),
        )(jnp.reshape(c, (1,)).astype(jnp.int32), g, rv))
    return outs


def _final_sum(name, ps, recv, transposed):
    chip = 2 * lax.axis_index("x") + lax.axis_index("y")
    outs = []
    for t, (p, rv) in enumerate(zip(ps, recv)):
        _, r, cols = rv.shape
        tr_out = transposed[t]
        oshape = (cols, r) if tr_out else (r, cols)

        def body(c_ref, p_ref, r0_ref, r1_ref, r2_ref, o_ref):
            acc = ((p_ref[...].astype(F32) + r0_ref[...].astype(F32)) + r1_ref[...].astype(F32)) + r2_ref[...].astype(F32)
            o_ref[...] = acc.T if tr_out else acc

        outs.append(pl.pallas_call(
            body, name=f"{name}_{t}",
            grid_spec=pltpu.PrefetchScalarGridSpec(
                num_scalar_prefetch=1, grid=(1,),
                in_specs=[pl.BlockSpec((None, r, cols), lambda i, cr: (cr[0], 0, 0))] +
                         [pl.BlockSpec((None, r, cols), lambda i, cr, j=j: (j, 0, 0)) for j in range(3)],
                out_specs=pl.BlockSpec(oshape, lambda i, cr: (0, 0))),
            out_shape=jax.ShapeDtypeStruct(oshape, F32),
        )(jnp.reshape(chip, (1,)).astype(jnp.int32), p, rv, rv, rv))
    return outs


def _sum_gathered(g):
    _, r, cols = g.shape

    def body(g_ref, o_ref):
        acc = g_ref[0]
        for d in range(1, NDEV):
            acc = acc + g_ref[d]
        o_ref[...] = acc

    return pl.pallas_call(body, name="small_sum", out_shape=jax.ShapeDtypeStruct((r, cols), F32))(g)


HBM = pl.BlockSpec(memory_space=pltpu.HBM)
SEM = pl.BlockSpec(memory_space=pltpu.SEMAPHORE)
EFFECT = pltpu.SideEffectType.DATAFLOW_SIDE_EFFECTING
NREL = NDEV - 1


def _related(k):
    x, y, c = lax.axis_index("x"), lax.axis_index("y"), lax.axis_index("c")
    px = 1 - x if k & 4 else x
    py = 1 - y if k & 2 else y
    pc = 1 - c if k & 1 else c
    return (px, py, pc), _dev_index(px, py, pc)


def _in_hbm(a):
    return pltpu.with_memory_space_constraint(a, pltpu.HBM)


ALL_RELS = tuple(range(1, NDEV))
CHIP_RELS = (4, 2, 6)


def _split_copies(xs, lands, send_sems, recv_sems, src_of, dst_of, rels):
    copies = []
    for t in range(len(xs)):
        for q, k in enumerate(rels):
            peer, peer_idx = _related(k)
            copies.append(pltpu.make_async_remote_copy(
                src_ref=src_of(xs[t], t, peer_idx), dst_ref=dst_of(lands[t], t, q, peer_idx),
                send_sem=send_sems.at[t * len(rels) + q], recv_sem=recv_sems.at[t * len(rels) + q],
                device_id=peer, device_id_type=MESH))
    return copies


def _exchange_start(name, xs, lands, src_of, dst_of, after, rels=ALL_RELS):
    nt = len(xs)

    def body(*refs):
        x_refs, land_refs = refs[:nt], refs[nt:2 * nt]
        send_sems, recv_sems = refs[2 * nt + 1], refs[2 * nt + 2]
        token = refs[-1]
        for cp in _split_copies(x_refs, land_refs, send_sems, recv_sems, src_of, dst_of, rels):
            cp.start()
        token[...] = jnp.zeros_like(token)

    res = pl.pallas_call(
        body, name=name,
        out_shape=(pltpu.SemaphoreType.DMA((nt * len(rels),)), pltpu.SemaphoreType.DMA((nt * len(rels),)),
                   *[pltpu.HBM(a.shape, a.dtype) for a in xs], *[pltpu.HBM(a.shape, a.dtype) for a in lands],
                   jax.ShapeDtypeStruct((8, 128), F32)),
        in_specs=[HBM] * (2 * nt) + [ANY],
        out_specs=(SEM, SEM, *[HBM] * (2 * nt), pl.BlockSpec(memory_space=pltpu.VMEM)),
        input_output_aliases={i: 2 + i for i in range(2 * nt)},
        compiler_params=pltpu.CompilerParams(has_side_effects=EFFECT),
    )(*[_in_hbm(a) for a in xs], *[_in_hbm(a) for a in lands], after)
    return res[0], res[1], res[2:2 + nt], res[2 + nt:2 + 2 * nt], res[-1]


def _exchange_wait(name, send_sems, recv_sems, xs, lands, src_of, dst_of, after, rels=ALL_RELS):
    nt = len(xs)

    def body(*refs):
        x_refs, land_refs = refs[:nt], refs[nt:2 * nt]
        send_sems, recv_sems = refs[2 * nt], refs[2 * nt + 1]
        for cp in _split_copies(x_refs, land_refs, send_sems, recv_sems, src_of, dst_of, rels):
            cp.wait_send()
            cp.wait_recv()

    res = pl.pallas_call(
        body, name=name,
        out_shape=(*[pltpu.HBM(a.shape, a.dtype) for a in xs], *[pltpu.HBM(a.shape, a.dtype) for a in lands]),
        in_specs=[HBM] * (2 * nt) + [SEM, SEM, ANY], out_specs=tuple([HBM] * (2 * nt)),
        input_output_aliases={i: i for i in range(2 * nt)},
        compiler_params=pltpu.CompilerParams(has_side_effects=EFFECT),
    )(*xs, *lands, send_sems, recv_sems, after)
    return res[:nt], res[nt:]


def _gather_start(name, shards, leads, after, rels=ALL_RELS):
    def src_of(x_ref, t, peer_idx):
        return x_ref

    def dst_of(land_ref, t, k, peer_idx):
        me = _dev_index(lax.axis_index("x"), lax.axis_index("y"), lax.axis_index("c"))
        return land_ref.at[(slice(None),) * leads[t] + (me,)]

    lands = [lax.empty(s.shape[:ld] + (NDEV,) + s.shape[ld:], s.dtype) for s, ld in zip(shards, leads)]
    return _exchange_start(name, shards, lands, src_of, dst_of, after, rels)


def _gather_wait(name, started, leads, after, rels=ALL_RELS):
    send_sems, recv_sems, shards, lands, _ = started

    def src_of(x_ref, t, peer_idx):
        return x_ref

    def dst_of(land_ref, t, k, peer_idx):
        return land_ref.at[(slice(None),) * leads[t] + (peer_idx,)]

    shards, lands = _exchange_wait(name, send_sems, recv_sems, shards, lands, src_of, dst_of, after, rels)
    me = _dev_index(lax.axis_index("x"), lax.axis_index("y"), lax.axis_index("c"))
    return [lax.dynamic_update_index_in_dim(g, s, me, ld) for g, s, ld in zip(lands, shards, leads)]


SIBLING_AND_CHIPS = (1,) + CHIP_RELS


def _forward_to_sibling(name, gathered, leads):
    nt = len(gathered)

    def body(*refs):
        ins, outs = refs[:nt], refs[nt:2 * nt]
        send_sems, recv_sems = refs[2 * nt:]
        x, y, c, chips = _mesh_pos()
        copies, arrivals = [], []
        for t in range(nt):
            for j, chip in enumerate(chips):
                def block(core):
                    return outs[t].at[(slice(None),) * leads[t] + (_dev_index(*chip, core),)]
                copies.append(pltpu.make_async_remote_copy(
                    src_ref=block(c), dst_ref=block(c), send_sem=send_sems.at[t, j], recv_sem=recv_sems.at[t, j],
                    device_id=(x, y, 1 - c), device_id_type=MESH))
                arrivals.append(pltpu.make_async_remote_copy(
                    src_ref=block(1 - c), dst_ref=block(1 - c), send_sem=send_sems.at[t, j], recv_sem=recv_sems.at[t, j],
                    device_id=(x, y, 1 - c), device_id_type=MESH))
        for cp in copies:
            cp.start()
        for cp in arrivals:
            cp.wait_recv()
        for cp in copies:
            cp.wait_send()

    return pl.pallas_call(
        body, name=name, in_specs=[ANY] * nt, out_specs=[ANY] * nt,
        out_shape=[jax.ShapeDtypeStruct(g.shape, g.dtype) for g in gathered],
        input_output_aliases={t: t for t in range(nt)},
        scratch_shapes=[pltpu.SemaphoreType.DMA((nt, 3)), pltpu.SemaphoreType.DMA((nt, 3))],
    )(*gathered)


def _scatter_src(x_ref, t, peer_idx):
    return x_ref.at[peer_idx]


def _scatter_dst(land_ref, t, q, peer_idx):
    return land_ref.at[q]


def _chips_src(x_ref, t, peer_idx):
    return x_ref.at[peer_idx // 2]


def _chips_start(name, ps, after):
    lands = [lax.empty((len(CHIP_RELS),) + p.shape[1:], p.dtype) for p in ps]
    return _exchange_start(name, ps, lands, _chips_src, _scatter_dst, after, CHIP_RELS)


def _chips_wait(name, started, after):
    send_sems, recv_sems, ps, lands, _ = started
    return _exchange_wait(name, send_sems, recv_sems, ps, lands, _chips_src, _scatter_dst, after, CHIP_RELS)


def _scatter_start(name, gs, after):
    lands = [lax.empty((NREL,) + g.shape[1:], g.dtype) for g in gs]
    return _exchange_start(name, gs, lands, _scatter_src, _scatter_dst, after)


def _scatter_wait(name, started, after, transposed):
    send_sems, recv_sems, gs, lands, _ = started
    gs, lands = _exchange_wait(name, send_sems, recv_sems, gs, lands, _scatter_src, _scatter_dst, after)
    me = _dev_index(lax.axis_index("x"), lax.axis_index("y"), lax.axis_index("c"))
    outs = []
    for t, (g, rv) in enumerate(zip(gs, lands)):
        _, r, cols = rv.shape
        tr_out = transposed[t]
        oshape = (cols, r) if tr_out else (r, cols)

        def body(c_ref, own_ref, rv_ref, o_ref):
            acc = own_ref[...].astype(F32)
            for k in range(NREL):
                acc = acc + rv_ref[k].astype(F32)
            o_ref[...] = acc.T if tr_out else acc

        outs.append(pl.pallas_call(
            body, name=f"{name}_sum_{t}",
            grid_spec=pltpu.PrefetchScalarGridSpec(
                num_scalar_prefetch=1, grid=(1,),
                in_specs=[pl.BlockSpec((None, r, cols), lambda i, cr: (cr[0], 0, 0)),
                          pl.BlockSpec((NREL, r, cols), lambda i, cr: (0, 0, 0))],
                out_specs=pl.BlockSpec(oshape, lambda i, cr: (0, 0))),
            out_shape=jax.ShapeDtypeStruct(oshape, F32), compiler_params=_cparams(VMEM_BIG),
        )(jnp.reshape(me, (1,)).astype(jnp.int32), g, rv))
    return outs


def _adamw(w, g, m, v):
    shape = w.shape
    cols = shape[-1]
    rows = math.prod(shape[:-1]) if len(shape) > 1 else 1
    w2, g2, m2, v2 = (jnp.reshape(t, (rows, cols)) for t in (w, g, m, v))
    tr = _pick(rows, (1024, 704, 512, 256, 128)) if rows * cols > 65536 else rows
    c1 = 1.0 / (1.0 - ADAM_B1 ** ADAM_STEP)
    c2 = 1.0 / (1.0 - ADAM_B2 ** ADAM_STEP)

    def body(w_ref, g_ref, m_ref, v_ref, d_ref, nm_ref, nv_ref):
        gv = g_ref[...]
        nm = ADAM_B1 * m_ref[...] + (1.0 - ADAM_B1) * gv
        nv = ADAM_B2 * v_ref[...] + (1.0 - ADAM_B2) * (gv * gv)
        d_ref[...] = -ADAM_LR * ((nm * c1) / (jnp.sqrt(nv * c2) + ADAM_EPS) + ADAM_WD * w_ref[...])
        nm_ref[...] = nm
        nv_ref[...] = nv

    spec = pl.BlockSpec((tr, cols), lambda i: (i, 0))
    osh = jax.ShapeDtypeStruct((rows, cols), F32)
    d, nm, nv = pl.pallas_call(
        body, name="adamw", grid=(rows // tr,), in_specs=[spec] * 4, out_specs=[spec] * 3, out_shape=[osh] * 3,
        compiler_params=_cparams(VMEM_BIG),
    )(w2, g2, m2, v2)
    return jnp.reshape(d, shape), jnp.reshape(nm, shape), jnp.reshape(nv, shape)


def _pad_heads(w, name):
    if name not in P_HEADS:
        return w
    nh, real = P_HEADS[name]
    w = jnp.reshape(w, w.shape[:-2] + (nh, real, w.shape[-1]))
    w = jnp.pad(w, [(0, 0)] * (w.ndim - 2) + [(0, HP - real), (0, 0)])
    return jnp.reshape(w, w.shape[:-3] + (nh * HP, w.shape[-1]))


def _unpad_heads(w, name):
    if name not in P_HEADS:
        return w
    nh, real = P_HEADS[name]
    w = jnp.reshape(w, w.shape[:-2] + (nh, HP, w.shape[-1]))[..., :real, :]
    return jnp.reshape(w, w.shape[:-3] + (nh * real, w.shape[-1]))


def _win_pad(win_t):
    segs, o = {}, 0
    for n, s in zip(IN_NAMES, IN_SIZES):
        segs[n] = win_t[..., o:o + s, :]
        o += s
    return jnp.concatenate([_pad_heads(segs[n], n) for n in P_ORDER], axis=-2)


def _win_unpad(win_p):
    segs = {n: _unpad_heads(win_p[..., P_OFF[n]:P_OFF[n] + P_WIDTH[n], :], n) for n in P_ORDER}
    return jnp.concatenate([segs[n] for n in IN_NAMES], axis=-2)


def _t(w):
    return jnp.swapaxes(w, -1, -2)


def _ffn_stacked(g_g, g_u, g_d):
    wg, wu, wd = (jnp.reshape(g, (2, DFF, D)) for g in (g_g, g_u, g_d))
    return [(wg, wu, wd, (j,)) for j in range(2)]


def _ffn_single(g_g, g_u, g_d):
    return tuple(jnp.reshape(g, (DFF, D)) for g in (g_g, g_u, g_d)) + ((),)


def _layer_weights(ffn, g_in, g_pa, g_pb, g_out, gains, w2, b2, bm, gn, gq, gk):
    w2p = jnp.pad(jnp.reshape(w2, (2, GLA_RANK, GLA_H, GLA_DK)), ((0, 0), (0, HP - GLA_RANK), (0, 0), (0, HP - GLA_DK)))
    b2p = jnp.pad(jnp.reshape(b2, (2, 1, GLA_H, GLA_DK)), ((0, 0), (0, 0), (0, 0), (0, HP - GLA_DK)))
    wpb_t = jnp.pad(jnp.reshape(g_pb, (D, ATT_H, HEAD_DIM)), ((0, 0), (0, 0), (0, HP - HEAD_DIM)))
    return dict(
        gains=jnp.reshape(gains, (1, 3, 1, D)), ffn=ffn,
        win_t=_win_pad(jnp.reshape(g_in, (1, D_IN, D))), wpa_t=jnp.reshape(g_pa, (1, D, 512)),
        wpb_t=jnp.reshape(wpb_t, (1, D, ATT_H * HP)), wout=jnp.reshape(g_out, (1, D, D)),
        w2p=jnp.reshape(w2p, (1, 2, HP, GLA_H * HP)), b2p=jnp.reshape(b2p, (1, 2, 1, GLA_H * HP)),
        bm=jnp.reshape(bm, (1, 2, 1, D)), gn=jnp.reshape(gn, (1, 1, GLA_H * HP)),
        gq=jnp.pad(jnp.reshape(gq, (1, 1, HEAD_DIM)), ((0, 0), (0, 0), (0, HP - HEAD_DIM))),
        gk=jnp.pad(jnp.reshape(gk, (1, 1, HEAD_DIM)), ((0, 0), (0, 0), (0, HP - HEAD_DIM))))


def _layer_fwd_lower(h, z, ffn0, gain1):
    return _ffn_fwd(h, z, *ffn0, gain1)


def _layer_fwd_upper(h, z, s0, w, tabs, next_gain):
    h, z, s1 = _mixer_fwd(h, z, w, 0, tabs, w["gains"][0, 2])
    h, z, s2 = _ffn_fwd(h, z, *w["ffn"][1], next_gain)
    return h, z, (s0, s1, s2)


def _layer_fwd(h, z, w, tabs, next_gain):
    h, z, s0 = _layer_fwd_lower(h, z, w["ffn"][0], w["gains"][0, 1])
    return _layer_fwd_upper(h, z, s0, w, tabs, next_gain)


def _layer_bwd_upper(dh, dhb, saved, w, tabs):
    _, s1, s2 = saved
    dh, dhb, dg2, dwg1, dwu1, dwd1 = _ffn_bwd(dh, dhb, s2, w["gains"][0, 2], *w["ffn"][1])
    dh, dhb, gm = _mixer_bwd(dh, dhb, s1, w["gains"][0, 1], w, 0, tabs)
    gm.update(gain2=dg2, wg1=dwg1, wu1=dwu1, wd1=dwd1)
    return dh, dhb, gm


def _layer_bwd_lower(dh, dhb, saved, w, gm):
    dh, dhb, dg0, dwg0, dwu0, dwd0 = _ffn_bwd(dh, dhb, saved[0], w["gains"][0, 0], *w["ffn"][0])
    gm.update(gain0=dg0, wg0=dwg0, wu0=dwu0, wd0=dwd0)
    return dh, dhb, gm


def _layer_bwd(dh, dhb, saved, w, tabs):
    dh, dhb, gm = _layer_bwd_upper(dh, dhb, saved, w, tabs)
    return _layer_bwd_lower(dh, dhb, saved, w, gm)


def _blocks(ts):
    return [jnp.reshape(t, (NDEV, t.shape[0] // NDEV, t.shape[1])) for t in ts]


def _upper_grads(g):
    d_in = _win_unpad(g["win_t"])
    d_pb = jnp.reshape(jnp.reshape(g["wpb_t"], (D, ATT_H, HP))[:, :, :HEAD_DIM], (D, 512))
    return _blocks([g["wg1"], g["wu1"], g["wd1"], d_in, g["wpa_t"], d_pb, g["wout"]])


def _lower_grads(g):
    return _blocks([g["wg0"], g["wu0"], g["wd0"]])


def _big_grads(g):
    return _lower_grads(g) + _upper_grads(g)


def kernel(x, meta_tokens, norm_gains, ffn_w_gate, ffn_w_up, ffn_w_down, w_in, gla_w2, gla_b2, gla_gn, q_norm, k_norm, w_pa, w_pb, b_merge, w_out, final_norm, loss_target, m_meta_tokens, m_norm_gains, m_ffn_w_gate, m_ffn_w_up, m_ffn_w_down, m_w_in, m_gla_w2, m_gla_b2, m_gla_gn, m_q_norm, m_k_norm, m_w_pa, m_w_pb, m_b_merge, m_w_out, m_final_norm, v_meta_tokens, v_norm_gains, v_ffn_w_gate, v_ffn_w_up, v_ffn_w_down, v_w_in, v_gla_w2, v_gla_b2, v_gla_gn, v_q_norm, v_k_norm, v_w_pa, v_w_pb, v_b_merge, v_w_out, v_final_norm):
    dev = _dev_index(lax.axis_index("x"), lax.axis_index("y"), lax.axis_index("c"))
    sh_g = _t(ffn_w_gate).astype(BF16)
    sh_u = _t(ffn_w_up).astype(BF16)
    sh_d = ffn_w_down.astype(BF16)
    sh_in = _t(w_in).astype(BF16)
    sh_pa = _t(w_pa).astype(BF16)
    sh_pb = _t(w_pb).astype(BF16)
    sh_out = w_out.astype(BF16)
    small = jnp.concatenate([jnp.reshape(t, (-1, 128)) for t in
                             (meta_tokens, norm_gains, gla_w2, gla_b2, b_merge)], axis=0)
    small = jnp.pad(small, ((0, 2), (0, 0)))
    def shards(l):
        return [sh_g[l], sh_u[l], sh_d[l], sh_in[l], sh_pa[l], sh_pb[l], sh_out[l]]

    w_leads = [1, 1, 1, 0, 0, 0, 0]
    *g0_ffn0, g_small = _all_gather("gather_layer0", [sh_g[0, 0], sh_u[0, 0], sh_d[0, 0], small], [0, 0, 0, 0])
    rest0 = [sh_g[0, 1], sh_u[0, 1], sh_d[0, 1], sh_in[0], sh_pa[0], sh_pb[0], sh_out[0]]
    rest_leads = [0] * len(rest0)
    started0 = _gather_start("gather_start_0", rest0, rest_leads, g_small, SIBLING_AND_CHIPS)
    meta_full = jnp.reshape(jnp.transpose(g_small[:, 0:16], (1, 0, 2)), (NMETA, D)) + started0[4][0, 0]
    gains_full = jnp.reshape(jnp.transpose(jnp.reshape(g_small[:, 16:28], (NDEV, DEPTH, 3, 128)), (1, 2, 0, 3)), (DEPTH, 3, D))
    w2_full = jnp.reshape(jnp.transpose(jnp.reshape(g_small[:, 28:60], (NDEV, DEPTH, 2, GLA_RANK, 32)), (1, 2, 3, 0, 4)),
                          (DEPTH, 2, GLA_RANK, 256))
    b2_full = jnp.reshape(jnp.transpose(jnp.reshape(g_small[:, 60:62], (NDEV, DEPTH, 2, 32)), (1, 2, 0, 3)), (DEPTH, 2, 256))
    bm_full = jnp.reshape(jnp.transpose(jnp.reshape(g_small[:, 62:70], (NDEV, DEPTH, 2, 128)), (1, 2, 0, 3)), (DEPTH, 2, D))

    def layer_weights(l, ffn, others, gains_l):
        return _layer_weights(ffn, *others, gains_l, w2_full[l], b2_full[l], bm_full[l], gla_gn[l], q_norm[l], k_norm[l])

    xl = x[0]
    lp = xl.shape[0] + NULL + NMETA
    tabs = _rope_tables(lp)
    h = jnp.concatenate([jnp.zeros((NULL, D), F32), meta_full, xl], axis=0)
    weights, saved, started = [], [], {}
    z = _rmsnorm_fwd(h, jnp.reshape(gains_full[0, 0], (1, D)))
    for l in range(DEPTH):
        next_gain = jnp.reshape(gains_full[l + 1, 0], (1, D)) if l + 1 < DEPTH else None
        if l == 0:
            ffn0 = _ffn_single(*g0_ffn0)
            h, z, s0 = _layer_fwd_lower(h, z, ffn0, jnp.reshape(gains_full[0, 1], (1, D)))
            rest = _forward_to_sibling("gather_forward_0", _gather_wait("gather_wait_0", started0, rest_leads, h,
                                                                         SIBLING_AND_CHIPS), rest_leads)
            started[1] = _gather_start("gather_start_1", shards(1), w_leads, rest[0])
            weights.append(layer_weights(0, [ffn0, _ffn_single(*rest[:3])], rest[3:], gains_full[0]))
            z = z + started[1][4][0, 0].astype(BF16)
            h, z, sv = _layer_fwd_upper(h, z, s0, weights[0], tabs, next_gain)
        else:
            tok = jnp.zeros((), F32)
            if l < DEPTH - 1:
                started[l + 1] = _gather_start(f"gather_start_{l + 1}", shards(l + 1), w_leads, h)
                tok = started[l + 1][4][0, 0]
            gathered = _gather_wait(f"gather_wait_{l}", started[l], w_leads, h)
            weights.append(layer_weights(l, _ffn_stacked(*gathered[:3]), gathered[3:], gains_full[l] + tok))
            h, z, sv = _layer_fwd(h, z, weights[l], tabs, next_gain)
        saved.append(sv)
    loss, dh, dhb, d_final = _loss_head(h, loss_target[0], jnp.reshape(final_norm, (1, D)))
    loss = lax.psum(loss[0, 0], ("x", "y", "c"))

    grads, scattering = [None] * DEPTH, {}
    tok = jnp.zeros((), F32)
    for l in reversed(range(DEPTH)):
        w = dict(weights[l], gains=weights[l]["gains"] + tok)
        if l > 0:
            dh, dhb, grads[l] = _layer_bwd(dh, dhb, saved[l], w, tabs)
            scattering[l] = _scatter_start(f"scatter_start_{l}", _big_grads(grads[l]), dhb)
            tok = scattering[l][4][0, 0]
        else:
            dh, dhb, gm = _layer_bwd_upper(dh, dhb, saved[l], w, tabs)
            ups = _upper_grads(gm)
            pair = _pair_sum("rs_pair_up", ups, _exchange_sibling("rs_sibling_up", ups))
            scattering[l] = _chips_start(f"scatter_start_{l}", pair, dhb)
            dhb = dhb + scattering[l][4][0, 0].astype(BF16)
            dh, dhb, grads[l] = _layer_bwd_lower(dh, dhb, saved[l], w, gm)
    grad_x = dh[NULL + NMETA:][None]
    t_lower, t_upper = [False, False, False], [False, False, False, True, True, True, False]
    lows = _lower_grads(grads[0])
    pair_lo = _pair_sum("rs_pair_lo", lows, _exchange_sibling("rs_sibling_lo", lows))
    started_lo = _chips_start("scatter_start_lo", pair_lo, dhb)
    red = [None] * DEPTH
    for l in reversed(range(1, DEPTH)):
        red[l] = _scatter_wait(f"scatter_wait_{l}", scattering[l], started_lo[4], t_lower + t_upper)
    pair, recv = _chips_wait("scatter_wait_0", scattering[0], red[1][-1])
    red_upper = _final_sum("rs_sum_up", pair, recv, t_upper)
    pair_lo, recv_lo = _chips_wait("scatter_wait_lo", started_lo, red_upper[-1])
    red[0] = _final_sum("rs_sum_lo", pair_lo, recv_lo, t_lower) + red_upper
    g_gate = jnp.stack([jnp.stack([red[l][0], red[l][3]]) for l in range(DEPTH)])
    g_up = jnp.stack([jnp.stack([red[l][1], red[l][4]]) for l in range(DEPTH)])
    g_down = jnp.stack([jnp.stack([red[l][2], red[l][5]]) for l in range(DEPTH)])
    g_win = jnp.stack([red[l][6] for l in range(DEPTH)])
    g_wpa = jnp.stack([red[l][7] for l in range(DEPTH)])
    g_wpb = jnp.stack([red[l][8] for l in range(DEPTH)])
    g_wout = jnp.stack([red[l][9] for l in range(DEPTH)])

    d_meta = dh[NULL:NULL + NMETA]
    d_gains = jnp.stack([jnp.concatenate([grads[l]["gain0"], grads[l]["gain"], grads[l]["gain2"]], axis=0)
                         for l in range(DEPTH)])
    d_w2 = jnp.stack([jnp.reshape(jnp.reshape(grads[l]["w2p"], (2, HP, GLA_H, HP))[:, :GLA_RANK, :, :GLA_DK],
                                  (2, GLA_RANK, 256)) for l in range(DEPTH)])
    d_b2 = jnp.stack([jnp.reshape(jnp.reshape(grads[l]["b2p"], (2, GLA_H, HP))[:, :, :GLA_DK], (2, 256))
                      for l in range(DEPTH)])
    d_gn = jnp.stack([grads[l]["gn"][0] for l in range(DEPTH)])
    d_gq = jnp.stack([grads[l]["gq"][0, :HEAD_DIM] for l in range(DEPTH)])
    d_gk = jnp.stack([grads[l]["gk"][0, :HEAD_DIM] for l in range(DEPTH)])
    d_bm = jnp.stack([jnp.concatenate([grads[l]["bma"], grads[l]["bmb"]], axis=0) for l in range(DEPTH)])
    parts = [d_meta, d_gains, d_w2, d_b2, d_gn, d_gq, d_gk, d_bm, d_final[0]]
    sizes = [p.size for p in parts]
    flat = jnp.concatenate([jnp.reshape(p, (-1,)) for p in parts])
    flat = jnp.reshape(flat, (-1, 128))
    nrow = flat.shape[0]
    flat = jnp.pad(flat, ((0, (-nrow) % 8), (0, 0)))
    (g_flat,) = _all_gather("gather_small_grads", [flat], [0])
    tot = jnp.reshape(_sum_gathered(g_flat), (-1,))
    full, o = [], 0
    for p, s in zip(parts, sizes):
        full.append(jnp.reshape(tot[o:o + s], p.shape))
        o += s
    f_meta, f_gains, f_w2, f_b2, f_gn, f_gq, f_gk, f_bm, f_final = full

    def mine(t, width):
        return lax.dynamic_slice_in_dim(t, dev * width, width, axis=t.ndim - 1)

    g_small = dict(meta_tokens=mine(f_meta, 128), norm_gains=mine(f_gains, 128), gla_w2=mine(f_w2, 32),
                   gla_b2=mine(f_b2, 32), gla_gn=f_gn, q_norm=f_gq, k_norm=f_gk, b_merge=mine(f_bm, 128),
                   final_norm=f_final)
    gr = dict(g_small, ffn_w_gate=g_gate, ffn_w_up=g_up, ffn_w_down=g_down, w_in=g_win, w_pa=g_wpa, w_pb=g_wpb,
              w_out=g_wout)
    ws = dict(meta_tokens=meta_tokens, norm_gains=norm_gains, ffn_w_gate=ffn_w_gate, ffn_w_up=ffn_w_up,
              ffn_w_down=ffn_w_down, w_in=w_in, gla_w2=gla_w2, gla_b2=gla_b2, gla_gn=gla_gn, q_norm=q_norm,
              k_norm=k_norm, w_pa=w_pa, w_pb=w_pb, b_merge=b_merge, w_out=w_out, final_norm=final_norm)
    ms = dict(meta_tokens=m_meta_tokens, norm_gains=m_norm_gains, ffn_w_gate=m_ffn_w_gate, ffn_w_up=m_ffn_w_up,
              ffn_w_down=m_ffn_w_down, w_in=m_w_in, gla_w2=m_gla_w2, gla_b2=m_gla_b2, gla_gn=m_gla_gn, q_norm=m_q_norm,
              k_norm=m_k_norm, w_pa=m_w_pa, w_pb=m_w_pb, b_merge=m_b_merge, w_out=m_w_out, final_norm=m_final_norm)
    vs = dict(meta_tokens=v_meta_tokens, norm_gains=v_norm_gains, ffn_w_gate=v_ffn_w_gate, ffn_w_up=v_ffn_w_up,
              ffn_w_down=v_ffn_w_down, w_in=v_w_in, gla_w2=v_gla_w2, gla_b2=v_gla_b2, gla_gn=v_gla_gn, q_norm=v_q_norm,
              k_norm=v_k_norm, w_pa=v_w_pa, w_pb=v_w_pb, b_merge=v_b_merge, w_out=v_w_out, final_norm=v_final_norm)
    names = ["meta_tokens", "norm_gains", "ffn_w_gate", "ffn_w_up", "ffn_w_down", "w_in", "gla_w2", "gla_b2", "gla_gn",
             "q_norm", "k_norm", "w_pa", "w_pb", "b_merge", "w_out", "final_norm"]
    deltas, new_m, new_v = [], [], []
    for n in names:
        if n in ("ffn_w_gate", "ffn_w_up"):
            dlt, nm, nv = (_t(r) for r in _adamw(_t(ws[n]), gr[n], _t(ms[n]), _t(vs[n])))
            gr[n] = _t(gr[n])
        else:
            dlt, nm, nv = _adamw(ws[n], gr[n], ms[n], vs[n])
        deltas.append(dlt)
        new_m.append(nm)
        new_v.append(nv)
    return (loss, grad_x, *[gr[n] for n in names], *deltas, *new_m, *new_v)
```

```python
import math

import jax
import jax.numpy as jnp
import numpy as np
from jax import lax
from jax.experimental import pallas as pl
from jax.experimental.pallas import tpu as pltpu

F32 = jnp.float32
BF16 = jnp.bfloat16
MESH = pl.DeviceIdType.MESH
ANY = pl.BlockSpec(memory_space=pl.ANY)

NDEV = 8
D = 1024
DFF = 2816
DEPTH = 4
NMETA = 16
NULL = 112
GRID_W = 64
EPS = 1e-6
HP = 128
GLA_H = 4
GLA_DK = 64
GLA_RANK = 16
GLA_TAU = 16.0
CHUNK = 64
ATT_H = 8
ATT_KV = 2
ATT_G = ATT_H // ATT_KV
HEAD_DIM = 64
ROPE_THETA = 10000.0

IN_SIZES = (256, 256, 512, 512, 16, 16, 512, 128, 128, 1024, 1024)
IN_NAMES = ("qa", "ka", "va", "ra", "lrf", "lrb", "qb", "kb", "vb", "ga", "gb")
D_IN = sum(IN_SIZES)
P_ORDER = ("qb", "ga", "gb", "qa", "ka", "va", "ra", "kb", "vb", "lrf", "lrb")
P_WIDTH = dict(qb=1024, ga=1024, gb=1024, qa=512, ka=512, va=512, ra=512, kb=256, vb=256, lrf=128, lrb=128)
P_OFF = {}
_o = 0
for _n in P_ORDER:
    P_OFF[_n] = _o
    _o += P_WIDTH[_n]
D_INP = _o
P_HEADS = dict(qa=(4, 64), ka=(4, 64), qb=(8, 64), kb=(2, 64), vb=(2, 64), lrf=(1, 16), lrb=(1, 16))

ADAM_LR = 0.001
ADAM_B1 = 0.9
ADAM_B2 = 0.999
ADAM_EPS = 1e-08
ADAM_WD = 0.01
ADAM_STEP = 10

VMEM_BIG = 58 * 1024 * 1024
MXU_N = 256


def _cparams(vmem=None):
    return pltpu.CompilerParams(vmem_limit_bytes=vmem) if vmem else pltpu.CompilerParams()


def _pick(n, prefs):
    for p in prefs:
        if n % p == 0:
            return p
    return n


def _tm(lp):
    return _pick(lp, (528, 512, 256, 128))


def _tm_wide(lp):
    return _pick(lp, (1056, 512, 256, 128))


_DN = {"nn": (((1,), (0,)), ((), ())), "nt": (((1,), (1,)), ((), ())), "tn": (((0,), (0,)), ((), ()))}


def _dot(a, b, mode="nn", precision=None):
    return lax.dot_general(a, b, _DN[mode], preferred_element_type=F32, precision=precision)


def _split(x):
    hi = x.astype(BF16)
    return hi, (x - hi.astype(F32)).astype(BF16)


def _dot_sel(t, x, mode="nn"):
    hi, lo = _split(x)
    return _dot(t, hi, mode) + _dot(t, lo, mode)


def _dot3(a, b, mode="nn"):
    ah, al = _split(a)
    bh, bl = _split(b)
    return _dot(ah, bh, mode) + (_dot(ah, bl, mode) + _dot(al, bh, mode))


def _sigmoid(x):
    return 0.5 * jnp.tanh(0.5 * x) + 0.5


def _mm(name, m, n, terms, outs, epilogue, extras=(), *, tm, tn, nk=1, nsub=1, i_outer=False, vmem=None):
    gm, gn = m // tm, n // tn
    assert gm * tm == m and gn * tn == n, (name, m, n, tm, tn)
    n_acc = 1 + max(t[3] for t in terms)

    def gmap(f):
        if i_outer:
            return lambda i, j, kk: f(i, j, kk)
        return lambda j, i, kk: f(i, j, kk)

    in_specs, args = [], []
    for a, b, mode, _, pa, pb in terms:
        kdim = a.shape[-2] if mode == "tn" else a.shape[-1]
        tk = kdim // nk
        assert tk * nk == kdim
        na, nb = (None,) * len(pa), (None,) * len(pb)
        if mode == "tn":
            in_specs.append(pl.BlockSpec(na + (tk, tm), gmap(lambda i, j, kk, pa=pa: pa + (kk, i))))
        else:
            in_specs.append(pl.BlockSpec(na + (tm, tk), gmap(lambda i, j, kk, pa=pa: pa + (i, kk))))
        if mode == "nt":
            in_specs.append(pl.BlockSpec(nb + (tn, tk), gmap(lambda i, j, kk, pb=pb: pb + (j, kk))))
        else:
            in_specs.append(pl.BlockSpec(nb + (tk, tn), gmap(lambda i, j, kk, pb=pb: pb + (kk, j))))
        args += [a, b]
    for arr, kind, off, pe in extras:
        ne = (None,) * len(pe)
        if kind == "mn":
            in_specs.append(pl.BlockSpec(ne + (tm, tn), gmap(lambda i, j, kk, off=off, pe=pe: pe + (i, j + off))))
        else:
            in_specs.append(pl.BlockSpec(ne + (1, tn), gmap(lambda i, j, kk, off=off, pe=pe: pe + (0, j + off))))
        args.append(arr)
    out_shape, out_specs = [], []
    for shape, dtype, kind, off, po in outs:
        no = (None,) * len(po)
        out_shape.append(jax.ShapeDtypeStruct(shape, dtype))
        if kind == "mn":
            out_specs.append(pl.BlockSpec(no + (tm, tn), gmap(lambda i, j, kk, off=off, po=po: po + (i, j + off))))
        else:
            assert not i_outer
            out_specs.append(pl.BlockSpec(no + (1, tn), gmap(lambda i, j, kk, off=off, po=po: po + (0, j + off))))
    n_t, n_e, n_o = len(terms), len(extras), len(outs)
    i_axis = 0 if i_outer else 1

    def body(*refs):
        ins = refs[: 2 * n_t]
        exs = refs[2 * n_t: 2 * n_t + n_e]
        ors = refs[2 * n_t + n_e: 2 * n_t + n_e + n_o]
        accs = refs[2 * n_t + n_e + n_o:]
        i = pl.program_id(i_axis)
        kk = pl.program_id(2)

        def partials(cs):
            part = [None] * n_acc
            for t, (_, _, mode, ai, _, _) in enumerate(terms):
                b_ref = ins[2 * t + 1]
                b_val = b_ref[cs, :] if mode == "nt" else b_ref[:, cs]
                p = _dot(ins[2 * t][...], b_val, mode)
                part[ai] = p if part[ai] is None else part[ai] + p
            return part

        def finish(vals, cs):
            res = epilogue(vals, [e[:, cs] for e in exs], i * tm)
            for (_, dtype, kind, _, _), o_ref, v in zip(outs, ors, res):
                if kind == "mn":
                    o_ref[:, cs] = v.astype(dtype)
                else:
                    @pl.when(i == 0)
                    def _():
                        o_ref[:, cs] = v.astype(dtype)

                    @pl.when(i != 0)
                    def _():
                        o_ref[:, cs] += v.astype(dtype)

        if nk == 1:
            w = tn // nsub
            for s in range(nsub):
                cs = slice(s * w, (s + 1) * w)
                finish(partials(cs), cs)
        else:
            part = partials(slice(None))
            @pl.when(kk == 0)
            def _():
                for a_ref, p in zip(accs, part):
                    a_ref[...] = p

            @pl.when(kk != 0)
            def _():
                for a_ref, p in zip(accs, part):
                    a_ref[...] += p

            @pl.when(kk == nk - 1)
            def _():
                finish([a_ref[...] for a_ref in accs], slice(None))

    scratch = [pltpu.VMEM((tm, tn), F32) for _ in range(n_acc)] if nk > 1 else []
    grid = (gm, gn, nk) if i_outer else (gn, gm, nk)
    res = pl.pallas_call(
        body, name=name, grid=grid, in_specs=in_specs, out_specs=out_specs, out_shape=out_shape,
        scratch_shapes=scratch, compiler_params=_cparams(vmem),
    )(*args)
    return res


def _term(a, b, mode, acc=0, pa=(), pb=()):
    return (a, b, mode, acc, tuple(pa), tuple(pb))


def _row_tile(lp):
    return _pick(lp, (384, 256, 128))


def _rmsnorm_fwd(h, gain):
    lp = h.shape[0]
    tr = _row_tile(lp)

    def body(h_ref, g_ref, z_ref):
        x = h_ref[...]
        r = lax.rsqrt(jnp.mean(x * x, axis=-1, keepdims=True) + EPS)
        z_ref[...] = (x * r * g_ref[...]).astype(BF16)

    return pl.pallas_call(
        body, name="rmsnorm_fwd", grid=(lp // tr,),
        in_specs=[pl.BlockSpec((tr, D), lambda i: (i, 0)), pl.BlockSpec((1, D), lambda i: (0, 0))],
        out_specs=pl.BlockSpec((tr, D), lambda i: (i, 0)),
        out_shape=jax.ShapeDtypeStruct((lp, D), BF16),
    )(h, gain)


def _loss_head(h, target, gain):
    lp = h.shape[0]
    tr = 128

    def body(h_ref, t_ref, g_ref, loss_ref, dh_ref, dhb_ref, dg_ref):
        i = pl.program_id(0)

        @pl.when(i == 0)
        def _():
            loss_ref[...] = jnp.zeros_like(loss_ref)
            dg_ref[...] = jnp.zeros_like(dg_ref)
            dh_ref[...] = jnp.zeros_like(dh_ref)
            dhb_ref[...] = jnp.zeros_like(dhb_ref)

        @pl.when(i != 0)
        def _():
            x = h_ref[...]
            g = g_ref[...]
            r = lax.rsqrt(jnp.mean(x * x, axis=-1, keepdims=True) + EPS)
            xh = x * r
            y = xh * g
            err = y - t_ref[...]
            loss_ref[...] += 0.5 * jnp.sum(jnp.sum(err * err, axis=-1, keepdims=True), axis=0, keepdims=True) / D
            dy = err * (1.0 / D)
            dg_ref[...] += jnp.sum(dy * xh, axis=0, keepdims=True)
            dxh = dy * g
            dx = r * (dxh - xh * jnp.mean(dxh * xh, axis=-1, keepdims=True))
            dh_ref[...] = dx
            dhb_ref[...] = dx.astype(BF16)

    row = pl.BlockSpec((tr, D), lambda i: (i, 0))
    vec = pl.BlockSpec((1, D), lambda i: (0, 0))
    return pl.pallas_call(
        body, name="loss_head", grid=(lp // tr,),
        in_specs=[row, pl.BlockSpec((tr, D), lambda i: (jnp.maximum(i - 1, 0), 0)), vec],
        out_specs=[pl.BlockSpec((1, 1), lambda i: (0, 0)), row, row, vec],
        out_shape=[jax.ShapeDtypeStruct((1, 1), F32), jax.ShapeDtypeStruct((lp, D), F32),
                   jax.ShapeDtypeStruct((lp, D), BF16), jax.ShapeDtypeStruct((1, D), F32)],
    )(h, target, gain)


def _silu_parts(g):
    s = _sigmoid(g)
    return g * s, s * (1.0 + g * (1.0 - s))


def _residual_norm_epi(scale, with_norm):
    def epi(accs, exs, row0):
        h2 = exs[0] + scale * accs[0]
        if not with_norm:
            return [h2]
        r = lax.rsqrt(jnp.mean(h2 * h2, axis=-1, keepdims=True) + EPS)
        return [h2, h2 * r * exs[1]]
    return epi


def _norm_bwd_epi(accs, exs, row0):
    dz = accs[0]
    x, res, g = exs
    r = lax.rsqrt(jnp.mean(x * x, axis=-1, keepdims=True) + EPS)
    xh = x * r
    dxh = dz * g
    dx = r * (dxh - xh * jnp.mean(dxh * xh, axis=-1, keepdims=True))
    rows = row0 + lax.broadcasted_iota(jnp.int32, (dz.shape[0], 1), 0)
    dh = jnp.where(rows >= NULL, res + dx, 0.0)
    return [dh, dh, jnp.sum(dz * xh, axis=0, keepdims=True)]


def _norm_bwd_outs(lp):
    return [((lp, D), F32, "mn", 0, ()), ((lp, D), BF16, "mn", 0, ()), ((1, D), F32, "nsum", 0, ())]


def _ffn_fwd(h, z, wg_t, wu_t, wd, pre, next_gain):
    lp = h.shape[0]
    tm = _tm(lp)

    def up_epi(accs, exs, row0):
        g, u = accs
        sg, _ = _silu_parts(g)
        return [g, u, sg * u]

    bshape = (lp, DFF)
    g_, u_, act = _mm("ffn_up", lp, DFF, [_term(z, wg_t, "nt", 0, (), pre), _term(z, wu_t, "nt", 1, (), pre)],
                      [(bshape, BF16, "mn", 0, ())] * 3, up_epi, tm=tm, tn=DFF, nsub=DFF // MXU_N, vmem=VMEM_BIG)

    with_norm = next_gain is not None
    res = _mm("ffn_down", lp, D, [_term(act, wd, "nn", 0, (), pre)],
              [((lp, D), F32, "mn", 0, ())] + ([((lp, D), BF16, "mn", 0, ())] if with_norm else []),
              _residual_norm_epi(0.5, with_norm),
              extras=[(h, "mn", 0, ())] + ([(next_gain, "n", 0, ())] if with_norm else []),
              tm=tm, tn=D, i_outer=True, vmem=VMEM_BIG)
    return res[0], (res[1] if with_norm else None), dict(h=h, z=z, g=g_, u=u_, act=act)


def _dw(name, a, b, m, n, scale=1.0):
    lp = a.shape[0]
    tm = _pick(m, (2944, 1408, 1024, 512, 256, 128))
    tn = _pick(n, (1024, 512, 256, 128)) if tm <= 1408 else _pick(n, (512, 256, 128))
    nk = lp // _pick(lp, (2112, 256, 128))

    def epi(accs, exs, row0):
        return [accs[0] * scale]

    (w,) = _mm(name, m, n, [_term(a, b, "tn")], [((m, n), BF16, "mn", 0, ())], epi, tm=tm, tn=tn, nk=nk,
               i_outer=True, vmem=VMEM_BIG)
    return w


def _ffn_bwd(dh, dhb, sv, gain, wg_t, wu_t, wd, pre):
    lp = dh.shape[0]
    tm = _tm(lp)

    def dact_epi(accs, exs, row0):
        g = exs[0].astype(F32)
        u = exs[1].astype(F32)
        da = 0.5 * accs[0]
        sg, dsg = _silu_parts(g)
        return [da * u * dsg, da * sg]

    dg_, du_ = _mm("ffn_dact", lp, DFF, [_term(dhb, wd, "nt", 0, (), pre)],
                   [((lp, DFF), BF16, "mn", 0, ())] * 2, dact_epi,
                   extras=[(sv["g"], "mn", 0, ()), (sv["u"], "mn", 0, ())], tm=tm, tn=DFF, nsub=DFF // MXU_N,
                   vmem=VMEM_BIG)
    d_wd = _dw("dw_down", sv["act"], dhb, DFF, D, 0.5)
    d_wg = _dw("dw_gate", dg_, sv["z"], DFF, D)
    d_wu = _dw("dw_up", du_, sv["z"], DFF, D)

    nk = 1
    dh2, dhb2, dgain = _mm("ffn_dz", lp, D, [_term(dg_, wg_t, "nn", 0, (), pre), _term(du_, wu_t, "nn", 0, (), pre)],
                           _norm_bwd_outs(lp), _norm_bwd_epi,
                           extras=[(sv["h"], "mn", 0, ()), (dh, "mn", 0, ()), (gain, "n", 0, ())],
                           tm=tm, tn=D, nk=nk, vmem=VMEM_BIG)
    return dh2, dhb2, dgain, d_wg, d_wu, d_wd


def _gla_gates(hin, w2p, b2p):
    lp = hin.shape[0]
    tr = _row_tile(lp)
    bf, bb = P_OFF["lrf"] // HP, P_OFF["lrb"] // HP

    def body(lf_ref, lb_ref, w_ref, b_ref, o_ref, c_ref):
        i = pl.program_id(0)
        rows = i * tr + lax.broadcasted_iota(jnp.int32, (tr, 1), 0)
        r = lax.broadcasted_iota(jnp.int32, (tr, tr), 0)
        c = lax.broadcasted_iota(jnp.int32, (tr, tr), 1)
        same = (r // CHUNK) == (c // CHUNK)
        for d, l_ref in enumerate((lf_ref, lb_ref)):
            logit = _dot3(l_ref[...].astype(F32), w_ref[d]) + b_ref[d]
            g = jnp.where(rows >= NULL, jax.nn.log_sigmoid(logit) * (1.0 / GLA_TAU), 0.0)
            o_ref[d] = g
            tmat = jnp.where(same & ((r >= c) if d == 0 else (r <= c)), 1.0, 0.0).astype(BF16)
            c_ref[d] = _dot_sel(tmat, g)

    spec = pl.BlockSpec((2, tr, 512), lambda i: (0, i, 0))
    return pl.pallas_call(
        body, name="gla_gates", grid=(lp // tr,),
        in_specs=[pl.BlockSpec((tr, HP), lambda i: (i, bf)), pl.BlockSpec((tr, HP), lambda i: (i, bb)),
                  pl.BlockSpec((2, HP, 512), lambda i: (0, 0, 0)), pl.BlockSpec((2, 1, 512), lambda i: (0, 0, 0))],
        out_specs=[spec, spec],
        out_shape=[jax.ShapeDtypeStruct((2, lp, 512), F32)] * 2,
    )(hin, hin, w2p, b2p)


def _gla_rows(lp):
    return _pick(lp, (384, 256, 128))


def _tri(d):
    r = lax.broadcasted_iota(jnp.int32, (CHUNK, CHUNK), 0)
    c = lax.broadcasted_iota(jnp.int32, (CHUNK, CHUNK), 1)
    return (r >= c) if d == 0 else (r <= c)


def _gla_fwd(hin, gates):
    lp = hin.shape[0]
    rb = _gla_rows(lp)
    nb = lp // rb
    cpb = rb // CHUNK
    nchunk = lp // CHUNK
    qo, ko, vo = P_OFF["qa"] // 512, P_OFF["ka"] // 512, P_OFF["va"] // 512
    scale = GLA_DK ** -0.5

    def body(qf, kf, vf, gf, qb, kb, vb_, gb, of, ob, sf, sb, st):
        @pl.when(pl.program_id(0) == 0)
        def _():
            st[...] = jnp.zeros_like(st)

        ins = ((qf, kf, vf, gf, of, sf), (qb, kb, vb_, gb, ob, sb))
        for ci in range(cpb):
            for d in range(2):
                q_ref, k_ref, v_ref, g_ref, o_ref, s_ref = ins[d]
                tri = _tri(d)
                c = ci if d == 0 else cpb - 1 - ci
                rows = slice(c * CHUNK, (c + 1) * CHUNK)
                for h in range(GLA_H):
                    sl = slice(h * HP, (h + 1) * HP)
                    q = q_ref[rows, sl].astype(F32) * scale
                    k = k_ref[rows, sl].astype(F32)
                    v = v_ref[rows, sl].astype(F32)
                    b = g_ref[rows, sl]
                    btot = b[CHUNK - 1:CHUNK] if d == 0 else b[0:1]
                    qd = (q * jnp.exp(b)).astype(BF16)
                    ki = (k * jnp.exp(-b)).astype(BF16)
                    ke = (k * jnp.exp(btot - b)).astype(BF16)
                    vb = v.astype(BF16)
                    att = jnp.where(tri, _dot(qd, ki, "nt"), 0.0)
                    s_prev = st[d, h]
                    o_ref[rows, sl] = _dot(att.astype(BF16), vb) + _dot(qd, s_prev.astype(BF16), "nt")
                    s_ref[h, c] = s_prev
                    st[d, h] = s_prev * jnp.exp(btot) + _dot(vb, ke, "tn")

    def specs(off):
        return (pl.BlockSpec((rb, 512), lambda b: (b, off)), pl.BlockSpec((rb, 512), lambda b: (nb - 1 - b, off)))

    (qf, qb), (kf, kb), (vf, vb2) = specs(qo), specs(ko), specs(vo)
    gf = pl.BlockSpec((None, rb, 512), lambda b: (0, b, 0))
    gb = pl.BlockSpec((None, rb, 512), lambda b: (1, nb - 1 - b, 0))
    of, ob = specs(0)
    sf = pl.BlockSpec((GLA_H, cpb, HP, HP), lambda b: (0, b, 0, 0))
    sb = pl.BlockSpec((GLA_H, cpb, HP, HP), lambda b: (0, nb - 1 - b, 0, 0))
    osh = jax.ShapeDtypeStruct((lp, GLA_H * HP), F32)
    ssh = jax.ShapeDtypeStruct((GLA_H, nchunk, HP, HP), F32)
    return pl.pallas_call(
        body, name="gla_fwd", grid=(nb,),
        in_specs=[qf, kf, vf, gf, qb, kb, vb2, gb], out_specs=[of, ob, sf, sb], out_shape=[osh, osh, ssh, ssh],
        scratch_shapes=[pltpu.VMEM((2, GLA_H, HP, HP), F32)], compiler_params=_cparams(VMEM_BIG),
    )(hin, hin, hin, gates, hin, hin, hin, gates)


def _gla_bwd(hin, gates, states, do):
    lp = hin.shape[0]
    rb = _gla_rows(lp)
    nb = lp // rb
    cpb = rb // CHUNK
    qo, ko, vo = P_OFF["qa"] // 512, P_OFF["ka"] // 512, P_OFF["va"] // 512
    scale = GLA_DK ** -0.5

    def body(qf, kf, vf, gf, sf, dof, qb, kb, vb_, gb, sb, dob,
             dqf, dkf, dvf, dgf, dqb, dkb, dvb, dgb, dst):
        @pl.when(pl.program_id(0) == 0)
        def _():
            dst[...] = jnp.zeros_like(dst)

        ins = ((qf, kf, vf, gf, sf, dof, dqf, dkf, dvf, dgf), (qb, kb, vb_, gb, sb, dob, dqb, dkb, dvb, dgb))
        for ci in range(cpb):
            for d in range(2):
                q_ref, k_ref, v_ref, g_ref, s_ref, do_ref, dq_ref, dk_ref, dv_ref, dg_ref = ins[d]
                tri, tri_t = _tri(d), _tri(1 - d)
                edge = lax.broadcasted_iota(jnp.int32, (CHUNK, 1), 0) == (CHUNK - 1 if d == 0 else 0)
                c = cpb - 1 - ci if d == 0 else ci
                rows = slice(c * CHUNK, (c + 1) * CHUNK)
                for h in range(GLA_H):
                    sl = slice(h * HP, (h + 1) * HP)
                    q = q_ref[rows, sl].astype(F32) * scale
                    k = k_ref[rows, sl].astype(F32)
                    v = v_ref[rows, sl].astype(F32)
                    dout = do_ref[rows, sl].astype(BF16)
                    b = g_ref[rows, sl]
                    btot = b[CHUNK - 1:CHUNK] if d == 0 else b[0:1]
                    e = jnp.exp(b)
                    ei = jnp.exp(-b)
                    et = jnp.exp(btot - b)
                    etot = jnp.exp(btot)
                    qd = q * e
                    ki = k * ei
                    ke = k * et
                    qdb, kib, keb, vb = qd.astype(BF16), ki.astype(BF16), ke.astype(BF16), v.astype(BF16)
                    att_t = jnp.where(tri_t, _dot(kib, qdb, "nt"), 0.0).astype(BF16)
                    d_att = jnp.where(tri, _dot(dout, vb, "nt"), 0.0).astype(BF16)
                    d_att_t = jnp.where(tri_t, _dot(vb, dout, "nt"), 0.0).astype(BF16)
                    s_prev = s_ref[h, c]
                    ds_t = dst[d, h]
                    ds_b = ds_t.astype(BF16)
                    dv = _dot(att_t, dout) + _dot(keb, ds_b, "nt")
                    d_qd = _dot(d_att, kib) + _dot(dout, s_prev.astype(BF16))
                    d_ki = _dot(d_att_t, qdb)
                    d_ke = _dot(vb, ds_b)
                    d_e = jnp.sum(s_prev * ds_t, axis=0, keepdims=True)
                    dst[d, h] = _dot(dout, qdb, "tn") + ds_t * etot
                    db = d_qd * qd - d_ki * ki - d_ke * ke
                    dbtot = jnp.sum(d_ke * ke, axis=0, keepdims=True) + d_e * etot
                    dq_ref[rows, sl] = (d_qd * e * scale).astype(BF16)
                    dk_ref[rows, sl] = (d_ki * ei + d_ke * et).astype(BF16)
                    dv_ref[rows, sl] = dv.astype(BF16)
                    dg_ref[rows, sl] = db + jnp.where(edge, dbtot, 0.0)

    def fw(off):
        return pl.BlockSpec((rb, 512), lambda b: (nb - 1 - b, off))

    def bw(off):
        return pl.BlockSpec((rb, 512), lambda b: (b, off))

    gf = pl.BlockSpec((None, rb, 512), lambda b: (0, nb - 1 - b, 0))
    gb = pl.BlockSpec((None, rb, 512), lambda b: (1, b, 0))
    sf = pl.BlockSpec((GLA_H, cpb, HP, HP), lambda b: (0, nb - 1 - b, 0, 0))
    sb = pl.BlockSpec((GLA_H, cpb, HP, HP), lambda b: (0, b, 0, 0))
    osh = jax.ShapeDtypeStruct((lp, GLA_H * HP), F32)
    osh_b = jax.ShapeDtypeStruct((lp, GLA_H * HP), BF16)
    res = pl.pallas_call(
        body, name="gla_bwd", grid=(nb,),
        in_specs=[fw(qo), fw(ko), fw(vo), gf, sf, fw(0), bw(qo), bw(ko), bw(vo), gb, sb, bw(0)],
        out_specs=[fw(0)] * 4 + [bw(0)] * 4, out_shape=[osh_b, osh_b, osh_b, osh] * 2,
        scratch_shapes=[pltpu.VMEM((2, GLA_H, HP, HP), F32)], compiler_params=_cparams(VMEM_BIG),
    )(hin, hin, hin, gates, states[0], do, hin, hin, hin, gates, states[1], do)
    return res[:4], res[4:]


def _gla_out_fwd(o2, hin, gn):
    lp = hin.shape[0]
    tr = _row_tile(lp)
    ro = P_OFF["ra"] // 512

    def body(of_ref, ob_ref, r_ref, gn_ref, a_ref):
        r = r_ref[...].astype(F32)
        sr, _ = _silu_parts(r)
        for h in range(GLA_H):
            sl = slice(h * HP, (h + 1) * HP)
            o = of_ref[:, sl] + ob_ref[:, sl]
            rs = lax.rsqrt(jnp.mean(o * o, axis=-1, keepdims=True) + EPS)
            a_ref[:, sl] = (o * rs * gn_ref[:, sl] * sr[:, sl]).astype(BF16)

    row = pl.BlockSpec((tr, 512), lambda i: (i, 0))
    return pl.pallas_call(
        body, name="gla_out_fwd", grid=(lp // tr,),
        in_specs=[row, row, pl.BlockSpec((tr, 512), lambda i: (i, ro)), pl.BlockSpec((1, 512), lambda i: (0, 0))],
        out_specs=row,
        out_shape=jax.ShapeDtypeStruct((lp, 512), BF16),
    )(o2[0], o2[1], hin, gn)


def _gla_out_bwd(da, o2, hin, gn):
    lp = hin.shape[0]
    tr = _row_tile(lp)
    ro = P_OFF["ra"] // 512

    def body(da_ref, of_ref, ob_ref, r_ref, gn_ref, do_ref, dr_ref, dgn_ref):
        i = pl.program_id(0)
        r = r_ref[...].astype(F32)
        sr, dsr = _silu_parts(r)
        da_v = da_ref[...]
        parts = []
        for h in range(GLA_H):
            sl = slice(h * HP, (h + 1) * HP)
            o = of_ref[:, sl] + ob_ref[:, sl]
            rs = lax.rsqrt(jnp.mean(o * o, axis=-1, keepdims=True) + EPS)
            oh = o * rs
            gn_h = gn_ref[:, sl]
            dah = da_v[:, sl]
            dr_ref[:, sl] = (dah * oh * gn_h * dsr[:, sl]).astype(BF16)
            t = dah * sr[:, sl]
            parts.append(jnp.sum(t * oh, axis=0, keepdims=True))
            doh = t * gn_h
            do_ref[:, sl] = rs * (doh - oh * jnp.mean(doh * oh, axis=-1, keepdims=True))
        part = jnp.concatenate(parts, axis=1)

        @pl.when(i == 0)
        def _():
            dgn_ref[...] = part

        @pl.when(i != 0)
        def _():
            dgn_ref[...] += part

    row = pl.BlockSpec((tr, 512), lambda i: (i, 0))
    return pl.pallas_call(
        body, name="gla_out_bwd", grid=(lp // tr,),
        in_specs=[row, row, row, pl.BlockSpec((tr, 512), lambda i: (i, ro)), pl.BlockSpec((1, 512), lambda i: (0, 0))],
        out_specs=[row, row, pl.BlockSpec((1, 512), lambda i: (0, 0))],
        out_shape=[jax.ShapeDtypeStruct((lp, 512), F32), jax.ShapeDtypeStruct((lp, 512), BF16),
                   jax.ShapeDtypeStruct((1, 512), F32)],
    )(da, o2[0], o2[1], hin, gn)


def _gla_in_bwd(gf, gb, gates, hin, w2p, others):
    lp = hin.shape[0]
    tr = _row_tile(lp)
    bf, bb = P_OFF["lrf"] // HP, P_OFF["lrb"] // HP
    names = tuple(others)

    def seg(name):
        return slice(P_OFF[name], P_OFF[name] + P_WIDTH[name])

    def body(dqf_ref, dkf_ref, dvf_ref, dgf_ref, dqb_ref, dkb_ref, dvb_ref, dgb_ref, g_ref, lf_ref, lb_ref, w_ref,
             *rest):
        other_refs, (o_ref, dw_ref, db_ref) = rest[:len(names)], rest[len(names):]
        i = pl.program_id(0)
        for n, ref in zip(names, other_refs):
            o_ref[:, seg(n)] = ref[...].astype(BF16)
        o_ref[:, seg("qa")] = (dqf_ref[...].astype(F32) + dqb_ref[...].astype(F32)).astype(BF16)
        o_ref[:, seg("ka")] = (dkf_ref[...].astype(F32) + dkb_ref[...].astype(F32)).astype(BF16)
        o_ref[:, seg("va")] = (dvf_ref[...].astype(F32) + dvb_ref[...].astype(F32)).astype(BF16)
        olr_ref = o_ref.at[:, P_OFF["lrf"]:P_OFF["lrf"] + 2 * HP]
        rows = i * tr + lax.broadcasted_iota(jnp.int32, (tr, 1), 0)
        r = lax.broadcasted_iota(jnp.int32, (tr, tr), 0)
        c = lax.broadcasted_iota(jnp.int32, (tr, tr), 1)
        same = (r // CHUNK) == (c // CHUNK)
        for d, (l_ref, dg_ref) in enumerate(((lf_ref, dgf_ref), (lb_ref, dgb_ref))):
            tmat = jnp.where(same & ((r <= c) if d == 0 else (r >= c)), 1.0, 0.0).astype(BF16)
            dg = _dot_sel(tmat, dg_ref[...])
            sig_neg = 1.0 - jnp.exp(GLA_TAU * g_ref[d])
            dlogit = jnp.where(rows >= NULL, dg * (1.0 / GLA_TAU) * sig_neg, 0.0)
            olr_ref[:, d * HP:(d + 1) * HP] = _dot3(dlogit, w_ref[d], "nt").astype(BF16)
            dw = _dot3(l_ref[...].astype(F32), dlogit, "tn")
            dbias = jnp.sum(dlogit, axis=0, keepdims=True)

            @pl.when(i == 0)
            def _():
                dw_ref[d] = dw
                db_ref[d] = dbias

            @pl.when(i != 0)
            def _():
                dw_ref[d] += dw
                db_ref[d] += dbias

    two = pl.BlockSpec((2, tr, 512), lambda i: (0, i, 0))
    row = pl.BlockSpec((tr, 512), lambda i: (i, 0))
    return pl.pallas_call(
        body, name="gla_in_bwd", grid=(lp // tr,),
        in_specs=[row] * 8 + [two, pl.BlockSpec((tr, HP), lambda i: (i, bf)),
                  pl.BlockSpec((tr, HP), lambda i: (i, bb)), pl.BlockSpec((2, HP, 512), lambda i: (0, 0, 0))] +
                 [pl.BlockSpec((tr, P_WIDTH[n]), lambda i: (i, 0)) for n in names],
        out_specs=[pl.BlockSpec((tr, D_INP), lambda i: (i, 0)),
                   pl.BlockSpec((2, HP, 512), lambda i: (0, 0, 0)), pl.BlockSpec((2, 1, 512), lambda i: (0, 0, 0))],
        out_shape=[jax.ShapeDtypeStruct((lp, D_INP), BF16), jax.ShapeDtypeStruct((2, HP, 512), F32),
                   jax.ShapeDtypeStruct((2, 1, 512), F32)],
        compiler_params=_cparams(VMEM_BIG),
    )(*gf, *gb, gates, hin, hin, w2p, *[others[n] for n in names])


def _rope_tables(lp):
    n_tok = lp - NULL - NMETA
    rows = n_tok // GRID_W
    row = np.repeat(np.arange(rows), GRID_W).astype(np.float32)
    col = np.tile(np.arange(GRID_W), rows).astype(np.float32)
    inv = (ROPE_THETA ** (-np.arange(0, 32, 2, dtype=np.float32) / 32)).astype(np.float32)
    ang = np.concatenate([row[:, None] * inv, col[:, None] * inv], axis=-1)
    ang = np.concatenate([np.zeros((NULL + NMETA, 32), np.float32), ang], axis=0)
    cos, sin = np.cos(ang).astype(np.float32), np.sin(ang).astype(np.float32)
    z16 = np.zeros((lp, 16), np.float32)
    z64 = np.zeros((lp, 64), np.float32)
    c = np.concatenate([cos[:, :16], cos[:, :16], cos[:, 16:], cos[:, 16:], z64], axis=1)
    a = np.concatenate([-sin[:, :16], z16, -sin[:, 16:], z16, z64], axis=1)
    b = np.concatenate([z16, sin[:, :16], z16, sin[:, 16:], z64], axis=1)
    return jnp.asarray(c), jnp.asarray(a), jnp.asarray(b)


def _rope(x, c, a, b):
    return x * c + pltpu.roll(x, HP - 16, 1) * a + pltpu.roll(x, 16, 1) * b


def _rope_t(dx, c, a, b):
    return dx * c + pltpu.roll(dx * a, 16, 1) + pltpu.roll(dx * b, HP - 16, 1)


def _attn_prep(hin, gq, gk, tabs):
    lp = hin.shape[0]
    tr = _row_tile(lp)
    qo, ko, vo = P_OFF["qb"] // 1024, P_OFF["kb"] // 256, P_OFF["vb"] // 256

    def body(q_ref, k_ref, v_ref, gq_ref, gk_ref, c_ref, a_ref, b_ref, oq_ref, ok_ref, ov_ref):
        c, a, b = c_ref[...], a_ref[...], b_ref[...]
        for src, g_ref, dst, nh, sc in ((q_ref, gq_ref, oq_ref, ATT_H, Q_SCALE), (k_ref, gk_ref, ok_ref, ATT_KV, 1.0)):
            for h in range(nh):
                sl = slice(h * HP, (h + 1) * HP)
                x = src[:, sl].astype(F32)
                r = lax.rsqrt(jnp.sum(x * x, axis=-1, keepdims=True) * (1.0 / HEAD_DIM) + EPS)
                dst[:, sl] = (_rope(x * r * g_ref[...], c, a, b) * sc).astype(BF16)
        lane = lax.broadcasted_iota(jnp.int32, (1, ATT_KV * HP), 1)
        ov_ref[...] = jnp.where(lane % HP == HEAD_DIM, 1.0, v_ref[...]).astype(BF16)

    tab = pl.BlockSpec((tr, HP), lambda i: (i, 0))
    vec = pl.BlockSpec((1, HP), lambda i: (0, 0))
    return pl.pallas_call(
        body, name="attn_prep", grid=(lp // tr,),
        in_specs=[pl.BlockSpec((tr, 1024), lambda i: (i, qo)), pl.BlockSpec((tr, 256), lambda i: (i, ko)),
                  pl.BlockSpec((tr, 256), lambda i: (i, vo)), vec, vec, tab, tab, tab],
        out_specs=[pl.BlockSpec((tr, 1024), lambda i: (i, 0)), pl.BlockSpec((tr, 256), lambda i: (i, 0)),
                   pl.BlockSpec((tr, 256), lambda i: (i, 0))],
        out_shape=[jax.ShapeDtypeStruct((lp, 1024), BF16), jax.ShapeDtypeStruct((lp, 256), BF16),
                   jax.ShapeDtypeStruct((lp, 256), BF16)],
    )(hin, hin, hin, gq, gk, *tabs)


def _attn_prep_bwd(dqr, dkr, hin, gq, gk, tabs):
    lp = hin.shape[0]
    tr = _row_tile(lp)
    qo, ko = P_OFF["qb"] // 1024, P_OFF["kb"] // 256

    def body(dq_ref, dk_ref, q_ref, k_ref, gq_ref, gk_ref, c_ref, a_ref, b_ref, oq_ref, ok_ref, dgq_ref, dgk_ref):
        i = pl.program_id(0)
        c, a, b = c_ref[...], a_ref[...], b_ref[...]
        for src, dsrc, g_ref, dst, dg_ref, nh, sc in (
                (q_ref, dq_ref, gq_ref, oq_ref, dgq_ref, ATT_H, Q_SCALE),
                (k_ref, dk_ref, gk_ref, ok_ref, dgk_ref, ATT_KV, 1.0)):
            acc = jnp.zeros((1, HP), F32)
            for h in range(nh):
                sl = slice(h * HP, (h + 1) * HP)
                x = src[:, sl].astype(F32)
                r = lax.rsqrt(jnp.sum(x * x, axis=-1, keepdims=True) * (1.0 / HEAD_DIM) + EPS)
                xh = x * r
                dxn = _rope_t(dsrc[:, sl] * sc, c, a, b)
                acc = acc + jnp.sum(dxn * xh, axis=0, keepdims=True)
                dxh = dxn * g_ref[...]
                dx = r * (dxh - xh * (jnp.sum(dxh * xh, axis=-1, keepdims=True) * (1.0 / HEAD_DIM)))
                dst[:, sl] = dx.astype(BF16)

            @pl.when(i == 0)
            def _():
                dg_ref[...] = acc

            @pl.when(i != 0)
            def _():
                dg_ref[...] += acc

    tab = pl.BlockSpec((tr, HP), lambda i: (i, 0))
    vec = pl.BlockSpec((1, HP), lambda i: (0, 0))
    return pl.pallas_call(
        body, name="attn_prep_bwd", grid=(lp // tr,),
        in_specs=[pl.BlockSpec((tr, 1024), lambda i: (i, 0)), pl.BlockSpec((tr, 256), lambda i: (i, 0)),
                  pl.BlockSpec((tr, 1024), lambda i: (i, qo)), pl.BlockSpec((tr, 256), lambda i: (i, ko)),
                  vec, vec, tab, tab, tab],
        out_specs=[pl.BlockSpec((tr, 1024), lambda i: (i, 0)), pl.BlockSpec((tr, 256), lambda i: (i, 0)), vec, vec],
        out_shape=[jax.ShapeDtypeStruct((lp, 1024), BF16), jax.ShapeDtypeStruct((lp, 256), BF16),
                   jax.ShapeDtypeStruct((1, HP), F32), jax.ShapeDtypeStruct((1, HP), F32)],
    )(dqr, dkr, hin, hin, gq, gk, *tabs)


QB = 128
GH = 2
Q_SCALE = HEAD_DIM ** -0.5 * math.log2(math.e)
LN2 = math.log(2.0)


def _stack(ref, g0, n):
    return jnp.concatenate([ref[:, (g0 + g) * HP:(g0 + g + 1) * HP] for g in range(n)], axis=0)


def _attn_fwd(qr, kr, vb):
    lp = qr.shape[0]
    nq = lp // QB

    def body(q_ref, k_ref, v_ref, o_ref, lse_ref):
        qb = pl.program_id(1)
        keys = lax.broadcasted_iota(jnp.int32, (1, lp), 1)
        lane = lax.broadcasted_iota(jnp.int32, (1, HP), 1)
        rows = qb * QB + lax.broadcasted_iota(jnp.int32, (QB, 1), 0)
        gh = ATT_G
        for ch in range(ATT_G // gh):
            qs = _stack(q_ref, ch * gh, gh)
            s = _dot(qs, k_ref[...], "nt")
            s = jnp.where(keys >= NULL, s, -1e30)
            m = jnp.max(s, axis=-1, keepdims=True)
            p = jnp.exp2(s - m).astype(BF16)
            o_raw = _dot(p, v_ref[...])
            l = jnp.sum(jnp.where(lane == HEAD_DIM, o_raw, 0.0), axis=-1, keepdims=True)
            o = jnp.where(lane < HEAD_DIM, o_raw / l, 0.0)
            lse = m + jnp.log2(l)
            for g in range(gh):
                sl = slice((ch * gh + g) * HP, (ch * gh + g + 1) * HP)
                o_ref[:, sl] = jnp.where(rows >= NULL, o[g * QB:(g + 1) * QB], 0.0).astype(BF16)
                lse_ref[:, sl] = jnp.broadcast_to(lse[g * QB:(g + 1) * QB], (QB, HP))

    qspec = pl.BlockSpec((QB, ATT_G * HP), lambda kv, qb: (qb, kv))
    kspec = pl.BlockSpec((lp, HP), lambda kv, qb: (0, kv))
    return pl.pallas_call(
        body, name="attn_fwd", grid=(ATT_KV, nq),
        in_specs=[qspec, kspec, kspec], out_specs=[qspec, qspec],
        out_shape=[jax.ShapeDtypeStruct((lp, ATT_H * HP), BF16), jax.ShapeDtypeStruct((lp, ATT_H * HP), F32)],
        compiler_params=_cparams(VMEM_BIG),
    )(qr, kr, vb)


def _attn_bwd(qr, kr, vb, o, lse, do):
    lp = qr.shape[0]
    nq = lp // QB

    def body(q_ref, k_ref, v_ref, o_ref, lse_ref, do_ref, dq_ref, dk_ref, dv_ref):
        qb = pl.program_id(1)

        @pl.when(qb == 0)
        def _():
            dk_ref[...] = jnp.zeros_like(dk_ref)
            dv_ref[...] = jnp.zeros_like(dv_ref)

        keys = lax.broadcasted_iota(jnp.int32, (1, lp), 1)
        k = k_ref[...]
        dk_acc, dv_acc = None, None
        for ch in range(ATT_G // GH):
            g0 = ch * GH
            qs = _stack(q_ref, g0, GH)
            dos = _stack(do_ref, g0, GH)
            os_ = _stack(o_ref, g0, GH).astype(F32)
            lse_s = jnp.concatenate([lse_ref[:, (g0 + g) * HP:(g0 + g) * HP + 1] for g in range(GH)], axis=0)
            delta = jnp.sum(dos * os_, axis=-1, keepdims=True) * LN2
            s = _dot(qs, k, "nt")
            p = jnp.where(keys >= NULL, jnp.exp2(s - lse_s), 0.0)
            dob = dos.astype(BF16)
            dp = _dot((dos * LN2).astype(BF16), v_ref[...], "nt")
            ds = (p * (dp - delta)).astype(BF16)
            dq = _dot(ds, k)
            for g in range(GH):
                dq_ref[:, (g0 + g) * HP:(g0 + g + 1) * HP] = dq[g * QB:(g + 1) * QB]
            dv_c = _dot(p.astype(BF16), dob, "tn")
            dk_c = _dot(ds, qs, "tn")
            dv_acc = dv_c if dv_acc is None else dv_acc + dv_c
            dk_acc = dk_c if dk_acc is None else dk_acc + dk_c
        dv_ref[...] += dv_acc
        dk_ref[...] += dk_acc

    qspec = pl.BlockSpec((QB, ATT_G * HP), lambda kv, qb: (qb, kv))
    kspec = pl.BlockSpec((lp, HP), lambda kv, qb: (0, kv))
    return pl.pallas_call(
        body, name="attn_bwd", grid=(ATT_KV, nq),
        in_specs=[qspec, kspec, kspec, qspec, qspec, qspec], out_specs=[qspec, kspec, kspec],
        out_shape=[jax.ShapeDtypeStruct((lp, ATT_H * HP), F32), jax.ShapeDtypeStruct((lp, ATT_KV * HP), F32),
                   jax.ShapeDtypeStruct((lp, ATT_KV * HP), F32)],
        compiler_params=_cparams(VMEM_BIG),
    )(qr, kr, vb, o, lse, do)


def _mixer_fwd(h, z, wl, l, tabs, next_gain):
    lp = h.shape[0]
    tm = _tm(lp)

    def id_epi(accs, exs, row0):
        return [accs[0]]

    (hin,) = _mm("in_proj", lp, D_INP, [_term(z, wl["win_t"], "nt", 0, (), (l,))], [((lp, D_INP), BF16, "mn", 0, ())],
                 id_epi, tm=_tm_wide(lp), tn=D_INP // 2, vmem=VMEM_BIG)
    gates, cum = _gla_gates(hin, wl["w2p"][l], wl["b2p"][l])
    o_f, o_b, s_f, s_b = _gla_fwd(hin, cum)
    o2, states = (o_f, o_b), (s_f, s_b)
    a = _gla_out_fwd(o2, hin, wl["gn"][l])
    qr, kr, vb = _attn_prep(hin, wl["gq"][l], wl["gk"][l], tabs)
    b, lse = _attn_fwd(qr, kr, vb)

    def merge_epi(accs, exs, row0):
        pa, pb = accs
        ga, gb, bma, bmb = exs
        y = _sigmoid(ga + bma) * pa + _sigmoid(gb + bmb) * pb
        return [y, pa, pb]

    y, pa, pb = _mm("merge", lp, D, [_term(a, wl["wpa_t"], "nt", 0, (), (l,)), _term(b, wl["wpb_t"], "nt", 1, (), (l,))],
                    [((lp, D), BF16, "mn", 0, ())] * 3, merge_epi,
                    extras=[(hin, "mn", P_OFF["ga"] // D, ()), (hin, "mn", P_OFF["gb"] // D, ()),
                            (wl["bm"], "n", 0, (l, 0)), (wl["bm"], "n", 0, (l, 1))],
                    tm=_tm_wide(lp), tn=D, nsub=D // MXU_N, i_outer=True, vmem=VMEM_BIG)

    h2, z2 = _mm("out_proj", lp, D, [_term(y, wl["wout"], "nn", 0, (), (l,))],
                 [((lp, D), F32, "mn", 0, ()), ((lp, D), BF16, "mn", 0, ())], _residual_norm_epi(1.0, True),
                 extras=[(h, "mn", 0, ()), (next_gain, "n", 0, ())], tm=_tm_wide(lp), tn=D, i_outer=True, vmem=VMEM_BIG)
    sv = dict(h=h, z=z, hin=hin, gates=gates, cum=cum, o2=o2, states=states, a=a, qr=qr, kr=kr, vb=vb, b=b, lse=lse,
              y=y, pa=pa, pb=pb)
    return h2, z2, sv


def _mixer_bwd(dh, dhb, sv, gain, wl, l, tabs):
    lp = dh.shape[0]
    tm = _tm(lp)
    hin = sv["hin"]

    def merge_bwd_epi(accs, exs, row0):
        dy = accs[0]
        ga, gb, pa, pb, bma, bmb = exs
        sa = _sigmoid(ga + bma)
        sb = _sigmoid(gb + bmb)
        dga = dy * pa.astype(F32) * sa * (1.0 - sa)
        dgb = dy * pb.astype(F32) * sb * (1.0 - sb)
        return [dy * sa, dy * sb, dga, dgb, jnp.sum(dga, axis=0, keepdims=True), jnp.sum(dgb, axis=0, keepdims=True)]

    big = ((lp, D), BF16, "mn", 0, ())
    vec = ((1, D), F32, "nsum", 0, ())
    dpa, dpb, dga, dgb, dbma, dbmb = _mm(
        "merge_bwd", lp, D, [_term(dhb, wl["wout"], "nt", 0, (), (l,))], [big, big, big, big, vec, vec], merge_bwd_epi,
        extras=[(hin, "mn", P_OFF["ga"] // D, ()), (hin, "mn", P_OFF["gb"] // D, ()), (sv["pa"], "mn", 0, ()),
                (sv["pb"], "mn", 0, ()), (wl["bm"], "n", 0, (l, 0)), (wl["bm"], "n", 0, (l, 1))],
        tm=tm, tn=D, nsub=D // MXU_N, vmem=VMEM_BIG)
    d_wout = _dw("dw_out", sv["y"], dhb, D, D)
    d_wpa_t = _dw("dw_pa", dpa, sv["a"], D, 512)
    d_wpb_t = _dw("dw_pb", dpb, sv["b"], D, ATT_H * HP)

    def id_epi(accs, exs, row0):
        return [accs[0]]

    (da,) = _mm("d_a", lp, 512, [_term(dpa, wl["wpa_t"], "nn", 0, (), (l,))], [((lp, 512), F32, "mn", 0, ())], id_epi,
                tm=_tm_wide(lp), tn=512, i_outer=True, vmem=VMEM_BIG)
    (db,) = _mm("d_b", lp, ATT_H * HP, [_term(dpb, wl["wpb_t"], "nn", 0, (), (l,))],
                [((lp, ATT_H * HP), F32, "mn", 0, ())], id_epi, tm=_tm_wide(lp), tn=512, i_outer=True, vmem=VMEM_BIG)
    d_o, d_ra, d_gn = _gla_out_bwd(da, sv["o2"], hin, wl["gn"][l])
    g_fw, g_bw = _gla_bwd(hin, sv["cum"], sv["states"], d_o)
    dqr, dkr, dvb = _attn_bwd(sv["qr"], sv["kr"], sv["vb"], sv["b"], sv["lse"], db)
    d_qb, d_kb, d_gq, d_gk = _attn_prep_bwd(dqr, dkr, hin, wl["gq"][l], wl["gk"][l], tabs)
    dhin, d_w2p, d_b2p = _gla_in_bwd(g_fw, g_bw, sv["gates"], hin, wl["w2p"][l],
                                     dict(qb=d_qb, ga=dga, gb=dgb, ra=d_ra, kb=d_kb, vb=dvb))
    d_win_t = _dw("dw_in", dhin, sv["z"], D_INP, D)
    dh2, dhb2, dgain = _mm("in_proj_dz", lp, D, [_term(dhin, wl["win_t"], "nn", 0, (), (l,))], _norm_bwd_outs(lp),
                           _norm_bwd_epi, extras=[(sv["h"], "mn", 0, ()), (dh, "mn", 0, ()), (gain, "n", 0, ())],
                           tm=tm, tn=D, nk=1, vmem=VMEM_BIG)
    grads = dict(gain=dgain, wout=d_wout, wpa_t=d_wpa_t, wpb_t=d_wpb_t, win_t=d_win_t, gn=d_gn, w2p=d_w2p, b2p=d_b2p,
                 gq=d_gq, gk=d_gk, bma=dbma, bmb=dbmb)
    return dh2, dhb2, grads


def _mesh_pos():
    x, y, c = lax.axis_index("x"), lax.axis_index("y"), lax.axis_index("c")
    chips = [(1 - x, y), (x, 1 - y), (1 - x, 1 - y)]
    return x, y, c, chips


def _dev_index(x, y, c):
    return 4 * x + 2 * y + c


def _all_gather(name, shards, leads):
    nt = len(shards)

    def blk(ref, lead, idx):
        return ref.at[(slice(None),) * lead + (idx,)]

    def body(*refs):
        xs, outs = refs[:nt], refs[nt:2 * nt]
        send_sems, recv_sems, local_sems = refs[2 * nt:]
        x, y, c, chips = _mesh_pos()
        me, sibling = (x, y, c), (x, y, 1 - c)

        def copy(t, k, block, to, own=False):
            dst = blk(outs[t], leads[t], _dev_index(*block))
            return pltpu.make_async_remote_copy(
                src_ref=xs[t] if own else dst, dst_ref=dst, send_sem=send_sems.at[t, k], recv_sem=recv_sems.at[t, k],
                device_id=to, device_id_type=MESH)

        locals_ = [pltpu.make_async_copy(xs[t], blk(outs[t], leads[t], _dev_index(*me)), local_sems.at[t])
                   for t in range(nt)]
        for cp in locals_:
            cp.start()
        first = []
        for t in range(nt):
            first.append(copy(t, 0, me, sibling, own=True))
            first += [copy(t, 1 + j, me, (*chip, c), own=True) for j, chip in enumerate(chips)]
        for cp in first:
            cp.start()
        passed = []
        for j, chip in enumerate(chips):
            for t in range(nt):
                copy(t, 1 + j, (*chip, c), me).wait_recv()
                fw = copy(t, 4 + j, (*chip, c), sibling)
                fw.start()
                passed.append(fw)
        for t in range(nt):
            copy(t, 0, sibling, me).wait_recv()
        for j, chip in enumerate(chips):
            for t in range(nt):
                copy(t, 4 + j, (*chip, 1 - c), me).wait_recv()
        for cp in first + passed:
            cp.wait_send()
        for cp in locals_:
            cp.wait()

    out_shape = [jax.ShapeDtypeStruct(s.shape[:ld] + (NDEV,) + s.shape[ld:], s.dtype) for s, ld in zip(shards, leads)]
    return pl.pallas_call(
        body, name=name, in_specs=[ANY] * nt, out_specs=[ANY] * nt, out_shape=out_shape,
        scratch_shapes=[pltpu.SemaphoreType.DMA((nt, 7)), pltpu.SemaphoreType.DMA((nt, 7)),
                        pltpu.SemaphoreType.DMA((nt,))],
    )(*shards)


def _exchange_sibling(name, gs):
    nt = len(gs)

    def body(*refs):
        xs, outs = refs[:nt], refs[nt:2 * nt]
        send_sems, recv_sems = refs[2 * nt:]
        x, y, c, _ = _mesh_pos()
        sibling = (x, y, 1 - c)
        copies = []
        for t in range(nt):
            for ch in range(4):
                copies.append(pltpu.make_async_remote_copy(
                    src_ref=xs[t].at[2 * ch + (1 - c)], dst_ref=outs[t].at[ch],
                    send_sem=send_sems.at[t, ch], recv_sem=recv_sems.at[t, ch],
                    device_id=sibling, device_id_type=MESH))
        for cp in copies:
            cp.start()
        for cp in copies:
            cp.wait()

    out_shape = [jax.ShapeDtypeStruct((4,) + g.shape[1:], g.dtype) for g in gs]
    return pl.pallas_call(
        body, name=name, in_specs=[ANY] * nt, out_specs=[ANY] * nt, out_shape=out_shape,
        scratch_shapes=[pltpu.SemaphoreType.DMA((nt, 4)), pltpu.SemaphoreType.DMA((nt, 4))],
    )(*gs)


def _pair_sum(name, gs, recv):
    c = lax.axis_index("c")
    outs = []
    for t, (g, rv) in enumerate(zip(gs, recv)):
        _, r, cols = rv.shape

        def body(c_ref, g_ref, r_ref, o_ref):
            o_ref[...] = (g_ref[...].astype(F32) + r_ref[...].astype(F32)).astype(o_ref.dtype)

        outs.append(pl.pallas_call(
            body, name=f"{name}_{t}",
            grid_spec=pltpu.PrefetchScalarGridSpec(
                num_scalar_prefetch=1, grid=(4,),
                in_specs=[pl.BlockSpec((None, r, cols), lambda ch, cr: (2 * ch + cr[0], 0, 0)),
                          pl.BlockSpec((None, r, cols), lambda ch, cr: (ch, 0, 0))],
                out_specs=pl.BlockSpec((None, r, cols), lambda ch, cr: (ch, 0, 0))),
            out_shape=jax.ShapeDtypeStruct(rv.shape, rv.dtype),
        )(jnp.reshape(c, (1,)).astype(jnp.int32), g, rv))
    return outs


def _final_sum(name, ps, recv, transposed):
    chip = 2 * lax.axis_index("x") + lax.axis_index("y")
    outs = []
    for t, (p, rv) in enumerate(zip(ps, recv)):
        _, r, cols = rv.shape
        tr_out = transposed[t]
        oshape = (cols, r) if tr_out else (r, cols)

        def body(c_ref, p_ref, r0_ref, r1_ref, r2_ref, o_ref):
            acc = ((p_ref[...].astype(F32) + r0_ref[...].astype(F32)) + r1_ref[...].astype(F32)) + r2_ref[...].astype(F32)
            o_ref[...] = acc.T if tr_out else acc

        outs.append(pl.pallas_call(
            body, name=f"{name}_{t}",
            grid_spec=pltpu.PrefetchScalarGridSpec(
                num_scalar_prefetch=1, grid=(1,),
                in_specs=[pl.BlockSpec((None, r, cols), lambda i, cr: (cr[0], 0, 0))] +
                         [pl.BlockSpec((None, r, cols), lambda i, cr, j=j: (j, 0, 0)) for j in range(3)],
                out_specs=pl.BlockSpec(oshape, lambda i, cr: (0, 0))),
            out_shape=jax.ShapeDtypeStruct(oshape, F32),
        )(jnp.reshape(chip, (1,)).astype(jnp.int32), p, rv, rv, rv))
    return outs


def _sum_gathered(g):
    _, r, cols = g.shape

    def body(g_ref, o_ref):
        acc = g_ref[0]
        for d in range(1, NDEV):
            acc = acc + g_ref[d]
        o_ref[...] = acc

    return pl.pallas_call(body, name="small_sum", out_shape=jax.ShapeDtypeStruct((r, cols), F32))(g)


HBM = pl.BlockSpec(memory_space=pltpu.HBM)
SEM = pl.BlockSpec(memory_space=pltpu.SEMAPHORE)
EFFECT = pltpu.SideEffectType.DATAFLOW_SIDE_EFFECTING
NREL = NDEV - 1


def _related(k):
    x, y, c = lax.axis_index("x"), lax.axis_index("y"), lax.axis_index("c")
    px = 1 - x if k & 4 else x
    py = 1 - y if k & 2 else y
    pc = 1 - c if k & 1 else c
    return (px, py, pc), _dev_index(px, py, pc)


def _in_hbm(a):
    return pltpu.with_memory_space_constraint(a, pltpu.HBM)


ALL_RELS = tuple(range(1, NDEV))
CHIP_RELS = (4, 2, 6)


def _split_copies(xs, lands, send_sems, recv_sems, src_of, dst_of, rels):
    copies = []
    for t in range(len(xs)):
        for q, k in enumerate(rels):
            peer, peer_idx = _related(k)
            copies.append(pltpu.make_async_remote_copy(
                src_ref=src_of(xs[t], t, peer_idx), dst_ref=dst_of(lands[t], t, q, peer_idx),
                send_sem=send_sems.at[t * len(rels) + q], recv_sem=recv_sems.at[t * len(rels) + q],
                device_id=peer, device_id_type=MESH))
    return copies


def _exchange_start(name, xs, lands, src_of, dst_of, after, rels=ALL_RELS):
    nt = len(xs)

    def body(*refs):
        x_refs, land_refs = refs[:nt], refs[nt:2 * nt]
        send_sems, recv_sems = refs[2 * nt + 1], refs[2 * nt + 2]
        token = refs[-1]
        for cp in _split_copies(x_refs, land_refs, send_sems, recv_sems, src_of, dst_of, rels):
            cp.start()
        token[...] = jnp.zeros_like(token)

    res = pl.pallas_call(
        body, name=name,
        out_shape=(pltpu.SemaphoreType.DMA((nt * len(rels),)), pltpu.SemaphoreType.DMA((nt * len(rels),)),
                   *[pltpu.HBM(a.shape, a.dtype) for a in xs], *[pltpu.HBM(a.shape, a.dtype) for a in lands],
                   jax.ShapeDtypeStruct((8, 128), F32)),
        in_specs=[HBM] * (2 * nt) + [ANY],
        out_specs=(SEM, SEM, *[HBM] * (2 * nt), pl.BlockSpec(memory_space=pltpu.VMEM)),
        input_output_aliases={i: 2 + i for i in range(2 * nt)},
        compiler_params=pltpu.CompilerParams(has_side_effects=EFFECT),
    )(*[_in_hbm(a) for a in xs], *[_in_hbm(a) for a in lands], after)
    return res[0], res[1], res[2:2 + nt], res[2 + nt:2 + 2 * nt], res[-1]


def _exchange_wait(name, send_sems, recv_sems, xs, lands, src_of, dst_of, after, rels=ALL_RELS):
    nt = len(xs)

    def body(*refs):
        x_refs, land_refs = refs[:nt], refs[nt:2 * nt]
        send_sems, recv_sems = refs[2 * nt], refs[2 * nt + 1]
        for cp in _split_copies(x_refs, land_refs, send_sems, recv_sems, src_of, dst_of, rels):
            cp.wait_send()
            cp.wait_recv()

    res = pl.pallas_call(
        body, name=name,
        out_shape=(*[pltpu.HBM(a.shape, a.dtype) for a in xs], *[pltpu.HBM(a.shape, a.dtype) for a in lands]),
        in_specs=[HBM] * (2 * nt) + [SEM, SEM, ANY], out_specs=tuple([HBM] * (2 * nt)),
        input_output_aliases={i: i for i in range(2 * nt)},
        compiler_params=pltpu.CompilerParams(has_side_effects=EFFECT),
    )(*xs, *lands, send_sems, recv_sems, after)
    return res[:nt], res[nt:]


def _gather_start(name, shards, leads, after, rels=ALL_RELS):
    def src_of(x_ref, t, peer_idx):
        return x_ref

    def dst_of(land_ref, t, k, peer_idx):
        me = _dev_index(lax.axis_index("x"), lax.axis_index("y"), lax.axis_index("c"))
        return land_ref.at[(slice(None),) * leads[t] + (me,)]

    lands = [lax.empty(s.shape[:ld] + (NDEV,) + s.shape[ld:], s.dtype) for s, ld in zip(shards, leads)]
    return _exchange_start(name, shards, lands, src_of, dst_of, after, rels)


def _gather_wait(name, started, leads, after, rels=ALL_RELS):
    send_sems, recv_sems, shards, lands, _ = started

    def src_of(x_ref, t, peer_idx):
        return x_ref

    def dst_of(land_ref, t, k, peer_idx):
        return land_ref.at[(slice(None),) * leads[t] + (peer_idx,)]

    shards, lands = _exchange_wait(name, send_sems, recv_sems, shards, lands, src_of, dst_of, after, rels)
    me = _dev_index(lax.axis_index("x"), lax.axis_index("y"), lax.axis_index("c"))
    return [lax.dynamic_update_index_in_dim(g, s, me, ld) for g, s, ld in zip(lands, shards, leads)]


SIBLING_AND_CHIPS = (1,) + CHIP_RELS


def _forward_to_sibling(name, gathered, leads):
    nt = len(gathered)

    def body(*refs):
        ins, outs = refs[:nt], refs[nt:2 * nt]
        send_sems, recv_sems = refs[2 * nt:]
        x, y, c, chips = _mesh_pos()
        copies, arrivals = [], []
        for t in range(nt):
            for j, chip in enumerate(chips):
                def block(core):
                    return outs[t].at[(slice(None),) * leads[t] + (_dev_index(*chip, core),)]
                copies.append(pltpu.make_async_remote_copy(
                    src_ref=block(c), dst_ref=block(c), send_sem=send_sems.at[t, j], recv_sem=recv_sems.at[t, j],
                    device_id=(x, y, 1 - c), device_id_type=MESH))
                arrivals.append(pltpu.make_async_remote_copy(
                    src_ref=block(1 - c), dst_ref=block(1 - c), send_sem=send_sems.at[t, j], recv_sem=recv_sems.at[t, j],
                    device_id=(x, y, 1 - c), device_id_type=MESH))
        for cp in copies:
            cp.start()
        for cp in arrivals:
            cp.wait_recv()
        for cp in copies:
            cp.wait_send()

    return pl.pallas_call(
        body, name=name, in_specs=[ANY] * nt, out_specs=[ANY] * nt,
        out_shape=[jax.ShapeDtypeStruct(g.shape, g.dtype) for g in gathered],
        input_output_aliases={t: t for t in range(nt)},
        scratch_shapes=[pltpu.SemaphoreType.DMA((nt, 3)), pltpu.SemaphoreType.DMA((nt, 3))],
    )(*gathered)


def _scatter_src(x_ref, t, peer_idx):
    return x_ref.at[peer_idx]


def _scatter_dst(land_ref, t, q, peer_idx):
    return land_ref.at[q]


def _chips_src(x_ref, t, peer_idx):
    return x_ref.at[peer_idx // 2]


def _chips_start(name, ps, after):
    lands = [lax.empty((len(CHIP_RELS),) + p.shape[1:], p.dtype) for p in ps]
    return _exchange_start(name, ps, lands, _chips_src, _scatter_dst, after, CHIP_RELS)


def _chips_wait(name, started, after):
    send_sems, recv_sems, ps, lands, _ = started
    return _exchange_wait(name, send_sems, recv_sems, ps, lands, _chips_src, _scatter_dst, after, CHIP_RELS)


def _scatter_start(name, gs, after):
    lands = [lax.empty((NREL,) + g.shape[1:], g.dtype) for g in gs]
    return _exchange_start(name, gs, lands, _scatter_src, _scatter_dst, after)


def _scatter_wait(name, started, after, transposed):
    send_sems, recv_sems, gs, lands, _ = started
    gs, lands = _exchange_wait(name, send_sems, recv_sems, gs, lands, _scatter_src, _scatter_dst, after)
    me = _dev_index(lax.axis_index("x"), lax.axis_index("y"), lax.axis_index("c"))
    outs = []
    for t, (g, rv) in enumerate(zip(gs, lands)):
        _, r, cols = rv.shape
        tr_out = transposed[t]
        oshape = (cols, r) if tr_out else (r, cols)

        def body(c_ref, own_ref, rv_ref, o_ref):
            acc = own_ref[...].astype(F32)
            for k in range(NREL):
                acc = acc + rv_ref[k].astype(F32)
            o_ref[...] = acc.T if tr_out else acc

        outs.append(pl.pallas_call(
            body, name=f"{name}_sum_{t}",
            grid_spec=pltpu.PrefetchScalarGridSpec(
                num_scalar_prefetch=1, grid=(1,),
                in_specs=[pl.BlockSpec((None, r, cols), lambda i, cr: (cr[0], 0, 0)),
                          pl.BlockSpec((NREL, r, cols), lambda i, cr: (0, 0, 0))],
                out_specs=pl.BlockSpec(oshape, lambda i, cr: (0, 0))),
            out_shape=jax.ShapeDtypeStruct(oshape, F32), compiler_params=_cparams(VMEM_BIG),
        )(jnp.reshape(me, (1,)).astype(jnp.int32), g, rv))
    return outs


def _adamw(w, g, m, v):
    shape = w.shape
    cols = shape[-1]
    rows = math.prod(shape[:-1]) if len(shape) > 1 else 1
    w2, g2, m2, v2 = (jnp.reshape(t, (rows, cols)) for t in (w, g, m, v))
    tr = _pick(rows, (1024, 704, 512, 256, 128)) if rows * cols > 65536 else rows
    c1 = 1.0 / (1.0 - ADAM_B1 ** ADAM_STEP)
    c2 = 1.0 / (1.0 - ADAM_B2 ** ADAM_STEP)

    def body(w_ref, g_ref, m_ref, v_ref, d_ref, nm_ref, nv_ref):
        gv = g_ref[...]
        nm = ADAM_B1 * m_ref[...] + (1.0 - ADAM_B1) * gv
        nv = ADAM_B2 * v_ref[...] + (1.0 - ADAM_B2) * (gv * gv)
        d_ref[...] = -ADAM_LR * ((nm * c1) / (jnp.sqrt(nv * c2) + ADAM_EPS) + ADAM_WD * w_ref[...])
        nm_ref[...] = nm
        nv_ref[...] = nv

    spec = pl.BlockSpec((tr, cols), lambda i: (i, 0))
    osh = jax.ShapeDtypeStruct((rows, cols), F32)
    d, nm, nv = pl.pallas_call(
        body, name="adamw", grid=(rows // tr,), in_specs=[spec] * 4, out_specs=[spec] * 3, out_shape=[osh] * 3,
        compiler_params=_cparams(VMEM_BIG),
    )(w2, g2, m2, v2)
    return jnp.reshape(d, shape), jnp.reshape(nm, shape), jnp.reshape(nv, shape)


def _pad_heads(w, name):
    if name not in P_HEADS:
        return w
    nh, real = P_HEADS[name]
    w = jnp.reshape(w, w.shape[:-2] + (nh, real, w.shape[-1]))
    w = jnp.pad(w, [(0, 0)] * (w.ndim - 2) + [(0, HP - real), (0, 0)])
    return jnp.reshape(w, w.shape[:-3] + (nh * HP, w.shape[-1]))


def _unpad_heads(w, name):
    if name not in P_HEADS:
        return w
    nh, real = P_HEADS[name]
    w = jnp.reshape(w, w.shape[:-2] + (nh, HP, w.shape[-1]))[..., :real, :]
    return jnp.reshape(w, w.shape[:-3] + (nh * real, w.shape[-1]))


def _win_pad(win_t):
    segs, o = {}, 0
    for n, s in zip(IN_NAMES, IN_SIZES):
        segs[n] = win_t[..., o:o + s, :]
        o += s
    return jnp.concatenate([_pad_heads(segs[n], n) for n in P_ORDER], axis=-2)


def _win_unpad(win_p):
    segs = {n: _unpad_heads(win_p[..., P_OFF[n]:P_OFF[n] + P_WIDTH[n], :], n) for n in P_ORDER}
    return jnp.concatenate([segs[n] for n in IN_NAMES], axis=-2)


def _t(w):
    return jnp.swapaxes(w, -1, -2)


def _ffn_stacked(g_g, g_u, g_d):
    wg, wu, wd = (jnp.reshape(g, (2, DFF, D)) for g in (g_g, g_u, g_d))
    return [(wg, wu, wd, (j,)) for j in range(2)]


def _ffn_single(g_g, g_u, g_d):
    return tuple(jnp.reshape(g, (DFF, D)) for g in (g_g, g_u, g_d)) + ((),)


def _layer_weights(ffn, g_in, g_pa, g_pb, g_out, gains, w2, b2, bm, gn, gq, gk):
    w2p = jnp.pad(jnp.reshape(w2, (2, GLA_RANK, GLA_H, GLA_DK)), ((0, 0), (0, HP - GLA_RANK), (0, 0), (0, HP - GLA_DK)))
    b2p = jnp.pad(jnp.reshape(b2, (2, 1, GLA_H, GLA_DK)), ((0, 0), (0, 0), (0, 0), (0, HP - GLA_DK)))
    wpb_t = jnp.pad(jnp.reshape(g_pb, (D, ATT_H, HEAD_DIM)), ((0, 0), (0, 0), (0, HP - HEAD_DIM)))
    return dict(
        gains=jnp.reshape(gains, (1, 3, 1, D)), ffn=ffn,
        win_t=_win_pad(jnp.reshape(g_in, (1, D_IN, D))), wpa_t=jnp.reshape(g_pa, (1, D, 512)),
        wpb_t=jnp.reshape(wpb_t, (1, D, ATT_H * HP)), wout=jnp.reshape(g_out, (1, D, D)),
        w2p=jnp.reshape(w2p, (1, 2, HP, GLA_H * HP)), b2p=jnp.reshape(b2p, (1, 2, 1, GLA_H * HP)),
        bm=jnp.reshape(bm, (1, 2, 1, D)), gn=jnp.reshape(gn, (1, 1, GLA_H * HP)),
        gq=jnp.pad(jnp.reshape(gq, (1, 1, HEAD_DIM)), ((0, 0), (0, 0), (0, HP - HEAD_DIM))),
        gk=jnp.pad(jnp.reshape(gk, (1, 1, HEAD_DIM)), ((0, 0), (0, 0), (0, HP - HEAD_DIM))))


def _layer_fwd_lower(h, z, ffn0, gain1):
    return _ffn_fwd(h, z, *ffn0, gain1)


def _layer_fwd_upper(h, z, s0, w, tabs, next_gain):
    h, z, s1 = _mixer_fwd(h, z, w, 0, tabs, w["gains"][0, 2])
    h, z, s2 = _ffn_fwd(h, z, *w["ffn"][1], next_gain)
    return h, z, (s0, s1, s2)


def _layer_fwd(h, z, w, tabs, next_gain):
    h, z, s0 = _layer_fwd_lower(h, z, w["ffn"][0], w["gains"][0, 1])
    return _layer_fwd_upper(h, z, s0, w, tabs, next_gain)


def _layer_bwd_upper(dh, dhb, saved, w, tabs):
    _, s1, s2 = saved
    dh, dhb, dg2, dwg1, dwu1, dwd1 = _ffn_bwd(dh, dhb, s2, w["gains"][0, 2], *w["ffn"][1])
    dh, dhb, gm = _mixer_bwd(dh, dhb, s1, w["gains"][0, 1], w, 0, tabs)
    gm.update(gain2=dg2, wg1=dwg1, wu1=dwu1, wd1=dwd1)
    return dh, dhb, gm


def _layer_bwd_lower(dh, dhb, saved, w, gm):
    dh, dhb, dg0, dwg0, dwu0, dwd0 = _ffn_bwd(dh, dhb, saved[0], w["gains"][0, 0], *w["ffn"][0])
    gm.update(gain0=dg0, wg0=dwg0, wu0=dwu0, wd0=dwd0)
    return dh, dhb, gm


def _layer_bwd(dh, dhb, saved, w, tabs):
    dh, dhb, gm = _layer_bwd_upper(dh, dhb, saved, w, tabs)
    return _layer_bwd_lower(dh, dhb, saved, w, gm)


def _blocks(ts):
    return [jnp.reshape(t, (NDEV, t.shape[0] // NDEV, t.shape[1])) for t in ts]


def _upper_grads(g):
    d_in = _win_unpad(g["win_t"])
    d_pb = jnp.reshape(jnp.reshape(g["wpb_t"], (D, ATT_H, HP))[:, :, :HEAD_DIM], (D, 512))
    return _blocks([g["wg1"], g["wu1"], g["wd1"], d_in, g["wpa_t"], d_pb, g["wout"]])


def _lower_grads(g):
    return _blocks([g["wg0"], g["wu0"], g["wd0"]])


def _big_grads(g):
    return _lower_grads(g) + _upper_grads(g)


def kernel(x, meta_tokens, norm_gains, ffn_w_gate, ffn_w_up, ffn_w_down, w_in, gla_w2, gla_b2, gla_gn, q_norm, k_norm, w_pa, w_pb, b_merge, w_out, final_norm, loss_target, m_meta_tokens, m_norm_gains, m_ffn_w_gate, m_ffn_w_up, m_ffn_w_down, m_w_in, m_gla_w2, m_gla_b2, m_gla_gn, m_q_norm, m_k_norm, m_w_pa, m_w_pb, m_b_merge, m_w_out, m_final_norm, v_meta_tokens, v_norm_gains, v_ffn_w_gate, v_ffn_w_up, v_ffn_w_down, v_w_in, v_gla_w2, v_gla_b2, v_gla_gn, v_q_norm, v_k_norm, v_w_pa, v_w_pb, v_b_merge, v_w_out, v_final_norm):
    dev = _dev_index(lax.axis_index("x"), lax.axis_index("y"), lax.axis_index("c"))
    sh_g = _t(ffn_w_gate).astype(BF16)
    sh_u = _t(ffn_w_up).astype(BF16)
    sh_d = ffn_w_down.astype(BF16)
    sh_in = _t(w_in).astype(BF16)
    sh_pa = _t(w_pa).astype(BF16)
    sh_pb = _t(w_pb).astype(BF16)
    sh_out = w_out.astype(BF16)
    small = jnp.concatenate([jnp.reshape(t, (-1, 128)) for t in
                             (meta_tokens, norm_gains, gla_w2, gla_b2, b_merge)], axis=0)
    small = jnp.pad(small, ((0, 2), (0, 0)))
    def shards(l):
        return [sh_g[l], sh_u[l], sh_d[l], sh_in[l], sh_pa[l], sh_pb[l], sh_out[l]]

    w_leads = [1, 1, 1, 0, 0, 0, 0]
    *g0_ffn0, g_small = _all_gather("gather_layer0", [sh_g[0, 0], sh_u[0, 0], sh_d[0, 0], small], [0, 0, 0, 0])
    rest0 = [sh_g[0, 1], sh_u[0, 1], sh_d[0, 1], sh_in[0], sh_pa[0], sh_pb[0], sh_out[0]]
    rest_leads = [0] * len(rest0)
    started0 = _gather_start("gather_start_0", rest0, rest_leads, g_small, SIBLING_AND_CHIPS)
    meta_full = jnp.reshape(jnp.transpose(g_small[:, 0:16], (1, 0, 2)), (NMETA, D)) + started0[4][0, 0]
    gains_full = jnp.reshape(jnp.transpose(jnp.reshape(g_small[:, 16:28], (NDEV, DEPTH, 3, 128)), (1, 2, 0, 3)), (DEPTH, 3, D))
    w2_full = jnp.reshape(jnp.transpose(jnp.reshape(g_small[:, 28:60], (NDEV, DEPTH, 2, GLA_RANK, 32)), (1, 2, 3, 0, 4)),
                          (DEPTH, 2, GLA_RANK, 256))
    b2_full = jnp.reshape(jnp.transpose(jnp.reshape(g_small[:, 60:62], (NDEV, DEPTH, 2, 32)), (1, 2, 0, 3)), (DEPTH, 2, 256))
    bm_full = jnp.reshape(jnp.transpose(jnp.reshape(g_small[:, 62:70], (NDEV, DEPTH, 2, 128)), (1, 2, 0, 3)), (DEPTH, 2, D))

    def layer_weights(l, ffn, others, gains_l):
        return _layer_weights(ffn, *others, gains_l, w2_full[l], b2_full[l], bm_full[l], gla_gn[l], q_norm[l], k_norm[l])

    xl = x[0]
    lp = xl.shape[0] + NULL + NMETA
    tabs = _rope_tables(lp)
    h = jnp.concatenate([jnp.zeros((NULL, D), F32), meta_full, xl], axis=0)
    weights, saved, started = [], [], {}
    z = _rmsnorm_fwd(h, jnp.reshape(gains_full[0, 0], (1, D)))
    for l in range(DEPTH):
        next_gain = jnp.reshape(gains_full[l + 1, 0], (1, D)) if l + 1 < DEPTH else None
        if l == 0:
            ffn0 = _ffn_single(*g0_ffn0)
            h, z, s0 = _layer_fwd_lower(h, z, ffn0, jnp.reshape(gains_full[0, 1], (1, D)))
            rest = _forward_to_sibling("gather_forward_0", _gather_wait("gather_wait_0", started0, rest_leads, h,
                                                                         SIBLING_AND_CHIPS), rest_leads)
            started[1] = _gather_start("gather_start_1", shards(1), w_leads, rest[0])
            weights.append(layer_weights(0, [ffn0, _ffn_single(*rest[:3])], rest[3:], gains_full[0]))
            z = z + started[1][4][0, 0].astype(BF16)
            h, z, sv = _layer_fwd_upper(h, z, s0, weights[0], tabs, next_gain)
        else:
            tok = jnp.zeros((), F32)
            if l < DEPTH - 1:
                started[l + 1] = _gather_start(f"gather_start_{l + 1}", shards(l + 1), w_leads, h)
                tok = started[l + 1][4][0, 0]
            gathered = _gather_wait(f"gather_wait_{l}", started[l], w_leads, h)
            weights.append(layer_weights(l, _ffn_stacked(*gathered[:3]), gathered[3:], gains_full[l] + tok))
            h, z, sv = _layer_fwd(h, z, weights[l], tabs, next_gain)
        saved.append(sv)
    loss, dh, dhb, d_final = _loss_head(h, loss_target[0], jnp.reshape(final_norm, (1, D)))
    loss = lax.psum(loss[0, 0], ("x", "y", "c"))

    grads, scattering = [None] * DEPTH, {}
    tok = jnp.zeros((), F32)
    for l in reversed(range(DEPTH)):
        w = dict(weights[l], gains=weights[l]["gains"] + tok)
        if l > 0:
            dh, dhb, grads[l] = _layer_bwd(dh, dhb, saved[l], w, tabs)
            scattering[l] = _scatter_start(f"scatter_start_{l}", _big_grads(grads[l]), dhb)
            tok = scattering[l][4][0, 0]
        else:
            dh, dhb, gm = _layer_bwd_upper(dh, dhb, saved[l], w, tabs)
            ups = _upper_grads(gm)
            pair = _pair_sum("rs_pair_up", ups, _exchange_sibling("rs_sibling_up", ups))
            scattering[l] = _chips_start(f"scatter_start_{l}", pair, dhb)
            dhb = dhb + scattering[l][4][0, 0].astype(BF16)
            dh, dhb, grads[l] = _layer_bwd_lower(dh, dhb, saved[l], w, gm)
    grad_x = dh[NULL + NMETA:][None]
    t_lower, t_upper = [False, False, False], [False, False, False, True, True, True, False]
    lows = _lower_grads(grads[0])
    pair_lo = _pair_sum("rs_pair_lo", lows, _exchange_sibling("rs_sibling_lo", lows))
    started_lo = _chips_start("scatter_start_lo", pair_lo, dhb)
    red = [None] * DEPTH
    for l in reversed(range(1, DEPTH)):
        red[l] = _scatter_wait(f"scatter_wait_{l}", scattering[l], started_lo[4], t_lower + t_upper)
    pair, recv = _chips_wait("scatter_wait_0", scattering[0], red[1][-1])
    red_upper = _final_sum("rs_sum_up", pair, recv, t_upper)
    pair_lo, recv_lo = _chips_wait("scatter_wait_lo", started_lo, red_upper[-1])
    red[0] = _final_sum("rs_sum_lo", pair_lo, recv_lo, t_lower) + red_upper
    g_gate = jnp.stack([jnp.stack([red[l][0], red[l][3]]) for l in range(DEPTH)])
    g_up = jnp.stack([jnp.stack([red[l][1], red[l][4]]) for l in range(DEPTH)])
    g_down = jnp.stack([jnp.stack([red[l][2], red[l][5]]) for l in range(DEPTH)])
    g_win = jnp.stack([red[l][6] for l in range(DEPTH)])
    g_wpa = jnp.stack([red[l][7] for l in range(DEPTH)])
    g_wpb = jnp.stack([red[l][8] for l in range(DEPTH)])
    g_wout = jnp.stack([red[l][9] for l in range(DEPTH)])

    d_meta = dh[NULL:NULL + NMETA]
    d_gains = jnp.stack([jnp.concatenate([grads[l]["gain0"], grads[l]["gain"], grads[l]["gain2"]], axis=0)
                         for l in range(DEPTH)])
    d_w2 = jnp.stack([jnp.reshape(jnp.reshape(grads[l]["w2p"], (2, HP, GLA_H, HP))[:, :GLA_RANK, :, :GLA_DK],
                                  (2, GLA_RANK, 256)) for l in range(DEPTH)])
    d_b2 = jnp.stack([jnp.reshape(jnp.reshape(grads[l]["b2p"], (2, GLA_H, HP))[:, :, :GLA_DK], (2, 256))
                      for l in range(DEPTH)])
    d_gn = jnp.stack([grads[l]["gn"][0] for l in range(DEPTH)])
    d_gq = jnp.stack([grads[l]["gq"][0, :HEAD_DIM] for l in range(DEPTH)])
    d_gk = jnp.stack([grads[l]["gk"][0, :HEAD_DIM] for l in range(DEPTH)])
    d_bm = jnp.stack([jnp.concatenate([grads[l]["bma"], grads[l]["bmb"]], axis=0) for l in range(DEPTH)])
    parts = [d_meta, d_gains, d_w2, d_b2, d_gn, d_gq, d_gk, d_bm, d_final[0]]
    sizes = [p.size for p in parts]
    flat = jnp.concatenate([jnp.reshape(p, (-1,)) for p in parts])
    flat = jnp.reshape(flat, (-1, 128))
    nrow = flat.shape[0]
    flat = jnp.pad(flat, ((0, (-nrow) % 8), (0, 0)))
    (g_flat,) = _all_gather("gather_small_grads", [flat], [0])
    tot = jnp.reshape(_sum_gathered(g_flat), (-1,))
    full, o = [], 0
    for p, s in zip(parts, sizes):
        full.append(jnp.reshape(tot[o:o + s], p.shape))
        o += s
    f_meta, f_gains, f_w2, f_b2, f_gn, f_gq, f_gk, f_bm, f_final = full

    def mine(t, width):
        return lax.dynamic_slice_in_dim(t, dev * width, width, axis=t.ndim - 1)

    g_small = dict(meta_tokens=mine(f_meta, 128), norm_gains=mine(f_gains, 128), gla_w2=mine(f_w2, 32),
                   gla_b2=mine(f_b2, 32), gla_gn=f_gn, q_norm=f_gq, k_norm=f_gk, b_merge=mine(f_bm, 128),
                   final_norm=f_final)
    gr = dict(g_small, ffn_w_gate=g_gate, ffn_w_up=g_up, ffn_w_down=g_down, w_in=g_win, w_pa=g_wpa, w_pb=g_wpb,
              w_out=g_wout)
    ws = dict(meta_tokens=meta_tokens, norm_gains=norm_gains, ffn_w_gate=ffn_w_gate, ffn_w_up=ffn_w_up,
              ffn_w_down=ffn_w_down, w_in=w_in, gla_w2=gla_w2, gla_b2=gla_b2, gla_gn=gla_gn, q_norm=q_norm,
              k_norm=k_norm, w_pa=w_pa, w_pb=w_pb, b_merge=b_merge, w_out=w_out, final_norm=final_norm)
    ms = dict(meta_tokens=m_meta_tokens, norm_gains=m_norm_gains, ffn_w_gate=m_ffn_w_gate, ffn_w_up=m_ffn_w_up,
              ffn_w_down=m_ffn_w_down, w_in=m_w_in, gla_w2=m_gla_w2, gla_b2=m_gla_b2, gla_gn=m_gla_gn, q_norm=m_q_norm,
              k_norm=m_k_norm, w_pa=m_w_pa, w_pb=m_w_pb, b_merge=m_b_merge, w_out=m_w_out, final_norm=m_final_norm)
    vs = dict(meta_tokens=v_meta_tokens, norm_gains=v_norm_gains, ffn_w_gate=v_ffn_w_gate, ffn_w_up=v_ffn_w_up,
              ffn_w_down=v_ffn_w_down, w_in=v_w_in, gla_w2=v_gla_w2, gla_b2=v_gla_b2, gla_gn=v_gla_gn, q_norm=v_q_norm,
              k_norm=v_k_norm, w_pa=v_w_pa, w_pb=v_w_pb, b_merge=v_b_merge, w_out=v_w_out, final_norm=v_final_norm)
    names = ["meta_tokens", "norm_gains", "ffn_w_gate", "ffn_w_up", "ffn_w_down", "w_in", "gla_w2", "gla_b2", "gla_gn",
             "q_norm", "k_norm", "w_pa", "w_pb", "b_merge", "w_out", "final_norm"]
    deltas, new_m, new_v = [], [], []
    for n in names:
        if n in ("ffn_w_gate", "ffn_w_up"):
            dlt, nm, nv = (_t(r) for r in _adamw(_t(ws[n]), gr[n], _t(ms[n]), _t(vs[n])))
            gr[n] = _t(gr[n])
        else:
            dlt, nm, nv = _adamw(ws[n], gr[n], ms[n], vs[n])
        deltas.append(dlt)
        new_m.append(nm)
        new_v.append(nv)
    return (loss, grad_x, *[gr[n] for n in names], *deltas, *new_m, *new_v)
```

```python
import math

import jax
import jax.numpy as jnp
import numpy as np
from jax import lax
from jax.experimental import pallas as pl
from jax.experimental.pallas import tpu as pltpu

F32 = jnp.float32
BF16 = jnp.bfloat16
MESH = pl.DeviceIdType.MESH
ANY = pl.BlockSpec(memory_space=pl.ANY)

NDEV = 8
D = 1024
DFF = 2816
DEPTH = 4
NMETA = 16
NULL = 112
GRID_W = 64
EPS = 1e-6
HP = 128
GLA_H = 4
GLA_DK = 64
GLA_RANK = 16
GLA_TAU = 16.0
CHUNK = 64
ATT_H = 8
ATT_KV = 2
ATT_G = ATT_H // ATT_KV
HEAD_DIM = 64
ROPE_THETA = 10000.0

IN_SIZES = (256, 256, 512, 512, 16, 16, 512, 128, 128, 1024, 1024)
IN_NAMES = ("qa", "ka", "va", "ra", "lrf", "lrb", "qb", "kb", "vb", "ga", "gb")
D_IN = sum(IN_SIZES)
P_ORDER = ("qb", "ga", "gb", "qa", "ka", "va", "ra", "kb", "vb", "lrf", "lrb")
P_WIDTH = dict(qb=1024, ga=1024, gb=1024, qa=512, ka=512, va=512, ra=512, kb=256, vb=256, lrf=128, lrb=128)
P_OFF = {}
_o = 0
for _n in P_ORDER:
    P_OFF[_n] = _o
    _o += P_WIDTH[_n]
D_INP = _o
P_HEADS = dict(qa=(4, 64), ka=(4, 64), qb=(8, 64), kb=(2, 64), vb=(2, 64), lrf=(1, 16), lrb=(1, 16))

ADAM_LR = 0.001
ADAM_B1 = 0.9
ADAM_B2 = 0.999
ADAM_EPS = 1e-08
ADAM_WD = 0.01
ADAM_STEP = 10

VMEM_BIG = 58 * 1024 * 1024
MXU_N = 256


def _cparams(vmem=None):
    return pltpu.CompilerParams(vmem_limit_bytes=vmem) if vmem else pltpu.CompilerParams()


def _pick(n, prefs):
    for p in prefs:
        if n % p == 0:
            return p
    return n


def _tm(lp):
    return _pick(lp, (528, 512, 256, 128))


def _tm_wide(lp):
    return _pick(lp, (1056, 512, 256, 128))


_DN = {"nn": (((1,), (0,)), ((), ())), "nt": (((1,), (1,)), ((), ())), "tn": (((0,), (0,)), ((), ()))}


def _dot(a, b, mode="nn", precision=None):
    return lax.dot_general(a, b, _DN[mode], preferred_element_type=F32, precision=precision)


def _split(x):
    hi = x.astype(BF16)
    return hi, (x - hi.astype(F32)).astype(BF16)


def _dot_sel(t, x, mode="nn"):
    hi, lo = _split(x)
    return _dot(t, hi, mode) + _dot(t, lo, mode)


def _dot3(a, b, mode="nn"):
    ah, al = _split(a)
    bh, bl = _split(b)
    return _dot(ah, bh, mode) + (_dot(ah, bl, mode) + _dot(al, bh, mode))


def _sigmoid(x):
    return 0.5 * jnp.tanh(0.5 * x) + 0.5


def _mm(name, m, n, terms, outs, epilogue, extras=(), *, tm, tn, nk=1, nsub=1, i_outer=False, vmem=None):
    gm, gn = m // tm, n // tn
    assert gm * tm == m and gn * tn == n, (name, m, n, tm, tn)
    n_acc = 1 + max(t[3] for t in terms)

    def gmap(f):
        if i_outer:
            return lambda i, j, kk: f(i, j, kk)
        return lambda j, i, kk: f(i, j, kk)

    in_specs, args = [], []
    for a, b, mode, _, pa, pb in terms:
        kdim = a.shape[-2] if mode == "tn" else a.shape[-1]
        tk = kdim // nk
        assert tk * nk == kdim
        na, nb = (None,) * len(pa), (None,) * len(pb)
        if mode == "tn":
            in_specs.append(pl.BlockSpec(na + (tk, tm), gmap(lambda i, j, kk, pa=pa: pa + (kk, i))))
        else:
            in_specs.append(pl.BlockSpec(na + (tm, tk), gmap(lambda i, j, kk, pa=pa: pa + (i, kk))))
        if mode == "nt":
            in_specs.append(pl.BlockSpec(nb + (tn, tk), gmap(lambda i, j, kk, pb=pb: pb + (j, kk))))
        else:
            in_specs.append(pl.BlockSpec(nb + (tk, tn), gmap(lambda i, j, kk, pb=pb: pb + (kk, j))))
        args += [a, b]
    for arr, kind, off, pe in extras:
        ne = (None,) * len(pe)
        if kind == "mn":
            in_specs.append(pl.BlockSpec(ne + (tm, tn), gmap(lambda i, j, kk, off=off, pe=pe: pe + (i, j + off))))
        else:
            in_specs.append(pl.BlockSpec(ne + (1, tn), gmap(lambda i, j, kk, off=off, pe=pe: pe + (0, j + off))))
        args.append(arr)
    out_shape, out_specs = [], []
    for shape, dtype, kind, off, po in outs:
        no = (None,) * len(po)
        out_shape.append(jax.ShapeDtypeStruct(shape, dtype))
        if kind == "mn":
            out_specs.append(pl.BlockSpec(no + (tm, tn), gmap(lambda i, j, kk, off=off, po=po: po + (i, j + off))))
        else:
            assert not i_outer
            out_specs.append(pl.BlockSpec(no + (1, tn), gmap(lambda i, j, kk, off=off, po=po: po + (0, j + off))))
    n_t, n_e, n_o = len(terms), len(extras), len(outs)
    i_axis = 0 if i_outer else 1

    def body(*refs):
        ins = refs[: 2 * n_t]
        exs = refs[2 * n_t: 2 * n_t + n_e]
        ors = refs[2 * n_t + n_e: 2 * n_t + n_e + n_o]
        accs = refs[2 * n_t + n_e + n_o:]
        i = pl.program_id(i_axis)
        kk = pl.program_id(2)

        def partials(cs):
            part = [None] * n_acc
            for t, (_, _, mode, ai, _, _) in enumerate(terms):
                b_ref = ins[2 * t + 1]
                b_val = b_ref[cs, :] if mode == "nt" else b_ref[:, cs]
                p = _dot(ins[2 * t][...], b_val, mode)
                part[ai] = p if part[ai] is None else part[ai] + p
            return part

        def finish(vals, cs):
            res = epilogue(vals, [e[:, cs] for e in exs], i * tm)
            for (_, dtype, kind, _, _), o_ref, v in zip(outs, ors, res):
                if kind == "mn":
                    o_ref[:, cs] = v.astype(dtype)
                else:
                    @pl.when(i == 0)
                    def _():
                        o_ref[:, cs] = v.astype(dtype)

                    @pl.when(i != 0)
                    def _():
                        o_ref[:, cs] += v.astype(dtype)

        if nk == 1:
            w = tn // nsub
            for s in range(nsub):
                cs = slice(s * w, (s + 1) * w)
                finish(partials(cs), cs)
        else:
            part = partials(slice(None))
            @pl.when(kk == 0)
            def _():
                for a_ref, p in zip(accs, part):
                    a_ref[...] = p

            @pl.when(kk != 0)
            def _():
                for a_ref, p in zip(accs, part):
                    a_ref[...] += p

            @pl.when(kk == nk - 1)
            def _():
                finish([a_ref[...] for a_ref in accs], slice(None))

    scratch = [pltpu.VMEM((tm, tn), F32) for _ in range(n_acc)] if nk > 1 else []
    grid = (gm, gn, nk) if i_outer else (gn, gm, nk)
    res = pl.pallas_call(
        body, name=name, grid=grid, in_specs=in_specs, out_specs=out_specs, out_shape=out_shape,
        scratch_shapes=scratch, compiler_params=_cparams(vmem),
    )(*args)
    return res


def _term(a, b, mode, acc=0, pa=(), pb=()):
    return (a, b, mode, acc, tuple(pa), tuple(pb))


def _row_tile(lp):
    return _pick(lp, (384, 256, 128))


def _rmsnorm_fwd(h, gain):
    lp = h.shape[0]
    tr = _row_tile(lp)

    def body(h_ref, g_ref, z_ref):
        x = h_ref[...]
        r = lax.rsqrt(jnp.mean(x * x, axis=-1, keepdims=True) + EPS)
        z_ref[...] = (x * r * g_ref[...]).astype(BF16)

    return pl.pallas_call(
        body, name="rmsnorm_fwd", grid=(lp // tr,),
        in_specs=[pl.BlockSpec((tr, D), lambda i: (i, 0)), pl.BlockSpec((1, D), lambda i: (0, 0))],
        out_specs=pl.BlockSpec((tr, D), lambda i: (i, 0)),
        out_shape=jax.ShapeDtypeStruct((lp, D), BF16),
    )(h, gain)


def _loss_head(h, target, gain):
    lp = h.shape[0]
    tr = 128

    def body(h_ref, t_ref, g_ref, loss_ref, dh_ref, dhb_ref, dg_ref):
        i = pl.program_id(0)

        @pl.when(i == 0)
        def _():
            loss_ref[...] = jnp.zeros_like(loss_ref)
            dg_ref[...] = jnp.zeros_like(dg_ref)
            dh_ref[...] = jnp.zeros_like(dh_ref)
            dhb_ref[...] = jnp.zeros_like(dhb_ref)

        @pl.when(i != 0)
        def _():
            x = h_ref[...]
            g = g_ref[...]
            r = lax.rsqrt(jnp.mean(x * x, axis=-1, keepdims=True) + EPS)
            xh = x * r
            y = xh * g
            err = y - t_ref[...]
            loss_ref[...] += 0.5 * jnp.sum(jnp.sum(err * err, axis=-1, keepdims=True), axis=0, keepdims=True) / D
            dy = err * (1.0 / D)
            dg_ref[...] += jnp.sum(dy * xh, axis=0, keepdims=True)
            dxh = dy * g
            dx = r * (dxh - xh * jnp.mean(dxh * xh, axis=-1, keepdims=True))
            dh_ref[...] = dx
            dhb_ref[...] = dx.astype(BF16)

    row = pl.BlockSpec((tr, D), lambda i: (i, 0))
    vec = pl.BlockSpec((1, D), lambda i: (0, 0))
    return pl.pallas_call(
        body, name="loss_head", grid=(lp // tr,),
        in_specs=[row, pl.BlockSpec((tr, D), lambda i: (jnp.maximum(i - 1, 0), 0)), vec],
        out_specs=[pl.BlockSpec((1, 1), lambda i: (0, 0)), row, row, vec],
        out_shape=[jax.ShapeDtypeStruct((1, 1), F32), jax.ShapeDtypeStruct((lp, D), F32),
                   jax.ShapeDtypeStruct((lp, D), BF16), jax.ShapeDtypeStruct((1, D), F32)],
    )(h, target, gain)


def _silu_parts(g):
    s = _sigmoid(g)
    return g * s, s * (1.0 + g * (1.0 - s))


def _residual_norm_epi(scale, with_norm):
    def epi(accs, exs, row0):
        h2 = exs[0] + scale * accs[0]
        if not with_norm:
            return [h2]
        r = lax.rsqrt(jnp.mean(h2 * h2, axis=-1, keepdims=True) + EPS)
        return [h2, h2 * r * exs[1]]
    return epi


def _norm_bwd_epi(accs, exs, row0):
    dz = accs[0]
    x, res, g = exs
    r = lax.rsqrt(jnp.mean(x * x, axis=-1, keepdims=True) + EPS)
    xh = x * r
    dxh = dz * g
    dx = r * (dxh - xh * jnp.mean(dxh * xh, axis=-1, keepdims=True))
    rows = row0 + lax.broadcasted_iota(jnp.int32, (dz.shape[0], 1), 0)
    dh = jnp.where(rows >= NULL, res + dx, 0.0)
    return [dh, dh, jnp.sum(dz * xh, axis=0, keepdims=True)]


def _norm_bwd_outs(lp):
    return [((lp, D), F32, "mn", 0, ()), ((lp, D), BF16, "mn", 0, ()), ((1, D), F32, "nsum", 0, ())]


def _ffn_fwd(h, z, wg_t, wu_t, wd, pre, next_gain):
    lp = h.shape[0]
    tm = _tm(lp)

    def up_epi(accs, exs, row0):
        g, u = accs
        sg, _ = _silu_parts(g)
        return [g, u, sg * u]

    bshape = (lp, DFF)
    g_, u_, act = _mm("ffn_up", lp, DFF, [_term(z, wg_t, "nt", 0, (), pre), _term(z, wu_t, "nt", 1, (), pre)],
                      [(bshape, BF16, "mn", 0, ())] * 3, up_epi, tm=tm, tn=DFF, nsub=DFF // MXU_N, vmem=VMEM_BIG)

    with_norm = next_gain is not None
    res = _mm("ffn_down", lp, D, [_term(act, wd, "nn", 0, (), pre)],
              [((lp, D), F32, "mn", 0, ())] + ([((lp, D), BF16, "mn", 0, ())] if with_norm else []),
              _residual_norm_epi(0.5, with_norm),
              extras=[(h, "mn", 0, ())] + ([(next_gain, "n", 0, ())] if with_norm else []),
              tm=tm, tn=D, i_outer=True, vmem=VMEM_BIG)
    return res[0], (res[1] if with_norm else None), dict(h=h, z=z, g=g_, u=u_, act=act)


def _dw(name, a, b, m, n, scale=1.0):
    lp = a.shape[0]
    tm = _pick(m, (2944, 1408, 1024, 512, 256, 128))
    tn = _pick(n, (1024, 512, 256, 128)) if tm <= 1408 else _pick(n, (512, 256, 128))
    nk = lp // _pick(lp, (2112, 256, 128))

    def epi(accs, exs, row0):
        return [accs[0] * scale]

    (w,) = _mm(name, m, n, [_term(a, b, "tn")], [((m, n), BF16, "mn", 0, ())], epi, tm=tm, tn=tn, nk=nk,
               i_outer=True, vmem=VMEM_BIG)
    return w


def _ffn_bwd(dh, dhb, sv, gain, wg_t, wu_t, wd, pre):
    lp = dh.shape[0]
    tm = _tm(lp)

    def dact_epi(accs, exs, row0):
        g = exs[0].astype(F32)
        u = exs[1].astype(F32)
        da = 0.5 * accs[0]
        sg, dsg = _silu_parts(g)
        return [da * u * dsg, da * sg]

    dg_, du_ = _mm("ffn_dact", lp, DFF, [_term(dhb, wd, "nt", 0, (), pre)],
                   [((lp, DFF), BF16, "mn", 0, ())] * 2, dact_epi,
                   extras=[(sv["g"], "mn", 0, ()), (sv["u"], "mn", 0, ())], tm=tm, tn=DFF, nsub=DFF // MXU_N,
                   vmem=VMEM_BIG)
    d_wd = _dw("dw_down", sv["act"], dhb, DFF, D, 0.5)
    d_wg = _dw("dw_gate", dg_, sv["z"], DFF, D)
    d_wu = _dw("dw_up", du_, sv["z"], DFF, D)

    nk = 1
    dh2, dhb2, dgain = _mm("ffn_dz", lp, D, [_term(dg_, wg_t, "nn", 0, (), pre), _term(du_, wu_t, "nn", 0, (), pre)],
                           _norm_bwd_outs(lp), _norm_bwd_epi,
                           extras=[(sv["h"], "mn", 0, ()), (dh, "mn", 0, ()), (gain, "n", 0, ())],
                           tm=tm, tn=D, nk=nk, vmem=VMEM_BIG)
    return dh2, dhb2, dgain, d_wg, d_wu, d_wd


def _gla_gates(hin, w2p, b2p):
    lp = hin.shape[0]
    tr = _row_tile(lp)
    bf, bb = P_OFF["lrf"] // HP, P_OFF["lrb"] // HP

    def body(lf_ref, lb_ref, w_ref, b_ref, o_ref, c_ref):
        i = pl.program_id(0)
        rows = i * tr + lax.broadcasted_iota(jnp.int32, (tr, 1), 0)
        r = lax.broadcasted_iota(jnp.int32, (tr, tr), 0)
        c = lax.broadcasted_iota(jnp.int32, (tr, tr), 1)
        same = (r // CHUNK) == (c // CHUNK)
        for d, l_ref in enumerate((lf_ref, lb_ref)):
            logit = _dot3(l_ref[...].astype(F32), w_ref[d]) + b_ref[d]
            g = jnp.where(rows >= NULL, jax.nn.log_sigmoid(logit) * (1.0 / GLA_TAU), 0.0)
            o_ref[d] = g
            tmat = jnp.where(same & ((r >= c) if d == 0 else (r <= c)), 1.0, 0.0).astype(BF16)
            c_ref[d] = _dot_sel(tmat, g)

    spec = pl.BlockSpec((2, tr, 512), lambda i: (0, i, 0))
    return pl.pallas_call(
        body, name="gla_gates", grid=(lp // tr,),
        in_specs=[pl.BlockSpec((tr, HP), lambda i: (i, bf)), pl.BlockSpec((tr, HP), lambda i: (i, bb)),
                  pl.BlockSpec((2, HP, 512), lambda i: (0, 0, 0)), pl.BlockSpec((2, 1, 512), lambda i: (0, 0, 0))],
        out_specs=[spec, spec],
        out_shape=[jax.ShapeDtypeStruct((2, lp, 512), F32)] * 2,
    )(hin, hin, w2p, b2p)


def _gla_rows(lp):
    return _pick(lp, (384, 256, 128))


def _tri(d):
    r = lax.broadcasted_iota(jnp.int32, (CHUNK, CHUNK), 0)
    c = lax.broadcasted_iota(jnp.int32, (CHUNK, CHUNK), 1)
    return (r >= c) if d == 0 else (r <= c)


def _gla_fwd(hin, gates):
    lp = hin.shape[0]
    rb = _gla_rows(lp)
    nb = lp // rb
    cpb = rb // CHUNK
    nchunk = lp // CHUNK
    qo, ko, vo = P_OFF["qa"] // 512, P_OFF["ka"] // 512, P_OFF["va"] // 512
    scale = GLA_DK ** -0.5

    def body(qf, kf, vf, gf, qb, kb, vb_, gb, of, ob, sf, sb, st):
        @pl.when(pl.program_id(0) == 0)
        def _():
            st[...] = jnp.zeros_like(st)

        ins = ((qf, kf, vf, gf, of, sf), (qb, kb, vb_, gb, ob, sb))
        for ci in range(cpb):
            for d in range(2):
                q_ref, k_ref, v_ref, g_ref, o_ref, s_ref = ins[d]
                tri = _tri(d)
                c = ci if d == 0 else cpb - 1 - ci
                rows = slice(c * CHUNK, (c + 1) * CHUNK)
                for h in range(GLA_H):
                    sl = slice(h * HP, (h + 1) * HP)
                    q = q_ref[rows, sl].astype(F32) * scale
                    k = k_ref[rows, sl].astype(F32)
                    v = v_ref[rows, sl].astype(F32)
                    b = g_ref[rows, sl]
                    btot = b[CHUNK - 1:CHUNK] if d == 0 else b[0:1]
                    qd = (q * jnp.exp(b)).astype(BF16)
                    ki = (k * jnp.exp(-b)).astype(BF16)
                    ke = (k * jnp.exp(btot - b)).astype(BF16)
                    vb = v.astype(BF16)
                    att = jnp.where(tri, _dot(qd, ki, "nt"), 0.0)
                    s_prev = st[d, h]
                    o_ref[rows, sl] = _dot(att.astype(BF16), vb) + _dot(qd, s_prev.astype(BF16), "nt")
                    s_ref[h, c] = s_prev
                    st[d, h] = s_prev * jnp.exp(btot) + _dot(vb, ke, "tn")

    def specs(off):
        return (pl.BlockSpec((rb, 512), lambda b: (b, off)), pl.BlockSpec((rb, 512), lambda b: (nb - 1 - b, off)))

    (qf, qb), (kf, kb), (vf, vb2) = specs(qo), specs(ko), specs(vo)
    gf = pl.BlockSpec((None, rb, 512), lambda b: (0, b, 0))
    gb = pl.BlockSpec((None, rb, 512), lambda b: (1, nb - 1 - b, 0))
    of, ob = specs(0)
    sf = pl.BlockSpec((GLA_H, cpb, HP, HP), lambda b: (0, b, 0, 0))
    sb = pl.BlockSpec((GLA_H, cpb, HP, HP), lambda b: (0, nb - 1 - b, 0, 0))
    osh = jax.ShapeDtypeStruct((lp, GLA_H * HP), F32)
    ssh = jax.ShapeDtypeStruct((GLA_H, nchunk, HP, HP), F32)
    return pl.pallas_call(
        body, name="gla_fwd", grid=(nb,),
        in_specs=[qf, kf, vf, gf, qb, kb, vb2, gb], out_specs=[of, ob, sf, sb], out_shape=[osh, osh, ssh, ssh],
        scratch_shapes=[pltpu.VMEM((2, GLA_H, HP, HP), F32)], compiler_params=_cparams(VMEM_BIG),
    )(hin, hin, hin, gates, hin, hin, hin, gates)


def _gla_bwd(hin, gates, states, do):
    lp = hin.shape[0]
    rb = _gla_rows(lp)
    nb = lp // rb
    cpb = rb // CHUNK
    qo, ko, vo = P_OFF["qa"] // 512, P_OFF["ka"] // 512, P_OFF["va"] // 512
    scale = GLA_DK ** -0.5

    def body(qf, kf, vf, gf, sf, dof, qb, kb, vb_, gb, sb, dob,
             dqf, dkf, dvf, dgf, dqb, dkb, dvb, dgb, dst):
        @pl.when(pl.program_id(0) == 0)
        def _():
            dst[...] = jnp.zeros_like(dst)

        ins = ((qf, kf, vf, gf, sf, dof, dqf, dkf, dvf, dgf), (qb, kb, vb_, gb, sb, dob, dqb, dkb, dvb, dgb))
        for ci in range(cpb):
            for d in range(2):
                q_ref, k_ref, v_ref, g_ref, s_ref, do_ref, dq_ref, dk_ref, dv_ref, dg_ref = ins[d]
                tri, tri_t = _tri(d), _tri(1 - d)
                edge = lax.broadcasted_iota(jnp.int32, (CHUNK, 1), 0) == (CHUNK - 1 if d == 0 else 0)
                c = cpb - 1 - ci if d == 0 else ci
                rows = slice(c * CHUNK, (c + 1) * CHUNK)
                for h in range(GLA_H):
                    sl = slice(h * HP, (h + 1) * HP)
                    q = q_ref[rows, sl].astype(F32) * scale
                    k = k_ref[rows, sl].astype(F32)
                    v = v_ref[rows, sl].astype(F32)
                    dout = do_ref[rows, sl].astype(BF16)
                    b = g_ref[rows, sl]
                    btot = b[CHUNK - 1:CHUNK] if d == 0 else b[0:1]
                    e = jnp.exp(b)
                    ei = jnp.exp(-b)
                    et = jnp.exp(btot - b)
                    etot = jnp.exp(btot)
                    qd = q * e
                    ki = k * ei
                    ke = k * et
                    qdb, kib, keb, vb = qd.astype(BF16), ki.astype(BF16), ke.astype(BF16), v.astype(BF16)
                    att_t = jnp.where(tri_t, _dot(kib, qdb, "nt"), 0.0).astype(BF16)
                    d_att = jnp.where(tri, _dot(dout, vb, "nt"), 0.0).astype(BF16)
                    d_att_t = jnp.where(tri_t, _dot(vb, dout, "nt"), 0.0).astype(BF16)
                    s_prev = s_ref[h, c]
                    ds_t = dst[d, h]
                    ds_b = ds_t.astype(BF16)
                    dv = _dot(att_t, dout) + _dot(keb, ds_b, "nt")
                    d_qd = _dot(d_att, kib) + _dot(dout, s_prev.astype(BF16))
                    d_ki = _dot(d_att_t, qdb)
                    d_ke = _dot(vb, ds_b)
                    d_e = jnp.sum(s_prev * ds_t, axis=0, keepdims=True)
                    dst[d, h] = _dot(dout, qdb, "tn") + ds_t * etot
                    db = d_qd * qd - d_ki * ki - d_ke * ke
                    dbtot = jnp.sum(d_ke * ke, axis=0, keepdims=True) + d_e * etot
                    dq_ref[rows, sl] = (d_qd * e * scale).astype(BF16)
                    dk_ref[rows, sl] = (d_ki * ei + d_ke * et).astype(BF16)
                    dv_ref[rows, sl] = dv.astype(BF16)
                    dg_ref[rows, sl] = db + jnp.where(edge, dbtot, 0.0)

    def fw(off):
        return pl.BlockSpec((rb, 512), lambda b: (nb - 1 - b, off))

    def bw(off):
        return pl.BlockSpec((rb, 512), lambda b: (b, off))

    gf = pl.BlockSpec((None, rb, 512), lambda b: (0, nb - 1 - b, 0))
    gb = pl.BlockSpec((None, rb, 512), lambda b: (1, b, 0))
    sf = pl.BlockSpec((GLA_H, cpb, HP, HP), lambda b: (0, nb - 1 - b, 0, 0))
    sb = pl.BlockSpec((GLA_H, cpb, HP, HP), lambda b: (0, b, 0, 0))
    osh = jax.ShapeDtypeStruct((lp, GLA_H * HP), F32)
    osh_b = jax.ShapeDtypeStruct((lp, GLA_H * HP), BF16)
    res = pl.pallas_call(
        body, name="gla_bwd", grid=(nb,),
        in_specs=[fw(qo), fw(ko), fw(vo), gf, sf, fw(0), bw(qo), bw(ko), bw(vo), gb, sb, bw(0)],
        out_specs=[fw(0)] * 4 + [bw(0)] * 4, out_shape=[osh_b, osh_b, osh_b, osh] * 2,
        scratch_shapes=[pltpu.VMEM((2, GLA_H, HP, HP), F32)], compiler_params=_cparams(VMEM_BIG),
    )(hin, hin, hin, gates, states[0], do, hin, hin, hin, gates, states[1], do)
    return res[:4], res[4:]


def _gla_out_fwd(o2, hin, gn):
    lp = hin.shape[0]
    tr = _row_tile(lp)
    ro = P_OFF["ra"] // 512

    def body(of_ref, ob_ref, r_ref, gn_ref, a_ref):
        r = r_ref[...].astype(F32)
        sr, _ = _silu_parts(r)
        for h in range(GLA_H):
            sl = slice(h * HP, (h + 1) * HP)
            o = of_ref[:, sl] + ob_ref[:, sl]
            rs = lax.rsqrt(jnp.mean(o * o, axis=-1, keepdims=True) + EPS)
            a_ref[:, sl] = (o * rs * gn_ref[:, sl] * sr[:, sl]).astype(BF16)

    row = pl.BlockSpec((tr, 512), lambda i: (i, 0))
    return pl.pallas_call(
        body, name="gla_out_fwd", grid=(lp // tr,),
        in_specs=[row, row, pl.BlockSpec((tr, 512), lambda i: (i, ro)), pl.BlockSpec((1, 512), lambda i: (0, 0))],
        out_specs=row,
        out_shape=jax.ShapeDtypeStruct((lp, 512), BF16),
    )(o2[0], o2[1], hin, gn)


def _gla_out_bwd(da, o2, hin, gn):
    lp = hin.shape[0]
    tr = _row_tile(lp)
    ro = P_OFF["ra"] // 512

    def body(da_ref, of_ref, ob_ref, r_ref, gn_ref, do_ref, dr_ref, dgn_ref):
        i = pl.program_id(0)
        r = r_ref[...].astype(F32)
        sr, dsr = _silu_parts(r)
        da_v = da_ref[...]
        parts = []
        for h in range(GLA_H):
            sl = slice(h * HP, (h + 1) * HP)
            o = of_ref[:, sl] + ob_ref[:, sl]
            rs = lax.rsqrt(jnp.mean(o * o, axis=-1, keepdims=True) + EPS)
            oh = o * rs
            gn_h = gn_ref[:, sl]
            dah = da_v[:, sl]
            dr_ref[:, sl] = (dah * oh * gn_h * dsr[:, sl]).astype(BF16)
            t = dah * sr[:, sl]
            parts.append(jnp.sum(t * oh, axis=0, keepdims=True))
            doh = t * gn_h
            do_ref[:, sl] = rs * (doh - oh * jnp.mean(doh * oh, axis=-1, keepdims=True))
        part = jnp.concatenate(parts, axis=1)

        @pl.when(i == 0)
        def _():
            dgn_ref[...] = part

        @pl.when(i != 0)
        def _():
            dgn_ref[...] += part

    row = pl.BlockSpec((tr, 512), lambda i: (i, 0))
    return pl.pallas_call(
        body, name="gla_out_bwd", grid=(lp // tr,),
        in_specs=[row, row, row, pl.BlockSpec((tr, 512), lambda i: (i, ro)), pl.BlockSpec((1, 512), lambda i: (0, 0))],
        out_specs=[row, row, pl.BlockSpec((1, 512), lambda i: (0, 0))],
        out_shape=[jax.ShapeDtypeStruct((lp, 512), F32), jax.ShapeDtypeStruct((lp, 512), BF16),
                   jax.ShapeDtypeStruct((1, 512), F32)],
    )(da, o2[0], o2[1], hin, gn)


def _gla_in_bwd(gf, gb, gates, hin, w2p, others):
    lp = hin.shape[0]
    tr = _row_tile(lp)
    bf, bb = P_OFF["lrf"] // HP, P_OFF["lrb"] // HP
    names = tuple(others)

    def seg(name):
        return slice(P_OFF[name], P_OFF[name] + P_WIDTH[name])

    def body(dqf_ref, dkf_ref, dvf_ref, dgf_ref, dqb_ref, dkb_ref, dvb_ref, dgb_ref, g_ref, lf_ref, lb_ref, w_ref,
             *rest):
        other_refs, (o_ref, dw_ref, db_ref) = rest[:len(names)], rest[len(names):]
        i = pl.program_id(0)
        for n, ref in zip(names, other_refs):
            o_ref[:, seg(n)] = ref[...].astype(BF16)
        o_ref[:, seg("qa")] = (dqf_ref[...].astype(F32) + dqb_ref[...].astype(F32)).astype(BF16)
        o_ref[:, seg("ka")] = (dkf_ref[...].astype(F32) + dkb_ref[...].astype(F32)).astype(BF16)
        o_ref[:, seg("va")] = (dvf_ref[...].astype(F32) + dvb_ref[...].astype(F32)).astype(BF16)
        olr_ref = o_ref.at[:, P_OFF["lrf"]:P_OFF["lrf"] + 2 * HP]
        rows = i * tr + lax.broadcasted_iota(jnp.int32, (tr, 1), 0)
        r = lax.broadcasted_iota(jnp.int32, (tr, tr), 0)
        c = lax.broadcasted_iota(jnp.int32, (tr, tr), 1)
        same = (r // CHUNK) == (c // CHUNK)
        for d, (l_ref, dg_ref) in enumerate(((lf_ref, dgf_ref), (lb_ref, dgb_ref))):
            tmat = jnp.where(same & ((r <= c) if d == 0 else (r >= c)), 1.0, 0.0).astype(BF16)
            dg = _dot_sel(tmat, dg_ref[...])
            sig_neg = 1.0 - jnp.exp(GLA_TAU * g_ref[d])
            dlogit = jnp.where(rows >= NULL, dg * (1.0 / GLA_TAU) * sig_neg, 0.0)
            olr_ref[:, d * HP:(d + 1) * HP] = _dot3(dlogit, w_ref[d], "nt").astype(BF16)
            dw = _dot3(l_ref[...].astype(F32), dlogit, "tn")
            dbias = jnp.sum(dlogit, axis=0, keepdims=True)

            @pl.when(i == 0)
            def _():
                dw_ref[d] = dw
                db_ref[d] = dbias

            @pl.when(i != 0)
            def _():
                dw_ref[d] += dw
                db_ref[d] += dbias

    two = pl.BlockSpec((2, tr, 512), lambda i: (0, i, 0))
    row = pl.BlockSpec((tr, 512), lambda i: (i, 0))
    return pl.pallas_call(
        body, name="gla_in_bwd", grid=(lp // tr,),
        in_specs=[row] * 8 + [two, pl.BlockSpec((tr, HP), lambda i: (i, bf)),
                  pl.BlockSpec((tr, HP), lambda i: (i, bb)), pl.BlockSpec((2, HP, 512), lambda i: (0, 0, 0))] +
                 [pl.BlockSpec((tr, P_WIDTH[n]), lambda i: (i, 0)) for n in names],
        out_specs=[pl.BlockSpec((tr, D_INP), lambda i: (i, 0)),
                   pl.BlockSpec((2, HP, 512), lambda i: (0, 0, 0)), pl.BlockSpec((2, 1, 512), lambda i: (0, 0, 0))],
        out_shape=[jax.ShapeDtypeStruct((lp, D_INP), BF16), jax.ShapeDtypeStruct((2, HP, 512), F32),
                   jax.ShapeDtypeStruct((2, 1, 512), F32)],
        compiler_params=_cparams(VMEM_BIG),
    )(*gf, *gb, gates, hin, hin, w2p, *[others[n] for n in names])


def _rope_tables(lp):
    n_tok = lp - NULL - NMETA
    rows = n_tok // GRID_W
    row = np.repeat(np.arange(rows), GRID_W).astype(np.float32)
    col = np.tile(np.arange(GRID_W), rows).astype(np.float32)
    inv = (ROPE_THETA ** (-np.arange(0, 32, 2, dtype=np.float32) / 32)).astype(np.float32)
    ang = np.concatenate([row[:, None] * inv, col[:, None] * inv], axis=-1)
    ang = np.concatenate([np.zeros((NULL + NMETA, 32), np.float32), ang], axis=0)
    cos, sin = np.cos(ang).astype(np.float32), np.sin(ang).astype(np.float32)
    z16 = np.zeros((lp, 16), np.float32)
    z64 = np.zeros((lp, 64), np.float32)
    c = np.concatenate([cos[:, :16], cos[:, :16], cos[:, 16:], cos[:, 16:], z64], axis=1)
    a = np.concatenate([-sin[:, :16], z16, -sin[:, 16:], z16, z64], axis=1)
    b = np.concatenate([z16, sin[:, :16], z16, sin[:, 16:], z64], axis=1)
    return jnp.asarray(c), jnp.asarray(a), jnp.asarray(b)


def _rope(x, c, a, b):
    return x * c + pltpu.roll(x, HP - 16, 1) * a + pltpu.roll(x, 16, 1) * b


def _rope_t(dx, c, a, b):
    return dx * c + pltpu.roll(dx * a, 16, 1) + pltpu.roll(dx * b, HP - 16, 1)


def _attn_prep(hin, gq, gk, tabs):
    lp = hin.shape[0]
    tr = _row_tile(lp)
    qo, ko, vo = P_OFF["qb"] // 1024, P_OFF["kb"] // 256, P_OFF["vb"] // 256

    def body(q_ref, k_ref, v_ref, gq_ref, gk_ref, c_ref, a_ref, b_ref, oq_ref, ok_ref, ov_ref):
        c, a, b = c_ref[...], a_ref[...], b_ref[...]
        for src, g_ref, dst, nh, sc in ((q_ref, gq_ref, oq_ref, ATT_H, Q_SCALE), (k_ref, gk_ref, ok_ref, ATT_KV, 1.0)):
            for h in range(nh):
                sl = slice(h * HP, (h + 1) * HP)
                x = src[:, sl].astype(F32)
                r = lax.rsqrt(jnp.sum(x * x, axis=-1, keepdims=True) * (1.0 / HEAD_DIM) + EPS)
                dst[:, sl] = (_rope(x * r * g_ref[...], c, a, b) * sc).astype(BF16)
        lane = lax.broadcasted_iota(jnp.int32, (1, ATT_KV * HP), 1)
        ov_ref[...] = jnp.where(lane % HP == HEAD_DIM, 1.0, v_ref[...]).astype(BF16)

    tab = pl.BlockSpec((tr, HP), lambda i: (i, 0))
    vec = pl.BlockSpec((1, HP), lambda i: (0, 0))
    return pl.pallas_call(
        body, name="attn_prep", grid=(lp // tr,),
        in_specs=[pl.BlockSpec((tr, 1024), lambda i: (i, qo)), pl.BlockSpec((tr, 256), lambda i: (i, ko)),
                  pl.BlockSpec((tr, 256), lambda i: (i, vo)), vec, vec, tab, tab, tab],
        out_specs=[pl.BlockSpec((tr, 1024), lambda i: (i, 0)), pl.BlockSpec((tr, 256), lambda i: (i, 0)),
                   pl.BlockSpec((tr, 256), lambda i: (i, 0))],
        out_shape=[jax.ShapeDtypeStruct((lp, 1024), BF16), jax.ShapeDtypeStruct((lp, 256), BF16),
                   jax.ShapeDtypeStruct((lp, 256), BF16)],
    )(hin, hin, hin, gq, gk, *tabs)


def _attn_prep_bwd(dqr, dkr, hin, gq, gk, tabs):
    lp = hin.shape[0]
    tr = _row_tile(lp)
    qo, ko = P_OFF["qb"] // 1024, P_OFF["kb"] // 256

    def body(dq_ref, dk_ref, q_ref, k_ref, gq_ref, gk_ref, c_ref, a_ref, b_ref, oq_ref, ok_ref, dgq_ref, dgk_ref):
        i = pl.program_id(0)
        c, a, b = c_ref[...], a_ref[...], b_ref[...]
        for src, dsrc, g_ref, dst, dg_ref, nh, sc in (
                (q_ref, dq_ref, gq_ref, oq_ref, dgq_ref, ATT_H, Q_SCALE),
                (k_ref, dk_ref, gk_ref, ok_ref, dgk_ref, ATT_KV, 1.0)):
            acc = jnp.zeros((1, HP), F32)
            for h in range(nh):
                sl = slice(h * HP, (h + 1) * HP)
                x = src[:, sl].astype(F32)
                r = lax.rsqrt(jnp.sum(x * x, axis=-1, keepdims=True) * (1.0 / HEAD_DIM) + EPS)
                xh = x * r
                dxn = _rope_t(dsrc[:, sl] * sc, c, a, b)
                acc = acc + jnp.sum(dxn * xh, axis=0, keepdims=True)
                dxh = dxn * g_ref[...]
                dx = r * (dxh - xh * (jnp.sum(dxh * xh, axis=-1, keepdims=True) * (1.0 / HEAD_DIM)))
                dst[:, sl] = dx.astype(BF16)

            @pl.when(i == 0)
            def _():
                dg_ref[...] = acc

            @pl.when(i != 0)
            def _():
                dg_ref[...] += acc

    tab = pl.BlockSpec((tr, HP), lambda i: (i, 0))
    vec = pl.BlockSpec((1, HP), lambda i: (0, 0))
    return pl.pallas_call(
        body, name="attn_prep_bwd", grid=(lp // tr,),
        in_specs=[pl.BlockSpec((tr, 1024), lambda i: (i, 0)), pl.BlockSpec((tr, 256), lambda i: (i, 0)),
                  pl.BlockSpec((tr, 1024), lambda i: (i, qo)), pl.BlockSpec((tr, 256), lambda i: (i, ko)),
                  vec, vec, tab, tab, tab],
        out_specs=[pl.BlockSpec((tr, 1024), lambda i: (i, 0)), pl.BlockSpec((tr, 256), lambda i: (i, 0)), vec, vec],
        out_shape=[jax.ShapeDtypeStruct((lp, 1024), BF16), jax.ShapeDtypeStruct((lp, 256), BF16),
                   jax.ShapeDtypeStruct((1, HP), F32), jax.ShapeDtypeStruct((1, HP), F32)],
    )(dqr, dkr, hin, hin, gq, gk, *tabs)


QB = 128
GH = 2
Q_SCALE = HEAD_DIM ** -0.5 * math.log2(math.e)
LN2 = math.log(2.0)


def _stack(ref, g0, n):
    return jnp.concatenate([ref[:, (g0 + g) * HP:(g0 + g + 1) * HP] for g in range(n)], axis=0)


def _attn_fwd(qr, kr, vb):
    lp = qr.shape[0]
    nq = lp // QB

    def body(q_ref, k_ref, v_ref, o_ref, lse_ref):
        qb = pl.program_id(1)
        keys = lax.broadcasted_iota(jnp.int32, (1, lp), 1)
        lane = lax.broadcasted_iota(jnp.int32, (1, HP), 1)
        rows = qb * QB + lax.broadcasted_iota(jnp.int32, (QB, 1), 0)
        gh = 1
        for ch in range(ATT_G // gh):
            qs = _stack(q_ref, ch * gh, gh)
            s = _dot(qs, k_ref[...], "nt")
            s = jnp.where(keys >= NULL, s, -1e30)
            m = jnp.max(s, axis=-1, keepdims=True)
            p = jnp.exp2(s - m).astype(BF16)
            o_raw = _dot(p, v_ref[...])
            l = jnp.sum(jnp.where(lane == HEAD_DIM, o_raw, 0.0), axis=-1, keepdims=True)
            o = jnp.where(lane < HEAD_DIM, o_raw / l, 0.0)
            lse = m + jnp.log2(l)
            for g in range(gh):
                sl = slice((ch * gh + g) * HP, (ch * gh + g + 1) * HP)
                o_ref[:, sl] = jnp.where(rows >= NULL, o[g * QB:(g + 1) * QB], 0.0).astype(BF16)
                lse_ref[:, sl] = jnp.broadcast_to(lse[g * QB:(g + 1) * QB], (QB, HP))

    qspec = pl.BlockSpec((QB, ATT_G * HP), lambda kv, qb: (qb, kv))
    kspec = pl.BlockSpec((lp, HP), lambda kv, qb: (0, kv))
    return pl.pallas_call(
        body, name="attn_fwd", grid=(ATT_KV, nq),
        in_specs=[qspec, kspec, kspec], out_specs=[qspec, qspec],
        out_shape=[jax.ShapeDtypeStruct((lp, ATT_H * HP), BF16), jax.ShapeDtypeStruct((lp, ATT_H * HP), F32)],
        compiler_params=_cparams(VMEM_BIG),
    )(qr, kr, vb)


def _attn_bwd(qr, kr, vb, o, lse, do):
    lp = qr.shape[0]
    nq = lp // QB

    def body(q_ref, k_ref, v_ref, o_ref, lse_ref, do_ref, dq_ref, dk_ref, dv_ref):
        qb = pl.program_id(1)

        @pl.when(qb == 0)
        def _():
            dk_ref[...] = jnp.zeros_like(dk_ref)
            dv_ref[...] = jnp.zeros_like(dv_ref)

        keys = lax.broadcasted_iota(jnp.int32, (1, lp), 1)
        k = k_ref[...]
        dk_acc, dv_acc = None, None
        for ch in range(ATT_G // GH):
            g0 = ch * GH
            qs = _stack(q_ref, g0, GH)
            dos = _stack(do_ref, g0, GH)
            os_ = _stack(o_ref, g0, GH).astype(F32)
            lse_s = jnp.concatenate([lse_ref[:, (g0 + g) * HP:(g0 + g) * HP + 1] for g in range(GH)], axis=0)
            delta = jnp.sum(dos * os_, axis=-1, keepdims=True) * LN2
            s = _dot(qs, k, "nt")
            p = jnp.where(keys >= NULL, jnp.exp2(s - lse_s), 0.0)
            dob = dos.astype(BF16)
            dp = _dot((dos * LN2).astype(BF16), v_ref[...], "nt")
            ds = (p * (dp - delta)).astype(BF16)
            dq = _dot(ds, k)
            for g in range(GH):
                dq_ref[:, (g0 + g) * HP:(g0 + g + 1) * HP] = dq[g * QB:(g + 1) * QB]
            dv_c = _dot(p.astype(BF16), dob, "tn")
            dk_c = _dot(ds, qs, "tn")
            dv_acc = dv_c if dv_acc is None else dv_acc + dv_c
            dk_acc = dk_c if dk_acc is None else dk_acc + dk_c
        dv_ref[...] += dv_acc
        dk_ref[...] += dk_acc

    qspec = pl.BlockSpec((QB, ATT_G * HP), lambda kv, qb: (qb, kv))
    kspec = pl.BlockSpec((lp, HP), lambda kv, qb: (0, kv))
    return pl.pallas_call(
        body, name="attn_bwd", grid=(ATT_KV, nq),
        in_specs=[qspec, kspec, kspec, qspec, qspec, qspec], out_specs=[qspec, kspec, kspec],
        out_shape=[jax.ShapeDtypeStruct((lp, ATT_H * HP), F32), jax.ShapeDtypeStruct((lp, ATT_KV * HP), F32),
                   jax.ShapeDtypeStruct((lp, ATT_KV * HP), F32)],
        compiler_params=_cparams(VMEM_BIG),
    )(qr, kr, vb, o, lse, do)


def _mixer_fwd(h, z, wl, l, tabs, next_gain):
    lp = h.shape[0]
    tm = _tm(lp)

    def id_epi(accs, exs, row0):
        return [accs[0]]

    (hin,) = _mm("in_proj", lp, D_INP, [_term(z, wl["win_t"], "nt", 0, (), (l,))], [((lp, D_INP), BF16, "mn", 0, ())],
                 id_epi, tm=_tm_wide(lp), tn=D_INP // 2, vmem=VMEM_BIG)
    gates, cum = _gla_gates(hin, wl["w2p"][l], wl["b2p"][l])
    o_f, o_b, s_f, s_b = _gla_fwd(hin, cum)
    o2, states = (o_f, o_b), (s_f, s_b)
    a = _gla_out_fwd(o2, hin, wl["gn"][l])
    qr, kr, vb = _attn_prep(hin, wl["gq"][l], wl["gk"][l], tabs)
    b, lse = _attn_fwd(qr, kr, vb)

    def merge_epi(accs, exs, row0):
        pa, pb = accs
        ga, gb, bma, bmb = exs
        y = _sigmoid(ga + bma) * pa + _sigmoid(gb + bmb) * pb
        return [y, pa, pb]

    y, pa, pb = _mm("merge", lp, D, [_term(a, wl["wpa_t"], "nt", 0, (), (l,)), _term(b, wl["wpb_t"], "nt", 1, (), (l,))],
                    [((lp, D), BF16, "mn", 0, ())] * 3, merge_epi,
                    extras=[(hin, "mn", P_OFF["ga"] // D, ()), (hin, "mn", P_OFF["gb"] // D, ()),
                            (wl["bm"], "n", 0, (l, 0)), (wl["bm"], "n", 0, (l, 1))],
                    tm=_tm_wide(lp), tn=D, nsub=D // MXU_N, i_outer=True, vmem=VMEM_BIG)

    h2, z2 = _mm("out_proj", lp, D, [_term(y, wl["wout"], "nn", 0, (), (l,))],
                 [((lp, D), F32, "mn", 0, ()), ((lp, D), BF16, "mn", 0, ())], _residual_norm_epi(1.0, True),
                 extras=[(h, "mn", 0, ()), (next_gain, "n", 0, ())], tm=_tm_wide(lp), tn=D, i_outer=True, vmem=VMEM_BIG)
    sv = dict(h=h, z=z, hin=hin, gates=gates, cum=cum, o2=o2, states=states, a=a, qr=qr, kr=kr, vb=vb, b=b, lse=lse,
              y=y, pa=pa, pb=pb)
    return h2, z2, sv


def _mixer_bwd(dh, dhb, sv, gain, wl, l, tabs):
    lp = dh.shape[0]
    tm = _tm(lp)
    hin = sv["hin"]

    def merge_bwd_epi(accs, exs, row0):
        dy = accs[0]
        ga, gb, pa, pb, bma, bmb = exs
        sa = _sigmoid(ga + bma)
        sb = _sigmoid(gb + bmb)
        dga = dy * pa.astype(F32) * sa * (1.0 - sa)
        dgb = dy * pb.astype(F32) * sb * (1.0 - sb)
        return [dy * sa, dy * sb, dga, dgb, jnp.sum(dga, axis=0, keepdims=True), jnp.sum(dgb, axis=0, keepdims=True)]

    big = ((lp, D), BF16, "mn", 0, ())
    vec = ((1, D), F32, "nsum", 0, ())
    dpa, dpb, dga, dgb, dbma, dbmb = _mm(
        "merge_bwd", lp, D, [_term(dhb, wl["wout"], "nt", 0, (), (l,))], [big, big, big, big, vec, vec], merge_bwd_epi,
        extras=[(hin, "mn", P_OFF["ga"] // D, ()), (hin, "mn", P_OFF["gb"] // D, ()), (sv["pa"], "mn", 0, ()),
                (sv["pb"], "mn", 0, ()), (wl["bm"], "n", 0, (l, 0)), (wl["bm"], "n", 0, (l, 1))],
        tm=tm, tn=D, nsub=D // MXU_N, vmem=VMEM_BIG)
    d_wout = _dw("dw_out", sv["y"], dhb, D, D)
    d_wpa_t = _dw("dw_pa", dpa, sv["a"], D, 512)
    d_wpb_t = _dw("dw_pb", dpb, sv["b"], D, ATT_H * HP)

    def id_epi(accs, exs, row0):
        return [accs[0]]

    (da,) = _mm("d_a", lp, 512, [_term(dpa, wl["wpa_t"], "nn", 0, (), (l,))], [((lp, 512), F32, "mn", 0, ())], id_epi,
                tm=_tm_wide(lp), tn=512, i_outer=True, vmem=VMEM_BIG)
    (db,) = _mm("d_b", lp, ATT_H * HP, [_term(dpb, wl["wpb_t"], "nn", 0, (), (l,))],
                [((lp, ATT_H * HP), F32, "mn", 0, ())], id_epi, tm=_tm_wide(lp), tn=512, i_outer=True, vmem=VMEM_BIG)
    d_o, d_ra, d_gn = _gla_out_bwd(da, sv["o2"], hin, wl["gn"][l])
    g_fw, g_bw = _gla_bwd(hin, sv["cum"], sv["states"], d_o)
    dqr, dkr, dvb = _attn_bwd(sv["qr"], sv["kr"], sv["vb"], sv["b"], sv["lse"], db)
    d_qb, d_kb, d_gq, d_gk = _attn_prep_bwd(dqr, dkr, hin, wl["gq"][l], wl["gk"][l], tabs)
    dhin, d_w2p, d_b2p = _gla_in_bwd(g_fw, g_bw, sv["gates"], hin, wl["w2p"][l],
                                     dict(qb=d_qb, ga=dga, gb=dgb, ra=d_ra, kb=d_kb, vb=dvb))
    d_win_t = _dw("dw_in", dhin, sv["z"], D_INP, D)
    dh2, dhb2, dgain = _mm("in_proj_dz", lp, D, [_term(dhin, wl["win_t"], "nn", 0, (), (l,))], _norm_bwd_outs(lp),
                           _norm_bwd_epi, extras=[(sv["h"], "mn", 0, ()), (dh, "mn", 0, ()), (gain, "n", 0, ())],
                           tm=tm, tn=D, nk=1, vmem=VMEM_BIG)
    grads = dict(gain=dgain, wout=d_wout, wpa_t=d_wpa_t, wpb_t=d_wpb_t, win_t=d_win_t, gn=d_gn, w2p=d_w2p, b2p=d_b2p,
                 gq=d_gq, gk=d_gk, bma=dbma, bmb=dbmb)
    return dh2, dhb2, grads


def _mesh_pos():
    x, y, c = lax.axis_index("x"), lax.axis_index("y"), lax.axis_index("c")
    chips = [(1 - x, y), (x, 1 - y), (1 - x, 1 - y)]
    return x, y, c, chips


def _dev_index(x, y, c):
    return 4 * x + 2 * y + c


def _all_gather(name, shards, leads):
    nt = len(shards)

    def blk(ref, lead, idx):
        return ref.at[(slice(None),) * lead + (idx,)]

    def body(*refs):
        xs, outs = refs[:nt], refs[nt:2 * nt]
        send_sems, recv_sems, local_sems = refs[2 * nt:]
        x, y, c, chips = _mesh_pos()
        me, sibling = (x, y, c), (x, y, 1 - c)

        def copy(t, k, block, to, own=False):
            dst = blk(outs[t], leads[t], _dev_index(*block))
            return pltpu.make_async_remote_copy(
                src_ref=xs[t] if own else dst, dst_ref=dst, send_sem=send_sems.at[t, k], recv_sem=recv_sems.at[t, k],
                device_id=to, device_id_type=MESH)

        locals_ = [pltpu.make_async_copy(xs[t], blk(outs[t], leads[t], _dev_index(*me)), local_sems.at[t])
                   for t in range(nt)]
        for cp in locals_:
            cp.start()
        first = []
        for t in range(nt):
            first.append(copy(t, 0, me, sibling, own=True))
            first += [copy(t, 1 + j, me, (*chip, c), own=True) for j, chip in enumerate(chips)]
        for cp in first:
            cp.start()
        passed = []
        for j, chip in enumerate(chips):
            for t in range(nt):
                copy(t, 1 + j, (*chip, c), me).wait_recv()
                fw = copy(t, 4 + j, (*chip, c), sibling)
                fw.start()
                passed.append(fw)
        for t in range(nt):
            copy(t, 0, sibling, me).wait_recv()
        for j, chip in enumerate(chips):
            for t in range(nt):
                copy(t, 4 + j, (*chip, 1 - c), me).wait_recv()
        for cp in first + passed:
            cp.wait_send()
        for cp in locals_:
            cp.wait()

    out_shape = [jax.ShapeDtypeStruct(s.shape[:ld] + (NDEV,) + s.shape[ld:], s.dtype) for s, ld in zip(shards, leads)]
    return pl.pallas_call(
        body, name=name, in_specs=[ANY] * nt, out_specs=[ANY] * nt, out_shape=out_shape,
        scratch_shapes=[pltpu.SemaphoreType.DMA((nt, 7)), pltpu.SemaphoreType.DMA((nt, 7)),
                        pltpu.SemaphoreType.DMA((nt,))],
    )(*shards)


def _exchange_sibling(name, gs):
    nt = len(gs)

    def body(*refs):
        xs, outs = refs[:nt], refs[nt:2 * nt]
        send_sems, recv_sems = refs[2 * nt:]
        x, y, c, _ = _mesh_pos()
        sibling = (x, y, 1 - c)
        copies = []
        for t in range(nt):
            for ch in range(4):
                copies.append(pltpu.make_async_remote_copy(
                    src_ref=xs[t].at[2 * ch + (1 - c)], dst_ref=outs[t].at[ch],
                    send_sem=send_sems.at[t, ch], recv_sem=recv_sems.at[t, ch],
                    device_id=sibling, device_id_type=MESH))
        for cp in copies:
            cp.start()
        for cp in copies:
            cp.wait()

    out_shape = [jax.ShapeDtypeStruct((4,) + g.shape[1:], g.dtype) for g in gs]
    return pl.pallas_call(
        body, name=name, in_specs=[ANY] * nt, out_specs=[ANY] * nt, out_shape=out_shape,
        scratch_shapes=[pltpu.SemaphoreType.DMA((nt, 4)), pltpu.SemaphoreType.DMA((nt, 4))],
    )(*gs)


def _pair_sum(name, gs, recv):
    c = lax.axis_index("c")
    outs = []
    for t, (g, rv) in enumerate(zip(gs, recv)):
        _, r, cols = rv.shape

        def body(c_ref, g_ref, r_ref, o_ref):
            o_ref[...] = (g_ref[...].astype(F32) + r_ref[...].astype(F32)).astype(o_ref.dtype)

        outs.append(pl.pallas_call(
            body, name=f"{name}_{t}",
            grid_spec=pltpu.PrefetchScalarGridSpec(
                num_scalar_prefetch=1, grid=(4,),
                in_specs=[pl.BlockSpec((None, r, cols), lambda ch, cr: (2 * ch + cr[0], 0, 0)),
                          pl.BlockSpec((None, r, cols), lambda ch, cr: (ch, 0, 0))],
                out_specs=pl.BlockSpec((None, r, cols), lambda ch, cr: (ch, 0, 0))),
            out_shape=jax.ShapeDtypeStruct(rv.shape, rv.dtype),
        )(jnp.reshape(c, (1,)).astype(jnp.int32), g, rv))
    return outs


def _final_sum(name, ps, recv, transposed):
    chip = 2 * lax.axis_index("x") + lax.axis_index("y")
    outs = []
    for t, (p, rv) in enumerate(zip(ps, recv)):
        _, r, cols = rv.shape
        tr_out = transposed[t]
        oshape = (cols, r) if tr_out else (r, cols)

        def body(c_ref, p_ref, r0_ref, r1_ref, r2_ref, o_ref):
            acc = ((p_ref[...].astype(F32) + r0_ref[...].astype(F32)) + r1_ref[...].astype(F32)) + r2_ref[...].astype(F32)
            o_ref[...] = acc.T if tr_out else acc

        outs.append(pl.pallas_call(
            body, name=f"{name}_{t}",
            grid_spec=pltpu.PrefetchScalarGridSpec(
                num_scalar_prefetch=1, grid=(1,),
                in_specs=[pl.BlockSpec((None, r, cols), lambda i, cr: (cr[0], 0, 0))] +
                         [pl.BlockSpec((None, r, cols), lambda i, cr, j=j: (j, 0, 0)) for j in range(3)],
                out_specs=pl.BlockSpec(oshape, lambda i, cr: (0, 0))),
            out_shape=jax.ShapeDtypeStruct(oshape, F32),
        )(jnp.reshape(chip, (1,)).astype(jnp.int32), p, rv, rv, rv))
    return outs


def _sum_gathered(g):
    _, r, cols = g.shape

    def body(g_ref, o_ref):
        acc = g_ref[0]
        for d in range(1, NDEV):
            acc = acc + g_ref[d]
        o_ref[...] = acc

    return pl.pallas_call(body, name="small_sum", out_shape=jax.ShapeDtypeStruct((r, cols), F32))(g)


HBM = pl.BlockSpec(memory_space=pltpu.HBM)
SEM = pl.BlockSpec(memory_space=pltpu.SEMAPHORE)
EFFECT = pltpu.SideEffectType.DATAFLOW_SIDE_EFFECTING
NREL = NDEV - 1


def _related(k):
    x, y, c = lax.axis_index("x"), lax.axis_index("y"), lax.axis_index("c")
    px = 1 - x if k & 4 else x
    py = 1 - y if k & 2 else y
    pc = 1 - c if k & 1 else c
    return (px, py, pc), _dev_index(px, py, pc)


def _in_hbm(a):
    return pltpu.with_memory_space_constraint(a, pltpu.HBM)


ALL_RELS = tuple(range(1, NDEV))
CHIP_RELS = (4, 2, 6)


def _split_copies(xs, lands, send_sems, recv_sems, src_of, dst_of, rels):
    copies = []
    for t in range(len(xs)):
        for q, k in enumerate(rels):
            peer, peer_idx = _related(k)
            copies.append(pltpu.make_async_remote_copy(
                src_ref=src_of(xs[t], t, peer_idx), dst_ref=dst_of(lands[t], t, q, peer_idx),
                send_sem=send_sems.at[t * len(rels) + q], recv_sem=recv_sems.at[t * len(rels) + q],
                device_id=peer, device_id_type=MESH))
    return copies


def _exchange_start(name, xs, lands, src_of, dst_of, after, rels=ALL_RELS):
    nt = len(xs)

    def body(*refs):
        x_refs, land_refs = refs[:nt], refs[nt:2 * nt]
        send_sems, recv_sems = refs[2 * nt + 1], refs[2 * nt + 2]
        token = refs[-1]
        for cp in _split_copies(x_refs, land_refs, send_sems, recv_sems, src_of, dst_of, rels):
            cp.start()
        token[...] = jnp.zeros_like(token)

    res = pl.pallas_call(
        body, name=name,
        out_shape=(pltpu.SemaphoreType.DMA((nt * len(rels),)), pltpu.SemaphoreType.DMA((nt * len(rels),)),
                   *[pltpu.HBM(a.shape, a.dtype) for a in xs], *[pltpu.HBM(a.shape, a.dtype) for a in lands],
                   jax.ShapeDtypeStruct((8, 128), F32)),
        in_specs=[HBM] * (2 * nt) + [ANY],
        out_specs=(SEM, SEM, *[HBM] * (2 * nt), pl.BlockSpec(memory_space=pltpu.VMEM)),
        input_output_aliases={i: 2 + i for i in range(2 * nt)},
        compiler_params=pltpu.CompilerParams(has_side_effects=EFFECT),
    )(*[_in_hbm(a) for a in xs], *[_in_hbm(a) for a in lands], after)
    return res[0], res[1], res[2:2 + nt], res[2 + nt:2 + 2 * nt], res[-1]


def _exchange_wait(name, send_sems, recv_sems, xs, lands, src_of, dst_of, after, rels=ALL_RELS):
    nt = len(xs)

    def body(*refs):
        x_refs, land_refs = refs[:nt], refs[nt:2 * nt]
        send_sems, recv_sems = refs[2 * nt], refs[2 * nt + 1]
        for cp in _split_copies(x_refs, land_refs, send_sems, recv_sems, src_of, dst_of, rels):
            cp.wait_send()
            cp.wait_recv()

    res = pl.pallas_call(
        body, name=name,
        out_shape=(*[pltpu.HBM(a.shape, a.dtype) for a in xs], *[pltpu.HBM(a.shape, a.dtype) for a in lands]),
        in_specs=[HBM] * (2 * nt) + [SEM, SEM, ANY], out_specs=tuple([HBM] * (2 * nt)),
        input_output_aliases={i: i for i in range(2 * nt)},
        compiler_params=pltpu.CompilerParams(has_side_effects=EFFECT),
    )(*xs, *lands, send_sems, recv_sems, after)
    return res[:nt], res[nt:]


def _gather_start(name, shards, leads, after, rels=ALL_RELS):
    def src_of(x_ref, t, peer_idx):
        return x_ref

    def dst_of(land_ref, t, k, peer_idx):
        me = _dev_index(lax.axis_index("x"), lax.axis_index("y"), lax.axis_index("c"))
        return land_ref.at[(slice(None),) * leads[t] + (me,)]

    lands = [lax.empty(s.shape[:ld] + (NDEV,) + s.shape[ld:], s.dtype) for s, ld in zip(shards, leads)]
    return _exchange_start(name, shards, lands, src_of, dst_of, after, rels)


def _gather_wait(name, started, leads, after, rels=ALL_RELS):
    send_sems, recv_sems, shards, lands, _ = started

    def src_of(x_ref, t, peer_idx):
        return x_ref

    def dst_of(land_ref, t, k, peer_idx):
        return land_ref.at[(slice(None),) * leads[t] + (peer_idx,)]

    shards, lands = _exchange_wait(name, send_sems, recv_sems, shards, lands, src_of, dst_of, after, rels)
    me = _dev_index(lax.axis_index("x"), lax.axis_index("y"), lax.axis_index("c"))
    return [lax.dynamic_update_index_in_dim(g, s, me, ld) for g, s, ld in zip(lands, shards, leads)]


SIBLING_AND_CHIPS = (1,) + CHIP_RELS


def _forward_to_sibling(name, gathered, leads):
    nt = len(gathered)

    def body(*refs):
        ins, outs = refs[:nt], refs[nt:2 * nt]
        send_sems, recv_sems = refs[2 * nt:]
        x, y, c, chips = _mesh_pos()
        copies, arrivals = [], []
        for t in range(nt):
            for j, chip in enumerate(chips):
                def block(core):
                    return outs[t].at[(slice(None),) * leads[t] + (_dev_index(*chip, core),)]
                copies.append(pltpu.make_async_remote_copy(
                    src_ref=block(c), dst_ref=block(c), send_sem=send_sems.at[t, j], recv_sem=recv_sems.at[t, j],
                    device_id=(x, y, 1 - c), device_id_type=MESH))
                arrivals.append(pltpu.make_async_remote_copy(
                    src_ref=block(1 - c), dst_ref=block(1 - c), send_sem=send_sems.at[t, j], recv_sem=recv_sems.at[t, j],
                    device_id=(x, y, 1 - c), device_id_type=MESH))
        for cp in copies:
            cp.start()
        for cp in arrivals:
            cp.wait_recv()
        for cp in copies:
            cp.wait_send()

    return pl.pallas_call(
        body, name=name, in_specs=[ANY] * nt, out_specs=[ANY] * nt,
        out_shape=[jax.ShapeDtypeStruct(g.shape, g.dtype) for g in gathered],
        input_output_aliases={t: t for t in range(nt)},
        scratch_shapes=[pltpu.SemaphoreType.DMA((nt, 3)), pltpu.SemaphoreType.DMA((nt, 3))],
    )(*gathered)


def _scatter_src(x_ref, t, peer_idx):
    return x_ref.at[peer_idx]


def _scatter_dst(land_ref, t, q, peer_idx):
    return land_ref.at[q]


def _chips_src(x_ref, t, peer_idx):
    return x_ref.at[peer_idx // 2]


def _chips_start(name, ps, after):
    lands = [lax.empty((len(CHIP_RELS),) + p.shape[1:], p.dtype) for p in ps]
    return _exchange_start(name, ps, lands, _chips_src, _scatter_dst, after, CHIP_RELS)


def _chips_wait(name, started, after):
    send_sems, recv_sems, ps, lands, _ = started
    return _exchange_wait(name, send_sems, recv_sems, ps, lands, _chips_src, _scatter_dst, after, CHIP_RELS)


def _scatter_start(name, gs, after):
    lands = [lax.empty((NREL,) + g.shape[1:], g.dtype) for g in gs]
    return _exchange_start(name, gs, lands, _scatter_src, _scatter_dst, after)


def _scatter_wait(name, started, after, transposed):
    send_sems, recv_sems, gs, lands, _ = started
    gs, lands = _exchange_wait(name, send_sems, recv_sems, gs, lands, _scatter_src, _scatter_dst, after)
    me = _dev_index(lax.axis_index("x"), lax.axis_index("y"), lax.axis_index("c"))
    outs = []
    for t, (g, rv) in enumerate(zip(gs, lands)):
        _, r, cols = rv.shape
        tr_out = transposed[t]
        oshape = (cols, r) if tr_out else (r, cols)

        def body(c_ref, own_ref, rv_ref, o_ref):
            acc = own_ref[...].astype(F32)
            for k in range(NREL):
                acc = acc + rv_ref[k].astype(F32)
            o_ref[...] = acc.T if tr_out else acc

        outs.append(pl.pallas_call(
            body, name=f"{name}_sum_{t}",
            grid_spec=pltpu.PrefetchScalarGridSpec(
                num_scalar_prefetch=1, grid=(1,),
                in_specs=[pl.BlockSpec((None, r, cols), lambda i, cr: (cr[0], 0, 0)),
                          pl.BlockSpec((NREL, r, cols), lambda i, cr: (0, 0, 0))],
                out_specs=pl.BlockSpec(oshape, lambda i, cr: (0, 0))),
            out_shape=jax.ShapeDtypeStruct(oshape, F32), compiler_params=_cparams(VMEM_BIG),
        )(jnp.reshape(me, (1,)).astype(jnp.int32), g, rv))
    return outs


def _adamw(w, g, m, v):
    shape = w.shape
    cols = shape[-1]
    rows = math.prod(shape[:-1]) if len(shape) > 1 else 1
    w2, g2, m2, v2 = (jnp.reshape(t, (rows, cols)) for t in (w, g, m, v))
    tr = _pick(rows, (1024, 704, 512, 256, 128)) if rows * cols > 65536 else rows
    c1 = 1.0 / (1.0 - ADAM_B1 ** ADAM_STEP)
    c2 = 1.0 / (1.0 - ADAM_B2 ** ADAM_STEP)

    def body(w_ref, g_ref, m_ref, v_ref, d_ref, nm_ref, nv_ref):
        gv = g_ref[...]
        nm = ADAM_B1 * m_ref[...] + (1.0 - ADAM_B1) * gv
        nv = ADAM_B2 * v_ref[...] + (1.0 - ADAM_B2) * (gv * gv)
        d_ref[...] = -ADAM_LR * ((nm * c1) / (jnp.sqrt(nv * c2) + ADAM_EPS) + ADAM_WD * w_ref[...])
        nm_ref[...] = nm
        nv_ref[...] = nv

    spec = pl.BlockSpec((tr, cols), lambda i: (i, 0))
    osh = jax.ShapeDtypeStruct((rows, cols), F32)
    d, nm, nv = pl.pallas_call(
        body, name="adamw", grid=(rows // tr,), in_specs=[spec] * 4, out_specs=[spec] * 3, out_shape=[osh] * 3,
        compiler_params=_cparams(VMEM_BIG),
    )(w2, g2, m2, v2)
    return jnp.reshape(d, shape), jnp.reshape(nm, shape), jnp.reshape(nv, shape)


def _pad_heads(w, name):
    if name not in P_HEADS:
        return w
    nh, real = P_HEADS[name]
    w = jnp.reshape(w, w.shape[:-2] + (nh, real, w.shape[-1]))
    w = jnp.pad(w, [(0, 0)] * (w.ndim - 2) + [(0, HP - real), (0, 0)])
    return jnp.reshape(w, w.shape[:-3] + (nh * HP, w.shape[-1]))


def _unpad_heads(w, name):
    if name not in P_HEADS:
        return w
    nh, real = P_HEADS[name]
    w = jnp.reshape(w, w.shape[:-2] + (nh, HP, w.shape[-1]))[..., :real, :]
    return jnp.reshape(w, w.shape[:-3] + (nh * real, w.shape[-1]))


def _win_pad(win_t):
    segs, o = {}, 0
    for n, s in zip(IN_NAMES, IN_SIZES):
        segs[n] = win_t[..., o:o + s, :]
        o += s
    return jnp.concatenate([_pad_heads(segs[n], n) for n in P_ORDER], axis=-2)


def _win_unpad(win_p):
    segs = {n: _unpad_heads(win_p[..., P_OFF[n]:P_OFF[n] + P_WIDTH[n], :], n) for n in P_ORDER}
    return jnp.concatenate([segs[n] for n in IN_NAMES], axis=-2)


def _t(w):
    return jnp.swapaxes(w, -1, -2)


def _ffn_stacked(g_g, g_u, g_d):
    wg, wu, wd = (jnp.reshape(g, (2, DFF, D)) for g in (g_g, g_u, g_d))
    return [(wg, wu, wd, (j,)) for j in range(2)]


def _ffn_single(g_g, g_u, g_d):
    return tuple(jnp.reshape(g, (DFF, D)) for g in (g_g, g_u, g_d)) + ((),)


def _layer_weights(ffn, g_in, g_pa, g_pb, g_out, gains, w2, b2, bm, gn, gq, gk):
    w2p = jnp.pad(jnp.reshape(w2, (2, GLA_RANK, GLA_H, GLA_DK)), ((0, 0), (0, HP - GLA_RANK), (0, 0), (0, HP - GLA_DK)))
    b2p = jnp.pad(jnp.reshape(b2, (2, 1, GLA_H, GLA_DK)), ((0, 0), (0, 0), (0, 0), (0, HP - GLA_DK)))
    wpb_t = jnp.pad(jnp.reshape(g_pb, (D, ATT_H, HEAD_DIM)), ((0, 0), (0, 0), (0, HP - HEAD_DIM)))
    return dict(
        gains=jnp.reshape(gains, (1, 3, 1, D)), ffn=ffn,
        win_t=_win_pad(jnp.reshape(g_in, (1, D_IN, D))), wpa_t=jnp.reshape(g_pa, (1, D, 512)),
        wpb_t=jnp.reshape(wpb_t, (1, D, ATT_H * HP)), wout=jnp.reshape(g_out, (1, D, D)),
        w2p=jnp.reshape(w2p, (1, 2, HP, GLA_H * HP)), b2p=jnp.reshape(b2p, (1, 2, 1, GLA_H * HP)),
        bm=jnp.reshape(bm, (1, 2, 1, D)), gn=jnp.reshape(gn, (1, 1, GLA_H * HP)),
        gq=jnp.pad(jnp.reshape(gq, (1, 1, HEAD_DIM)), ((0, 0), (0, 0), (0, HP - HEAD_DIM))),
        gk=jnp.pad(jnp.reshape(gk, (1, 1, HEAD_DIM)), ((0, 0), (0, 0), (0, HP - HEAD_DIM))))


def _layer_fwd_lower(h, z, ffn0, gain1):
    return _ffn_fwd(h, z, *ffn0, gain1)


def _layer_fwd_upper(h, z, s0, w, tabs, next_gain):
    h, z, s1 = _mixer_fwd(h, z, w, 0, tabs, w["gains"][0, 2])
    h, z, s2 = _ffn_fwd(h, z, *w["ffn"][1], next_gain)
    return h, z, (s0, s1, s2)


def _layer_fwd(h, z, w, tabs, next_gain):
    h, z, s0 = _layer_fwd_lower(h, z, w["ffn"][0], w["gains"][0, 1])
    return _layer_fwd_upper(h, z, s0, w, tabs, next_gain)


def _layer_bwd_upper(dh, dhb, saved, w, tabs):
    _, s1, s2 = saved
    dh, dhb, dg2, dwg1, dwu1, dwd1 = _ffn_bwd(dh, dhb, s2, w["gains"][0, 2], *w["ffn"][1])
    dh, dhb, gm = _mixer_bwd(dh, dhb, s1, w["gains"][0, 1], w, 0, tabs)
    gm.update(gain2=dg2, wg1=dwg1, wu1=dwu1, wd1=dwd1)
    return dh, dhb, gm


def _layer_bwd_lower(dh, dhb, saved, w, gm):
    dh, dhb, dg0, dwg0, dwu0, dwd0 = _ffn_bwd(dh, dhb, saved[0], w["gains"][0, 0], *w["ffn"][0])
    gm.update(gain0=dg0, wg0=dwg0, wu0=dwu0, wd0=dwd0)
    return dh, dhb, gm


def _layer_bwd(dh, dhb, saved, w, tabs):
    dh, dhb, gm = _layer_bwd_upper(dh, dhb, saved, w, tabs)
    return _layer_bwd_lower(dh, dhb, saved, w, gm)


def _blocks(ts):
    return [jnp.reshape(t, (NDEV, t.shape[0] // NDEV, t.shape[1])) for t in ts]


def _upper_grads(g):
    d_in = _win_unpad(g["win_t"])
    d_pb = jnp.reshape(jnp.reshape(g["wpb_t"], (D, ATT_H, HP))[:, :, :HEAD_DIM], (D, 512))
    return _blocks([g["wg1"], g["wu1"], g["wd1"], d_in, g["wpa_t"], d_pb, g["wout"]])


def _lower_grads(g):
    return _blocks([g["wg0"], g["wu0"], g["wd0"]])


def _big_grads(g):
    return _lower_grads(g) + _upper_grads(g)


def kernel(x, meta_tokens, norm_gains, ffn_w_gate, ffn_w_up, ffn_w_down, w_in, gla_w2, gla_b2, gla_gn, q_norm, k_norm, w_pa, w_pb, b_merge, w_out, final_norm, loss_target, m_meta_tokens, m_norm_gains, m_ffn_w_gate, m_ffn_w_up, m_ffn_w_down, m_w_in, m_gla_w2, m_gla_b2, m_gla_gn, m_q_norm, m_k_norm, m_w_pa, m_w_pb, m_b_merge, m_w_out, m_final_norm, v_meta_tokens, v_norm_gains, v_ffn_w_gate, v_ffn_w_up, v_ffn_w_down, v_w_in, v_gla_w2, v_gla_b2, v_gla_gn, v_q_norm, v_k_norm, v_w_pa, v_w_pb, v_b_merge, v_w_out, v_final_norm):
    dev = _dev_index(lax.axis_index("x"), lax.axis_index("y"), lax.axis_index("c"))
    sh_g = _t(ffn_w_gate).astype(BF16)
    sh_u = _t(ffn_w_up).astype(BF16)
    sh_d = ffn_w_down.astype(BF16)
    sh_in = _t(w_in).astype(BF16)
    sh_pa = _t(w_pa).astype(BF16)
    sh_pb = _t(w_pb).astype(BF16)
    sh_out = w_out.astype(BF16)
    small = jnp.concatenate([jnp.reshape(t, (-1, 128)) for t in
                             (meta_tokens, norm_gains, gla_w2, gla_b2, b_merge)], axis=0)
    small = jnp.pad(small, ((0, 2), (0, 0)))
    def shards(l):
        return [sh_g[l], sh_u[l], sh_d[l], sh_in[l], sh_pa[l], sh_pb[l], sh_out[l]]

    w_leads = [1, 1, 1, 0, 0, 0, 0]
    *g0_ffn0, g_small = _all_gather("gather_layer0", [sh_g[0, 0], sh_u[0, 0], sh_d[0, 0], small], [0, 0, 0, 0])
    rest0 = [sh_g[0, 1], sh_u[0, 1], sh_d[0, 1], sh_in[0], sh_pa[0], sh_pb[0], sh_out[0]]
    rest_leads = [0] * len(rest0)
    started0 = _gather_start("gather_start_0", rest0, rest_leads, g_small, SIBLING_AND_CHIPS)
    meta_full = jnp.reshape(jnp.transpose(g_small[:, 0:16], (1, 0, 2)), (NMETA, D)) + started0[4][0, 0]
    gains_full = jnp.reshape(jnp.transpose(jnp.reshape(g_small[:, 16:28], (NDEV, DEPTH, 3, 128)), (1, 2, 0, 3)), (DEPTH, 3, D))
    w2_full = jnp.reshape(jnp.transpose(jnp.reshape(g_small[:, 28:60], (NDEV, DEPTH, 2, GLA_RANK, 32)), (1, 2, 3, 0, 4)),
                          (DEPTH, 2, GLA_RANK, 256))
    b2_full = jnp.reshape(jnp.transpose(jnp.reshape(g_small[:, 60:62], (NDEV, DEPTH, 2, 32)), (1, 2, 0, 3)), (DEPTH, 2, 256))
    bm_full = jnp.reshape(jnp.transpose(jnp.reshape(g_small[:, 62:70], (NDEV, DEPTH, 2, 128)), (1, 2, 0, 3)), (DEPTH, 2, D))

    def layer_weights(l, ffn, others, gains_l):
        return _layer_weights(ffn, *others, gains_l, w2_full[l], b2_full[l], bm_full[l], gla_gn[l], q_norm[l], k_norm[l])

    xl = x[0]
    lp = xl.shape[0] + NULL + NMETA
    tabs = _rope_tables(lp)
    h = jnp.concatenate([jnp.zeros((NULL, D), F32), meta_full, xl], axis=0)
    weights, saved, started = [], [], {}
    z = _rmsnorm_fwd(h, jnp.reshape(gains_full[0, 0], (1, D)))
    for l in range(DEPTH):
        next_gain = jnp.reshape(gains_full[l + 1, 0], (1, D)) if l + 1 < DEPTH else None
        if l == 0:
            ffn0 = _ffn_single(*g0_ffn0)
            h, z, s0 = _layer_fwd_lower(h, z, ffn0, jnp.reshape(gains_full[0, 1], (1, D)))
            rest = _forward_to_sibling("gather_forward_0", _gather_wait("gather_wait_0", started0, rest_leads, h,
                                                                         SIBLING_AND_CHIPS), rest_leads)
            started[1] = _gather_start("gather_start_1", shards(1), w_leads, rest[0])
            weights.append(layer_weights(0, [ffn0, _ffn_single(*rest[:3])], rest[3:], gains_full[0]))
            z = z + started[1][4][0, 0].astype(BF16)
            h, z, sv = _layer_fwd_upper(h, z, s0, weights[0], tabs, next_gain)
        else:
            tok = jnp.zeros((), F32)
            if l < DEPTH - 1:
                started[l + 1] = _gather_start(f"gather_start_{l + 1}", shards(l + 1), w_leads, h)
                tok = started[l + 1][4][0, 0]
            gathered = _gather_wait(f"gather_wait_{l}", started[l], w_leads, h)
            weights.append(layer_weights(l, _ffn_stacked(*gathered[:3]), gathered[3:], gains_full[l] + tok))
            h, z, sv = _layer_fwd(h, z, weights[l], tabs, next_gain)
        saved.append(sv)
    loss, dh, dhb, d_final = _loss_head(h, loss_target[0], jnp.reshape(final_norm, (1, D)))
    loss = lax.psum(loss[0, 0], ("x", "y", "c"))

    grads, scattering = [None] * DEPTH, {}
    tok = jnp.zeros((), F32)
    for l in reversed(range(DEPTH)):
        w = dict(weights[l], gains=weights[l]["gains"] + tok)
        if l > 0:
            dh, dhb, grads[l] = _layer_bwd(dh, dhb, saved[l], w, tabs)
            scattering[l] = _scatter_start(f"scatter_start_{l}", _big_grads(grads[l]), dhb)
            tok = scattering[l][4][0, 0]
        else:
            dh, dhb, gm = _layer_bwd_upper(dh, dhb, saved[l], w, tabs)
            ups = _upper_grads(gm)
            pair = _pair_sum("rs_pair_up", ups, _exchange_sibling("rs_sibling_up", ups))
            scattering[l] = _chips_start(f"scatter_start_{l}", pair, dhb)
            dhb = dhb + scattering[l][4][0, 0].astype(BF16)
            dh, dhb, grads[l] = _layer_bwd_lower(dh, dhb, saved[l], w, gm)
    grad_x = dh[NULL + NMETA:][None]
    t_lower, t_upper = [False, False, False], [False, False, False, True, True, True, False]
    lows = _lower_grads(grads[0])
    pair_lo = _pair_sum("rs_pair_lo", lows, _exchange_sibling("rs_sibling_lo", lows))
    started_lo = _chips_start("scatter_start_lo", pair_lo, dhb)
    red = [None] * DEPTH
    for l in reversed(range(1, DEPTH)):
        red[l] = _scatter_wait(f"scatter_wait_{l}", scattering[l], started_lo[4], t_lower + t_upper)
    pair, recv = _chips_wait("scatter_wait_0", scattering[0], red[1][-1])
    red_upper = _final_sum("rs_sum_up", pair, recv, t_upper)
    pair_lo, recv_lo = _chips_wait("scatter_wait_lo", started_lo, red_upper[-1])
    red[0] = _final_sum("rs_sum_lo", pair_lo, recv_lo, t_lower) + red_upper
    g_gate = jnp.stack([jnp.stack([red[l][0], red[l][3]]) for l in range(DEPTH)])
    g_up = jnp.stack([jnp.stack([red[l][1], red[l][4]]) for l in range(DEPTH)])
    g_down = jnp.stack([jnp.stack([red[l][2], red[l][5]]) for l in range(DEPTH)])
    g_win = jnp.stack([red[l][6] for l in range(DEPTH)])
    g_wpa = jnp.stack([red[l][7] for l in range(DEPTH)])
    g_wpb = jnp.stack([red[l][8] for l in range(DEPTH)])
    g_wout = jnp.stack([red[l][9] for l in range(DEPTH)])

    d_meta = dh[NULL:NULL + NMETA]
    d_gains = jnp.stack([jnp.concatenate([grads[l]["gain0"], grads[l]["gain"], grads[l]["gain2"]], axis=0)
                         for l in range(DEPTH)])
    d_w2 = jnp.stack([jnp.reshape(jnp.reshape(grads[l]["w2p"], (2, HP, GLA_H, HP))[:, :GLA_RANK, :, :GLA_DK],
                                  (2, GLA_RANK, 256)) for l in range(DEPTH)])
    d_b2 = jnp.stack([jnp.reshape(jnp.reshape(grads[l]["b2p"], (2, GLA_H, HP))[:, :, :GLA_DK], (2, 256))
                      for l in range(DEPTH)])
    d_gn = jnp.stack([grads[l]["gn"][0] for l in range(DEPTH)])
    d_gq = jnp.stack([grads[l]["gq"][0, :HEAD_DIM] for l in range(DEPTH)])
    d_gk = jnp.stack([grads[l]["gk"][0, :HEAD_DIM] for l in range(DEPTH)])
    d_bm = jnp.stack([jnp.concatenate([grads[l]["bma"], grads[l]["bmb"]], axis=0) for l in range(DEPTH)])
    parts = [d_meta, d_gains, d_w2, d_b2, d_gn, d_gq, d_gk, d_bm, d_final[0]]
    sizes = [p.size for p in parts]
    flat = jnp.concatenate([jnp.reshape(p, (-1,)) for p in parts])
    flat = jnp.reshape(flat, (-1, 128))
    nrow = flat.shape[0]
    flat = jnp.pad(flat, ((0, (-nrow) % 8), (0, 0)))
    (g_flat,) = _all_gather("gather_small_grads", [flat], [0])
    tot = jnp.reshape(_sum_gathered(g_flat), (-1,))
    full, o = [], 0
    for p, s in zip(parts, sizes):
        full.append(jnp.reshape(tot[o:o + s], p.shape))
        o += s
    f_meta, f_gains, f_w2, f_b2, f_gn, f_gq, f_gk, f_bm, f_final = full

    def mine(t, width):
        return lax.dynamic_slice_in_dim(t, dev * width, width, axis=t.ndim - 1)

    g_small = dict(meta_tokens=mine(f_meta, 128), norm_gains=mine(f_gains, 128), gla_w2=mine(f_w2, 32),
                   gla_b2=mine(f_b2, 32), gla_gn=f_gn, q_norm=f_gq, k_norm=f_gk, b_merge=mine(f_bm, 128),
                   final_norm=f_final)
    gr = dict(g_small, ffn_w_gate=g_gate, ffn_w_up=g_up, ffn_w_down=g_down, w_in=g_win, w_pa=g_wpa, w_pb=g_wpb,
              w_out=g_wout)
    ws = dict(meta_tokens=meta_tokens, norm_gains=norm_gains, ffn_w_gate=ffn_w_gate, ffn_w_up=ffn_w_up,
              ffn_w_down=ffn_w_down, w_in=w_in, gla_w2=gla_w2, gla_b2=gla_b2, gla_gn=gla_gn, q_norm=q_norm,
              k_norm=k_norm, w_pa=w_pa, w_pb=w_pb, b_merge=b_merge, w_out=w_out, final_norm=final_norm)
    ms = dict(meta_tokens=m_meta_tokens, norm_gains=m_norm_gains, ffn_w_gate=m_ffn_w_gate, ffn_w_up=m_ffn_w_up,
              ffn_w_down=m_ffn_w_down, w_in=m_w_in, gla_w2=m_gla_w2, gla_b2=m_gla_b2, gla_gn=m_gla_gn, q_norm=m_q_norm,
              k_norm=m_k_norm, w_pa=m_w_pa, w_pb=m_w_pb, b_merge=m_b_merge, w_out=m_w_out, final_norm=m_final_norm)
    vs = dict(meta_tokens=v_meta_tokens, norm_gains=v_norm_gains, ffn_w_gate=v_ffn_w_gate, ffn_w_up=v_ffn_w_up,
              ffn_w_down=v_ffn_w_down, w_in=v_w_in, gla_w2=v_gla_w2, gla_b2=v_gla_b2, gla_gn=v_gla_gn, q_norm=v_q_norm,
              k_norm=v_k_norm, w_pa=v_w_pa, w_pb=v_w_pb, b_merge=v_b_merge, w_out=v_w_out, final_norm=v_final_norm)
    names = ["meta_tokens", "norm_gains", "ffn_w_gate", "ffn_w_up", "ffn_w_down", "w_in", "gla_w2", "gla_b2", "gla_gn",
             "q_norm", "k_norm", "w_pa", "w_pb", "b_merge", "w_out", "final_norm"]
    deltas, new_m, new_v = [], [], []
    for n in names:
        if n in ("ffn_w_gate", "ffn_w_up"):
            dlt, nm, nv = (_t(r) for r in _adamw(_t(ws[n]), gr[n], _t(ms[n]), _t(vs[n])))
            gr[n] = _t(gr[n])
        else:
            dlt, nm, nv = _adamw(ws[n], gr[n], ms[n], vs[n])
        deltas.append(dlt)
        new_m.append(nm)
        new_v.append(nv)
    return (loss, grad_x, *[gr[n] for n in names], *deltas, *new_m, *new_v)
```

```python
import math

import jax
import jax.numpy as jnp
import numpy as np
from jax import lax
from jax.experimental import pallas as pl
from jax.experimental.pallas import tpu as pltpu

F32 = jnp.float32
BF16 = jnp.bfloat16
MESH = pl.DeviceIdType.MESH
ANY = pl.BlockSpec(memory_space=pl.ANY)

NDEV = 8
D = 1024
DFF = 2816
DEPTH = 4
NMETA = 16
NULL = 112
GRID_W = 64
EPS = 1e-6
HP = 128
GLA_H = 4
GLA_DK = 64
GLA_RANK = 16
GLA_TAU = 16.0
CHUNK = 64
ATT_H = 8
ATT_KV = 2
ATT_G = ATT_H // ATT_KV
HEAD_DIM = 64
ROPE_THETA = 10000.0

IN_SIZES = (256, 256, 512, 512, 16, 16, 512, 128, 128, 1024, 1024)
IN_NAMES = ("qa", "ka", "va", "ra", "lrf", "lrb", "qb", "kb", "vb", "ga", "gb")
D_IN = sum(IN_SIZES)
P_ORDER = ("qb", "ga", "gb", "qa", "ka", "va", "ra", "kb", "vb", "lrf", "lrb")
P_WIDTH = dict(qb=1024, ga=1024, gb=1024, qa=512, ka=512, va=512, ra=512, kb=256, vb=256, lrf=128, lrb=128)
P_OFF = {}
_o = 0
for _n in P_ORDER:
    P_OFF[_n] = _o
    _o += P_WIDTH[_n]
D_INP = _o
P_HEADS = dict(qa=(4, 64), ka=(4, 64), qb=(8, 64), kb=(2, 64), vb=(2, 64), lrf=(1, 16), lrb=(1, 16))

ADAM_LR = 0.001
ADAM_B1 = 0.9
ADAM_B2 = 0.999
ADAM_EPS = 1e-08
ADAM_WD = 0.01
ADAM_STEP = 10

VMEM_BIG = 58 * 1024 * 1024
MXU_N = 256


def _cparams(vmem=None):
    return pltpu.CompilerParams(vmem_limit_bytes=vmem) if vmem else pltpu.CompilerParams()


def _pick(n, prefs):
    for p in prefs:
        if n % p == 0:
            return p
    return n


def _tm(lp):
    return _pick(lp, (528, 512, 256, 128))


def _tm_wide(lp):
    return _pick(lp, (1056, 512, 256, 128))


_DN = {"nn": (((1,), (0,)), ((), ())), "nt": (((1,), (1,)), ((), ())), "tn": (((0,), (0,)), ((), ()))}


def _dot(a, b, mode="nn", precision=None):
    return lax.dot_general(a, b, _DN[mode], preferred_element_type=F32, precision=precision)


def _split(x):
    hi = x.astype(BF16)
    return hi, (x - hi.astype(F32)).astype(BF16)


def _dot_sel(t, x, mode="nn"):
    hi, lo = _split(x)
    return _dot(t, hi, mode) + _dot(t, lo, mode)


def _dot3(a, b, mode="nn"):
    ah, al = _split(a)
    bh, bl = _split(b)
    return _dot(ah, bh, mode) + (_dot(ah, bl, mode) + _dot(al, bh, mode))


def _sigmoid(x):
    return 0.5 * jnp.tanh(0.5 * x) + 0.5


def _mm(name, m, n, terms, outs, epilogue, extras=(), *, tm, tn, nk=1, nsub=1, i_outer=False, vmem=None):
    gm, gn = m // tm, n // tn
    assert gm * tm == m and gn * tn == n, (name, m, n, tm, tn)
    n_acc = 1 + max(t[3] for t in terms)

    def gmap(f):
        if i_outer:
            return lambda i, j, kk: f(i, j, kk)
        return lambda j, i, kk: f(i, j, kk)

    in_specs, args = [], []
    for a, b, mode, _, pa, pb in terms:
        kdim = a.shape[-2] if mode == "tn" else a.shape[-1]
        tk = kdim // nk
        assert tk * nk == kdim
        na, nb = (None,) * len(pa), (None,) * len(pb)
        if mode == "tn":
            in_specs.append(pl.BlockSpec(na + (tk, tm), gmap(lambda i, j, kk, pa=pa: pa + (kk, i))))
        else:
            in_specs.append(pl.BlockSpec(na + (tm, tk), gmap(lambda i, j, kk, pa=pa: pa + (i, kk))))
        if mode == "nt":
            in_specs.append(pl.BlockSpec(nb + (tn, tk), gmap(lambda i, j, kk, pb=pb: pb + (j, kk))))
        else:
            in_specs.append(pl.BlockSpec(nb + (tk, tn), gmap(lambda i, j, kk, pb=pb: pb + (kk, j))))
        args += [a, b]
    for arr, kind, off, pe in extras:
        ne = (None,) * len(pe)
        if kind == "mn":
            in_specs.append(pl.BlockSpec(ne + (tm, tn), gmap(lambda i, j, kk, off=off, pe=pe: pe + (i, j + off))))
        else:
            in_specs.append(pl.BlockSpec(ne + (1, tn), gmap(lambda i, j, kk, off=off, pe=pe: pe + (0, j + off))))
        args.append(arr)
    out_shape, out_specs = [], []
    for shape, dtype, kind, off, po in outs:
        no = (None,) * len(po)
        out_shape.append(jax.ShapeDtypeStruct(shape, dtype))
        if kind == "mn":
            out_specs.append(pl.BlockSpec(no + (tm, tn), gmap(lambda i, j, kk, off=off, po=po: po + (i, j + off))))
        else:
            assert not i_outer
            out_specs.append(pl.BlockSpec(no + (1, tn), gmap(lambda i, j, kk, off=off, po=po: po + (0, j + off))))
    n_t, n_e, n_o = len(terms), len(extras), len(outs)
    i_axis = 0 if i_outer else 1

    def body(*refs):
        ins = refs[: 2 * n_t]
        exs = refs[2 * n_t: 2 * n_t + n_e]
        ors = refs[2 * n_t + n_e: 2 * n_t + n_e + n_o]
        accs = refs[2 * n_t + n_e + n_o:]
        i = pl.program_id(i_axis)
        kk = pl.program_id(2)

        def partials(cs):
            part = [None] * n_acc
            for t, (_, _, mode, ai, _, _) in enumerate(terms):
                b_ref = ins[2 * t + 1]
                b_val = b_ref[cs, :] if mode == "nt" else b_ref[:, cs]
                p = _dot(ins[2 * t][...], b_val, mode)
                part[ai] = p if part[ai] is None else part[ai] + p
            return part

        def finish(vals, cs):
            res = epilogue(vals, [e[:, cs] for e in exs], i * tm)
            for (_, dtype, kind, _, _), o_ref, v in zip(outs, ors, res):
                if kind == "mn":
                    o_ref[:, cs] = v.astype(dtype)
                else:
                    @pl.when(i == 0)
                    def _():
                        o_ref[:, cs] = v.astype(dtype)

                    @pl.when(i != 0)
                    def _():
                        o_ref[:, cs] += v.astype(dtype)

        if nk == 1:
            w = tn // nsub
            for s in range(nsub):
                cs = slice(s * w, (s + 1) * w)
                finish(partials(cs), cs)
        else:
            part = partials(slice(None))
            @pl.when(kk == 0)
            def _():
                for a_ref, p in zip(accs, part):
                    a_ref[...] = p

            @pl.when(kk != 0)
            def _():
                for a_ref, p in zip(accs, part):
                    a_ref[...] += p

            @pl.when(kk == nk - 1)
            def _():
                finish([a_ref[...] for a_ref in accs], slice(None))

    scratch = [pltpu.VMEM((tm, tn), F32) for _ in range(n_acc)] if nk > 1 else []
    grid = (gm, gn, nk) if i_outer else (gn, gm, nk)
    res = pl.pallas_call(
        body, name=name, grid=grid, in_specs=in_specs, out_specs=out_specs, out_shape=out_shape,
        scratch_shapes=scratch, compiler_params=_cparams(vmem),
    )(*args)
    return res


def _term(a, b, mode, acc=0, pa=(), pb=()):
    return (a, b, mode, acc, tuple(pa), tuple(pb))


def _row_tile(lp):
    return _pick(lp, (384, 256, 128))


def _rmsnorm_fwd(h, gain):
    lp = h.shape[0]
    tr = _row_tile(lp)

    def body(h_ref, g_ref, z_ref):
        x = h_ref[...]
        r = lax.rsqrt(jnp.mean(x * x, axis=-1, keepdims=True) + EPS)
        z_ref[...] = (x * r * g_ref[...]).astype(BF16)

    return pl.pallas_call(
        body, name="rmsnorm_fwd", grid=(lp // tr,),
        in_specs=[pl.BlockSpec((tr, D), lambda i: (i, 0)), pl.BlockSpec((1, D), lambda i: (0, 0))],
        out_specs=pl.BlockSpec((tr, D), lambda i: (i, 0)),
        out_shape=jax.ShapeDtypeStruct((lp, D), BF16),
    )(h, gain)


def _loss_head(h, target, gain):
    lp = h.shape[0]
    tr = 128

    def body(h_ref, t_ref, g_ref, loss_ref, dh_ref, dhb_ref, dg_ref):
        i = pl.program_id(0)

        @pl.when(i == 0)
        def _():
            loss_ref[...] = jnp.zeros_like(loss_ref)
            dg_ref[...] = jnp.zeros_like(dg_ref)
            dh_ref[...] = jnp.zeros_like(dh_ref)
            dhb_ref[...] = jnp.zeros_like(dhb_ref)

        @pl.when(i != 0)
        def _():
            x = h_ref[...]
            g = g_ref[...]
            r = lax.rsqrt(jnp.mean(x * x, axis=-1, keepdims=True) + EPS)
            xh = x * r
            y = xh * g
            err = y - t_ref[...]
            loss_ref[...] += 0.5 * jnp.sum(jnp.sum(err * err, axis=-1, keepdims=True), axis=0, keepdims=True) / D
            dy = err * (1.0 / D)
            dg_ref[...] += jnp.sum(dy * xh, axis=0, keepdims=True)
            dxh = dy * g
            dx = r * (dxh - xh * jnp.mean(dxh * xh, axis=-1, keepdims=True))
            dh_ref[...] = dx
            dhb_ref[...] = dx.astype(BF16)

    row = pl.BlockSpec((tr, D), lambda i: (i, 0))
    vec = pl.BlockSpec((1, D), lambda i: (0, 0))
    return pl.pallas_call(
        body, name="loss_head", grid=(lp // tr,),
        in_specs=[row, pl.BlockSpec((tr, D), lambda i: (jnp.maximum(i - 1, 0), 0)), vec],
        out_specs=[pl.BlockSpec((1, 1), lambda i: (0, 0)), row, row, vec],
        out_shape=[jax.ShapeDtypeStruct((1, 1), F32), jax.ShapeDtypeStruct((lp, D), F32),
                   jax.ShapeDtypeStruct((lp, D), BF16), jax.ShapeDtypeStruct((1, D), F32)],
    )(h, target, gain)


def _silu_parts(g):
    s = _sigmoid(g)
    return g * s, s * (1.0 + g * (1.0 - s))


def _residual_norm_epi(scale, with_norm):
    def epi(accs, exs, row0):
        h2 = exs[0] + scale * accs[0]
        if not with_norm:
            return [h2]
        r = lax.rsqrt(jnp.mean(h2 * h2, axis=-1, keepdims=True) + EPS)
        return [h2, h2 * r * exs[1]]
    return epi


def _norm_bwd_epi(accs, exs, row0):
    dz = accs[0]
    x, res, g = exs
    r = lax.rsqrt(jnp.mean(x * x, axis=-1, keepdims=True) + EPS)
    xh = x * r
    dxh = dz * g
    dx = r * (dxh - xh * jnp.mean(dxh * xh, axis=-1, keepdims=True))
    rows = row0 + lax.broadcasted_iota(jnp.int32, (dz.shape[0], 1), 0)
    dh = jnp.where(rows >= NULL, res + dx, 0.0)
    return [dh, dh, jnp.sum(dz * xh, axis=0, keepdims=True)]


def _norm_bwd_outs(lp):
    return [((lp, D), F32, "mn", 0, ()), ((lp, D), BF16, "mn", 0, ()), ((1, D), F32, "nsum", 0, ())]


def _ffn_fwd(h, z, wg_t, wu_t, wd, pre, next_gain):
    lp = h.shape[0]
    tm = _tm(lp)

    def up_epi(accs, exs, row0):
        g, u = accs
        sg, _ = _silu_parts(g)
        return [g, u, sg * u]

    bshape = (lp, DFF)
    g_, u_, act = _mm("ffn_up", lp, DFF, [_term(z, wg_t, "nt", 0, (), pre), _term(z, wu_t, "nt", 1, (), pre)],
                      [(bshape, BF16, "mn", 0, ())] * 3, up_epi, tm=tm, tn=DFF, nsub=DFF // MXU_N, vmem=VMEM_BIG)

    with_norm = next_gain is not None
    res = _mm("ffn_down", lp, D, [_term(act, wd, "nn", 0, (), pre)],
              [((lp, D), F32, "mn", 0, ())] + ([((lp, D), BF16, "mn", 0, ())] if with_norm else []),
              _residual_norm_epi(0.5, with_norm),
              extras=[(h, "mn", 0, ())] + ([(next_gain, "n", 0, ())] if with_norm else []),
              tm=tm, tn=D, i_outer=True, vmem=VMEM_BIG)
    return res[0], (res[1] if with_norm else None), dict(h=h, z=z, g=g_, u=u_, act=act)


def _dw(name, a, b, m, n, scale=1.0):
    lp = a.shape[0]
    tm = _pick(m, (2944, 1408, 1024, 512, 256, 128))
    tn = _pick(n, (1024, 512, 256, 128)) if tm <= 1408 else _pick(n, (512, 256, 128))
    nk = lp // _pick(lp, (2112, 256, 128))

    def epi(accs, exs, row0):
        return [accs[0] * scale]

    (w,) = _mm(name, m, n, [_term(a, b, "tn")], [((m, n), BF16, "mn", 0, ())], epi, tm=tm, tn=tn, nk=nk,
               i_outer=True, vmem=VMEM_BIG)
    return w


def _ffn_bwd(dh, dhb, sv, gain, wg_t, wu_t, wd, pre):
    lp = dh.shape[0]
    tm = _tm(lp)

    def dact_epi(accs, exs, row0):
        g = exs[0].astype(F32)
        u = exs[1].astype(F32)
        da = 0.5 * accs[0]
        sg, dsg = _silu_parts(g)
        return [da * u * dsg, da * sg]

    dg_, du_ = _mm("ffn_dact", lp, DFF, [_term(dhb, wd, "nt", 0, (), pre)],
                   [((lp, DFF), BF16, "mn", 0, ())] * 2, dact_epi,
                   extras=[(sv["g"], "mn", 0, ()), (sv["u"], "mn", 0, ())], tm=tm, tn=DFF, nsub=DFF // MXU_N,
                   vmem=VMEM_BIG)
    d_wd = _dw("dw_down", sv["act"], dhb, DFF, D, 0.5)
    d_wg = _dw("dw_gate", dg_, sv["z"], DFF, D)
    d_wu = _dw("dw_up", du_, sv["z"], DFF, D)

    nk = 1
    dh2, dhb2, dgain = _mm("ffn_dz", lp, D, [_term(dg_, wg_t, "nn", 0, (), pre), _term(du_, wu_t, "nn", 0, (), pre)],
                           _norm_bwd_outs(lp), _norm_bwd_epi,
                           extras=[(sv["h"], "mn", 0, ()), (dh, "mn", 0, ()), (gain, "n", 0, ())],
                           tm=tm, tn=D, nk=nk, vmem=VMEM_BIG)
    return dh2, dhb2, dgain, d_wg, d_wu, d_wd


def _gla_gates(hin, w2p, b2p):
    lp = hin.shape[0]
    tr = _row_tile(lp)
    bf, bb = P_OFF["lrf"] // HP, P_OFF["lrb"] // HP

    def body(lf_ref, lb_ref, w_ref, b_ref, o_ref, c_ref):
        i = pl.program_id(0)
        rows = i * tr + lax.broadcasted_iota(jnp.int32, (tr, 1), 0)
        r = lax.broadcasted_iota(jnp.int32, (tr, tr), 0)
        c = lax.broadcasted_iota(jnp.int32, (tr, tr), 1)
        same = (r // CHUNK) == (c // CHUNK)
        for d, l_ref in enumerate((lf_ref, lb_ref)):
            logit = _dot3(l_ref[...].astype(F32), w_ref[d]) + b_ref[d]
            g = jnp.where(rows >= NULL, jax.nn.log_sigmoid(logit) * (1.0 / GLA_TAU), 0.0)
            o_ref[d] = g
            tmat = jnp.where(same & ((r >= c) if d == 0 else (r <= c)), 1.0, 0.0).astype(BF16)
            c_ref[d] = _dot_sel(tmat, g)

    spec = pl.BlockSpec((2, tr, 512), lambda i: (0, i, 0))
    return pl.pallas_call(
        body, name="gla_gates", grid=(lp // tr,),
        in_specs=[pl.BlockSpec((tr, HP), lambda i: (i, bf)), pl.BlockSpec((tr, HP), lambda i: (i, bb)),
                  pl.BlockSpec((2, HP, 512), lambda i: (0, 0, 0)), pl.BlockSpec((2, 1, 512), lambda i: (0, 0, 0))],
        out_specs=[spec, spec],
        out_shape=[jax.ShapeDtypeStruct((2, lp, 512), F32)] * 2,
    )(hin, hin, w2p, b2p)


def _gla_rows(lp):
    return _pick(lp, (384, 256, 128))


def _tri(d):
    r = lax.broadcasted_iota(jnp.int32, (CHUNK, CHUNK), 0)
    c = lax.broadcasted_iota(jnp.int32, (CHUNK, CHUNK), 1)
    return (r >= c) if d == 0 else (r <= c)


def _gla_fwd(hin, gates):
    lp = hin.shape[0]
    rb = _gla_rows(lp)
    nb = lp // rb
    cpb = rb // CHUNK
    nchunk = lp // CHUNK
    qo, ko, vo = P_OFF["qa"] // 512, P_OFF["ka"] // 512, P_OFF["va"] // 512
    scale = GLA_DK ** -0.5

    def body(qf, kf, vf, gf, qb, kb, vb_, gb, of, ob, sf, sb, st):
        @pl.when(pl.program_id(0) == 0)
        def _():
            st[...] = jnp.zeros_like(st)

        ins = ((qf, kf, vf, gf, of, sf), (qb, kb, vb_, gb, ob, sb))
        for ci in range(cpb):
            for d in range(2):
                q_ref, k_ref, v_ref, g_ref, o_ref, s_ref = ins[d]
                tri = _tri(d)
                c = ci if d == 0 else cpb - 1 - ci
                rows = slice(c * CHUNK, (c + 1) * CHUNK)
                for h in range(GLA_H):
                    sl = slice(h * HP, (h + 1) * HP)
                    q = q_ref[rows, sl].astype(F32) * scale
                    k = k_ref[rows, sl].astype(F32)
                    v = v_ref[rows, sl].astype(F32)
                    b = g_ref[rows, sl]
                    btot = b[CHUNK - 1:CHUNK] if d == 0 else b[0:1]
                    qd = (q * jnp.exp(b)).astype(BF16)
                    ki = (k * jnp.exp(-b)).astype(BF16)
                    ke = (k * jnp.exp(btot - b)).astype(BF16)
                    vb = v.astype(BF16)
                    att = jnp.where(tri, _dot(qd, ki, "nt"), 0.0)
                    s_prev = st[d, h]
                    o_ref[rows, sl] = _dot(att.astype(BF16), vb) + _dot(qd, s_prev.astype(BF16), "nt")
                    s_ref[h, c] = s_prev
                    st[d, h] = s_prev * jnp.exp(btot) + _dot(vb, ke, "tn")

    def specs(off):
        return (pl.BlockSpec((rb, 512), lambda b: (b, off)), pl.BlockSpec((rb, 512), lambda b: (nb - 1 - b, off)))

    (qf, qb), (kf, kb), (vf, vb2) = specs(qo), specs(ko), specs(vo)
    gf = pl.BlockSpec((None, rb, 512), lambda b: (0, b, 0))
    gb = pl.BlockSpec((None, rb, 512), lambda b: (1, nb - 1 - b, 0))
    of, ob = specs(0)
    sf = pl.BlockSpec((GLA_H, cpb, HP, HP), lambda b: (0, b, 0, 0))
    sb = pl.BlockSpec((GLA_H, cpb, HP, HP), lambda b: (0, nb - 1 - b, 0, 0))
    osh = jax.ShapeDtypeStruct((lp, GLA_H * HP), F32)
    ssh = jax.ShapeDtypeStruct((GLA_H, nchunk, HP, HP), F32)
    return pl.pallas_call(
        body, name="gla_fwd", grid=(nb,),
        in_specs=[qf, kf, vf, gf, qb, kb, vb2, gb], out_specs=[of, ob, sf, sb], out_shape=[osh, osh, ssh, ssh],
        scratch_shapes=[pltpu.VMEM((2, GLA_H, HP, HP), F32)], compiler_params=_cparams(VMEM_BIG),
    )(hin, hin, hin, gates, hin, hin, hin, gates)


def _gla_bwd(hin, gates, states, do):
    lp = hin.shape[0]
    rb = _gla_rows(lp)
    nb = lp // rb
    cpb = rb // CHUNK
    qo, ko, vo = P_OFF["qa"] // 512, P_OFF["ka"] // 512, P_OFF["va"] // 512
    scale = GLA_DK ** -0.5

    def body(qf, kf, vf, gf, sf, dof, qb, kb, vb_, gb, sb, dob,
             dqf, dkf, dvf, dgf, dqb, dkb, dvb, dgb, dst):
        @pl.when(pl.program_id(0) == 0)
        def _():
            dst[...] = jnp.zeros_like(dst)

        ins = ((qf, kf, vf, gf, sf, dof, dqf, dkf, dvf, dgf), (qb, kb, vb_, gb, sb, dob, dqb, dkb, dvb, dgb))
        for ci in range(cpb):
            for d in range(2):
                q_ref, k_ref, v_ref, g_ref, s_ref, do_ref, dq_ref, dk_ref, dv_ref, dg_ref = ins[d]
                tri, tri_t = _tri(d), _tri(1 - d)
                edge = lax.broadcasted_iota(jnp.int32, (CHUNK, 1), 0) == (CHUNK - 1 if d == 0 else 0)
                c = cpb - 1 - ci if d == 0 else ci
                rows = slice(c * CHUNK, (c + 1) * CHUNK)
                for h in range(GLA_H):
                    sl = slice(h * HP, (h + 1) * HP)
                    q = q_ref[rows, sl].astype(F32) * scale
                    k = k_ref[rows, sl].astype(F32)
                    v = v_ref[rows, sl].astype(F32)
                    dout = do_ref[rows, sl].astype(BF16)
                    b = g_ref[rows, sl]
                    btot = b[CHUNK - 1:CHUNK] if d == 0 else b[0:1]
                    e = jnp.exp(b)
                    ei = jnp.exp(-b)
                    et = jnp.exp(btot - b)
                    etot = jnp.exp(btot)
                    qd = q * e
                    ki = k * ei
                    ke = k * et
                    qdb, kib, keb, vb = qd.astype(BF16), ki.astype(BF16), ke.astype(BF16), v.astype(BF16)
                    att_t = jnp.where(tri_t, _dot(kib, qdb, "nt"), 0.0).astype(BF16)
                    d_att = jnp.where(tri, _dot(dout, vb, "nt"), 0.0).astype(BF16)
                    d_att_t = jnp.where(tri_t, _dot(vb, dout, "nt"), 0.0).astype(BF16)
                    s_prev = s_ref[h, c]
                    ds_t = dst[d, h]
                    ds_b = ds_t.astype(BF16)
                    dv = _dot(att_t, dout) + _dot(keb, ds_b, "nt")
                    d_qd = _dot(d_att, kib) + _dot(dout, s_prev.astype(BF16))
                    d_ki = _dot(d_att_t, qdb)
                    d_ke = _dot(vb, ds_b)
                    d_e = jnp.sum(s_prev * ds_t, axis=0, keepdims=True)
                    dst[d, h] = _dot(dout, qdb, "tn") + ds_t * etot
                    db = d_qd * qd - d_ki * ki - d_ke * ke
                    dbtot = jnp.sum(d_ke * ke, axis=0, keepdims=True) + d_e * etot
                    dq_ref[rows, sl] = (d_qd * e * scale).astype(BF16)
                    dk_ref[rows, sl] = (d_ki * ei + d_ke * et).astype(BF16)
                    dv_ref[rows, sl] = dv.astype(BF16)
                    dg_ref[rows, sl] = db + jnp.where(edge, dbtot, 0.0)

    def fw(off):
        return pl.BlockSpec((rb, 512), lambda b: (nb - 1 - b, off))

    def bw(off):
        return pl.BlockSpec((rb, 512), lambda b: (b, off))

    gf = pl.BlockSpec((None, rb, 512), lambda b: (0, nb - 1 - b, 0))
    gb = pl.BlockSpec((None, rb, 512), lambda b: (1, b, 0))
    sf = pl.BlockSpec((GLA_H, cpb, HP, HP), lambda b: (0, nb - 1 - b, 0, 0))
    sb = pl.BlockSpec((GLA_H, cpb, HP, HP), lambda b: (0, b, 0, 0))
    osh = jax.ShapeDtypeStruct((lp, GLA_H * HP), F32)
    osh_b = jax.ShapeDtypeStruct((lp, GLA_H * HP), BF16)
    res = pl.pallas_call(
        body, name="gla_bwd", grid=(nb,),
        in_specs=[fw(qo), fw(ko), fw(vo), gf, sf, fw(0), bw(qo), bw(ko), bw(vo), gb, sb, bw(0)],
        out_specs=[fw(0)] * 4 + [bw(0)] * 4, out_shape=[osh_b, osh_b, osh_b, osh] * 2,
        scratch_shapes=[pltpu.VMEM((2, GLA_H, HP, HP), F32)], compiler_params=_cparams(VMEM_BIG),
    )(hin, hin, hin, gates, states[0], do, hin, hin, hin, gates, states[1], do)
    return res[:4], res[4:]


def _gla_out_fwd(o2, hin, gn):
    lp = hin.shape[0]
    tr = _row_tile(lp)
    ro = P_OFF["ra"] // 512

    def body(of_ref, ob_ref, r_ref, gn_ref, a_ref):
        r = r_ref[...].astype(F32)
        sr, _ = _silu_parts(r)
        for h in range(GLA_H):
            sl = slice(h * HP, (h + 1) * HP)
            o = of_ref[:, sl] + ob_ref[:, sl]
            rs = lax.rsqrt(jnp.mean(o * o, axis=-1, keepdims=True) + EPS)
            a_ref[:, sl] = (o * rs * gn_ref[:, sl] * sr[:, sl]).astype(BF16)

    row = pl.BlockSpec((tr, 512), lambda i: (i, 0))
    return pl.pallas_call(
        body, name="gla_out_fwd", grid=(lp // tr,),
        in_specs=[row, row, pl.BlockSpec((tr, 512), lambda i: (i, ro)), pl.BlockSpec((1, 512), lambda i: (0, 0))],
        out_specs=row,
        out_shape=jax.ShapeDtypeStruct((lp, 512), BF16),
    )(o2[0], o2[1], hin, gn)


def _gla_out_bwd(da, o2, hin, gn):
    lp = hin.shape[0]
    tr = _row_tile(lp)
    ro = P_OFF["ra"] // 512

    def body(da_ref, of_ref, ob_ref, r_ref, gn_ref, do_ref, dr_ref, dgn_ref):
        i = pl.program_id(0)
        r = r_ref[...].astype(F32)
        sr, dsr = _silu_parts(r)
        da_v = da_ref[...]
        parts = []
        for h in range(GLA_H):
            sl = slice(h * HP, (h + 1) * HP)
            o = of_ref[:, sl] + ob_ref[:, sl]
            rs = lax.rsqrt(jnp.mean(o * o, axis=-1, keepdims=True) + EPS)
            oh = o * rs
            gn_h = gn_ref[:, sl]
            dah = da_v[:, sl]
            dr_ref[:, sl] = (dah * oh * gn_h * dsr[:, sl]).astype(BF16)
            t = dah * sr[:, sl]
            parts.append(jnp.sum(t * oh, axis=0, keepdims=True))
            doh = t * gn_h
            do_ref[:, sl] = rs * (doh - oh * jnp.mean(doh * oh, axis=-1, keepdims=True))
        part = jnp.concatenate(parts, axis=1)

        @pl.when(i == 0)
        def _():
            dgn_ref[...] = part

        @pl.when(i != 0)
        def _():
            dgn_ref[...] += part

    row = pl.BlockSpec((tr, 512), lambda i: (i, 0))
    return pl.pallas_call(
        body, name="gla_out_bwd", grid=(lp // tr,),
        in_specs=[row, row, row, pl.BlockSpec((tr, 512), lambda i: (i, ro)), pl.BlockSpec((1, 512), lambda i: (0, 0))],
        out_specs=[row, row, pl.BlockSpec((1, 512), lambda i: (0, 0))],
        out_shape=[jax.ShapeDtypeStruct((lp, 512), F32), jax.ShapeDtypeStruct((lp, 512), BF16),
                   jax.ShapeDtypeStruct((1, 512), F32)],
    )(da, o2[0], o2[1], hin, gn)


def _gla_in_bwd(gf, gb, gates, hin, w2p, others):
    lp = hin.shape[0]
    tr = _row_tile(lp)
    bf, bb = P_OFF["lrf"] // HP, P_OFF["lrb"] // HP
    names = tuple(others)

    def seg(name):
        return slice(P_OFF[name], P_OFF[name] + P_WIDTH[name])

    def body(dqf_ref, dkf_ref, dvf_ref, dgf_ref, dqb_ref, dkb_ref, dvb_ref, dgb_ref, g_ref, lf_ref, lb_ref, w_ref,
             *rest):
        other_refs, (o_ref, dw_ref, db_ref) = rest[:len(names)], rest[len(names):]
        i = pl.program_id(0)
        for n, ref in zip(names, other_refs):
            o_ref[:, seg(n)] = ref[...].astype(BF16)
        o_ref[:, seg("qa")] = (dqf_ref[...].astype(F32) + dqb_ref[...].astype(F32)).astype(BF16)
        o_ref[:, seg("ka")] = (dkf_ref[...].astype(F32) + dkb_ref[...].astype(F32)).astype(BF16)
        o_ref[:, seg("va")] = (dvf_ref[...].astype(F32) + dvb_ref[...].astype(F32)).astype(BF16)
        olr_ref = o_ref.at[:, P_OFF["lrf"]:P_OFF["lrf"] + 2 * HP]
        rows = i * tr + lax.broadcasted_iota(jnp.int32, (tr, 1), 0)
        r = lax.broadcasted_iota(jnp.int32, (tr, tr), 0)
        c = lax.broadcasted_iota(jnp.int32, (tr, tr), 1)
        same = (r // CHUNK) == (c // CHUNK)
        for d, (l_ref, dg_ref) in enumerate(((lf_ref, dgf_ref), (lb_ref, dgb_ref))):
            tmat = jnp.where(same & ((r <= c) if d == 0 else (r >= c)), 1.0, 0.0).astype(BF16)
            dg = _dot_sel(tmat, dg_ref[...])
            sig_neg = 1.0 - jnp.exp(GLA_TAU * g_ref[d])
            dlogit = jnp.where(rows >= NULL, dg * (1.0 / GLA_TAU) * sig_neg, 0.0)
            olr_ref[:, d * HP:(d + 1) * HP] = _dot3(dlogit, w_ref[d], "nt").astype(BF16)
            dw = _dot3(l_ref[...].astype(F32), dlogit, "tn")
            dbias = jnp.sum(dlogit, axis=0, keepdims=True)

            @pl.when(i == 0)
            def _():
                dw_ref[d] = dw
                db_ref[d] = dbias

            @pl.when(i != 0)
            def _():
                dw_ref[d] += dw
                db_ref[d] += dbias

    two = pl.BlockSpec((2, tr, 512), lambda i: (0, i, 0))
    row = pl.BlockSpec((tr, 512), lambda i: (i, 0))
    return pl.pallas_call(
        body, name="gla_in_bwd", grid=(lp // tr,),
        in_specs=[row] * 8 + [two, pl.BlockSpec((tr, HP), lambda i: (i, bf)),
                  pl.BlockSpec((tr, HP), lambda i: (i, bb)), pl.BlockSpec((2, HP, 512), lambda i: (0, 0, 0))] +
                 [pl.BlockSpec((tr, P_WIDTH[n]), lambda i: (i, 0)) for n in names],
        out_specs=[pl.BlockSpec((tr, D_INP), lambda i: (i, 0)),
                   pl.BlockSpec((2, HP, 512), lambda i: (0, 0, 0)), pl.BlockSpec((2, 1, 512), lambda i: (0, 0, 0))],
        out_shape=[jax.ShapeDtypeStruct((lp, D_INP), BF16), jax.ShapeDtypeStruct((2, HP, 512), F32),
                   jax.ShapeDtypeStruct((2, 1, 512), F32)],
        compiler_params=_cparams(VMEM_BIG),
    )(*gf, *gb, gates, hin, hin, w2p, *[others[n] for n in names])


def _rope_tables(lp):
    n_tok = lp - NULL - NMETA
    rows = n_tok // GRID_W
    row = np.repeat(np.arange(rows), GRID_W).astype(np.float32)
    col = np.tile(np.arange(GRID_W), rows).astype(np.float32)
    inv = (ROPE_THETA ** (-np.arange(0, 32, 2, dtype=np.float32) / 32)).astype(np.float32)
    ang = np.concatenate([row[:, None] * inv, col[:, None] * inv], axis=-1)
    ang = np.concatenate([np.zeros((NULL + NMETA, 32), np.float32), ang], axis=0)
    cos, sin = np.cos(ang).astype(np.float32), np.sin(ang).astype(np.float32)
    z16 = np.zeros((lp, 16), np.float32)
    z64 = np.zeros((lp, 64), np.float32)
    c = np.concatenate([cos[:, :16], cos[:, :16], cos[:, 16:], cos[:, 16:], z64], axis=1)
    a = np.concatenate([-sin[:, :16], z16, -sin[:, 16:], z16, z64], axis=1)
    b = np.concatenate([z16, sin[:, :16], z16, sin[:, 16:], z64], axis=1)
    return jnp.asarray(c), jnp.asarray(a), jnp.asarray(b)


def _rope(x, c, a, b):
    return x * c + pltpu.roll(x, HP - 16, 1) * a + pltpu.roll(x, 16, 1) * b


def _rope_t(dx, c, a, b):
    return dx * c + pltpu.roll(dx * a, 16, 1) + pltpu.roll(dx * b, HP - 16, 1)


def _attn_prep(hin, gq, gk, tabs):
    lp = hin.shape[0]
    tr = _row_tile(lp)
    qo, ko, vo = P_OFF["qb"] // 1024, P_OFF["kb"] // 256, P_OFF["vb"] // 256

    def body(q_ref, k_ref, v_ref, gq_ref, gk_ref, c_ref, a_ref, b_ref, oq_ref, ok_ref, ov_ref):
        c, a, b = c_ref[...], a_ref[...], b_ref[...]
        for src, g_ref, dst, nh, sc in ((q_ref, gq_ref, oq_ref, ATT_H, Q_SCALE), (k_ref, gk_ref, ok_ref, ATT_KV, 1.0)):
            for h in range(nh):
                sl = slice(h * HP, (h + 1) * HP)
                x = src[:, sl].astype(F32)
                r = lax.rsqrt(jnp.sum(x * x, axis=-1, keepdims=True) * (1.0 / HEAD_DIM) + EPS)
                dst[:, sl] = (_rope(x * r * g_ref[...], c, a, b) * sc).astype(BF16)
        lane = lax.broadcasted_iota(jnp.int32, (1, ATT_KV * HP), 1)
        ov_ref[...] = jnp.where(lane % HP == HEAD_DIM, 1.0, v_ref[...]).astype(BF16)

    tab = pl.BlockSpec((tr, HP), lambda i: (i, 0))
    vec = pl.BlockSpec((1, HP), lambda i: (0, 0))
    return pl.pallas_call(
        body, name="attn_prep", grid=(lp // tr,),
        in_specs=[pl.BlockSpec((tr, 1024), lambda i: (i, qo)), pl.BlockSpec((tr, 256), lambda i: (i, ko)),
                  pl.BlockSpec((tr, 256), lambda i: (i, vo)), vec, vec, tab, tab, tab],
        out_specs=[pl.BlockSpec((tr, 1024), lambda i: (i, 0)), pl.BlockSpec((tr, 256), lambda i: (i, 0)),
                   pl.BlockSpec((tr, 256), lambda i: (i, 0))],
        out_shape=[jax.ShapeDtypeStruct((lp, 1024), BF16), jax.ShapeDtypeStruct((lp, 256), BF16),
                   jax.ShapeDtypeStruct((lp, 256), BF16)],
    )(hin, hin, hin, gq, gk, *tabs)


def _attn_prep_bwd(dqr, dkr, hin, gq, gk, tabs):
    lp = hin.shape[0]
    tr = _row_tile(lp)
    qo, ko = P_OFF["qb"] // 1024, P_OFF["kb"] // 256

    def body(dq_ref, dk_ref, q_ref, k_ref, gq_ref, gk_ref, c_ref, a_ref, b_ref, oq_ref, ok_ref, dgq_ref, dgk_ref):
        i = pl.program_id(0)
        c, a, b = c_ref[...], a_ref[...], b_ref[...]
        for src, dsrc, g_ref, dst, dg_ref, nh, sc in (
                (q_ref, dq_ref, gq_ref, oq_ref, dgq_ref, ATT_H, Q_SCALE),
                (k_ref, dk_ref, gk_ref, ok_ref, dgk_ref, ATT_KV, 1.0)):
            acc = jnp.zeros((1, HP), F32)
            for h in range(nh):
                sl = slice(h * HP, (h + 1) * HP)
                x = src[:, sl].astype(F32)
                r = lax.rsqrt(jnp.sum(x * x, axis=-1, keepdims=True) * (1.0 / HEAD_DIM) + EPS)
                xh = x * r
                dxn = _rope_t(dsrc[:, sl] * sc, c, a, b)
                acc = acc + jnp.sum(dxn * xh, axis=0, keepdims=True)
                dxh = dxn * g_ref[...]
                dx = r * (dxh - xh * (jnp.sum(dxh * xh, axis=-1, keepdims=True) * (1.0 / HEAD_DIM)))
                dst[:, sl] = dx.astype(BF16)

            @pl.when(i == 0)
            def _():
                dg_ref[...] = acc

            @pl.when(i != 0)
            def _():
                dg_ref[...] += acc

    tab = pl.BlockSpec((tr, HP), lambda i: (i, 0))
    vec = pl.BlockSpec((1, HP), lambda i: (0, 0))
    return pl.pallas_call(
        body, name="attn_prep_bwd", grid=(lp // tr,),
        in_specs=[pl.BlockSpec((tr, 1024), lambda i: (i, 0)), pl.BlockSpec((tr, 256), lambda i: (i, 0)),
                  pl.BlockSpec((tr, 1024), lambda i: (i, qo)), pl.BlockSpec((tr, 256), lambda i: (i, ko)),
                  vec, vec, tab, tab, tab],
        out_specs=[pl.BlockSpec((tr, 1024), lambda i: (i, 0)), pl.BlockSpec((tr, 256), lambda i: (i, 0)), vec, vec],
        out_shape=[jax.ShapeDtypeStruct((lp, 1024), BF16), jax.ShapeDtypeStruct((lp, 256), BF16),
                   jax.ShapeDtypeStruct((1, HP), F32), jax.ShapeDtypeStruct((1, HP), F32)],
    )(dqr, dkr, hin, hin, gq, gk, *tabs)


QB = 128
GH = 2
Q_SCALE = HEAD_DIM ** -0.5 * math.log2(math.e)
LN2 = math.log(2.0)


def _stack(ref, g0, n):
    return jnp.concatenate([ref[:, (g0 + g) * HP:(g0 + g + 1) * HP] for g in range(n)], axis=0)


def _attn_fwd(qr, kr, vb):
    lp = qr.shape[0]
    nq = lp // QB

    def body(q_ref, k_ref, v_ref, o_ref, lse_ref):
        qb = pl.program_id(1)
        keys = lax.broadcasted_iota(jnp.int32, (1, lp), 1)
        lane = lax.broadcasted_iota(jnp.int32, (1, HP), 1)
        rows = qb * QB + lax.broadcasted_iota(jnp.int32, (QB, 1), 0)
        for ch in range(ATT_G // GH):
            qs = _stack(q_ref, ch * GH, GH)
            s = _dot(qs, k_ref[...], "nt")
            s = jnp.where(keys >= NULL, s, -1e30)
            m = jnp.max(s, axis=-1, keepdims=True)
            p = jnp.exp2(s - m).astype(BF16)
            o_raw = _dot(p, v_ref[...])
            l = jnp.sum(jnp.where(lane == HEAD_DIM, o_raw, 0.0), axis=-1, keepdims=True)
            o = jnp.where(lane < HEAD_DIM, o_raw / l, 0.0)
            lse = m + jnp.log2(l)
            for g in range(GH):
                sl = slice((ch * GH + g) * HP, (ch * GH + g + 1) * HP)
                o_ref[:, sl] = jnp.where(rows >= NULL, o[g * QB:(g + 1) * QB], 0.0).astype(BF16)
                lse_ref[:, sl] = jnp.broadcast_to(lse[g * QB:(g + 1) * QB], (QB, HP))

    qspec = pl.BlockSpec((QB, ATT_G * HP), lambda kv, qb: (qb, kv))
    kspec = pl.BlockSpec((lp, HP), lambda kv, qb: (0, kv))
    return pl.pallas_call(
        body, name="attn_fwd", grid=(ATT_KV, nq),
        in_specs=[qspec, kspec, kspec], out_specs=[qspec, qspec],
        out_shape=[jax.ShapeDtypeStruct((lp, ATT_H * HP), BF16), jax.ShapeDtypeStruct((lp, ATT_H * HP), F32)],
        compiler_params=_cparams(VMEM_BIG),
    )(qr, kr, vb)


def _attn_bwd(qr, kr, vb, o, lse, do):
    lp = qr.shape[0]
    nq = lp // QB

    def body(q_ref, k_ref, v_ref, o_ref, lse_ref, do_ref, dq_ref, dk_ref, dv_ref):
        qb = pl.program_id(1)

        @pl.when(qb == 0)
        def _():
            dk_ref[...] = jnp.zeros_like(dk_ref)
            dv_ref[...] = jnp.zeros_like(dv_ref)

        kb = _pick(lp, (1408, 512, 256, 128))
        chains = []
        for ch in range(ATT_G // GH):
            g0 = ch * GH
            qs = _stack(q_ref, g0, GH)
            dos = _stack(do_ref, g0, GH)
            os_ = _stack(o_ref, g0, GH).astype(F32)
            lse_s = jnp.concatenate([lse_ref[:, (g0 + g) * HP:(g0 + g) * HP + 1] for g in range(GH)], axis=0)
            delta = jnp.sum(dos * os_, axis=-1, keepdims=True) * LN2
            chains.append((qs, dos.astype(BF16), (dos * LN2).astype(BF16), lse_s, delta))
        dqs = [None] * len(chains)
        for t in range(lp // kb):
            ks = slice(t * kb, (t + 1) * kb)
            k_t, v_t = k_ref[ks, :], v_ref[ks, :]
            keys = t * kb + lax.broadcasted_iota(jnp.int32, (1, kb), 1)
            dk_acc, dv_acc = None, None
            for ci, (qs, dob, dob2, lse_s, delta) in enumerate(chains):
                s = _dot(qs, k_t, "nt")
                p = jnp.exp2(s - lse_s)
                if t == 0:
                    p = jnp.where(keys >= NULL, p, 0.0)
                dp = _dot(dob2, v_t, "nt")
                ds = (p * (dp - delta)).astype(BF16)
                dq_c = _dot(ds, k_t)
                dqs[ci] = dq_c if dqs[ci] is None else dqs[ci] + dq_c
                dv_c = _dot(p.astype(BF16), dob, "tn")
                dk_c = _dot(ds, qs, "tn")
                dv_acc = dv_c if dv_acc is None else dv_acc + dv_c
                dk_acc = dk_c if dk_acc is None else dk_acc + dk_c
            dv_ref[ks, :] += dv_acc
            dk_ref[ks, :] += dk_acc
        for ci, dq in enumerate(dqs):
            for g in range(GH):
                dq_ref[:, (ci * GH + g) * HP:(ci * GH + g + 1) * HP] = dq[g * QB:(g + 1) * QB]

    qspec = pl.BlockSpec((QB, ATT_G * HP), lambda kv, qb: (qb, kv))
    kspec = pl.BlockSpec((lp, HP), lambda kv, qb: (0, kv))
    return pl.pallas_call(
        body, name="attn_bwd", grid=(ATT_KV, nq),
        in_specs=[qspec, kspec, kspec, qspec, qspec, qspec], out_specs=[qspec, kspec, kspec],
        out_shape=[jax.ShapeDtypeStruct((lp, ATT_H * HP), F32), jax.ShapeDtypeStruct((lp, ATT_KV * HP), F32),
                   jax.ShapeDtypeStruct((lp, ATT_KV * HP), F32)],
        compiler_params=_cparams(VMEM_BIG),
    )(qr, kr, vb, o, lse, do)


def _mixer_fwd(h, z, wl, l, tabs, next_gain):
    lp = h.shape[0]
    tm = _tm(lp)

    def id_epi(accs, exs, row0):
        return [accs[0]]

    (hin,) = _mm("in_proj", lp, D_INP, [_term(z, wl["win_t"], "nt", 0, (), (l,))], [((lp, D_INP), BF16, "mn", 0, ())],
                 id_epi, tm=_tm_wide(lp), tn=D_INP // 2, vmem=VMEM_BIG)
    gates, cum = _gla_gates(hin, wl["w2p"][l], wl["b2p"][l])
    o_f, o_b, s_f, s_b = _gla_fwd(hin, cum)
    o2, states = (o_f, o_b), (s_f, s_b)
    a = _gla_out_fwd(o2, hin, wl["gn"][l])
    qr, kr, vb = _attn_prep(hin, wl["gq"][l], wl["gk"][l], tabs)
    b, lse = _attn_fwd(qr, kr, vb)

    def merge_epi(accs, exs, row0):
        pa, pb = accs
        ga, gb, bma, bmb = exs
        y = _sigmoid(ga + bma) * pa + _sigmoid(gb + bmb) * pb
        return [y, pa, pb]

    y, pa, pb = _mm("merge", lp, D, [_term(a, wl["wpa_t"], "nt", 0, (), (l,)), _term(b, wl["wpb_t"], "nt", 1, (), (l,))],
                    [((lp, D), BF16, "mn", 0, ())] * 3, merge_epi,
                    extras=[(hin, "mn", P_OFF["ga"] // D, ()), (hin, "mn", P_OFF["gb"] // D, ()),
                            (wl["bm"], "n", 0, (l, 0)), (wl["bm"], "n", 0, (l, 1))],
                    tm=_tm_wide(lp), tn=D, nsub=D // MXU_N, i_outer=True, vmem=VMEM_BIG)

    h2, z2 = _mm("out_proj", lp, D, [_term(y, wl["wout"], "nn", 0, (), (l,))],
                 [((lp, D), F32, "mn", 0, ()), ((lp, D), BF16, "mn", 0, ())], _residual_norm_epi(1.0, True),
                 extras=[(h, "mn", 0, ()), (next_gain, "n", 0, ())], tm=_tm_wide(lp), tn=D, i_outer=True, vmem=VMEM_BIG)
    sv = dict(h=h, z=z, hin=hin, gates=gates, cum=cum, o2=o2, states=states, a=a, qr=qr, kr=kr, vb=vb, b=b, lse=lse,
              y=y, pa=pa, pb=pb)
    return h2, z2, sv


def _mixer_bwd(dh, dhb, sv, gain, wl, l, tabs):
    lp = dh.shape[0]
    tm = _tm(lp)
    hin = sv["hin"]

    def merge_bwd_epi(accs, exs, row0):
        dy = accs[0]
        ga, gb, pa, pb, bma, bmb = exs
        sa = _sigmoid(ga + bma)
        sb = _sigmoid(gb + bmb)
        dga = dy * pa.astype(F32) * sa * (1.0 - sa)
        dgb = dy * pb.astype(F32) * sb * (1.0 - sb)
        return [dy * sa, dy * sb, dga, dgb, jnp.sum(dga, axis=0, keepdims=True), jnp.sum(dgb, axis=0, keepdims=True)]

    big = ((lp, D), BF16, "mn", 0, ())
    vec = ((1, D), F32, "nsum", 0, ())
    dpa, dpb, dga, dgb, dbma, dbmb = _mm(
        "merge_bwd", lp, D, [_term(dhb, wl["wout"], "nt", 0, (), (l,))], [big, big, big, big, vec, vec], merge_bwd_epi,
        extras=[(hin, "mn", P_OFF["ga"] // D, ()), (hin, "mn", P_OFF["gb"] // D, ()), (sv["pa"], "mn", 0, ()),
                (sv["pb"], "mn", 0, ()), (wl["bm"], "n", 0, (l, 0)), (wl["bm"], "n", 0, (l, 1))],
        tm=tm, tn=D, nsub=D // MXU_N, vmem=VMEM_BIG)
    d_wout = _dw("dw_out", sv["y"], dhb, D, D)
    d_wpa_t = _dw("dw_pa", dpa, sv["a"], D, 512)
    d_wpb_t = _dw("dw_pb", dpb, sv["b"], D, ATT_H * HP)

    def id_epi(accs, exs, row0):
        return [accs[0]]

    (da,) = _mm("d_a", lp, 512, [_term(dpa, wl["wpa_t"], "nn", 0, (), (l,))], [((lp, 512), F32, "mn", 0, ())], id_epi,
                tm=_tm_wide(lp), tn=512, i_outer=True, vmem=VMEM_BIG)
    (db,) = _mm("d_b", lp, ATT_H * HP, [_term(dpb, wl["wpb_t"], "nn", 0, (), (l,))],
                [((lp, ATT_H * HP), F32, "mn", 0, ())], id_epi, tm=_tm_wide(lp), tn=512, i_outer=True, vmem=VMEM_BIG)
    d_o, d_ra, d_gn = _gla_out_bwd(da, sv["o2"], hin, wl["gn"][l])
    g_fw, g_bw = _gla_bwd(hin, sv["cum"], sv["states"], d_o)
    dqr, dkr, dvb = _attn_bwd(sv["qr"], sv["kr"], sv["vb"], sv["b"], sv["lse"], db)
    d_qb, d_kb, d_gq, d_gk = _attn_prep_bwd(dqr, dkr, hin, wl["gq"][l], wl["gk"][l], tabs)
    dhin, d_w2p, d_b2p = _gla_in_bwd(g_fw, g_bw, sv["gates"], hin, wl["w2p"][l],
                                     dict(qb=d_qb, ga=dga, gb=dgb, ra=d_ra, kb=d_kb, vb=dvb))
    d_win_t = _dw("dw_in", dhin, sv["z"], D_INP, D)
    dh2, dhb2, dgain = _mm("in_proj_dz", lp, D, [_term(dhin, wl["win_t"], "nn", 0, (), (l,))], _norm_bwd_outs(lp),
                           _norm_bwd_epi, extras=[(sv["h"], "mn", 0, ()), (dh, "mn", 0, ()), (gain, "n", 0, ())],
                           tm=tm, tn=D, nk=1, vmem=VMEM_BIG)
    grads = dict(gain=dgain, wout=d_wout, wpa_t=d_wpa_t, wpb_t=d_wpb_t, win_t=d_win_t, gn=d_gn, w2p=d_w2p, b2p=d_b2p,
                 gq=d_gq, gk=d_gk, bma=dbma, bmb=dbmb)
    return dh2, dhb2, grads


def _mesh_pos():
    x, y, c = lax.axis_index("x"), lax.axis_index("y"), lax.axis_index("c")
    chips = [(1 - x, y), (x, 1 - y), (1 - x, 1 - y)]
    return x, y, c, chips


def _dev_index(x, y, c):
    return 4 * x + 2 * y + c


def _all_gather(name, shards, leads):
    nt = len(shards)

    def blk(ref, lead, idx):
        return ref.at[(slice(None),) * lead + (idx,)]

    def body(*refs):
        xs, outs = refs[:nt], refs[nt:2 * nt]
        send_sems, recv_sems, local_sems = refs[2 * nt:]
        x, y, c, chips = _mesh_pos()
        me, sibling = (x, y, c), (x, y, 1 - c)

        def copy(t, k, block, to, own=False):
            dst = blk(outs[t], leads[t], _dev_index(*block))
            return pltpu.make_async_remote_copy(
                src_ref=xs[t] if own else dst, dst_ref=dst, send_sem=send_sems.at[t, k], recv_sem=recv_sems.at[t, k],
                device_id=to, device_id_type=MESH)

        locals_ = [pltpu.make_async_copy(xs[t], blk(outs[t], leads[t], _dev_index(*me)), local_sems.at[t])
                   for t in range(nt)]
        for cp in locals_:
            cp.start()
        first = []
        for t in range(nt):
            first.append(copy(t, 0, me, sibling, own=True))
            first += [copy(t, 1 + j, me, (*chip, c), own=True) for j, chip in enumerate(chips)]
        for cp in first:
            cp.start()
        passed = []
        for j, chip in enumerate(chips):
            for t in range(nt):
                copy(t, 1 + j, (*chip, c), me).wait_recv()
                fw = copy(t, 4 + j, (*chip, c), sibling)
                fw.start()
                passed.append(fw)
        for t in range(nt):
            copy(t, 0, sibling, me).wait_recv()
        for j, chip in enumerate(chips):
            for t in range(nt):
                copy(t, 4 + j, (*chip, 1 - c), me).wait_recv()
        for cp in first + passed:
            cp.wait_send()
        for cp in locals_:
            cp.wait()

    out_shape = [jax.ShapeDtypeStruct(s.shape[:ld] + (NDEV,) + s.shape[ld:], s.dtype) for s, ld in zip(shards, leads)]
    return pl.pallas_call(
        body, name=name, in_specs=[ANY] * nt, out_specs=[ANY] * nt, out_shape=out_shape,
        scratch_shapes=[pltpu.SemaphoreType.DMA((nt, 7)), pltpu.SemaphoreType.DMA((nt, 7)),
                        pltpu.SemaphoreType.DMA((nt,))],
    )(*shards)


def _exchange_sibling(name, gs):
    nt = len(gs)

    def body(*refs):
        xs, outs = refs[:nt], refs[nt:2 * nt]
        send_sems, recv_sems = refs[2 * nt:]
        x, y, c, _ = _mesh_pos()
        sibling = (x, y, 1 - c)
        copies = []
        for t in range(nt):
            for ch in range(4):
                copies.append(pltpu.make_async_remote_copy(
                    src_ref=xs[t].at[2 * ch + (1 - c)], dst_ref=outs[t].at[ch],
                    send_sem=send_sems.at[t, ch], recv_sem=recv_sems.at[t, ch],
                    device_id=sibling, device_id_type=MESH))
        for cp in copies:
            cp.start()
        for cp in copies:
            cp.wait()

    out_shape = [jax.ShapeDtypeStruct((4,) + g.shape[1:], g.dtype) for g in gs]
    return pl.pallas_call(
        body, name=name, in_specs=[ANY] * nt, out_specs=[ANY] * nt, out_shape=out_shape,
        scratch_shapes=[pltpu.SemaphoreType.DMA((nt, 4)), pltpu.SemaphoreType.DMA((nt, 4))],
    )(*gs)


def _pair_sum(name, gs, recv):
    c = lax.axis_index("c")
    outs = []
    for t, (g, rv) in enumerate(zip(gs, recv)):
        _, r, cols = rv.shape

        def body(c_ref, g_ref, r_ref, o_ref):
            o_ref[...] = (g_ref[...].astype(F32) + r_ref[...].astype(F32)).astype(o_ref.dtype)

        outs.append(pl.pallas_call(
            body, name=f"{name}_{t}",
            grid_spec=pltpu.PrefetchScalarGridSpec(
                num_scalar_prefetch=1, grid=(4,),
                in_specs=[pl.BlockSpec((None, r, cols), lambda ch, cr: (2 * ch + cr[0], 0, 0)),
                          pl.BlockSpec((None, r, cols), lambda ch, cr: (ch, 0, 0))],
                out_specs=pl.BlockSpec((None, r, cols), lambda ch, cr: (ch, 0, 0))),
            out_shape=jax.ShapeDtypeStruct(rv.shape, rv.dtype),
        )(jnp.reshape(c, (1,)).astype(jnp.int32), g, rv))
    return outs


def _final_sum(name, ps, recv, transposed):
    chip = 2 * lax.axis_index("x") + lax.axis_index("y")
    outs = []
    for t, (p, rv) in enumerate(zip(ps, recv)):
        _, r, cols = rv.shape
        tr_out = transposed[t]
        oshape = (cols, r) if tr_out else (r, cols)

        def body(c_ref, p_ref, r0_ref, r1_ref, r2_ref, o_ref):
            acc = ((p_ref[...].astype(F32) + r0_ref[...].astype(F32)) + r1_ref[...].astype(F32)) + r2_ref[...].astype(F32)
            o_ref[...] = acc.T if tr_out else acc

        outs.append(pl.pallas_call(
            body, name=f"{name}_{t}",
            grid_spec=pltpu.PrefetchScalarGridSpec(
                num_scalar_prefetch=1, grid=(1,),
                in_specs=[pl.BlockSpec((None, r, cols), lambda i, cr: (cr[0], 0, 0))] +
                         [pl.BlockSpec((None, r, cols), lambda i, cr, j=j: (j, 0, 0)) for j in range(3)],
                out_specs=pl.BlockSpec(oshape, lambda i, cr: (0, 0))),
            out_shape=jax.ShapeDtypeStruct(oshape, F32),
        )(jnp.reshape(chip, (1,)).astype(jnp.int32), p, rv, rv, rv))
    return outs


def _sum_gathered(g):
    _, r, cols = g.shape

    def body(g_ref, o_ref):
        acc = g_ref[0]
        for d in range(1, NDEV):
            acc = acc + g_ref[d]
        o_ref[...] = acc

    return pl.pallas_call(body, name="small_sum", out_shape=jax.ShapeDtypeStruct((r, cols), F32))(g)


HBM = pl.BlockSpec(memory_space=pltpu.HBM)
SEM = pl.BlockSpec(memory_space=pltpu.SEMAPHORE)
EFFECT = pltpu.SideEffectType.DATAFLOW_SIDE_EFFECTING
NREL = NDEV - 1


def _related(k):
    x, y, c = lax.axis_index("x"), lax.axis_index("y"), lax.axis_index("c")
    px = 1 - x if k & 4 else x
    py = 1 - y if k & 2 else y
    pc = 1 - c if k & 1 else c
    return (px, py, pc), _dev_index(px, py, pc)


def _in_hbm(a):
    return pltpu.with_memory_space_constraint(a, pltpu.HBM)


ALL_RELS = tuple(range(1, NDEV))
CHIP_RELS = (4, 2, 6)


def _split_copies(xs, lands, send_sems, recv_sems, src_of, dst_of, rels):
    copies = []
    for t in range(len(xs)):
        for q, k in enumerate(rels):
            peer, peer_idx = _related(k)
            copies.append(pltpu.make_async_remote_copy(
                src_ref=src_of(xs[t], t, peer_idx), dst_ref=dst_of(lands[t], t, q, peer_idx),
                send_sem=send_sems.at[t * len(rels) + q], recv_sem=recv_sems.at[t * len(rels) + q],
                device_id=peer, device_id_type=MESH))
    return copies


def _exchange_start(name, xs, lands, src_of, dst_of, after, rels=ALL_RELS):
    nt = len(xs)

    def body(*refs):
        x_refs, land_refs = refs[:nt], refs[nt:2 * nt]
        send_sems, recv_sems = refs[2 * nt + 1], refs[2 * nt + 2]
        token = refs[-1]
        for cp in _split_copies(x_refs, land_refs, send_sems, recv_sems, src_of, dst_of, rels):
            cp.start()
        token[...] = jnp.zeros_like(token)

    res = pl.pallas_call(
        body, name=name,
        out_shape=(pltpu.SemaphoreType.DMA((nt * len(rels),)), pltpu.SemaphoreType.DMA((nt * len(rels),)),
                   *[pltpu.HBM(a.shape, a.dtype) for a in xs], *[pltpu.HBM(a.shape, a.dtype) for a in lands],
                   jax.ShapeDtypeStruct((8, 128), F32)),
        in_specs=[HBM] * (2 * nt) + [ANY],
        out_specs=(SEM, SEM, *[HBM] * (2 * nt), pl.BlockSpec(memory_space=pltpu.VMEM)),
        input_output_aliases={i: 2 + i for i in range(2 * nt)},
        compiler_params=pltpu.CompilerParams(has_side_effects=EFFECT),
    )(*[_in_hbm(a) for a in xs], *[_in_hbm(a) for a in lands], after)
    return res[0], res[1], res[2:2 + nt], res[2 + nt:2 + 2 * nt], res[-1]


def _exchange_wait(name, send_sems, recv_sems, xs, lands, src_of, dst_of, after, rels=ALL_RELS):
    nt = len(xs)

    def body(*refs):
        x_refs, land_refs = refs[:nt], refs[nt:2 * nt]
        send_sems, recv_sems = refs[2 * nt], refs[2 * nt + 1]
        for cp in _split_copies(x_refs, land_refs, send_sems, recv_sems, src_of, dst_of, rels):
            cp.wait_send()
            cp.wait_recv()

    res = pl.pallas_call(
        body, name=name,
        out_shape=(*[pltpu.HBM(a.shape, a.dtype) for a in xs], *[pltpu.HBM(a.shape, a.dtype) for a in lands]),
        in_specs=[HBM] * (2 * nt) + [SEM, SEM, ANY], out_specs=tuple([HBM] * (2 * nt)),
        input_output_aliases={i: i for i in range(2 * nt)},
        compiler_params=pltpu.CompilerParams(has_side_effects=EFFECT),
    )(*xs, *lands, send_sems, recv_sems, after)
    return res[:nt], res[nt:]


def _gather_start(name, shards, leads, after, rels=ALL_RELS):
    def src_of(x_ref, t, peer_idx):
        return x_ref

    def dst_of(land_ref, t, k, peer_idx):
        me = _dev_index(lax.axis_index("x"), lax.axis_index("y"), lax.axis_index("c"))
        return land_ref.at[(slice(None),) * leads[t] + (me,)]

    lands = [lax.empty(s.shape[:ld] + (NDEV,) + s.shape[ld:], s.dtype) for s, ld in zip(shards, leads)]
    return _exchange_start(name, shards, lands, src_of, dst_of, after, rels)


def _gather_wait(name, started, leads, after, rels=ALL_RELS):
    send_sems, recv_sems, shards, lands, _ = started

    def src_of(x_ref, t, peer_idx):
        return x_ref

    def dst_of(land_ref, t, k, peer_idx):
        return land_ref.at[(slice(None),) * leads[t] + (peer_idx,)]

    shards, lands = _exchange_wait(name, send_sems, recv_sems, shards, lands, src_of, dst_of, after, rels)
    me = _dev_index(lax.axis_index("x"), lax.axis_index("y"), lax.axis_index("c"))
    return [lax.dynamic_update_index_in_dim(g, s, me, ld) for g, s, ld in zip(lands, shards, leads)]


SIBLING_AND_CHIPS = (1,) + CHIP_RELS


def _forward_to_sibling(name, gathered, leads):
    nt = len(gathered)

    def body(*refs):
        ins, outs = refs[:nt], refs[nt:2 * nt]
        send_sems, recv_sems = refs[2 * nt:]
        x, y, c, chips = _mesh_pos()
        copies, arrivals = [], []
        for t in range(nt):
            for j, chip in enumerate(chips):
                def block(core):
                    return outs[t].at[(slice(None),) * leads[t] + (_dev_index(*chip, core),)]
                copies.append(pltpu.make_async_remote_copy(
                    src_ref=block(c), dst_ref=block(c), send_sem=send_sems.at[t, j], recv_sem=recv_sems.at[t, j],
                    device_id=(x, y, 1 - c), device_id_type=MESH))
                arrivals.append(pltpu.make_async_remote_copy(
                    src_ref=block(1 - c), dst_ref=block(1 - c), send_sem=send_sems.at[t, j], recv_sem=recv_sems.at[t, j],
                    device_id=(x, y, 1 - c), device_id_type=MESH))
        for cp in copies:
            cp.start()
        for cp in arrivals:
            cp.wait_recv()
        for cp in copies:
            cp.wait_send()

    return pl.pallas_call(
        body, name=name, in_specs=[ANY] * nt, out_specs=[ANY] * nt,
        out_shape=[jax.ShapeDtypeStruct(g.shape, g.dtype) for g in gathered],
        input_output_aliases={t: t for t in range(nt)},
        scratch_shapes=[pltpu.SemaphoreType.DMA((nt, 3)), pltpu.SemaphoreType.DMA((nt, 3))],
    )(*gathered)


def _scatter_src(x_ref, t, peer_idx):
    return x_ref.at[peer_idx]


def _scatter_dst(land_ref, t, q, peer_idx):
    return land_ref.at[q]


def _chips_src(x_ref, t, peer_idx):
    return x_ref.at[peer_idx // 2]


def _chips_start(name, ps, after):
    lands = [lax.empty((len(CHIP_RELS),) + p.shape[1:], p.dtype) for p in ps]
    return _exchange_start(name, ps, lands, _chips_src, _scatter_dst, after, CHIP_RELS)


def _chips_wait(name, started, after):
    send_sems, recv_sems, ps, lands, _ = started
    return _exchange_wait(name, send_sems, recv_sems, ps, lands, _chips_src, _scatter_dst, after, CHIP_RELS)


def _scatter_start(name, gs, after):
    lands = [lax.empty((NREL,) + g.shape[1:], g.dtype) for g in gs]
    return _exchange_start(name, gs, lands, _scatter_src, _scatter_dst, after)


def _scatter_wait(name, started, after, transposed):
    send_sems, recv_sems, gs, lands, _ = started
    gs, lands = _exchange_wait(name, send_sems, recv_sems, gs, lands, _scatter_src, _scatter_dst, after)
    me = _dev_index(lax.axis_index("x"), lax.axis_index("y"), lax.axis_index("c"))
    outs = []
    for t, (g, rv) in enumerate(zip(gs, lands)):
        _, r, cols = rv.shape
        tr_out = transposed[t]
        oshape = (cols, r) if tr_out else (r, cols)

        def body(c_ref, own_ref, rv_ref, o_ref):
            acc = own_ref[...].astype(F32)
            for k in range(NREL):
                acc = acc + rv_ref[k].astype(F32)
            o_ref[...] = acc.T if tr_out else acc

        outs.append(pl.pallas_call(
            body, name=f"{name}_sum_{t}",
            grid_spec=pltpu.PrefetchScalarGridSpec(
                num_scalar_prefetch=1, grid=(1,),
                in_specs=[pl.BlockSpec((None, r, cols), lambda i, cr: (cr[0], 0, 0)),
                          pl.BlockSpec((NREL, r, cols), lambda i, cr: (0, 0, 0))],
                out_specs=pl.BlockSpec(oshape, lambda i, cr: (0, 0))),
            out_shape=jax.ShapeDtypeStruct(oshape, F32), compiler_params=_cparams(VMEM_BIG),
        )(jnp.reshape(me, (1,)).astype(jnp.int32), g, rv))
    return outs


def _adamw(w, g, m, v):
    shape = w.shape
    cols = shape[-1]
    rows = math.prod(shape[:-1]) if len(shape) > 1 else 1
    w2, g2, m2, v2 = (jnp.reshape(t, (rows, cols)) for t in (w, g, m, v))
    tr = _pick(rows, (1024, 704, 512, 256, 128)) if rows * cols > 65536 else rows
    c1 = 1.0 / (1.0 - ADAM_B1 ** ADAM_STEP)
    c2 = 1.0 / (1.0 - ADAM_B2 ** ADAM_STEP)

    def body(w_ref, g_ref, m_ref, v_ref, d_ref, nm_ref, nv_ref):
        gv = g_ref[...]
        nm = ADAM_B1 * m_ref[...] + (1.0 - ADAM_B1) * gv
        nv = ADAM_B2 * v_ref[...] + (1.0 - ADAM_B2) * (gv * gv)
        d_ref[...] = -ADAM_LR * ((nm * c1) / (jnp.sqrt(nv * c2) + ADAM_EPS) + ADAM_WD * w_ref[...])
        nm_ref[...] = nm
        nv_ref[...] = nv

    spec = pl.BlockSpec((tr, cols), lambda i: (i, 0))
    osh = jax.ShapeDtypeStruct((rows, cols), F32)
    d, nm, nv = pl.pallas_call(
        body, name="adamw", grid=(rows // tr,), in_specs=[spec] * 4, out_specs=[spec] * 3, out_shape=[osh] * 3,
        compiler_params=_cparams(VMEM_BIG),
    )(w2, g2, m2, v2)
    return jnp.reshape(d, shape), jnp.reshape(nm, shape), jnp.reshape(nv, shape)


def _pad_heads(w, name):
    if name not in P_HEADS:
        return w
    nh, real = P_HEADS[name]
    w = jnp.reshape(w, w.shape[:-2] + (nh, real, w.shape[-1]))
    w = jnp.pad(w, [(0, 0)] * (w.ndim - 2) + [(0, HP - real), (0, 0)])
    return jnp.reshape(w, w.shape[:-3] + (nh * HP, w.shape[-1]))


def _unpad_heads(w, name):
    if name not in P_HEADS:
        return w
    nh, real = P_HEADS[name]
    w = jnp.reshape(w, w.shape[:-2] + (nh, HP, w.shape[-1]))[..., :real, :]
    return jnp.reshape(w, w.shape[:-3] + (nh * real, w.shape[-1]))


def _win_pad(win_t):
    segs, o = {}, 0
    for n, s in zip(IN_NAMES, IN_SIZES):
        segs[n] = win_t[..., o:o + s, :]
        o += s
    return jnp.concatenate([_pad_heads(segs[n], n) for n in P_ORDER], axis=-2)


def _win_unpad(win_p):
    segs = {n: _unpad_heads(win_p[..., P_OFF[n]:P_OFF[n] + P_WIDTH[n], :], n) for n in P_ORDER}
    return jnp.concatenate([segs[n] for n in IN_NAMES], axis=-2)


def _t(w):
    return jnp.swapaxes(w, -1, -2)


def _ffn_stacked(g_g, g_u, g_d):
    wg, wu, wd = (jnp.reshape(g, (2, DFF, D)) for g in (g_g, g_u, g_d))
    return [(wg, wu, wd, (j,)) for j in range(2)]


def _ffn_single(g_g, g_u, g_d):
    return tuple(jnp.reshape(g, (DFF, D)) for g in (g_g, g_u, g_d)) + ((),)


def _layer_weights(ffn, g_in, g_pa, g_pb, g_out, gains, w2, b2, bm, gn, gq, gk):
    w2p = jnp.pad(jnp.reshape(w2, (2, GLA_RANK, GLA_H, GLA_DK)), ((0, 0), (0, HP - GLA_RANK), (0, 0), (0, HP - GLA_DK)))
    b2p = jnp.pad(jnp.reshape(b2, (2, 1, GLA_H, GLA_DK)), ((0, 0), (0, 0), (0, 0), (0, HP - GLA_DK)))
    wpb_t = jnp.pad(jnp.reshape(g_pb, (D, ATT_H, HEAD_DIM)), ((0, 0), (0, 0), (0, HP - HEAD_DIM)))
    return dict(
        gains=jnp.reshape(gains, (1, 3, 1, D)), ffn=ffn,
        win_t=_win_pad(jnp.reshape(g_in, (1, D_IN, D))), wpa_t=jnp.reshape(g_pa, (1, D, 512)),
        wpb_t=jnp.reshape(wpb_t, (1, D, ATT_H * HP)), wout=jnp.reshape(g_out, (1, D, D)),
        w2p=jnp.reshape(w2p, (1, 2, HP, GLA_H * HP)), b2p=jnp.reshape(b2p, (1, 2, 1, GLA_H * HP)),
        bm=jnp.reshape(bm, (1, 2, 1, D)), gn=jnp.reshape(gn, (1, 1, GLA_H * HP)),
        gq=jnp.pad(jnp.reshape(gq, (1, 1, HEAD_DIM)), ((0, 0), (0, 0), (0, HP - HEAD_DIM))),
        gk=jnp.pad(jnp.reshape(gk, (1, 1, HEAD_DIM)), ((0, 0), (0, 0), (0, HP - HEAD_DIM))))


def _layer_fwd_lower(h, z, ffn0, gain1):
    return _ffn_fwd(h, z, *ffn0, gain1)


def _layer_fwd_upper(h, z, s0, w, tabs, next_gain):
    h, z, s1 = _mixer_fwd(h, z, w, 0, tabs, w["gains"][0, 2])
    h, z, s2 = _ffn_fwd(h, z, *w["ffn"][1], next_gain)
    return h, z, (s0, s1, s2)


def _layer_fwd(h, z, w, tabs, next_gain):
    h, z, s0 = _layer_fwd_lower(h, z, w["ffn"][0], w["gains"][0, 1])
    return _layer_fwd_upper(h, z, s0, w, tabs, next_gain)


def _layer_bwd_upper(dh, dhb, saved, w, tabs):
    _, s1, s2 = saved
    dh, dhb, dg2, dwg1, dwu1, dwd1 = _ffn_bwd(dh, dhb, s2, w["gains"][0, 2], *w["ffn"][1])
    dh, dhb, gm = _mixer_bwd(dh, dhb, s1, w["gains"][0, 1], w, 0, tabs)
    gm.update(gain2=dg2, wg1=dwg1, wu1=dwu1, wd1=dwd1)
    return dh, dhb, gm


def _layer_bwd_lower(dh, dhb, saved, w, gm):
    dh, dhb, dg0, dwg0, dwu0, dwd0 = _ffn_bwd(dh, dhb, saved[0], w["gains"][0, 0], *w["ffn"][0])
    gm.update(gain0=dg0, wg0=dwg0, wu0=dwu0, wd0=dwd0)
    return dh, dhb, gm


def _layer_bwd(dh, dhb, saved, w, tabs):
    dh, dhb, gm = _layer_bwd_upper(dh, dhb, saved, w, tabs)
    return _layer_bwd_lower(dh, dhb, saved, w, gm)


def _blocks(ts):
    return [jnp.reshape(t, (NDEV, t.shape[0] // NDEV, t.shape[1])) for t in ts]


def _upper_grads(g):
    d_in = _win_unpad(g["win_t"])
    d_pb = jnp.reshape(jnp.reshape(g["wpb_t"], (D, ATT_H, HP))[:, :, :HEAD_DIM], (D, 512))
    return _blocks([g["wg1"], g["wu1"], g["wd1"], d_in, g["wpa_t"], d_pb, g["wout"]])


def _lower_grads(g):
    return _blocks([g["wg0"], g["wu0"], g["wd0"]])


def _big_grads(g):
    return _lower_grads(g) + _upper_grads(g)


def kernel(x, meta_tokens, norm_gains, ffn_w_gate, ffn_w_up, ffn_w_down, w_in, gla_w2, gla_b2, gla_gn, q_norm, k_norm, w_pa, w_pb, b_merge, w_out, final_norm, loss_target, m_meta_tokens, m_norm_gains, m_ffn_w_gate, m_ffn_w_up, m_ffn_w_down, m_w_in, m_gla_w2, m_gla_b2, m_gla_gn, m_q_norm, m_k_norm, m_w_pa, m_w_pb, m_b_merge, m_w_out, m_final_norm, v_meta_tokens, v_norm_gains, v_ffn_w_gate, v_ffn_w_up, v_ffn_w_down, v_w_in, v_gla_w2, v_gla_b2, v_gla_gn, v_q_norm, v_k_norm, v_w_pa, v_w_pb, v_b_merge, v_w_out, v_final_norm):
    dev = _dev_index(lax.axis_index("x"), lax.axis_index("y"), lax.axis_index("c"))
    sh_g = _t(ffn_w_gate).astype(BF16)
    sh_u = _t(ffn_w_up).astype(BF16)
    sh_d = ffn_w_down.astype(BF16)
    sh_in = _t(w_in).astype(BF16)
    sh_pa = _t(w_pa).astype(BF16)
    sh_pb = _t(w_pb).astype(BF16)
    sh_out = w_out.astype(BF16)
    small = jnp.concatenate([jnp.reshape(t, (-1, 128)) for t in
                             (meta_tokens, norm_gains, gla_w2, gla_b2, b_merge)], axis=0)
    small = jnp.pad(small, ((0, 2), (0, 0)))
    def shards(l):
        return [sh_g[l], sh_u[l], sh_d[l], sh_in[l], sh_pa[l], sh_pb[l], sh_out[l]]

    w_leads = [1, 1, 1, 0, 0, 0, 0]
    *g0_ffn0, g_small = _all_gather("gather_layer0", [sh_g[0, 0], sh_u[0, 0], sh_d[0, 0], small], [0, 0, 0, 0])
    rest0 = [sh_g[0, 1], sh_u[0, 1], sh_d[0, 1], sh_in[0], sh_pa[0], sh_pb[0], sh_out[0]]
    rest_leads = [0] * len(rest0)
    started0 = _gather_start("gather_start_0", rest0, rest_leads, g_small, SIBLING_AND_CHIPS)
    meta_full = jnp.reshape(jnp.transpose(g_small[:, 0:16], (1, 0, 2)), (NMETA, D)) + started0[4][0, 0]
    gains_full = jnp.reshape(jnp.transpose(jnp.reshape(g_small[:, 16:28], (NDEV, DEPTH, 3, 128)), (1, 2, 0, 3)), (DEPTH, 3, D))
    w2_full = jnp.reshape(jnp.transpose(jnp.reshape(g_small[:, 28:60], (NDEV, DEPTH, 2, GLA_RANK, 32)), (1, 2, 3, 0, 4)),
                          (DEPTH, 2, GLA_RANK, 256))
    b2_full = jnp.reshape(jnp.transpose(jnp.reshape(g_small[:, 60:62], (NDEV, DEPTH, 2, 32)), (1, 2, 0, 3)), (DEPTH, 2, 256))
    bm_full = jnp.reshape(jnp.transpose(jnp.reshape(g_small[:, 62:70], (NDEV, DEPTH, 2, 128)), (1, 2, 0, 3)), (DEPTH, 2, D))

    def layer_weights(l, ffn, others, gains_l):
        return _layer_weights(ffn, *others, gains_l, w2_full[l], b2_full[l], bm_full[l], gla_gn[l], q_norm[l], k_norm[l])

    xl = x[0]
    lp = xl.shape[0] + NULL + NMETA
    tabs = _rope_tables(lp)
    h = jnp.concatenate([jnp.zeros((NULL, D), F32), meta_full, xl], axis=0)
    weights, saved, started = [], [], {}
    z = _rmsnorm_fwd(h, jnp.reshape(gains_full[0, 0], (1, D)))
    for l in range(DEPTH):
        next_gain = jnp.reshape(gains_full[l + 1, 0], (1, D)) if l + 1 < DEPTH else None
        if l == 0:
            ffn0 = _ffn_single(*g0_ffn0)
            h, z, s0 = _layer_fwd_lower(h, z, ffn0, jnp.reshape(gains_full[0, 1], (1, D)))
            rest = _forward_to_sibling("gather_forward_0", _gather_wait("gather_wait_0", started0, rest_leads, h,
                                                                         SIBLING_AND_CHIPS), rest_leads)
            started[1] = _gather_start("gather_start_1", shards(1), w_leads, rest[0])
            weights.append(layer_weights(0, [ffn0, _ffn_single(*rest[:3])], rest[3:], gains_full[0]))
            z = z + started[1][4][0, 0].astype(BF16)
            h, z, sv = _layer_fwd_upper(h, z, s0, weights[0], tabs, next_gain)
        else:
            tok = jnp.zeros((), F32)
            if l < DEPTH - 1:
                started[l + 1] = _gather_start(f"gather_start_{l + 1}", shards(l + 1), w_leads, h)
                tok = started[l + 1][4][0, 0]
            gathered = _gather_wait(f"gather_wait_{l}", started[l], w_leads, h)
            weights.append(layer_weights(l, _ffn_stacked(*gathered[:3]), gathered[3:], gains_full[l] + tok))
            h, z, sv = _layer_fwd(h, z, weights[l], tabs, next_gain)
        saved.append(sv)
    loss, dh, dhb, d_final = _loss_head(h, loss_target[0], jnp.reshape(final_norm, (1, D)))
    loss = lax.psum(loss[0, 0], ("x", "y", "c"))

    grads, scattering = [None] * DEPTH, {}
    tok = jnp.zeros((), F32)
    for l in reversed(range(DEPTH)):
        w = dict(weights[l], gains=weights[l]["gains"] + tok)
        if l > 0:
            dh, dhb, grads[l] = _layer_bwd(dh, dhb, saved[l], w, tabs)
            scattering[l] = _scatter_start(f"scatter_start_{l}", _big_grads(grads[l]), dhb)
            tok = scattering[l][4][0, 0]
        else:
            dh, dhb, gm = _layer_bwd_upper(dh, dhb, saved[l], w, tabs)
            ups = _upper_grads(gm)
            pair = _pair_sum("rs_pair_up", ups, _exchange_sibling("rs_sibling_up", ups))
            scattering[l] = _chips_start(f"scatter_start_{l}", pair, dhb)
            dhb = dhb + scattering[l][4][0, 0].astype(BF16)
            dh, dhb, grads[l] = _layer_bwd_lower(dh, dhb, saved[l], w, gm)
    grad_x = dh[NULL + NMETA:][None]
    t_lower, t_upper = [False, False, False], [False, False, False, True, True, True, False]
    lows = _lower_grads(grads[0])
    pair_lo = _pair_sum("rs_pair_lo", lows, _exchange_sibling("rs_sibling_lo", lows))
    started_lo = _chips_start("scatter_start_lo", pair_lo, dhb)
    red = [None] * DEPTH
    for l in reversed(range(1, DEPTH)):
        red[l] = _scatter_wait(f"scatter_wait_{l}", scattering[l], started_lo[4], t_lower + t_upper)
    pair, recv = _chips_wait("scatter_wait_0", scattering[0], red[1][-1])
    red_upper = _final_sum("rs_sum_up", pair, recv, t_upper)
    pair_lo, recv_lo = _chips_wait("scatter_wait_lo", started_lo, red_upper[-1])
    red[0] = _final_sum("rs_sum_lo", pair_lo, recv_lo, t_lower) + red_upper
    g_gate = jnp.stack([jnp.stack([red[l][0], red[l][3]]) for l in range(DEPTH)])
    g_up = jnp.stack([jnp.stack([red[l][1], red[l][4]]) for l in range(DEPTH)])
    g_down = jnp.stack([jnp.stack([red[l][2], red[l][5]]) for l in range(DEPTH)])
    g_win = jnp.stack([red[l][6] for l in range(DEPTH)])
    g_wpa = jnp.stack([red[l][7] for l in range(DEPTH)])
    g_wpb = jnp.stack([red[l][8] for l in range(DEPTH)])
    g_wout = jnp.stack([red[l][9] for l in range(DEPTH)])

    d_meta = dh[NULL:NULL + NMETA]
    d_gains = jnp.stack([jnp.concatenate([grads[l]["gain0"], grads[l]["gain"], grads[l]["gain2"]], axis=0)
                         for l in range(DEPTH)])
    d_w2 = jnp.stack([jnp.reshape(jnp.reshape(grads[l]["w2p"], (2, HP, GLA_H, HP))[:, :GLA_RANK, :, :GLA_DK],
                                  (2, GLA_RANK, 256)) for l in range(DEPTH)])
    d_b2 = jnp.stack([jnp.reshape(jnp.reshape(grads[l]["b2p"], (2, GLA_H, HP))[:, :, :GLA_DK], (2, 256))
                      for l in range(DEPTH)])
    d_gn = jnp.stack([grads[l]["gn"][0] for l in range(DEPTH)])
    d_gq = jnp.stack([grads[l]["gq"][0, :HEAD_DIM] for l in range(DEPTH)])
    d_gk = jnp.stack([grads[l]["gk"][0, :HEAD_DIM] for l in range(DEPTH)])
    d_bm = jnp.stack([jnp.concatenate([grads[l]["bma"], grads[l]["bmb"]], axis=0) for l in range(DEPTH)])
    parts = [d_meta, d_gains, d_w2, d_b2, d_gn, d_gq, d_gk, d_bm, d_final[0]]
    sizes = [p.size for p in parts]
    flat = jnp.concatenate([jnp.reshape(p, (-1,)) for p in parts])
    flat = jnp.reshape(flat, (-1, 128))
    nrow = flat.shape[0]
    flat = jnp.pad(flat, ((0, (-nrow) % 8), (0, 0)))
    (g_flat,) = _all_gather("gather_small_grads", [flat], [0])
    tot = jnp.reshape(_sum_gathered(g_flat), (-1,))
    full, o = [], 0
    for p, s in zip(parts, sizes):
        full.append(jnp.reshape(tot[o:o + s], p.shape))
        o += s
    f_meta, f_gains, f_w2, f_b2, f_gn, f_gq, f_gk, f_bm, f_final = full

    def mine(t, width):
        return lax.dynamic_slice_in_dim(t, dev * width, width, axis=t.ndim - 1)

    g_small = dict(meta_tokens=mine(f_meta, 128), norm_gains=mine(f_gains, 128), gla_w2=mine(f_w2, 32),
                   gla_b2=mine(f_b2, 32), gla_gn=f_gn, q_norm=f_gq, k_norm=f_gk, b_merge=mine(f_bm, 128),
                   final_norm=f_final)
    gr = dict(g_small, ffn_w_gate=g_gate, ffn_w_up=g_up, ffn_w_down=g_down, w_in=g_win, w_pa=g_wpa, w_pb=g_wpb,
              w_out=g_wout)
    ws = dict(meta_tokens=meta_tokens, norm_gains=norm_gains, ffn_w_gate=ffn_w_gate, ffn_w_up=ffn_w_up,
              ffn_w_down=ffn_w_down, w_in=w_in, gla_w2=gla_w2, gla_b2=gla_b2, gla_gn=gla_gn, q_norm=q_norm,
              k_norm=k_norm, w_pa=w_pa, w_pb=w_pb, b_merge=b_merge, w_out=w_out, final_norm=final_norm)
    ms = dict(meta_tokens=m_meta_tokens, norm_gains=m_norm_gains, ffn_w_gate=m_ffn_w_gate, ffn_w_up=m_ffn_w_up,
              ffn_w_down=m_ffn_w_down, w_in=m_w_in, gla_w2=m_gla_w2, gla_b2=m_gla_b2, gla_gn=m_gla_gn, q_norm=m_q_norm,
              k_norm=m_k_norm, w_pa=m_w_pa, w_pb=m_w_pb, b_merge=m_b_merge, w_out=m_w_out, final_norm=m_final_norm)
    vs = dict(meta_tokens=v_meta_tokens, norm_gains=v_norm_gains, ffn_w_gate=v_ffn_w_gate, ffn_w_up=v_ffn_w_up,
              ffn_w_down=v_ffn_w_down, w_in=v_w_in, gla_w2=v_gla_w2, gla_b2=v_gla_b2, gla_gn=v_gla_gn, q_norm=v_q_norm,
              k_norm=v_k_norm, w_pa=v_w_pa, w_pb=v_w_pb, b_merge=v_b_merge, w_out=v_w_out, final_norm=v_final_norm)
    names = ["meta_tokens", "norm_gains", "ffn_w_gate", "ffn_w_up", "ffn_w_down", "w_in", "gla_w2", "gla_b2", "gla_gn",
             "q_norm", "k_norm", "w_pa", "w_pb", "b_merge", "w_out", "final_norm"]
    deltas, new_m, new_v = [], [], []
    for n in names:
        if n in ("ffn_w_gate", "ffn_w_up"):
            dlt, nm, nv = (_t(r) for r in _adamw(_t(ws[n]), gr[n], _t(ms[n]), _t(vs[n])))
            gr[n] = _t(gr[n])
        else:
            dlt, nm, nv = _adamw(ws[n], gr[n], ms[n], vs[n])
        deltas.append(dlt)
        new_m.append(nm)
        new_v.append(nv)
    return (loss, grad_x, *[gr[n] for n in names], *deltas, *new_m, *new_v)
```
